```python
import math
import jax, jax.numpy as jnp
from jax import lax
import numpy as np

D_MODEL = 2048
BATCH = 8
SEQ = 2048
DEPTH = 1

HEAD_DIM = 128
N_HEADS_FOX = 8
N_HEADS_DIL = 8
D_FOX = N_HEADS_FOX * HEAD_DIM
D_DIL = N_HEADS_DIL * HEAD_DIM
MIX_WIDTH = D_FOX + D_DIL
D_IN_PROJ = 3 * D_FOX + N_HEADS_FOX + 3 * D_DIL
D_FF = 256 * (-(-8 * D_MODEL // (3 * 256)))
PLE_DIM = 256
Q_BLOCK = 128
DILATED_PATTERNS = ((128, 1), (512, 4), (2048, 16))
N_REL_BUCKETS = 32
REL_MAX_DISTANCE = 2048
RMS_EPS = 1e-6
NEG_INF = -1e30

kernel_name = "hybrid_fox_dilated_macaron_block"


def rmsnorm(x, g):
    xf = x.astype(jnp.float32)
    y = xf * lax.rsqrt(jnp.mean(xf * xf, axis=-1, keepdims=True) + RMS_EPS)
    return (y * g.astype(jnp.float32)).astype(x.dtype)


def swiglu(h, w_gate, w_up, w_down):
    return (jax.nn.silu(h @ w_gate) * (h @ w_up)) @ w_down


def t5_bucket(dist):
    max_exact = N_REL_BUCKETS // 2
    d = jnp.maximum(dist, 1).astype(jnp.float32)
    large = max_exact + (jnp.log(d / max_exact) / math.log(REL_MAX_DISTANCE / max_exact)
                         * (N_REL_BUCKETS - max_exact)).astype(jnp.int32)
    large = jnp.minimum(large, N_REL_BUCKETS - 1)
    return jnp.where(dist < max_exact, dist, large)


def forgetting_attention(q, k, v, log_f):
    B, S, H, Dh = q.shape
    scale = HEAD_DIM ** -0.5
    c = jnp.cumsum(log_f, axis=1).transpose(0, 2, 1)
    outs = []
    for i in range(S // Q_BLOCK):
        t0, t1 = i * Q_BLOCK, (i + 1) * Q_BLOCK
        s = jnp.einsum('bqhd,bkhd->bhqk', q[:, t0:t1], k[:, :t1]).astype(jnp.float32) * scale
        decay = c[:, :, t0:t1, None] - c[:, :, None, :t1]
        causal = (t0 + jnp.arange(Q_BLOCK))[:, None] >= jnp.arange(t1)[None, :]
        s = jnp.where(causal, s + decay, NEG_INF)
        pr = jax.nn.softmax(s, axis=-1)
        outs.append(jnp.einsum('bhqk,bkhd->bqhd', pr.astype(v.dtype), v[:, :t1]))
    return jnp.concatenate(outs, axis=1)


def dilated_window_attention(q, k, v, rel_table, window, dilation):
    B, S, H, Dh = q.shape
    n = window // dilation
    blk = n
    L = S // dilation
    nb = -(-L // blk)
    Lp = nb * blk
    scale = HEAD_DIM ** -0.5

    def sub(a):
        return a.reshape(B, L, dilation, H, Dh).transpose(0, 2, 1, 3, 4)

    qs = jnp.pad(sub(q), ((0, 0), (0, 0), (0, Lp - L), (0, 0), (0, 0)))
    kv_pad = ((0, 0), (0, 0), (blk, Lp - L), (0, 0), (0, 0))
    ks = jnp.pad(sub(k), kv_pad)
    vs = jnp.pad(sub(v), kv_pad)

    def windows(a):
        prev = a[:, :, :Lp].reshape(B, dilation, nb, blk, H, Dh)
        cur = a[:, :, blk:].reshape(B, dilation, nb, blk, H, Dh)
        return jnp.concatenate([prev, cur], axis=3)

    qb = qs.reshape(B, dilation, nb, blk, H, Dh)
    kw, vw = windows(ks), windows(vs)

    qi = jnp.arange(blk)[:, None]
    ki = jnp.arange(2 * blk)[None, :]
    rel = qi + blk - ki
    key_sub = (jnp.arange(nb) * blk)[:, None, None] - blk + ki[None]
    valid = (rel >= 0)[None] & (rel <= n)[None] & (key_sub >= 0)
    bias = rel_table[t5_bucket(jnp.maximum(rel, 0) * dilation)].transpose(2, 0, 1)

    s = jnp.einsum('brnqhd,brnkhd->brnhqk', qb, kw).astype(jnp.float32) * scale
    s = jnp.where(valid[:, None], s + bias.astype(jnp.float32), NEG_INF)
    m = jnp.max(s, axis=-1, keepdims=True)
    e = jnp.exp(s - m)
    denom = jnp.sum(e, axis=-1, keepdims=True)
    o = jnp.einsum('brnhqk,brnkhd->brnqhd', (e / denom).astype(v.dtype), vw)
    lse = (m + jnp.log(denom))[..., 0]

    o = o.reshape(B, dilation, Lp, H, Dh)[:, :, :L].transpose(0, 2, 1, 3, 4).reshape(B, S, H, Dh)
    lse = lse.transpose(0, 1, 2, 4, 3).reshape(B, dilation, Lp, H)[:, :, :L]
    lse = lse.transpose(0, 2, 1, 3).reshape(B, S, H)
    return o, lse


def token_mixer(h, w_in, b_f, w_o, rel_table):
    B, S, _ = h.shape
    u = h @ w_in
    splits = np.cumsum([D_FOX, D_FOX, D_FOX, N_HEADS_FOX, D_DIL, D_DIL]).tolist()
    qa, ka, va, f_logit, qb, kb, vb = jnp.split(u, splits, axis=-1)
    heads = lambda a, H: a.reshape(B, S, H, HEAD_DIM)

    log_f = jax.nn.log_sigmoid((f_logit + b_f).astype(jnp.float32))
    o_a = forgetting_attention(heads(qa, N_HEADS_FOX), heads(ka, N_HEADS_FOX),
                               heads(va, N_HEADS_FOX), log_f)

    qb, kb, vb = heads(qb, N_HEADS_DIL), heads(kb, N_HEADS_DIL), heads(vb, N_HEADS_DIL)
    outs, lses = [], []
    for window, dilation in DILATED_PATTERNS:
        o_i, lse_i = dilated_window_attention(qb, kb, vb, rel_table, window, dilation)
        outs.append(o_i)
        lses.append(lse_i)
    wts = jax.nn.softmax(jnp.stack(lses, axis=0), axis=0)
    o_b = jnp.sum(wts[..., None] * jnp.stack(outs, axis=0).astype(jnp.float32), axis=0).astype(h.dtype)

    cat = jnp.concatenate([o_a.reshape(B, S, D_FOX), o_b.reshape(B, S, D_DIL)], axis=-1)
    return cat @ w_o


def _fwd_setup_inputs(seed: int = 0) -> dict:
    key = jax.random.key(seed)
    ks = jax.random.split(key, 24)
    nrm = lambda k, shape, s: jax.random.normal(k, shape, jnp.float32) * s
    gain = lambda k: 1.0 + 0.05 * jax.random.normal(k, (DEPTH, D_MODEL), jnp.float32)
    return {
        "x": nrm(ks[0], (BATCH, SEQ, D_MODEL), 1.0),
        "p": nrm(ks[1], (DEPTH, BATCH, SEQ, PLE_DIM), 1.0),
        "norm_ffn1": gain(ks[2]),
        "ffn1_w_gate": nrm(ks[3], (DEPTH, D_MODEL, D_FF), D_MODEL ** -0.5),
        "ffn1_w_up": nrm(ks[4], (DEPTH, D_MODEL, D_FF), D_MODEL ** -0.5),
        "ffn1_w_down": nrm(ks[5], (DEPTH, D_FF, D_MODEL), D_FF ** -0.5),
        "norm_mix": gain(ks[6]),
        "w_in": nrm(ks[7], (DEPTH, D_MODEL, D_IN_PROJ), D_MODEL ** -0.5),
        "b_f": 2.0 + nrm(ks[8], (DEPTH, N_HEADS_FOX), 0.1),
        "w_o": nrm(ks[9], (DEPTH, MIX_WIDTH, D_MODEL), MIX_WIDTH ** -0.5),
        "norm_ffn2": gain(ks[10]),
        "ffn2_w_gate": nrm(ks[11], (DEPTH, D_MODEL, D_FF), D_MODEL ** -0.5),
        "ffn2_w_up": nrm(ks[12], (DEPTH, D_MODEL, D_FF), D_MODEL ** -0.5),
        "ffn2_w_down": nrm(ks[13], (DEPTH, D_FF, D_MODEL), D_FF ** -0.5),
        "norm_ple": gain(ks[14]),
        "w_ple_gate": nrm(ks[15], (DEPTH, D_MODEL, D_MODEL), D_MODEL ** -0.5),
        "w_ple_proj": nrm(ks[16], (DEPTH, PLE_DIM, D_MODEL), PLE_DIM ** -0.5),
        "rel_table": nrm(ks[17], (N_REL_BUCKETS, N_HEADS_DIL), 0.5),
        "norm_final": 1.0 + 0.05 * jax.random.normal(ks[18], (D_MODEL,), jnp.float32),
    }


def _fwd_reference(x, p, norm_ffn1, ffn1_w_gate, ffn1_w_up, ffn1_w_down, norm_mix, w_in, b_f, w_o,
              norm_ffn2, ffn2_w_gate, ffn2_w_up, ffn2_w_down, norm_ple, w_ple_gate, w_ple_proj,
              rel_table, norm_final):
    for i in range(DEPTH):
        x = x + 0.5 * swiglu(rmsnorm(x, norm_ffn1[i]), ffn1_w_gate[i], ffn1_w_up[i], ffn1_w_down[i])
        x = x + token_mixer(rmsnorm(x, norm_mix[i]), w_in[i], b_f[i], w_o[i], rel_table)
        x = x + 0.5 * swiglu(rmsnorm(x, norm_ffn2[i]), ffn2_w_gate[i], ffn2_w_up[i], ffn2_w_down[i])
        gate = jax.nn.sigmoid(rmsnorm(x, norm_ple[i]) @ w_ple_gate[i])
        x = x + gate * (p[i] @ w_ple_proj[i])
    return rmsnorm(x, norm_final)


import jax as _jax
import jax.numpy as _jnp

TWIN_FORMAT = 'train_step'
FWD_PARAMS = ['x', 'p', 'norm_ffn1', 'ffn1_w_gate', 'ffn1_w_up', 'ffn1_w_down', 'norm_mix', 'w_in', 'b_f', 'w_o', 'norm_ffn2', 'ffn2_w_gate', 'ffn2_w_up', 'ffn2_w_down', 'norm_ple', 'w_ple_gate', 'w_ple_proj', 'rel_table', 'norm_final']
TWIN_WEIGHTS = ['norm_ffn1', 'ffn1_w_gate', 'ffn1_w_up', 'ffn1_w_down', 'norm_mix', 'w_in', 'b_f', 'w_o', 'norm_ffn2', 'ffn2_w_gate', 'ffn2_w_up', 'ffn2_w_down', 'norm_ple', 'w_ple_gate', 'w_ple_proj', 'rel_table', 'norm_final']
TWIN_DIFF_INPUT = 'x'
TWIN_INPUTS = ['x', 'p', 'norm_ffn1', 'ffn1_w_gate', 'ffn1_w_up', 'ffn1_w_down', 'norm_mix', 'w_in', 'b_f', 'w_o', 'norm_ffn2', 'ffn2_w_gate', 'ffn2_w_up', 'ffn2_w_down', 'norm_ple', 'w_ple_gate', 'w_ple_proj', 'rel_table', 'norm_final', 'loss_target', 'm_norm_ffn1', 'm_ffn1_w_gate', 'm_ffn1_w_up', 'm_ffn1_w_down', 'm_norm_mix', 'm_w_in', 'm_b_f', 'm_w_o', 'm_norm_ffn2', 'm_ffn2_w_gate', 'm_ffn2_w_up', 'm_ffn2_w_down', 'm_norm_ple', 'm_w_ple_gate', 'm_w_ple_proj', 'm_rel_table', 'm_norm_final', 'v_norm_ffn1', 'v_ffn1_w_gate', 'v_ffn1_w_up', 'v_ffn1_w_down', 'v_norm_mix', 'v_w_in', 'v_b_f', 'v_w_o', 'v_norm_ffn2', 'v_ffn2_w_gate', 'v_ffn2_w_up', 'v_ffn2_w_down', 'v_norm_ple', 'v_w_ple_gate', 'v_w_ple_proj', 'v_rel_table', 'v_norm_final']
TWIN_OUTPUTS = ['loss', 'grad_x', 'grad_norm_ffn1', 'grad_ffn1_w_gate', 'grad_ffn1_w_up', 'grad_ffn1_w_down', 'grad_norm_mix', 'grad_w_in', 'grad_b_f', 'grad_w_o', 'grad_norm_ffn2', 'grad_ffn2_w_gate', 'grad_ffn2_w_up', 'grad_ffn2_w_down', 'grad_norm_ple', 'grad_w_ple_gate', 'grad_w_ple_proj', 'grad_rel_table', 'grad_norm_final', 'delta_norm_ffn1', 'delta_ffn1_w_gate', 'delta_ffn1_w_up', 'delta_ffn1_w_down', 'delta_norm_mix', 'delta_w_in', 'delta_b_f', 'delta_w_o', 'delta_norm_ffn2', 'delta_ffn2_w_gate', 'delta_ffn2_w_up', 'delta_ffn2_w_down', 'delta_norm_ple', 'delta_w_ple_gate', 'delta_w_ple_proj', 'delta_rel_table', 'delta_norm_final', 'new_m_norm_ffn1', 'new_m_ffn1_w_gate', 'new_m_ffn1_w_up', 'new_m_ffn1_w_down', 'new_m_norm_mix', 'new_m_w_in', 'new_m_b_f', 'new_m_w_o', 'new_m_norm_ffn2', 'new_m_ffn2_w_gate', 'new_m_ffn2_w_up', 'new_m_ffn2_w_down', 'new_m_norm_ple', 'new_m_w_ple_gate', 'new_m_w_ple_proj', 'new_m_rel_table', 'new_m_norm_final', 'new_v_norm_ffn1', 'new_v_ffn1_w_gate', 'new_v_ffn1_w_up', 'new_v_ffn1_w_down', 'new_v_norm_mix', 'new_v_w_in', 'new_v_b_f', 'new_v_w_o', 'new_v_norm_ffn2', 'new_v_ffn2_w_gate', 'new_v_ffn2_w_up', 'new_v_ffn2_w_down', 'new_v_norm_ple', 'new_v_w_ple_gate', 'new_v_w_ple_proj', 'new_v_rel_table', 'new_v_norm_final']
TWIN_LEAF_KINDS = {'loss': 'loss', 'grad_x': 'grad_x', 'grad_norm_ffn1': 'grad_w', 'grad_ffn1_w_gate': 'grad_w', 'grad_ffn1_w_up': 'grad_w', 'grad_ffn1_w_down': 'grad_w', 'grad_norm_mix': 'grad_w', 'grad_w_in': 'grad_w', 'grad_b_f': 'grad_w', 'grad_w_o': 'grad_w', 'grad_norm_ffn2': 'grad_w', 'grad_ffn2_w_gate': 'grad_w', 'grad_ffn2_w_up': 'grad_w', 'grad_ffn2_w_down': 'grad_w', 'grad_norm_ple': 'grad_w', 'grad_w_ple_gate': 'grad_w', 'grad_w_ple_proj': 'grad_w', 'grad_rel_table': 'grad_w', 'grad_norm_final': 'grad_w', 'delta_norm_ffn1': 'delta_w', 'delta_ffn1_w_gate': 'delta_w', 'delta_ffn1_w_up': 'delta_w', 'delta_ffn1_w_down': 'delta_w', 'delta_norm_mix': 'delta_w', 'delta_w_in': 'delta_w', 'delta_b_f': 'delta_w', 'delta_w_o': 'delta_w', 'delta_norm_ffn2': 'delta_w', 'delta_ffn2_w_gate': 'delta_w', 'delta_ffn2_w_up': 'delta_w', 'delta_ffn2_w_down': 'delta_w', 'delta_norm_ple': 'delta_w', 'delta_w_ple_gate': 'delta_w', 'delta_w_ple_proj': 'delta_w', 'delta_rel_table': 'delta_w', 'delta_norm_final': 'delta_w', 'new_m_norm_ffn1': 'new_m', 'new_m_ffn1_w_gate': 'new_m', 'new_m_ffn1_w_up': 'new_m', 'new_m_ffn1_w_down': 'new_m', 'new_m_norm_mix': 'new_m', 'new_m_w_in': 'new_m', 'new_m_b_f': 'new_m', 'new_m_w_o': 'new_m', 'new_m_norm_ffn2': 'new_m', 'new_m_ffn2_w_gate': 'new_m', 'new_m_ffn2_w_up': 'new_m', 'new_m_ffn2_w_down': 'new_m', 'new_m_norm_ple': 'new_m', 'new_m_w_ple_gate': 'new_m', 'new_m_w_ple_proj': 'new_m', 'new_m_rel_table': 'new_m', 'new_m_norm_final': 'new_m', 'new_v_norm_ffn1': 'new_v', 'new_v_ffn1_w_gate': 'new_v', 'new_v_ffn1_w_up': 'new_v', 'new_v_ffn1_w_down': 'new_v', 'new_v_norm_mix': 'new_v', 'new_v_w_in': 'new_v', 'new_v_b_f': 'new_v', 'new_v_w_o': 'new_v', 'new_v_norm_ffn2': 'new_v', 'new_v_ffn2_w_gate': 'new_v', 'new_v_ffn2_w_up': 'new_v', 'new_v_ffn2_w_down': 'new_v', 'new_v_norm_ple': 'new_v', 'new_v_w_ple_gate': 'new_v', 'new_v_w_ple_proj': 'new_v', 'new_v_rel_table': 'new_v', 'new_v_norm_final': 'new_v'}


def _forward(args):
    return _fwd_reference(*[args[k] for k in FWD_PARAMS])


def _output_shape():
    out = _jax.eval_shape(lambda: _forward(_fwd_setup_inputs(0)))
    return out.shape, out.dtype

N_MICROBATCH = 1
ADAM_LR = 0.001
ADAM_B1 = 0.9
ADAM_B2 = 0.999
ADAM_EPS = 1e-08
ADAM_WD = 0.01
ADAM_STEP = 10
PER_EXAMPLE_BATCH_AXIS = {'x': 0, 'p': 1, 'loss_target': 0}
SHARED_INPUTS = []
_WEIGHT_DTYPES = {'norm_ffn1': _jnp.float32, 'ffn1_w_gate': _jnp.float32, 'ffn1_w_up': _jnp.float32, 'ffn1_w_down': _jnp.float32, 'norm_mix': _jnp.float32, 'w_in': _jnp.float32, 'b_f': _jnp.float32, 'w_o': _jnp.float32, 'norm_ffn2': _jnp.float32, 'ffn2_w_gate': _jnp.float32, 'ffn2_w_up': _jnp.float32, 'ffn2_w_down': _jnp.float32, 'norm_ple': _jnp.float32, 'w_ple_gate': _jnp.float32, 'w_ple_proj': _jnp.float32, 'rel_table': _jnp.float32, 'norm_final': _jnp.float32}
MOMENT_SCALE = {'norm_ffn1': 2.621557e-02, 'ffn1_w_gate': 1.132161e-02, 'ffn1_w_up': 1.097226e-02, 'ffn1_w_down': 1.819815e-02, 'norm_mix': 2.816308e-02, 'w_in': 1.608427e-02, 'b_f': 1.326525e-01, 'w_o': 1.865955e-02, 'norm_ffn2': 2.259876e-02, 'ffn2_w_gate': 9.594596e-03, 'ffn2_w_up': 9.298396e-03, 'ffn2_w_down': 1.542122e-02, 'norm_ple': 1.081811e-02, 'w_ple_gate': 1.070097e-02, 'w_ple_proj': 2.805455e-02, 'rel_table': 1.904308e-02, 'norm_final': 8.007043e+00}


def _to_microbatches(a, axis):
    t = _jnp.moveaxis(a, axis, 0)
    t = t.reshape((N_MICROBATCH, t.shape[0] // N_MICROBATCH) + t.shape[1:])
    return _jnp.moveaxis(t, 1, axis + 1)


def setup_inputs(seed: int = 0) -> dict:
    inp = _fwd_setup_inputs(seed)
    key = _jax.random.fold_in(_jax.random.key(seed), 7919)
    shape, _ = _output_shape()
    out = dict(inp)
    out["loss_target"] = _jax.random.normal(_jax.random.fold_in(key, 0), shape, _jnp.float32)
    for i, name in enumerate(TWIN_WEIGHTS):
        w = inp[name].astype(_jnp.float32)
        if MOMENT_SCALE is None:
            s = _jnp.sqrt(_jnp.mean(_jnp.square(w)) + 1e-30)
        else:
            s = MOMENT_SCALE[name]
        km, kv = _jax.random.split(_jax.random.fold_in(key, i + 1))
        out[name] = w
        out["m_" + name] = s * _jax.random.normal(km, w.shape, _jnp.float32)
        out["v_" + name] = (s * s) * _jax.random.uniform(kv, w.shape, _jnp.float32, 0.5, 1.5)
    if N_MICROBATCH > 1:
        for name, axis in PER_EXAMPLE_BATCH_AXIS.items():
            out[name] = _to_microbatches(out[name], axis)
    return {'x': out['x'], 'p': out['p'], 'norm_ffn1': out['norm_ffn1'], 'ffn1_w_gate': out['ffn1_w_gate'], 'ffn1_w_up': out['ffn1_w_up'], 'ffn1_w_down': out['ffn1_w_down'], 'norm_mix': out['norm_mix'], 'w_in': out['w_in'], 'b_f': out['b_f'], 'w_o': out['w_o'], 'norm_ffn2': out['norm_ffn2'], 'ffn2_w_gate': out['ffn2_w_gate'], 'ffn2_w_up': out['ffn2_w_up'], 'ffn2_w_down': out['ffn2_w_down'], 'norm_ple': out['norm_ple'], 'w_ple_gate': out['w_ple_gate'], 'w_ple_proj': out['w_ple_proj'], 'rel_table': out['rel_table'], 'norm_final': out['norm_final'], 'loss_target': out['loss_target'], 'm_norm_ffn1': out['m_norm_ffn1'], 'm_ffn1_w_gate': out['m_ffn1_w_gate'], 'm_ffn1_w_up': out['m_ffn1_w_up'], 'm_ffn1_w_down': out['m_ffn1_w_down'], 'm_norm_mix': out['m_norm_mix'], 'm_w_in': out['m_w_in'], 'm_b_f': out['m_b_f'], 'm_w_o': out['m_w_o'], 'm_norm_ffn2': out['m_norm_ffn2'], 'm_ffn2_w_gate': out['m_ffn2_w_gate'], 'm_ffn2_w_up': out['m_ffn2_w_up'], 'm_ffn2_w_down': out['m_ffn2_w_down'], 'm_norm_ple': out['m_norm_ple'], 'm_w_ple_gate': out['m_w_ple_gate'], 'm_w_ple_proj': out['m_w_ple_proj'], 'm_rel_table': out['m_rel_table'], 'm_norm_final': out['m_norm_final'], 'v_norm_ffn1': out['v_norm_ffn1'], 'v_ffn1_w_gate': out['v_ffn1_w_gate'], 'v_ffn1_w_up': out['v_ffn1_w_up'], 'v_ffn1_w_down': out['v_ffn1_w_down'], 'v_norm_mix': out['v_norm_mix'], 'v_w_in': out['v_w_in'], 'v_b_f': out['v_b_f'], 'v_w_o': out['v_w_o'], 'v_norm_ffn2': out['v_norm_ffn2'], 'v_ffn2_w_gate': out['v_ffn2_w_gate'], 'v_ffn2_w_up': out['v_ffn2_w_up'], 'v_ffn2_w_down': out['v_ffn2_w_down'], 'v_norm_ple': out['v_norm_ple'], 'v_w_ple_gate': out['v_w_ple_gate'], 'v_w_ple_proj': out['v_w_ple_proj'], 'v_rel_table': out['v_rel_table'], 'v_norm_final': out['v_norm_final']}


def _loss(weights, diff, rest, loss_target):
    with _jax.named_scope("forward"):
        args = {**rest, TWIN_DIFF_INPUT: diff, **{k: w.astype(_WEIGHT_DTYPES[k]) for k, w in weights.items()}}
        y = _forward(args)
    with _jax.named_scope("loss_head"):
        err = _jnp.square(y.astype(_jnp.float32) - loss_target)
        return 0.5 * _jnp.sum(_jnp.mean(err, axis=-1)) if err.ndim else 0.5 * err


def _adamw(w, g, m, v):
    m = ADAM_B1 * m + (1.0 - ADAM_B1) * g
    v = ADAM_B2 * v + (1.0 - ADAM_B2) * _jnp.square(g)
    m_hat = m / (1.0 - ADAM_B1 ** ADAM_STEP)
    v_hat = v / (1.0 - ADAM_B2 ** ADAM_STEP)
    delta = -ADAM_LR * (m_hat / (_jnp.sqrt(v_hat) + ADAM_EPS) + ADAM_WD * w)
    return delta, m, v


def reference(x, p, norm_ffn1, ffn1_w_gate, ffn1_w_up, ffn1_w_down, norm_mix, w_in, b_f, w_o, norm_ffn2, ffn2_w_gate, ffn2_w_up, ffn2_w_down, norm_ple, w_ple_gate, w_ple_proj, rel_table, norm_final, loss_target, m_norm_ffn1, m_ffn1_w_gate, m_ffn1_w_up, m_ffn1_w_down, m_norm_mix, m_w_in, m_b_f, m_w_o, m_norm_ffn2, m_ffn2_w_gate, m_ffn2_w_up, m_ffn2_w_down, m_norm_ple, m_w_ple_gate, m_w_ple_proj, m_rel_table, m_norm_final, v_norm_ffn1, v_ffn1_w_gate, v_ffn1_w_up, v_ffn1_w_down, v_norm_mix, v_w_in, v_b_f, v_w_o, v_norm_ffn2, v_ffn2_w_gate, v_ffn2_w_up, v_ffn2_w_down, v_norm_ple, v_w_ple_gate, v_w_ple_proj, v_rel_table, v_norm_final):
    given = dict(x=x, p=p, norm_ffn1=norm_ffn1, ffn1_w_gate=ffn1_w_gate, ffn1_w_up=ffn1_w_up, ffn1_w_down=ffn1_w_down, norm_mix=norm_mix, w_in=w_in, b_f=b_f, w_o=w_o, norm_ffn2=norm_ffn2, ffn2_w_gate=ffn2_w_gate, ffn2_w_up=ffn2_w_up, ffn2_w_down=ffn2_w_down, norm_ple=norm_ple, w_ple_gate=w_ple_gate, w_ple_proj=w_ple_proj, rel_table=rel_table, norm_final=norm_final, loss_target=loss_target, m_norm_ffn1=m_norm_ffn1, m_ffn1_w_gate=m_ffn1_w_gate, m_ffn1_w_up=m_ffn1_w_up, m_ffn1_w_down=m_ffn1_w_down, m_norm_mix=m_norm_mix, m_w_in=m_w_in, m_b_f=m_b_f, m_w_o=m_w_o, m_norm_ffn2=m_norm_ffn2, m_ffn2_w_gate=m_ffn2_w_gate, m_ffn2_w_up=m_ffn2_w_up, m_ffn2_w_down=m_ffn2_w_down, m_norm_ple=m_norm_ple, m_w_ple_gate=m_w_ple_gate, m_w_ple_proj=m_w_ple_proj, m_rel_table=m_rel_table, m_norm_final=m_norm_final, v_norm_ffn1=v_norm_ffn1, v_ffn1_w_gate=v_ffn1_w_gate, v_ffn1_w_up=v_ffn1_w_up, v_ffn1_w_down=v_ffn1_w_down, v_norm_mix=v_norm_mix, v_w_in=v_w_in, v_b_f=v_b_f, v_w_o=v_w_o, v_norm_ffn2=v_norm_ffn2, v_ffn2_w_gate=v_ffn2_w_gate, v_ffn2_w_up=v_ffn2_w_up, v_ffn2_w_down=v_ffn2_w_down, v_norm_ple=v_norm_ple, v_w_ple_gate=v_w_ple_gate, v_w_ple_proj=v_w_ple_proj, v_rel_table=v_rel_table, v_norm_final=v_norm_final)
    weights = {n: given[n] for n in TWIN_WEIGHTS}
    shared = {n: given[n] for n in SHARED_INPUTS}
    per_example = {n: given[n] for n in ['x', 'p']}
    grad_fn = _jax.value_and_grad(_loss, argnums=(0, 1))

    def one_microbatch(ex, loss_target):
        ex = dict(ex)
        diff = ex.pop(TWIN_DIFF_INPUT)
        return grad_fn(weights, diff, {**shared, **ex}, loss_target)

    if N_MICROBATCH == 1:
        loss, (grad_w, grad_x) = one_microbatch(per_example, given["loss_target"])
    else:
        def body(carry, xs):
            loss_sum, grad_sum = carry
            l_k, (gw_k, gx_k) = one_microbatch(xs[0], xs[1])
            with _jax.named_scope("update"):
                return (loss_sum + l_k, _jax.tree.map(_jnp.add, grad_sum, gw_k)), gx_k

        init = (_jnp.zeros((), _jnp.float32), _jax.tree.map(_jnp.zeros_like, weights))
        (loss, grad_w), grad_x = _jax.lax.scan(body, init, (per_example, given["loss_target"]))
    with _jax.named_scope("update"):
        delta_w, new_m, new_v = {}, {}, {}
        for n in TWIN_WEIGHTS:
            delta_w[n], new_m[n], new_v[n] = _adamw(weights[n], grad_w[n], given["m_" + n], given["v_" + n])
    return (loss, grad_x, *[grad_w[n] for n in TWIN_WEIGHTS], *[delta_w[n] for n in TWIN_WEIGHTS],
            *[new_m[n] for n in TWIN_WEIGHTS], *[new_v[n] for n in TWIN_WEIGHTS])
```

```python
import functools
import math

import numpy as np
import jax
import jax.numpy as jnp
from jax import lax
from jax.experimental import pallas as pl
from jax.experimental.pallas import tpu as pltpu

F32 = jnp.float32
BF16 = jnp.bfloat16

N_DEV = 8
HEAD_DIM = 128
N_FOX = 8
N_DIL = 8
N_HEADS = N_FOX + N_DIL
DILATED_PATTERNS = ((128, 1), (512, 4), (2048, 16))
N_REL_BUCKETS = 32
REL_MAX_DISTANCE = 2048
RMS_EPS = 1e-6
NEG_INF = -1e30
LANES = 128
VMEM_LIMIT = 56 * 1024 * 1024

ADAM_LR = 0.001
ADAM_B1 = 0.9
ADAM_B2 = 0.999
ADAM_EPS = 1e-08
ADAM_WD = 0.01
ADAM_STEP = 10

MESH = pl.DeviceIdType.MESH


def _params(sem):
    return pltpu.CompilerParams(dimension_semantics=sem, vmem_limit_bytes=VMEM_LIMIT)


def _dot(a, b, ca, cb, precision=None):
    return lax.dot_general(a, b, (((ca,), (cb,)), ((), ())),
                           preferred_element_type=F32, precision=precision)


def _sigmoid(z):
    return 1.0 / (1.0 + jnp.exp(-z))


def _tile(n, want):
    t = min(n, want)
    assert n % t == 0, (n, t)
    return t


def rms_fwd(x, g, name):
    T, D = x.shape
    tm = _tile(T, 256)

    def body(x_ref, g_ref, h_ref):
        xv = x_ref[...]
        r = lax.rsqrt(jnp.mean(xv * xv, axis=-1, keepdims=True) + RMS_EPS)
        h_ref[...] = (xv * r * g_ref[...]).astype(BF16)

    return pl.pallas_call(
        body, name=name, grid=(T // tm,),
        in_specs=[pl.BlockSpec((tm, D), lambda i: (i, 0)), pl.BlockSpec((1, D), lambda i: (0, 0))],
        out_specs=pl.BlockSpec((tm, D), lambda i: (i, 0)),
        out_shape=jax.ShapeDtypeStruct((T, D), BF16),
        compiler_params=_params(("parallel",)),
    )(x, g)


def rms_bwd(dh, x, g, dres, name):
    T, D = x.shape
    tm = _tile(T, 256)

    def body(dh_ref, x_ref, g_ref, dres_ref, dx_ref, dg_ref):
        i = pl.program_id(0)
        xv = x_ref[...]
        r = lax.rsqrt(jnp.mean(xv * xv, axis=-1, keepdims=True) + RMS_EPS)
        xh = xv * r
        d = dh_ref[...]
        u = d * g_ref[...]
        dx_ref[...] = dres_ref[...] + r * (u - xh * jnp.mean(u * xh, axis=-1, keepdims=True))
        part = jnp.sum(d * xh, axis=0, keepdims=True)

        @pl.when(i == 0)
        def _():
            dg_ref[...] = part

        @pl.when(i > 0)
        def _():
            dg_ref[...] += part

    row = pl.BlockSpec((tm, D), lambda i: (i, 0))
    vec = pl.BlockSpec((1, D), lambda i: (0, 0))
    return pl.pallas_call(
        body, name=name, grid=(T // tm,),
        in_specs=[row, row, vec, row], out_specs=[row, vec],
        out_shape=[jax.ShapeDtypeStruct((T, D), F32), jax.ShapeDtypeStruct((1, D), F32)],
        compiler_params=_params(("arbitrary",)),
    )(dh, x, g, dres)


def final_loss_bwd(x, g, target, name):
    T, D = x.shape
    tm = _tile(T, 256)

    def body(x_ref, g_ref, t_ref, loss_ref, dx_ref, dg_ref):
        i = pl.program_id(0)
        xv = x_ref[...]
        gv = g_ref[...]
        r = lax.rsqrt(jnp.mean(xv * xv, axis=-1, keepdims=True) + RMS_EPS)
        xh = xv * r
        e = xh * gv - t_ref[...]
        lpart = 0.5 * jnp.sum(jnp.mean(e * e, axis=-1, keepdims=True), axis=0, keepdims=True)
        lrow = jnp.broadcast_to(lpart, (1, LANES))
        d = e * (1.0 / D)
        u = d * gv
        dx_ref[...] = r * (u - xh * jnp.mean(u * xh, axis=-1, keepdims=True))
        part = jnp.sum(d * xh, axis=0, keepdims=True)

        @pl.when(i == 0)
        def _():
            dg_ref[...] = part
            loss_ref[...] = lrow

        @pl.when(i > 0)
        def _():
            dg_ref[...] += part
            loss_ref[...] += lrow

    row = pl.BlockSpec((tm, D), lambda i: (i, 0))
    vec = pl.BlockSpec((1, D), lambda i: (0, 0))
    return pl.pallas_call(
        body, name=name, grid=(T // tm,),
        in_specs=[row, vec, row],
        out_specs=[pl.BlockSpec((1, LANES), lambda i: (0, 0)), row, vec],
        out_shape=[jax.ShapeDtypeStruct((1, LANES), F32), jax.ShapeDtypeStruct((T, D), F32),
                   jax.ShapeDtypeStruct((1, D), F32)],
        compiler_params=_params(("arbitrary",)),
    )(x, g, target)


def _bf(v, scale=None):
    if scale is not None:
        v = v * scale
    return v.astype(BF16)


def mm_nn(a, b, name, *, tn, out_dtype, tm=512, n_out=None, b_block=None, b_map=None,
          res=None):
    T, K = a.shape
    N = n_out if n_out is not None else b.shape[1]
    tm = _tile(T, tm)
    tn = _tile(N, tn)
    b_block = b_block or (K, tn)
    b_map = b_map or (lambda n, i: (0, n))

    def body(*refs):
        a_ref, b_ref = refs[0], refs[1]
        o_ref = refs[-1]
        acc = _dot(_bf(a_ref[...]), _bf(b_ref[...]), 1, 0)
        if res is not None:
            acc = refs[2][...] + acc
        o_ref[...] = acc.astype(out_dtype)

    in_specs = [pl.BlockSpec((tm, K), lambda n, i: (i, 0)), pl.BlockSpec(b_block, b_map)]
    args = [a, b]
    if res is not None:
        in_specs.append(pl.BlockSpec((tm, tn), lambda n, i: (i, n)))
        args.append(res)
    return pl.pallas_call(
        body, name=name, grid=(N // tn, T // tm), in_specs=in_specs,
        out_specs=pl.BlockSpec((tm, tn), lambda n, i: (i, n)),
        out_shape=jax.ShapeDtypeStruct((T, N), out_dtype),
        compiler_params=_params(("parallel", "parallel")),
    )(*args)


def mm_nt(pairs, name, *, tn, out_dtype, tm=512):
    T = pairs[0][0].shape[0]
    N = pairs[0][1].shape[0]
    tm = _tile(T, tm)
    tn = _tile(N, tn)
    npair = len(pairs)

    def body(*refs):
        o_ref = refs[-1]
        acc = None
        for q in range(npair):
            part = _dot(_bf(refs[2 * q][...]), _bf(refs[2 * q + 1][...]), 1, 1)
            acc = part if acc is None else acc + part
        o_ref[...] = acc.astype(out_dtype)

    in_specs, args = [], []
    for a, b in pairs:
        K = a.shape[1]
        in_specs += [pl.BlockSpec((tm, K), lambda n, i: (i, 0)), pl.BlockSpec((tn, K), lambda n, i: (n, 0))]
        args += [a, b]
    return pl.pallas_call(
        body, name=name, grid=(N // tn, T // tm), in_specs=in_specs,
        out_specs=pl.BlockSpec((tm, tn), lambda n, i: (i, n)),
        out_shape=jax.ShapeDtypeStruct((T, N), out_dtype),
        compiler_params=_params(("parallel", "parallel")),
    )(*args)


def mm_tn(a, b, name, *, grid, a_block, a_map, b_block, b_map, o_block, o_map, out_shape,
          b_scale=None):
    def body(a_ref, b_ref, o_ref):
        o_ref[...] = _dot(_bf(a_ref[...]), _bf(b_ref[...], b_scale), 0, 0).astype(BF16)

    return pl.pallas_call(
        body, name=name, grid=grid,
        in_specs=[pl.BlockSpec(a_block, a_map), pl.BlockSpec(b_block, b_map)],
        out_specs=pl.BlockSpec(o_block, o_map),
        out_shape=jax.ShapeDtypeStruct(out_shape, BF16),
        compiler_params=_params(("parallel",) * len(grid)),
    )(a, b)


def mm_tn_plain(a, b, name, *, tm=512, tn=512, b_scale=None):
    T, M = a.shape
    N = b.shape[1]
    tm = _tile(M, tm)
    tn = _tile(N, tn)
    return mm_tn(a, b, name, grid=(M // tm, N // tn),
                 a_block=(T, tm), a_map=lambda m, n: (0, m),
                 b_block=(T, tn), b_map=lambda m, n: (0, n),
                 o_block=(tm, tn), o_map=lambda m, n: (m, n),
                 out_shape=(M, N), b_scale=b_scale)


def ffn_up(h, wgu, gi, ui, nper, name):
    T, D = h.shape
    F8 = wgu.shape[1]
    tm = _tile(T, 512)
    nt = T // tm

    def body(h_ref, wg_ref, wu_ref, a_ref, b_ref, s_ref):
        hv = h_ref[...]
        a = _dot(hv, wg_ref[...], 1, 0)
        b = _dot(hv, wu_ref[...], 1, 0)
        a_ref[...] = a.astype(BF16)
        b_ref[...] = b.astype(BF16)
        s_ref[...] = (a * _sigmoid(a) * b).astype(BF16)

    blk = pl.BlockSpec((tm, F8), lambda j, i: (j * nt + i, 0))
    shp = jax.ShapeDtypeStruct((N_DEV * T, F8), BF16)
    return pl.pallas_call(
        body, name=name, grid=(N_DEV, nt),
        in_specs=[pl.BlockSpec((tm, D), lambda j, i: (i, 0)),
                  pl.BlockSpec((D, F8), lambda j, i: (j * nper + gi, 0)),
                  pl.BlockSpec((D, F8), lambda j, i: (j * nper + ui, 0))],
        out_specs=[blk, blk, blk], out_shape=[shp, shp, shp],
        compiler_params=_params(("parallel", "parallel")),
    )(h, wgu, wgu)


def ffn_down(s, wd, di, nper, x, name):
    T, D = x.shape
    F8 = s.shape[1]
    tm = _tile(T, 512)
    nt = T // tm

    def body(s_ref, w_ref, x_ref, o_ref, acc_ref):
        j = pl.program_id(1)
        part = _dot(s_ref[...], w_ref[...], 1, 0)

        @pl.when(j == 0)
        def _():
            acc_ref[...] = part

        @pl.when(j > 0)
        def _():
            acc_ref[...] += part

        @pl.when(j == N_DEV - 1)
        def _():
            o_ref[...] = x_ref[...] + 0.5 * acc_ref[...]

    return pl.pallas_call(
        body, name=name, grid=(nt, N_DEV),
        in_specs=[pl.BlockSpec((tm, F8), lambda i, j: (j * nt + i, 0)),
                  pl.BlockSpec((F8, D), lambda i, j: (j * nper + di, 0)),
                  pl.BlockSpec((tm, D), lambda i, j: (i, 0))],
        out_specs=pl.BlockSpec((tm, D), lambda i, j: (i, 0)),
        out_shape=jax.ShapeDtypeStruct((T, D), F32),
        scratch_shapes=[pltpu.VMEM((tm, D), F32)],
        compiler_params=_params(("parallel", "arbitrary")),
    )(s, wd, x)


def ffn_bwd_act(dx, wd, di, nper_d, a, b, name):
    T, D = dx.shape
    F8 = a.shape[1]
    tm = _tile(T, 512)
    nt = T // tm

    def body(dx_ref, w_ref, a_ref, b_ref, da_ref, db_ref):
        ds = _dot(_bf(dx_ref[...], 0.5), w_ref[...], 1, 1)
        av = a_ref[...].astype(F32)
        bv = b_ref[...].astype(F32)
        sg = _sigmoid(av)
        da_ref[...] = (ds * bv * (sg * (1.0 + av * (1.0 - sg)))).astype(BF16)
        db_ref[...] = (ds * (av * sg)).astype(BF16)

    blk = pl.BlockSpec((tm, F8), lambda j, i: (j * nt + i, 0))
    shp = jax.ShapeDtypeStruct((N_DEV * T, F8), BF16)
    return pl.pallas_call(
        body, name=name, grid=(N_DEV, nt),
        in_specs=[pl.BlockSpec((tm, D), lambda j, i: (i, 0)),
                  pl.BlockSpec((F8, D), lambda j, i: (j * nper_d + di, 0)), blk, blk],
        out_specs=[blk, blk], out_shape=[shp, shp],
        compiler_params=_params(("parallel", "parallel")),
    )(dx, wd, a, b)


def ffn_bwd_dh(da, db, wgu, gi, ui, nper, D, name):
    F8 = da.shape[1]
    T = da.shape[0] // N_DEV
    tm = _tile(T, 512)
    nt = T // tm

    def body(da_ref, db_ref, wg_ref, wu_ref, o_ref, acc_ref):
        j = pl.program_id(1)
        part = _dot(da_ref[...], wg_ref[...], 1, 1) + _dot(db_ref[...], wu_ref[...], 1, 1)

        @pl.when(j == 0)
        def _():
            acc_ref[...] = part

        @pl.when(j > 0)
        def _():
            acc_ref[...] += part

        @pl.when(j == N_DEV - 1)
        def _():
            o_ref[...] = acc_ref[...]

    blk = pl.BlockSpec((tm, F8), lambda i, j: (j * nt + i, 0))
    return pl.pallas_call(
        body, name=name, grid=(nt, N_DEV),
        in_specs=[blk, blk,
                  pl.BlockSpec((D, F8), lambda i, j: (j * nper + gi, 0)),
                  pl.BlockSpec((D, F8), lambda i, j: (j * nper + ui, 0))],
        out_specs=pl.BlockSpec((tm, D), lambda i, j: (i, 0)),
        out_shape=jax.ShapeDtypeStruct((T, D), F32),
        scratch_shapes=[pltpu.VMEM((tm, D), F32)],
        compiler_params=_params(("parallel", "arbitrary")),
    )(da, db, wgu, wgu)


def ffn_bwd_dw_in(h, dact, name):
    T, D = h.shape
    F8 = dact.shape[1]
    tm = _tile(D, 512)
    nm = D // tm
    return mm_tn(h, dact, name, grid=(N_DEV, nm),
                 a_block=(T, tm), a_map=lambda j, m: (0, m),
                 b_block=(T, F8), b_map=lambda j, m: (j, 0),
                 o_block=(tm, F8), o_map=lambda j, m: (j * nm + m, 0),
                 out_shape=(N_DEV * D, F8))


def ffn_bwd_dw_down(s, dx, name):
    T, D = dx.shape
    F8 = s.shape[1]
    tn = _tile(D, 512)
    return mm_tn(s, dx, name, grid=(N_DEV, D // tn),
                 a_block=(T, F8), a_map=lambda j, n: (j, 0),
                 b_block=(T, tn), b_map=lambda j, n: (0, n),
                 o_block=(F8, tn), o_map=lambda j, n: (j, n),
                 out_shape=(N_DEV * F8, D), b_scale=0.5)


def _t5_bucket_np(dist):
    max_exact = N_REL_BUCKETS // 2
    d = np.maximum(dist, 1).astype(np.float64)
    large = max_exact + (np.log(d / max_exact) / math.log(REL_MAX_DISTANCE / max_exact)
                         * (N_REL_BUCKETS - max_exact)).astype(np.int64)
    large32 = max_exact + (np.log(d.astype(np.float32) / np.float32(max_exact))
                           / np.float32(math.log(REL_MAX_DISTANCE / max_exact))
                           * np.float32(N_REL_BUCKETS - max_exact)).astype(np.int64)
    assert np.array_equal(large, large32)
    large = np.minimum(large, N_REL_BUCKETS - 1)
    return np.where(dist < max_exact, dist, large)


def _distance_tables(T, tq):
    dist = np.arange(T)
    mult = np.zeros(T, np.int64)
    for window, dilation in DILATED_PATTERNS:
        mult += ((dist % dilation == 0) & (dist // dilation <= window // dilation)).astype(np.int64)
    logm = np.where(mult > 0, np.log(np.maximum(mult, 1)), NEG_INF).astype(np.float32)
    bucket = _t5_bucket_np(dist).astype(np.int32)
    nkb = T // tq
    k = np.arange(nkb)[:, None, None]
    r = np.arange(tq)[None, :, None]
    c = np.arange(tq)[None, None, :]
    delta = k * tq + r - c
    return bucket, logm, delta


def bias_tiles(rel_table, T, tq):
    bucket, logm, delta = _distance_tables(T, tq)
    per_dist = rel_table[bucket, :] + logm[:, None]
    per_dist = jnp.concatenate([jnp.full((1, N_DIL), NEG_INF, F32), per_dist], axis=0)
    idx = np.where(delta >= 0, delta + 1, 0).astype(np.int32)
    dil = jnp.transpose(per_dist[idx], (3, 0, 1, 2))
    causal = jnp.asarray(np.where(delta >= 0, 0.0, NEG_INF).astype(np.float32))[None]
    return jnp.concatenate([causal, dil], axis=0)


def fox_gate_fwd(uf, bf, name):
    T = uf.shape[0]
    tb = _tile(T, 512)

    def body(u_ref, b_ref, c_ref, ct_ref):
        lane = lax.broadcasted_iota(jnp.int32, (1, LANES), 1)
        tri = (lax.broadcasted_iota(jnp.int32, (tb, tb), 0)
               >= lax.broadcasted_iota(jnp.int32, (tb, tb), 1)).astype(F32)
        carry = jnp.zeros((1, LANES), F32)
        for blk in range(T // tb):
            z = u_ref[pl.ds(blk * tb, tb), :] + b_ref[...]
            lf = jnp.minimum(z, 0.0) - jnp.log1p(jnp.exp(-jnp.abs(z)))
            lf = jnp.where(lane < N_FOX, lf, 0.0)
            cb = _dot(tri, lf, 1, 0, precision=lax.Precision.HIGHEST) + carry
            c_ref[pl.ds(blk * tb, tb), :] = cb
            ct_ref[:, pl.ds(blk * tb, tb)] = cb.T
            carry = cb[tb - 1:tb, :]

    return pl.pallas_call(
        body, name=name,
        out_shape=[jax.ShapeDtypeStruct((T, LANES), F32), jax.ShapeDtypeStruct((LANES, T), F32)],
        compiler_params=_params(None),
    )(uf, bf)


def fox_gate_bwd(dct, uf, bf, name):
    T = uf.shape[0]
    tb = _tile(T, 512)

    def body(d_ref, u_ref, b_ref, du_ref, db_ref):
        lane = lax.broadcasted_iota(jnp.int32, (1, LANES), 1)
        triu = (lax.broadcasted_iota(jnp.int32, (tb, tb), 0)
                <= lax.broadcasted_iota(jnp.int32, (tb, tb), 1)).astype(F32)
        carry = jnp.zeros((1, LANES), F32)
        dbv = jnp.zeros((1, LANES), F32)
        for blk in reversed(range(T // tb)):
            dc = d_ref[:, pl.ds(blk * tb, tb)].T
            dlf = _dot(triu, dc, 1, 0, precision=lax.Precision.HIGHEST) + carry
            carry = dlf[0:1, :]
            z = u_ref[pl.ds(blk * tb, tb), :] + b_ref[...]
            dz = jnp.where(lane < N_FOX, dlf * (1.0 - _sigmoid(z)), 0.0)
            du_ref[pl.ds(blk * tb, tb), :] = dz
            dbv = dbv + jnp.sum(dz, axis=0, keepdims=True)
        db_ref[...] = dbv

    return pl.pallas_call(
        body, name=name,
        out_shape=[jax.ShapeDtypeStruct((T, LANES), F32), jax.ShapeDtypeStruct((1, LANES), F32)],
        compiler_params=_params(None),
    )(dct, uf, bf)


def _bias_slot(h):
    return jnp.maximum(h - (N_FOX - 1), 0)


def _scores(q_ref, k_ref, c_ref, ct_ref, tb_ref, h, i, T, tq):
    scale = HEAD_DIM ** -0.5
    nkb = T // tq
    s = _dot(q_ref[...], k_ref[...], 1, 1) * scale
    lane = lax.broadcasted_iota(jnp.int32, (1, LANES), 1)
    c_col = jnp.sum(jnp.where(lane == h, c_ref[...], 0.0), axis=1, keepdims=True)
    c_row = ct_ref[pl.ds(h, 1), :]
    chunks = []
    for jb in range(nkb):
        kk = i - jb
        tile = tb_ref[jnp.maximum(kk, 0)]
        chunks.append(jnp.where(kk >= 0, tile, NEG_INF))
    bias = jnp.concatenate(chunks, axis=1)
    return s + (c_col - c_row) + bias


def _attn_specs(T, tq):
    nkb = T // tq
    return [
        pl.BlockSpec((tq, HEAD_DIM), lambda h, i: (i, h)),
        pl.BlockSpec((T, HEAD_DIM), lambda h, i: (0, N_HEADS + h)),
        pl.BlockSpec((T, HEAD_DIM), lambda h, i: (0, 2 * N_HEADS + h)),
        pl.BlockSpec((tq, LANES), lambda h, i: (i, 0)),
        pl.BlockSpec((LANES, T), lambda h, i: (0, 0)),
        pl.BlockSpec((None, nkb, tq, tq), lambda h, i: (_bias_slot(h), 0, 0, 0)),
    ]


def attention_fwd(qkv, c, ct, tiles, name):
    T = qkv.shape[0]
    tq = tiles.shape[2]

    def body(q_ref, k_ref, v_ref, c_ref, ct_ref, tb_ref, o_ref):
        h, i = pl.program_id(0), pl.program_id(1)
        s = _scores(q_ref, k_ref, c_ref, ct_ref, tb_ref, h, i, T, tq)
        p = jnp.exp(s - jnp.max(s, axis=1, keepdims=True))
        l = jnp.sum(p, axis=1, keepdims=True)
        o_ref[...] = _dot((p / l).astype(BF16), v_ref[...], 1, 0).astype(BF16)

    return pl.pallas_call(
        body, name=name, grid=(N_HEADS, T // tq),
        in_specs=[qkv_spec for qkv_spec in _attn_specs(T, tq)],
        out_specs=pl.BlockSpec((tq, HEAD_DIM), lambda h, i: (i, h)),
        out_shape=jax.ShapeDtypeStruct((T, N_HEADS * HEAD_DIM), BF16),
        compiler_params=_params(("parallel", "parallel")),
    )(qkv, qkv, qkv, c, ct, tiles)


def attention_bwd(qkv, c, ct, tiles, do, name):
    T = qkv.shape[0]
    tq = tiles.shape[2]
    nkb = T // tq
    nq = T // tq
    scale = HEAD_DIM ** -0.5

    def body(q_ref, k_ref, v_ref, c_ref, ct_ref, tb_ref, do_ref,
             dq_ref, dk_ref, dv_ref, dct_ref, dtb_ref, dk_acc, dv_acc):
        h, i = pl.program_id(0), pl.program_id(1)
        s = _scores(q_ref, k_ref, c_ref, ct_ref, tb_ref, h, i, T, tq)
        p = jnp.exp(s - jnp.max(s, axis=1, keepdims=True))
        p = p / jnp.sum(p, axis=1, keepdims=True)
        dov = do_ref[...]
        dp = _dot(dov, v_ref[...], 1, 1)
        ds = p * (dp - jnp.sum(p * dp, axis=1, keepdims=True))
        ds_b = ds.astype(BF16)
        dq_ref[...] = (_dot(ds_b, k_ref[...], 1, 0) * scale).astype(BF16)
        dk_part = _dot(ds_b, q_ref[...], 0, 0) * scale
        dv_part = _dot(p.astype(BF16), dov, 0, 0)
        dc_part = -jnp.sum(ds, axis=0, keepdims=True)

        @pl.when(i == 0)
        def _():
            dk_acc[...] = dk_part
            dv_acc[...] = dv_part
            dct_ref[...] = dc_part
            dtb_ref[...] = jnp.zeros_like(dtb_ref)

        @pl.when(i > 0)
        def _():
            dk_acc[...] += dk_part
            dv_acc[...] += dv_part
            dct_ref[...] += dc_part

        for jb in range(nkb):
            kk = i - jb

            @pl.when(kk >= 0)
            def _():
                dtb_ref[kk] += ds[:, jb * tq:(jb + 1) * tq]

        @pl.when(i == nq - 1)
        def _():
            dk_ref[...] = dk_acc[...].astype(BF16)
            dv_ref[...] = dv_acc[...].astype(BF16)

    head_cols = jax.ShapeDtypeStruct((T, N_HEADS * HEAD_DIM), BF16)
    return pl.pallas_call(
        body, name=name, grid=(N_HEADS, nq),
        in_specs=_attn_specs(T, tq) + [pl.BlockSpec((tq, HEAD_DIM), lambda h, i: (i, h))],
        out_specs=[pl.BlockSpec((tq, HEAD_DIM), lambda h, i: (i, h)),
                   pl.BlockSpec((T, HEAD_DIM), lambda h, i: (0, h)),
                   pl.BlockSpec((T, HEAD_DIM), lambda h, i: (0, h)),
                   pl.BlockSpec((None, 1, T), lambda h, i: (h, 0, 0)),
                   pl.BlockSpec((None, nkb, tq, tq), lambda h, i: (_bias_slot(h), 0, 0, 0))],
        out_shape=[head_cols, head_cols, head_cols,
                   jax.ShapeDtypeStruct((N_HEADS, 1, T), F32),
                   jax.ShapeDtypeStruct((1 + N_DIL, nkb, tq, tq), F32)],
        scratch_shapes=[pltpu.VMEM((T, HEAD_DIM), F32), pltpu.VMEM((T, HEAD_DIM), F32)],
        compiler_params=_params(("arbitrary", "arbitrary")),
    )(qkv, qkv, qkv, c, ct, tiles, do)


def rel_table_grad(dtiles, T, name):
    tq = dtiles.shape[2]
    nkb = T // tq
    bucket, _, delta = _distance_tables(T, tq)
    bidx = np.where(delta >= 0, bucket[np.maximum(delta, 0)], -1).astype(np.int32)
    present = [sorted(set(np.unique(bidx[k]).tolist()) - {-1}) for k in range(nkb)]

    def body(d_ref, b_ref, o_ref):
        lane = lax.broadcasted_iota(jnp.int32, (1, LANES), 1)
        row = jnp.zeros((1, LANES), F32)
        for k in range(nkb):
            d = d_ref[k]
            bi = b_ref[k]
            for b in present[k]:
                v = jnp.sum(jnp.sum(jnp.where(bi == b, d, 0.0), axis=0, keepdims=True),
                            axis=1, keepdims=True)
                row = row + jnp.where(lane == b, v, 0.0)
        o_ref[...] = row

    return pl.pallas_call(
        body, name=name, grid=(N_DIL,),
        in_specs=[pl.BlockSpec((None, nkb, tq, tq), lambda h: (h + 1, 0, 0, 0)),
                  pl.BlockSpec((nkb, tq, tq), lambda h: (0, 0, 0))],
        out_specs=pl.BlockSpec((None, 1, LANES), lambda h: (h, 0, 0)),
        out_shape=jax.ShapeDtypeStruct((N_DIL, 1, LANES), F32),
        compiler_params=_params(("parallel",)),
    )(dtiles, jnp.asarray(bidx))


def ple_combine(x, z, pp, name):
    T, D = x.shape
    tm = _tile(T, 256)

    def body(x_ref, z_ref, p_ref, o_ref):
        o_ref[...] = x_ref[...] + _sigmoid(z_ref[...]) * p_ref[...]

    row = pl.BlockSpec((tm, D), lambda i: (i, 0))
    return pl.pallas_call(
        body, name=name, grid=(T // tm,), in_specs=[row, row, row], out_specs=row,
        out_shape=jax.ShapeDtypeStruct((T, D), F32), compiler_params=_params(("parallel",)),
    )(x, z, pp)


def ple_bwd_elem(dx, z, pp, name):
    T, D = dx.shape
    tm = _tile(T, 256)

    def body(dx_ref, z_ref, p_ref, dz_ref, dp_ref):
        gate = _sigmoid(z_ref[...])
        d = dx_ref[...]
        dz_ref[...] = (d * p_ref[...] * gate * (1.0 - gate)).astype(BF16)
        dp_ref[...] = (d * gate).astype(BF16)

    row = pl.BlockSpec((tm, D), lambda i: (i, 0))
    shp = jax.ShapeDtypeStruct((T, D), BF16)
    return pl.pallas_call(
        body, name=name, grid=(T // tm,), in_specs=[row, row, row], out_specs=[row, row],
        out_shape=[shp, shp], compiler_params=_params(("parallel",)),
    )(dx, z, pp)


def _peer_list():
    x, y, c = lax.axis_index("x"), lax.axis_index("y"), lax.axis_index("c")
    me = 4 * x + 2 * y + c
    peers = []
    for fx in (0, 1):
        for fy in (0, 1):
            for fc in (0, 1):
                if fx or fy or fc:
                    px, py, pc = x ^ fx, y ^ fy, c ^ fc
                    peers.append(((px, py, pc), 4 * px + 2 * py + pc))
    return me, peers


def gather_weights(shards, name):
    n = len(shards)

    def body(*refs):
        ins, outs = refs[:n], refs[n:2 * n]
        send_sems, recv_sems, local_sems = refs[2 * n:]
        me, peers = _peer_list()
        local = [pltpu.make_async_copy(ins[a], outs[a].at[me], local_sems.at[a]) for a in range(n)]
        for cp in local:
            cp.start()
        sends, recvs = [], []
        for k, (dev, idx) in enumerate(peers):
            for a in range(n):
                s = k * n + a
                sends.append(pltpu.make_async_remote_copy(
                    src_ref=ins[a], dst_ref=outs[a].at[me], send_sem=send_sems.at[s],
                    recv_sem=recv_sems.at[s], device_id=dev, device_id_type=MESH))
                recvs.append(pltpu.make_async_remote_copy(
                    src_ref=ins[a], dst_ref=outs[a].at[idx], send_sem=send_sems.at[s],
                    recv_sem=recv_sems.at[s], device_id=dev, device_id_type=MESH))
        for cp in sends:
            cp.start()
        for cp in recvs:
            cp.wait_recv()
        for cp in sends:
            cp.wait_send()
        for cp in local:
            cp.wait()

    anyspec = pl.BlockSpec(memory_space=pl.ANY)
    return pl.pallas_call(
        body, name=name,
        in_specs=[anyspec] * n, out_specs=[anyspec] * n,
        out_shape=[jax.ShapeDtypeStruct((N_DEV,) + s.shape, s.dtype) for s in shards],
        scratch_shapes=[pltpu.SemaphoreType.DMA((7 * n,)), pltpu.SemaphoreType.DMA((7 * n,)),
                        pltpu.SemaphoreType.DMA((n,))],
    )(*shards)


def scatter_grads(fulls, small, name):
    arrays = list(fulls) + [small]
    n = len(arrays)

    def body(*refs):
        ins, outs = refs[:n], refs[n:2 * n]
        send_sems, recv_sems, local_sems = refs[2 * n:]
        me, peers = _peer_list()

        def src(a, dst_idx):
            return ins[a] if a == n - 1 else ins[a].at[dst_idx]

        local = [pltpu.make_async_copy(src(a, me), outs[a].at[me], local_sems.at[a]) for a in range(n)]
        for cp in local:
            cp.start()
        sends, recvs = [], []
        for k, (dev, idx) in enumerate(peers):
            for a in range(n):
                s = k * n + a
                sends.append(pltpu.make_async_remote_copy(
                    src_ref=src(a, idx), dst_ref=outs[a].at[me], send_sem=send_sems.at[s],
                    recv_sem=recv_sems.at[s], device_id=dev, device_id_type=MESH))
                recvs.append(pltpu.make_async_remote_copy(
                    src_ref=src(a, idx), dst_ref=outs[a].at[idx], send_sem=send_sems.at[s],
                    recv_sem=recv_sems.at[s], device_id=dev, device_id_type=MESH))
        for cp in sends:
            cp.start()
        for cp in recvs:
            cp.wait_recv()
        for cp in sends:
            cp.wait_send()
        for cp in local:
            cp.wait()

    anyspec = pl.BlockSpec(memory_space=pl.ANY)
    out_shape = [jax.ShapeDtypeStruct(f.shape, f.dtype) for f in fulls]
    out_shape.append(jax.ShapeDtypeStruct((N_DEV,) + small.shape, small.dtype))
    return pl.pallas_call(
        body, name=name,
        in_specs=[anyspec] * n, out_specs=[anyspec] * n, out_shape=out_shape,
        scratch_shapes=[pltpu.SemaphoreType.DMA((7 * n,)), pltpu.SemaphoreType.DMA((7 * n,)),
                        pltpu.SemaphoreType.DMA((n,))],
    )(*arrays)


def _adamw_math(w, g, m, v):
    m = ADAM_B1 * m + (1.0 - ADAM_B1) * g
    v = ADAM_B2 * v + (1.0 - ADAM_B2) * (g * g)
    m_hat = m / (1.0 - ADAM_B1 ** ADAM_STEP)
    v_hat = v / (1.0 - ADAM_B2 ** ADAM_STEP)
    delta = -ADAM_LR * (m_hat / (jnp.sqrt(v_hat) + ADAM_EPS) + ADAM_WD * w)
    return delta, m, v


def adamw_sharded(parts, w, m, v, name):
    R, C = w.shape
    tr = R
    for cand in (128, 64, 32, 16):
        if R % cand == 0:
            tr = cand
            break

    def body(p_ref, w_ref, m_ref, v_ref, g_ref, d_ref, nm_ref, nv_ref):
        g = p_ref[0].astype(F32)
        for s in range(1, N_DEV):
            g = g + p_ref[s].astype(F32)
        delta, nm, nv = _adamw_math(w_ref[...], g, m_ref[...], v_ref[...])
        g_ref[...] = g
        d_ref[...] = delta
        nm_ref[...] = nm
        nv_ref[...] = nv

    row = pl.BlockSpec((tr, C), lambda i: (i, 0))
    shp = jax.ShapeDtypeStruct((R, C), F32)
    return pl.pallas_call(
        body, name=name, grid=(R // tr,),
        in_specs=[pl.BlockSpec((N_DEV, tr, C), lambda i: (0, i, 0)), row, row, row],
        out_specs=[row, row, row, row], out_shape=[shp, shp, shp, shp],
        compiler_params=_params(("parallel",)),
    )(parts, w, m, v)


def adamw_small(parts, w, m, v, name):
    R, C = w.shape

    def body(p_ref, w_ref, m_ref, v_ref, g_ref, d_ref, nm_ref, nv_ref):
        g = p_ref[0]
        for s in range(1, N_DEV):
            g = g + p_ref[s]
        delta, nm, nv = _adamw_math(w_ref[...], g, m_ref[...], v_ref[...])
        g_ref[...] = g
        d_ref[...] = delta
        nm_ref[...] = nm
        nv_ref[...] = nv

    shp = jax.ShapeDtypeStruct((R, C), F32)
    return pl.pallas_call(
        body, name=name, out_shape=[shp, shp, shp, shp], compiler_params=_params(None),
    )(parts, w, m, v)


_ROW_NORM_FFN1, _ROW_NORM_MIX, _ROW_NORM_FFN2, _ROW_NORM_PLE, _ROW_NORM_FINAL = 0, 1, 2, 3, 4
_ROW_B_F, _ROW_REL, _ROW_LOSS, _SMALL_ROWS = 5, 6, 7, 8


def _pack_small(D, norm_ffn1, norm_mix, norm_ffn2, norm_ple, norm_final, b_f, rel_table):
    def row(v):
        v = v.reshape(1, -1)
        return jnp.pad(v, ((0, 0), (0, D - v.shape[1])))
    return jnp.concatenate([row(norm_ffn1), row(norm_mix), row(norm_ffn2), row(norm_ple),
                            row(norm_final), row(b_f), row(rel_table),
                            jnp.zeros((1, D), F32)], axis=0)


def _unpack_small(a, shapes):
    return {"norm_ffn1": a[_ROW_NORM_FFN1].reshape(shapes["norm_ffn1"]),
            "norm_mix": a[_ROW_NORM_MIX].reshape(shapes["norm_mix"]),
            "b_f": a[_ROW_B_F, :N_FOX].reshape(shapes["b_f"]),
            "norm_ffn2": a[_ROW_NORM_FFN2].reshape(shapes["norm_ffn2"]),
            "norm_ple": a[_ROW_NORM_PLE].reshape(shapes["norm_ple"]),
            "rel_table": a[_ROW_REL, :N_REL_BUCKETS * N_DIL].reshape(shapes["rel_table"]),
            "norm_final": a[_ROW_NORM_FINAL].reshape(shapes["norm_final"])}


def local_step(x, p, tgt, g_ffn1, g_mix, g_ffn2, g_ple, g_final, b_f, rel_table,
               wgu, wd, w3, wf, wo, wpg, wpp):
    T, D = x.shape
    P = p.shape[1]
    CW = D // N_DEV
    tq = _tile(T, 256)

    h1 = rms_fwd(x, g_ffn1, "rms_ffn1")
    a1, b1, s1 = ffn_up(h1, wgu, 0, 1, 4, "ffn1_up")
    x1 = ffn_down(s1, wd, 0, 2, x, "ffn1_down")

    h2 = rms_fwd(x1, g_mix, "rms_mix")
    qkv = mm_nn(h2, w3, "mix_qkv", tn=768, out_dtype=BF16)
    uf = mm_nn(h2, wf, "mix_forget", tn=LANES, out_dtype=F32)
    bfp = jnp.pad(b_f.reshape(1, N_FOX), ((0, 0), (0, LANES - N_FOX)))
    c, ct = fox_gate_fwd(uf, bfp, "fox_gate")
    tiles = bias_tiles(rel_table, T, tq)
    cat = attention_fwd(qkv, c, ct, tiles, "attention")
    x2 = mm_nn(cat, wo, "mix_out", tn=512, out_dtype=F32, res=x1)

    h3 = rms_fwd(x2, g_ffn2, "rms_ffn2")
    a2, b2, s2 = ffn_up(h3, wgu, 2, 3, 4, "ffn2_up")
    x3 = ffn_down(s2, wd, 1, 2, x2, "ffn2_down")

    h4 = rms_fwd(x3, g_ple, "rms_ple")
    z = mm_nn(h4, wpg, "ple_gate", tn=512, out_dtype=F32)
    pp = mm_nn(p, wpp, "ple_proj", tn=CW, out_dtype=F32, n_out=D,
               b_block=(P, CW), b_map=lambda n, i: (n, 0))
    x4 = ple_combine(x3, z, pp, "ple_combine")
    loss_row, dx4, dg_final = final_loss_bwd(x4, g_final, tgt, "final_loss")

    grads = {}
    dz, dpp = ple_bwd_elem(dx4, z, pp, "ple_bwd_elem")
    grads["w_ple_proj"] = mm_tn(p, dpp, "ple_proj_dw", grid=(N_DEV,),
                                a_block=(T, P), a_map=lambda n: (0, 0),
                                b_block=(T, CW), b_map=lambda n: (0, n),
                                o_block=(P, CW), o_map=lambda n: (n, 0),
                                out_shape=(N_DEV * P, CW))
    grads["w_ple_gate"] = mm_tn_plain(h4, dz, "ple_gate_dw")
    dh4 = mm_nt([(dz, wpg)], "ple_gate_dh", tn=512, out_dtype=F32)
    dx3, dg_ple = rms_bwd(dh4, x3, g_ple, dx4, "rms_ple_bwd")

    da2, db2 = ffn_bwd_act(dx3, wd, 1, 2, a2, b2, "ffn2_bwd_act")
    grads["ffn2_w_down"] = ffn_bwd_dw_down(s2, dx3, "ffn2_down_dw")
    grads["ffn2_w_gate"] = ffn_bwd_dw_in(h3, da2, "ffn2_gate_dw")
    grads["ffn2_w_up"] = ffn_bwd_dw_in(h3, db2, "ffn2_up_dw")
    dh3 = ffn_bwd_dh(da2, db2, wgu, 2, 3, 4, D, "ffn2_bwd_dh")
    dx2, dg_ffn2 = rms_bwd(dh3, x2, g_ffn2, dx3, "rms_ffn2_bwd")

    dcat = mm_nt([(dx2, wo)], "mix_out_dh", tn=512, out_dtype=BF16)
    grads["w_o"] = mm_tn_plain(cat, dx2, "mix_out_dw")
    dq, dk, dv, dct, dtiles = attention_bwd(qkv, c, ct, tiles, dcat, "attention_bwd")
    dctp = jnp.pad(dct[:, 0, :], ((0, LANES - N_HEADS), (0, 0)))
    duf, dbf = fox_gate_bwd(dctp, uf, bfp, "fox_gate_bwd")
    drel = rel_table_grad(dtiles, T, "rel_table_grad")[:, 0, :N_REL_BUCKETS].T
    du3 = jnp.concatenate([dq, dk, dv], axis=1)
    grads["w3"] = mm_tn_plain(h2, du3, "mix_qkv_dw", tn=768)
    grads["wf"] = mm_tn_plain(h2, duf, "mix_forget_dw", tn=LANES)
    dh2 = mm_nt([(du3, w3), (duf, wf)], "mix_in_dh", tn=512, out_dtype=F32)
    dx1, dg_mix = rms_bwd(dh2, x1, g_mix, dx2, "rms_mix_bwd")

    da1, db1 = ffn_bwd_act(dx1, wd, 0, 2, a1, b1, "ffn1_bwd_act")
    grads["ffn1_w_down"] = ffn_bwd_dw_down(s1, dx1, "ffn1_down_dw")
    grads["ffn1_w_gate"] = ffn_bwd_dw_in(h1, da1, "ffn1_gate_dw")
    grads["ffn1_w_up"] = ffn_bwd_dw_in(h1, db1, "ffn1_up_dw")
    dh1 = ffn_bwd_dh(da1, db1, wgu, 0, 1, 4, D, "ffn1_bwd_dh")
    dx0, dg_ffn1 = rms_bwd(dh1, x, g_ffn1, dx1, "rms_ffn1_bwd")

    small = _pack_small(D, dg_ffn1, dg_mix, dg_ffn2, dg_ple, dg_final, dbf[:, :N_FOX], drel)
    small = small.at[_ROW_LOSS, :LANES].set(loss_row[0])
    return grads, dx0, small


def _split_w_in(w_in_full):
    df, dd = N_FOX * HEAD_DIM, N_DIL * HEAD_DIM
    o = np.cumsum([0, df, df, df, N_FOX, dd, dd, dd]).tolist()
    qa, ka, va, f, qb, kb, vb = [w_in_full[:, o[i]:o[i + 1]] for i in range(7)]
    return jnp.concatenate([qa, qb, ka, kb, va, vb], axis=1), f


def _join_w_in(d3, dfg):
    df, dd = N_FOX * HEAD_DIM, N_DIL * HEAD_DIM
    o = np.cumsum([0, df, dd, df, dd, df, dd]).tolist()
    qa, qb, ka, kb, va, vb = [d3[:, o[i]:o[i + 1]] for i in range(6)]
    return jnp.concatenate([qa, ka, va, dfg, qb, kb, vb], axis=1)


def kernel(x, p, norm_ffn1, ffn1_w_gate, ffn1_w_up, ffn1_w_down, norm_mix, w_in, b_f, w_o, norm_ffn2, ffn2_w_gate, ffn2_w_up, ffn2_w_down, norm_ple, w_ple_gate, w_ple_proj, rel_table, norm_final, loss_target, m_norm_ffn1, m_ffn1_w_gate, m_ffn1_w_up, m_ffn1_w_down, m_norm_mix, m_w_in, m_b_f, m_w_o, m_norm_ffn2, m_ffn2_w_gate, m_ffn2_w_up, m_ffn2_w_down, m_norm_ple, m_w_ple_gate, m_w_ple_proj, m_rel_table, m_norm_final, v_norm_ffn1, v_ffn1_w_gate, v_ffn1_w_up, v_ffn1_w_down, v_norm_mix, v_w_in, v_b_f, v_w_o, v_norm_ffn2, v_ffn2_w_gate, v_ffn2_w_up, v_ffn2_w_down, v_norm_ple, v_w_ple_gate, v_w_ple_proj, v_rel_table, v_norm_final):
    names = ["norm_ffn1", "ffn1_w_gate", "ffn1_w_up", "ffn1_w_down", "norm_mix", "w_in", "b_f", "w_o",
             "norm_ffn2", "ffn2_w_gate", "ffn2_w_up", "ffn2_w_down", "norm_ple", "w_ple_gate",
             "w_ple_proj", "rel_table", "norm_final"]
    w = dict(zip(names, [norm_ffn1, ffn1_w_gate, ffn1_w_up, ffn1_w_down, norm_mix, w_in, b_f, w_o,
                         norm_ffn2, ffn2_w_gate, ffn2_w_up, ffn2_w_down, norm_ple, w_ple_gate,
                         w_ple_proj, rel_table, norm_final]))
    m = dict(zip(names, [m_norm_ffn1, m_ffn1_w_gate, m_ffn1_w_up, m_ffn1_w_down, m_norm_mix, m_w_in,
                         m_b_f, m_w_o, m_norm_ffn2, m_ffn2_w_gate, m_ffn2_w_up, m_ffn2_w_down,
                         m_norm_ple, m_w_ple_gate, m_w_ple_proj, m_rel_table, m_norm_final]))
    v = dict(zip(names, [v_norm_ffn1, v_ffn1_w_gate, v_ffn1_w_up, v_ffn1_w_down, v_norm_mix, v_w_in,
                         v_b_f, v_w_o, v_norm_ffn2, v_ffn2_w_gate, v_ffn2_w_up, v_ffn2_w_down,
                         v_norm_ple, v_w_ple_gate, v_w_ple_proj, v_rel_table, v_norm_final]))
    sharded = ["ffn1_w_gate", "ffn1_w_up", "ffn1_w_down", "w_in", "w_o", "ffn2_w_gate", "ffn2_w_up",
               "ffn2_w_down", "w_ple_gate", "w_ple_proj"]
    small_names = [n for n in names if n not in sharded]

    xs, ps, tgt = x[0], p[0, 0], loss_target[0]
    T, D = xs.shape
    sh = {n: w[n][0] for n in sharded}
    F8 = sh["ffn1_w_gate"].shape[1]
    WIN8 = sh["w_in"].shape[1]

    wgu_s = jnp.concatenate([sh["ffn1_w_gate"], sh["ffn1_w_up"], sh["ffn2_w_gate"], sh["ffn2_w_up"]],
                            axis=0).astype(BF16)
    wd_s = jnp.concatenate([sh["ffn1_w_down"], sh["ffn2_w_down"]], axis=0).astype(BF16)
    shards = [wgu_s, wd_s, sh["w_in"].astype(BF16), sh["w_o"].astype(BF16),
              sh["w_ple_gate"].astype(BF16), sh["w_ple_proj"].astype(BF16)]
    wgu_g, wd_g, win_g, wo_g, wpg_g, wpp_g = gather_weights(shards, "gather_weights")
    wgu = wgu_g.reshape(N_DEV * 4 * D, F8)
    wd = wd_g.reshape(N_DEV * 2 * F8, D)
    w_in_full = jnp.transpose(win_g, (1, 0, 2)).reshape(D, N_DEV * WIN8)
    w3, wf8 = _split_w_in(w_in_full)
    wf = jnp.pad(wf8, ((0, 0), (0, LANES - N_FOX)))
    wo = wo_g.reshape(-1, D)
    wpg = wpg_g.reshape(-1, D)
    wpp = wpp_g.reshape(-1, wpp_g.shape[2])

    grads, dx0, small = local_step(
        xs, ps, tgt, w["norm_ffn1"], w["norm_mix"], w["norm_ffn2"], w["norm_ple"],
        w["norm_final"].reshape(1, D), w["b_f"], w["rel_table"], wgu, wd, w3, wf, wo, wpg, wpp)

    d_w_in = _join_w_in(grads["w3"], grads["wf"][:, :N_FOX])
    d_w_in = jnp.transpose(d_w_in.reshape(D, N_DEV, WIN8), (1, 0, 2))
    full = {n: grads[n].reshape((N_DEV,) + sh[n].shape) for n in sharded if n != "w_in"}
    full["w_in"] = d_w_in
    outs = scatter_grads([full[n] for n in sharded], small, "scatter_grads")
    parts = dict(zip(sharded, outs[:-1]))
    small_parts = outs[-1]

    res = {}
    for n in sharded:
        g, d, nm, nv = adamw_sharded(parts[n], sh[n], m[n][0], v[n][0], "adamw_" + n)
        res[n] = tuple(a.reshape(w[n].shape) for a in (g, d, nm, nv))
    pack = lambda t: _pack_small(D, t["norm_ffn1"], t["norm_mix"], t["norm_ffn2"], t["norm_ple"],
                                 t["norm_final"], t["b_f"], t["rel_table"])
    gs, ds, ms, vs = adamw_small(small_parts, pack(w), pack(m), pack(v), "adamw_small")
    shapes = {n: w[n].shape for n in small_names}
    unpacked = [_unpack_small(a, shapes) for a in (gs, ds, ms, vs)]
    for n in small_names:
        res[n] = tuple(u[n] for u in unpacked)
    loss = gs[_ROW_LOSS, 0]

    out = [loss, dx0.reshape(x.shape)]
    for k in range(4):
        out += [res[n][k] for n in names]
    return tuple(out)
```

```python
import functools
import math

import numpy as np
import jax
import jax.numpy as jnp
from jax import lax
from jax.experimental import pallas as pl
from jax.experimental.pallas import tpu as pltpu

F32 = jnp.float32
BF16 = jnp.bfloat16

N_DEV = 8
HEAD_DIM = 128
N_FOX = 8
N_DIL = 8
N_HEADS = N_FOX + N_DIL
DILATED_PATTERNS = ((128, 1), (512, 4), (2048, 16))
N_REL_BUCKETS = 32
REL_MAX_DISTANCE = 2048
RMS_EPS = 1e-6
NEG_INF = -1e30
LANES = 128
VMEM_LIMIT = 56 * 1024 * 1024

ADAM_LR = 0.001
ADAM_B1 = 0.9
ADAM_B2 = 0.999
ADAM_EPS = 1e-08
ADAM_WD = 0.01
ADAM_STEP = 10

MESH = pl.DeviceIdType.MESH


def _params(sem):
    return pltpu.CompilerParams(dimension_semantics=sem, vmem_limit_bytes=VMEM_LIMIT)


def _dot(a, b, ca, cb, precision=None):
    return lax.dot_general(a, b, (((ca,), (cb,)), ((), ())),
                           preferred_element_type=F32, precision=precision)


def _sigmoid(z):
    return 1.0 / (1.0 + jnp.exp(-z))


def _tile(n, want):
    t = min(n, want)
    assert n % t == 0, (n, t)
    return t


def _dep_spec(ngrid):
    return pl.BlockSpec((8, LANES), lambda *_: (0, 0))


def rms_fwd(x, g, name, dep=None):
    T, D = x.shape
    tm = _tile(T, 256)

    def body(x_ref, g_ref, *rest):
        h_ref = rest[-1]
        xv = x_ref[...]
        r = lax.rsqrt(jnp.mean(xv * xv, axis=-1, keepdims=True) + RMS_EPS)
        h_ref[...] = (xv * r * g_ref[...]).astype(BF16)

    in_specs = [pl.BlockSpec((tm, D), lambda i: (i, 0)), pl.BlockSpec((1, D), lambda i: (0, 0))]
    args = [x, g]
    if dep is not None:
        in_specs.append(_dep_spec(1))
        args.append(dep)
    return pl.pallas_call(
        body, name=name, grid=(T // tm,), in_specs=in_specs,
        out_specs=pl.BlockSpec((tm, D), lambda i: (i, 0)),
        out_shape=jax.ShapeDtypeStruct((T, D), BF16),
        compiler_params=_params(("parallel",)),
    )(*args)


def rms_bwd(dh, x, g, dres, name):
    T, D = x.shape
    tm = _tile(T, 256)

    def body(dh_ref, x_ref, g_ref, dres_ref, dx_ref, dg_ref):
        i = pl.program_id(0)
        xv = x_ref[...]
        r = lax.rsqrt(jnp.mean(xv * xv, axis=-1, keepdims=True) + RMS_EPS)
        xh = xv * r
        d = dh_ref[...]
        u = d * g_ref[...]
        dx_ref[...] = dres_ref[...] + r * (u - xh * jnp.mean(u * xh, axis=-1, keepdims=True))
        part = jnp.sum(d * xh, axis=0, keepdims=True)

        @pl.when(i == 0)
        def _():
            dg_ref[...] = part

        @pl.when(i > 0)
        def _():
            dg_ref[...] += part

    row = pl.BlockSpec((tm, D), lambda i: (i, 0))
    vec = pl.BlockSpec((1, D), lambda i: (0, 0))
    return pl.pallas_call(
        body, name=name, grid=(T // tm,),
        in_specs=[row, row, vec, row], out_specs=[row, vec],
        out_shape=[jax.ShapeDtypeStruct((T, D), F32), jax.ShapeDtypeStruct((1, D), F32)],
        compiler_params=_params(("arbitrary",)),
    )(dh, x, g, dres)


def final_loss_bwd(x, g, target, name):
    T, D = x.shape
    tm = _tile(T, 256)

    def body(x_ref, g_ref, t_ref, loss_ref, dx_ref, dg_ref):
        i = pl.program_id(0)
        xv = x_ref[...]
        gv = g_ref[...]
        r = lax.rsqrt(jnp.mean(xv * xv, axis=-1, keepdims=True) + RMS_EPS)
        xh = xv * r
        e = xh * gv - t_ref[...]
        lpart = 0.5 * jnp.sum(jnp.mean(e * e, axis=-1, keepdims=True), axis=0, keepdims=True)
        lrow = jnp.broadcast_to(lpart, (1, LANES))
        d = e * (1.0 / D)
        u = d * gv
        dx_ref[...] = r * (u - xh * jnp.mean(u * xh, axis=-1, keepdims=True))
        part = jnp.sum(d * xh, axis=0, keepdims=True)

        @pl.when(i == 0)
        def _():
            dg_ref[...] = part
            loss_ref[...] = lrow

        @pl.when(i > 0)
        def _():
            dg_ref[...] += part
            loss_ref[...] += lrow

    row = pl.BlockSpec((tm, D), lambda i: (i, 0))
    vec = pl.BlockSpec((1, D), lambda i: (0, 0))
    return pl.pallas_call(
        body, name=name, grid=(T // tm,),
        in_specs=[row, vec, row],
        out_specs=[pl.BlockSpec((1, LANES), lambda i: (0, 0)), row, vec],
        out_shape=[jax.ShapeDtypeStruct((1, LANES), F32), jax.ShapeDtypeStruct((T, D), F32),
                   jax.ShapeDtypeStruct((1, D), F32)],
        compiler_params=_params(("arbitrary",)),
    )(x, g, target)


def _bf(v, scale=None):
    if scale is not None:
        v = v * scale
    return v.astype(BF16)


def mm_nn(a, b, name, *, tn, out_dtype, tm=512, n_out=None, b_block=None, b_map=None,
          res=None):
    T, K = a.shape
    N = n_out if n_out is not None else b.shape[1]
    tm = _tile(T, tm)
    tn = _tile(N, tn)
    b_block = b_block or (K, tn)
    b_map = b_map or (lambda n, i: (0, n))

    def body(*refs):
        a_ref, b_ref = refs[0], refs[1]
        o_ref = refs[-1]
        acc = _dot(_bf(a_ref[...]), _bf(b_ref[...]), 1, 0)
        if res is not None:
            acc = refs[2][...] + acc
        o_ref[...] = acc.astype(out_dtype)

    in_specs = [pl.BlockSpec((tm, K), lambda n, i: (i, 0)), pl.BlockSpec(b_block, b_map)]
    args = [a, b]
    if res is not None:
        in_specs.append(pl.BlockSpec((tm, tn), lambda n, i: (i, n)))
        args.append(res)
    return pl.pallas_call(
        body, name=name, grid=(N // tn, T // tm), in_specs=in_specs,
        out_specs=pl.BlockSpec((tm, tn), lambda n, i: (i, n)),
        out_shape=jax.ShapeDtypeStruct((T, N), out_dtype),
        compiler_params=_params(("parallel", "parallel")),
    )(*args)


def mm_nt(pairs, name, *, tn, out_dtype, tm=512, dep=None):
    T = pairs[0][0].shape[0]
    N = pairs[0][1].shape[0]
    tm = _tile(T, tm)
    tn = _tile(N, tn)
    npair = len(pairs)

    def body(*refs):
        o_ref = refs[-1]
        acc = None
        for q in range(npair):
            part = _dot(_bf(refs[2 * q][...]), _bf(refs[2 * q + 1][...]), 1, 1)
            acc = part if acc is None else acc + part
        o_ref[...] = acc.astype(out_dtype)

    in_specs, args = [], []
    for a, b in pairs:
        K = a.shape[1]
        in_specs += [pl.BlockSpec((tm, K), lambda n, i: (i, 0)), pl.BlockSpec((tn, K), lambda n, i: (n, 0))]
        args += [a, b]
    if dep is not None:
        in_specs.append(_dep_spec(2))
        args.append(dep)
    return pl.pallas_call(
        body, name=name, grid=(N // tn, T // tm), in_specs=in_specs,
        out_specs=pl.BlockSpec((tm, tn), lambda n, i: (i, n)),
        out_shape=jax.ShapeDtypeStruct((T, N), out_dtype),
        compiler_params=_params(("parallel", "parallel")),
    )(*args)


def mm_tn(a, b, name, *, grid, a_block, a_map, b_block, b_map, o_block, o_map, out_shape,
          b_scale=None):
    def body(a_ref, b_ref, o_ref):
        o_ref[...] = _dot(_bf(a_ref[...]), _bf(b_ref[...], b_scale), 0, 0).astype(BF16)

    return pl.pallas_call(
        body, name=name, grid=grid,
        in_specs=[pl.BlockSpec(a_block, a_map), pl.BlockSpec(b_block, b_map)],
        out_specs=pl.BlockSpec(o_block, o_map),
        out_shape=jax.ShapeDtypeStruct(out_shape, BF16),
        compiler_params=_params(("parallel",) * len(grid)),
    )(a, b)


def mm_tn_plain(a, b, name, *, tm=512, tn=512, b_scale=None):
    T, M = a.shape
    N = b.shape[1]
    tm = _tile(M, tm)
    tn = _tile(N, tn)
    return mm_tn(a, b, name, grid=(M // tm, N // tn),
                 a_block=(T, tm), a_map=lambda m, n: (0, m),
                 b_block=(T, tn), b_map=lambda m, n: (0, n),
                 o_block=(tm, tn), o_map=lambda m, n: (m, n),
                 out_shape=(M, N), b_scale=b_scale)


def ffn_up(h, wgu, gi, ui, nper, name):
    T, D = h.shape
    F8 = wgu.shape[1]
    tm = _tile(T, 512)
    nt = T // tm

    def body(h_ref, wg_ref, wu_ref, a_ref, b_ref, s_ref):
        hv = h_ref[...]
        a = _dot(hv, wg_ref[...], 1, 0)
        b = _dot(hv, wu_ref[...], 1, 0)
        a_ref[...] = a.astype(BF16)
        b_ref[...] = b.astype(BF16)
        s_ref[...] = (a * _sigmoid(a) * b).astype(BF16)

    blk = pl.BlockSpec((tm, F8), lambda j, i: (j * nt + i, 0))
    shp = jax.ShapeDtypeStruct((N_DEV * T, F8), BF16)
    return pl.pallas_call(
        body, name=name, grid=(N_DEV, nt),
        in_specs=[pl.BlockSpec((tm, D), lambda j, i: (i, 0)),
                  pl.BlockSpec((D, F8), lambda j, i: (j * nper + gi, 0)),
                  pl.BlockSpec((D, F8), lambda j, i: (j * nper + ui, 0))],
        out_specs=[blk, blk, blk], out_shape=[shp, shp, shp],
        compiler_params=_params(("parallel", "parallel")),
    )(h, wgu, wgu)


def ffn_down(s, wd, di, nper, x, name):
    T, D = x.shape
    F8 = s.shape[1]
    tm = _tile(T, 512)
    nt = T // tm

    def body(s_ref, w_ref, x_ref, o_ref, acc_ref):
        j = pl.program_id(1)
        part = _dot(s_ref[...], w_ref[...], 1, 0)

        @pl.when(j == 0)
        def _():
            acc_ref[...] = part

        @pl.when(j > 0)
        def _():
            acc_ref[...] += part

        @pl.when(j == N_DEV - 1)
        def _():
            o_ref[...] = x_ref[...] + 0.5 * acc_ref[...]

    return pl.pallas_call(
        body, name=name, grid=(nt, N_DEV),
        in_specs=[pl.BlockSpec((tm, F8), lambda i, j: (j * nt + i, 0)),
                  pl.BlockSpec((F8, D), lambda i, j: (j * nper + di, 0)),
                  pl.BlockSpec((tm, D), lambda i, j: (i, 0))],
        out_specs=pl.BlockSpec((tm, D), lambda i, j: (i, 0)),
        out_shape=jax.ShapeDtypeStruct((T, D), F32),
        scratch_shapes=[pltpu.VMEM((tm, D), F32)],
        compiler_params=_params(("parallel", "arbitrary")),
    )(s, wd, x)


def ffn_bwd_act(dx, wd, di, nper_d, a, b, name, dep=None):
    T, D = dx.shape
    F8 = a.shape[1]
    tm = _tile(T, 512)
    nt = T // tm

    def body(dx_ref, w_ref, a_ref, b_ref, *rest):
        da_ref, db_ref = rest[-2], rest[-1]
        ds = _dot(_bf(dx_ref[...], 0.5), w_ref[...], 1, 1)
        av = a_ref[...].astype(F32)
        bv = b_ref[...].astype(F32)
        sg = _sigmoid(av)
        da_ref[...] = (ds * bv * (sg * (1.0 + av * (1.0 - sg)))).astype(BF16)
        db_ref[...] = (ds * (av * sg)).astype(BF16)

    blk = pl.BlockSpec((tm, F8), lambda j, i: (j * nt + i, 0))
    shp = jax.ShapeDtypeStruct((N_DEV * T, F8), BF16)
    in_specs = [pl.BlockSpec((tm, D), lambda j, i: (i, 0)),
                pl.BlockSpec((F8, D), lambda j, i: (j * nper_d + di, 0)), blk, blk]
    args = [dx, wd, a, b]
    if dep is not None:
        in_specs.append(_dep_spec(2))
        args.append(dep)
    return pl.pallas_call(
        body, name=name, grid=(N_DEV, nt), in_specs=in_specs,
        out_specs=[blk, blk], out_shape=[shp, shp],
        compiler_params=_params(("parallel", "parallel")),
    )(*args)


def ffn_bwd_dh(da, db, wgu, gi, ui, nper, D, name, dep=None):
    F8 = da.shape[1]
    T = da.shape[0] // N_DEV
    tm = _tile(T, 512)
    nt = T // tm

    def body(da_ref, db_ref, wg_ref, wu_ref, *rest):
        o_ref, acc_ref = rest[-2], rest[-1]
        j = pl.program_id(1)
        part = _dot(da_ref[...], wg_ref[...], 1, 1) + _dot(db_ref[...], wu_ref[...], 1, 1)

        @pl.when(j == 0)
        def _():
            acc_ref[...] = part

        @pl.when(j > 0)
        def _():
            acc_ref[...] += part

        @pl.when(j == N_DEV - 1)
        def _():
            o_ref[...] = acc_ref[...]

    blk = pl.BlockSpec((tm, F8), lambda i, j: (j * nt + i, 0))
    in_specs = [blk, blk,
                pl.BlockSpec((D, F8), lambda i, j: (j * nper + gi, 0)),
                pl.BlockSpec((D, F8), lambda i, j: (j * nper + ui, 0))]
    args = [da, db, wgu, wgu]
    if dep is not None:
        in_specs.append(_dep_spec(2))
        args.append(dep)
    return pl.pallas_call(
        body, name=name, grid=(nt, N_DEV), in_specs=in_specs,
        out_specs=pl.BlockSpec((tm, D), lambda i, j: (i, 0)),
        out_shape=jax.ShapeDtypeStruct((T, D), F32),
        scratch_shapes=[pltpu.VMEM((tm, D), F32)],
        compiler_params=_params(("parallel", "arbitrary")),
    )(*args)


def ffn_bwd_dw_in(h, dact, name):
    T, D = h.shape
    F8 = dact.shape[1]
    tm = _tile(D, 512)
    nm = D // tm
    return mm_tn(h, dact, name, grid=(N_DEV, nm),
                 a_block=(T, tm), a_map=lambda j, m: (0, m),
                 b_block=(T, F8), b_map=lambda j, m: (j, 0),
                 o_block=(tm, F8), o_map=lambda j, m: (j * nm + m, 0),
                 out_shape=(N_DEV * D, F8))


def ffn_bwd_dw_down(s, dx, name):
    T, D = dx.shape
    F8 = s.shape[1]
    tn = _tile(D, 512)
    return mm_tn(s, dx, name, grid=(N_DEV, D // tn),
                 a_block=(T, F8), a_map=lambda j, n: (j, 0),
                 b_block=(T, tn), b_map=lambda j, n: (0, n),
                 o_block=(F8, tn), o_map=lambda j, n: (j, n),
                 out_shape=(N_DEV * F8, D), b_scale=0.5)


def _t5_bucket_np(dist):
    max_exact = N_REL_BUCKETS // 2
    d = np.maximum(dist, 1).astype(np.float64)
    large = max_exact + (np.log(d / max_exact) / math.log(REL_MAX_DISTANCE / max_exact)
                         * (N_REL_BUCKETS - max_exact)).astype(np.int64)
    large32 = max_exact + (np.log(d.astype(np.float32) / np.float32(max_exact))
                           / np.float32(math.log(REL_MAX_DISTANCE / max_exact))
                           * np.float32(N_REL_BUCKETS - max_exact)).astype(np.int64)
    assert np.array_equal(large, large32)
    large = np.minimum(large, N_REL_BUCKETS - 1)
    return np.where(dist < max_exact, dist, large)


def _distance_tables(T, tq):
    dist = np.arange(T)
    mult = np.zeros(T, np.int64)
    for window, dilation in DILATED_PATTERNS:
        mult += ((dist % dilation == 0) & (dist // dilation <= window // dilation)).astype(np.int64)
    logm = np.where(mult > 0, np.log(np.maximum(mult, 1)), NEG_INF).astype(np.float32)
    bucket = _t5_bucket_np(dist).astype(np.int32)
    nkb = T // tq
    k = np.arange(nkb)[:, None, None]
    r = np.arange(tq)[None, :, None]
    c = np.arange(tq)[None, None, :]
    delta = k * tq + r - c
    return bucket, logm, delta


def _tile_buckets(T, tq):
    bucket, logm, delta = _distance_tables(T, tq)
    safe = np.maximum(delta, 0)
    bidx = np.where(delta >= 0, bucket[safe], -1).astype(np.int32)
    logm_t = np.where(delta >= 0, logm[safe], NEG_INF).astype(np.float32)
    present = [sorted(set(np.unique(bidx[k]).tolist()) - {-1}) for k in range(T // tq)]
    return bidx, logm_t, present


def bias_tiles(rel_table, T, tq):
    bidx, logm_t, present = _tile_buckets(T, tq)
    nkb = T // tq

    def body(tab_ref, b_ref, lm_ref, o_ref):
        slot = pl.program_id(0)

        @pl.when(slot == 0)
        def _():
            o_ref[...] = jnp.where(b_ref[...] >= 0, 0.0, NEG_INF)

        @pl.when(slot > 0)
        def _():
            for k in range(nkb):
                bi = b_ref[k]
                acc = lm_ref[k]
                for b in present[k]:
                    acc = acc + jnp.where(bi == b, tab_ref[b, slot - 1], 0.0)
                o_ref[k] = acc

    full = pl.BlockSpec((nkb, tq, tq), lambda s: (0, 0, 0))
    return pl.pallas_call(
        body, name="bias_tiles", grid=(1 + N_DIL,),
        in_specs=[pl.BlockSpec(memory_space=pltpu.SMEM), full, full],
        out_specs=pl.BlockSpec((None, nkb, tq, tq), lambda s: (s, 0, 0, 0)),
        out_shape=jax.ShapeDtypeStruct((1 + N_DIL, nkb, tq, tq), F32),
        compiler_params=_params(("parallel",)),
    )(rel_table, jnp.asarray(bidx), jnp.asarray(logm_t))


def fox_gate_fwd(uf, bf, name):
    T = uf.shape[0]
    tb = _tile(T, 512)

    def body(u_ref, b_ref, c_ref, ct_ref):
        lane = lax.broadcasted_iota(jnp.int32, (1, LANES), 1)
        tri = (lax.broadcasted_iota(jnp.int32, (tb, tb), 0)
               >= lax.broadcasted_iota(jnp.int32, (tb, tb), 1)).astype(F32)
        carry = jnp.zeros((1, LANES), F32)
        for blk in range(T // tb):
            z = u_ref[pl.ds(blk * tb, tb), :] + b_ref[...]
            lf = jnp.minimum(z, 0.0) - jnp.log1p(jnp.exp(-jnp.abs(z)))
            lf = jnp.where(lane < N_FOX, lf, 0.0)
            cb = _dot(tri, lf, 1, 0, precision=lax.Precision.HIGHEST) + carry
            c_ref[pl.ds(blk * tb, tb), :] = cb
            ct_ref[:, pl.ds(blk * tb, tb)] = cb.T
            carry = cb[tb - 1:tb, :]

    return pl.pallas_call(
        body, name=name,
        out_shape=[jax.ShapeDtypeStruct((T, LANES), F32), jax.ShapeDtypeStruct((LANES, T), F32)],
        compiler_params=_params(None),
    )(uf, bf)


def fox_gate_bwd(dct, uf, bf, name):
    T = uf.shape[0]
    tb = _tile(T, 512)

    def body(d_ref, u_ref, b_ref, du_ref, db_ref):
        lane = lax.broadcasted_iota(jnp.int32, (1, LANES), 1)
        triu = (lax.broadcasted_iota(jnp.int32, (tb, tb), 0)
                <= lax.broadcasted_iota(jnp.int32, (tb, tb), 1)).astype(F32)
        carry = jnp.zeros((1, LANES), F32)
        dbv = jnp.zeros((1, LANES), F32)
        for blk in reversed(range(T // tb)):
            dc = d_ref[:, pl.ds(blk * tb, tb)].T
            dlf = _dot(triu, dc, 1, 0, precision=lax.Precision.HIGHEST) + carry
            carry = dlf[0:1, :]
            z = u_ref[pl.ds(blk * tb, tb), :] + b_ref[...]
            dz = jnp.where(lane < N_FOX, dlf * (1.0 - _sigmoid(z)), 0.0)
            du_ref[pl.ds(blk * tb, tb), :] = dz
            dbv = dbv + jnp.sum(dz, axis=0, keepdims=True)
        db_ref[...] = dbv

    return pl.pallas_call(
        body, name=name,
        out_shape=[jax.ShapeDtypeStruct((T, LANES), F32), jax.ShapeDtypeStruct((1, LANES), F32)],
        compiler_params=_params(None),
    )(dct, uf, bf)


def _bias_slot(h):
    return jnp.maximum(h - (N_FOX - 1), 0)


def _scores(q_ref, k_ref, c_ref, ct_ref, tb_ref, h, i, T, tq):
    scale = HEAD_DIM ** -0.5
    nkb = T // tq
    s = _dot(q_ref[...], k_ref[...], 1, 1) * scale
    lane = lax.broadcasted_iota(jnp.int32, (1, LANES), 1)
    c_col = jnp.sum(jnp.where(lane == h, c_ref[...], 0.0), axis=1, keepdims=True)
    c_row = ct_ref[pl.ds(h, 1), :]
    chunks = []
    for jb in range(nkb):
        kk = i - jb
        tile = tb_ref[jnp.maximum(kk, 0)]
        chunks.append(jnp.where(kk >= 0, tile, NEG_INF))
    bias = jnp.concatenate(chunks, axis=1)
    return s + (c_col - c_row) + bias


def _attn_specs(T, tq):
    nkb = T // tq
    return [
        pl.BlockSpec((tq, HEAD_DIM), lambda h, i: (i, h)),
        pl.BlockSpec((T, HEAD_DIM), lambda h, i: (0, N_HEADS + h)),
        pl.BlockSpec((T, HEAD_DIM), lambda h, i: (0, 2 * N_HEADS + h)),
        pl.BlockSpec((tq, LANES), lambda h, i: (i, 0)),
        pl.BlockSpec((LANES, T), lambda h, i: (0, 0)),
        pl.BlockSpec((None, nkb, tq, tq), lambda h, i: (_bias_slot(h), 0, 0, 0)),
    ]


def attention_fwd(qkv, c, ct, tiles, name):
    T = qkv.shape[0]
    tq = tiles.shape[2]

    def body(q_ref, k_ref, v_ref, c_ref, ct_ref, tb_ref, o_ref):
        h, i = pl.program_id(0), pl.program_id(1)
        s = _scores(q_ref, k_ref, c_ref, ct_ref, tb_ref, h, i, T, tq)
        p = jnp.exp(s - jnp.max(s, axis=1, keepdims=True))
        l = jnp.sum(p, axis=1, keepdims=True)
        o_ref[...] = _dot((p / l).astype(BF16), v_ref[...], 1, 0).astype(BF16)

    return pl.pallas_call(
        body, name=name, grid=(N_HEADS, T // tq),
        in_specs=[qkv_spec for qkv_spec in _attn_specs(T, tq)],
        out_specs=pl.BlockSpec((tq, HEAD_DIM), lambda h, i: (i, h)),
        out_shape=jax.ShapeDtypeStruct((T, N_HEADS * HEAD_DIM), BF16),
        compiler_params=_params(("parallel", "parallel")),
    )(qkv, qkv, qkv, c, ct, tiles)


def attention_bwd(qkv, c, ct, tiles, do, name):
    T = qkv.shape[0]
    tq = tiles.shape[2]
    nkb = T // tq
    nq = T // tq
    scale = HEAD_DIM ** -0.5

    def body(q_ref, k_ref, v_ref, c_ref, ct_ref, tb_ref, do_ref,
             dq_ref, dk_ref, dv_ref, dct_ref, dtb_ref, dk_acc, dv_acc):
        h, i = pl.program_id(0), pl.program_id(1)
        s = _scores(q_ref, k_ref, c_ref, ct_ref, tb_ref, h, i, T, tq)
        p = jnp.exp(s - jnp.max(s, axis=1, keepdims=True))
        p = p / jnp.sum(p, axis=1, keepdims=True)
        dov = do_ref[...]
        dp = _dot(dov, v_ref[...], 1, 1)
        ds = p * (dp - jnp.sum(p * dp, axis=1, keepdims=True))
        ds_b = ds.astype(BF16)
        dq_ref[...] = (_dot(ds_b, k_ref[...], 1, 0) * scale).astype(BF16)
        dk_part = _dot(ds_b, q_ref[...], 0, 0) * scale
        dv_part = _dot(p.astype(BF16), dov, 0, 0)
        dc_part = -jnp.sum(ds, axis=0, keepdims=True)

        @pl.when(i == 0)
        def _():
            dk_acc[...] = dk_part
            dv_acc[...] = dv_part
            dct_ref[...] = dc_part
            dtb_ref[...] = jnp.zeros_like(dtb_ref)

        @pl.when(i > 0)
        def _():
            dk_acc[...] += dk_part
            dv_acc[...] += dv_part
            dct_ref[...] += dc_part

        for jb in range(nkb):
            kk = i - jb

            @pl.when(kk >= 0)
            def _():
                dtb_ref[kk] += ds[:, jb * tq:(jb + 1) * tq]

        @pl.when(i == nq - 1)
        def _():
            dk_ref[...] = dk_acc[...].astype(BF16)
            dv_ref[...] = dv_acc[...].astype(BF16)

    head_cols = jax.ShapeDtypeStruct((T, N_HEADS * HEAD_DIM), BF16)
    return pl.pallas_call(
        body, name=name, grid=(N_HEADS, nq),
        in_specs=_attn_specs(T, tq) + [pl.BlockSpec((tq, HEAD_DIM), lambda h, i: (i, h))],
        out_specs=[pl.BlockSpec((tq, HEAD_DIM), lambda h, i: (i, h)),
                   pl.BlockSpec((T, HEAD_DIM), lambda h, i: (0, h)),
                   pl.BlockSpec((T, HEAD_DIM), lambda h, i: (0, h)),
                   pl.BlockSpec((None, 1, T), lambda h, i: (h, 0, 0)),
                   pl.BlockSpec((None, nkb, tq, tq), lambda h, i: (_bias_slot(h), 0, 0, 0))],
        out_shape=[head_cols, head_cols, head_cols,
                   jax.ShapeDtypeStruct((N_HEADS, 1, T), F32),
                   jax.ShapeDtypeStruct((1 + N_DIL, nkb, tq, tq), F32)],
        scratch_shapes=[pltpu.VMEM((T, HEAD_DIM), F32), pltpu.VMEM((T, HEAD_DIM), F32)],
        compiler_params=_params(("arbitrary", "arbitrary")),
    )(qkv, qkv, qkv, c, ct, tiles, do)


def rel_table_grad(dtiles, T, name):
    tq = dtiles.shape[2]
    nkb = T // tq
    bidx, _, present = _tile_buckets(T, tq)

    def body(d_ref, b_ref, o_ref):
        lane = lax.broadcasted_iota(jnp.int32, (1, LANES), 1)
        row = jnp.zeros((1, LANES), F32)
        for k in range(nkb):
            d = d_ref[k]
            bi = b_ref[k]
            for b in present[k]:
                v = jnp.sum(jnp.sum(jnp.where(bi == b, d, 0.0), axis=0, keepdims=True),
                            axis=1, keepdims=True)
                row = row + jnp.where(lane == b, v, 0.0)
        o_ref[...] = row

    return pl.pallas_call(
        body, name=name, grid=(N_DIL,),
        in_specs=[pl.BlockSpec((None, nkb, tq, tq), lambda h: (h + 1, 0, 0, 0)),
                  pl.BlockSpec((nkb, tq, tq), lambda h: (0, 0, 0))],
        out_specs=pl.BlockSpec((None, 1, LANES), lambda h: (h, 0, 0)),
        out_shape=jax.ShapeDtypeStruct((N_DIL, 1, LANES), F32),
        compiler_params=_params(("parallel",)),
    )(dtiles, jnp.asarray(bidx))


def ple_combine(x, z, pp, name):
    T, D = x.shape
    tm = _tile(T, 256)

    def body(x_ref, z_ref, p_ref, o_ref):
        o_ref[...] = x_ref[...] + _sigmoid(z_ref[...]) * p_ref[...]

    row = pl.BlockSpec((tm, D), lambda i: (i, 0))
    return pl.pallas_call(
        body, name=name, grid=(T // tm,), in_specs=[row, row, row], out_specs=row,
        out_shape=jax.ShapeDtypeStruct((T, D), F32), compiler_params=_params(("parallel",)),
    )(x, z, pp)


def ple_bwd_elem(dx, z, pp, name):
    T, D = dx.shape
    tm = _tile(T, 256)

    def body(dx_ref, z_ref, p_ref, dz_ref, dp_ref):
        gate = _sigmoid(z_ref[...])
        d = dx_ref[...]
        dz_ref[...] = (d * p_ref[...] * gate * (1.0 - gate)).astype(BF16)
        dp_ref[...] = (d * gate).astype(BF16)

    row = pl.BlockSpec((tm, D), lambda i: (i, 0))
    shp = jax.ShapeDtypeStruct((T, D), BF16)
    return pl.pallas_call(
        body, name=name, grid=(T // tm,), in_specs=[row, row, row], out_specs=[row, row],
        out_shape=[shp, shp], compiler_params=_params(("parallel",)),
    )(dx, z, pp)


def _peer_list():
    x, y, c = lax.axis_index("x"), lax.axis_index("y"), lax.axis_index("c")
    me = 4 * x + 2 * y + c
    peers = []
    for fx in (0, 1):
        for fy in (0, 1):
            for fc in (0, 1):
                if fx or fy or fc:
                    px = 1 - x if fx else x
                    py = 1 - y if fy else y
                    pc = 1 - c if fc else c
                    peers.append(((px, py, pc), 4 * px + 2 * py + pc))
    return me, peers


_HBM = pl.BlockSpec(memory_space=pltpu.HBM)
_SEM = pl.BlockSpec(memory_space=pltpu.SEMAPHORE)
_EFFECT = pltpu.SideEffectType.DATAFLOW_SIDE_EFFECTING
N_PEERS = N_DEV - 1


def _in_hbm(a):
    return pltpu.with_memory_space_constraint(a, pltpu.HBM)


def _exchange_copies(srcs, lands, send_sems, recv_sems, blockwise):
    me, peers = _peer_list()
    sends, recvs = [], []
    for a in range(len(srcs)):
        for k, (dev, idx) in enumerate(peers):
            src = srcs[a].at[idx] if blockwise[a] else srcs[a]
            sends.append(pltpu.make_async_remote_copy(
                src_ref=src, dst_ref=lands[a].at[me], send_sem=send_sems[a].at[k],
                recv_sem=recv_sems[a].at[k], device_id=dev, device_id_type=MESH))
            recvs.append(pltpu.make_async_remote_copy(
                src_ref=src, dst_ref=lands[a].at[idx], send_sem=send_sems[a].at[k],
                recv_sem=recv_sems[a].at[k], device_id=dev, device_id_type=MESH))
    return sends, recvs


def exchange_start(srcs, lands, blockwise, name):
    n = len(srcs)

    def body(*refs):
        src_in, land_in = refs[:n], refs[n:2 * n]
        send_sems, recv_sems = refs[2 * n:3 * n], refs[3 * n:4 * n]
        token = refs[6 * n]
        sends, _ = _exchange_copies(src_in, land_in, send_sems, recv_sems, blockwise)
        for cp in sends:
            cp.start()
        token[...] = jnp.zeros_like(token)

    out_shape = ([pltpu.SemaphoreType.DMA((N_PEERS,))] * (2 * n)
                 + [pltpu.HBM(s.shape, s.dtype) for s in srcs]
                 + [pltpu.HBM(l.shape, l.dtype) for l in lands]
                 + [jax.ShapeDtypeStruct((8, LANES), F32)])
    aliases = {a: 2 * n + a for a in range(2 * n)}
    outs = pl.pallas_call(
        body, name=name, out_shape=out_shape,
        in_specs=[_HBM] * (2 * n),
        out_specs=[_SEM] * (2 * n) + [_HBM] * (2 * n) + [pl.BlockSpec(memory_space=pltpu.VMEM)],
        input_output_aliases=aliases,
        compiler_params=pltpu.CompilerParams(has_side_effects=_EFFECT),
    )(*[_in_hbm(s) for s in srcs], *[_in_hbm(l) for l in lands])
    return (outs[:n], outs[n:2 * n], outs[2 * n:3 * n], outs[3 * n:4 * n], outs[4 * n])


def exchange_wait(send_sems, recv_sems, srcs, lands, blockwise, after, name):
    n = len(srcs)

    def body(*refs):
        src_in, land_in = refs[:n], refs[n:2 * n]
        ss, rs = refs[2 * n:3 * n], refs[3 * n:4 * n]
        sends, recvs = _exchange_copies(src_in, land_in, ss, rs, blockwise)
        for cp in sends:
            cp.wait_send()
        for cp in recvs:
            cp.wait_recv()

    outs = pl.pallas_call(
        body, name=name,
        out_shape=[pltpu.HBM(s.shape, s.dtype) for s in srcs] + [pltpu.HBM(l.shape, l.dtype) for l in lands],
        in_specs=[_HBM] * (2 * n) + [_SEM] * (2 * n) + [pl.BlockSpec(memory_space=pl.ANY)],
        out_specs=[_HBM] * (2 * n),
        input_output_aliases={a: a for a in range(2 * n)},
        compiler_params=pltpu.CompilerParams(has_side_effects=_EFFECT),
    )(*srcs, *lands, *send_sems, *recv_sems, after)
    return outs[n:]


def _landing(own_block, me):
    empty = lax.empty((N_DEV,) + own_block.shape, own_block.dtype)
    return lax.dynamic_update_slice(empty, own_block[None], (me,) + (0,) * own_block.ndim)


def _adamw_math(w, g, m, v):
    m = ADAM_B1 * m + (1.0 - ADAM_B1) * g
    v = ADAM_B2 * v + (1.0 - ADAM_B2) * (g * g)
    m_hat = m / (1.0 - ADAM_B1 ** ADAM_STEP)
    v_hat = v / (1.0 - ADAM_B2 ** ADAM_STEP)
    delta = -ADAM_LR * (m_hat / (jnp.sqrt(v_hat) + ADAM_EPS) + ADAM_WD * w)
    return delta, m, v


def adamw_sharded(parts, w, m, v, name):
    R, C = w.shape
    tr = R
    for cand in (128, 64, 32, 16):
        if R % cand == 0:
            tr = cand
            break

    def body(p_ref, w_ref, m_ref, v_ref, g_ref, d_ref, nm_ref, nv_ref):
        g = p_ref[0].astype(F32)
        for s in range(1, N_DEV):
            g = g + p_ref[s].astype(F32)
        delta, nm, nv = _adamw_math(w_ref[...], g, m_ref[...], v_ref[...])
        g_ref[...] = g
        d_ref[...] = delta
        nm_ref[...] = nm
        nv_ref[...] = nv

    row = pl.BlockSpec((tr, C), lambda i: (i, 0))
    shp = jax.ShapeDtypeStruct((R, C), F32)
    return pl.pallas_call(
        body, name=name, grid=(R // tr,),
        in_specs=[pl.BlockSpec((N_DEV, tr, C), lambda i: (0, i, 0)), row, row, row],
        out_specs=[row, row, row, row], out_shape=[shp, shp, shp, shp],
        compiler_params=_params(("parallel",)),
    )(parts, w, m, v)


def adamw_small(parts, w, m, v, name):
    R, C = w.shape

    def body(p_ref, w_ref, m_ref, v_ref, g_ref, d_ref, nm_ref, nv_ref):
        g = p_ref[0]
        for s in range(1, N_DEV):
            g = g + p_ref[s]
        delta, nm, nv = _adamw_math(w_ref[...], g, m_ref[...], v_ref[...])
        g_ref[...] = g
        d_ref[...] = delta
        nm_ref[...] = nm
        nv_ref[...] = nv

    shp = jax.ShapeDtypeStruct((R, C), F32)
    return pl.pallas_call(
        body, name=name, out_shape=[shp, shp, shp, shp], compiler_params=_params(None),
    )(parts, w, m, v)


_ROW_NORM_FFN1, _ROW_NORM_MIX, _ROW_NORM_FFN2, _ROW_NORM_PLE, _ROW_NORM_FINAL = 0, 1, 2, 3, 4
_ROW_B_F, _ROW_REL, _ROW_LOSS, _SMALL_ROWS = 5, 6, 7, 8


def _pack_small(D, norm_ffn1, norm_mix, norm_ffn2, norm_ple, norm_final, b_f, rel_table):
    def row(v):
        v = v.reshape(1, -1)
        return jnp.pad(v, ((0, 0), (0, D - v.shape[1])))
    return jnp.concatenate([row(norm_ffn1), row(norm_mix), row(norm_ffn2), row(norm_ple),
                            row(norm_final), row(b_f), row(rel_table),
                            jnp.zeros((1, D), F32)], axis=0)


def _unpack_small(a, shapes):
    return {"norm_ffn1": a[_ROW_NORM_FFN1].reshape(shapes["norm_ffn1"]),
            "norm_mix": a[_ROW_NORM_MIX].reshape(shapes["norm_mix"]),
            "b_f": a[_ROW_B_F, :N_FOX].reshape(shapes["b_f"]),
            "norm_ffn2": a[_ROW_NORM_FFN2].reshape(shapes["norm_ffn2"]),
            "norm_ple": a[_ROW_NORM_PLE].reshape(shapes["norm_ple"]),
            "rel_table": a[_ROW_REL, :N_REL_BUCKETS * N_DIL].reshape(shapes["rel_table"]),
            "norm_final": a[_ROW_NORM_FINAL].reshape(shapes["norm_final"])}


def local_step(x, p, tgt, g_ffn1, g_mix, g_ffn2, g_ple, g_final, b_f, rel_table,
               weights, emit, first_dep):
    T, D = x.shape
    P = p.shape[1]
    CW = D // N_DEV
    tq = _tile(T, 256)

    h1 = rms_fwd(x, g_ffn1, "rms_ffn1", dep=first_dep)
    wgu1, wd1 = weights("ffn1", h1)
    a1, b1, s1 = ffn_up(h1, wgu1, 0, 1, 2, "ffn1_up")
    x1 = ffn_down(s1, wd1, 0, 1, x, "ffn1_down")

    h2 = rms_fwd(x1, g_mix, "rms_mix")
    w3, wf, wo = weights("mix", h2)
    qkv = mm_nn(h2, w3, "mix_qkv", tn=768, out_dtype=BF16)
    uf = mm_nn(h2, wf, "mix_forget", tn=LANES, out_dtype=F32)
    bfp = jnp.pad(b_f.reshape(1, N_FOX), ((0, 0), (0, LANES - N_FOX)))
    c, ct = fox_gate_fwd(uf, bfp, "fox_gate")
    tiles = bias_tiles(rel_table, T, tq)
    cat = attention_fwd(qkv, c, ct, tiles, "attention")
    x2 = mm_nn(cat, wo, "mix_out", tn=512, out_dtype=F32, res=x1)

    h3 = rms_fwd(x2, g_ffn2, "rms_ffn2")
    wgu2, wd2 = weights("ffn2", h3)
    a2, b2, s2 = ffn_up(h3, wgu2, 0, 1, 2, "ffn2_up")
    x3 = ffn_down(s2, wd2, 0, 1, x2, "ffn2_down")

    h4 = rms_fwd(x3, g_ple, "rms_ple")
    wpg, wpp = weights("ple", h4)
    z = mm_nn(h4, wpg, "ple_gate", tn=512, out_dtype=F32)
    pp = mm_nn(p, wpp, "ple_proj", tn=CW, out_dtype=F32, n_out=D,
               b_block=(P, CW), b_map=lambda n, i: (n, 0))
    x4 = ple_combine(x3, z, pp, "ple_combine")
    loss_row, dx4, dg_final = final_loss_bwd(x4, g_final, tgt, "final_loss")

    grads = {}
    dz, dpp = ple_bwd_elem(dx4, z, pp, "ple_bwd_elem")
    grads["w_ple_proj"] = mm_tn(p, dpp, "ple_proj_dw", grid=(N_DEV,),
                                a_block=(T, P), a_map=lambda n: (0, 0),
                                b_block=(T, CW), b_map=lambda n: (0, n),
                                o_block=(P, CW), o_map=lambda n: (n, 0),
                                out_shape=(N_DEV * P, CW))
    grads["w_ple_gate"] = mm_tn_plain(h4, dz, "ple_gate_dw")
    tok = emit("ple", grads)
    dh4 = mm_nt([(dz, wpg)], "ple_gate_dh", tn=512, out_dtype=F32, dep=tok)
    dx3, dg_ple = rms_bwd(dh4, x3, g_ple, dx4, "rms_ple_bwd")

    da2, db2 = ffn_bwd_act(dx3, wd2, 0, 1, a2, b2, "ffn2_bwd_act")
    grads["ffn2_w_down"] = ffn_bwd_dw_down(s2, dx3, "ffn2_down_dw")
    grads["ffn2_w_gate"] = ffn_bwd_dw_in(h3, da2, "ffn2_gate_dw")
    grads["ffn2_w_up"] = ffn_bwd_dw_in(h3, db2, "ffn2_up_dw")
    tok = emit("ffn2", grads)
    dh3 = ffn_bwd_dh(da2, db2, wgu2, 0, 1, 2, D, "ffn2_bwd_dh")
    dx2, dg_ffn2 = rms_bwd(dh3, x2, g_ffn2, dx3, "rms_ffn2_bwd")

    dcat = mm_nt([(dx2, wo)], "mix_out_dh", tn=512, out_dtype=BF16, dep=tok)
    grads["w_o"] = mm_tn_plain(cat, dx2, "mix_out_dw")
    dq, dk, dv, dct, dtiles = attention_bwd(qkv, c, ct, tiles, dcat, "attention_bwd")
    dctp = jnp.pad(dct[:, 0, :], ((0, LANES - N_HEADS), (0, 0)))
    duf, dbf = fox_gate_bwd(dctp, uf, bfp, "fox_gate_bwd")
    drel = rel_table_grad(dtiles, T, "rel_table_grad")[:, 0, :N_REL_BUCKETS].T
    du3 = jnp.concatenate([dq, dk, dv], axis=1)
    grads["w3"] = mm_tn_plain(h2, du3, "mix_qkv_dw", tn=768)
    grads["wf"] = mm_tn_plain(h2, duf, "mix_forget_dw", tn=LANES)
    tok = emit("mix", grads)
    dh2 = mm_nt([(du3, w3), (duf, wf)], "mix_in_dh", tn=512, out_dtype=F32, dep=tok)
    dx1, dg_mix = rms_bwd(dh2, x1, g_mix, dx2, "rms_mix_bwd")

    da1, db1 = ffn_bwd_act(dx1, wd1, 0, 1, a1, b1, "ffn1_bwd_act")
    grads["ffn1_w_down"] = ffn_bwd_dw_down(s1, dx1, "ffn1_down_dw")
    grads["ffn1_w_gate"] = ffn_bwd_dw_in(h1, da1, "ffn1_gate_dw")
    grads["ffn1_w_up"] = ffn_bwd_dw_in(h1, db1, "ffn1_up_dw")
    tok = emit("ffn1", grads)
    dh1 = ffn_bwd_dh(da1, db1, wgu1, 0, 1, 2, D, "ffn1_bwd_dh", dep=tok)
    dx0, dg_ffn1 = rms_bwd(dh1, x, g_ffn1, dx1, "rms_ffn1_bwd")

    small = _pack_small(D, dg_ffn1, dg_mix, dg_ffn2, dg_ple, dg_final, dbf[:, :N_FOX], drel)
    small = small.at[_ROW_LOSS, :LANES].set(loss_row[0])
    grads["small"] = small
    emit("small", grads)
    return dx0


def _split_w_in(w_in_full):
    df, dd = N_FOX * HEAD_DIM, N_DIL * HEAD_DIM
    o = np.cumsum([0, df, df, df, N_FOX, dd, dd, dd]).tolist()
    qa, ka, va, f, qb, kb, vb = [w_in_full[:, o[i]:o[i + 1]] for i in range(7)]
    return jnp.concatenate([qa, qb, ka, kb, va, vb], axis=1), f


def _join_w_in(d3, dfg):
    df, dd = N_FOX * HEAD_DIM, N_DIL * HEAD_DIM
    o = np.cumsum([0, df, dd, df, dd, df, dd]).tolist()
    qa, qb, ka, kb, va, vb = [d3[:, o[i]:o[i + 1]] for i in range(6)]
    return jnp.concatenate([qa, ka, va, dfg, qb, kb, vb], axis=1)


def kernel(x, p, norm_ffn1, ffn1_w_gate, ffn1_w_up, ffn1_w_down, norm_mix, w_in, b_f, w_o, norm_ffn2, ffn2_w_gate, ffn2_w_up, ffn2_w_down, norm_ple, w_ple_gate, w_ple_proj, rel_table, norm_final, loss_target, m_norm_ffn1, m_ffn1_w_gate, m_ffn1_w_up, m_ffn1_w_down, m_norm_mix, m_w_in, m_b_f, m_w_o, m_norm_ffn2, m_ffn2_w_gate, m_ffn2_w_up, m_ffn2_w_down, m_norm_ple, m_w_ple_gate, m_w_ple_proj, m_rel_table, m_norm_final, v_norm_ffn1, v_ffn1_w_gate, v_ffn1_w_up, v_ffn1_w_down, v_norm_mix, v_w_in, v_b_f, v_w_o, v_norm_ffn2, v_ffn2_w_gate, v_ffn2_w_up, v_ffn2_w_down, v_norm_ple, v_w_ple_gate, v_w_ple_proj, v_rel_table, v_norm_final):
    names = ["norm_ffn1", "ffn1_w_gate", "ffn1_w_up", "ffn1_w_down", "norm_mix", "w_in", "b_f", "w_o",
             "norm_ffn2", "ffn2_w_gate", "ffn2_w_up", "ffn2_w_down", "norm_ple", "w_ple_gate",
             "w_ple_proj", "rel_table", "norm_final"]
    w = dict(zip(names, [norm_ffn1, ffn1_w_gate, ffn1_w_up, ffn1_w_down, norm_mix, w_in, b_f, w_o,
                         norm_ffn2, ffn2_w_gate, ffn2_w_up, ffn2_w_down, norm_ple, w_ple_gate,
                         w_ple_proj, rel_table, norm_final]))
    m = dict(zip(names, [m_norm_ffn1, m_ffn1_w_gate, m_ffn1_w_up, m_ffn1_w_down, m_norm_mix, m_w_in,
                         m_b_f, m_w_o, m_norm_ffn2, m_ffn2_w_gate, m_ffn2_w_up, m_ffn2_w_down,
                         m_norm_ple, m_w_ple_gate, m_w_ple_proj, m_rel_table, m_norm_final]))
    v = dict(zip(names, [v_norm_ffn1, v_ffn1_w_gate, v_ffn1_w_up, v_ffn1_w_down, v_norm_mix, v_w_in,
                         v_b_f, v_w_o, v_norm_ffn2, v_ffn2_w_gate, v_ffn2_w_up, v_ffn2_w_down,
                         v_norm_ple, v_w_ple_gate, v_w_ple_proj, v_rel_table, v_norm_final]))
    sharded = ["ffn1_w_gate", "ffn1_w_up", "ffn1_w_down", "w_in", "w_o", "ffn2_w_gate", "ffn2_w_up",
               "ffn2_w_down", "w_ple_gate", "w_ple_proj"]
    small_names = [n for n in names if n not in sharded]

    xs, ps, tgt = x[0], p[0, 0], loss_target[0]
    T, D = xs.shape
    sh = {n: w[n][0] for n in sharded}
    F8 = sh["ffn1_w_gate"].shape[1]
    WIN8 = sh["w_in"].shape[1]

    me = 4 * lax.axis_index("x") + 2 * lax.axis_index("y") + lax.axis_index("c")

    cat0 = lambda ns: jnp.concatenate([sh[n] for n in ns], axis=0).astype(BF16)
    gather_groups = {
        "ffn1": [cat0(["ffn1_w_gate", "ffn1_w_up"]), sh["ffn1_w_down"].astype(BF16)],
        "mix": [sh["w_in"].astype(BF16), sh["w_o"].astype(BF16)],
        "ffn2": [cat0(["ffn2_w_gate", "ffn2_w_up"]), sh["ffn2_w_down"].astype(BF16)],
        "ple": [sh["w_ple_gate"].astype(BF16), sh["w_ple_proj"].astype(BF16)],
    }
    order = ["ffn1", "mix", "ffn2", "ple"]
    g_srcs = [s for grp in order for s in gather_groups[grp]]
    g_lands = [_landing(s, me) for s in g_srcs]
    g_send, g_recv, g_srcs, g_lands, g_token = exchange_start(
        g_srcs, g_lands, [False] * len(g_srcs), "gather_start")

    def weights(group, after):
        k = 2 * order.index(group)
        sl = slice(k, k + 2)
        a0, a1 = exchange_wait(g_send[sl], g_recv[sl], g_srcs[sl], g_lands[sl], [False, False], after,
                               "gather_wait_" + group)
        if group in ("ffn1", "ffn2"):
            return a0.reshape(N_DEV * 2 * D, F8), a1.reshape(N_DEV * F8, D)
        if group == "ple":
            return a0.reshape(-1, D), a1.reshape(-1, a1.shape[2])
        w_in_full = jnp.transpose(a0, (1, 0, 2)).reshape(D, N_DEV * WIN8)
        w3, wf8 = _split_w_in(w_in_full)
        return w3, jnp.pad(wf8, ((0, 0), (0, LANES - N_FOX))), a1.reshape(-1, D)

    scatter_groups = {
        "ple": ["w_ple_gate", "w_ple_proj"],
        "ffn2": ["ffn2_w_gate", "ffn2_w_up", "ffn2_w_down"],
        "mix": ["w_in", "w_o"],
        "ffn1": ["ffn1_w_gate", "ffn1_w_up", "ffn1_w_down"],
        "small": ["small"],
    }
    started = {}

    def emit(group, grads):
        srcs = []
        for n in scatter_groups[group]:
            if n == "w_in":
                d_w_in = _join_w_in(grads["w3"], grads["wf"][:, :N_FOX])
                srcs.append(jnp.transpose(d_w_in.reshape(D, N_DEV, WIN8), (1, 0, 2)))
            elif n == "small":
                srcs.append(grads["small"])
            else:
                srcs.append(grads[n].reshape((N_DEV,) + sh[n].shape))
        blockwise = [n != "small" for n in scatter_groups[group]]
        lands = [_landing(lax.dynamic_index_in_dim(s, me, 0, keepdims=False) if b else s, me)
                 for s, b in zip(srcs, blockwise)]
        ss, rs, srcs, lands, token = exchange_start(srcs, lands, blockwise, "scatter_start_" + group)
        started[group] = (ss, rs, srcs, lands, blockwise)
        return token

    dx0 = local_step(
        xs, ps, tgt, w["norm_ffn1"], w["norm_mix"], w["norm_ffn2"], w["norm_ple"],
        w["norm_final"].reshape(1, D), w["b_f"], w["rel_table"], weights, emit, g_token)

    res = {}
    after = dx0
    for group in ["ple", "ffn2", "mix", "ffn1"]:
        ss, rs, srcs, lands, blockwise = started[group]
        parts = exchange_wait(ss, rs, srcs, lands, blockwise, after, "scatter_wait_" + group)
        for n, part in zip(scatter_groups[group], parts):
            g, d, nm, nv = adamw_sharded(part, sh[n], m[n][0], v[n][0], "adamw_" + n)
            res[n] = tuple(a.reshape(w[n].shape) for a in (g, d, nm, nv))
            after = g
    ss, rs, srcs, lands, blockwise = started["small"]
    small_parts, = exchange_wait(ss, rs, srcs, lands, blockwise, after, "scatter_wait_small")
    pack = lambda t: _pack_small(D, t["norm_ffn1"], t["norm_mix"], t["norm_ffn2"], t["norm_ple"],
                                 t["norm_final"], t["b_f"], t["rel_table"])
    gs, ds, ms, vs = adamw_small(small_parts, pack(w), pack(m), pack(v), "adamw_small")
    shapes = {n: w[n].shape for n in small_names}
    unpacked = [_unpack_small(a, shapes) for a in (gs, ds, ms, vs)]
    for n in small_names:
        res[n] = tuple(u[n] for u in unpacked)
    loss = gs[_ROW_LOSS, 0]

    out = [loss, dx0.reshape(x.shape)]
    for k in range(4):
        out += [res[n][k] for n in names]
    return tuple(out)
```

```python
import functools
import math

import numpy as np
import jax
import jax.numpy as jnp
from jax import lax
from jax.experimental import pallas as pl
from jax.experimental.pallas import tpu as pltpu

F32 = jnp.float32
BF16 = jnp.bfloat16

N_DEV = 8
HEAD_DIM = 128
N_FOX = 8
N_DIL = 8
N_HEADS = N_FOX + N_DIL
DILATED_PATTERNS = ((128, 1), (512, 4), (2048, 16))
N_REL_BUCKETS = 32
REL_MAX_DISTANCE = 2048
RMS_EPS = 1e-6
NEG_INF = -1e30
LANES = 128
VMEM_LIMIT = 56 * 1024 * 1024

ADAM_LR = 0.001
ADAM_B1 = 0.9
ADAM_B2 = 0.999
ADAM_EPS = 1e-08
ADAM_WD = 0.01
ADAM_STEP = 10

MESH = pl.DeviceIdType.MESH


def _params(sem):
    return pltpu.CompilerParams(dimension_semantics=sem, vmem_limit_bytes=VMEM_LIMIT)


def _dot(a, b, ca, cb, precision=None):
    return lax.dot_general(a, b, (((ca,), (cb,)), ((), ())),
                           preferred_element_type=F32, precision=precision)


def _sigmoid(z):
    return 1.0 / (1.0 + jnp.exp(-z))


def _tile(n, want):
    t = min(n, want)
    assert n % t == 0, (n, t)
    return t


def _dep_spec(ngrid):
    return pl.BlockSpec((8, LANES), lambda *_: (0, 0))


def rms_fwd(x, g, name, dep=None):
    T, D = x.shape
    tm = _tile(T, 256)

    def body(x_ref, g_ref, *rest):
        h_ref = rest[-1]
        xv = x_ref[...]
        r = lax.rsqrt(jnp.mean(xv * xv, axis=-1, keepdims=True) + RMS_EPS)
        h_ref[...] = (xv * r * g_ref[...]).astype(BF16)

    in_specs = [pl.BlockSpec((tm, D), lambda i: (i, 0)), pl.BlockSpec((1, D), lambda i: (0, 0))]
    args = [x, g]
    if dep is not None:
        in_specs.append(_dep_spec(1))
        args.append(dep)
    return pl.pallas_call(
        body, name=name, grid=(T // tm,), in_specs=in_specs,
        out_specs=pl.BlockSpec((tm, D), lambda i: (i, 0)),
        out_shape=jax.ShapeDtypeStruct((T, D), BF16),
        compiler_params=_params(("parallel",)),
    )(*args)


def rms_bwd(dh, x, g, dres, name, dep=None):
    T, D = x.shape
    tm = _tile(T, 256)

    def body(dh_ref, x_ref, g_ref, dres_ref, *rest):
        dx_ref, dg_ref = rest[-2], rest[-1]
        i = pl.program_id(0)
        xv = x_ref[...]
        r = lax.rsqrt(jnp.mean(xv * xv, axis=-1, keepdims=True) + RMS_EPS)
        xh = xv * r
        d = dh_ref[...]
        u = d * g_ref[...]
        dx_ref[...] = dres_ref[...] + r * (u - xh * jnp.mean(u * xh, axis=-1, keepdims=True))
        part = jnp.sum(d * xh, axis=0, keepdims=True)

        @pl.when(i == 0)
        def _():
            dg_ref[...] = part

        @pl.when(i > 0)
        def _():
            dg_ref[...] += part

    row = pl.BlockSpec((tm, D), lambda i: (i, 0))
    vec = pl.BlockSpec((1, D), lambda i: (0, 0))
    in_specs = [row, row, vec, row]
    args = [dh, x, g, dres]
    if dep is not None:
        in_specs.append(_dep_spec(1))
        args.append(dep)
    return pl.pallas_call(
        body, name=name, grid=(T // tm,),
        in_specs=in_specs, out_specs=[row, vec],
        out_shape=[jax.ShapeDtypeStruct((T, D), F32), jax.ShapeDtypeStruct((1, D), F32)],
        compiler_params=_params(("arbitrary",)),
    )(*args)


def final_loss_bwd(x, g, target, name):
    T, D = x.shape
    tm = _tile(T, 256)

    def body(x_ref, g_ref, t_ref, loss_ref, dx_ref, dg_ref):
        i = pl.program_id(0)
        xv = x_ref[...]
        gv = g_ref[...]
        r = lax.rsqrt(jnp.mean(xv * xv, axis=-1, keepdims=True) + RMS_EPS)
        xh = xv * r
        e = xh * gv - t_ref[...]
        lpart = 0.5 * jnp.sum(jnp.mean(e * e, axis=-1, keepdims=True), axis=0, keepdims=True)
        lrow = jnp.broadcast_to(lpart, (1, LANES))
        d = e * (1.0 / D)
        u = d * gv
        dx_ref[...] = r * (u - xh * jnp.mean(u * xh, axis=-1, keepdims=True))
        part = jnp.sum(d * xh, axis=0, keepdims=True)

        @pl.when(i == 0)
        def _():
            dg_ref[...] = part
            loss_ref[...] = lrow

        @pl.when(i > 0)
        def _():
            dg_ref[...] += part
            loss_ref[...] += lrow

    row = pl.BlockSpec((tm, D), lambda i: (i, 0))
    vec = pl.BlockSpec((1, D), lambda i: (0, 0))
    return pl.pallas_call(
        body, name=name, grid=(T // tm,),
        in_specs=[row, vec, row],
        out_specs=[pl.BlockSpec((1, LANES), lambda i: (0, 0)), row, vec],
        out_shape=[jax.ShapeDtypeStruct((1, LANES), F32), jax.ShapeDtypeStruct((T, D), F32),
                   jax.ShapeDtypeStruct((1, D), F32)],
        compiler_params=_params(("arbitrary",)),
    )(x, g, target)


def _bf(v, scale=None):
    if scale is not None:
        v = v * scale
    return v.astype(BF16)


def mm_nn(a, b, name, *, tn, out_dtype, tm=512, n_out=None, b_block=None, b_map=None,
          res=None):
    T, K = a.shape
    N = n_out if n_out is not None else b.shape[1]
    tm = _tile(T, tm)
    tn = _tile(N, tn)
    b_block = b_block or (K, tn)
    b_map = b_map or (lambda n, i: (0, n))

    def body(*refs):
        a_ref, b_ref = refs[0], refs[1]
        o_ref = refs[-1]
        acc = _dot(_bf(a_ref[...]), _bf(b_ref[...]), 1, 0)
        if res is not None:
            acc = refs[2][...] + acc
        o_ref[...] = acc.astype(out_dtype)

    in_specs = [pl.BlockSpec((tm, K), lambda n, i: (i, 0)), pl.BlockSpec(b_block, b_map)]
    args = [a, b]
    if res is not None:
        in_specs.append(pl.BlockSpec((tm, tn), lambda n, i: (i, n)))
        args.append(res)
    return pl.pallas_call(
        body, name=name, grid=(N // tn, T // tm), in_specs=in_specs,
        out_specs=pl.BlockSpec((tm, tn), lambda n, i: (i, n)),
        out_shape=jax.ShapeDtypeStruct((T, N), out_dtype),
        compiler_params=_params(("parallel", "parallel")),
    )(*args)


def mm_nt(pairs, name, *, tn, out_dtype, tm=512, dep=None):
    T = pairs[0][0].shape[0]
    N = pairs[0][1].shape[0]
    tm = _tile(T, tm)
    tn = _tile(N, tn)
    npair = len(pairs)

    def body(*refs):
        o_ref = refs[-1]
        acc = None
        for q in range(npair):
            part = _dot(_bf(refs[2 * q][...]), _bf(refs[2 * q + 1][...]), 1, 1)
            acc = part if acc is None else acc + part
        o_ref[...] = acc.astype(out_dtype)

    in_specs, args = [], []
    for a, b in pairs:
        K = a.shape[1]
        in_specs += [pl.BlockSpec((tm, K), lambda n, i: (i, 0)), pl.BlockSpec((tn, K), lambda n, i: (n, 0))]
        args += [a, b]
    if dep is not None:
        in_specs.append(_dep_spec(2))
        args.append(dep)
    return pl.pallas_call(
        body, name=name, grid=(N // tn, T // tm), in_specs=in_specs,
        out_specs=pl.BlockSpec((tm, tn), lambda n, i: (i, n)),
        out_shape=jax.ShapeDtypeStruct((T, N), out_dtype),
        compiler_params=_params(("parallel", "parallel")),
    )(*args)


def mm_tn(a, b, name, *, grid, a_block, a_map, b_block, b_map, o_block, o_map, out_shape,
          b_scale=None):
    def body(a_ref, b_ref, o_ref):
        o_ref[...] = _dot(_bf(a_ref[...]), _bf(b_ref[...], b_scale), 0, 0).astype(BF16)

    return pl.pallas_call(
        body, name=name, grid=grid,
        in_specs=[pl.BlockSpec(a_block, a_map), pl.BlockSpec(b_block, b_map)],
        out_specs=pl.BlockSpec(o_block, o_map),
        out_shape=jax.ShapeDtypeStruct(out_shape, BF16),
        compiler_params=_params(("parallel",) * len(grid)),
    )(a, b)


def mm_tn_plain(a, b, name, *, tm=512, tn=512, b_scale=None):
    T, M = a.shape
    N = b.shape[1]
    tm = _tile(M, tm)
    tn = _tile(N, tn)
    return mm_tn(a, b, name, grid=(M // tm, N // tn),
                 a_block=(T, tm), a_map=lambda m, n: (0, m),
                 b_block=(T, tn), b_map=lambda m, n: (0, n),
                 o_block=(tm, tn), o_map=lambda m, n: (m, n),
                 out_shape=(M, N), b_scale=b_scale)


def ffn_up(h, wgu, gi, ui, nper, name):
    T, D = h.shape
    F8 = wgu.shape[1]
    tm = _tile(T, 512)
    nt = T // tm

    def body(h_ref, wg_ref, wu_ref, a_ref, b_ref, s_ref):
        hv = h_ref[...]
        a = _dot(hv, wg_ref[...], 1, 0)
        b = _dot(hv, wu_ref[...], 1, 0)
        a_ref[...] = a.astype(BF16)
        b_ref[...] = b.astype(BF16)
        s_ref[...] = (a * _sigmoid(a) * b).astype(BF16)

    blk = pl.BlockSpec((tm, F8), lambda j, i: (j * nt + i, 0))
    shp = jax.ShapeDtypeStruct((N_DEV * T, F8), BF16)
    return pl.pallas_call(
        body, name=name, grid=(N_DEV, nt),
        in_specs=[pl.BlockSpec((tm, D), lambda j, i: (i, 0)),
                  pl.BlockSpec((D, F8), lambda j, i: (j * nper + gi, 0)),
                  pl.BlockSpec((D, F8), lambda j, i: (j * nper + ui, 0))],
        out_specs=[blk, blk, blk], out_shape=[shp, shp, shp],
        compiler_params=_params(("parallel", "parallel")),
    )(h, wgu, wgu)


def ffn_down(s, wd, di, nper, x, name):
    T, D = x.shape
    F8 = s.shape[1]
    tm = _tile(T, 512)
    nt = T // tm

    def body(s_ref, w_ref, x_ref, o_ref, acc_ref):
        j = pl.program_id(1)
        part = _dot(s_ref[...], w_ref[...], 1, 0)

        @pl.when(j == 0)
        def _():
            acc_ref[...] = part

        @pl.when(j > 0)
        def _():
            acc_ref[...] += part

        @pl.when(j == N_DEV - 1)
        def _():
            o_ref[...] = x_ref[...] + 0.5 * acc_ref[...]

    return pl.pallas_call(
        body, name=name, grid=(nt, N_DEV),
        in_specs=[pl.BlockSpec((tm, F8), lambda i, j: (j * nt + i, 0)),
                  pl.BlockSpec((F8, D), lambda i, j: (j * nper + di, 0)),
                  pl.BlockSpec((tm, D), lambda i, j: (i, 0))],
        out_specs=pl.BlockSpec((tm, D), lambda i, j: (i, 0)),
        out_shape=jax.ShapeDtypeStruct((T, D), F32),
        scratch_shapes=[pltpu.VMEM((tm, D), F32)],
        compiler_params=_params(("parallel", "arbitrary")),
    )(s, wd, x)


def ffn_bwd_act(dx, wd, di, nper_d, a, b, name, dep=None):
    T, D = dx.shape
    F8 = a.shape[1]
    tm = _tile(T, 512)
    nt = T // tm

    def body(dx_ref, w_ref, a_ref, b_ref, *rest):
        da_ref, db_ref = rest[-2], rest[-1]
        ds = _dot(_bf(dx_ref[...], 0.5), w_ref[...], 1, 1)
        av = a_ref[...].astype(F32)
        bv = b_ref[...].astype(F32)
        sg = _sigmoid(av)
        da_ref[...] = (ds * bv * (sg * (1.0 + av * (1.0 - sg)))).astype(BF16)
        db_ref[...] = (ds * (av * sg)).astype(BF16)

    blk = pl.BlockSpec((tm, F8), lambda j, i: (j * nt + i, 0))
    shp = jax.ShapeDtypeStruct((N_DEV * T, F8), BF16)
    in_specs = [pl.BlockSpec((tm, D), lambda j, i: (i, 0)),
                pl.BlockSpec((F8, D), lambda j, i: (j * nper_d + di, 0)), blk, blk]
    args = [dx, wd, a, b]
    if dep is not None:
        in_specs.append(_dep_spec(2))
        args.append(dep)
    return pl.pallas_call(
        body, name=name, grid=(N_DEV, nt), in_specs=in_specs,
        out_specs=[blk, blk], out_shape=[shp, shp],
        compiler_params=_params(("parallel", "parallel")),
    )(*args)


def ffn_bwd_dh(da, db, wgu, gi, ui, nper, D, name, dep=None):
    F8 = da.shape[1]
    T = da.shape[0] // N_DEV
    tm = _tile(T, 512)
    nt = T // tm

    def body(da_ref, db_ref, wg_ref, wu_ref, *rest):
        o_ref, acc_ref = rest[-2], rest[-1]
        j = pl.program_id(1)
        part = _dot(da_ref[...], wg_ref[...], 1, 1) + _dot(db_ref[...], wu_ref[...], 1, 1)

        @pl.when(j == 0)
        def _():
            acc_ref[...] = part

        @pl.when(j > 0)
        def _():
            acc_ref[...] += part

        @pl.when(j == N_DEV - 1)
        def _():
            o_ref[...] = acc_ref[...]

    blk = pl.BlockSpec((tm, F8), lambda i, j: (j * nt + i, 0))
    in_specs = [blk, blk,
                pl.BlockSpec((D, F8), lambda i, j: (j * nper + gi, 0)),
                pl.BlockSpec((D, F8), lambda i, j: (j * nper + ui, 0))]
    args = [da, db, wgu, wgu]
    if dep is not None:
        in_specs.append(_dep_spec(2))
        args.append(dep)
    return pl.pallas_call(
        body, name=name, grid=(nt, N_DEV), in_specs=in_specs,
        out_specs=pl.BlockSpec((tm, D), lambda i, j: (i, 0)),
        out_shape=jax.ShapeDtypeStruct((T, D), F32),
        scratch_shapes=[pltpu.VMEM((tm, D), F32)],
        compiler_params=_params(("parallel", "arbitrary")),
    )(*args)


def ffn_bwd_dw_in(h, dact, name):
    T, D = h.shape
    F8 = dact.shape[1]
    tm = _tile(D, 512)
    nm = D // tm
    return mm_tn(h, dact, name, grid=(N_DEV, nm),
                 a_block=(T, tm), a_map=lambda j, m: (0, m),
                 b_block=(T, F8), b_map=lambda j, m: (j, 0),
                 o_block=(tm, F8), o_map=lambda j, m: (j * nm + m, 0),
                 out_shape=(N_DEV * D, F8))


def ffn_bwd_dw_down(s, dx, name):
    T, D = dx.shape
    F8 = s.shape[1]
    tn = _tile(D, 512)
    return mm_tn(s, dx, name, grid=(N_DEV, D // tn),
                 a_block=(T, F8), a_map=lambda j, n: (j, 0),
                 b_block=(T, tn), b_map=lambda j, n: (0, n),
                 o_block=(F8, tn), o_map=lambda j, n: (j, n),
                 out_shape=(N_DEV * F8, D), b_scale=0.5)


def _t5_bucket_np(dist):
    max_exact = N_REL_BUCKETS // 2
    d = np.maximum(dist, 1).astype(np.float64)
    large = max_exact + (np.log(d / max_exact) / math.log(REL_MAX_DISTANCE / max_exact)
                         * (N_REL_BUCKETS - max_exact)).astype(np.int64)
    large32 = max_exact + (np.log(d.astype(np.float32) / np.float32(max_exact))
                           / np.float32(math.log(REL_MAX_DISTANCE / max_exact))
                           * np.float32(N_REL_BUCKETS - max_exact)).astype(np.int64)
    assert np.array_equal(large, large32)
    large = np.minimum(large, N_REL_BUCKETS - 1)
    return np.where(dist < max_exact, dist, large)


def _distance_tables(T, tq):
    dist = np.arange(T)
    mult = np.zeros(T, np.int64)
    for window, dilation in DILATED_PATTERNS:
        mult += ((dist % dilation == 0) & (dist // dilation <= window // dilation)).astype(np.int64)
    logm = np.where(mult > 0, np.log(np.maximum(mult, 1)), NEG_INF).astype(np.float32)
    bucket = _t5_bucket_np(dist).astype(np.int32)
    nkb = T // tq
    k = np.arange(nkb)[:, None, None]
    r = np.arange(tq)[None, :, None]
    c = np.arange(tq)[None, None, :]
    delta = k * tq + r - c
    return bucket, logm, delta


def _tile_buckets(T, tq):
    bucket, logm, delta = _distance_tables(T, tq)
    safe = np.maximum(delta, 0)
    bidx = np.where(delta >= 0, bucket[safe], -1).astype(np.int32)
    logm_t = np.where(delta >= 0, logm[safe], NEG_INF).astype(np.float32)
    present = [sorted(set(np.unique(bidx[k]).tolist()) - {-1}) for k in range(T // tq)]
    return bidx, logm_t, present


def bias_tiles(rel_table, T, tq):
    bidx, logm_t, present = _tile_buckets(T, tq)
    nkb = T // tq

    def body(tab_ref, b_ref, lm_ref, o_ref):
        slot = pl.program_id(0)

        @pl.when(slot == 0)
        def _():
            o_ref[...] = jnp.where(b_ref[...] >= 0, 0.0, NEG_INF)

        @pl.when(slot > 0)
        def _():
            for k in range(nkb):
                bi = b_ref[k]
                acc = lm_ref[k]
                for b in present[k]:
                    acc = acc + jnp.where(bi == b, tab_ref[b, slot - 1], 0.0)
                o_ref[k] = acc

    full = pl.BlockSpec((nkb, tq, tq), lambda s: (0, 0, 0))
    return pl.pallas_call(
        body, name="bias_tiles", grid=(1 + N_DIL,),
        in_specs=[pl.BlockSpec(memory_space=pltpu.SMEM), full, full],
        out_specs=pl.BlockSpec((None, nkb, tq, tq), lambda s: (s, 0, 0, 0)),
        out_shape=jax.ShapeDtypeStruct((1 + N_DIL, nkb, tq, tq), F32),
        compiler_params=_params(("parallel",)),
    )(rel_table, jnp.asarray(bidx), jnp.asarray(logm_t))


def fox_gate_fwd(uf, bf, name):
    T = uf.shape[0]
    tb = _tile(T, 512)

    def body(u_ref, b_ref, c_ref, ct_ref):
        lane = lax.broadcasted_iota(jnp.int32, (1, LANES), 1)
        tri = (lax.broadcasted_iota(jnp.int32, (tb, tb), 0)
               >= lax.broadcasted_iota(jnp.int32, (tb, tb), 1)).astype(F32)
        carry = jnp.zeros((1, LANES), F32)
        for blk in range(T // tb):
            z = u_ref[pl.ds(blk * tb, tb), :] + b_ref[...]
            lf = jnp.minimum(z, 0.0) - jnp.log1p(jnp.exp(-jnp.abs(z)))
            lf = jnp.where(lane < N_FOX, lf, 0.0)
            cb = _dot(tri, lf, 1, 0, precision=lax.Precision.HIGHEST) + carry
            c_ref[pl.ds(blk * tb, tb), :] = cb
            ct_ref[:, pl.ds(blk * tb, tb)] = cb.T
            carry = cb[tb - 1:tb, :]

    return pl.pallas_call(
        body, name=name,
        out_shape=[jax.ShapeDtypeStruct((T, LANES), F32), jax.ShapeDtypeStruct((LANES, T), F32)],
        compiler_params=_params(None),
    )(uf, bf)


def fox_gate_bwd(dct, uf, bf, name):
    T = uf.shape[0]
    tb = _tile(T, 512)

    def body(d_ref, u_ref, b_ref, du_ref, db_ref):
        lane = lax.broadcasted_iota(jnp.int32, (1, LANES), 1)
        triu = (lax.broadcasted_iota(jnp.int32, (tb, tb), 0)
                <= lax.broadcasted_iota(jnp.int32, (tb, tb), 1)).astype(F32)
        carry = jnp.zeros((1, LANES), F32)
        dbv = jnp.zeros((1, LANES), F32)
        for blk in reversed(range(T // tb)):
            dc = d_ref[:, pl.ds(blk * tb, tb)].T
            dlf = _dot(triu, dc, 1, 0, precision=lax.Precision.HIGHEST) + carry
            carry = dlf[0:1, :]
            z = u_ref[pl.ds(blk * tb, tb), :] + b_ref[...]
            dz = jnp.where(lane < N_FOX, dlf * (1.0 - _sigmoid(z)), 0.0)
            du_ref[pl.ds(blk * tb, tb), :] = dz
            dbv = dbv + jnp.sum(dz, axis=0, keepdims=True)
        db_ref[...] = dbv

    return pl.pallas_call(
        body, name=name,
        out_shape=[jax.ShapeDtypeStruct((T, LANES), F32), jax.ShapeDtypeStruct((1, LANES), F32)],
        compiler_params=_params(None),
    )(dct, uf, bf)


def _bias_slot(h):
    return jnp.maximum(h - (N_FOX - 1), 0)


def _scores(q_ref, k_ref, c_ref, ct_ref, tb_ref, h, i, T, tq):
    scale = HEAD_DIM ** -0.5
    nkb = T // tq
    s = _dot(q_ref[...], k_ref[...], 1, 1) * scale
    lane = lax.broadcasted_iota(jnp.int32, (1, LANES), 1)
    c_col = jnp.sum(jnp.where(lane == h, c_ref[...], 0.0), axis=1, keepdims=True)
    c_row = ct_ref[pl.ds(h, 1), :]
    chunks = []
    for jb in range(nkb):
        kk = i - jb
        tile = tb_ref[jnp.maximum(kk, 0)]
        chunks.append(jnp.where(kk >= 0, tile, NEG_INF))
    bias = jnp.concatenate(chunks, axis=1)
    return s + (c_col - c_row) + bias


def _attn_specs(T, tq):
    nkb = T // tq
    return [
        pl.BlockSpec((tq, HEAD_DIM), lambda h, i: (i, h)),
        pl.BlockSpec((T, HEAD_DIM), lambda h, i: (0, N_HEADS + h)),
        pl.BlockSpec((T, HEAD_DIM), lambda h, i: (0, 2 * N_HEADS + h)),
        pl.BlockSpec((tq, LANES), lambda h, i: (i, 0)),
        pl.BlockSpec((LANES, T), lambda h, i: (0, 0)),
        pl.BlockSpec((None, nkb, tq, tq), lambda h, i: (_bias_slot(h), 0, 0, 0)),
    ]


def attention_fwd(qkv, c, ct, tiles, name):
    T = qkv.shape[0]
    tq = tiles.shape[2]

    def body(q_ref, k_ref, v_ref, c_ref, ct_ref, tb_ref, o_ref):
        h, i = pl.program_id(0), pl.program_id(1)
        s = _scores(q_ref, k_ref, c_ref, ct_ref, tb_ref, h, i, T, tq)
        p = jnp.exp(s - jnp.max(s, axis=1, keepdims=True))
        l = jnp.sum(p, axis=1, keepdims=True)
        o_ref[...] = _dot((p / l).astype(BF16), v_ref[...], 1, 0).astype(BF16)

    return pl.pallas_call(
        body, name=name, grid=(N_HEADS, T // tq),
        in_specs=[qkv_spec for qkv_spec in _attn_specs(T, tq)],
        out_specs=pl.BlockSpec((tq, HEAD_DIM), lambda h, i: (i, h)),
        out_shape=jax.ShapeDtypeStruct((T, N_HEADS * HEAD_DIM), BF16),
        compiler_params=_params(("parallel", "parallel")),
    )(qkv, qkv, qkv, c, ct, tiles)


def attention_bwd(qkv, c, ct, tiles, do, name):
    T = qkv.shape[0]
    tq = tiles.shape[2]
    nkb = T // tq
    nq = T // tq
    scale = HEAD_DIM ** -0.5

    def body(q_ref, k_ref, v_ref, c_ref, ct_ref, tb_ref, do_ref,
             dq_ref, dk_ref, dv_ref, dct_ref, dtb_ref, dk_acc, dv_acc):
        h, i = pl.program_id(0), pl.program_id(1)
        s = _scores(q_ref, k_ref, c_ref, ct_ref, tb_ref, h, i, T, tq)
        p = jnp.exp(s - jnp.max(s, axis=1, keepdims=True))
        p = p / jnp.sum(p, axis=1, keepdims=True)
        dov = do_ref[...]
        dp = _dot(dov, v_ref[...], 1, 1)
        ds = p * (dp - jnp.sum(p * dp, axis=1, keepdims=True))
        ds_b = ds.astype(BF16)
        dq_ref[...] = (_dot(ds_b, k_ref[...], 1, 0) * scale).astype(BF16)
        dk_part = _dot(ds_b, q_ref[...], 0, 0) * scale
        dv_part = _dot(p.astype(BF16), dov, 0, 0)
        dc_part = -jnp.sum(ds, axis=0, keepdims=True)

        @pl.when(i == 0)
        def _():
            dk_acc[...] = dk_part
            dv_acc[...] = dv_part
            dct_ref[...] = dc_part
            dtb_ref[...] = jnp.zeros_like(dtb_ref)

        @pl.when(i > 0)
        def _():
            dk_acc[...] += dk_part
            dv_acc[...] += dv_part
            dct_ref[...] += dc_part

        for jb in range(nkb):
            kk = i - jb

            @pl.when(kk >= 0)
            def _():
                dtb_ref[kk] += ds[:, jb * tq:(jb + 1) * tq]

        @pl.when(i == nq - 1)
        def _():
            dk_ref[...] = dk_acc[...].astype(BF16)
            dv_ref[...] = dv_acc[...].astype(BF16)

    head_cols = jax.ShapeDtypeStruct((T, N_HEADS * HEAD_DIM), BF16)
    return pl.pallas_call(
        body, name=name, grid=(N_HEADS, nq),
        in_specs=_attn_specs(T, tq) + [pl.BlockSpec((tq, HEAD_DIM), lambda h, i: (i, h))],
        out_specs=[pl.BlockSpec((tq, HEAD_DIM), lambda h, i: (i, h)),
                   pl.BlockSpec((T, HEAD_DIM), lambda h, i: (0, h)),
                   pl.BlockSpec((T, HEAD_DIM), lambda h, i: (0, h)),
                   pl.BlockSpec((None, 1, T), lambda h, i: (h, 0, 0)),
                   pl.BlockSpec((None, nkb, tq, tq), lambda h, i: (_bias_slot(h), 0, 0, 0))],
        out_shape=[head_cols, head_cols, head_cols,
                   jax.ShapeDtypeStruct((N_HEADS, 1, T), F32),
                   jax.ShapeDtypeStruct((1 + N_DIL, nkb, tq, tq), F32)],
        scratch_shapes=[pltpu.VMEM((T, HEAD_DIM), F32), pltpu.VMEM((T, HEAD_DIM), F32)],
        compiler_params=_params(("arbitrary", "arbitrary")),
    )(qkv, qkv, qkv, c, ct, tiles, do)


def rel_table_grad(dtiles, T, name):
    tq = dtiles.shape[2]
    nkb = T // tq
    bidx, _, present = _tile_buckets(T, tq)

    def body(d_ref, b_ref, o_ref):
        lane = lax.broadcasted_iota(jnp.int32, (1, LANES), 1)
        row = jnp.zeros((1, LANES), F32)
        for k in range(nkb):
            d = d_ref[k]
            bi = b_ref[k]
            for b in present[k]:
                v = jnp.sum(jnp.sum(jnp.where(bi == b, d, 0.0), axis=0, keepdims=True),
                            axis=1, keepdims=True)
                row = row + jnp.where(lane == b, v, 0.0)
        o_ref[...] = row

    return pl.pallas_call(
        body, name=name, grid=(N_DIL,),
        in_specs=[pl.BlockSpec((None, nkb, tq, tq), lambda h: (h + 1, 0, 0, 0)),
                  pl.BlockSpec((nkb, tq, tq), lambda h: (0, 0, 0))],
        out_specs=pl.BlockSpec((None, 1, LANES), lambda h: (h, 0, 0)),
        out_shape=jax.ShapeDtypeStruct((N_DIL, 1, LANES), F32),
        compiler_params=_params(("parallel",)),
    )(dtiles, jnp.asarray(bidx))


def ple_combine(x, z, pp, name):
    T, D = x.shape
    tm = _tile(T, 256)

    def body(x_ref, z_ref, p_ref, o_ref):
        o_ref[...] = x_ref[...] + _sigmoid(z_ref[...]) * p_ref[...]

    row = pl.BlockSpec((tm, D), lambda i: (i, 0))
    return pl.pallas_call(
        body, name=name, grid=(T // tm,), in_specs=[row, row, row], out_specs=row,
        out_shape=jax.ShapeDtypeStruct((T, D), F32), compiler_params=_params(("parallel",)),
    )(x, z, pp)


def ple_bwd_elem(dx, z, pp, name):
    T, D = dx.shape
    tm = _tile(T, 256)

    def body(dx_ref, z_ref, p_ref, dz_ref, dp_ref):
        gate = _sigmoid(z_ref[...])
        d = dx_ref[...]
        dz_ref[...] = (d * p_ref[...] * gate * (1.0 - gate)).astype(BF16)
        dp_ref[...] = (d * gate).astype(BF16)

    row = pl.BlockSpec((tm, D), lambda i: (i, 0))
    shp = jax.ShapeDtypeStruct((T, D), BF16)
    return pl.pallas_call(
        body, name=name, grid=(T // tm,), in_specs=[row, row, row], out_specs=[row, row],
        out_shape=[shp, shp], compiler_params=_params(("parallel",)),
    )(dx, z, pp)


def _peer_list():
    x, y, c = lax.axis_index("x"), lax.axis_index("y"), lax.axis_index("c")
    me = 4 * x + 2 * y + c
    peers = []
    for fx in (0, 1):
        for fy in (0, 1):
            for fc in (0, 1):
                if fx or fy or fc:
                    px = 1 - x if fx else x
                    py = 1 - y if fy else y
                    pc = 1 - c if fc else c
                    peers.append(((px, py, pc), 4 * px + 2 * py + pc))
    return me, peers


_HBM = pl.BlockSpec(memory_space=pltpu.HBM)
_SEM = pl.BlockSpec(memory_space=pltpu.SEMAPHORE)
_EFFECT = pltpu.SideEffectType.DATAFLOW_SIDE_EFFECTING
N_PEERS = N_DEV - 1


def _in_hbm(a):
    return pltpu.with_memory_space_constraint(a, pltpu.HBM)


def _exchange_copies(srcs, lands, send_sems, recv_sems, blockwise):
    me, peers = _peer_list()
    sends, recvs = [], []
    for a in range(len(srcs)):
        for k, (dev, idx) in enumerate(peers):
            src = srcs[a].at[idx] if blockwise[a] else srcs[a]
            sends.append(pltpu.make_async_remote_copy(
                src_ref=src, dst_ref=lands[a].at[me], send_sem=send_sems[a].at[k],
                recv_sem=recv_sems[a].at[k], device_id=dev, device_id_type=MESH))
            recvs.append(pltpu.make_async_remote_copy(
                src_ref=src, dst_ref=lands[a].at[idx], send_sem=send_sems[a].at[k],
                recv_sem=recv_sems[a].at[k], device_id=dev, device_id_type=MESH))
    return sends, recvs


def exchange_start(srcs, lands, blockwise, name):
    n = len(srcs)

    def body(*refs):
        src_in, land_in = refs[:n], refs[n:2 * n]
        send_sems, recv_sems = refs[2 * n:3 * n], refs[3 * n:4 * n]
        token = refs[6 * n]
        sends, _ = _exchange_copies(src_in, land_in, send_sems, recv_sems, blockwise)
        for cp in sends:
            cp.start()
        token[...] = jnp.zeros_like(token)

    out_shape = ([pltpu.SemaphoreType.DMA((N_PEERS,))] * (2 * n)
                 + [pltpu.HBM(s.shape, s.dtype) for s in srcs]
                 + [pltpu.HBM(l.shape, l.dtype) for l in lands]
                 + [jax.ShapeDtypeStruct((8, LANES), F32)])
    aliases = {a: 2 * n + a for a in range(2 * n)}
    outs = pl.pallas_call(
        body, name=name, out_shape=out_shape,
        in_specs=[_HBM] * (2 * n),
        out_specs=[_SEM] * (2 * n) + [_HBM] * (2 * n) + [pl.BlockSpec(memory_space=pltpu.VMEM)],
        input_output_aliases=aliases,
        compiler_params=pltpu.CompilerParams(has_side_effects=_EFFECT),
    )(*[_in_hbm(s) for s in srcs], *[_in_hbm(l) for l in lands])
    return (outs[:n], outs[n:2 * n], outs[2 * n:3 * n], outs[3 * n:4 * n], outs[4 * n])


def exchange_wait(send_sems, recv_sems, srcs, lands, blockwise, after, name):
    n = len(srcs)

    def body(*refs):
        src_in, land_in = refs[:n], refs[n:2 * n]
        ss, rs = refs[2 * n:3 * n], refs[3 * n:4 * n]
        sends, recvs = _exchange_copies(src_in, land_in, ss, rs, blockwise)
        for cp in sends:
            cp.wait_send()
        for cp in recvs:
            cp.wait_recv()

    outs = pl.pallas_call(
        body, name=name,
        out_shape=[pltpu.HBM(s.shape, s.dtype) for s in srcs] + [pltpu.HBM(l.shape, l.dtype) for l in lands],
        in_specs=[_HBM] * (2 * n) + [_SEM] * (2 * n) + [pl.BlockSpec(memory_space=pl.ANY)],
        out_specs=[_HBM] * (2 * n),
        input_output_aliases={a: a for a in range(2 * n)},
        compiler_params=pltpu.CompilerParams(has_side_effects=_EFFECT),
    )(*srcs, *lands, *send_sems, *recv_sems, after)
    return outs[n:]


def _landing(own_block, me, slots=N_DEV):
    empty = lax.empty((slots,) + own_block.shape, own_block.dtype)
    return lax.dynamic_update_slice(empty, own_block[None], (me,) + (0,) * own_block.ndim)


N_CHIPS = N_DEV // 2
_CHIP_FLIPS = ((1, 0), (0, 1), (1, 1))


def _xyc():
    return lax.axis_index("x"), lax.axis_index("y"), lax.axis_index("c")


def _other_chips(x, y):
    return [(1 - x if fx else x, 1 - y if fy else y) for fx, fy in _CHIP_FLIPS]


def _remote(src, dst, send_sem, recv_sem, dev):
    return pltpu.make_async_remote_copy(src_ref=src, dst_ref=dst, send_sem=send_sem, recv_sem=recv_sem,
                                        device_id=dev, device_id_type=MESH)


def comm_call(name, bufs, sems_in, sems_out, fn, after=None, want_token=False):
    nb, ni, no = len(bufs), len(sems_in), len(sems_out)
    na = 0 if after is None else 1

    def body(*refs):
        buf_refs = refs[:nb]
        sin = refs[nb:nb + ni]
        sout = refs[nb + ni + na:nb + ni + na + no]
        fn(buf_refs, sin, sout)
        if want_token:
            tok = refs[nb + ni + na + no + nb]
            tok[...] = jnp.zeros_like(tok)

    out_shape = list(sems_out) + [pltpu.HBM(b.shape, b.dtype) for b in bufs]
    out_specs = [_SEM] * no + [_HBM] * nb
    if want_token:
        out_shape.append(jax.ShapeDtypeStruct((8, LANES), F32))
        out_specs.append(pl.BlockSpec(memory_space=pltpu.VMEM))
    args = [_in_hbm(b) for b in bufs] + list(sems_in) + ([after] if na else [])
    outs = pl.pallas_call(
        body, name=name, out_shape=out_shape,
        in_specs=[_HBM] * nb + [_SEM] * ni + [pl.BlockSpec(memory_space=pl.ANY)] * na,
        out_specs=out_specs, input_output_aliases={a: no + a for a in range(nb)},
        compiler_params=pltpu.CompilerParams(has_side_effects=_EFFECT),
    )(*args)
    return list(outs[:no]), list(outs[no:no + nb]), (outs[no + nb] if want_token else None)


def _dma_sems(*sizes):
    return [pltpu.SemaphoreType.DMA((s,)) for s in sizes]


def gather_start(srcs, lands, name):
    n = len(srcs)

    def fn(bufs, sin, sout):
        x, y, c = _xyc()
        me = 4 * x + 2 * y + c
        for a in range(n):
            src, land = bufs[a], bufs[n + a]
            send, recv_d, recv_i = sout[3 * a:3 * a + 3]
            _remote(src, land.at[me], send.at[0], recv_d.at[0], (x, y, 1 - c)).start()
            for k, (px, py) in enumerate(_other_chips(x, y)):
                _remote(src, land.at[me], send.at[1 + k], recv_i.at[k], (px, py, c)).start()

    return comm_call(name, list(srcs) + list(lands), [], _dma_sems(4, 1, 3) * n, fn, want_token=True)


def gather_forward(srcs, lands, recv_i, after, name):
    n = len(srcs)

    def fn(bufs, sin, sout):
        x, y, c = _xyc()
        for a in range(n):
            src, land = bufs[a], bufs[n + a]
            f_send, f_recv = sout[2 * a:2 * a + 2]
            for k, (px, py) in enumerate(_other_chips(x, y)):
                blk = land.at[4 * px + 2 * py + c]
                _remote(src, blk, f_send.at[k], sin[a].at[k], (px, py, c)).wait_recv()
                _remote(blk, blk, f_send.at[k], f_recv.at[k], (x, y, 1 - c)).start()

    sems, bufs, _ = comm_call(name, list(srcs) + list(lands), recv_i, _dma_sems(3, 3) * n, fn, after=after)
    return sems, bufs


def gather_wait(srcs, lands, send, recv_d, f_send, f_recv, after, name):
    n = len(srcs)

    def fn(bufs, sin, sout):
        x, y, c = _xyc()
        sib = (x, y, 1 - c)
        for a in range(n):
            src, land = bufs[a], bufs[n + a]
            s_send, s_recv_d, s_fsend, s_frecv = sin[4 * a:4 * a + 4]
            sib_blk = land.at[4 * x + 2 * y + 1 - c]
            for k in range(4):
                _remote(src, sib_blk, s_send.at[k], s_recv_d.at[0], sib).wait_send()
            _remote(src, sib_blk, s_send.at[0], s_recv_d.at[0], sib).wait_recv()
            for k, (px, py) in enumerate(_other_chips(x, y)):
                cp = _remote(src, land.at[4 * px + 2 * py + 1 - c], s_fsend.at[k], s_frecv.at[k], sib)
                cp.wait_send()
                cp.wait_recv()

    sems_in = []
    for a in range(n):
        sems_in += [send[a], recv_d[a], f_send[a], f_recv[a]]
    _, bufs, _ = comm_call(name, list(srcs) + list(lands), sems_in, [], fn, after=after)
    return bufs[n:]


def scatter_pair_start(src4s, lands, name):
    n = len(src4s)

    def fn(bufs, sin, sout):
        x, y, c = _xyc()
        for a in range(n):
            _remote(bufs[a].at[:, 1 - c], bufs[n + a], sout[2 * a].at[0], sout[2 * a + 1].at[0],
                    (x, y, 1 - c)).start()

    return comm_call(name, list(src4s) + list(lands), [], _dma_sems(1, 1) * n, fn, want_token=True)


def scatter_pair_wait(src4s, lands, sems, after, name):
    n = len(src4s)

    def fn(bufs, sin, sout):
        x, y, c = _xyc()
        for a in range(n):
            cp = _remote(bufs[a].at[:, 1 - c], bufs[n + a], sin[2 * a].at[0], sin[2 * a + 1].at[0], (x, y, 1 - c))
            cp.wait_send()
            cp.wait_recv()

    _, bufs, _ = comm_call(name, list(src4s) + list(lands), sems, [], fn, after=after)
    return bufs[:n], bufs[n:]


def _row_tile(R):
    for cand in (128, 64, 32, 16):
        if R % cand == 0:
            return cand
    return R


def chip_sum(src4, land, c, name):
    _, _, R, C = src4.shape
    tr = _row_tile(R)

    def body(c_ref, a_ref, b_ref, o_ref):
        o_ref[...] = (a_ref[...].astype(F32) + b_ref[...].astype(F32)).astype(BF16)

    grid_spec = pltpu.PrefetchScalarGridSpec(
        num_scalar_prefetch=1, grid=(N_CHIPS, R // tr),
        in_specs=[pl.BlockSpec((None, None, tr, C), lambda q, i, cr: (q, cr[0], i, 0)),
                  pl.BlockSpec((None, tr, C), lambda q, i, cr: (q, i, 0))],
        out_specs=pl.BlockSpec((None, tr, C), lambda q, i, cr: (q, i, 0)))
    return pl.pallas_call(
        body, name=name, grid_spec=grid_spec,
        out_shape=jax.ShapeDtypeStruct((N_CHIPS, R, C), BF16),
        compiler_params=_params(("parallel", "parallel")),
    )(c.reshape(1).astype(jnp.int32), src4, land)


def scatter_chip_start(sums, lands, name):
    n = len(sums)

    def fn(bufs, sin, sout):
        x, y, c = _xyc()
        for a in range(n):
            for k, (px, py) in enumerate(_other_chips(x, y)):
                _remote(bufs[a].at[2 * px + py], bufs[n + a].at[2 * x + y], sout[2 * a].at[k], sout[2 * a + 1].at[k],
                        (px, py, c)).start()

    return comm_call(name, list(sums) + list(lands), [], _dma_sems(3, 3) * n, fn, want_token=True)


def scatter_chip_wait(sums, lands, sems, after, name):
    n = len(sums)

    def fn(bufs, sin, sout):
        x, y, c = _xyc()
        for a in range(n):
            for k, (px, py) in enumerate(_other_chips(x, y)):
                cp = _remote(bufs[a].at[2 * px + py], bufs[n + a].at[2 * px + py], sin[2 * a].at[k],
                             sin[2 * a + 1].at[k], (px, py, c))
                cp.wait_send()
                cp.wait_recv()

    _, bufs, _ = comm_call(name, list(sums) + list(lands), sems, [], fn, after=after)
    return bufs[:n], bufs[n:]


def _adamw_math(w, g, m, v):
    m = ADAM_B1 * m + (1.0 - ADAM_B1) * g
    v = ADAM_B2 * v + (1.0 - ADAM_B2) * (g * g)
    m_hat = m / (1.0 - ADAM_B1 ** ADAM_STEP)
    v_hat = v / (1.0 - ADAM_B2 ** ADAM_STEP)
    delta = -ADAM_LR * (m_hat / (jnp.sqrt(v_hat) + ADAM_EPS) + ADAM_WD * w)
    return delta, m, v


def adamw_sharded(parts, w, m, v, name):
    R, C = w.shape
    S = parts.shape[0]
    tr = _row_tile(R)

    def body(p_ref, w_ref, m_ref, v_ref, g_ref, d_ref, nm_ref, nv_ref):
        g = p_ref[0].astype(F32)
        for s in range(1, S):
            g = g + p_ref[s].astype(F32)
        delta, nm, nv = _adamw_math(w_ref[...], g, m_ref[...], v_ref[...])
        g_ref[...] = g
        d_ref[...] = delta
        nm_ref[...] = nm
        nv_ref[...] = nv

    row = pl.BlockSpec((tr, C), lambda i: (i, 0))
    shp = jax.ShapeDtypeStruct((R, C), F32)
    return pl.pallas_call(
        body, name=name, grid=(R // tr,),
        in_specs=[pl.BlockSpec((S, tr, C), lambda i: (0, i, 0)), row, row, row],
        out_specs=[row, row, row, row], out_shape=[shp, shp, shp, shp],
        compiler_params=_params(("parallel",)),
    )(parts, w, m, v)


def adamw_small(parts, w, m, v, name):
    R, C = w.shape

    def body(p_ref, w_ref, m_ref, v_ref, g_ref, d_ref, nm_ref, nv_ref):
        g = p_ref[0]
        for s in range(1, N_DEV):
            g = g + p_ref[s]
        delta, nm, nv = _adamw_math(w_ref[...], g, m_ref[...], v_ref[...])
        g_ref[...] = g
        d_ref[...] = delta
        nm_ref[...] = nm
        nv_ref[...] = nv

    shp = jax.ShapeDtypeStruct((R, C), F32)
    return pl.pallas_call(
        body, name=name, out_shape=[shp, shp, shp, shp], compiler_params=_params(None),
    )(parts, w, m, v)


_ROW_NORM_FFN1, _ROW_NORM_MIX, _ROW_NORM_FFN2, _ROW_NORM_PLE, _ROW_NORM_FINAL = 0, 1, 2, 3, 4
_ROW_B_F, _ROW_REL, _ROW_LOSS, _SMALL_ROWS = 5, 6, 7, 8


def _pack_small(D, norm_ffn1, norm_mix, norm_ffn2, norm_ple, norm_final, b_f, rel_table):
    def row(v):
        v = v.reshape(1, -1)
        return jnp.pad(v, ((0, 0), (0, D - v.shape[1])))
    return jnp.concatenate([row(norm_ffn1), row(norm_mix), row(norm_ffn2), row(norm_ple),
                            row(norm_final), row(b_f), row(rel_table),
                            jnp.zeros((1, D), F32)], axis=0)


def _unpack_small(a, shapes):
    return {"norm_ffn1": a[_ROW_NORM_FFN1].reshape(shapes["norm_ffn1"]),
            "norm_mix": a[_ROW_NORM_MIX].reshape(shapes["norm_mix"]),
            "b_f": a[_ROW_B_F, :N_FOX].reshape(shapes["b_f"]),
            "norm_ffn2": a[_ROW_NORM_FFN2].reshape(shapes["norm_ffn2"]),
            "norm_ple": a[_ROW_NORM_PLE].reshape(shapes["norm_ple"]),
            "rel_table": a[_ROW_REL, :N_REL_BUCKETS * N_DIL].reshape(shapes["rel_table"]),
            "norm_final": a[_ROW_NORM_FINAL].reshape(shapes["norm_final"])}


def local_step(x, p, tgt, g_ffn1, g_mix, g_ffn2, g_ple, g_final, b_f, rel_table,
               forward, weights, emit, emit2, first_dep):
    T, D = x.shape
    P = p.shape[1]
    CW = D // N_DEV
    tq = _tile(T, 256)

    h1 = rms_fwd(x, g_ffn1, "rms_ffn1", dep=first_dep)
    forward("ffn1", h1)
    wgu1, wd1 = weights("ffn1", h1)
    a1, b1, s1 = ffn_up(h1, wgu1, 0, 1, 2, "ffn1_up")
    forward("mix", s1)
    x1 = ffn_down(s1, wd1, 0, 1, x, "ffn1_down")

    h2 = rms_fwd(x1, g_mix, "rms_mix")
    w3, wf, wo = weights("mix", h2)
    qkv = mm_nn(h2, w3, "mix_qkv", tn=768, out_dtype=BF16)
    forward("ffn2", qkv)
    uf = mm_nn(h2, wf, "mix_forget", tn=LANES, out_dtype=F32)
    bfp = jnp.pad(b_f.reshape(1, N_FOX), ((0, 0), (0, LANES - N_FOX)))
    c, ct = fox_gate_fwd(uf, bfp, "fox_gate")
    tiles = bias_tiles(rel_table, T, tq)
    cat = attention_fwd(qkv, c, ct, tiles, "attention")
    x2 = mm_nn(cat, wo, "mix_out", tn=512, out_dtype=F32, res=x1)

    h3 = rms_fwd(x2, g_ffn2, "rms_ffn2")
    wgu2, wd2 = weights("ffn2", h3)
    a2, b2, s2 = ffn_up(h3, wgu2, 0, 1, 2, "ffn2_up")
    forward("ple", s2)
    x3 = ffn_down(s2, wd2, 0, 1, x2, "ffn2_down")

    h4 = rms_fwd(x3, g_ple, "rms_ple")
    wpg, wpp = weights("ple", h4)
    z = mm_nn(h4, wpg, "ple_gate", tn=512, out_dtype=F32)
    pp = mm_nn(p, wpp, "ple_proj", tn=CW, out_dtype=F32, n_out=D,
               b_block=(P, CW), b_map=lambda n, i: (n, 0))
    x4 = ple_combine(x3, z, pp, "ple_combine")
    loss_row, dx4, dg_final = final_loss_bwd(x4, g_final, tgt, "final_loss")

    grads = {}
    dz, dpp = ple_bwd_elem(dx4, z, pp, "ple_bwd_elem")
    grads["w_ple_proj"] = mm_tn(p, dpp, "ple_proj_dw", grid=(N_DEV,),
                                a_block=(T, P), a_map=lambda n: (0, 0),
                                b_block=(T, CW), b_map=lambda n: (0, n),
                                o_block=(P, CW), o_map=lambda n: (n, 0),
                                out_shape=(N_DEV * P, CW))
    grads["w_ple_gate"] = mm_tn_plain(h4, dz, "ple_gate_dw")
    tok = emit("ple", grads)
    dh4 = mm_nt([(dz, wpg)], "ple_gate_dh", tn=512, out_dtype=F32, dep=tok)
    tok = emit2("ple", dh4)
    dx3, dg_ple = rms_bwd(dh4, x3, g_ple, dx4, "rms_ple_bwd", dep=tok)

    da2, db2 = ffn_bwd_act(dx3, wd2, 0, 1, a2, b2, "ffn2_bwd_act")
    grads["ffn2_w_down"] = ffn_bwd_dw_down(s2, dx3, "ffn2_down_dw")
    grads["ffn2_w_gate"] = ffn_bwd_dw_in(h3, da2, "ffn2_gate_dw")
    grads["ffn2_w_up"] = ffn_bwd_dw_in(h3, db2, "ffn2_up_dw")
    tok = emit("ffn2", grads)
    dh3 = ffn_bwd_dh(da2, db2, wgu2, 0, 1, 2, D, "ffn2_bwd_dh", dep=tok)
    tok = emit2("ffn2", dh3)
    dx2, dg_ffn2 = rms_bwd(dh3, x2, g_ffn2, dx3, "rms_ffn2_bwd", dep=tok)

    dcat = mm_nt([(dx2, wo)], "mix_out_dh", tn=512, out_dtype=BF16)
    grads["w_o"] = mm_tn_plain(cat, dx2, "mix_out_dw")
    dq, dk, dv, dct, dtiles = attention_bwd(qkv, c, ct, tiles, dcat, "attention_bwd")
    dctp = jnp.pad(dct[:, 0, :], ((0, LANES - N_HEADS), (0, 0)))
    duf, dbf = fox_gate_bwd(dctp, uf, bfp, "fox_gate_bwd")
    drel = rel_table_grad(dtiles, T, "rel_table_grad")[:, 0, :N_REL_BUCKETS].T
    du3 = jnp.concatenate([dq, dk, dv], axis=1)
    grads["w3"] = mm_tn_plain(h2, du3, "mix_qkv_dw", tn=768)
    grads["wf"] = mm_tn_plain(h2, duf, "mix_forget_dw", tn=LANES)
    tok = emit("mix", grads)
    dh2 = mm_nt([(du3, w3), (duf, wf)], "mix_in_dh", tn=512, out_dtype=F32, dep=tok)
    tok = emit2("mix", dh2)
    dx1, dg_mix = rms_bwd(dh2, x1, g_mix, dx2, "rms_mix_bwd", dep=tok)

    da1, db1 = ffn_bwd_act(dx1, wd1, 0, 1, a1, b1, "ffn1_bwd_act")
    grads["ffn1_w_down"] = ffn_bwd_dw_down(s1, dx1, "ffn1_down_dw")
    grads["ffn1_w_gate"] = ffn_bwd_dw_in(h1, da1, "ffn1_gate_dw")
    grads["ffn1_w_up"] = ffn_bwd_dw_in(h1, db1, "ffn1_up_dw")
    tok = emit("ffn1", grads)
    dh1 = ffn_bwd_dh(da1, db1, wgu1, 0, 1, 2, D, "ffn1_bwd_dh", dep=tok)
    tok = emit2("ffn1", dh1)
    dx0, dg_ffn1 = rms_bwd(dh1, x, g_ffn1, dx1, "rms_ffn1_bwd", dep=tok)

    small = _pack_small(D, dg_ffn1, dg_mix, dg_ffn2, dg_ple, dg_final, dbf[:, :N_FOX], drel)
    small = small.at[_ROW_LOSS, :LANES].set(loss_row[0])
    grads["small"] = small
    emit("small", grads)
    return dx0


def _split_w_in(w_in_full):
    df, dd = N_FOX * HEAD_DIM, N_DIL * HEAD_DIM
    o = np.cumsum([0, df, df, df, N_FOX, dd, dd, dd]).tolist()
    qa, ka, va, f, qb, kb, vb = [w_in_full[:, o[i]:o[i + 1]] for i in range(7)]
    return jnp.concatenate([qa, qb, ka, kb, va, vb], axis=1), f


def _join_w_in(d3, dfg):
    df, dd = N_FOX * HEAD_DIM, N_DIL * HEAD_DIM
    o = np.cumsum([0, df, dd, df, dd, df, dd]).tolist()
    qa, qb, ka, kb, va, vb = [d3[:, o[i]:o[i + 1]] for i in range(6)]
    return jnp.concatenate([qa, ka, va, dfg, qb, kb, vb], axis=1)


def kernel(x, p, norm_ffn1, ffn1_w_gate, ffn1_w_up, ffn1_w_down, norm_mix, w_in, b_f, w_o, norm_ffn2, ffn2_w_gate, ffn2_w_up, ffn2_w_down, norm_ple, w_ple_gate, w_ple_proj, rel_table, norm_final, loss_target, m_norm_ffn1, m_ffn1_w_gate, m_ffn1_w_up, m_ffn1_w_down, m_norm_mix, m_w_in, m_b_f, m_w_o, m_norm_ffn2, m_ffn2_w_gate, m_ffn2_w_up, m_ffn2_w_down, m_norm_ple, m_w_ple_gate, m_w_ple_proj, m_rel_table, m_norm_final, v_norm_ffn1, v_ffn1_w_gate, v_ffn1_w_up, v_ffn1_w_down, v_norm_mix, v_w_in, v_b_f, v_w_o, v_norm_ffn2, v_ffn2_w_gate, v_ffn2_w_up, v_ffn2_w_down, v_norm_ple, v_w_ple_gate, v_w_ple_proj, v_rel_table, v_norm_final):
    names = ["norm_ffn1", "ffn1_w_gate", "ffn1_w_up", "ffn1_w_down", "norm_mix", "w_in", "b_f", "w_o",
             "norm_ffn2", "ffn2_w_gate", "ffn2_w_up", "ffn2_w_down", "norm_ple", "w_ple_gate",
             "w_ple_proj", "rel_table", "norm_final"]
    w = dict(zip(names, [norm_ffn1, ffn1_w_gate, ffn1_w_up, ffn1_w_down, norm_mix, w_in, b_f, w_o,
                         norm_ffn2, ffn2_w_gate, ffn2_w_up, ffn2_w_down, norm_ple, w_ple_gate,
                         w_ple_proj, rel_table, norm_final]))
    m = dict(zip(names, [m_norm_ffn1, m_ffn1_w_gate, m_ffn1_w_up, m_ffn1_w_down, m_norm_mix, m_w_in,
                         m_b_f, m_w_o, m_norm_ffn2, m_ffn2_w_gate, m_ffn2_w_up, m_ffn2_w_down,
                         m_norm_ple, m_w_ple_gate, m_w_ple_proj, m_rel_table, m_norm_final]))
    v = dict(zip(names, [v_norm_ffn1, v_ffn1_w_gate, v_ffn1_w_up, v_ffn1_w_down, v_norm_mix, v_w_in,
                         v_b_f, v_w_o, v_norm_ffn2, v_ffn2_w_gate, v_ffn2_w_up, v_ffn2_w_down,
                         v_norm_ple, v_w_ple_gate, v_w_ple_proj, v_rel_table, v_norm_final]))
    sharded = ["ffn1_w_gate", "ffn1_w_up", "ffn1_w_down", "w_in", "w_o", "ffn2_w_gate", "ffn2_w_up",
               "ffn2_w_down", "w_ple_gate", "w_ple_proj"]
    small_names = [n for n in names if n not in sharded]

    xs, ps, tgt = x[0], p[0, 0], loss_target[0]
    T, D = xs.shape
    sh = {n: w[n][0] for n in sharded}
    F8 = sh["ffn1_w_gate"].shape[1]
    WIN8 = sh["w_in"].shape[1]

    me = 4 * lax.axis_index("x") + 2 * lax.axis_index("y") + lax.axis_index("c")

    cat0 = lambda ns: jnp.concatenate([sh[n] for n in ns], axis=0).astype(BF16)
    gather_groups = {
        "ffn1": [cat0(["ffn1_w_gate", "ffn1_w_up"]), sh["ffn1_w_down"].astype(BF16)],
        "mix": [sh["w_in"].astype(BF16), sh["w_o"].astype(BF16)],
        "ffn2": [cat0(["ffn2_w_gate", "ffn2_w_up"]), sh["ffn2_w_down"].astype(BF16)],
        "ple": [sh["w_ple_gate"].astype(BF16), sh["w_ple_proj"].astype(BF16)],
    }
    order = ["ffn1", "mix", "ffn2", "ple"]
    g_srcs = [s for grp in order for s in gather_groups[grp]]
    g_lands = [_landing(s, me) for s in g_srcs]
    n_g = len(g_srcs)
    g_sems, g_bufs, g_token = gather_start(g_srcs, g_lands, "gather_start")
    g_srcs, g_lands = g_bufs[:n_g], g_bufs[n_g:]
    g_send, g_recv_d, g_recv_i = g_sems[0::3], g_sems[1::3], g_sems[2::3]
    passed = {}

    def forward(group, after):
        k = 2 * order.index(group)
        sl = slice(k, k + 2)
        f_sems, bufs = gather_forward(g_srcs[sl], g_lands[sl], g_recv_i[sl], after, "gather_forward_" + group)
        passed[group] = (f_sems[0::2], f_sems[1::2], bufs[:2], bufs[2:])

    def weights(group, after):
        k = 2 * order.index(group)
        sl = slice(k, k + 2)
        f_send, f_recv, srcs, lands = passed[group]
        a0, a1 = gather_wait(srcs, lands, g_send[sl], g_recv_d[sl], f_send, f_recv, after,
                             "gather_wait_" + group)
        if group in ("ffn1", "ffn2"):
            return a0.reshape(N_DEV * 2 * D, F8), a1.reshape(N_DEV * F8, D)
        if group == "ple":
            return a0.reshape(-1, D), a1.reshape(-1, a1.shape[2])
        w_in_full = jnp.transpose(a0, (1, 0, 2)).reshape(D, N_DEV * WIN8)
        w3, wf8 = _split_w_in(w_in_full)
        return w3, jnp.pad(wf8, ((0, 0), (0, LANES - N_FOX))), a1.reshape(-1, D)

    scatter_groups = {
        "ple": ["w_ple_gate", "w_ple_proj"],
        "ffn2": ["ffn2_w_gate", "ffn2_w_up", "ffn2_w_down"],
        "mix": ["w_in", "w_o"],
        "ffn1": ["ffn1_w_gate", "ffn1_w_up", "ffn1_w_down"],
        "small": ["small"],
    }
    x_i, y_i, c_i = _xyc()
    my_chip = 2 * x_i + y_i
    pair_stage, chip_stage, small_stage = {}, {}, {}

    def emit(group, grads):
        if group == "small":
            src = grads["small"]
            ss, rs, srcs, lands, token = exchange_start([src], [_landing(src, me)], [False], "scatter_start_small")
            small_stage["small"] = (ss, rs, srcs, lands)
            return token
        src4s = []
        for n in scatter_groups[group]:
            if n == "w_in":
                d_w_in = _join_w_in(grads["w3"], grads["wf"][:, :N_FOX])
                full = jnp.transpose(d_w_in.reshape(D, N_DEV, WIN8), (1, 0, 2))
            else:
                full = grads[n].reshape((N_DEV,) + sh[n].shape)
            src4s.append(full.reshape((N_CHIPS, 2) + sh[n].shape))
        lands = [lax.empty((N_CHIPS,) + s.shape[2:], BF16) for s in src4s]
        sems, bufs, token = scatter_pair_start(src4s, lands, "scatter_pair_start_" + group)
        k = len(src4s)
        pair_stage[group] = (sems, bufs[:k], bufs[k:])
        return token

    def emit2(group, after):
        sems, src4s, lands = pair_stage[group]
        src4s, lands = scatter_pair_wait(src4s, lands, sems, after, "scatter_pair_wait_" + group)
        sums = [chip_sum(s4, la, c_i, "chip_sum_" + n)
                for s4, la, n in zip(src4s, lands, scatter_groups[group])]
        chip_lands = [_landing(lax.dynamic_index_in_dim(s, my_chip, 0, keepdims=False), my_chip, slots=N_CHIPS)
                      for s in sums]
        sems, bufs, token = scatter_chip_start(sums, chip_lands, "scatter_chip_start_" + group)
        k = len(sums)
        chip_stage[group] = (sems, bufs[:k], bufs[k:])
        return token

    dx0 = local_step(
        xs, ps, tgt, w["norm_ffn1"], w["norm_mix"], w["norm_ffn2"], w["norm_ple"],
        w["norm_final"].reshape(1, D), w["b_f"], w["rel_table"], forward, weights, emit, emit2, g_token)

    res = {}
    after = dx0
    for group in ["ple", "ffn2", "mix", "ffn1"]:
        sems, sums, chip_lands = chip_stage[group]
        _, parts = scatter_chip_wait(sums, chip_lands, sems, after, "scatter_chip_wait_" + group)
        for n, part in zip(scatter_groups[group], parts):
            g, d, nm, nv = adamw_sharded(part, sh[n], m[n][0], v[n][0], "adamw_" + n)
            res[n] = tuple(a.reshape(w[n].shape) for a in (g, d, nm, nv))
            after = g
    ss, rs, srcs, lands = small_stage["small"]
    small_parts, = exchange_wait(ss, rs, srcs, lands, [False], after, "scatter_wait_small")
    pack = lambda t: _pack_small(D, t["norm_ffn1"], t["norm_mix"], t["norm_ffn2"], t["norm_ple"],
                                 t["norm_final"], t["b_f"], t["rel_table"])
    gs, ds, ms, vs = adamw_small(small_parts, pack(w), pack(m), pack(v), "adamw_small")
    shapes = {n: w[n].shape for n in small_names}
    unpacked = [_unpack_small(a, shapes) for a in (gs, ds, ms, vs)]
    for n in small_names:
        res[n] = tuple(u[n] for u in unpacked)
    loss = gs[_ROW_LOSS, 0]

    out = [loss, dx0.reshape(x.shape)]
    for k in range(4):
        out += [res[n][k] for n in names]
    return tuple(out)
```

```python
import functools
import math

import numpy as np
import jax
import jax.numpy as jnp
from jax import lax
from jax.experimental import pallas as pl
from jax.experimental.pallas import tpu as pltpu

F32 = jnp.float32
BF16 = jnp.bfloat16

N_DEV = 8
HEAD_DIM = 128
N_FOX = 8
N_DIL = 8
N_HEADS = N_FOX + N_DIL
DILATED_PATTERNS = ((128, 1), (512, 4), (2048, 16))
N_REL_BUCKETS = 32
REL_MAX_DISTANCE = 2048
RMS_EPS = 1e-6
NEG_INF = -1e30
LANES = 128
VMEM_LIMIT = 56 * 1024 * 1024

ADAM_LR = 0.001
ADAM_B1 = 0.9
ADAM_B2 = 0.999
ADAM_EPS = 1e-08
ADAM_WD = 0.01
ADAM_STEP = 10

MESH = pl.DeviceIdType.MESH


def _params(sem):
    return pltpu.CompilerParams(dimension_semantics=sem, vmem_limit_bytes=VMEM_LIMIT)


def _dot(a, b, ca, cb, precision=None):
    return lax.dot_general(a, b, (((ca,), (cb,)), ((), ())),
                           preferred_element_type=F32, precision=precision)


def _sigmoid(z):
    return 1.0 / (1.0 + jnp.exp(-z))


def _tile(n, want):
    t = min(n, want)
    assert n % t == 0, (n, t)
    return t


def _dep_spec(ngrid):
    return pl.BlockSpec((8, LANES), lambda *_: (0, 0))


def rms_fwd(x, g, name, dep=None):
    T, D = x.shape
    tm = _tile(T, 256)

    def body(x_ref, g_ref, *rest):
        h_ref = rest[-1]
        xv = x_ref[...]
        r = lax.rsqrt(jnp.mean(xv * xv, axis=-1, keepdims=True) + RMS_EPS)
        h_ref[...] = (xv * r * g_ref[...]).astype(BF16)

    in_specs = [pl.BlockSpec((tm, D), lambda i: (i, 0)), pl.BlockSpec((1, D), lambda i: (0, 0))]
    args = [x, g]
    if dep is not None:
        in_specs.append(_dep_spec(1))
        args.append(dep)
    return pl.pallas_call(
        body, name=name, grid=(T // tm,), in_specs=in_specs,
        out_specs=pl.BlockSpec((tm, D), lambda i: (i, 0)),
        out_shape=jax.ShapeDtypeStruct((T, D), BF16),
        compiler_params=_params(("parallel",)),
    )(*args)


def rms_bwd(dh, x, g, dres, name, dep=None):
    T, D = x.shape
    tm = _tile(T, 256)

    def body(dh_ref, x_ref, g_ref, dres_ref, *rest):
        dx_ref, dg_ref = rest[-2], rest[-1]
        i = pl.program_id(0)
        xv = x_ref[...]
        r = lax.rsqrt(jnp.mean(xv * xv, axis=-1, keepdims=True) + RMS_EPS)
        xh = xv * r
        d = dh_ref[...]
        u = d * g_ref[...]
        dx_ref[...] = dres_ref[...] + r * (u - xh * jnp.mean(u * xh, axis=-1, keepdims=True))
        part = jnp.sum(d * xh, axis=0, keepdims=True)

        @pl.when(i == 0)
        def _():
            dg_ref[...] = part

        @pl.when(i > 0)
        def _():
            dg_ref[...] += part

    row = pl.BlockSpec((tm, D), lambda i: (i, 0))
    vec = pl.BlockSpec((1, D), lambda i: (0, 0))
    in_specs = [row, row, vec, row]
    args = [dh, x, g, dres]
    if dep is not None:
        in_specs.append(_dep_spec(1))
        args.append(dep)
    return pl.pallas_call(
        body, name=name, grid=(T // tm,),
        in_specs=in_specs, out_specs=[row, vec],
        out_shape=[jax.ShapeDtypeStruct((T, D), F32), jax.ShapeDtypeStruct((1, D), F32)],
        compiler_params=_params(("arbitrary",)),
    )(*args)


def final_loss_bwd(x, g, target, name):
    T, D = x.shape
    tm = _tile(T, 256)

    def body(x_ref, g_ref, t_ref, loss_ref, dx_ref, dg_ref):
        i = pl.program_id(0)
        xv = x_ref[...]
        gv = g_ref[...]
        r = lax.rsqrt(jnp.mean(xv * xv, axis=-1, keepdims=True) + RMS_EPS)
        xh = xv * r
        e = xh * gv - t_ref[...]
        lpart = 0.5 * jnp.sum(jnp.mean(e * e, axis=-1, keepdims=True), axis=0, keepdims=True)
        lrow = jnp.broadcast_to(lpart, (1, LANES))
        d = e * (1.0 / D)
        u = d * gv
        dx_ref[...] = r * (u - xh * jnp.mean(u * xh, axis=-1, keepdims=True))
        part = jnp.sum(d * xh, axis=0, keepdims=True)

        @pl.when(i == 0)
        def _():
            dg_ref[...] = part
            loss_ref[...] = lrow

        @pl.when(i > 0)
        def _():
            dg_ref[...] += part
            loss_ref[...] += lrow

    row = pl.BlockSpec((tm, D), lambda i: (i, 0))
    vec = pl.BlockSpec((1, D), lambda i: (0, 0))
    return pl.pallas_call(
        body, name=name, grid=(T // tm,),
        in_specs=[row, vec, row],
        out_specs=[pl.BlockSpec((1, LANES), lambda i: (0, 0)), row, vec],
        out_shape=[jax.ShapeDtypeStruct((1, LANES), F32), jax.ShapeDtypeStruct((T, D), F32),
                   jax.ShapeDtypeStruct((1, D), F32)],
        compiler_params=_params(("arbitrary",)),
    )(x, g, target)


def _bf(v, scale=None):
    if scale is not None:
        v = v * scale
    return v.astype(BF16)


def mm_nn(a, b, name, *, tn, out_dtype, tm=512, n_out=None, b_block=None, b_map=None,
          res=None):
    T, K = a.shape
    N = n_out if n_out is not None else b.shape[1]
    tm = _tile(T, tm)
    tn = _tile(N, tn)
    b_block = b_block or (K, tn)
    b_map = b_map or (lambda n, i: (0, n))

    def body(*refs):
        a_ref, b_ref = refs[0], refs[1]
        o_ref = refs[-1]
        acc = _dot(_bf(a_ref[...]), _bf(b_ref[...]), 1, 0)
        if res is not None:
            acc = refs[2][...] + acc
        o_ref[...] = acc.astype(out_dtype)

    in_specs = [pl.BlockSpec((tm, K), lambda n, i: (i, 0)), pl.BlockSpec(b_block, b_map)]
    args = [a, b]
    if res is not None:
        in_specs.append(pl.BlockSpec((tm, tn), lambda n, i: (i, n)))
        args.append(res)
    return pl.pallas_call(
        body, name=name, grid=(N // tn, T // tm), in_specs=in_specs,
        out_specs=pl.BlockSpec((tm, tn), lambda n, i: (i, n)),
        out_shape=jax.ShapeDtypeStruct((T, N), out_dtype),
        compiler_params=_params(("parallel", "parallel")),
    )(*args)


def mm_nt(pairs, name, *, tn, out_dtype, tm=512, dep=None):
    T = pairs[0][0].shape[0]
    N = pairs[0][1].shape[0]
    tm = _tile(T, tm)
    tn = _tile(N, tn)
    npair = len(pairs)

    def body(*refs):
        o_ref = refs[-1]
        acc = None
        for q in range(npair):
            part = _dot(_bf(refs[2 * q][...]), _bf(refs[2 * q + 1][...]), 1, 1)
            acc = part if acc is None else acc + part
        o_ref[...] = acc.astype(out_dtype)

    in_specs, args = [], []
    for a, b in pairs:
        K = a.shape[1]
        in_specs += [pl.BlockSpec((tm, K), lambda n, i: (i, 0)), pl.BlockSpec((tn, K), lambda n, i: (n, 0))]
        args += [a, b]
    if dep is not None:
        in_specs.append(_dep_spec(2))
        args.append(dep)
    return pl.pallas_call(
        body, name=name, grid=(N // tn, T // tm), in_specs=in_specs,
        out_specs=pl.BlockSpec((tm, tn), lambda n, i: (i, n)),
        out_shape=jax.ShapeDtypeStruct((T, N), out_dtype),
        compiler_params=_params(("parallel", "parallel")),
    )(*args)


def mm_tn(a, b, name, *, grid, a_block, a_map, b_block, b_map, o_block, o_map, out_shape,
          b_scale=None, dep=None):
    def body(a_ref, b_ref, *rest):
        rest[-1][...] = _dot(_bf(a_ref[...]), _bf(b_ref[...], b_scale), 0, 0).astype(BF16)

    in_specs = [pl.BlockSpec(a_block, a_map), pl.BlockSpec(b_block, b_map)]
    args = [a, b]
    if dep is not None:
        in_specs.append(_dep_spec(len(grid)))
        args.append(dep)
    return pl.pallas_call(
        body, name=name, grid=grid, in_specs=in_specs,
        out_specs=pl.BlockSpec(o_block, o_map),
        out_shape=jax.ShapeDtypeStruct(out_shape, BF16),
        compiler_params=_params(("parallel",) * len(grid)),
    )(*args)


def mm_tn_plain(a, b, name, *, tm=512, tn=512, b_scale=None):
    T, M = a.shape
    N = b.shape[1]
    tm = _tile(M, tm)
    tn = _tile(N, tn)
    return mm_tn(a, b, name, grid=(M // tm, N // tn),
                 a_block=(T, tm), a_map=lambda m, n: (0, m),
                 b_block=(T, tn), b_map=lambda m, n: (0, n),
                 o_block=(tm, tn), o_map=lambda m, n: (m, n),
                 out_shape=(M, N), b_scale=b_scale)


def ffn_up(h, wgu, gi, ui, nper, name):
    T, D = h.shape
    F8 = wgu.shape[1]
    tm = _tile(T, 512)
    nt = T // tm

    def body(h_ref, wg_ref, wu_ref, a_ref, b_ref, s_ref):
        hv = h_ref[...]
        a = _dot(hv, wg_ref[...], 1, 0)
        b = _dot(hv, wu_ref[...], 1, 0)
        a_ref[...] = a.astype(BF16)
        b_ref[...] = b.astype(BF16)
        s_ref[...] = (a * _sigmoid(a) * b).astype(BF16)

    blk = pl.BlockSpec((tm, F8), lambda j, i: (j * nt + i, 0))
    shp = jax.ShapeDtypeStruct((N_DEV * T, F8), BF16)
    return pl.pallas_call(
        body, name=name, grid=(N_DEV, nt),
        in_specs=[pl.BlockSpec((tm, D), lambda j, i: (i, 0)),
                  pl.BlockSpec((D, F8), lambda j, i: (j * nper + gi, 0)),
                  pl.BlockSpec((D, F8), lambda j, i: (j * nper + ui, 0))],
        out_specs=[blk, blk, blk], out_shape=[shp, shp, shp],
        compiler_params=_params(("parallel", "parallel")),
    )(h, wgu, wgu)


def ffn_down(s, wd, di, nper, x, name):
    T, D = x.shape
    F8 = s.shape[1]
    tm = _tile(T, 512)
    nt = T // tm

    def body(s_ref, w_ref, x_ref, o_ref, acc_ref):
        j = pl.program_id(1)
        part = _dot(s_ref[...], w_ref[...], 1, 0)

        @pl.when(j == 0)
        def _():
            acc_ref[...] = part

        @pl.when(j > 0)
        def _():
            acc_ref[...] += part

        @pl.when(j == N_DEV - 1)
        def _():
            o_ref[...] = x_ref[...] + 0.5 * acc_ref[...]

    return pl.pallas_call(
        body, name=name, grid=(nt, N_DEV),
        in_specs=[pl.BlockSpec((tm, F8), lambda i, j: (j * nt + i, 0)),
                  pl.BlockSpec((F8, D), lambda i, j: (j * nper + di, 0)),
                  pl.BlockSpec((tm, D), lambda i, j: (i, 0))],
        out_specs=pl.BlockSpec((tm, D), lambda i, j: (i, 0)),
        out_shape=jax.ShapeDtypeStruct((T, D), F32),
        scratch_shapes=[pltpu.VMEM((tm, D), F32)],
        compiler_params=_params(("parallel", "arbitrary")),
    )(s, wd, x)


def ffn_bwd_act(dx, wd, di, nper_d, a, b, name, dep=None):
    T, D = dx.shape
    F8 = a.shape[1]
    tm = _tile(T, 512)
    nt = T // tm

    def body(dx_ref, w_ref, a_ref, b_ref, *rest):
        da_ref, db_ref = rest[-2], rest[-1]
        ds = _dot(_bf(dx_ref[...], 0.5), w_ref[...], 1, 1)
        av = a_ref[...].astype(F32)
        bv = b_ref[...].astype(F32)
        sg = _sigmoid(av)
        da_ref[...] = (ds * bv * (sg * (1.0 + av * (1.0 - sg)))).astype(BF16)
        db_ref[...] = (ds * (av * sg)).astype(BF16)

    blk = pl.BlockSpec((tm, F8), lambda j, i: (j * nt + i, 0))
    shp = jax.ShapeDtypeStruct((N_DEV * T, F8), BF16)
    in_specs = [pl.BlockSpec((tm, D), lambda j, i: (i, 0)),
                pl.BlockSpec((F8, D), lambda j, i: (j * nper_d + di, 0)), blk, blk]
    args = [dx, wd, a, b]
    if dep is not None:
        in_specs.append(_dep_spec(2))
        args.append(dep)
    return pl.pallas_call(
        body, name=name, grid=(N_DEV, nt), in_specs=in_specs,
        out_specs=[blk, blk], out_shape=[shp, shp],
        compiler_params=_params(("parallel", "parallel")),
    )(*args)


def ffn_bwd_dh(da, db, wgu, gi, ui, nper, D, name, dep=None):
    F8 = da.shape[1]
    T = da.shape[0] // N_DEV
    tm = _tile(T, 512)
    nt = T // tm

    def body(da_ref, db_ref, wg_ref, wu_ref, *rest):
        o_ref, acc_ref = rest[-2], rest[-1]
        j = pl.program_id(1)
        part = _dot(da_ref[...], wg_ref[...], 1, 1) + _dot(db_ref[...], wu_ref[...], 1, 1)

        @pl.when(j == 0)
        def _():
            acc_ref[...] = part

        @pl.when(j > 0)
        def _():
            acc_ref[...] += part

        @pl.when(j == N_DEV - 1)
        def _():
            o_ref[...] = acc_ref[...]

    blk = pl.BlockSpec((tm, F8), lambda i, j: (j * nt + i, 0))
    in_specs = [blk, blk,
                pl.BlockSpec((D, F8), lambda i, j: (j * nper + gi, 0)),
                pl.BlockSpec((D, F8), lambda i, j: (j * nper + ui, 0))]
    args = [da, db, wgu, wgu]
    if dep is not None:
        in_specs.append(_dep_spec(2))
        args.append(dep)
    return pl.pallas_call(
        body, name=name, grid=(nt, N_DEV), in_specs=in_specs,
        out_specs=pl.BlockSpec((tm, D), lambda i, j: (i, 0)),
        out_shape=jax.ShapeDtypeStruct((T, D), F32),
        scratch_shapes=[pltpu.VMEM((tm, D), F32)],
        compiler_params=_params(("parallel", "arbitrary")),
    )(*args)


def ffn_bwd_dw_in(h, dact, name, dep=None):
    T, D = h.shape
    F8 = dact.shape[1]
    tm = _tile(D, 512)
    nm = D // tm
    return mm_tn(h, dact, name, grid=(N_DEV, nm),
                 a_block=(T, tm), a_map=lambda j, m: (0, m),
                 b_block=(T, F8), b_map=lambda j, m: (j, 0),
                 o_block=(tm, F8), o_map=lambda j, m: (j * nm + m, 0),
                 out_shape=(N_DEV * D, F8), dep=dep)


def ffn_bwd_dw_down(s, dx, name):
    T, D = dx.shape
    F8 = s.shape[1]
    tn = _tile(D, 512)
    return mm_tn(s, dx, name, grid=(N_DEV, D // tn),
                 a_block=(T, F8), a_map=lambda j, n: (j, 0),
                 b_block=(T, tn), b_map=lambda j, n: (0, n),
                 o_block=(F8, tn), o_map=lambda j, n: (j, n),
                 out_shape=(N_DEV * F8, D), b_scale=0.5)


def _t5_bucket_np(dist):
    max_exact = N_REL_BUCKETS // 2
    d = np.maximum(dist, 1).astype(np.float64)
    large = max_exact + (np.log(d / max_exact) / math.log(REL_MAX_DISTANCE / max_exact)
                         * (N_REL_BUCKETS - max_exact)).astype(np.int64)
    large32 = max_exact + (np.log(d.astype(np.float32) / np.float32(max_exact))
                           / np.float32(math.log(REL_MAX_DISTANCE / max_exact))
                           * np.float32(N_REL_BUCKETS - max_exact)).astype(np.int64)
    assert np.array_equal(large, large32)
    large = np.minimum(large, N_REL_BUCKETS - 1)
    return np.where(dist < max_exact, dist, large)


def _distance_tables(T, tq):
    dist = np.arange(T)
    mult = np.zeros(T, np.int64)
    for window, dilation in DILATED_PATTERNS:
        mult += ((dist % dilation == 0) & (dist // dilation <= window // dilation)).astype(np.int64)
    logm = np.where(mult > 0, np.log(np.maximum(mult, 1)), NEG_INF).astype(np.float32)
    bucket = _t5_bucket_np(dist).astype(np.int32)
    nkb = T // tq
    k = np.arange(nkb)[:, None, None]
    r = np.arange(tq)[None, :, None]
    c = np.arange(tq)[None, None, :]
    delta = k * tq + r - c
    return bucket, logm, delta


def _tile_buckets(T, tq):
    bucket, logm, delta = _distance_tables(T, tq)
    safe = np.maximum(delta, 0)
    bidx = np.where(delta >= 0, bucket[safe], -1).astype(np.int32)
    logm_t = np.where(delta >= 0, logm[safe], NEG_INF).astype(np.float32)
    present = [sorted(set(np.unique(bidx[k]).tolist()) - {-1}) for k in range(T // tq)]
    return bidx, logm_t, present


def bias_tiles(rel_table, T, tq):
    bidx, logm_t, present = _tile_buckets(T, tq)
    nkb = T // tq

    def body(tab_ref, b_ref, lm_ref, o_ref):
        slot = pl.program_id(0)

        @pl.when(slot == 0)
        def _():
            o_ref[...] = jnp.where(b_ref[...] >= 0, 0.0, NEG_INF)

        @pl.when(slot > 0)
        def _():
            for k in range(nkb):
                bi = b_ref[k]
                acc = lm_ref[k]
                for b in present[k]:
                    acc = acc + jnp.where(bi == b, tab_ref[b, slot - 1], 0.0)
                o_ref[k] = acc

    full = pl.BlockSpec((nkb, tq, tq), lambda s: (0, 0, 0))
    return pl.pallas_call(
        body, name="bias_tiles", grid=(1 + N_DIL,),
        in_specs=[pl.BlockSpec(memory_space=pltpu.SMEM), full, full],
        out_specs=pl.BlockSpec((None, nkb, tq, tq), lambda s: (s, 0, 0, 0)),
        out_shape=jax.ShapeDtypeStruct((1 + N_DIL, nkb, tq, tq), F32),
        compiler_params=_params(("parallel",)),
    )(rel_table, jnp.asarray(bidx), jnp.asarray(logm_t))


def fox_gate_fwd(uf, bf, name):
    T = uf.shape[0]
    tb = _tile(T, 512)

    def body(u_ref, b_ref, c_ref, ct_ref):
        lane = lax.broadcasted_iota(jnp.int32, (1, LANES), 1)
        tri = (lax.broadcasted_iota(jnp.int32, (tb, tb), 0)
               >= lax.broadcasted_iota(jnp.int32, (tb, tb), 1)).astype(F32)
        carry = jnp.zeros((1, LANES), F32)
        for blk in range(T // tb):
            z = u_ref[pl.ds(blk * tb, tb), :] + b_ref[...]
            lf = jnp.minimum(z, 0.0) - jnp.log1p(jnp.exp(-jnp.abs(z)))
            lf = jnp.where(lane < N_FOX, lf, 0.0)
            cb = _dot(tri, lf, 1, 0, precision=lax.Precision.HIGHEST) + carry
            c_ref[pl.ds(blk * tb, tb), :] = cb
            ct_ref[:, pl.ds(blk * tb, tb)] = cb.T
            carry = cb[tb - 1:tb, :]

    return pl.pallas_call(
        body, name=name,
        out_shape=[jax.ShapeDtypeStruct((T, LANES), F32), jax.ShapeDtypeStruct((LANES, T), F32)],
        compiler_params=_params(None),
    )(uf, bf)


def fox_gate_bwd(dct, uf, bf, name):
    T = uf.shape[0]
    tb = _tile(T, 512)

    def body(d_ref, u_ref, b_ref, du_ref, db_ref):
        lane = lax.broadcasted_iota(jnp.int32, (1, LANES), 1)
        triu = (lax.broadcasted_iota(jnp.int32, (tb, tb), 0)
                <= lax.broadcasted_iota(jnp.int32, (tb, tb), 1)).astype(F32)
        carry = jnp.zeros((1, LANES), F32)
        dbv = jnp.zeros((1, LANES), F32)
        for blk in reversed(range(T // tb)):
            dc = d_ref[:, pl.ds(blk * tb, tb)].T
            dlf = _dot(triu, dc, 1, 0, precision=lax.Precision.HIGHEST) + carry
            carry = dlf[0:1, :]
            z = u_ref[pl.ds(blk * tb, tb), :] + b_ref[...]
            dz = jnp.where(lane < N_FOX, dlf * (1.0 - _sigmoid(z)), 0.0)
            du_ref[pl.ds(blk * tb, tb), :] = dz
            dbv = dbv + jnp.sum(dz, axis=0, keepdims=True)
        db_ref[...] = dbv

    return pl.pallas_call(
        body, name=name,
        out_shape=[jax.ShapeDtypeStruct((T, LANES), F32), jax.ShapeDtypeStruct((1, LANES), F32)],
        compiler_params=_params(None),
    )(dct, uf, bf)


def _bias_slot(h):
    return jnp.maximum(h - (N_FOX - 1), 0)


def _scores(q_ref, k_ref, c_ref, ct_ref, tb_ref, h, i, tq):
    scale = HEAD_DIM ** -0.5
    n = (i + 1) * tq
    rows = pl.ds(i * tq, tq)
    s = _dot(q_ref[rows, :], k_ref[pl.ds(0, n), :], 1, 1) * scale
    lane = lax.broadcasted_iota(jnp.int32, (1, LANES), 1)
    c_col = jnp.sum(jnp.where(lane == h, c_ref[rows, :], 0.0), axis=1, keepdims=True)
    c_row = ct_ref[pl.ds(h, 1), pl.ds(0, n)]
    bias = jnp.concatenate([tb_ref[i - jb] for jb in range(i + 1)], axis=1)
    return s + (c_col - c_row) + bias


def _attn_specs(T, tq):
    nkb = T // tq
    return [
        pl.BlockSpec((T, HEAD_DIM), lambda h: (0, h)),
        pl.BlockSpec((T, HEAD_DIM), lambda h: (0, N_HEADS + h)),
        pl.BlockSpec((T, HEAD_DIM), lambda h: (0, 2 * N_HEADS + h)),
        pl.BlockSpec((T, LANES), lambda h: (0, 0)),
        pl.BlockSpec((LANES, T), lambda h: (0, 0)),
        pl.BlockSpec((None, nkb, tq, tq), lambda h: (_bias_slot(h), 0, 0, 0)),
    ]


def attention_fwd(qkv, c, ct, tiles, name):
    T = qkv.shape[0]
    tq = tiles.shape[2]

    def body(q_ref, k_ref, v_ref, c_ref, ct_ref, tb_ref, o_ref):
        h = pl.program_id(0)
        for i in range(T // tq):
            s = _scores(q_ref, k_ref, c_ref, ct_ref, tb_ref, h, i, tq)
            p = jnp.exp(s - jnp.max(s, axis=1, keepdims=True))
            l = jnp.sum(p, axis=1, keepdims=True)
            o = _dot((p / l).astype(BF16), v_ref[pl.ds(0, (i + 1) * tq), :], 1, 0)
            o_ref[pl.ds(i * tq, tq), :] = o.astype(BF16)

    return pl.pallas_call(
        body, name=name, grid=(N_HEADS,),
        in_specs=_attn_specs(T, tq),
        out_specs=pl.BlockSpec((T, HEAD_DIM), lambda h: (0, h)),
        out_shape=jax.ShapeDtypeStruct((T, N_HEADS * HEAD_DIM), BF16),
        compiler_params=_params(("parallel",)),
    )(qkv, qkv, qkv, c, ct, tiles)


def attention_bwd(qkv, c, ct, tiles, do, name):
    T = qkv.shape[0]
    tq = tiles.shape[2]
    nkb = T // tq
    scale = HEAD_DIM ** -0.5

    def body(q_ref, k_ref, v_ref, c_ref, ct_ref, tb_ref, do_ref,
             dq_ref, dk_ref, dv_ref, dct_ref, dtb_ref, dk_acc, dv_acc):
        h = pl.program_id(0)
        dk_acc[...] = jnp.zeros_like(dk_acc)
        dv_acc[...] = jnp.zeros_like(dv_acc)
        dct_ref[...] = jnp.zeros_like(dct_ref)
        dtb_ref[...] = jnp.zeros_like(dtb_ref)
        for i in range(nkb):
            rows, keys = pl.ds(i * tq, tq), pl.ds(0, (i + 1) * tq)
            s = _scores(q_ref, k_ref, c_ref, ct_ref, tb_ref, h, i, tq)
            p = jnp.exp(s - jnp.max(s, axis=1, keepdims=True))
            p = p / jnp.sum(p, axis=1, keepdims=True)
            dov = do_ref[rows, :]
            dp = _dot(dov, v_ref[keys, :], 1, 1)
            ds = p * (dp - jnp.sum(p * dp, axis=1, keepdims=True))
            ds_b = ds.astype(BF16)
            dq_ref[rows, :] = (_dot(ds_b, k_ref[keys, :], 1, 0) * scale).astype(BF16)
            dk_acc[keys, :] += _dot(ds_b, q_ref[rows, :], 0, 0) * scale
            dv_acc[keys, :] += _dot(p.astype(BF16), dov, 0, 0)
            dct_ref[:, keys] += -jnp.sum(ds, axis=0, keepdims=True)
            for jb in range(i + 1):
                dtb_ref[i - jb] += ds[:, jb * tq:(jb + 1) * tq]
        dk_ref[...] = dk_acc[...].astype(BF16)
        dv_ref[...] = dv_acc[...].astype(BF16)

    head_cols = jax.ShapeDtypeStruct((T, N_HEADS * HEAD_DIM), BF16)
    col = pl.BlockSpec((T, HEAD_DIM), lambda h: (0, h))
    return pl.pallas_call(
        body, name=name, grid=(N_HEADS,),
        in_specs=_attn_specs(T, tq) + [col],
        out_specs=[col, col, col,
                   pl.BlockSpec((None, 1, T), lambda h: (h, 0, 0)),
                   pl.BlockSpec((None, nkb, tq, tq), lambda h: (_bias_slot(h), 0, 0, 0))],
        out_shape=[head_cols, head_cols, head_cols,
                   jax.ShapeDtypeStruct((N_HEADS, 1, T), F32),
                   jax.ShapeDtypeStruct((1 + N_DIL, nkb, tq, tq), F32)],
        scratch_shapes=[pltpu.VMEM((T, HEAD_DIM), F32), pltpu.VMEM((T, HEAD_DIM), F32)],
        compiler_params=_params(("arbitrary",)),
    )(qkv, qkv, qkv, c, ct, tiles, do)


def rel_table_grad(dtiles, T, name):
    tq = dtiles.shape[2]
    nkb = T // tq
    bidx, _, present = _tile_buckets(T, tq)

    def body(d_ref, b_ref, o_ref):
        lane = lax.broadcasted_iota(jnp.int32, (1, LANES), 1)
        row = jnp.zeros((1, LANES), F32)
        for k in range(nkb):
            d = d_ref[k]
            bi = b_ref[k]
            for b in present[k]:
                v = jnp.sum(jnp.sum(jnp.where(bi == b, d, 0.0), axis=0, keepdims=True),
                            axis=1, keepdims=True)
                row = row + jnp.where(lane == b, v, 0.0)
        o_ref[...] = row

    return pl.pallas_call(
        body, name=name, grid=(N_DIL,),
        in_specs=[pl.BlockSpec((None, nkb, tq, tq), lambda h: (h + 1, 0, 0, 0)),
                  pl.BlockSpec((nkb, tq, tq), lambda h: (0, 0, 0))],
        out_specs=pl.BlockSpec((None, 1, LANES), lambda h: (h, 0, 0)),
        out_shape=jax.ShapeDtypeStruct((N_DIL, 1, LANES), F32),
        compiler_params=_params(("parallel",)),
    )(dtiles, jnp.asarray(bidx))


def ple_combine(x, z, pp, name):
    T, D = x.shape
    tm = _tile(T, 256)

    def body(x_ref, z_ref, p_ref, o_ref):
        o_ref[...] = x_ref[...] + _sigmoid(z_ref[...]) * p_ref[...]

    row = pl.BlockSpec((tm, D), lambda i: (i, 0))
    return pl.pallas_call(
        body, name=name, grid=(T // tm,), in_specs=[row, row, row], out_specs=row,
        out_shape=jax.ShapeDtypeStruct((T, D), F32), compiler_params=_params(("parallel",)),
    )(x, z, pp)


def ple_bwd_elem(dx, z, pp, name):
    T, D = dx.shape
    tm = _tile(T, 256)

    def body(dx_ref, z_ref, p_ref, dz_ref, dp_ref):
        gate = _sigmoid(z_ref[...])
        d = dx_ref[...]
        dz_ref[...] = (d * p_ref[...] * gate * (1.0 - gate)).astype(BF16)
        dp_ref[...] = (d * gate).astype(BF16)

    row = pl.BlockSpec((tm, D), lambda i: (i, 0))
    shp = jax.ShapeDtypeStruct((T, D), BF16)
    return pl.pallas_call(
        body, name=name, grid=(T // tm,), in_specs=[row, row, row], out_specs=[row, row],
        out_shape=[shp, shp], compiler_params=_params(("parallel",)),
    )(dx, z, pp)


def _peer_list():
    x, y, c = lax.axis_index("x"), lax.axis_index("y"), lax.axis_index("c")
    me = 4 * x + 2 * y + c
    peers = []
    for fx in (0, 1):
        for fy in (0, 1):
            for fc in (0, 1):
                if fx or fy or fc:
                    px = 1 - x if fx else x
                    py = 1 - y if fy else y
                    pc = 1 - c if fc else c
                    peers.append(((px, py, pc), 4 * px + 2 * py + pc))
    return me, peers


_HBM = pl.BlockSpec(memory_space=pltpu.HBM)
_SEM = pl.BlockSpec(memory_space=pltpu.SEMAPHORE)
_EFFECT = pltpu.SideEffectType.DATAFLOW_SIDE_EFFECTING
N_PEERS = N_DEV - 1


def _in_hbm(a):
    return pltpu.with_memory_space_constraint(a, pltpu.HBM)


def _exchange_copies(srcs, lands, send_sems, recv_sems, blockwise):
    me, peers = _peer_list()
    sends, recvs = [], []
    for a in range(len(srcs)):
        for k, (dev, idx) in enumerate(peers):
            src = srcs[a].at[idx] if blockwise[a] else srcs[a]
            sends.append(pltpu.make_async_remote_copy(
                src_ref=src, dst_ref=lands[a].at[me], send_sem=send_sems[a].at[k],
                recv_sem=recv_sems[a].at[k], device_id=dev, device_id_type=MESH))
            recvs.append(pltpu.make_async_remote_copy(
                src_ref=src, dst_ref=lands[a].at[idx], send_sem=send_sems[a].at[k],
                recv_sem=recv_sems[a].at[k], device_id=dev, device_id_type=MESH))
    return sends, recvs


def exchange_start(srcs, lands, blockwise, name):
    n = len(srcs)

    def body(*refs):
        src_in, land_in = refs[:n], refs[n:2 * n]
        send_sems, recv_sems = refs[2 * n:3 * n], refs[3 * n:4 * n]
        token = refs[6 * n]
        sends, _ = _exchange_copies(src_in, land_in, send_sems, recv_sems, blockwise)
        for cp in sends:
            cp.start()
        token[...] = jnp.zeros_like(token)

    out_shape = ([pltpu.SemaphoreType.DMA((N_PEERS,))] * (2 * n)
                 + [pltpu.HBM(s.shape, s.dtype) for s in srcs]
                 + [pltpu.HBM(l.shape, l.dtype) for l in lands]
                 + [jax.ShapeDtypeStruct((8, LANES), F32)])
    aliases = {a: 2 * n + a for a in range(2 * n)}
    outs = pl.pallas_call(
        body, name=name, out_shape=out_shape,
        in_specs=[_HBM] * (2 * n),
        out_specs=[_SEM] * (2 * n) + [_HBM] * (2 * n) + [pl.BlockSpec(memory_space=pltpu.VMEM)],
        input_output_aliases=aliases,
        compiler_params=pltpu.CompilerParams(has_side_effects=_EFFECT),
    )(*[_in_hbm(s) for s in srcs], *[_in_hbm(l) for l in lands])
    return (outs[:n], outs[n:2 * n], outs[2 * n:3 * n], outs[3 * n:4 * n], outs[4 * n])


def exchange_wait(send_sems, recv_sems, srcs, lands, blockwise, after, name):
    n = len(srcs)

    def body(*refs):
        src_in, land_in = refs[:n], refs[n:2 * n]
        ss, rs = refs[2 * n:3 * n], refs[3 * n:4 * n]
        sends, recvs = _exchange_copies(src_in, land_in, ss, rs, blockwise)
        for cp in sends:
            cp.wait_send()
        for cp in recvs:
            cp.wait_recv()

    outs = pl.pallas_call(
        body, name=name,
        out_shape=[pltpu.HBM(s.shape, s.dtype) for s in srcs] + [pltpu.HBM(l.shape, l.dtype) for l in lands],
        in_specs=[_HBM] * (2 * n) + [_SEM] * (2 * n) + [pl.BlockSpec(memory_space=pl.ANY)],
        out_specs=[_HBM] * (2 * n),
        input_output_aliases={a: a for a in range(2 * n)},
        compiler_params=pltpu.CompilerParams(has_side_effects=_EFFECT),
    )(*srcs, *lands, *send_sems, *recv_sems, after)
    return outs[n:]


def _landing(own_block, me, slots=N_DEV):
    empty = lax.empty((slots,) + own_block.shape, own_block.dtype)
    return lax.dynamic_update_slice(empty, own_block[None], (me,) + (0,) * own_block.ndim)


N_CHIPS = N_DEV // 2
_CHIP_FLIPS = ((1, 0), (0, 1), (1, 1))


def _xyc():
    return lax.axis_index("x"), lax.axis_index("y"), lax.axis_index("c")


def _other_chips(x, y):
    return [(1 - x if fx else x, 1 - y if fy else y) for fx, fy in _CHIP_FLIPS]


def _remote(src, dst, send_sem, recv_sem, dev):
    return pltpu.make_async_remote_copy(src_ref=src, dst_ref=dst, send_sem=send_sem, recv_sem=recv_sem,
                                        device_id=dev, device_id_type=MESH)


def comm_call(name, bufs, sems_in, sems_out, fn, after=None, want_token=False):
    nb, ni, no = len(bufs), len(sems_in), len(sems_out)
    na = 0 if after is None else 1

    def body(*refs):
        buf_refs = refs[:nb]
        sin = refs[nb:nb + ni]
        sout = refs[nb + ni + na:nb + ni + na + no]
        fn(buf_refs, sin, sout)
        if want_token:
            tok = refs[nb + ni + na + no + nb]
            tok[...] = jnp.zeros_like(tok)

    out_shape = list(sems_out) + [pltpu.HBM(b.shape, b.dtype) for b in bufs]
    out_specs = [_SEM] * no + [_HBM] * nb
    if want_token:
        out_shape.append(jax.ShapeDtypeStruct((8, LANES), F32))
        out_specs.append(pl.BlockSpec(memory_space=pltpu.VMEM))
    args = [_in_hbm(b) for b in bufs] + list(sems_in) + ([after] if na else [])
    outs = pl.pallas_call(
        body, name=name, out_shape=out_shape,
        in_specs=[_HBM] * nb + [_SEM] * ni + [pl.BlockSpec(memory_space=pl.ANY)] * na,
        out_specs=out_specs, input_output_aliases={a: no + a for a in range(nb)},
        compiler_params=pltpu.CompilerParams(has_side_effects=_EFFECT),
    )(*args)
    return list(outs[:no]), list(outs[no:no + nb]), (outs[no + nb] if want_token else None)


def _dma_sems(*sizes):
    return [pltpu.SemaphoreType.DMA((s,)) for s in sizes]


def gather_start(srcs, lands, name):
    n = len(srcs)

    def fn(bufs, sin, sout):
        x, y, c = _xyc()
        me = 4 * x + 2 * y + c
        for a in range(n):
            src, land = bufs[a], bufs[n + a]
            send, recv_d, recv_i = sout[3 * a:3 * a + 3]
            _remote(src, land.at[me], send.at[0], recv_d.at[0], (x, y, 1 - c)).start()
            for k, (px, py) in enumerate(_other_chips(x, y)):
                _remote(src, land.at[me], send.at[1 + k], recv_i.at[k], (px, py, c)).start()

    return comm_call(name, list(srcs) + list(lands), [], _dma_sems(4, 1, 3) * n, fn, want_token=True)


def gather_forward(srcs, lands, recv_i, after, name):
    n = len(srcs)

    def fn(bufs, sin, sout):
        x, y, c = _xyc()
        for a in range(n):
            src, land = bufs[a], bufs[n + a]
            f_send, f_recv = sout[2 * a:2 * a + 2]
            for k, (px, py) in enumerate(_other_chips(x, y)):
                blk = land.at[4 * px + 2 * py + c]
                _remote(src, blk, f_send.at[k], sin[a].at[k], (px, py, c)).wait_recv()
                _remote(blk, blk, f_send.at[k], f_recv.at[k], (x, y, 1 - c)).start()

    sems, bufs, _ = comm_call(name, list(srcs) + list(lands), recv_i, _dma_sems(3, 3) * n, fn, after=after)
    return sems, bufs


def gather_wait(srcs, lands, send, recv_d, f_send, f_recv, after, name):
    n = len(srcs)

    def fn(bufs, sin, sout):
        x, y, c = _xyc()
        sib = (x, y, 1 - c)
        for a in range(n):
            src, land = bufs[a], bufs[n + a]
            s_send, s_recv_d, s_fsend, s_frecv = sin[4 * a:4 * a + 4]
            sib_blk = land.at[4 * x + 2 * y + 1 - c]
            for k in range(4):
                _remote(src, sib_blk, s_send.at[k], s_recv_d.at[0], sib).wait_send()
            _remote(src, sib_blk, s_send.at[0], s_recv_d.at[0], sib).wait_recv()
            for k, (px, py) in enumerate(_other_chips(x, y)):
                cp = _remote(src, land.at[4 * px + 2 * py + 1 - c], s_fsend.at[k], s_frecv.at[k], sib)
                cp.wait_send()
                cp.wait_recv()

    sems_in = []
    for a in range(n):
        sems_in += [send[a], recv_d[a], f_send[a], f_recv[a]]
    _, bufs, _ = comm_call(name, list(srcs) + list(lands), sems_in, [], fn, after=after)
    return bufs[n:]


def scatter_pair_start(src4s, lands, name, after=None):
    n = len(src4s)

    def fn(bufs, sin, sout):
        x, y, c = _xyc()
        for a in range(n):
            _remote(bufs[a].at[:, 1 - c], bufs[n + a], sout[2 * a].at[0], sout[2 * a + 1].at[0],
                    (x, y, 1 - c)).start()

    return comm_call(name, list(src4s) + list(lands), [], _dma_sems(1, 1) * n, fn, after=after, want_token=True)


def scatter_pair_wait(src4s, lands, sems, after, name):
    n = len(src4s)

    def fn(bufs, sin, sout):
        x, y, c = _xyc()
        for a in range(n):
            cp = _remote(bufs[a].at[:, 1 - c], bufs[n + a], sin[2 * a].at[0], sin[2 * a + 1].at[0], (x, y, 1 - c))
            cp.wait_send()
            cp.wait_recv()

    _, bufs, _ = comm_call(name, list(src4s) + list(lands), sems, [], fn, after=after)
    return bufs[:n], bufs[n:]


def _row_tile(R):
    for cand in (128, 64, 32, 16):
        if R % cand == 0:
            return cand
    return R


def chip_sum(src4, land, c, name):
    _, _, R, C = src4.shape
    tr = R

    def body(c_ref, a_ref, b_ref, o_ref):
        o_ref[...] = (a_ref[...].astype(F32) + b_ref[...].astype(F32)).astype(BF16)

    grid_spec = pltpu.PrefetchScalarGridSpec(
        num_scalar_prefetch=1, grid=(N_CHIPS, R // tr),
        in_specs=[pl.BlockSpec((None, None, tr, C), lambda q, i, cr: (q, cr[0], i, 0)),
                  pl.BlockSpec((None, tr, C), lambda q, i, cr: (q, i, 0))],
        out_specs=pl.BlockSpec((None, tr, C), lambda q, i, cr: (q, i, 0)))
    return pl.pallas_call(
        body, name=name, grid_spec=grid_spec,
        out_shape=jax.ShapeDtypeStruct((N_CHIPS, R, C), BF16),
        compiler_params=_params(("parallel", "parallel")),
    )(c.reshape(1).astype(jnp.int32), src4, land)


def scatter_chip_start(sums, lands, name):
    n = len(sums)

    def fn(bufs, sin, sout):
        x, y, c = _xyc()
        for a in range(n):
            for k, (px, py) in enumerate(_other_chips(x, y)):
                _remote(bufs[a].at[2 * px + py], bufs[n + a].at[2 * x + y], sout[2 * a].at[k], sout[2 * a + 1].at[k],
                        (px, py, c)).start()

    return comm_call(name, list(sums) + list(lands), [], _dma_sems(3, 3) * n, fn, want_token=True)


def scatter_chip_wait(sums, lands, sems, after, name):
    n = len(sums)

    def fn(bufs, sin, sout):
        x, y, c = _xyc()
        for a in range(n):
            for k, (px, py) in enumerate(_other_chips(x, y)):
                cp = _remote(bufs[a].at[2 * px + py], bufs[n + a].at[2 * px + py], sin[2 * a].at[k],
                             sin[2 * a + 1].at[k], (px, py, c))
                cp.wait_send()
                cp.wait_recv()

    _, bufs, _ = comm_call(name, list(sums) + list(lands), sems, [], fn, after=after)
    return bufs[:n], bufs[n:]


def _adamw_math(w, g, m, v):
    m = ADAM_B1 * m + (1.0 - ADAM_B1) * g
    v = ADAM_B2 * v + (1.0 - ADAM_B2) * (g * g)
    m_hat = m / (1.0 - ADAM_B1 ** ADAM_STEP)
    v_hat = v / (1.0 - ADAM_B2 ** ADAM_STEP)
    delta = -ADAM_LR * (m_hat / (jnp.sqrt(v_hat) + ADAM_EPS) + ADAM_WD * w)
    return delta, m, v


def adamw_sharded(parts, w, m, v, name):
    R, C = w.shape
    S = parts.shape[0]
    tr = _row_tile(R)

    def body(p_ref, w_ref, m_ref, v_ref, g_ref, d_ref, nm_ref, nv_ref):
        g = p_ref[0].astype(F32)
        for s in range(1, S):
            g = g + p_ref[s].astype(F32)
        delta, nm, nv = _adamw_math(w_ref[...], g, m_ref[...], v_ref[...])
        g_ref[...] = g
        d_ref[...] = delta
        nm_ref[...] = nm
        nv_ref[...] = nv

    row = pl.BlockSpec((tr, C), lambda i: (i, 0))
    shp = jax.ShapeDtypeStruct((R, C), F32)
    return pl.pallas_call(
        body, name=name, grid=(R // tr,),
        in_specs=[pl.BlockSpec((S, tr, C), lambda i: (0, i, 0)), row, row, row],
        out_specs=[row, row, row, row], out_shape=[shp, shp, shp, shp],
        compiler_params=_params(("parallel",)),
    )(parts, w, m, v)


def adamw_small(parts, w, m, v, name):
    R, C = w.shape

    def body(p_ref, w_ref, m_ref, v_ref, g_ref, d_ref, nm_ref, nv_ref):
        g = p_ref[0]
        for s in range(1, N_DEV):
            g = g + p_ref[s]
        delta, nm, nv = _adamw_math(w_ref[...], g, m_ref[...], v_ref[...])
        g_ref[...] = g
        d_ref[...] = delta
        nm_ref[...] = nm
        nv_ref[...] = nv

    shp = jax.ShapeDtypeStruct((R, C), F32)
    return pl.pallas_call(
        body, name=name, out_shape=[shp, shp, shp, shp], compiler_params=_params(None),
    )(parts, w, m, v)


_ROW_NORM_FFN1, _ROW_NORM_MIX, _ROW_NORM_FFN2, _ROW_NORM_PLE, _ROW_NORM_FINAL = 0, 1, 2, 3, 4
_ROW_B_F, _ROW_REL, _ROW_LOSS, _SMALL_ROWS = 5, 6, 7, 8


def _pack_small(D, norm_ffn1, norm_mix, norm_ffn2, norm_ple, norm_final, b_f, rel_table):
    def row(v):
        v = v.reshape(1, -1)
        return jnp.pad(v, ((0, 0), (0, D - v.shape[1])))
    return jnp.concatenate([row(norm_ffn1), row(norm_mix), row(norm_ffn2), row(norm_ple),
                            row(norm_final), row(b_f), row(rel_table),
                            jnp.zeros((1, D), F32)], axis=0)


def _unpack_small(a, shapes):
    return {"norm_ffn1": a[_ROW_NORM_FFN1].reshape(shapes["norm_ffn1"]),
            "norm_mix": a[_ROW_NORM_MIX].reshape(shapes["norm_mix"]),
            "b_f": a[_ROW_B_F, :N_FOX].reshape(shapes["b_f"]),
            "norm_ffn2": a[_ROW_NORM_FFN2].reshape(shapes["norm_ffn2"]),
            "norm_ple": a[_ROW_NORM_PLE].reshape(shapes["norm_ple"]),
            "rel_table": a[_ROW_REL, :N_REL_BUCKETS * N_DIL].reshape(shapes["rel_table"]),
            "norm_final": a[_ROW_NORM_FINAL].reshape(shapes["norm_final"])}


def local_step(x, p, tgt, g_ffn1, g_mix, g_ffn2, g_ple, g_final, b_f, rel_table,
               forward, weights, emit, emit2, first_dep):
    T, D = x.shape
    P = p.shape[1]
    CW = D // N_DEV
    tq = _tile(T, 256)

    h1 = rms_fwd(x, g_ffn1, "rms_ffn1", dep=first_dep)
    tiles = bias_tiles(rel_table, T, tq)
    forward("ffn1_gu", tiles)
    wgu1, = weights("ffn1_gu", h1)
    a1, b1, s1 = ffn_up(h1, wgu1, 0, 1, 2, "ffn1_up")
    forward("ffn1_d", s1)
    wd1, = weights("ffn1_d", s1)
    x1 = ffn_down(s1, wd1, 0, 1, x, "ffn1_down")

    h2 = rms_fwd(x1, g_mix, "rms_mix")
    forward("mix", h2)
    w3, wf, wo = weights("mix", h2)
    qkv = mm_nn(h2, w3, "mix_qkv", tn=768, out_dtype=BF16)
    uf = mm_nn(h2, wf, "mix_forget", tn=LANES, out_dtype=F32)
    bfp = jnp.pad(b_f.reshape(1, N_FOX), ((0, 0), (0, LANES - N_FOX)))
    c, ct = fox_gate_fwd(uf, bfp, "fox_gate")
    cat = attention_fwd(qkv, c, ct, tiles, "attention")
    forward("ffn2", cat)
    x2 = mm_nn(cat, wo, "mix_out", tn=512, out_dtype=F32, res=x1)

    h3 = rms_fwd(x2, g_ffn2, "rms_ffn2")
    wgu2, wd2 = weights("ffn2", h3)
    a2, b2, s2 = ffn_up(h3, wgu2, 0, 1, 2, "ffn2_up")
    forward("ple", s2)
    x3 = ffn_down(s2, wd2, 0, 1, x2, "ffn2_down")

    h4 = rms_fwd(x3, g_ple, "rms_ple")
    wpg, wpp = weights("ple", h4)
    z = mm_nn(h4, wpg, "ple_gate", tn=512, out_dtype=F32)
    pp = mm_nn(p, wpp, "ple_proj", tn=CW, out_dtype=F32, n_out=D,
               b_block=(P, CW), b_map=lambda n, i: (n, 0))
    x4 = ple_combine(x3, z, pp, "ple_combine")
    loss_row, dx4, dg_final = final_loss_bwd(x4, g_final, tgt, "final_loss")

    grads = {}
    dz, dpp = ple_bwd_elem(dx4, z, pp, "ple_bwd_elem")
    grads["w_ple_proj"] = mm_tn(p, dpp, "ple_proj_dw", grid=(N_DEV,),
                                a_block=(T, P), a_map=lambda n: (0, 0),
                                b_block=(T, CW), b_map=lambda n: (0, n),
                                o_block=(P, CW), o_map=lambda n: (n, 0),
                                out_shape=(N_DEV * P, CW))
    grads["w_ple_gate"] = mm_tn_plain(h4, dz, "ple_gate_dw")
    tok = emit("ple", grads)
    dh4 = mm_nt([(dz, wpg)], "ple_gate_dh", tn=512, out_dtype=F32, dep=tok)
    tok = emit2("ple", dh4)
    dx3, dg_ple = rms_bwd(dh4, x3, g_ple, dx4, "rms_ple_bwd", dep=tok)

    da2, db2 = ffn_bwd_act(dx3, wd2, 0, 1, a2, b2, "ffn2_bwd_act")
    grads["ffn2_w_down"] = ffn_bwd_dw_down(s2, dx3, "ffn2_down_dw")
    grads["ffn2_w_gate"] = ffn_bwd_dw_in(h3, da2, "ffn2_gate_dw")
    grads["ffn2_w_up"] = ffn_bwd_dw_in(h3, db2, "ffn2_up_dw")
    tok = emit("ffn2", grads)
    dh3 = ffn_bwd_dh(da2, db2, wgu2, 0, 1, 2, D, "ffn2_bwd_dh", dep=tok)
    tok = emit2("ffn2", dh3)
    dx2, dg_ffn2 = rms_bwd(dh3, x2, g_ffn2, dx3, "rms_ffn2_bwd", dep=tok)

    dcat = mm_nt([(dx2, wo)], "mix_out_dh", tn=512, out_dtype=BF16)
    grads["w_o"] = mm_tn_plain(cat, dx2, "mix_out_dw")
    dq, dk, dv, dct, dtiles = attention_bwd(qkv, c, ct, tiles, dcat, "attention_bwd")
    dctp = jnp.pad(dct[:, 0, :], ((0, LANES - N_HEADS), (0, 0)))
    duf, dbf = fox_gate_bwd(dctp, uf, bfp, "fox_gate_bwd")
    drel = rel_table_grad(dtiles, T, "rel_table_grad")[:, 0, :N_REL_BUCKETS].T
    du3 = jnp.concatenate([dq, dk, dv], axis=1)
    grads["w3"] = mm_tn_plain(h2, du3, "mix_qkv_dw", tn=768)
    grads["wf"] = mm_tn_plain(h2, duf, "mix_forget_dw", tn=LANES)
    tok = emit("mix", grads)
    dh2 = mm_nt([(du3, w3), (duf, wf)], "mix_in_dh", tn=512, out_dtype=F32, dep=tok)
    tok = emit2("mix", dh2)
    dx1, dg_mix = rms_bwd(dh2, x1, g_mix, dx2, "rms_mix_bwd", dep=tok)

    da1, db1 = ffn_bwd_act(dx1, wd1, 0, 1, a1, b1, "ffn1_bwd_act")
    grads["ffn1_w_down"] = ffn_bwd_dw_down(s1, dx1, "ffn1_down_dw")
    tok = emit("ffn1_d", grads)
    grads["ffn1_w_gate"] = ffn_bwd_dw_in(h1, da1, "ffn1_gate_dw", dep=tok)
    tok = emit2("ffn1_d", grads["ffn1_w_gate"])
    tok = emit("ffn1_g", grads, after=tok)
    grads["ffn1_w_up"] = ffn_bwd_dw_in(h1, db1, "ffn1_up_dw", dep=tok)
    tok = emit2("ffn1_g", grads["ffn1_w_up"])
    tok = emit("ffn1_u", grads, after=tok)
    dh1 = ffn_bwd_dh(da1, db1, wgu1, 0, 1, 2, D, "ffn1_bwd_dh", dep=tok)
    tok = emit2("ffn1_u", dh1)
    dx0, dg_ffn1 = rms_bwd(dh1, x, g_ffn1, dx1, "rms_ffn1_bwd", dep=tok)

    small = _pack_small(D, dg_ffn1, dg_mix, dg_ffn2, dg_ple, dg_final, dbf[:, :N_FOX], drel)
    small = small.at[_ROW_LOSS, :LANES].set(loss_row[0])
    grads["small"] = small
    emit("small", grads)
    return dx0


def _split_w_in(w_in_full):
    df, dd = N_FOX * HEAD_DIM, N_DIL * HEAD_DIM
    o = np.cumsum([0, df, df, df, N_FOX, dd, dd, dd]).tolist()
    qa, ka, va, f, qb, kb, vb = [w_in_full[:, o[i]:o[i + 1]] for i in range(7)]
    return jnp.concatenate([qa, qb, ka, kb, va, vb], axis=1), f


def _join_w_in(d3, dfg):
    df, dd = N_FOX * HEAD_DIM, N_DIL * HEAD_DIM
    o = np.cumsum([0, df, dd, df, dd, df, dd]).tolist()
    qa, qb, ka, kb, va, vb = [d3[:, o[i]:o[i + 1]] for i in range(6)]
    return jnp.concatenate([qa, ka, va, dfg, qb, kb, vb], axis=1)


def kernel(x, p, norm_ffn1, ffn1_w_gate, ffn1_w_up, ffn1_w_down, norm_mix, w_in, b_f, w_o, norm_ffn2, ffn2_w_gate, ffn2_w_up, ffn2_w_down, norm_ple, w_ple_gate, w_ple_proj, rel_table, norm_final, loss_target, m_norm_ffn1, m_ffn1_w_gate, m_ffn1_w_up, m_ffn1_w_down, m_norm_mix, m_w_in, m_b_f, m_w_o, m_norm_ffn2, m_ffn2_w_gate, m_ffn2_w_up, m_ffn2_w_down, m_norm_ple, m_w_ple_gate, m_w_ple_proj, m_rel_table, m_norm_final, v_norm_ffn1, v_ffn1_w_gate, v_ffn1_w_up, v_ffn1_w_down, v_norm_mix, v_w_in, v_b_f, v_w_o, v_norm_ffn2, v_ffn2_w_gate, v_ffn2_w_up, v_ffn2_w_down, v_norm_ple, v_w_ple_gate, v_w_ple_proj, v_rel_table, v_norm_final):
    names = ["norm_ffn1", "ffn1_w_gate", "ffn1_w_up", "ffn1_w_down", "norm_mix", "w_in", "b_f", "w_o",
             "norm_ffn2", "ffn2_w_gate", "ffn2_w_up", "ffn2_w_down", "norm_ple", "w_ple_gate",
             "w_ple_proj", "rel_table", "norm_final"]
    w = dict(zip(names, [norm_ffn1, ffn1_w_gate, ffn1_w_up, ffn1_w_down, norm_mix, w_in, b_f, w_o,
                         norm_ffn2, ffn2_w_gate, ffn2_w_up, ffn2_w_down, norm_ple, w_ple_gate,
                         w_ple_proj, rel_table, norm_final]))
    m = dict(zip(names, [m_norm_ffn1, m_ffn1_w_gate, m_ffn1_w_up, m_ffn1_w_down, m_norm_mix, m_w_in,
                         m_b_f, m_w_o, m_norm_ffn2, m_ffn2_w_gate, m_ffn2_w_up, m_ffn2_w_down,
                         m_norm_ple, m_w_ple_gate, m_w_ple_proj, m_rel_table, m_norm_final]))
    v = dict(zip(names, [v_norm_ffn1, v_ffn1_w_gate, v_ffn1_w_up, v_ffn1_w_down, v_norm_mix, v_w_in,
                         v_b_f, v_w_o, v_norm_ffn2, v_ffn2_w_gate, v_ffn2_w_up, v_ffn2_w_down,
                         v_norm_ple, v_w_ple_gate, v_w_ple_proj, v_rel_table, v_norm_final]))
    sharded = ["ffn1_w_gate", "ffn1_w_up", "ffn1_w_down", "w_in", "w_o", "ffn2_w_gate", "ffn2_w_up",
               "ffn2_w_down", "w_ple_gate", "w_ple_proj"]
    small_names = [n for n in names if n not in sharded]

    xs, ps, tgt = x[0], p[0, 0], loss_target[0]
    T, D = xs.shape
    sh = {n: w[n][0] for n in sharded}
    F8 = sh["ffn1_w_gate"].shape[1]
    WIN8 = sh["w_in"].shape[1]

    me = 4 * lax.axis_index("x") + 2 * lax.axis_index("y") + lax.axis_index("c")

    cat0 = lambda ns: jnp.concatenate([sh[n] for n in ns], axis=0).astype(BF16)
    gather_groups = {
        "ffn1_gu": [cat0(["ffn1_w_gate", "ffn1_w_up"])],
        "ffn1_d": [sh["ffn1_w_down"].astype(BF16)],
        "mix": [sh["w_in"].astype(BF16), sh["w_o"].astype(BF16)],
        "ffn2": [cat0(["ffn2_w_gate", "ffn2_w_up"]), sh["ffn2_w_down"].astype(BF16)],
        "ple": [sh["w_ple_gate"].astype(BF16), sh["w_ple_proj"].astype(BF16)],
    }
    order = ["ffn1_gu", "ffn1_d", "mix", "ffn2", "ple"]
    g_srcs = [s for grp in order for s in gather_groups[grp]]
    g_lands = [_landing(s, me) for s in g_srcs]
    n_g = len(g_srcs)
    g_sems, g_bufs, g_token = gather_start(g_srcs, g_lands, "gather_start")
    g_srcs, g_lands = g_bufs[:n_g], g_bufs[n_g:]
    g_send, g_recv_d, g_recv_i = g_sems[0::3], g_sems[1::3], g_sems[2::3]
    first = np.cumsum([0] + [len(gather_groups[grp]) for grp in order]).tolist()
    passed = {}

    def arrays_of(group):
        k = order.index(group)
        return slice(first[k], first[k + 1])

    def forward(group, after):
        sl = arrays_of(group)
        f_sems, bufs = gather_forward(g_srcs[sl], g_lands[sl], g_recv_i[sl], after, "gather_forward_" + group)
        k = len(bufs) // 2
        passed[group] = (f_sems[0::2], f_sems[1::2], bufs[:k], bufs[k:])

    def weights(group, after):
        sl = arrays_of(group)
        f_send, f_recv, srcs, lands = passed[group]
        got = gather_wait(srcs, lands, g_send[sl], g_recv_d[sl], f_send, f_recv, after, "gather_wait_" + group)
        if group == "ffn1_gu":
            return (got[0].reshape(N_DEV * 2 * D, F8),)
        if group == "ffn1_d":
            return (got[0].reshape(N_DEV * F8, D),)
        a0, a1 = got
        if group == "ffn2":
            return a0.reshape(N_DEV * 2 * D, F8), a1.reshape(N_DEV * F8, D)
        if group == "ple":
            return a0.reshape(-1, D), a1.reshape(-1, a1.shape[2])
        w_in_full = jnp.transpose(a0, (1, 0, 2)).reshape(D, N_DEV * WIN8)
        w3, wf8 = _split_w_in(w_in_full)
        return w3, jnp.pad(wf8, ((0, 0), (0, LANES - N_FOX))), a1.reshape(-1, D)

    scatter_groups = {
        "ple": ["w_ple_gate", "w_ple_proj"],
        "ffn2": ["ffn2_w_gate", "ffn2_w_up", "ffn2_w_down"],
        "mix": ["w_in", "w_o"],
        "ffn1_d": ["ffn1_w_down"],
        "ffn1_g": ["ffn1_w_gate"],
        "ffn1_u": ["ffn1_w_up"],
    }
    x_i, y_i, c_i = _xyc()
    my_chip = 2 * x_i + y_i
    pair_stage, chip_stage, small_stage = {}, {}, {}

    def emit(group, grads, after=None):
        if group == "small":
            src = grads["small"]
            ss, rs, srcs, lands, token = exchange_start([src], [_landing(src, me)], [False], "scatter_start_small")
            small_stage["small"] = (ss, rs, srcs, lands)
            return token
        src4s = []
        for n in scatter_groups[group]:
            if n == "w_in":
                d_w_in = _join_w_in(grads["w3"], grads["wf"][:, :N_FOX])
                full = jnp.transpose(d_w_in.reshape(D, N_DEV, WIN8), (1, 0, 2))
            else:
                full = grads[n].reshape((N_DEV,) + sh[n].shape)
            src4s.append(full.reshape((N_CHIPS, 2) + sh[n].shape))
        lands = [lax.empty((N_CHIPS,) + s.shape[2:], BF16) for s in src4s]
        sems, bufs, token = scatter_pair_start(src4s, lands, "scatter_pair_start_" + group, after=after)
        k = len(src4s)
        pair_stage[group] = (sems, bufs[:k], bufs[k:])
        return token

    def emit2(group, after):
        sems, src4s, lands = pair_stage[group]
        src4s, lands = scatter_pair_wait(src4s, lands, sems, after, "scatter_pair_wait_" + group)
        sums = [chip_sum(s4, la, c_i, "chip_sum_" + n)
                for s4, la, n in zip(src4s, lands, scatter_groups[group])]
        chip_lands = [_landing(lax.dynamic_index_in_dim(s, my_chip, 0, keepdims=False), my_chip, slots=N_CHIPS)
                      for s in sums]
        sems, bufs, token = scatter_chip_start(sums, chip_lands, "scatter_chip_start_" + group)
        k = len(sums)
        chip_stage[group] = (sems, bufs[:k], bufs[k:])
        return token

    dx0 = local_step(
        xs, ps, tgt, w["norm_ffn1"], w["norm_mix"], w["norm_ffn2"], w["norm_ple"],
        w["norm_final"].reshape(1, D), w["b_f"], w["rel_table"], forward, weights, emit, emit2, g_token)

    res = {}
    after = dx0
    for group in ["ple", "ffn2", "mix", "ffn1_d", "ffn1_g", "ffn1_u"]:
        sems, sums, chip_lands = chip_stage[group]
        _, parts = scatter_chip_wait(sums, chip_lands, sems, after, "scatter_chip_wait_" + group)
        for n, part in zip(scatter_groups[group], parts):
            g, d, nm, nv = adamw_sharded(part, sh[n], m[n][0], v[n][0], "adamw_" + n)
            res[n] = tuple(a.reshape(w[n].shape) for a in (g, d, nm, nv))
            after = g
    ss, rs, srcs, lands = small_stage["small"]
    small_parts, = exchange_wait(ss, rs, srcs, lands, [False], after, "scatter_wait_small")
    pack = lambda t: _pack_small(D, t["norm_ffn1"], t["norm_mix"], t["norm_ffn2"], t["norm_ple"],
                                 t["norm_final"], t["b_f"], t["rel_table"])
    gs, ds, ms, vs = adamw_small(small_parts, pack(w), pack(m), pack(v), "adamw_small")
    shapes = {n: w[n].shape for n in small_names}
    unpacked = [_unpack_small(a, shapes) for a in (gs, ds, ms, vs)]
    for n in small_names:
        res[n] = tuple(u[n] for u in unpacked)
    loss = gs[_ROW_LOSS, 0]

    out = [loss, dx0.reshape(x.shape)]
    for k in range(4):
        out += [res[n][k] for n in names]
    return tuple(out)
```

```python
import functools
import math

import numpy as np
import jax
import jax.numpy as jnp
from jax import lax
from jax.experimental import pallas as pl
from jax.experimental.pallas import tpu as pltpu

F32 = jnp.float32
BF16 = jnp.bfloat16

N_DEV = 8
HEAD_DIM = 128
N_FOX = 8
N_DIL = 8
N_HEADS = N_FOX + N_DIL
DILATED_PATTERNS = ((128, 1), (512, 4), (2048, 16))
N_REL_BUCKETS = 32
REL_MAX_DISTANCE = 2048
RMS_EPS = 1e-6
NEG_INF = -1e30
LANES = 128
VMEM_LIMIT = 56 * 1024 * 1024

ADAM_LR = 0.001
ADAM_B1 = 0.9
ADAM_B2 = 0.999
ADAM_EPS = 1e-08
ADAM_WD = 0.01
ADAM_STEP = 10

MESH = pl.DeviceIdType.MESH


def _params(sem):
    return pltpu.CompilerParams(dimension_semantics=sem, vmem_limit_bytes=VMEM_LIMIT)


def _dot(a, b, ca, cb, precision=None):
    return lax.dot_general(a, b, (((ca,), (cb,)), ((), ())),
                           preferred_element_type=F32, precision=precision)


def _sigmoid(z):
    return 1.0 / (1.0 + jnp.exp(-z))


def _tile(n, want):
    t = min(n, want)
    assert n % t == 0, (n, t)
    return t


def _dep_spec(ngrid):
    return pl.BlockSpec((8, LANES), lambda *_: (0, 0))


def rms_fwd(x, g, name, dep=None):
    T, D = x.shape
    tm = _tile(T, 256)

    def body(x_ref, g_ref, *rest):
        h_ref = rest[-1]
        xv = x_ref[...]
        r = lax.rsqrt(jnp.mean(xv * xv, axis=-1, keepdims=True) + RMS_EPS)
        h_ref[...] = (xv * r * g_ref[...]).astype(BF16)

    in_specs = [pl.BlockSpec((tm, D), lambda i: (i, 0)), pl.BlockSpec((1, D), lambda i: (0, 0))]
    args = [x, g]
    if dep is not None:
        in_specs.append(_dep_spec(1))
        args.append(dep)
    return pl.pallas_call(
        body, name=name, grid=(T // tm,), in_specs=in_specs,
        out_specs=pl.BlockSpec((tm, D), lambda i: (i, 0)),
        out_shape=jax.ShapeDtypeStruct((T, D), BF16),
        compiler_params=_params(("parallel",)),
    )(*args)


def rms_bwd(dh, x, g, dres, name, dep=None):
    T, D = x.shape
    tm = _tile(T, 256)

    def body(dh_ref, x_ref, g_ref, dres_ref, *rest):
        dx_ref, dg_ref = rest[-2], rest[-1]
        i = pl.program_id(0)
        xv = x_ref[...]
        r = lax.rsqrt(jnp.mean(xv * xv, axis=-1, keepdims=True) + RMS_EPS)
        xh = xv * r
        d = dh_ref[...]
        u = d * g_ref[...]
        dx_ref[...] = dres_ref[...] + r * (u - xh * jnp.mean(u * xh, axis=-1, keepdims=True))
        part = jnp.sum(d * xh, axis=0, keepdims=True)

        @pl.when(i == 0)
        def _():
            dg_ref[...] = part

        @pl.when(i > 0)
        def _():
            dg_ref[...] += part

    row = pl.BlockSpec((tm, D), lambda i: (i, 0))
    vec = pl.BlockSpec((1, D), lambda i: (0, 0))
    in_specs = [row, row, vec, row]
    args = [dh, x, g, dres]
    if dep is not None:
        in_specs.append(_dep_spec(1))
        args.append(dep)
    return pl.pallas_call(
        body, name=name, grid=(T // tm,),
        in_specs=in_specs, out_specs=[row, vec],
        out_shape=[jax.ShapeDtypeStruct((T, D), F32), jax.ShapeDtypeStruct((1, D), F32)],
        compiler_params=_params(("arbitrary",)),
    )(*args)


def final_loss_bwd(x, g, target, name):
    T, D = x.shape
    tm = _tile(T, 256)

    def body(x_ref, g_ref, t_ref, loss_ref, dx_ref, dg_ref):
        i = pl.program_id(0)
        xv = x_ref[...]
        gv = g_ref[...]
        r = lax.rsqrt(jnp.mean(xv * xv, axis=-1, keepdims=True) + RMS_EPS)
        xh = xv * r
        e = xh * gv - t_ref[...]
        lpart = 0.5 * jnp.sum(jnp.mean(e * e, axis=-1, keepdims=True), axis=0, keepdims=True)
        lrow = jnp.broadcast_to(lpart, (1, LANES))
        d = e * (1.0 / D)
        u = d * gv
        dx_ref[...] = r * (u - xh * jnp.mean(u * xh, axis=-1, keepdims=True))
        part = jnp.sum(d * xh, axis=0, keepdims=True)

        @pl.when(i == 0)
        def _():
            dg_ref[...] = part
            loss_ref[...] = lrow

        @pl.when(i > 0)
        def _():
            dg_ref[...] += part
            loss_ref[...] += lrow

    row = pl.BlockSpec((tm, D), lambda i: (i, 0))
    vec = pl.BlockSpec((1, D), lambda i: (0, 0))
    return pl.pallas_call(
        body, name=name, grid=(T // tm,),
        in_specs=[row, vec, row],
        out_specs=[pl.BlockSpec((1, LANES), lambda i: (0, 0)), row, vec],
        out_shape=[jax.ShapeDtypeStruct((1, LANES), F32), jax.ShapeDtypeStruct((T, D), F32),
                   jax.ShapeDtypeStruct((1, D), F32)],
        compiler_params=_params(("arbitrary",)),
    )(x, g, target)


def _bf(v, scale=None):
    if scale is not None:
        v = v * scale
    return v.astype(BF16)


def mm_nn(a, b, name, *, tn, out_dtype, tm=512, n_out=None, b_block=None, b_map=None,
          res=None):
    T, K = a.shape
    N = n_out if n_out is not None else b.shape[1]
    tm = _tile(T, tm)
    tn = _tile(N, tn)
    b_block = b_block or (K, tn)
    b_map = b_map or (lambda n, i: (0, n))

    def body(*refs):
        a_ref, b_ref = refs[0], refs[1]
        o_ref = refs[-1]
        acc = _dot(_bf(a_ref[...]), _bf(b_ref[...]), 1, 0)
        if res is not None:
            acc = refs[2][...] + acc
        o_ref[...] = acc.astype(out_dtype)

    in_specs = [pl.BlockSpec((tm, K), lambda n, i: (i, 0)), pl.BlockSpec(b_block, b_map)]
    args = [a, b]
    if res is not None:
        in_specs.append(pl.BlockSpec((tm, tn), lambda n, i: (i, n)))
        args.append(res)
    return pl.pallas_call(
        body, name=name, grid=(N // tn, T // tm), in_specs=in_specs,
        out_specs=pl.BlockSpec((tm, tn), lambda n, i: (i, n)),
        out_shape=jax.ShapeDtypeStruct((T, N), out_dtype),
        compiler_params=_params(("parallel", "parallel")),
    )(*args)


def mm_nn_sum(pairs, name, *, tn, out_dtype, tm=512, dep=None):
    T = pairs[0][0].shape[0]
    N = pairs[0][1].shape[1]
    tm = _tile(T, tm)
    tn = _tile(N, tn)
    npair = len(pairs)

    def body(*refs):
        acc = None
        for q in range(npair):
            part = _dot(_bf(refs[2 * q][...]), _bf(refs[2 * q + 1][...]), 1, 0)
            acc = part if acc is None else acc + part
        refs[-1][...] = acc.astype(out_dtype)

    in_specs, args = [], []
    for a, b in pairs:
        K = a.shape[1]
        in_specs += [pl.BlockSpec((tm, K), lambda n, i: (i, 0)), pl.BlockSpec((K, tn), lambda n, i: (0, n))]
        args += [a, b]
    if dep is not None:
        in_specs.append(_dep_spec(2))
        args.append(dep)
    return pl.pallas_call(
        body, name=name, grid=(N // tn, T // tm), in_specs=in_specs,
        out_specs=pl.BlockSpec((tm, tn), lambda n, i: (i, n)),
        out_shape=jax.ShapeDtypeStruct((T, N), out_dtype),
        compiler_params=_params(("parallel", "parallel")),
    )(*args)


def mm_nt(pairs, name, *, tn, out_dtype, tm=512, dep=None):
    T = pairs[0][0].shape[0]
    N = pairs[0][1].shape[0]
    tm = _tile(T, tm)
    tn = _tile(N, tn)
    npair = len(pairs)

    def body(*refs):
        o_ref = refs[-1]
        acc = None
        for q in range(npair):
            part = _dot(_bf(refs[2 * q][...]), _bf(refs[2 * q + 1][...]), 1, 1)
            acc = part if acc is None else acc + part
        o_ref[...] = acc.astype(out_dtype)

    in_specs, args = [], []
    for a, b in pairs:
        K = a.shape[1]
        in_specs += [pl.BlockSpec((tm, K), lambda n, i: (i, 0)), pl.BlockSpec((tn, K), lambda n, i: (n, 0))]
        args += [a, b]
    if dep is not None:
        in_specs.append(_dep_spec(2))
        args.append(dep)
    return pl.pallas_call(
        body, name=name, grid=(N // tn, T // tm), in_specs=in_specs,
        out_specs=pl.BlockSpec((tm, tn), lambda n, i: (i, n)),
        out_shape=jax.ShapeDtypeStruct((T, N), out_dtype),
        compiler_params=_params(("parallel", "parallel")),
    )(*args)


def mm_tn(a, b, name, *, grid, a_block, a_map, b_block, b_map, o_block, o_map, out_shape,
          b_scale=None, dep=None):
    def body(a_ref, b_ref, *rest):
        rest[-1][...] = _dot(_bf(a_ref[...]), _bf(b_ref[...], b_scale), 0, 0).astype(BF16)

    in_specs = [pl.BlockSpec(a_block, a_map), pl.BlockSpec(b_block, b_map)]
    args = [a, b]
    if dep is not None:
        in_specs.append(_dep_spec(len(grid)))
        args.append(dep)
    return pl.pallas_call(
        body, name=name, grid=grid, in_specs=in_specs,
        out_specs=pl.BlockSpec(o_block, o_map),
        out_shape=jax.ShapeDtypeStruct(out_shape, BF16),
        compiler_params=_params(("parallel",) * len(grid)),
    )(*args)


def mm_tn_plain(a, b, name, *, tm=512, tn=512, b_scale=None):
    T, M = a.shape
    N = b.shape[1]
    tm = _tile(M, tm)
    tn = _tile(N, tn)
    return mm_tn(a, b, name, grid=(M // tm, N // tn),
                 a_block=(T, tm), a_map=lambda m, n: (0, m),
                 b_block=(T, tn), b_map=lambda m, n: (0, n),
                 o_block=(tm, tn), o_map=lambda m, n: (m, n),
                 out_shape=(M, N), b_scale=b_scale)


def ffn_up(h, wgu, gi, ui, nper, name):
    T, D = h.shape
    F8 = wgu.shape[0] // (N_DEV * nper)
    tm = _tile(T, 512)
    nt = T // tm

    def body(h_ref, wg_ref, wu_ref, a_ref, b_ref, s_ref):
        hv = h_ref[...]
        a = _dot(hv, wg_ref[...], 1, 1)
        b = _dot(hv, wu_ref[...], 1, 1)
        a_ref[...] = a.astype(BF16)
        b_ref[...] = b.astype(BF16)
        s_ref[...] = (a * _sigmoid(a) * b).astype(BF16)

    blk = pl.BlockSpec((tm, F8), lambda j, i: (j * nt + i, 0))
    shp = jax.ShapeDtypeStruct((N_DEV * T, F8), BF16)
    return pl.pallas_call(
        body, name=name, grid=(N_DEV, nt),
        in_specs=[pl.BlockSpec((tm, D), lambda j, i: (i, 0)),
                  pl.BlockSpec((F8, D), lambda j, i: (j * nper + gi, 0)),
                  pl.BlockSpec((F8, D), lambda j, i: (j * nper + ui, 0))],
        out_specs=[blk, blk, blk], out_shape=[shp, shp, shp],
        compiler_params=_params(("parallel", "parallel")),
    )(h, wgu, wgu)


def ffn_down(s, wd, di, nper, x, name):
    T, D = x.shape
    F8 = s.shape[1]
    tm = _tile(T, 512)
    nt = T // tm

    def body(s_ref, w_ref, x_ref, o_ref, acc_ref):
        j = pl.program_id(1)
        part = _dot(s_ref[...], w_ref[...], 1, 0)

        @pl.when(j == 0)
        def _():
            acc_ref[...] = part

        @pl.when(j > 0)
        def _():
            acc_ref[...] += part

        @pl.when(j == N_DEV - 1)
        def _():
            o_ref[...] = x_ref[...] + 0.5 * acc_ref[...]

    return pl.pallas_call(
        body, name=name, grid=(nt, N_DEV),
        in_specs=[pl.BlockSpec((tm, F8), lambda i, j: (j * nt + i, 0)),
                  pl.BlockSpec((F8, D), lambda i, j: (j * nper + di, 0)),
                  pl.BlockSpec((tm, D), lambda i, j: (i, 0))],
        out_specs=pl.BlockSpec((tm, D), lambda i, j: (i, 0)),
        out_shape=jax.ShapeDtypeStruct((T, D), F32),
        scratch_shapes=[pltpu.VMEM((tm, D), F32)],
        compiler_params=_params(("parallel", "arbitrary")),
    )(s, wd, x)


def ffn_bwd_act(dx, wd, di, nper_d, a, b, name, dep=None):
    T, D = dx.shape
    F8 = a.shape[1]
    tm = _tile(T, 512)
    nt = T // tm

    def body(dx_ref, w_ref, a_ref, b_ref, *rest):
        da_ref, db_ref = rest[-2], rest[-1]
        ds = _dot(_bf(dx_ref[...], 0.5), w_ref[...], 1, 1)
        av = a_ref[...].astype(F32)
        bv = b_ref[...].astype(F32)
        sg = _sigmoid(av)
        da_ref[...] = (ds * bv * (sg * (1.0 + av * (1.0 - sg)))).astype(BF16)
        db_ref[...] = (ds * (av * sg)).astype(BF16)

    blk = pl.BlockSpec((tm, F8), lambda j, i: (j * nt + i, 0))
    shp = jax.ShapeDtypeStruct((N_DEV * T, F8), BF16)
    in_specs = [pl.BlockSpec((tm, D), lambda j, i: (i, 0)),
                pl.BlockSpec((F8, D), lambda j, i: (j * nper_d + di, 0)), blk, blk]
    args = [dx, wd, a, b]
    if dep is not None:
        in_specs.append(_dep_spec(2))
        args.append(dep)
    return pl.pallas_call(
        body, name=name, grid=(N_DEV, nt), in_specs=in_specs,
        out_specs=[blk, blk], out_shape=[shp, shp],
        compiler_params=_params(("parallel", "parallel")),
    )(*args)


def ffn_bwd_dh(da, db, wgu, gi, ui, nper, D, name, dep=None):
    F8 = da.shape[1]
    T = da.shape[0] // N_DEV
    tm = _tile(T, 512)
    nt = T // tm

    def body(da_ref, db_ref, wg_ref, wu_ref, *rest):
        o_ref, acc_ref = rest[-2], rest[-1]
        j = pl.program_id(1)
        part = _dot(da_ref[...], wg_ref[...], 1, 0) + _dot(db_ref[...], wu_ref[...], 1, 0)

        @pl.when(j == 0)
        def _():
            acc_ref[...] = part

        @pl.when(j > 0)
        def _():
            acc_ref[...] += part

        @pl.when(j == N_DEV - 1)
        def _():
            o_ref[...] = acc_ref[...]

    blk = pl.BlockSpec((tm, F8), lambda i, j: (j * nt + i, 0))
    in_specs = [blk, blk,
                pl.BlockSpec((F8, D), lambda i, j: (j * nper + gi, 0)),
                pl.BlockSpec((F8, D), lambda i, j: (j * nper + ui, 0))]
    args = [da, db, wgu, wgu]
    if dep is not None:
        in_specs.append(_dep_spec(2))
        args.append(dep)
    return pl.pallas_call(
        body, name=name, grid=(nt, N_DEV), in_specs=in_specs,
        out_specs=pl.BlockSpec((tm, D), lambda i, j: (i, 0)),
        out_shape=jax.ShapeDtypeStruct((T, D), F32),
        scratch_shapes=[pltpu.VMEM((tm, D), F32)],
        compiler_params=_params(("parallel", "arbitrary")),
    )(*args)


def ffn_bwd_dw_in(h, dact, name, dep=None):
    T, D = h.shape
    F8 = dact.shape[1]
    tm = _tile(D, 512)
    return mm_tn(dact, h, name, grid=(N_DEV, D // tm),
                 a_block=(T, F8), a_map=lambda j, m: (j, 0),
                 b_block=(T, tm), b_map=lambda j, m: (0, m),
                 o_block=(F8, tm), o_map=lambda j, m: (j, m),
                 out_shape=(N_DEV * F8, D), dep=dep)


def ffn_bwd_dw_down(s, dx, name):
    T, D = dx.shape
    F8 = s.shape[1]
    tn = _tile(D, 512)
    return mm_tn(s, dx, name, grid=(N_DEV, D // tn),
                 a_block=(T, F8), a_map=lambda j, n: (j, 0),
                 b_block=(T, tn), b_map=lambda j, n: (0, n),
                 o_block=(F8, tn), o_map=lambda j, n: (j, n),
                 out_shape=(N_DEV * F8, D), b_scale=0.5)


def _t5_bucket_np(dist):
    max_exact = N_REL_BUCKETS // 2
    d = np.maximum(dist, 1).astype(np.float64)
    large = max_exact + (np.log(d / max_exact) / math.log(REL_MAX_DISTANCE / max_exact)
                         * (N_REL_BUCKETS - max_exact)).astype(np.int64)
    large32 = max_exact + (np.log(d.astype(np.float32) / np.float32(max_exact))
                           / np.float32(math.log(REL_MAX_DISTANCE / max_exact))
                           * np.float32(N_REL_BUCKETS - max_exact)).astype(np.int64)
    assert np.array_equal(large, large32)
    large = np.minimum(large, N_REL_BUCKETS - 1)
    return np.where(dist < max_exact, dist, large)


def _distance_tables(T, tq):
    dist = np.arange(T)
    mult = np.zeros(T, np.int64)
    for window, dilation in DILATED_PATTERNS:
        mult += ((dist % dilation == 0) & (dist // dilation <= window // dilation)).astype(np.int64)
    logm = np.where(mult > 0, np.log(np.maximum(mult, 1)), NEG_INF).astype(np.float32)
    bucket = _t5_bucket_np(dist).astype(np.int32)
    nkb = T // tq
    k = np.arange(nkb)[:, None, None]
    r = np.arange(tq)[None, :, None]
    c = np.arange(tq)[None, None, :]
    delta = k * tq + r - c
    return bucket, logm, delta


def _tile_buckets(T, tq):
    bucket, logm, delta = _distance_tables(T, tq)
    safe = np.maximum(delta, 0)
    bidx = np.where(delta >= 0, bucket[safe], -1).astype(np.int32)
    logm_t = np.where(delta >= 0, logm[safe], NEG_INF).astype(np.float32)
    present = [sorted(set(np.unique(bidx[k]).tolist()) - {-1}) for k in range(T // tq)]
    return bidx, logm_t, present


def bias_tiles(rel_table, T, tq):
    bidx, logm_t, present = _tile_buckets(T, tq)
    nkb = T // tq

    def body(tab_ref, b_ref, lm_ref, o_ref):
        slot = pl.program_id(0)

        @pl.when(slot == 0)
        def _():
            o_ref[...] = jnp.where(b_ref[...] >= 0, 0.0, NEG_INF)

        @pl.when(slot > 0)
        def _():
            for k in range(nkb):
                bi = b_ref[k]
                acc = lm_ref[k]
                for b in present[k]:
                    acc = acc + jnp.where(bi == b, tab_ref[b, slot - 1], 0.0)
                o_ref[k] = acc

    full = pl.BlockSpec((nkb, tq, tq), lambda s: (0, 0, 0))
    return pl.pallas_call(
        body, name="bias_tiles", grid=(1 + N_DIL,),
        in_specs=[pl.BlockSpec(memory_space=pltpu.SMEM), full, full],
        out_specs=pl.BlockSpec((None, nkb, tq, tq), lambda s: (s, 0, 0, 0)),
        out_shape=jax.ShapeDtypeStruct((1 + N_DIL, nkb, tq, tq), F32),
        compiler_params=_params(("parallel",)),
    )(rel_table, jnp.asarray(bidx), jnp.asarray(logm_t))


def fox_gate_fwd(uf, bf, name):
    T = uf.shape[0]
    tb = _tile(T, 512)

    def body(u_ref, b_ref, c_ref, ct_ref):
        lane = lax.broadcasted_iota(jnp.int32, (1, LANES), 1)
        tri = (lax.broadcasted_iota(jnp.int32, (tb, tb), 0)
               >= lax.broadcasted_iota(jnp.int32, (tb, tb), 1)).astype(F32)
        carry = jnp.zeros((1, LANES), F32)
        for blk in range(T // tb):
            z = u_ref[pl.ds(blk * tb, tb), :] + b_ref[...]
            lf = jnp.minimum(z, 0.0) - jnp.log1p(jnp.exp(-jnp.abs(z)))
            lf = jnp.where(lane < N_FOX, lf, 0.0)
            cb = _dot(tri, lf, 1, 0, precision=lax.Precision.HIGHEST) + carry
            c_ref[pl.ds(blk * tb, tb), :] = cb
            ct_ref[:, pl.ds(blk * tb, tb)] = cb.T
            carry = cb[tb - 1:tb, :]

    return pl.pallas_call(
        body, name=name,
        out_shape=[jax.ShapeDtypeStruct((T, LANES), F32), jax.ShapeDtypeStruct((LANES, T), F32)],
        compiler_params=_params(None),
    )(uf, bf)


def fox_gate_bwd(dct, uf, bf, name):
    T = uf.shape[0]
    tb = _tile(T, 512)

    def body(d_ref, u_ref, b_ref, du_ref, db_ref):
        lane = lax.broadcasted_iota(jnp.int32, (1, LANES), 1)
        triu = (lax.broadcasted_iota(jnp.int32, (tb, tb), 0)
                <= lax.broadcasted_iota(jnp.int32, (tb, tb), 1)).astype(F32)
        carry = jnp.zeros((1, LANES), F32)
        dbv = jnp.zeros((1, LANES), F32)
        for blk in reversed(range(T // tb)):
            dc = d_ref[:, pl.ds(blk * tb, tb)].T
            dlf = _dot(triu, dc, 1, 0, precision=lax.Precision.HIGHEST) + carry
            carry = dlf[0:1, :]
            z = u_ref[pl.ds(blk * tb, tb), :] + b_ref[...]
            dz = jnp.where(lane < N_FOX, dlf * (1.0 - _sigmoid(z)), 0.0)
            du_ref[pl.ds(blk * tb, tb), :] = dz
            dbv = dbv + jnp.sum(dz, axis=0, keepdims=True)
        db_ref[...] = dbv

    return pl.pallas_call(
        body, name=name,
        out_shape=[jax.ShapeDtypeStruct((T, LANES), F32), jax.ShapeDtypeStruct((1, LANES), F32)],
        compiler_params=_params(None),
    )(dct, uf, bf)


def _bias_slot(h):
    return jnp.maximum(h - (N_FOX - 1), 0)


def _scores(q_ref, k_ref, c_ref, ct_ref, tb_ref, h, i, tq):
    scale = HEAD_DIM ** -0.5
    n = (i + 1) * tq
    rows = pl.ds(i * tq, tq)
    s = _dot(q_ref[rows, :], k_ref[pl.ds(0, n), :], 1, 1) * scale
    lane = lax.broadcasted_iota(jnp.int32, (1, LANES), 1)
    c_col = jnp.sum(jnp.where(lane == h, c_ref[rows, :], 0.0), axis=1, keepdims=True)
    c_row = ct_ref[pl.ds(h, 1), pl.ds(0, n)]
    bias = jnp.concatenate([tb_ref[i - jb] for jb in range(i + 1)], axis=1)
    return s + (c_col - c_row) + bias


def _attn_specs(T, tq):
    nkb = T // tq
    return [
        pl.BlockSpec((T, HEAD_DIM), lambda h: (0, h)),
        pl.BlockSpec((T, HEAD_DIM), lambda h: (0, N_HEADS + h)),
        pl.BlockSpec((T, HEAD_DIM), lambda h: (0, 2 * N_HEADS + h)),
        pl.BlockSpec((T, LANES), lambda h: (0, 0)),
        pl.BlockSpec((LANES, T), lambda h: (0, 0)),
        pl.BlockSpec((None, nkb, tq, tq), lambda h: (_bias_slot(h), 0, 0, 0)),
    ]


def attention_fwd(qkv, c, ct, tiles, name):
    T = qkv.shape[0]
    tq = tiles.shape[2]

    def body(q_ref, k_ref, v_ref, c_ref, ct_ref, tb_ref, o_ref):
        h = pl.program_id(0)
        for i in range(T // tq):
            s = _scores(q_ref, k_ref, c_ref, ct_ref, tb_ref, h, i, tq)
            p = jnp.exp(s - jnp.max(s, axis=1, keepdims=True))
            l = jnp.sum(p, axis=1, keepdims=True)
            o = _dot((p / l).astype(BF16), v_ref[pl.ds(0, (i + 1) * tq), :], 1, 0)
            o_ref[pl.ds(i * tq, tq), :] = o.astype(BF16)

    return pl.pallas_call(
        body, name=name, grid=(N_HEADS,),
        in_specs=_attn_specs(T, tq),
        out_specs=pl.BlockSpec((T, HEAD_DIM), lambda h: (0, h)),
        out_shape=jax.ShapeDtypeStruct((T, N_HEADS * HEAD_DIM), BF16),
        compiler_params=_params(("parallel",)),
    )(qkv, qkv, qkv, c, ct, tiles)


def attention_bwd(qkv, c, ct, tiles, do, name):
    T = qkv.shape[0]
    tq = tiles.shape[2]
    nkb = T // tq
    scale = HEAD_DIM ** -0.5

    def body(q_ref, k_ref, v_ref, c_ref, ct_ref, tb_ref, do_ref,
             dq_ref, dk_ref, dv_ref, dct_ref, dtb_ref, dk_acc, dv_acc):
        h = pl.program_id(0)
        dk_acc[...] = jnp.zeros_like(dk_acc)
        dv_acc[...] = jnp.zeros_like(dv_acc)
        dct_ref[...] = jnp.zeros_like(dct_ref)
        dtb_ref[...] = jnp.zeros_like(dtb_ref)
        for i in range(nkb):
            rows, keys = pl.ds(i * tq, tq), pl.ds(0, (i + 1) * tq)
            s = _scores(q_ref, k_ref, c_ref, ct_ref, tb_ref, h, i, tq)
            p = jnp.exp(s - jnp.max(s, axis=1, keepdims=True))
            p = p / jnp.sum(p, axis=1, keepdims=True)
            dov = do_ref[rows, :]
            dp = _dot(dov, v_ref[keys, :], 1, 1)
            ds = p * (dp - jnp.sum(p * dp, axis=1, keepdims=True))
            ds_b = ds.astype(BF16)
            dq_ref[rows, :] = (_dot(ds_b, k_ref[keys, :], 1, 0) * scale).astype(BF16)
            dk_acc[keys, :] += _dot(ds_b, q_ref[rows, :], 0, 0) * scale
            dv_acc[keys, :] += _dot(p.astype(BF16), dov, 0, 0)
            dct_ref[:, keys] += -jnp.sum(ds, axis=0, keepdims=True)
            for jb in range(i + 1):
                dtb_ref[i - jb] += ds[:, jb * tq:(jb + 1) * tq]
        dk_ref[...] = dk_acc[...].astype(BF16)
        dv_ref[...] = dv_acc[...].astype(BF16)

    head_cols = jax.ShapeDtypeStruct((T, N_HEADS * HEAD_DIM), BF16)
    col = pl.BlockSpec((T, HEAD_DIM), lambda h: (0, h))
    return pl.pallas_call(
        body, name=name, grid=(N_HEADS,),
        in_specs=_attn_specs(T, tq) + [col],
        out_specs=[col, col, col,
                   pl.BlockSpec((None, 1, T), lambda h: (h, 0, 0)),
                   pl.BlockSpec((None, nkb, tq, tq), lambda h: (_bias_slot(h), 0, 0, 0))],
        out_shape=[head_cols, head_cols, head_cols,
                   jax.ShapeDtypeStruct((N_HEADS, 1, T), F32),
                   jax.ShapeDtypeStruct((1 + N_DIL, nkb, tq, tq), F32)],
        scratch_shapes=[pltpu.VMEM((T, HEAD_DIM), F32), pltpu.VMEM((T, HEAD_DIM), F32)],
        compiler_params=_params(("arbitrary",)),
    )(qkv, qkv, qkv, c, ct, tiles, do)


def rel_table_grad(dtiles, T, name):
    tq = dtiles.shape[2]
    nkb = T // tq
    bidx, _, present = _tile_buckets(T, tq)

    def body(d_ref, b_ref, o_ref):
        lane = lax.broadcasted_iota(jnp.int32, (1, LANES), 1)
        row = jnp.zeros((1, LANES), F32)
        for k in range(nkb):
            d = d_ref[k]
            bi = b_ref[k]
            for b in present[k]:
                v = jnp.sum(jnp.sum(jnp.where(bi == b, d, 0.0), axis=0, keepdims=True),
                            axis=1, keepdims=True)
                row = row + jnp.where(lane == b, v, 0.0)
        o_ref[...] = row

    return pl.pallas_call(
        body, name=name, grid=(N_DIL,),
        in_specs=[pl.BlockSpec((None, nkb, tq, tq), lambda h: (h + 1, 0, 0, 0)),
                  pl.BlockSpec((nkb, tq, tq), lambda h: (0, 0, 0))],
        out_specs=pl.BlockSpec((None, 1, LANES), lambda h: (h, 0, 0)),
        out_shape=jax.ShapeDtypeStruct((N_DIL, 1, LANES), F32),
        compiler_params=_params(("parallel",)),
    )(dtiles, jnp.asarray(bidx))


def ple_combine(x, z, pp, name):
    T, D = x.shape
    tm = _tile(T, 256)

    def body(x_ref, z_ref, p_ref, o_ref):
        o_ref[...] = x_ref[...] + _sigmoid(z_ref[...]) * p_ref[...]

    row = pl.BlockSpec((tm, D), lambda i: (i, 0))
    return pl.pallas_call(
        body, name=name, grid=(T // tm,), in_specs=[row, row, row], out_specs=row,
        out_shape=jax.ShapeDtypeStruct((T, D), F32), compiler_params=_params(("parallel",)),
    )(x, z, pp)


def ple_bwd_elem(dx, z, pp, name):
    T, D = dx.shape
    tm = _tile(T, 256)

    def body(dx_ref, z_ref, p_ref, dz_ref, dp_ref):
        gate = _sigmoid(z_ref[...])
        d = dx_ref[...]
        dz_ref[...] = (d * p_ref[...] * gate * (1.0 - gate)).astype(BF16)
        dp_ref[...] = (d * gate).astype(BF16)

    row = pl.BlockSpec((tm, D), lambda i: (i, 0))
    shp = jax.ShapeDtypeStruct((T, D), BF16)
    return pl.pallas_call(
        body, name=name, grid=(T // tm,), in_specs=[row, row, row], out_specs=[row, row],
        out_shape=[shp, shp], compiler_params=_params(("parallel",)),
    )(dx, z, pp)


def _peer_list():
    x, y, c = lax.axis_index("x"), lax.axis_index("y"), lax.axis_index("c")
    me = 4 * x + 2 * y + c
    peers = []
    for fx in (0, 1):
        for fy in (0, 1):
            for fc in (0, 1):
                if fx or fy or fc:
                    px = 1 - x if fx else x
                    py = 1 - y if fy else y
                    pc = 1 - c if fc else c
                    peers.append(((px, py, pc), 4 * px + 2 * py + pc))
    return me, peers


_HBM = pl.BlockSpec(memory_space=pltpu.HBM)
_SEM = pl.BlockSpec(memory_space=pltpu.SEMAPHORE)
_EFFECT = pltpu.SideEffectType.DATAFLOW_SIDE_EFFECTING
N_PEERS = N_DEV - 1


def _in_hbm(a):
    return pltpu.with_memory_space_constraint(a, pltpu.HBM)


def _exchange_copies(srcs, lands, send_sems, recv_sems, blockwise):
    me, peers = _peer_list()
    sends, recvs = [], []
    for a in range(len(srcs)):
        for k, (dev, idx) in enumerate(peers):
            src = srcs[a].at[idx] if blockwise[a] else srcs[a]
            sends.append(pltpu.make_async_remote_copy(
                src_ref=src, dst_ref=lands[a].at[me], send_sem=send_sems[a].at[k],
                recv_sem=recv_sems[a].at[k], device_id=dev, device_id_type=MESH))
            recvs.append(pltpu.make_async_remote_copy(
                src_ref=src, dst_ref=lands[a].at[idx], send_sem=send_sems[a].at[k],
                recv_sem=recv_sems[a].at[k], device_id=dev, device_id_type=MESH))
    return sends, recvs


def exchange_start(srcs, lands, blockwise, name):
    n = len(srcs)

    def body(*refs):
        src_in, land_in = refs[:n], refs[n:2 * n]
        send_sems, recv_sems = refs[2 * n:3 * n], refs[3 * n:4 * n]
        token = refs[6 * n]
        sends, _ = _exchange_copies(src_in, land_in, send_sems, recv_sems, blockwise)
        for cp in sends:
            cp.start()
        token[...] = jnp.zeros_like(token)

    out_shape = ([pltpu.SemaphoreType.DMA((N_PEERS,))] * (2 * n)
                 + [pltpu.HBM(s.shape, s.dtype) for s in srcs]
                 + [pltpu.HBM(l.shape, l.dtype) for l in lands]
                 + [jax.ShapeDtypeStruct((8, LANES), F32)])
    aliases = {a: 2 * n + a for a in range(2 * n)}
    outs = pl.pallas_call(
        body, name=name, out_shape=out_shape,
        in_specs=[_HBM] * (2 * n),
        out_specs=[_SEM] * (2 * n) + [_HBM] * (2 * n) + [pl.BlockSpec(memory_space=pltpu.VMEM)],
        input_output_aliases=aliases,
        compiler_params=pltpu.CompilerParams(has_side_effects=_EFFECT),
    )(*[_in_hbm(s) for s in srcs], *[_in_hbm(l) for l in lands])
    return (outs[:n], outs[n:2 * n], outs[2 * n:3 * n], outs[3 * n:4 * n], outs[4 * n])


def exchange_wait(send_sems, recv_sems, srcs, lands, blockwise, after, name):
    n = len(srcs)

    def body(*refs):
        src_in, land_in = refs[:n], refs[n:2 * n]
        ss, rs = refs[2 * n:3 * n], refs[3 * n:4 * n]
        sends, recvs = _exchange_copies(src_in, land_in, ss, rs, blockwise)
        for cp in sends:
            cp.wait_send()
        for cp in recvs:
            cp.wait_recv()

    outs = pl.pallas_call(
        body, name=name,
        out_shape=[pltpu.HBM(s.shape, s.dtype) for s in srcs] + [pltpu.HBM(l.shape, l.dtype) for l in lands],
        in_specs=[_HBM] * (2 * n) + [_SEM] * (2 * n) + [pl.BlockSpec(memory_space=pl.ANY)],
        out_specs=[_HBM] * (2 * n),
        input_output_aliases={a: a for a in range(2 * n)},
        compiler_params=pltpu.CompilerParams(has_side_effects=_EFFECT),
    )(*srcs, *lands, *send_sems, *recv_sems, after)
    return outs[n:]


def _landing(own_block, me, slots=N_DEV):
    empty = lax.empty((slots,) + own_block.shape, own_block.dtype)
    return lax.dynamic_update_slice(empty, own_block[None], (me,) + (0,) * own_block.ndim)


N_CHIPS = N_DEV // 2
_CHIP_FLIPS = ((1, 0), (0, 1), (1, 1))


def _xyc():
    return lax.axis_index("x"), lax.axis_index("y"), lax.axis_index("c")


def _other_chips(x, y):
    return [(1 - x if fx else x, 1 - y if fy else y) for fx, fy in _CHIP_FLIPS]


def _remote(src, dst, send_sem, recv_sem, dev):
    return pltpu.make_async_remote_copy(src_ref=src, dst_ref=dst, send_sem=send_sem, recv_sem=recv_sem,
                                        device_id=dev, device_id_type=MESH)


def comm_call(name, bufs, sems_in, sems_out, fn, after=None, want_token=False):
    nb, ni, no = len(bufs), len(sems_in), len(sems_out)
    na = 0 if after is None else 1

    def body(*refs):
        buf_refs = refs[:nb]
        sin = refs[nb:nb + ni]
        sout = refs[nb + ni + na:nb + ni + na + no]
        fn(buf_refs, sin, sout)
        if want_token:
            tok = refs[nb + ni + na + no + nb]
            tok[...] = jnp.zeros_like(tok)

    out_shape = list(sems_out) + [pltpu.HBM(b.shape, b.dtype) for b in bufs]
    out_specs = [_SEM] * no + [_HBM] * nb
    if want_token:
        out_shape.append(jax.ShapeDtypeStruct((8, LANES), F32))
        out_specs.append(pl.BlockSpec(memory_space=pltpu.VMEM))
    args = [_in_hbm(b) for b in bufs] + list(sems_in) + ([after] if na else [])
    outs = pl.pallas_call(
        body, name=name, out_shape=out_shape,
        in_specs=[_HBM] * nb + [_SEM] * ni + [pl.BlockSpec(memory_space=pl.ANY)] * na,
        out_specs=out_specs, input_output_aliases={a: no + a for a in range(nb)},
        compiler_params=pltpu.CompilerParams(has_side_effects=_EFFECT),
    )(*args)
    return list(outs[:no]), list(outs[no:no + nb]), (outs[no + nb] if want_token else None)


def _dma_sems(*sizes):
    return [pltpu.SemaphoreType.DMA((s,)) for s in sizes]


def gather_start(srcs, lands, name):
    n = len(srcs)

    def fn(bufs, sin, sout):
        x, y, c = _xyc()
        me = 4 * x + 2 * y + c
        for a in range(n):
            src, land = bufs[a], bufs[n + a]
            send, recv_d, recv_i = sout[3 * a:3 * a + 3]
            _remote(src, land.at[me], send.at[0], recv_d.at[0], (x, y, 1 - c)).start()
            for k, (px, py) in enumerate(_other_chips(x, y)):
                _remote(src, land.at[me], send.at[1 + k], recv_i.at[k], (px, py, c)).start()

    return comm_call(name, list(srcs) + list(lands), [], _dma_sems(4, 1, 3) * n, fn, want_token=True)


def gather_forward(srcs, lands, recv_i, after, name):
    n = len(srcs)

    def fn(bufs, sin, sout):
        x, y, c = _xyc()
        for a in range(n):
            src, land = bufs[a], bufs[n + a]
            f_send, f_recv = sout[2 * a:2 * a + 2]
            for k, (px, py) in enumerate(_other_chips(x, y)):
                blk = land.at[4 * px + 2 * py + c]
                _remote(src, blk, f_send.at[k], sin[a].at[k], (px, py, c)).wait_recv()
                _remote(blk, blk, f_send.at[k], f_recv.at[k], (x, y, 1 - c)).start()

    sems, bufs, _ = comm_call(name, list(srcs) + list(lands), recv_i, _dma_sems(3, 3) * n, fn, after=after)
    return sems, bufs


def gather_wait(srcs, lands, send, recv_d, f_send, f_recv, after, name):
    n = len(srcs)

    def fn(bufs, sin, sout):
        x, y, c = _xyc()
        sib = (x, y, 1 - c)
        for a in range(n):
            src, land = bufs[a], bufs[n + a]
            s_send, s_recv_d, s_fsend, s_frecv = sin[4 * a:4 * a + 4]
            sib_blk = land.at[4 * x + 2 * y + 1 - c]
            for k in range(4):
                _remote(src, sib_blk, s_send.at[k], s_recv_d.at[0], sib).wait_send()
            _remote(src, sib_blk, s_send.at[0], s_recv_d.at[0], sib).wait_recv()
            for k, (px, py) in enumerate(_other_chips(x, y)):
                cp = _remote(src, land.at[4 * px + 2 * py + 1 - c], s_fsend.at[k], s_frecv.at[k], sib)
                cp.wait_send()
                cp.wait_recv()

    sems_in = []
    for a in range(n):
        sems_in += [send[a], recv_d[a], f_send[a], f_recv[a]]
    _, bufs, _ = comm_call(name, list(srcs) + list(lands), sems_in, [], fn, after=after)
    return bufs[n:]


def scatter_pair_start(src4s, lands, name, after=None):
    n = len(src4s)

    def fn(bufs, sin, sout):
        x, y, c = _xyc()
        for a in range(n):
            _remote(bufs[a].at[:, 1 - c], bufs[n + a], sout[2 * a].at[0], sout[2 * a + 1].at[0],
                    (x, y, 1 - c)).start()

    return comm_call(name, list(src4s) + list(lands), [], _dma_sems(1, 1) * n, fn, after=after, want_token=True)


def scatter_pair_wait(src4s, lands, sems, after, name):
    n = len(src4s)

    def fn(bufs, sin, sout):
        x, y, c = _xyc()
        for a in range(n):
            cp = _remote(bufs[a].at[:, 1 - c], bufs[n + a], sin[2 * a].at[0], sin[2 * a + 1].at[0], (x, y, 1 - c))
            cp.wait_send()
            cp.wait_recv()

    _, bufs, _ = comm_call(name, list(src4s) + list(lands), sems, [], fn, after=after)
    return bufs[:n], bufs[n:]


def _row_tile(R):
    for cand in (128, 64, 32, 16):
        if R % cand == 0:
            return cand
    return R


def chip_sum(src4, land, c, name):
    _, _, R, C = src4.shape
    tr = R

    def body(c_ref, a_ref, b_ref, o_ref):
        o_ref[...] = (a_ref[...].astype(F32) + b_ref[...].astype(F32)).astype(BF16)

    grid_spec = pltpu.PrefetchScalarGridSpec(
        num_scalar_prefetch=1, grid=(N_CHIPS, R // tr),
        in_specs=[pl.BlockSpec((None, None, tr, C), lambda q, i, cr: (q, cr[0], i, 0)),
                  pl.BlockSpec((None, tr, C), lambda q, i, cr: (q, i, 0))],
        out_specs=pl.BlockSpec((None, tr, C), lambda q, i, cr: (q, i, 0)))
    return pl.pallas_call(
        body, name=name, grid_spec=grid_spec,
        out_shape=jax.ShapeDtypeStruct((N_CHIPS, R, C), BF16),
        compiler_params=_params(("parallel", "parallel")),
    )(c.reshape(1).astype(jnp.int32), src4, land)


def scatter_chip_start(sums, lands, name):
    n = len(sums)

    def fn(bufs, sin, sout):
        x, y, c = _xyc()
        for a in range(n):
            for k, (px, py) in enumerate(_other_chips(x, y)):
                _remote(bufs[a].at[2 * px + py], bufs[n + a].at[2 * x + y], sout[2 * a].at[k], sout[2 * a + 1].at[k],
                        (px, py, c)).start()

    return comm_call(name, list(sums) + list(lands), [], _dma_sems(3, 3) * n, fn, want_token=True)


def scatter_chip_wait(sums, lands, sems, after, name):
    n = len(sums)

    def fn(bufs, sin, sout):
        x, y, c = _xyc()
        for a in range(n):
            for k, (px, py) in enumerate(_other_chips(x, y)):
                cp = _remote(bufs[a].at[2 * px + py], bufs[n + a].at[2 * px + py], sin[2 * a].at[k],
                             sin[2 * a + 1].at[k], (px, py, c))
                cp.wait_send()
                cp.wait_recv()

    _, bufs, _ = comm_call(name, list(sums) + list(lands), sems, [], fn, after=after)
    return bufs[:n], bufs[n:]


def _adamw_math(w, g, m, v):
    m = ADAM_B1 * m + (1.0 - ADAM_B1) * g
    v = ADAM_B2 * v + (1.0 - ADAM_B2) * (g * g)
    m_hat = m / (1.0 - ADAM_B1 ** ADAM_STEP)
    v_hat = v / (1.0 - ADAM_B2 ** ADAM_STEP)
    delta = -ADAM_LR * (m_hat / (jnp.sqrt(v_hat) + ADAM_EPS) + ADAM_WD * w)
    return delta, m, v


def adamw_sharded(parts, w, m, v, name):
    R, C = w.shape
    S = parts.shape[0]
    tr = _row_tile(R)

    def body(p_ref, w_ref, m_ref, v_ref, g_ref, d_ref, nm_ref, nv_ref):
        g = p_ref[0].astype(F32)
        for s in range(1, S):
            g = g + p_ref[s].astype(F32)
        delta, nm, nv = _adamw_math(w_ref[...], g, m_ref[...], v_ref[...])
        g_ref[...] = g
        d_ref[...] = delta
        nm_ref[...] = nm
        nv_ref[...] = nv

    row = pl.BlockSpec((tr, C), lambda i: (i, 0))
    shp = jax.ShapeDtypeStruct((R, C), F32)
    return pl.pallas_call(
        body, name=name, grid=(R // tr,),
        in_specs=[pl.BlockSpec((S, tr, C), lambda i: (0, i, 0)), row, row, row],
        out_specs=[row, row, row, row], out_shape=[shp, shp, shp, shp],
        compiler_params=_params(("parallel",)),
    )(parts, w, m, v)


def adamw_small(parts, w, m, v, name):
    R, C = w.shape

    def body(p_ref, w_ref, m_ref, v_ref, g_ref, d_ref, nm_ref, nv_ref):
        g = p_ref[0]
        for s in range(1, N_DEV):
            g = g + p_ref[s]
        delta, nm, nv = _adamw_math(w_ref[...], g, m_ref[...], v_ref[...])
        g_ref[...] = g
        d_ref[...] = delta
        nm_ref[...] = nm
        nv_ref[...] = nv

    shp = jax.ShapeDtypeStruct((R, C), F32)
    return pl.pallas_call(
        body, name=name, out_shape=[shp, shp, shp, shp], compiler_params=_params(None),
    )(parts, w, m, v)


_ROW_NORM_FFN1, _ROW_NORM_MIX, _ROW_NORM_FFN2, _ROW_NORM_PLE, _ROW_NORM_FINAL = 0, 1, 2, 3, 4
_ROW_B_F, _ROW_REL, _ROW_LOSS, _SMALL_ROWS = 5, 6, 7, 8


def _pack_small(D, norm_ffn1, norm_mix, norm_ffn2, norm_ple, norm_final, b_f, rel_table):
    def row(v):
        v = v.reshape(1, -1)
        return jnp.pad(v, ((0, 0), (0, D - v.shape[1])))
    return jnp.concatenate([row(norm_ffn1), row(norm_mix), row(norm_ffn2), row(norm_ple),
                            row(norm_final), row(b_f), row(rel_table),
                            jnp.zeros((1, D), F32)], axis=0)


def _unpack_small(a, shapes):
    return {"norm_ffn1": a[_ROW_NORM_FFN1].reshape(shapes["norm_ffn1"]),
            "norm_mix": a[_ROW_NORM_MIX].reshape(shapes["norm_mix"]),
            "b_f": a[_ROW_B_F, :N_FOX].reshape(shapes["b_f"]),
            "norm_ffn2": a[_ROW_NORM_FFN2].reshape(shapes["norm_ffn2"]),
            "norm_ple": a[_ROW_NORM_PLE].reshape(shapes["norm_ple"]),
            "rel_table": a[_ROW_REL, :N_REL_BUCKETS * N_DIL].reshape(shapes["rel_table"]),
            "norm_final": a[_ROW_NORM_FINAL].reshape(shapes["norm_final"])}


def local_step(x, p, tgt, g_ffn1, g_mix, g_ffn2, g_ple, g_final, b_f, rel_table,
               forward, weights, emit, emit2, first_dep):
    T, D = x.shape
    P = p.shape[1]
    CW = D // N_DEV
    tq = _tile(T, 256)

    h1 = rms_fwd(x, g_ffn1, "rms_ffn1", dep=first_dep)
    tiles = bias_tiles(rel_table, T, tq)
    forward("ffn1_gu", tiles)
    wgu1, = weights("ffn1_gu", h1)
    a1, b1, s1 = ffn_up(h1, wgu1, 0, 1, 2, "ffn1_up")
    forward("ffn1_d", s1)
    wd1, = weights("ffn1_d", s1)
    x1 = ffn_down(s1, wd1, 0, 1, x, "ffn1_down")

    h2 = rms_fwd(x1, g_mix, "rms_mix")
    forward("mix", h2)
    w3, wf, wo = weights("mix", h2)
    qkv = mm_nt([(h2, w3)], "mix_qkv", tn=768, out_dtype=BF16)
    uf = mm_nt([(h2, wf)], "mix_forget", tn=LANES, out_dtype=F32)
    bfp = jnp.pad(b_f.reshape(1, N_FOX), ((0, 0), (0, LANES - N_FOX)))
    c, ct = fox_gate_fwd(uf, bfp, "fox_gate")
    cat = attention_fwd(qkv, c, ct, tiles, "attention")
    forward("ffn2", cat)
    x2 = mm_nn(cat, wo, "mix_out", tn=512, out_dtype=F32, res=x1)

    h3 = rms_fwd(x2, g_ffn2, "rms_ffn2")
    wgu2, wd2 = weights("ffn2", h3)
    a2, b2, s2 = ffn_up(h3, wgu2, 0, 1, 2, "ffn2_up")
    forward("ple", s2)
    x3 = ffn_down(s2, wd2, 0, 1, x2, "ffn2_down")

    h4 = rms_fwd(x3, g_ple, "rms_ple")
    wpg, wpp = weights("ple", h4)
    z = mm_nn(h4, wpg, "ple_gate", tn=512, out_dtype=F32)
    pp = mm_nn(p, wpp, "ple_proj", tn=CW, out_dtype=F32, n_out=D,
               b_block=(P, CW), b_map=lambda n, i: (n, 0))
    x4 = ple_combine(x3, z, pp, "ple_combine")
    loss_row, dx4, dg_final = final_loss_bwd(x4, g_final, tgt, "final_loss")

    grads = {}
    dz, dpp = ple_bwd_elem(dx4, z, pp, "ple_bwd_elem")
    grads["w_ple_proj"] = mm_tn(p, dpp, "ple_proj_dw", grid=(N_DEV,),
                                a_block=(T, P), a_map=lambda n: (0, 0),
                                b_block=(T, CW), b_map=lambda n: (0, n),
                                o_block=(P, CW), o_map=lambda n: (n, 0),
                                out_shape=(N_DEV * P, CW))
    grads["w_ple_gate"] = mm_tn_plain(h4, dz, "ple_gate_dw")
    tok = emit("ple", grads)
    dh4 = mm_nt([(dz, wpg)], "ple_gate_dh", tn=512, out_dtype=F32, dep=tok)
    tok = emit2("ple", dh4)
    dx3, dg_ple = rms_bwd(dh4, x3, g_ple, dx4, "rms_ple_bwd", dep=tok)

    da2, db2 = ffn_bwd_act(dx3, wd2, 0, 1, a2, b2, "ffn2_bwd_act")
    grads["ffn2_w_down"] = ffn_bwd_dw_down(s2, dx3, "ffn2_down_dw")
    grads["ffn2_w_gate"] = ffn_bwd_dw_in(h3, da2, "ffn2_gate_dw")
    grads["ffn2_w_up"] = ffn_bwd_dw_in(h3, db2, "ffn2_up_dw")
    tok = emit("ffn2", grads)
    dh3 = ffn_bwd_dh(da2, db2, wgu2, 0, 1, 2, D, "ffn2_bwd_dh", dep=tok)
    tok = emit2("ffn2", dh3)
    dx2, dg_ffn2 = rms_bwd(dh3, x2, g_ffn2, dx3, "rms_ffn2_bwd", dep=tok)

    dcat = mm_nt([(dx2, wo)], "mix_out_dh", tn=512, out_dtype=BF16)
    grads["w_o"] = mm_tn_plain(cat, dx2, "mix_out_dw")
    dq, dk, dv, dct, dtiles = attention_bwd(qkv, c, ct, tiles, dcat, "attention_bwd")
    dctp = jnp.pad(dct[:, 0, :], ((0, LANES - N_HEADS), (0, 0)))
    duf, dbf = fox_gate_bwd(dctp, uf, bfp, "fox_gate_bwd")
    drel = rel_table_grad(dtiles, T, "rel_table_grad")[:, 0, :N_REL_BUCKETS].T
    du3 = jnp.concatenate([dq, dk, dv], axis=1)
    grads["w3"] = mm_tn_plain(du3, h2, "mix_qkv_dw", tm=768)
    grads["wf"] = mm_tn_plain(duf, h2, "mix_forget_dw", tm=LANES)
    tok = emit("mix", grads)
    dh2 = mm_nn_sum([(du3, w3), (duf, wf)], "mix_in_dh", tn=512, out_dtype=F32, dep=tok)
    tok = emit2("mix", dh2)
    dx1, dg_mix = rms_bwd(dh2, x1, g_mix, dx2, "rms_mix_bwd", dep=tok)

    da1, db1 = ffn_bwd_act(dx1, wd1, 0, 1, a1, b1, "ffn1_bwd_act")
    grads["ffn1_w_down"] = ffn_bwd_dw_down(s1, dx1, "ffn1_down_dw")
    tok = emit("ffn1_d", grads)
    grads["ffn1_w_gate"] = ffn_bwd_dw_in(h1, da1, "ffn1_gate_dw", dep=tok)
    tok = emit2("ffn1_d", grads["ffn1_w_gate"])
    tok = emit("ffn1_g", grads, after=tok)
    grads["ffn1_w_up"] = ffn_bwd_dw_in(h1, db1, "ffn1_up_dw", dep=tok)
    tok = emit2("ffn1_g", grads["ffn1_w_up"])
    tok = emit("ffn1_u", grads, after=tok)
    dh1 = ffn_bwd_dh(da1, db1, wgu1, 0, 1, 2, D, "ffn1_bwd_dh", dep=tok)
    tok = emit2("ffn1_u", dh1)
    dx0, dg_ffn1 = rms_bwd(dh1, x, g_ffn1, dx1, "rms_ffn1_bwd", dep=tok)

    small = _pack_small(D, dg_ffn1, dg_mix, dg_ffn2, dg_ple, dg_final, dbf[:, :N_FOX], drel)
    small = small.at[_ROW_LOSS, :LANES].set(loss_row[0])
    grads["small"] = small
    emit("small", grads)
    return dx0


def _split_w_in(w_in_t):
    df, dd = N_FOX * HEAD_DIM, N_DIL * HEAD_DIM
    o = np.cumsum([0, df, df, df, N_FOX, dd, dd, dd]).tolist()
    qa, ka, va, f, qb, kb, vb = [w_in_t[o[i]:o[i + 1]] for i in range(7)]
    return jnp.concatenate([qa, qb, ka, kb, va, vb], axis=0), f


def _join_w_in(d3, dfg):
    df, dd = N_FOX * HEAD_DIM, N_DIL * HEAD_DIM
    o = np.cumsum([0, df, dd, df, dd, df, dd]).tolist()
    qa, qb, ka, kb, va, vb = [d3[o[i]:o[i + 1]] for i in range(6)]
    return jnp.concatenate([qa, ka, va, dfg, qb, kb, vb], axis=0)


def rows_to_bf16(a3, name):
    R, _, C = a3.shape
    tc = _tile(C, 512)

    def body(a_ref, o_ref):
        o_ref[...] = a_ref[...].astype(BF16)

    return pl.pallas_call(
        body, name=name, grid=(C // tc,),
        in_specs=[pl.BlockSpec((R, None, tc), lambda n: (0, 0, n))],
        out_specs=pl.BlockSpec((R, tc), lambda n: (0, n)),
        out_shape=jax.ShapeDtypeStruct((R, C), BF16),
        compiler_params=_params(("parallel",)),
    )(a3)


def adamw_rows3d(parts, w3, m3, v3, name):
    R, _, C = w3.shape
    S = parts.shape[0]
    tc = _tile(C, 256)

    def body(p_ref, w_ref, m_ref, v_ref, g_ref, d_ref, nm_ref, nv_ref):
        g = p_ref[0].astype(F32)
        for s in range(1, S):
            g = g + p_ref[s].astype(F32)
        delta, nm, nv = _adamw_math(w_ref[...], g, m_ref[...], v_ref[...])
        g_ref[...] = g
        d_ref[...] = delta
        nm_ref[...] = nm
        nv_ref[...] = nv

    col = pl.BlockSpec((R, None, tc), lambda n: (0, 0, n))
    shp = jax.ShapeDtypeStruct((R, 1, C), F32)
    return pl.pallas_call(
        body, name=name, grid=(C // tc,),
        in_specs=[pl.BlockSpec((S, R, tc), lambda n: (0, 0, n)), col, col, col],
        out_specs=[col, col, col, col], out_shape=[shp, shp, shp, shp],
        compiler_params=_params(("parallel",)),
    )(parts, w3, m3, v3)


def kernel(x, p, norm_ffn1, ffn1_w_gate, ffn1_w_up, ffn1_w_down, norm_mix, w_in, b_f, w_o, norm_ffn2, ffn2_w_gate, ffn2_w_up, ffn2_w_down, norm_ple, w_ple_gate, w_ple_proj, rel_table, norm_final, loss_target, m_norm_ffn1, m_ffn1_w_gate, m_ffn1_w_up, m_ffn1_w_down, m_norm_mix, m_w_in, m_b_f, m_w_o, m_norm_ffn2, m_ffn2_w_gate, m_ffn2_w_up, m_ffn2_w_down, m_norm_ple, m_w_ple_gate, m_w_ple_proj, m_rel_table, m_norm_final, v_norm_ffn1, v_ffn1_w_gate, v_ffn1_w_up, v_ffn1_w_down, v_norm_mix, v_w_in, v_b_f, v_w_o, v_norm_ffn2, v_ffn2_w_gate, v_ffn2_w_up, v_ffn2_w_down, v_norm_ple, v_w_ple_gate, v_w_ple_proj, v_rel_table, v_norm_final):
    names = ["norm_ffn1", "ffn1_w_gate", "ffn1_w_up", "ffn1_w_down", "norm_mix", "w_in", "b_f", "w_o",
             "norm_ffn2", "ffn2_w_gate", "ffn2_w_up", "ffn2_w_down", "norm_ple", "w_ple_gate",
             "w_ple_proj", "rel_table", "norm_final"]
    w = dict(zip(names, [norm_ffn1, ffn1_w_gate, ffn1_w_up, ffn1_w_down, norm_mix, w_in, b_f, w_o,
                         norm_ffn2, ffn2_w_gate, ffn2_w_up, ffn2_w_down, norm_ple, w_ple_gate,
                         w_ple_proj, rel_table, norm_final]))
    m = dict(zip(names, [m_norm_ffn1, m_ffn1_w_gate, m_ffn1_w_up, m_ffn1_w_down, m_norm_mix, m_w_in,
                         m_b_f, m_w_o, m_norm_ffn2, m_ffn2_w_gate, m_ffn2_w_up, m_ffn2_w_down,
                         m_norm_ple, m_w_ple_gate, m_w_ple_proj, m_rel_table, m_norm_final]))
    v = dict(zip(names, [v_norm_ffn1, v_ffn1_w_gate, v_ffn1_w_up, v_ffn1_w_down, v_norm_mix, v_w_in,
                         v_b_f, v_w_o, v_norm_ffn2, v_ffn2_w_gate, v_ffn2_w_up, v_ffn2_w_down,
                         v_norm_ple, v_w_ple_gate, v_w_ple_proj, v_rel_table, v_norm_final]))
    sharded = ["ffn1_w_gate", "ffn1_w_up", "ffn1_w_down", "w_in", "w_o", "ffn2_w_gate", "ffn2_w_up",
               "ffn2_w_down", "w_ple_gate", "w_ple_proj"]
    small_names = [n for n in names if n not in sharded]

    xs, ps, tgt = x[0], p[0, 0], loss_target[0]
    T, D = xs.shape
    transposed = ("ffn1_w_gate", "ffn1_w_up", "ffn2_w_gate", "ffn2_w_up")

    def view(t, n):
        if n in transposed:
            return t[n][0].T
        if n == "w_in":
            return jnp.transpose(t[n], (2, 0, 1))
        return t[n][0]

    def unview(a, n):
        if n in transposed:
            return a.T.reshape(w[n].shape)
        if n == "w_in":
            return jnp.transpose(a, (1, 2, 0))
        return a.reshape(w[n].shape)

    sh = {n: view(w, n) for n in sharded}
    m_sh = {n: view(m, n) for n in sharded}
    v_sh = {n: view(v, n) for n in sharded}
    F8 = sh["ffn1_w_down"].shape[0]
    WIN8 = sh["w_in"].shape[0]
    w_in_bf = rows_to_bf16(sh["w_in"], "w_in_bf16")

    me = 4 * lax.axis_index("x") + 2 * lax.axis_index("y") + lax.axis_index("c")

    cat0 = lambda ns: jnp.concatenate([sh[n] for n in ns], axis=0).astype(BF16)
    gather_groups = {
        "ffn1_gu": [cat0(["ffn1_w_gate", "ffn1_w_up"])],
        "ffn1_d": [sh["ffn1_w_down"].astype(BF16)],
        "mix": [w_in_bf, sh["w_o"].astype(BF16)],
        "ffn2": [cat0(["ffn2_w_gate", "ffn2_w_up"]), sh["ffn2_w_down"].astype(BF16)],
        "ple": [sh["w_ple_gate"].astype(BF16), sh["w_ple_proj"].astype(BF16)],
    }
    order = ["ffn1_gu", "ffn1_d", "mix", "ffn2", "ple"]
    g_srcs = [s for grp in order for s in gather_groups[grp]]
    g_lands = [_landing(s, me) for s in g_srcs]
    n_g = len(g_srcs)
    g_sems, g_bufs, g_token = gather_start(g_srcs, g_lands, "gather_start")
    g_srcs, g_lands = g_bufs[:n_g], g_bufs[n_g:]
    g_send, g_recv_d, g_recv_i = g_sems[0::3], g_sems[1::3], g_sems[2::3]
    first = np.cumsum([0] + [len(gather_groups[grp]) for grp in order]).tolist()
    passed = {}

    def arrays_of(group):
        k = order.index(group)
        return slice(first[k], first[k + 1])

    def forward(group, after):
        sl = arrays_of(group)
        f_sems, bufs = gather_forward(g_srcs[sl], g_lands[sl], g_recv_i[sl], after, "gather_forward_" + group)
        k = len(bufs) // 2
        passed[group] = (f_sems[0::2], f_sems[1::2], bufs[:k], bufs[k:])

    def weights(group, after):
        sl = arrays_of(group)
        f_send, f_recv, srcs, lands = passed[group]
        got = gather_wait(srcs, lands, g_send[sl], g_recv_d[sl], f_send, f_recv, after, "gather_wait_" + group)
        if group == "ffn1_gu":
            return (got[0].reshape(N_DEV * 2 * F8, D),)
        if group == "ffn1_d":
            return (got[0].reshape(N_DEV * F8, D),)
        a0, a1 = got
        if group == "ffn2":
            return a0.reshape(N_DEV * 2 * F8, D), a1.reshape(N_DEV * F8, D)
        if group == "ple":
            return a0.reshape(-1, D), a1.reshape(-1, a1.shape[2])
        w3, wf8 = _split_w_in(a0.reshape(N_DEV * WIN8, D))
        return w3, jnp.pad(wf8, ((0, LANES - N_FOX), (0, 0))), a1.reshape(-1, D)

    scatter_groups = {
        "ple": ["w_ple_gate", "w_ple_proj"],
        "ffn2": ["ffn2_w_gate", "ffn2_w_up", "ffn2_w_down"],
        "mix": ["w_in", "w_o"],
        "ffn1_d": ["ffn1_w_down"],
        "ffn1_g": ["ffn1_w_gate"],
        "ffn1_u": ["ffn1_w_up"],
    }
    x_i, y_i, c_i = _xyc()
    my_chip = 2 * x_i + y_i
    pair_stage, chip_stage, small_stage = {}, {}, {}

    def emit(group, grads, after=None):
        if group == "small":
            src = grads["small"]
            ss, rs, srcs, lands, token = exchange_start([src], [_landing(src, me)], [False], "scatter_start_small")
            small_stage["small"] = (ss, rs, srcs, lands)
            return token
        src4s = []
        for n in scatter_groups[group]:
            if n == "w_in":
                full = _join_w_in(grads["w3"], grads["wf"][:N_FOX])
                src4s.append(full.reshape(N_CHIPS, 2, WIN8, D))
            else:
                src4s.append(grads[n].reshape((N_CHIPS, 2) + sh[n].shape))
        lands = [lax.empty((N_CHIPS,) + s.shape[2:], BF16) for s in src4s]
        sems, bufs, token = scatter_pair_start(src4s, lands, "scatter_pair_start_" + group, after=after)
        k = len(src4s)
        pair_stage[group] = (sems, bufs[:k], bufs[k:])
        return token

    def emit2(group, after):
        sems, src4s, lands = pair_stage[group]
        src4s, lands = scatter_pair_wait(src4s, lands, sems, after, "scatter_pair_wait_" + group)
        sums = [chip_sum(s4, la, c_i, "chip_sum_" + n)
                for s4, la, n in zip(src4s, lands, scatter_groups[group])]
        chip_lands = [_landing(lax.dynamic_index_in_dim(s, my_chip, 0, keepdims=False), my_chip, slots=N_CHIPS)
                      for s in sums]
        sems, bufs, token = scatter_chip_start(sums, chip_lands, "scatter_chip_start_" + group)
        k = len(sums)
        chip_stage[group] = (sems, bufs[:k], bufs[k:])
        return token

    dx0 = local_step(
        xs, ps, tgt, w["norm_ffn1"], w["norm_mix"], w["norm_ffn2"], w["norm_ple"],
        w["norm_final"].reshape(1, D), w["b_f"], w["rel_table"], forward, weights, emit, emit2, g_token)

    res = {}
    after = dx0
    for group in ["ple", "ffn2", "mix", "ffn1_d", "ffn1_g", "ffn1_u"]:
        sems, sums, chip_lands = chip_stage[group]
        _, parts = scatter_chip_wait(sums, chip_lands, sems, after, "scatter_chip_wait_" + group)
        for n, part in zip(scatter_groups[group], parts):
            update = adamw_rows3d if n == "w_in" else adamw_sharded
            g, d, nm, nv = update(part, sh[n], m_sh[n], v_sh[n], "adamw_" + n)
            res[n] = tuple(unview(a, n) for a in (g, d, nm, nv))
            after = g
    ss, rs, srcs, lands = small_stage["small"]
    small_parts, = exchange_wait(ss, rs, srcs, lands, [False], after, "scatter_wait_small")
    pack = lambda t: _pack_small(D, t["norm_ffn1"], t["norm_mix"], t["norm_ffn2"], t["norm_ple"],
                                 t["norm_final"], t["b_f"], t["rel_table"])
    gs, ds, ms, vs = adamw_small(small_parts, pack(w), pack(m), pack(v), "adamw_small")
    shapes = {n: w[n].shape for n in small_names}
    unpacked = [_unpack_small(a, shapes) for a in (gs, ds, ms, vs)]
    for n in small_names:
        res[n] = tuple(u[n] for u in unpacked)
    loss = gs[_ROW_LOSS, 0]

    out = [loss, dx0.reshape(x.shape)]
    for k in range(4):
        out += [res[n][k] for n in names]
    return tuple(out)
```

```python
import functools
import math

import numpy as np
import jax
import jax.numpy as jnp
from jax import lax
from jax.experimental import pallas as pl
from jax.experimental.pallas import tpu as pltpu

F32 = jnp.float32
BF16 = jnp.bfloat16

N_DEV = 8
HEAD_DIM = 128
N_FOX = 8
N_DIL = 8
N_HEADS = N_FOX + N_DIL
DILATED_PATTERNS = ((128, 1), (512, 4), (2048, 16))
N_REL_BUCKETS = 32
REL_MAX_DISTANCE = 2048
RMS_EPS = 1e-6
NEG_INF = -1e30
LANES = 128
VMEM_LIMIT = 56 * 1024 * 1024

ADAM_LR = 0.001
ADAM_B1 = 0.9
ADAM_B2 = 0.999
ADAM_EPS = 1e-08
ADAM_WD = 0.01
ADAM_STEP = 10

MESH = pl.DeviceIdType.MESH


def _params(sem):
    return pltpu.CompilerParams(dimension_semantics=sem, vmem_limit_bytes=VMEM_LIMIT)


def _dot(a, b, ca, cb, precision=None):
    return lax.dot_general(a, b, (((ca,), (cb,)), ((), ())),
                           preferred_element_type=F32, precision=precision)


def _sigmoid(z):
    return 1.0 / (1.0 + jnp.exp(-z))


def _tile(n, want):
    t = min(n, want)
    assert n % t == 0, (n, t)
    return t


def _dep_spec(ngrid):
    return pl.BlockSpec((8, LANES), lambda *_: (0, 0))


def rms_fwd(x, g, name, dep=None):
    T, D = x.shape
    tm = _tile(T, 256)

    def body(x_ref, g_ref, *rest):
        h_ref = rest[-1]
        xv = x_ref[...]
        r = lax.rsqrt(jnp.mean(xv * xv, axis=-1, keepdims=True) + RMS_EPS)
        h_ref[...] = (xv * r * g_ref[...]).astype(BF16)

    in_specs = [pl.BlockSpec((tm, D), lambda i: (i, 0)), pl.BlockSpec((1, D), lambda i: (0, 0))]
    args = [x, g]
    if dep is not None:
        in_specs.append(_dep_spec(1))
        args.append(dep)
    return pl.pallas_call(
        body, name=name, grid=(T // tm,), in_specs=in_specs,
        out_specs=pl.BlockSpec((tm, D), lambda i: (i, 0)),
        out_shape=jax.ShapeDtypeStruct((T, D), BF16),
        compiler_params=_params(("parallel",)),
    )(*args)


def rms_bwd(dh, x, g, dres, name, dep=None):
    T, D = x.shape
    tm = _tile(T, 256)

    def body(dh_ref, x_ref, g_ref, dres_ref, *rest):
        dx_ref, dg_ref = rest[-2], rest[-1]
        i = pl.program_id(0)
        xv = x_ref[...]
        r = lax.rsqrt(jnp.mean(xv * xv, axis=-1, keepdims=True) + RMS_EPS)
        xh = xv * r
        d = dh_ref[...]
        u = d * g_ref[...]
        dx_ref[...] = dres_ref[...] + r * (u - xh * jnp.mean(u * xh, axis=-1, keepdims=True))
        part = jnp.sum(d * xh, axis=0, keepdims=True)

        @pl.when(i == 0)
        def _():
            dg_ref[...] = part

        @pl.when(i > 0)
        def _():
            dg_ref[...] += part

    row = pl.BlockSpec((tm, D), lambda i: (i, 0))
    vec = pl.BlockSpec((1, D), lambda i: (0, 0))
    in_specs = [row, row, vec, row]
    args = [dh, x, g, dres]
    if dep is not None:
        in_specs.append(_dep_spec(1))
        args.append(dep)
    return pl.pallas_call(
        body, name=name, grid=(T // tm,),
        in_specs=in_specs, out_specs=[row, vec],
        out_shape=[jax.ShapeDtypeStruct((T, D), F32), jax.ShapeDtypeStruct((1, D), F32)],
        compiler_params=_params(("arbitrary",)),
    )(*args)


def final_loss_bwd(x, g, target, name):
    T, D = x.shape
    tm = _tile(T, 256)

    def body(x_ref, g_ref, t_ref, loss_ref, dx_ref, dg_ref):
        i = pl.program_id(0)
        xv = x_ref[...]
        gv = g_ref[...]
        r = lax.rsqrt(jnp.mean(xv * xv, axis=-1, keepdims=True) + RMS_EPS)
        xh = xv * r
        e = xh * gv - t_ref[...]
        lpart = 0.5 * jnp.sum(jnp.mean(e * e, axis=-1, keepdims=True), axis=0, keepdims=True)
        lrow = jnp.broadcast_to(lpart, (1, LANES))
        d = e * (1.0 / D)
        u = d * gv
        dx_ref[...] = r * (u - xh * jnp.mean(u * xh, axis=-1, keepdims=True))
        part = jnp.sum(d * xh, axis=0, keepdims=True)

        @pl.when(i == 0)
        def _():
            dg_ref[...] = part
            loss_ref[...] = lrow

        @pl.when(i > 0)
        def _():
            dg_ref[...] += part
            loss_ref[...] += lrow

    row = pl.BlockSpec((tm, D), lambda i: (i, 0))
    vec = pl.BlockSpec((1, D), lambda i: (0, 0))
    return pl.pallas_call(
        body, name=name, grid=(T // tm,),
        in_specs=[row, vec, row],
        out_specs=[pl.BlockSpec((1, LANES), lambda i: (0, 0)), row, vec],
        out_shape=[jax.ShapeDtypeStruct((1, LANES), F32), jax.ShapeDtypeStruct((T, D), F32),
                   jax.ShapeDtypeStruct((1, D), F32)],
        compiler_params=_params(("arbitrary",)),
    )(x, g, target)


def _bf(v, scale=None):
    if scale is not None:
        v = v * scale
    return v.astype(BF16)


def mm_nn(a, b, name, *, tn, out_dtype, tm=512, n_out=None, b_block=None, b_map=None,
          res=None):
    T, K = a.shape
    N = n_out if n_out is not None else b.shape[1]
    tm = _tile(T, tm)
    tn = _tile(N, tn)
    b_block = b_block or (K, tn)
    b_map = b_map or (lambda n, i: (0, n))

    def body(*refs):
        a_ref, b_ref = refs[0], refs[1]
        o_ref = refs[-1]
        acc = _dot(_bf(a_ref[...]), _bf(b_ref[...]), 1, 0)
        if res is not None:
            acc = refs[2][...] + acc
        o_ref[...] = acc.astype(out_dtype)

    in_specs = [pl.BlockSpec((tm, K), lambda n, i: (i, 0)), pl.BlockSpec(b_block, b_map)]
    args = [a, b]
    if res is not None:
        in_specs.append(pl.BlockSpec((tm, tn), lambda n, i: (i, n)))
        args.append(res)
    return pl.pallas_call(
        body, name=name, grid=(N // tn, T // tm), in_specs=in_specs,
        out_specs=pl.BlockSpec((tm, tn), lambda n, i: (i, n)),
        out_shape=jax.ShapeDtypeStruct((T, N), out_dtype),
        compiler_params=_params(("parallel", "parallel")),
    )(*args)


def mm_nn_sum(pairs, name, *, tn, out_dtype, tm=512, dep=None):
    T = pairs[0][0].shape[0]
    N = pairs[0][1].shape[1]
    tm = _tile(T, tm)
    tn = _tile(N, tn)
    npair = len(pairs)

    def body(*refs):
        acc = None
        for q in range(npair):
            part = _dot(_bf(refs[2 * q][...]), _bf(refs[2 * q + 1][...]), 1, 0)
            acc = part if acc is None else acc + part
        refs[-1][...] = acc.astype(out_dtype)

    in_specs, args = [], []
    for a, b in pairs:
        K = a.shape[1]
        in_specs += [pl.BlockSpec((tm, K), lambda n, i: (i, 0)), pl.BlockSpec((K, tn), lambda n, i: (0, n))]
        args += [a, b]
    if dep is not None:
        in_specs.append(_dep_spec(2))
        args.append(dep)
    return pl.pallas_call(
        body, name=name, grid=(N // tn, T // tm), in_specs=in_specs,
        out_specs=pl.BlockSpec((tm, tn), lambda n, i: (i, n)),
        out_shape=jax.ShapeDtypeStruct((T, N), out_dtype),
        compiler_params=_params(("parallel", "parallel")),
    )(*args)


def mm_nt(pairs, name, *, tn, out_dtype, tm=512, dep=None):
    T = pairs[0][0].shape[0]
    N = pairs[0][1].shape[0]
    tm = _tile(T, tm)
    tn = _tile(N, tn)
    npair = len(pairs)

    def body(*refs):
        o_ref = refs[-1]
        acc = None
        for q in range(npair):
            part = _dot(_bf(refs[2 * q][...]), _bf(refs[2 * q + 1][...]), 1, 1)
            acc = part if acc is None else acc + part
        o_ref[...] = acc.astype(out_dtype)

    in_specs, args = [], []
    for a, b in pairs:
        K = a.shape[1]
        in_specs += [pl.BlockSpec((tm, K), lambda n, i: (i, 0)), pl.BlockSpec((tn, K), lambda n, i: (n, 0))]
        args += [a, b]
    if dep is not None:
        in_specs.append(_dep_spec(2))
        args.append(dep)
    return pl.pallas_call(
        body, name=name, grid=(N // tn, T // tm), in_specs=in_specs,
        out_specs=pl.BlockSpec((tm, tn), lambda n, i: (i, n)),
        out_shape=jax.ShapeDtypeStruct((T, N), out_dtype),
        compiler_params=_params(("parallel", "parallel")),
    )(*args)


def mm_tn(a, b, name, *, grid, a_block, a_map, b_block, b_map, o_block, o_map, out_shape,
          b_scale=None, dep=None):
    def body(a_ref, b_ref, *rest):
        rest[-1][...] = _dot(_bf(a_ref[...]), _bf(b_ref[...], b_scale), 0, 0).astype(BF16)

    in_specs = [pl.BlockSpec(a_block, a_map), pl.BlockSpec(b_block, b_map)]
    args = [a, b]
    if dep is not None:
        in_specs.append(_dep_spec(len(grid)))
        args.append(dep)
    return pl.pallas_call(
        body, name=name, grid=grid, in_specs=in_specs,
        out_specs=pl.BlockSpec(o_block, o_map),
        out_shape=jax.ShapeDtypeStruct(out_shape, BF16),
        compiler_params=_params(("parallel",) * len(grid)),
    )(*args)


def mm_tn_plain(a, b, name, *, tm=512, tn=512, b_scale=None):
    T, M = a.shape
    N = b.shape[1]
    tm = _tile(M, tm)
    tn = _tile(N, tn)
    return mm_tn(a, b, name, grid=(M // tm, N // tn),
                 a_block=(T, tm), a_map=lambda m, n: (0, m),
                 b_block=(T, tn), b_map=lambda m, n: (0, n),
                 o_block=(tm, tn), o_map=lambda m, n: (m, n),
                 out_shape=(M, N), b_scale=b_scale)


def ffn_up(h, wgu, gi, ui, nper, name):
    T, D = h.shape
    F8 = wgu.shape[0] // (N_DEV * nper)
    tm = _tile(T, 512)
    nt = T // tm

    def body(h_ref, wg_ref, wu_ref, a_ref, b_ref, s_ref):
        hv = h_ref[...]
        a = _dot(hv, wg_ref[...], 1, 1)
        b = _dot(hv, wu_ref[...], 1, 1)
        a_ref[...] = a.astype(BF16)
        b_ref[...] = b.astype(BF16)
        s_ref[...] = (a * _sigmoid(a) * b).astype(BF16)

    blk = pl.BlockSpec((tm, F8), lambda j, i: (j * nt + i, 0))
    shp = jax.ShapeDtypeStruct((N_DEV * T, F8), BF16)
    return pl.pallas_call(
        body, name=name, grid=(N_DEV, nt),
        in_specs=[pl.BlockSpec((tm, D), lambda j, i: (i, 0)),
                  pl.BlockSpec((F8, D), lambda j, i: (j * nper + gi, 0)),
                  pl.BlockSpec((F8, D), lambda j, i: (j * nper + ui, 0))],
        out_specs=[blk, blk, blk], out_shape=[shp, shp, shp],
        compiler_params=_params(("parallel", "parallel")),
    )(h, wgu, wgu)


def ffn_down(s, wd, di, nper, x, name):
    T, D = x.shape
    F8 = s.shape[1]
    tm = _tile(T, 512)
    nt = T // tm

    def body(s_ref, w_ref, x_ref, o_ref, acc_ref):
        j = pl.program_id(1)
        part = _dot(s_ref[...], w_ref[...], 1, 0)

        @pl.when(j == 0)
        def _():
            acc_ref[...] = part

        @pl.when(j > 0)
        def _():
            acc_ref[...] += part

        @pl.when(j == N_DEV - 1)
        def _():
            o_ref[...] = x_ref[...] + 0.5 * acc_ref[...]

    return pl.pallas_call(
        body, name=name, grid=(nt, N_DEV),
        in_specs=[pl.BlockSpec((tm, F8), lambda i, j: (j * nt + i, 0)),
                  pl.BlockSpec((F8, D), lambda i, j: (j * nper + di, 0)),
                  pl.BlockSpec((tm, D), lambda i, j: (i, 0))],
        out_specs=pl.BlockSpec((tm, D), lambda i, j: (i, 0)),
        out_shape=jax.ShapeDtypeStruct((T, D), F32),
        scratch_shapes=[pltpu.VMEM((tm, D), F32)],
        compiler_params=_params(("parallel", "arbitrary")),
    )(s, wd, x)


def ffn_bwd_act(dx, wd, di, nper_d, a, b, name, dep=None):
    T, D = dx.shape
    F8 = a.shape[1]
    tm = _tile(T, 512)
    nt = T // tm

    def body(dx_ref, w_ref, a_ref, b_ref, *rest):
        da_ref, db_ref = rest[-2], rest[-1]
        ds = _dot(_bf(dx_ref[...], 0.5), w_ref[...], 1, 1)
        av = a_ref[...].astype(F32)
        bv = b_ref[...].astype(F32)
        sg = _sigmoid(av)
        da_ref[...] = (ds * bv * (sg * (1.0 + av * (1.0 - sg)))).astype(BF16)
        db_ref[...] = (ds * (av * sg)).astype(BF16)

    blk = pl.BlockSpec((tm, F8), lambda j, i: (j * nt + i, 0))
    shp = jax.ShapeDtypeStruct((N_DEV * T, F8), BF16)
    in_specs = [pl.BlockSpec((tm, D), lambda j, i: (i, 0)),
                pl.BlockSpec((F8, D), lambda j, i: (j * nper_d + di, 0)), blk, blk]
    args = [dx, wd, a, b]
    if dep is not None:
        in_specs.append(_dep_spec(2))
        args.append(dep)
    return pl.pallas_call(
        body, name=name, grid=(N_DEV, nt), in_specs=in_specs,
        out_specs=[blk, blk], out_shape=[shp, shp],
        compiler_params=_params(("parallel", "parallel")),
    )(*args)


def ffn_bwd_dh(da, db, wgu, gi, ui, nper, D, name, dep=None):
    F8 = da.shape[1]
    T = da.shape[0] // N_DEV
    tm = _tile(T, 512)
    nt = T // tm

    def body(da_ref, db_ref, wg_ref, wu_ref, *rest):
        o_ref, acc_ref = rest[-2], rest[-1]
        j = pl.program_id(1)
        part = _dot(da_ref[...], wg_ref[...], 1, 0) + _dot(db_ref[...], wu_ref[...], 1, 0)

        @pl.when(j == 0)
        def _():
            acc_ref[...] = part

        @pl.when(j > 0)
        def _():
            acc_ref[...] += part

        @pl.when(j == N_DEV - 1)
        def _():
            o_ref[...] = acc_ref[...]

    blk = pl.BlockSpec((tm, F8), lambda i, j: (j * nt + i, 0))
    in_specs = [blk, blk,
                pl.BlockSpec((F8, D), lambda i, j: (j * nper + gi, 0)),
                pl.BlockSpec((F8, D), lambda i, j: (j * nper + ui, 0))]
    args = [da, db, wgu, wgu]
    if dep is not None:
        in_specs.append(_dep_spec(2))
        args.append(dep)
    return pl.pallas_call(
        body, name=name, grid=(nt, N_DEV), in_specs=in_specs,
        out_specs=pl.BlockSpec((tm, D), lambda i, j: (i, 0)),
        out_shape=jax.ShapeDtypeStruct((T, D), F32),
        scratch_shapes=[pltpu.VMEM((tm, D), F32)],
        compiler_params=_params(("parallel", "arbitrary")),
    )(*args)


def ffn_bwd_dw_in(h, dact, name, dep=None):
    T, D = h.shape
    F8 = dact.shape[1]
    tm = _tile(D, 512)
    return mm_tn(dact, h, name, grid=(N_DEV, D // tm),
                 a_block=(T, F8), a_map=lambda j, m: (j, 0),
                 b_block=(T, tm), b_map=lambda j, m: (0, m),
                 o_block=(F8, tm), o_map=lambda j, m: (j, m),
                 out_shape=(N_DEV * F8, D), dep=dep)


def ffn_bwd_dw_down(s, dx, name):
    T, D = dx.shape
    F8 = s.shape[1]
    tn = _tile(D, 512)
    return mm_tn(s, dx, name, grid=(N_DEV, D // tn),
                 a_block=(T, F8), a_map=lambda j, n: (j, 0),
                 b_block=(T, tn), b_map=lambda j, n: (0, n),
                 o_block=(F8, tn), o_map=lambda j, n: (j, n),
                 out_shape=(N_DEV * F8, D), b_scale=0.5)


def _t5_bucket_np(dist):
    max_exact = N_REL_BUCKETS // 2
    d = np.maximum(dist, 1).astype(np.float64)
    large = max_exact + (np.log(d / max_exact) / math.log(REL_MAX_DISTANCE / max_exact)
                         * (N_REL_BUCKETS - max_exact)).astype(np.int64)
    large32 = max_exact + (np.log(d.astype(np.float32) / np.float32(max_exact))
                           / np.float32(math.log(REL_MAX_DISTANCE / max_exact))
                           * np.float32(N_REL_BUCKETS - max_exact)).astype(np.int64)
    assert np.array_equal(large, large32)
    large = np.minimum(large, N_REL_BUCKETS - 1)
    return np.where(dist < max_exact, dist, large)


def _distance_tables(T, tq):
    dist = np.arange(T)
    mult = np.zeros(T, np.int64)
    for window, dilation in DILATED_PATTERNS:
        mult += ((dist % dilation == 0) & (dist // dilation <= window // dilation)).astype(np.int64)
    logm = np.where(mult > 0, np.log(np.maximum(mult, 1)), NEG_INF).astype(np.float32)
    bucket = _t5_bucket_np(dist).astype(np.int32)
    nkb = T // tq
    k = np.arange(nkb)[:, None, None]
    r = np.arange(tq)[None, :, None]
    c = np.arange(tq)[None, None, :]
    delta = k * tq + r - c
    return bucket, logm, delta


def _tile_buckets(T, tq):
    bucket, logm, delta = _distance_tables(T, tq)
    safe = np.maximum(delta, 0)
    bidx = np.where(delta >= 0, bucket[safe], -1).astype(np.int32)
    logm_t = np.where(delta >= 0, logm[safe], NEG_INF).astype(np.float32)
    present = [sorted(set(np.unique(bidx[k]).tolist()) - {-1}) for k in range(T // tq)]
    return bidx, logm_t, present


def bias_tiles(rel_table, T, tq):
    bidx, logm_t, present = _tile_buckets(T, tq)
    nkb = T // tq

    def body(tab_ref, b_ref, lm_ref, o_ref):
        slot = pl.program_id(0)

        @pl.when(slot == 0)
        def _():
            o_ref[...] = jnp.where(b_ref[...] >= 0, 0.0, NEG_INF)

        @pl.when(slot > 0)
        def _():
            for k in range(nkb):
                bi = b_ref[k]
                acc = lm_ref[k]
                for b in present[k]:
                    acc = acc + jnp.where(bi == b, tab_ref[b, slot - 1], 0.0)
                o_ref[k] = acc

    full = pl.BlockSpec((nkb, tq, tq), lambda s: (0, 0, 0))
    return pl.pallas_call(
        body, name="bias_tiles", grid=(1 + N_DIL,),
        in_specs=[pl.BlockSpec(memory_space=pltpu.SMEM), full, full],
        out_specs=pl.BlockSpec((None, nkb, tq, tq), lambda s: (s, 0, 0, 0)),
        out_shape=jax.ShapeDtypeStruct((1 + N_DIL, nkb, tq, tq), F32),
        compiler_params=_params(("parallel",)),
    )(rel_table, jnp.asarray(bidx), jnp.asarray(logm_t))


def fox_gate_fwd(uf, bf, name):
    T = uf.shape[0]
    tb = _tile(T, 512)

    def body(u_ref, b_ref, c_ref, ct_ref):
        lane = lax.broadcasted_iota(jnp.int32, (1, LANES), 1)
        tri = (lax.broadcasted_iota(jnp.int32, (tb, tb), 0)
               >= lax.broadcasted_iota(jnp.int32, (tb, tb), 1)).astype(F32)
        carry = jnp.zeros((1, LANES), F32)
        for blk in range(T // tb):
            z = u_ref[pl.ds(blk * tb, tb), :] + b_ref[...]
            lf = jnp.minimum(z, 0.0) - jnp.log1p(jnp.exp(-jnp.abs(z)))
            lf = jnp.where(lane < N_FOX, lf, 0.0)
            cb = _dot(tri, lf, 1, 0, precision=lax.Precision.HIGHEST) + carry
            c_ref[pl.ds(blk * tb, tb), :] = cb
            ct_ref[:, pl.ds(blk * tb, tb)] = cb.T
            carry = cb[tb - 1:tb, :]

    return pl.pallas_call(
        body, name=name,
        out_shape=[jax.ShapeDtypeStruct((T, LANES), F32), jax.ShapeDtypeStruct((LANES, T), F32)],
        compiler_params=_params(None),
    )(uf, bf)


def fox_gate_bwd(dct, uf, bf, name):
    T = uf.shape[0]
    tb = _tile(T, 512)

    def body(d_ref, u_ref, b_ref, du_ref, db_ref):
        lane = lax.broadcasted_iota(jnp.int32, (1, LANES), 1)
        triu = (lax.broadcasted_iota(jnp.int32, (tb, tb), 0)
                <= lax.broadcasted_iota(jnp.int32, (tb, tb), 1)).astype(F32)
        carry = jnp.zeros((1, LANES), F32)
        dbv = jnp.zeros((1, LANES), F32)
        for blk in reversed(range(T // tb)):
            dc = d_ref[:, pl.ds(blk * tb, tb)].T
            dlf = _dot(triu, dc, 1, 0, precision=lax.Precision.HIGHEST) + carry
            carry = dlf[0:1, :]
            z = u_ref[pl.ds(blk * tb, tb), :] + b_ref[...]
            dz = jnp.where(lane < N_FOX, dlf * (1.0 - _sigmoid(z)), 0.0)
            du_ref[pl.ds(blk * tb, tb), :] = dz
            dbv = dbv + jnp.sum(dz, axis=0, keepdims=True)
        db_ref[...] = dbv

    return pl.pallas_call(
        body, name=name,
        out_shape=[jax.ShapeDtypeStruct((T, LANES), F32), jax.ShapeDtypeStruct((1, LANES), F32)],
        compiler_params=_params(None),
    )(dct, uf, bf)


def _bias_slot(h):
    return jnp.maximum(h - (N_FOX - 1), 0)


def _scores(q_ref, k_ref, c_ref, ct_ref, tb_ref, h, i, tq):
    scale = HEAD_DIM ** -0.5
    n = (i + 1) * tq
    rows = pl.ds(i * tq, tq)
    s = _dot(q_ref[rows, :], k_ref[pl.ds(0, n), :], 1, 1) * scale
    lane = lax.broadcasted_iota(jnp.int32, (1, LANES), 1)
    c_col = jnp.sum(jnp.where(lane == h, c_ref[rows, :], 0.0), axis=1, keepdims=True)
    c_row = ct_ref[pl.ds(h, 1), pl.ds(0, n)]
    bias = jnp.concatenate([tb_ref[i - jb] for jb in range(i + 1)], axis=1)
    return s + (c_col - c_row) + bias


def _attn_specs(T, tq):
    nkb = T // tq
    return [
        pl.BlockSpec((T, HEAD_DIM), lambda h: (0, h)),
        pl.BlockSpec((T, HEAD_DIM), lambda h: (0, N_HEADS + h)),
        pl.BlockSpec((T, HEAD_DIM), lambda h: (0, 2 * N_HEADS + h)),
        pl.BlockSpec((T, LANES), lambda h: (0, 0)),
        pl.BlockSpec((LANES, T), lambda h: (0, 0)),
        pl.BlockSpec((None, nkb, tq, tq), lambda h: (_bias_slot(h), 0, 0, 0)),
    ]


def attention_fwd(qkv, c, ct, tiles, name):
    T = qkv.shape[0]
    tq = tiles.shape[2]

    def body(q_ref, k_ref, v_ref, c_ref, ct_ref, tb_ref, o_ref):
        h = pl.program_id(0)
        for i in range(T // tq):
            s = _scores(q_ref, k_ref, c_ref, ct_ref, tb_ref, h, i, tq)
            p = jnp.exp(s - jnp.max(s, axis=1, keepdims=True))
            l = jnp.sum(p, axis=1, keepdims=True)
            o = _dot((p / l).astype(BF16), v_ref[pl.ds(0, (i + 1) * tq), :], 1, 0)
            o_ref[pl.ds(i * tq, tq), :] = o.astype(BF16)

    return pl.pallas_call(
        body, name=name, grid=(N_HEADS,),
        in_specs=_attn_specs(T, tq),
        out_specs=pl.BlockSpec((T, HEAD_DIM), lambda h: (0, h)),
        out_shape=jax.ShapeDtypeStruct((T, N_HEADS * HEAD_DIM), BF16),
        compiler_params=_params(("parallel",)),
    )(qkv, qkv, qkv, c, ct, tiles)


def attention_bwd(qkv, c, ct, tiles, do, name):
    T = qkv.shape[0]
    tq = tiles.shape[2]
    nkb = T // tq
    scale = HEAD_DIM ** -0.5

    def body(q_ref, k_ref, v_ref, c_ref, ct_ref, tb_ref, do_ref,
             dq_ref, dk_ref, dv_ref, dct_ref, dtb_ref, dk_acc, dv_acc):
        h = pl.program_id(0)
        dk_acc[...] = jnp.zeros_like(dk_acc)
        dv_acc[...] = jnp.zeros_like(dv_acc)
        dct_ref[...] = jnp.zeros_like(dct_ref)
        dtb_ref[...] = jnp.zeros_like(dtb_ref)
        for i in range(nkb):
            rows, keys = pl.ds(i * tq, tq), pl.ds(0, (i + 1) * tq)
            s = _scores(q_ref, k_ref, c_ref, ct_ref, tb_ref, h, i, tq)
            p = jnp.exp(s - jnp.max(s, axis=1, keepdims=True))
            p = p / jnp.sum(p, axis=1, keepdims=True)
            dov = do_ref[rows, :]
            dp = _dot(dov, v_ref[keys, :], 1, 1)
            ds = p * (dp - jnp.sum(p * dp, axis=1, keepdims=True))
            ds_b = ds.astype(BF16)
            dq_ref[rows, :] = (_dot(ds_b, k_ref[keys, :], 1, 0) * scale).astype(BF16)
            dk_acc[keys, :] += _dot(ds_b, q_ref[rows, :], 0, 0) * scale
            dv_acc[keys, :] += _dot(p.astype(BF16), dov, 0, 0)
            dct_ref[:, keys] += -jnp.sum(ds, axis=0, keepdims=True)
            for jb in range(i + 1):
                dtb_ref[i - jb] += ds[:, jb * tq:(jb + 1) * tq]
        dk_ref[...] = dk_acc[...].astype(BF16)
        dv_ref[...] = dv_acc[...].astype(BF16)

    head_cols = jax.ShapeDtypeStruct((T, N_HEADS * HEAD_DIM), BF16)
    col = pl.BlockSpec((T, HEAD_DIM), lambda h: (0, h))
    return pl.pallas_call(
        body, name=name, grid=(N_HEADS,),
        in_specs=_attn_specs(T, tq) + [col],
        out_specs=[col, col, col,
                   pl.BlockSpec((None, 1, T), lambda h: (h, 0, 0)),
                   pl.BlockSpec((None, nkb, tq, tq), lambda h: (_bias_slot(h), 0, 0, 0))],
        out_shape=[head_cols, head_cols, head_cols,
                   jax.ShapeDtypeStruct((N_HEADS, 1, T), F32),
                   jax.ShapeDtypeStruct((1 + N_DIL, nkb, tq, tq), F32)],
        scratch_shapes=[pltpu.VMEM((T, HEAD_DIM), F32), pltpu.VMEM((T, HEAD_DIM), F32)],
        compiler_params=_params(("arbitrary",)),
    )(qkv, qkv, qkv, c, ct, tiles, do)


def rel_table_grad(dtiles, T, name):
    tq = dtiles.shape[2]
    nkb = T // tq
    bidx, _, present = _tile_buckets(T, tq)

    def body(d_ref, b_ref, o_ref):
        lane = lax.broadcasted_iota(jnp.int32, (1, LANES), 1)
        row = jnp.zeros((1, LANES), F32)
        for k in range(nkb):
            d = d_ref[k]
            bi = b_ref[k]
            for b in present[k]:
                v = jnp.sum(jnp.sum(jnp.where(bi == b, d, 0.0), axis=0, keepdims=True),
                            axis=1, keepdims=True)
                row = row + jnp.where(lane == b, v, 0.0)
        o_ref[...] = row

    return pl.pallas_call(
        body, name=name, grid=(N_DIL,),
        in_specs=[pl.BlockSpec((None, nkb, tq, tq), lambda h: (h + 1, 0, 0, 0)),
                  pl.BlockSpec((nkb, tq, tq), lambda h: (0, 0, 0))],
        out_specs=pl.BlockSpec((None, 1, LANES), lambda h: (h, 0, 0)),
        out_shape=jax.ShapeDtypeStruct((N_DIL, 1, LANES), F32),
        compiler_params=_params(("parallel",)),
    )(dtiles, jnp.asarray(bidx))


def ple_combine(x, z, pp, name):
    T, D = x.shape
    tm = _tile(T, 256)

    def body(x_ref, z_ref, p_ref, o_ref):
        o_ref[...] = x_ref[...] + _sigmoid(z_ref[...]) * p_ref[...]

    row = pl.BlockSpec((tm, D), lambda i: (i, 0))
    return pl.pallas_call(
        body, name=name, grid=(T // tm,), in_specs=[row, row, row], out_specs=row,
        out_shape=jax.ShapeDtypeStruct((T, D), F32), compiler_params=_params(("parallel",)),
    )(x, z, pp)


def ple_bwd_elem(dx, z, pp, name):
    T, D = dx.shape
    tm = _tile(T, 256)

    def body(dx_ref, z_ref, p_ref, dz_ref, dp_ref):
        gate = _sigmoid(z_ref[...])
        d = dx_ref[...]
        dz_ref[...] = (d * p_ref[...] * gate * (1.0 - gate)).astype(BF16)
        dp_ref[...] = (d * gate).astype(BF16)

    row = pl.BlockSpec((tm, D), lambda i: (i, 0))
    shp = jax.ShapeDtypeStruct((T, D), BF16)
    return pl.pallas_call(
        body, name=name, grid=(T // tm,), in_specs=[row, row, row], out_specs=[row, row],
        out_shape=[shp, shp], compiler_params=_params(("parallel",)),
    )(dx, z, pp)


def _peer_list():
    x, y, c = lax.axis_index("x"), lax.axis_index("y"), lax.axis_index("c")
    me = 4 * x + 2 * y + c
    peers = []
    for fx in (0, 1):
        for fy in (0, 1):
            for fc in (0, 1):
                if fx or fy or fc:
                    px = 1 - x if fx else x
                    py = 1 - y if fy else y
                    pc = 1 - c if fc else c
                    peers.append(((px, py, pc), 4 * px + 2 * py + pc))
    return me, peers


_HBM = pl.BlockSpec(memory_space=pltpu.HBM)
_SEM = pl.BlockSpec(memory_space=pltpu.SEMAPHORE)
_EFFECT = pltpu.SideEffectType.DATAFLOW_SIDE_EFFECTING
N_PEERS = N_DEV - 1


def _in_hbm(a):
    return pltpu.with_memory_space_constraint(a, pltpu.HBM)


def _exchange_copies(srcs, lands, send_sems, recv_sems, blockwise):
    me, peers = _peer_list()
    sends, recvs = [], []
    for a in range(len(srcs)):
        for k, (dev, idx) in enumerate(peers):
            src = srcs[a].at[idx] if blockwise[a] else srcs[a]
            sends.append(pltpu.make_async_remote_copy(
                src_ref=src, dst_ref=lands[a].at[me], send_sem=send_sems[a].at[k],
                recv_sem=recv_sems[a].at[k], device_id=dev, device_id_type=MESH))
            recvs.append(pltpu.make_async_remote_copy(
                src_ref=src, dst_ref=lands[a].at[idx], send_sem=send_sems[a].at[k],
                recv_sem=recv_sems[a].at[k], device_id=dev, device_id_type=MESH))
    return sends, recvs


def exchange_start(srcs, lands, blockwise, name):
    n = len(srcs)

    def body(*refs):
        src_in, land_in = refs[:n], refs[n:2 * n]
        send_sems, recv_sems = refs[2 * n:3 * n], refs[3 * n:4 * n]
        token = refs[6 * n]
        sends, _ = _exchange_copies(src_in, land_in, send_sems, recv_sems, blockwise)
        for cp in sends:
            cp.start()
        token[...] = jnp.zeros_like(token)

    out_shape = ([pltpu.SemaphoreType.DMA((N_PEERS,))] * (2 * n)
                 + [pltpu.HBM(s.shape, s.dtype) for s in srcs]
                 + [pltpu.HBM(l.shape, l.dtype) for l in lands]
                 + [jax.ShapeDtypeStruct((8, LANES), F32)])
    aliases = {a: 2 * n + a for a in range(2 * n)}
    outs = pl.pallas_call(
        body, name=name, out_shape=out_shape,
        in_specs=[_HBM] * (2 * n),
        out_specs=[_SEM] * (2 * n) + [_HBM] * (2 * n) + [pl.BlockSpec(memory_space=pltpu.VMEM)],
        input_output_aliases=aliases,
        compiler_params=pltpu.CompilerParams(has_side_effects=_EFFECT),
    )(*[_in_hbm(s) for s in srcs], *[_in_hbm(l) for l in lands])
    return (outs[:n], outs[n:2 * n], outs[2 * n:3 * n], outs[3 * n:4 * n], outs[4 * n])


def exchange_wait(send_sems, recv_sems, srcs, lands, blockwise, after, name):
    n = len(srcs)

    def body(*refs):
        src_in, land_in = refs[:n], refs[n:2 * n]
        ss, rs = refs[2 * n:3 * n], refs[3 * n:4 * n]
        sends, recvs = _exchange_copies(src_in, land_in, ss, rs, blockwise)
        for cp in sends:
            cp.wait_send()
        for cp in recvs:
            cp.wait_recv()

    outs = pl.pallas_call(
        body, name=name,
        out_shape=[pltpu.HBM(s.shape, s.dtype) for s in srcs] + [pltpu.HBM(l.shape, l.dtype) for l in lands],
        in_specs=[_HBM] * (2 * n) + [_SEM] * (2 * n) + [pl.BlockSpec(memory_space=pl.ANY)],
        out_specs=[_HBM] * (2 * n),
        input_output_aliases={a: a for a in range(2 * n)},
        compiler_params=pltpu.CompilerParams(has_side_effects=_EFFECT),
    )(*srcs, *lands, *send_sems, *recv_sems, after)
    return outs[n:]


def _landing(own_block, me, slots=N_DEV):
    empty = lax.empty((slots,) + own_block.shape, own_block.dtype)
    return lax.dynamic_update_slice(empty, own_block[None], (me,) + (0,) * own_block.ndim)


N_CHIPS = N_DEV // 2
_CHIP_FLIPS = ((1, 0), (0, 1), (1, 1))


def _xyc():
    return lax.axis_index("x"), lax.axis_index("y"), lax.axis_index("c")


def _other_chips(x, y):
    return [(1 - x if fx else x, 1 - y if fy else y) for fx, fy in _CHIP_FLIPS]


def _remote(src, dst, send_sem, recv_sem, dev):
    return pltpu.make_async_remote_copy(src_ref=src, dst_ref=dst, send_sem=send_sem, recv_sem=recv_sem,
                                        device_id=dev, device_id_type=MESH)


def comm_call(name, bufs, sems_in, sems_out, fn, after=None, want_token=False):
    nb, ni, no = len(bufs), len(sems_in), len(sems_out)
    na = 0 if after is None else 1

    def body(*refs):
        buf_refs = refs[:nb]
        sin = refs[nb:nb + ni]
        sout = refs[nb + ni + na:nb + ni + na + no]
        fn(buf_refs, sin, sout)
        if want_token:
            tok = refs[nb + ni + na + no + nb]
            tok[...] = jnp.zeros_like(tok)

    out_shape = list(sems_out) + [pltpu.HBM(b.shape, b.dtype) for b in bufs]
    out_specs = [_SEM] * no + [_HBM] * nb
    if want_token:
        out_shape.append(jax.ShapeDtypeStruct((8, LANES), F32))
        out_specs.append(pl.BlockSpec(memory_space=pltpu.VMEM))
    args = [_in_hbm(b) for b in bufs] + list(sems_in) + ([after] if na else [])
    outs = pl.pallas_call(
        body, name=name, out_shape=out_shape,
        in_specs=[_HBM] * nb + [_SEM] * ni + [pl.BlockSpec(memory_space=pl.ANY)] * na,
        out_specs=out_specs, input_output_aliases={a: no + a for a in range(nb)},
        compiler_params=pltpu.CompilerParams(has_side_effects=_EFFECT),
    )(*args)
    return list(outs[:no]), list(outs[no:no + nb]), (outs[no + nb] if want_token else None)


def _dma_sems(*sizes):
    return [pltpu.SemaphoreType.DMA((s,)) for s in sizes]


def _relay_route(x, y, c):
    flip_x, flip_y = c * (1 - 2 * x), c * (1 - 2 * y)
    from_chip = (x + flip_x, (1 - y) - flip_y)
    to_chip = ((1 - x) - flip_x, y + flip_y)
    return from_chip, to_chip


def gather_start(srcs, lands, name):
    n = len(srcs)

    def fn(bufs, sin, sout):
        x, y, c = _xyc()
        me = 4 * x + 2 * y + c
        for a in range(n):
            src, land = bufs[a], bufs[n + a]
            send, recv_d, recv_i = sout[3 * a:3 * a + 3]
            _remote(src, land.at[me], send.at[0], recv_d.at[0], (x, y, 1 - c)).start()
            for k, (px, py) in enumerate(_other_chips(x, y)[:2]):
                _remote(src, land.at[me], send.at[1 + k], recv_i.at[k], (px, py, c)).start()

    return comm_call(name, list(srcs) + list(lands), [], _dma_sems(3, 1, 2) * n, fn, want_token=True)


def gather_forward(srcs, lands, recv_i, after, name):
    n = len(srcs)

    def fn(bufs, sin, sout):
        x, y, c = _xyc()
        (sx, sy), (dx, dy) = _relay_route(x, y, c)
        for a in range(n):
            src, land = bufs[a], bufs[n + a]
            f_send, f_recv, r_send, r_recv = sout[4 * a:4 * a + 4]
            nbrs = _other_chips(x, y)[:2]
            for k, (px, py) in enumerate(nbrs):
                _remote(src, land.at[4 * px + 2 * py + c], f_send.at[k], sin[a].at[k], (px, py, c)).wait_recv()
            passed_on = land.at[4 * sx + 2 * sy + c]
            _remote(passed_on, passed_on, r_send.at[0], r_recv.at[0], (dx, dy, c)).start()
            for k, (px, py) in enumerate(nbrs):
                blk = land.at[4 * px + 2 * py + c]
                _remote(blk, blk, f_send.at[k], f_recv.at[k], (x, y, 1 - c)).start()

    sems, bufs, _ = comm_call(name, list(srcs) + list(lands), recv_i, _dma_sems(2, 2, 1, 1) * n, fn, after=after)
    return sems, bufs


def gather_forward_diagonal(srcs, lands, r_recv, after, name):
    n = len(srcs)

    def fn(bufs, sin, sout):
        x, y, c = _xyc()
        for a in range(n):
            src, land = bufs[a], bufs[n + a]
            g_send, g_recv = sout[2 * a:2 * a + 2]
            blk = land.at[4 * (1 - x) + 2 * (1 - y) + c]
            _remote(src, blk, g_send.at[0], sin[a].at[0], (x, y, 1 - c)).wait_recv()
            _remote(blk, blk, g_send.at[0], g_recv.at[0], (x, y, 1 - c)).start()

    sems, bufs, _ = comm_call(name, list(srcs) + list(lands), r_recv, _dma_sems(1, 1) * n, fn, after=after)
    return sems, bufs


def gather_wait(srcs, lands, sems_by_array, after, name):
    n = len(srcs)

    def fn(bufs, sin, sout):
        x, y, c = _xyc()
        sib = (x, y, 1 - c)
        for a in range(n):
            src, land = bufs[a], bufs[n + a]
            s_send, s_recv_d, s_fsend, s_frecv, s_rsend, s_gsend, s_grecv = sin[7 * a:7 * a + 7]
            sib_blk = land.at[4 * x + 2 * y + 1 - c]
            for k in range(3):
                _remote(src, sib_blk, s_send.at[k], s_recv_d.at[0], sib).wait_send()
            _remote(src, sib_blk, s_send.at[0], s_recv_d.at[0], sib).wait_recv()
            _remote(src, sib_blk, s_rsend.at[0], s_recv_d.at[0], sib).wait_send()
            for k, (px, py) in enumerate(_other_chips(x, y)[:2]):
                cp = _remote(src, land.at[4 * px + 2 * py + 1 - c], s_fsend.at[k], s_frecv.at[k], sib)
                cp.wait_send()
                cp.wait_recv()
            cp = _remote(src, land.at[4 * (1 - x) + 2 * (1 - y) + 1 - c], s_gsend.at[0], s_grecv.at[0], sib)
            cp.wait_send()
            cp.wait_recv()

    sems_in = [s for per_array in sems_by_array for s in per_array]
    _, bufs, _ = comm_call(name, list(srcs) + list(lands), sems_in, [], fn, after=after)
    return bufs[n:]


def scatter_pair_start(src4s, lands, name, after=None):
    n = len(src4s)

    def fn(bufs, sin, sout):
        x, y, c = _xyc()
        for a in range(n):
            _remote(bufs[a].at[:, 1 - c], bufs[n + a], sout[2 * a].at[0], sout[2 * a + 1].at[0],
                    (x, y, 1 - c)).start()

    return comm_call(name, list(src4s) + list(lands), [], _dma_sems(1, 1) * n, fn, after=after, want_token=True)


def scatter_pair_wait(src4s, lands, sems, after, name):
    n = len(src4s)

    def fn(bufs, sin, sout):
        x, y, c = _xyc()
        for a in range(n):
            cp = _remote(bufs[a].at[:, 1 - c], bufs[n + a], sin[2 * a].at[0], sin[2 * a + 1].at[0], (x, y, 1 - c))
            cp.wait_send()
            cp.wait_recv()

    _, bufs, _ = comm_call(name, list(src4s) + list(lands), sems, [], fn, after=after)
    return bufs[:n], bufs[n:]


def _row_tile(R):
    for cand in (128, 64, 32, 16):
        if R % cand == 0:
            return cand
    return R


def chip_sum(src4, land, c, name):
    _, _, R, C = src4.shape
    tr = R

    def body(c_ref, a_ref, b_ref, o_ref):
        o_ref[...] = (a_ref[...].astype(F32) + b_ref[...].astype(F32)).astype(BF16)

    grid_spec = pltpu.PrefetchScalarGridSpec(
        num_scalar_prefetch=1, grid=(N_CHIPS, R // tr),
        in_specs=[pl.BlockSpec((None, None, tr, C), lambda q, i, cr: (q, cr[0], i, 0)),
                  pl.BlockSpec((None, tr, C), lambda q, i, cr: (q, i, 0))],
        out_specs=pl.BlockSpec((None, tr, C), lambda q, i, cr: (q, i, 0)))
    return pl.pallas_call(
        body, name=name, grid_spec=grid_spec,
        out_shape=jax.ShapeDtypeStruct((N_CHIPS, R, C), BF16),
        compiler_params=_params(("parallel", "parallel")),
    )(c.reshape(1).astype(jnp.int32), src4, land)


def scatter_chip_start(sums, lands, name):
    n = len(sums)

    def fn(bufs, sin, sout):
        x, y, c = _xyc()
        for a in range(n):
            for k, (px, py) in enumerate(_other_chips(x, y)):
                _remote(bufs[a].at[2 * px + py], bufs[n + a].at[2 * x + y], sout[2 * a].at[k], sout[2 * a + 1].at[k],
                        (px, py, c)).start()

    return comm_call(name, list(sums) + list(lands), [], _dma_sems(3, 3) * n, fn, want_token=True)


def scatter_chip_wait(sums, lands, sems, after, name):
    n = len(sums)

    def fn(bufs, sin, sout):
        x, y, c = _xyc()
        for a in range(n):
            for k, (px, py) in enumerate(_other_chips(x, y)):
                cp = _remote(bufs[a].at[2 * px + py], bufs[n + a].at[2 * px + py], sin[2 * a].at[k],
                             sin[2 * a + 1].at[k], (px, py, c))
                cp.wait_send()
                cp.wait_recv()

    _, bufs, _ = comm_call(name, list(sums) + list(lands), sems, [], fn, after=after)
    return bufs[:n], bufs[n:]


def _adamw_math(w, g, m, v):
    m = ADAM_B1 * m + (1.0 - ADAM_B1) * g
    v = ADAM_B2 * v + (1.0 - ADAM_B2) * (g * g)
    m_hat = m / (1.0 - ADAM_B1 ** ADAM_STEP)
    v_hat = v / (1.0 - ADAM_B2 ** ADAM_STEP)
    delta = -ADAM_LR * (m_hat / (jnp.sqrt(v_hat) + ADAM_EPS) + ADAM_WD * w)
    return delta, m, v


def adamw_sharded(parts, w, m, v, name):
    R, C = w.shape
    S = parts.shape[0]
    tr = _row_tile(R)

    def body(p_ref, w_ref, m_ref, v_ref, g_ref, d_ref, nm_ref, nv_ref):
        g = p_ref[0].astype(F32)
        for s in range(1, S):
            g = g + p_ref[s].astype(F32)
        delta, nm, nv = _adamw_math(w_ref[...], g, m_ref[...], v_ref[...])
        g_ref[...] = g
        d_ref[...] = delta
        nm_ref[...] = nm
        nv_ref[...] = nv

    row = pl.BlockSpec((tr, C), lambda i: (i, 0))
    shp = jax.ShapeDtypeStruct((R, C), F32)
    return pl.pallas_call(
        body, name=name, grid=(R // tr,),
        in_specs=[pl.BlockSpec((S, tr, C), lambda i: (0, i, 0)), row, row, row],
        out_specs=[row, row, row, row], out_shape=[shp, shp, shp, shp],
        compiler_params=_params(("parallel",)),
    )(parts, w, m, v)


def adamw_small(parts, w, m, v, name):
    R, C = w.shape

    def body(p_ref, w_ref, m_ref, v_ref, g_ref, d_ref, nm_ref, nv_ref):
        g = p_ref[0]
        for s in range(1, N_DEV):
            g = g + p_ref[s]
        delta, nm, nv = _adamw_math(w_ref[...], g, m_ref[...], v_ref[...])
        g_ref[...] = g
        d_ref[...] = delta
        nm_ref[...] = nm
        nv_ref[...] = nv

    shp = jax.ShapeDtypeStruct((R, C), F32)
    return pl.pallas_call(
        body, name=name, out_shape=[shp, shp, shp, shp], compiler_params=_params(None),
    )(parts, w, m, v)


_ROW_NORM_FFN1, _ROW_NORM_MIX, _ROW_NORM_FFN2, _ROW_NORM_PLE, _ROW_NORM_FINAL = 0, 1, 2, 3, 4
_ROW_B_F, _ROW_REL, _ROW_LOSS, _SMALL_ROWS = 5, 6, 7, 8


def _pack_small(D, norm_ffn1, norm_mix, norm_ffn2, norm_ple, norm_final, b_f, rel_table):
    def row(v):
        v = v.reshape(1, -1)
        return jnp.pad(v, ((0, 0), (0, D - v.shape[1])))
    return jnp.concatenate([row(norm_ffn1), row(norm_mix), row(norm_ffn2), row(norm_ple),
                            row(norm_final), row(b_f), row(rel_table),
                            jnp.zeros((1, D), F32)], axis=0)


def _unpack_small(a, shapes):
    return {"norm_ffn1": a[_ROW_NORM_FFN1].reshape(shapes["norm_ffn1"]),
            "norm_mix": a[_ROW_NORM_MIX].reshape(shapes["norm_mix"]),
            "b_f": a[_ROW_B_F, :N_FOX].reshape(shapes["b_f"]),
            "norm_ffn2": a[_ROW_NORM_FFN2].reshape(shapes["norm_ffn2"]),
            "norm_ple": a[_ROW_NORM_PLE].reshape(shapes["norm_ple"]),
            "rel_table": a[_ROW_REL, :N_REL_BUCKETS * N_DIL].reshape(shapes["rel_table"]),
            "norm_final": a[_ROW_NORM_FINAL].reshape(shapes["norm_final"])}


def local_step(x, p, tgt, g_ffn1, g_mix, g_ffn2, g_ple, g_final, b_f, rel_table,
               forward, weights, emit, emit2, first_dep):
    T, D = x.shape
    P = p.shape[1]
    CW = D // N_DEV
    tq = _tile(T, 256)

    h1 = rms_fwd(x, g_ffn1, "rms_ffn1", dep=first_dep)
    tiles = bias_tiles(rel_table, T, tq)
    forward("ffn1_gu", tiles)
    wgu1, = weights("ffn1_gu", h1)
    a1, b1, s1 = ffn_up(h1, wgu1, 0, 1, 2, "ffn1_up")
    forward("ffn1_d", s1)
    wd1, = weights("ffn1_d", s1)
    x1 = ffn_down(s1, wd1, 0, 1, x, "ffn1_down")

    h2 = rms_fwd(x1, g_mix, "rms_mix")
    forward("mix", h2)
    w3, wf, wo = weights("mix", h2)
    qkv = mm_nt([(h2, w3)], "mix_qkv", tn=768, out_dtype=BF16)
    uf = mm_nt([(h2, wf)], "mix_forget", tn=LANES, out_dtype=F32)
    bfp = jnp.pad(b_f.reshape(1, N_FOX), ((0, 0), (0, LANES - N_FOX)))
    c, ct = fox_gate_fwd(uf, bfp, "fox_gate")
    cat = attention_fwd(qkv, c, ct, tiles, "attention")
    forward("ffn2", cat)
    x2 = mm_nn(cat, wo, "mix_out", tn=512, out_dtype=F32, res=x1)

    h3 = rms_fwd(x2, g_ffn2, "rms_ffn2")
    wgu2, wd2 = weights("ffn2", h3)
    a2, b2, s2 = ffn_up(h3, wgu2, 0, 1, 2, "ffn2_up")
    forward("ple", s2)
    x3 = ffn_down(s2, wd2, 0, 1, x2, "ffn2_down")

    h4 = rms_fwd(x3, g_ple, "rms_ple")
    wpg, wpp = weights("ple", h4)
    z = mm_nn(h4, wpg, "ple_gate", tn=512, out_dtype=F32)
    pp = mm_nn(p, wpp, "ple_proj", tn=CW, out_dtype=F32, n_out=D,
               b_block=(P, CW), b_map=lambda n, i: (n, 0))
    x4 = ple_combine(x3, z, pp, "ple_combine")
    loss_row, dx4, dg_final = final_loss_bwd(x4, g_final, tgt, "final_loss")

    grads = {}
    dz, dpp = ple_bwd_elem(dx4, z, pp, "ple_bwd_elem")
    grads["w_ple_proj"] = mm_tn(p, dpp, "ple_proj_dw", grid=(N_DEV,),
                                a_block=(T, P), a_map=lambda n: (0, 0),
                                b_block=(T, CW), b_map=lambda n: (0, n),
                                o_block=(P, CW), o_map=lambda n: (n, 0),
                                out_shape=(N_DEV * P, CW))
    grads["w_ple_gate"] = mm_tn_plain(h4, dz, "ple_gate_dw")
    tok = emit("ple", grads)
    dh4 = mm_nt([(dz, wpg)], "ple_gate_dh", tn=512, out_dtype=F32, dep=tok)
    tok = emit2("ple", dh4)
    dx3, dg_ple = rms_bwd(dh4, x3, g_ple, dx4, "rms_ple_bwd", dep=tok)

    da2, db2 = ffn_bwd_act(dx3, wd2, 0, 1, a2, b2, "ffn2_bwd_act")
    grads["ffn2_w_down"] = ffn_bwd_dw_down(s2, dx3, "ffn2_down_dw")
    grads["ffn2_w_gate"] = ffn_bwd_dw_in(h3, da2, "ffn2_gate_dw")
    grads["ffn2_w_up"] = ffn_bwd_dw_in(h3, db2, "ffn2_up_dw")
    tok = emit("ffn2", grads)
    dh3 = ffn_bwd_dh(da2, db2, wgu2, 0, 1, 2, D, "ffn2_bwd_dh", dep=tok)
    tok = emit2("ffn2", dh3)
    dx2, dg_ffn2 = rms_bwd(dh3, x2, g_ffn2, dx3, "rms_ffn2_bwd", dep=tok)

    dcat = mm_nt([(dx2, wo)], "mix_out_dh", tn=512, out_dtype=BF16)
    grads["w_o"] = mm_tn_plain(cat, dx2, "mix_out_dw")
    dq, dk, dv, dct, dtiles = attention_bwd(qkv, c, ct, tiles, dcat, "attention_bwd")
    dctp = jnp.pad(dct[:, 0, :], ((0, LANES - N_HEADS), (0, 0)))
    duf, dbf = fox_gate_bwd(dctp, uf, bfp, "fox_gate_bwd")
    drel = rel_table_grad(dtiles, T, "rel_table_grad")[:, 0, :N_REL_BUCKETS].T
    du3 = jnp.concatenate([dq, dk, dv], axis=1)
    grads["w3"] = mm_tn_plain(du3, h2, "mix_qkv_dw", tm=768)
    grads["wf"] = mm_tn_plain(duf, h2, "mix_forget_dw", tm=LANES)
    tok = emit("mix", grads)
    dh2 = mm_nn_sum([(du3, w3), (duf, wf)], "mix_in_dh", tn=512, out_dtype=F32, dep=tok)
    tok = emit2("mix", dh2)
    dx1, dg_mix = rms_bwd(dh2, x1, g_mix, dx2, "rms_mix_bwd", dep=tok)

    da1, db1 = ffn_bwd_act(dx1, wd1, 0, 1, a1, b1, "ffn1_bwd_act")
    grads["ffn1_w_down"] = ffn_bwd_dw_down(s1, dx1, "ffn1_down_dw")
    tok = emit("ffn1_d", grads)
    grads["ffn1_w_gate"] = ffn_bwd_dw_in(h1, da1, "ffn1_gate_dw", dep=tok)
    tok = emit2("ffn1_d", grads["ffn1_w_gate"])
    tok = emit("ffn1_g", grads, after=tok)
    grads["ffn1_w_up"] = ffn_bwd_dw_in(h1, db1, "ffn1_up_dw", dep=tok)
    tok = emit2("ffn1_g", grads["ffn1_w_up"])
    tok = emit("ffn1_u", grads, after=tok)
    dh1 = ffn_bwd_dh(da1, db1, wgu1, 0, 1, 2, D, "ffn1_bwd_dh", dep=tok)
    tok = emit2("ffn1_u", dh1)
    dx0, dg_ffn1 = rms_bwd(dh1, x, g_ffn1, dx1, "rms_ffn1_bwd", dep=tok)

    small = _pack_small(D, dg_ffn1, dg_mix, dg_ffn2, dg_ple, dg_final, dbf[:, :N_FOX], drel)
    small = small.at[_ROW_LOSS, :LANES].set(loss_row[0])
    grads["small"] = small
    emit("small", grads)
    return dx0


def _split_w_in(w_in_t):
    df, dd = N_FOX * HEAD_DIM, N_DIL * HEAD_DIM
    o = np.cumsum([0, df, df, df, N_FOX, dd, dd, dd]).tolist()
    qa, ka, va, f, qb, kb, vb = [w_in_t[o[i]:o[i + 1]] for i in range(7)]
    return jnp.concatenate([qa, qb, ka, kb, va, vb], axis=0), f


def _join_w_in(d3, dfg):
    df, dd = N_FOX * HEAD_DIM, N_DIL * HEAD_DIM
    o = np.cumsum([0, df, dd, df, dd, df, dd]).tolist()
    qa, qb, ka, kb, va, vb = [d3[o[i]:o[i + 1]] for i in range(6)]
    return jnp.concatenate([qa, ka, va, dfg, qb, kb, vb], axis=0)


def rows_to_bf16(a3, name):
    R, _, C = a3.shape
    tc = _tile(C, 512)

    def body(a_ref, o_ref):
        o_ref[...] = a_ref[...].astype(BF16)

    return pl.pallas_call(
        body, name=name, grid=(C // tc,),
        in_specs=[pl.BlockSpec((R, None, tc), lambda n: (0, 0, n))],
        out_specs=pl.BlockSpec((R, tc), lambda n: (0, n)),
        out_shape=jax.ShapeDtypeStruct((R, C), BF16),
        compiler_params=_params(("parallel",)),
    )(a3)


def adamw_rows3d(parts, w3, m3, v3, name):
    R, _, C = w3.shape
    S = parts.shape[0]
    tc = _tile(C, 256)

    def body(p_ref, w_ref, m_ref, v_ref, g_ref, d_ref, nm_ref, nv_ref):
        g = p_ref[0].astype(F32)
        for s in range(1, S):
            g = g + p_ref[s].astype(F32)
        delta, nm, nv = _adamw_math(w_ref[...], g, m_ref[...], v_ref[...])
        g_ref[...] = g
        d_ref[...] = delta
        nm_ref[...] = nm
        nv_ref[...] = nv

    col = pl.BlockSpec((R, None, tc), lambda n: (0, 0, n))
    shp = jax.ShapeDtypeStruct((R, 1, C), F32)
    return pl.pallas_call(
        body, name=name, grid=(C // tc,),
        in_specs=[pl.BlockSpec((S, R, tc), lambda n: (0, 0, n)), col, col, col],
        out_specs=[col, col, col, col], out_shape=[shp, shp, shp, shp],
        compiler_params=_params(("parallel",)),
    )(parts, w3, m3, v3)


def kernel(x, p, norm_ffn1, ffn1_w_gate, ffn1_w_up, ffn1_w_down, norm_mix, w_in, b_f, w_o, norm_ffn2, ffn2_w_gate, ffn2_w_up, ffn2_w_down, norm_ple, w_ple_gate, w_ple_proj, rel_table, norm_final, loss_target, m_norm_ffn1, m_ffn1_w_gate, m_ffn1_w_up, m_ffn1_w_down, m_norm_mix, m_w_in, m_b_f, m_w_o, m_norm_ffn2, m_ffn2_w_gate, m_ffn2_w_up, m_ffn2_w_down, m_norm_ple, m_w_ple_gate, m_w_ple_proj, m_rel_table, m_norm_final, v_norm_ffn1, v_ffn1_w_gate, v_ffn1_w_up, v_ffn1_w_down, v_norm_mix, v_w_in, v_b_f, v_w_o, v_norm_ffn2, v_ffn2_w_gate, v_ffn2_w_up, v_ffn2_w_down, v_norm_ple, v_w_ple_gate, v_w_ple_proj, v_rel_table, v_norm_final):
    names = ["norm_ffn1", "ffn1_w_gate", "ffn1_w_up", "ffn1_w_down", "norm_mix", "w_in", "b_f", "w_o",
             "norm_ffn2", "ffn2_w_gate", "ffn2_w_up", "ffn2_w_down", "norm_ple", "w_ple_gate",
             "w_ple_proj", "rel_table", "norm_final"]
    w = dict(zip(names, [norm_ffn1, ffn1_w_gate, ffn1_w_up, ffn1_w_down, norm_mix, w_in, b_f, w_o,
                         norm_ffn2, ffn2_w_gate, ffn2_w_up, ffn2_w_down, norm_ple, w_ple_gate,
                         w_ple_proj, rel_table, norm_final]))
    m = dict(zip(names, [m_norm_ffn1, m_ffn1_w_gate, m_ffn1_w_up, m_ffn1_w_down, m_norm_mix, m_w_in,
                         m_b_f, m_w_o, m_norm_ffn2, m_ffn2_w_gate, m_ffn2_w_up, m_ffn2_w_down,
                         m_norm_ple, m_w_ple_gate, m_w_ple_proj, m_rel_table, m_norm_final]))
    v = dict(zip(names, [v_norm_ffn1, v_ffn1_w_gate, v_ffn1_w_up, v_ffn1_w_down, v_norm_mix, v_w_in,
                         v_b_f, v_w_o, v_norm_ffn2, v_ffn2_w_gate, v_ffn2_w_up, v_ffn2_w_down,
                         v_norm_ple, v_w_ple_gate, v_w_ple_proj, v_rel_table, v_norm_final]))
    sharded = ["ffn1_w_gate", "ffn1_w_up", "ffn1_w_down", "w_in", "w_o", "ffn2_w_gate", "ffn2_w_up",
               "ffn2_w_down", "w_ple_gate", "w_ple_proj"]
    small_names = [n for n in names if n not in sharded]

    xs, ps, tgt = x[0], p[0, 0], loss_target[0]
    T, D = xs.shape
    transposed = ("ffn1_w_gate", "ffn1_w_up", "ffn2_w_gate", "ffn2_w_up")

    def view(t, n):
        if n in transposed:
            return t[n][0].T
        if n == "w_in":
            return jnp.transpose(t[n], (2, 0, 1))
        return t[n][0]

    def unview(a, n):
        if n in transposed:
            return a.T.reshape(w[n].shape)
        if n == "w_in":
            return jnp.transpose(a, (1, 2, 0))
        return a.reshape(w[n].shape)

    sh = {n: view(w, n) for n in sharded}
    m_sh = {n: view(m, n) for n in sharded}
    v_sh = {n: view(v, n) for n in sharded}
    F8 = sh["ffn1_w_down"].shape[0]
    WIN8 = sh["w_in"].shape[0]
    w_in_bf = rows_to_bf16(sh["w_in"], "w_in_bf16")

    me = 4 * lax.axis_index("x") + 2 * lax.axis_index("y") + lax.axis_index("c")

    cat0 = lambda ns: jnp.concatenate([sh[n] for n in ns], axis=0).astype(BF16)
    gather_groups = {
        "ffn1_gu": [cat0(["ffn1_w_gate", "ffn1_w_up"])],
        "ffn1_d": [sh["ffn1_w_down"].astype(BF16)],
        "mix": [w_in_bf, sh["w_o"].astype(BF16)],
        "ffn2": [cat0(["ffn2_w_gate", "ffn2_w_up"]), sh["ffn2_w_down"].astype(BF16)],
        "ple": [sh["w_ple_gate"].astype(BF16), sh["w_ple_proj"].astype(BF16)],
    }
    order = ["ffn1_gu", "ffn1_d", "mix", "ffn2", "ple"]
    g_srcs = [s for grp in order for s in gather_groups[grp]]
    g_lands = [_landing(s, me) for s in g_srcs]
    n_g = len(g_srcs)
    g_sems, g_bufs, g_token = gather_start(g_srcs, g_lands, "gather_start")
    g_srcs, g_lands = g_bufs[:n_g], g_bufs[n_g:]
    g_send, g_recv_d, g_recv_i = g_sems[0::3], g_sems[1::3], g_sems[2::3]
    first = np.cumsum([0] + [len(gather_groups[grp]) for grp in order]).tolist()
    passed = {}

    def arrays_of(group):
        k = order.index(group)
        return slice(first[k], first[k + 1])

    def forward(group, after):
        sl = arrays_of(group)
        f_sems, bufs = gather_forward(g_srcs[sl], g_lands[sl], g_recv_i[sl], after, "gather_forward_" + group)
        k = len(bufs) // 2
        passed[group] = (f_sems, bufs[:k], bufs[k:])

    def weights(group, after):
        sl = arrays_of(group)
        f_sems, srcs, lands = passed[group]
        f_send, f_recv, r_send, r_recv = f_sems[0::4], f_sems[1::4], f_sems[2::4], f_sems[3::4]
        g_sems2, bufs = gather_forward_diagonal(srcs, lands, r_recv, after, "gather_forward_diagonal_" + group)
        k = len(bufs) // 2
        srcs, lands = bufs[:k], bufs[k:]
        per_array = list(zip(g_send[sl], g_recv_d[sl], f_send, f_recv, r_send, g_sems2[0::2], g_sems2[1::2]))
        got = gather_wait(srcs, lands, per_array, after, "gather_wait_" + group)
        if group == "ffn1_gu":
            return (got[0].reshape(N_DEV * 2 * F8, D),)
        if group == "ffn1_d":
            return (got[0].reshape(N_DEV * F8, D),)
        a0, a1 = got
        if group == "ffn2":
            return a0.reshape(N_DEV * 2 * F8, D), a1.reshape(N_DEV * F8, D)
        if group == "ple":
            return a0.reshape(-1, D), a1.reshape(-1, a1.shape[2])
        w3, wf8 = _split_w_in(a0.reshape(N_DEV * WIN8, D))
        return w3, jnp.pad(wf8, ((0, LANES - N_FOX), (0, 0))), a1.reshape(-1, D)

    scatter_groups = {
        "ple": ["w_ple_gate", "w_ple_proj"],
        "ffn2": ["ffn2_w_gate", "ffn2_w_up", "ffn2_w_down"],
        "mix": ["w_in", "w_o"],
        "ffn1_d": ["ffn1_w_down"],
        "ffn1_g": ["ffn1_w_gate"],
        "ffn1_u": ["ffn1_w_up"],
    }
    x_i, y_i, c_i = _xyc()
    my_chip = 2 * x_i + y_i
    pair_stage, chip_stage, small_stage = {}, {}, {}

    def emit(group, grads, after=None):
        if group == "small":
            src = grads["small"]
            ss, rs, srcs, lands, token = exchange_start([src], [_landing(src, me)], [False], "scatter_start_small")
            small_stage["small"] = (ss, rs, srcs, lands)
            return token
        src4s = []
        for n in scatter_groups[group]:
            if n == "w_in":
                full = _join_w_in(grads["w3"], grads["wf"][:N_FOX])
                src4s.append(full.reshape(N_CHIPS, 2, WIN8, D))
            else:
                src4s.append(grads[n].reshape((N_CHIPS, 2) + sh[n].shape))
        lands = [lax.empty((N_CHIPS,) + s.shape[2:], BF16) for s in src4s]
        sems, bufs, token = scatter_pair_start(src4s, lands, "scatter_pair_start_" + group, after=after)
        k = len(src4s)
        pair_stage[group] = (sems, bufs[:k], bufs[k:])
        return token

    def emit2(group, after):
        sems, src4s, lands = pair_stage[group]
        src4s, lands = scatter_pair_wait(src4s, lands, sems, after, "scatter_pair_wait_" + group)
        sums = [chip_sum(s4, la, c_i, "chip_sum_" + n)
                for s4, la, n in zip(src4s, lands, scatter_groups[group])]
        chip_lands = [_landing(lax.dynamic_index_in_dim(s, my_chip, 0, keepdims=False), my_chip, slots=N_CHIPS)
                      for s in sums]
        sems, bufs, token = scatter_chip_start(sums, chip_lands, "scatter_chip_start_" + group)
        k = len(sums)
        chip_stage[group] = (sems, bufs[:k], bufs[k:])
        return token

    dx0 = local_step(
        xs, ps, tgt, w["norm_ffn1"], w["norm_mix"], w["norm_ffn2"], w["norm_ple"],
        w["norm_final"].reshape(1, D), w["b_f"], w["rel_table"], forward, weights, emit, emit2, g_token)

    res = {}
    after = dx0
    for group in ["ple", "ffn2", "mix", "ffn1_d", "ffn1_g", "ffn1_u"]:
        sems, sums, chip_lands = chip_stage[group]
        _, parts = scatter_chip_wait(sums, chip_lands, sems, after, "scatter_chip_wait_" + group)
        for n, part in zip(scatter_groups[group], parts):
            update = adamw_rows3d if n == "w_in" else adamw_sharded
            g, d, nm, nv = update(part, sh[n], m_sh[n], v_sh[n], "adamw_" + n)
            res[n] = tuple(unview(a, n) for a in (g, d, nm, nv))
            after = g
    ss, rs, srcs, lands = small_stage["small"]
    small_parts, = exchange_wait(ss, rs, srcs, lands, [False], after, "scatter_wait_small")
    pack = lambda t: _pack_small(D, t["norm_ffn1"], t["norm_mix"], t["norm_ffn2"], t["norm_ple"],
                                 t["norm_final"], t["b_f"], t["rel_table"])
    gs, ds, ms, vs = adamw_small(small_parts, pack(w), pack(m), pack(v), "adamw_small")
    shapes = {n: w[n].shape for n in small_names}
    unpacked = [_unpack_small(a, shapes) for a in (gs, ds, ms, vs)]
    for n in small_names:
        res[n] = tuple(u[n] for u in unpacked)
    loss = gs[_ROW_LOSS, 0]

    out = [loss, dx0.reshape(x.shape)]
    for k in range(4):
        out += [res[n][k] for n in names]
    return tuple(out)
```

```python
import functools
import math

import numpy as np
import jax
import jax.numpy as jnp
from jax import lax
from jax.experimental import pallas as pl
from jax.experimental.pallas import tpu as pltpu

F32 = jnp.float32
BF16 = jnp.bfloat16

N_DEV = 8
HEAD_DIM = 128
N_FOX = 8
N_DIL = 8
N_HEADS = N_FOX + N_DIL
DILATED_PATTERNS = ((128, 1), (512, 4), (2048, 16))
N_REL_BUCKETS = 32
REL_MAX_DISTANCE = 2048
RMS_EPS = 1e-6
NEG_INF = -1e30
LANES = 128
VMEM_LIMIT = 56 * 1024 * 1024

ADAM_LR = 0.001
ADAM_B1 = 0.9
ADAM_B2 = 0.999
ADAM_EPS = 1e-08
ADAM_WD = 0.01
ADAM_STEP = 10

MESH = pl.DeviceIdType.MESH


def _params(sem):
    return pltpu.CompilerParams(dimension_semantics=sem, vmem_limit_bytes=VMEM_LIMIT)


def _dot(a, b, ca, cb, precision=None):
    return lax.dot_general(a, b, (((ca,), (cb,)), ((), ())),
                           preferred_element_type=F32, precision=precision)


def _sigmoid(z):
    return 1.0 / (1.0 + jnp.exp(-z))


def _tile(n, want):
    t = min(n, want)
    assert n % t == 0, (n, t)
    return t


def _dep_spec(ngrid):
    return pl.BlockSpec((8, LANES), lambda *_: (0, 0))


def rms_fwd(x, g, name, dep=None):
    T, D = x.shape
    tm = _tile(T, 256)

    def body(x_ref, g_ref, *rest):
        h_ref = rest[-1]
        xv = x_ref[...]
        r = lax.rsqrt(jnp.mean(xv * xv, axis=-1, keepdims=True) + RMS_EPS)
        h_ref[...] = (xv * r * g_ref[...]).astype(BF16)

    in_specs = [pl.BlockSpec((tm, D), lambda i: (i, 0)), pl.BlockSpec((1, D), lambda i: (0, 0))]
    args = [x, g]
    if dep is not None:
        in_specs.append(_dep_spec(1))
        args.append(dep)
    return pl.pallas_call(
        body, name=name, grid=(T // tm,), in_specs=in_specs,
        out_specs=pl.BlockSpec((tm, D), lambda i: (i, 0)),
        out_shape=jax.ShapeDtypeStruct((T, D), BF16),
        compiler_params=_params(("parallel",)),
    )(*args)


def rms_bwd(dh, x, g, dres, name, dep=None):
    T, D = x.shape
    tm = _tile(T, 256)

    def body(dh_ref, x_ref, g_ref, dres_ref, *rest):
        dx_ref, dxh_ref, dg_ref = rest[-3], rest[-2], rest[-1]
        i = pl.program_id(0)
        xv = x_ref[...]
        r = lax.rsqrt(jnp.mean(xv * xv, axis=-1, keepdims=True) + RMS_EPS)
        xh = xv * r
        d = dh_ref[...]
        u = d * g_ref[...]
        dx = dres_ref[...] + r * (u - xh * jnp.mean(u * xh, axis=-1, keepdims=True))
        dx_ref[...] = dx
        dxh_ref[...] = (0.5 * dx).astype(BF16)
        part = jnp.sum(d * xh, axis=0, keepdims=True)

        @pl.when(i == 0)
        def _():
            dg_ref[...] = part

        @pl.when(i > 0)
        def _():
            dg_ref[...] += part

    row = pl.BlockSpec((tm, D), lambda i: (i, 0))
    vec = pl.BlockSpec((1, D), lambda i: (0, 0))
    in_specs = [row, row, vec, row]
    args = [dh, x, g, dres]
    if dep is not None:
        in_specs.append(_dep_spec(1))
        args.append(dep)
    return pl.pallas_call(
        body, name=name, grid=(T // tm,),
        in_specs=in_specs, out_specs=[row, row, vec],
        out_shape=[jax.ShapeDtypeStruct((T, D), F32), jax.ShapeDtypeStruct((T, D), BF16),
                   jax.ShapeDtypeStruct((1, D), F32)],
        compiler_params=_params(("arbitrary",)),
    )(*args)


def final_loss_bwd(x, g, target, name):
    T, D = x.shape
    tm = _tile(T, 256)

    def body(x_ref, g_ref, t_ref, loss_ref, dx_ref, dg_ref):
        i = pl.program_id(0)
        xv = x_ref[...]
        gv = g_ref[...]
        r = lax.rsqrt(jnp.mean(xv * xv, axis=-1, keepdims=True) + RMS_EPS)
        xh = xv * r
        e = xh * gv - t_ref[...]
        lpart = 0.5 * jnp.sum(jnp.mean(e * e, axis=-1, keepdims=True), axis=0, keepdims=True)
        lrow = jnp.broadcast_to(lpart, (1, LANES))
        d = e * (1.0 / D)
        u = d * gv
        dx_ref[...] = r * (u - xh * jnp.mean(u * xh, axis=-1, keepdims=True))
        part = jnp.sum(d * xh, axis=0, keepdims=True)

        @pl.when(i == 0)
        def _():
            dg_ref[...] = part
            loss_ref[...] = lrow

        @pl.when(i > 0)
        def _():
            dg_ref[...] += part
            loss_ref[...] += lrow

    row = pl.BlockSpec((tm, D), lambda i: (i, 0))
    vec = pl.BlockSpec((1, D), lambda i: (0, 0))
    return pl.pallas_call(
        body, name=name, grid=(T // tm,),
        in_specs=[row, vec, row],
        out_specs=[pl.BlockSpec((1, LANES), lambda i: (0, 0)), row, vec],
        out_shape=[jax.ShapeDtypeStruct((1, LANES), F32), jax.ShapeDtypeStruct((T, D), F32),
                   jax.ShapeDtypeStruct((1, D), F32)],
        compiler_params=_params(("arbitrary",)),
    )(x, g, target)


def _bf(v, scale=None):
    if scale is not None:
        v = v * scale
    return v.astype(BF16)


def mm_nn(a, b, name, *, tn, out_dtype, tm=512, n_out=None, b_block=None, b_map=None,
          res=None):
    T, K = a.shape
    N = n_out if n_out is not None else b.shape[1]
    tm = _tile(T, tm)
    tn = _tile(N, tn)
    b_block = b_block or (K, tn)
    b_map = b_map or (lambda n, i: (0, n))

    def body(*refs):
        a_ref, b_ref = refs[0], refs[1]
        o_ref = refs[-1]
        acc = _dot(_bf(a_ref[...]), _bf(b_ref[...]), 1, 0)
        if res is not None:
            acc = refs[2][...] + acc
        o_ref[...] = acc.astype(out_dtype)

    in_specs = [pl.BlockSpec((tm, K), lambda n, i: (i, 0)), pl.BlockSpec(b_block, b_map)]
    args = [a, b]
    if res is not None:
        in_specs.append(pl.BlockSpec((tm, tn), lambda n, i: (i, n)))
        args.append(res)
    return pl.pallas_call(
        body, name=name, grid=(N // tn, T // tm), in_specs=in_specs,
        out_specs=pl.BlockSpec((tm, tn), lambda n, i: (i, n)),
        out_shape=jax.ShapeDtypeStruct((T, N), out_dtype),
        compiler_params=_params(("parallel", "parallel")),
    )(*args)


def mm_nn_sum(pairs, name, *, tn, out_dtype, tm=512, dep=None):
    T = pairs[0][0].shape[0]
    N = pairs[0][1].shape[1]
    tm = _tile(T, tm)
    tn = _tile(N, tn)
    npair = len(pairs)

    def body(*refs):
        acc = None
        for q in range(npair):
            part = _dot(_bf(refs[2 * q][...]), _bf(refs[2 * q + 1][...]), 1, 0)
            acc = part if acc is None else acc + part
        refs[-1][...] = acc.astype(out_dtype)

    in_specs, args = [], []
    for a, b in pairs:
        K = a.shape[1]
        in_specs += [pl.BlockSpec((tm, K), lambda n, i: (i, 0)), pl.BlockSpec((K, tn), lambda n, i: (0, n))]
        args += [a, b]
    if dep is not None:
        in_specs.append(_dep_spec(2))
        args.append(dep)
    return pl.pallas_call(
        body, name=name, grid=(N // tn, T // tm), in_specs=in_specs,
        out_specs=pl.BlockSpec((tm, tn), lambda n, i: (i, n)),
        out_shape=jax.ShapeDtypeStruct((T, N), out_dtype),
        compiler_params=_params(("parallel", "parallel")),
    )(*args)


def mm_nt(pairs, name, *, tn, out_dtype, tm=512, dep=None):
    T = pairs[0][0].shape[0]
    N = pairs[0][1].shape[0]
    tm = _tile(T, tm)
    tn = _tile(N, tn)
    npair = len(pairs)

    def body(*refs):
        o_ref = refs[-1]
        acc = None
        for q in range(npair):
            part = _dot(_bf(refs[2 * q][...]), _bf(refs[2 * q + 1][...]), 1, 1)
            acc = part if acc is None else acc + part
        o_ref[...] = acc.astype(out_dtype)

    in_specs, args = [], []
    for a, b in pairs:
        K = a.shape[1]
        in_specs += [pl.BlockSpec((tm, K), lambda n, i: (i, 0)), pl.BlockSpec((tn, K), lambda n, i: (n, 0))]
        args += [a, b]
    if dep is not None:
        in_specs.append(_dep_spec(2))
        args.append(dep)
    return pl.pallas_call(
        body, name=name, grid=(N // tn, T // tm), in_specs=in_specs,
        out_specs=pl.BlockSpec((tm, tn), lambda n, i: (i, n)),
        out_shape=jax.ShapeDtypeStruct((T, N), out_dtype),
        compiler_params=_params(("parallel", "parallel")),
    )(*args)


def mm_tn(a, b, name, *, grid, a_block, a_map, b_block, b_map, o_block, o_map, out_shape,
          b_scale=None, dep=None):
    def body(a_ref, b_ref, *rest):
        rest[-1][...] = _dot(_bf(a_ref[...]), _bf(b_ref[...], b_scale), 0, 0).astype(BF16)

    in_specs = [pl.BlockSpec(a_block, a_map), pl.BlockSpec(b_block, b_map)]
    args = [a, b]
    if dep is not None:
        in_specs.append(_dep_spec(len(grid)))
        args.append(dep)
    return pl.pallas_call(
        body, name=name, grid=grid, in_specs=in_specs,
        out_specs=pl.BlockSpec(o_block, o_map),
        out_shape=jax.ShapeDtypeStruct(out_shape, BF16),
        compiler_params=_params(("parallel",) * len(grid)),
    )(*args)


def mm_tn_plain(a, b, name, *, tm=512, tn=512, b_scale=None):
    T, M = a.shape
    N = b.shape[1]
    tm = _tile(M, tm)
    tn = _tile(N, tn)
    return mm_tn(a, b, name, grid=(M // tm, N // tn),
                 a_block=(T, tm), a_map=lambda m, n: (0, m),
                 b_block=(T, tn), b_map=lambda m, n: (0, n),
                 o_block=(tm, tn), o_map=lambda m, n: (m, n),
                 out_shape=(M, N), b_scale=b_scale)


def ffn_up(h, wgu, gi, ui, nper, name):
    T, D = h.shape
    F8 = wgu.shape[0] // (N_DEV * nper)
    tm = _tile(T, 512)
    nt = T // tm

    def body(h_ref, wg_ref, wu_ref, ga_ref, gb_ref, s_ref):
        hv = h_ref[...]
        a = _dot(hv, wg_ref[...], 1, 1)
        b = _dot(hv, wu_ref[...], 1, 1)
        sg = _sigmoid(a)
        silu = a * sg
        ga_ref[...] = (b * (sg * (1.0 + a * (1.0 - sg)))).astype(BF16)
        gb_ref[...] = silu.astype(BF16)
        s_ref[...] = (silu * b).astype(BF16)

    blk = pl.BlockSpec((tm, F8), lambda j, i: (j * nt + i, 0))
    shp = jax.ShapeDtypeStruct((N_DEV * T, F8), BF16)
    return pl.pallas_call(
        body, name=name, grid=(N_DEV, nt),
        in_specs=[pl.BlockSpec((tm, D), lambda j, i: (i, 0)),
                  pl.BlockSpec((F8, D), lambda j, i: (j * nper + gi, 0)),
                  pl.BlockSpec((F8, D), lambda j, i: (j * nper + ui, 0))],
        out_specs=[blk, blk, blk], out_shape=[shp, shp, shp],
        compiler_params=_params(("parallel", "parallel")),
    )(h, wgu, wgu)


def ffn_down(s, wd, di, nper, x, name):
    T, D = x.shape
    F8 = s.shape[1]
    tm = _tile(T, 512)
    nt = T // tm

    def body(s_ref, w_ref, x_ref, o_ref, acc_ref):
        j = pl.program_id(1)
        part = _dot(s_ref[...], w_ref[...], 1, 0)

        @pl.when(j == 0)
        def _():
            acc_ref[...] = part

        @pl.when(j > 0)
        def _():
            acc_ref[...] += part

        @pl.when(j == N_DEV - 1)
        def _():
            o_ref[...] = x_ref[...] + 0.5 * acc_ref[...]

    return pl.pallas_call(
        body, name=name, grid=(nt, N_DEV),
        in_specs=[pl.BlockSpec((tm, F8), lambda i, j: (j * nt + i, 0)),
                  pl.BlockSpec((F8, D), lambda i, j: (j * nper + di, 0)),
                  pl.BlockSpec((tm, D), lambda i, j: (i, 0))],
        out_specs=pl.BlockSpec((tm, D), lambda i, j: (i, 0)),
        out_shape=jax.ShapeDtypeStruct((T, D), F32),
        scratch_shapes=[pltpu.VMEM((tm, D), F32)],
        compiler_params=_params(("parallel", "arbitrary")),
    )(s, wd, x)


def ffn_bwd_act(dxh, wd, di, nper_d, a, b, name, dep=None):
    T, D = dxh.shape
    F8 = a.shape[1]
    tm = _tile(T, 512)
    nt = T // tm

    def body(dx_ref, w_ref, a_ref, b_ref, *rest):
        da_ref, db_ref = rest[-2], rest[-1]
        ds = _dot(dx_ref[...], w_ref[...], 1, 1)
        da_ref[...] = (ds * a_ref[...].astype(F32)).astype(BF16)
        db_ref[...] = (ds * b_ref[...].astype(F32)).astype(BF16)

    blk = pl.BlockSpec((tm, F8), lambda j, i: (j * nt + i, 0))
    shp = jax.ShapeDtypeStruct((N_DEV * T, F8), BF16)
    in_specs = [pl.BlockSpec((tm, D), lambda j, i: (i, 0)),
                pl.BlockSpec((F8, D), lambda j, i: (j * nper_d + di, 0)), blk, blk]
    args = [dxh, wd, a, b]
    if dep is not None:
        in_specs.append(_dep_spec(2))
        args.append(dep)
    return pl.pallas_call(
        body, name=name, grid=(N_DEV, nt), in_specs=in_specs,
        out_specs=[blk, blk], out_shape=[shp, shp],
        compiler_params=_params(("parallel", "parallel")),
    )(*args)


def ffn_bwd_dh(da, db, wgu, gi, ui, nper, D, name, dep=None):
    F8 = da.shape[1]
    T = da.shape[0] // N_DEV
    tm = _tile(T, 512)
    nt = T // tm

    def body(da_ref, db_ref, wg_ref, wu_ref, *rest):
        o_ref, acc_ref = rest[-2], rest[-1]
        j = pl.program_id(1)
        part = _dot(da_ref[...], wg_ref[...], 1, 0) + _dot(db_ref[...], wu_ref[...], 1, 0)

        @pl.when(j == 0)
        def _():
            acc_ref[...] = part

        @pl.when(j > 0)
        def _():
            acc_ref[...] += part

        @pl.when(j == N_DEV - 1)
        def _():
            o_ref[...] = acc_ref[...]

    blk = pl.BlockSpec((tm, F8), lambda i, j: (j * nt + i, 0))
    in_specs = [blk, blk,
                pl.BlockSpec((F8, D), lambda i, j: (j * nper + gi, 0)),
                pl.BlockSpec((F8, D), lambda i, j: (j * nper + ui, 0))]
    args = [da, db, wgu, wgu]
    if dep is not None:
        in_specs.append(_dep_spec(2))
        args.append(dep)
    return pl.pallas_call(
        body, name=name, grid=(nt, N_DEV), in_specs=in_specs,
        out_specs=pl.BlockSpec((tm, D), lambda i, j: (i, 0)),
        out_shape=jax.ShapeDtypeStruct((T, D), F32),
        scratch_shapes=[pltpu.VMEM((tm, D), F32)],
        compiler_params=_params(("parallel", "arbitrary")),
    )(*args)


def ffn_bwd_dw_in(h, dact, name, dep=None):
    T, D = h.shape
    F8 = dact.shape[1]
    tm = _tile(D, 512)
    return mm_tn(dact, h, name, grid=(N_DEV, D // tm),
                 a_block=(T, F8), a_map=lambda j, m: (j, 0),
                 b_block=(T, tm), b_map=lambda j, m: (0, m),
                 o_block=(F8, tm), o_map=lambda j, m: (j, m),
                 out_shape=(N_DEV * F8, D), dep=dep)


def ffn_bwd_dw_down(s, dx, name):
    T, D = dx.shape
    F8 = s.shape[1]
    tn = _tile(D, 512)
    return mm_tn(s, dx, name, grid=(N_DEV, D // tn),
                 a_block=(T, F8), a_map=lambda j, n: (j, 0),
                 b_block=(T, tn), b_map=lambda j, n: (0, n),
                 o_block=(F8, tn), o_map=lambda j, n: (j, n),
                 out_shape=(N_DEV * F8, D))


def _t5_bucket_np(dist):
    max_exact = N_REL_BUCKETS // 2
    d = np.maximum(dist, 1).astype(np.float64)
    large = max_exact + (np.log(d / max_exact) / math.log(REL_MAX_DISTANCE / max_exact)
                         * (N_REL_BUCKETS - max_exact)).astype(np.int64)
    large32 = max_exact + (np.log(d.astype(np.float32) / np.float32(max_exact))
                           / np.float32(math.log(REL_MAX_DISTANCE / max_exact))
                           * np.float32(N_REL_BUCKETS - max_exact)).astype(np.int64)
    assert np.array_equal(large, large32)
    large = np.minimum(large, N_REL_BUCKETS - 1)
    return np.where(dist < max_exact, dist, large)


def _distance_tables(T, tq):
    dist = np.arange(T)
    mult = np.zeros(T, np.int64)
    for window, dilation in DILATED_PATTERNS:
        mult += ((dist % dilation == 0) & (dist // dilation <= window // dilation)).astype(np.int64)
    logm = np.where(mult > 0, np.log(np.maximum(mult, 1)), NEG_INF).astype(np.float32)
    bucket = _t5_bucket_np(dist).astype(np.int32)
    nkb = T // tq
    k = np.arange(nkb)[:, None, None]
    r = np.arange(tq)[None, :, None]
    c = np.arange(tq)[None, None, :]
    delta = k * tq + r - c
    return bucket, logm, delta


def _tile_buckets(T, tq):
    bucket, logm, delta = _distance_tables(T, tq)
    safe = np.maximum(delta, 0)
    bidx = np.where(delta >= 0, bucket[safe], -1).astype(np.int32)
    logm_t = np.where(delta >= 0, logm[safe], NEG_INF).astype(np.float32)
    present = [sorted(set(np.unique(bidx[k]).tolist()) - {-1}) for k in range(T // tq)]
    return bidx, logm_t, present


def bias_tiles(rel_table, T, tq):
    bidx, logm_t, present = _tile_buckets(T, tq)
    nkb = T // tq

    def body(tab_ref, b_ref, lm_ref, o_ref):
        slot = pl.program_id(0)

        @pl.when(slot == 0)
        def _():
            o_ref[...] = jnp.where(b_ref[...] >= 0, 0.0, NEG_INF)

        @pl.when(slot > 0)
        def _():
            for k in range(nkb):
                bi = b_ref[k]
                acc = lm_ref[k]
                for b in present[k]:
                    acc = acc + jnp.where(bi == b, tab_ref[b, slot - 1], 0.0)
                o_ref[k] = acc

    full = pl.BlockSpec((nkb, tq, tq), lambda s: (0, 0, 0))
    return pl.pallas_call(
        body, name="bias_tiles", grid=(1 + N_DIL,),
        in_specs=[pl.BlockSpec(memory_space=pltpu.SMEM), full, full],
        out_specs=pl.BlockSpec((None, nkb, tq, tq), lambda s: (s, 0, 0, 0)),
        out_shape=jax.ShapeDtypeStruct((1 + N_DIL, nkb, tq, tq), F32),
        compiler_params=_params(("parallel",)),
    )(rel_table, jnp.asarray(bidx), jnp.asarray(logm_t))


def fox_gate_fwd(uf, bf, name):
    T = uf.shape[0]
    tb = _tile(T, 512)

    def body(u_ref, b_ref, c_ref, ct_ref):
        lane = lax.broadcasted_iota(jnp.int32, (1, LANES), 1)
        tri = (lax.broadcasted_iota(jnp.int32, (tb, tb), 0)
               >= lax.broadcasted_iota(jnp.int32, (tb, tb), 1)).astype(F32)
        carry = jnp.zeros((1, LANES), F32)
        for blk in range(T // tb):
            z = u_ref[pl.ds(blk * tb, tb), :] + b_ref[...]
            lf = jnp.minimum(z, 0.0) - jnp.log1p(jnp.exp(-jnp.abs(z)))
            lf = jnp.where(lane < N_FOX, lf, 0.0)
            cb = _dot(tri, lf, 1, 0, precision=lax.Precision.HIGHEST) + carry
            c_ref[pl.ds(blk * tb, tb), :] = cb
            ct_ref[:, pl.ds(blk * tb, tb)] = cb.T
            carry = cb[tb - 1:tb, :]

    return pl.pallas_call(
        body, name=name,
        out_shape=[jax.ShapeDtypeStruct((T, LANES), F32), jax.ShapeDtypeStruct((LANES, T), F32)],
        compiler_params=_params(None),
    )(uf, bf)


def fox_gate_bwd(dct, uf, bf, name):
    T = uf.shape[0]
    tb = _tile(T, 512)

    def body(d_ref, u_ref, b_ref, du_ref, db_ref):
        lane = lax.broadcasted_iota(jnp.int32, (1, LANES), 1)
        triu = (lax.broadcasted_iota(jnp.int32, (tb, tb), 0)
                <= lax.broadcasted_iota(jnp.int32, (tb, tb), 1)).astype(F32)
        carry = jnp.zeros((1, LANES), F32)
        dbv = jnp.zeros((1, LANES), F32)
        for blk in reversed(range(T // tb)):
            dc = d_ref[:, pl.ds(blk * tb, tb)].T
            dlf = _dot(triu, dc, 1, 0, precision=lax.Precision.HIGHEST) + carry
            carry = dlf[0:1, :]
            z = u_ref[pl.ds(blk * tb, tb), :] + b_ref[...]
            dz = jnp.where(lane < N_FOX, dlf * (1.0 - _sigmoid(z)), 0.0)
            du_ref[pl.ds(blk * tb, tb), :] = dz
            dbv = dbv + jnp.sum(dz, axis=0, keepdims=True)
        db_ref[...] = dbv

    return pl.pallas_call(
        body, name=name,
        out_shape=[jax.ShapeDtypeStruct((T, LANES), F32), jax.ShapeDtypeStruct((1, LANES), F32)],
        compiler_params=_params(None),
    )(dct, uf, bf)


def _bias_slot(h):
    return jnp.maximum(h - (N_FOX - 1), 0)


def _scores(q_ref, k_ref, c_ref, ct_ref, tb_ref, h, i, tq):
    scale = HEAD_DIM ** -0.5
    n = (i + 1) * tq
    rows = pl.ds(i * tq, tq)
    s = _dot(q_ref[rows, :], k_ref[pl.ds(0, n), :], 1, 1) * scale
    lane = lax.broadcasted_iota(jnp.int32, (1, LANES), 1)
    c_col = jnp.sum(jnp.where(lane == h, c_ref[rows, :], 0.0), axis=1, keepdims=True)
    c_row = ct_ref[pl.ds(h, 1), pl.ds(0, n)]
    bias = jnp.concatenate([tb_ref[i - jb] for jb in range(i + 1)], axis=1)
    return s + (c_col - c_row) + bias


def _attn_specs(T, tq):
    nkb = T // tq
    return [
        pl.BlockSpec((T, HEAD_DIM), lambda h: (0, h)),
        pl.BlockSpec((T, HEAD_DIM), lambda h: (0, N_HEADS + h)),
        pl.BlockSpec((T, HEAD_DIM), lambda h: (0, 2 * N_HEADS + h)),
        pl.BlockSpec((T, LANES), lambda h: (0, 0)),
        pl.BlockSpec((LANES, T), lambda h: (0, 0)),
        pl.BlockSpec((None, nkb, tq, tq), lambda h: (_bias_slot(h), 0, 0, 0)),
    ]


def attention_fwd(qkv, c, ct, tiles, name):
    T = qkv.shape[0]
    tq = tiles.shape[2]

    def body(q_ref, k_ref, v_ref, c_ref, ct_ref, tb_ref, o_ref):
        h = pl.program_id(0)
        for i in range(T // tq):
            s = _scores(q_ref, k_ref, c_ref, ct_ref, tb_ref, h, i, tq)
            p = jnp.exp(s - jnp.max(s, axis=1, keepdims=True))
            l = jnp.sum(p, axis=1, keepdims=True)
            o = _dot((p / l).astype(BF16), v_ref[pl.ds(0, (i + 1) * tq), :], 1, 0)
            o_ref[pl.ds(i * tq, tq), :] = o.astype(BF16)

    return pl.pallas_call(
        body, name=name, grid=(N_HEADS,),
        in_specs=_attn_specs(T, tq),
        out_specs=pl.BlockSpec((T, HEAD_DIM), lambda h: (0, h)),
        out_shape=jax.ShapeDtypeStruct((T, N_HEADS * HEAD_DIM), BF16),
        compiler_params=_params(("parallel",)),
    )(qkv, qkv, qkv, c, ct, tiles)


def attention_bwd(qkv, c, ct, tiles, do, name):
    T = qkv.shape[0]
    tq = tiles.shape[2]
    nkb = T // tq
    scale = HEAD_DIM ** -0.5

    def body(q_ref, k_ref, v_ref, c_ref, ct_ref, tb_ref, do_ref,
             dq_ref, dk_ref, dv_ref, dct_ref, dtb_ref, dk_acc, dv_acc):
        h = pl.program_id(0)
        dk_acc[...] = jnp.zeros_like(dk_acc)
        dv_acc[...] = jnp.zeros_like(dv_acc)
        dct_ref[...] = jnp.zeros_like(dct_ref)
        dtb_ref[...] = jnp.zeros_like(dtb_ref)
        for i in range(nkb):
            rows, keys = pl.ds(i * tq, tq), pl.ds(0, (i + 1) * tq)
            s = _scores(q_ref, k_ref, c_ref, ct_ref, tb_ref, h, i, tq)
            p = jnp.exp(s - jnp.max(s, axis=1, keepdims=True))
            p = p / jnp.sum(p, axis=1, keepdims=True)
            dov = do_ref[rows, :]
            dp = _dot(dov, v_ref[keys, :], 1, 1)
            ds = p * (dp - jnp.sum(p * dp, axis=1, keepdims=True))
            ds_b = ds.astype(BF16)
            dq_ref[rows, :] = (_dot(ds_b, k_ref[keys, :], 1, 0) * scale).astype(BF16)
            dk_acc[keys, :] += _dot(ds_b, q_ref[rows, :], 0, 0) * scale
            dv_acc[keys, :] += _dot(p.astype(BF16), dov, 0, 0)
            dct_ref[:, keys] += -jnp.sum(ds, axis=0, keepdims=True)
            for jb in range(i + 1):
                dtb_ref[i - jb] += ds[:, jb * tq:(jb + 1) * tq]
        dk_ref[...] = dk_acc[...].astype(BF16)
        dv_ref[...] = dv_acc[...].astype(BF16)

    head_cols = jax.ShapeDtypeStruct((T, N_HEADS * HEAD_DIM), BF16)
    col = pl.BlockSpec((T, HEAD_DIM), lambda h: (0, h))
    return pl.pallas_call(
        body, name=name, grid=(N_HEADS,),
        in_specs=_attn_specs(T, tq) + [col],
        out_specs=[col, col, col,
                   pl.BlockSpec((None, 1, T), lambda h: (h, 0, 0)),
                   pl.BlockSpec((None, nkb, tq, tq), lambda h: (_bias_slot(h), 0, 0, 0))],
        out_shape=[head_cols, head_cols, head_cols,
                   jax.ShapeDtypeStruct((N_HEADS, 1, T), F32),
                   jax.ShapeDtypeStruct((1 + N_DIL, nkb, tq, tq), F32)],
        scratch_shapes=[pltpu.VMEM((T, HEAD_DIM), F32), pltpu.VMEM((T, HEAD_DIM), F32)],
        compiler_params=_params(("arbitrary",)),
    )(qkv, qkv, qkv, c, ct, tiles, do)


def rel_table_grad(dtiles, T, name):
    tq = dtiles.shape[2]
    nkb = T // tq
    bidx, _, present = _tile_buckets(T, tq)

    def body(d_ref, b_ref, o_ref):
        lane = lax.broadcasted_iota(jnp.int32, (1, LANES), 1)
        row = jnp.zeros((1, LANES), F32)
        for k in range(nkb):
            d = d_ref[k]
            bi = b_ref[k]
            for b in present[k]:
                v = jnp.sum(jnp.sum(jnp.where(bi == b, d, 0.0), axis=0, keepdims=True),
                            axis=1, keepdims=True)
                row = row + jnp.where(lane == b, v, 0.0)
        o_ref[...] = row

    return pl.pallas_call(
        body, name=name, grid=(N_DIL,),
        in_specs=[pl.BlockSpec((None, nkb, tq, tq), lambda h: (h + 1, 0, 0, 0)),
                  pl.BlockSpec((nkb, tq, tq), lambda h: (0, 0, 0))],
        out_specs=pl.BlockSpec((None, 1, LANES), lambda h: (h, 0, 0)),
        out_shape=jax.ShapeDtypeStruct((N_DIL, 1, LANES), F32),
        compiler_params=_params(("parallel",)),
    )(dtiles, jnp.asarray(bidx))


def ple_combine(x, z, pp, name):
    T, D = x.shape
    tm = _tile(T, 256)

    def body(x_ref, z_ref, p_ref, o_ref):
        o_ref[...] = x_ref[...] + _sigmoid(z_ref[...]) * p_ref[...]

    row = pl.BlockSpec((tm, D), lambda i: (i, 0))
    return pl.pallas_call(
        body, name=name, grid=(T // tm,), in_specs=[row, row, row], out_specs=row,
        out_shape=jax.ShapeDtypeStruct((T, D), F32), compiler_params=_params(("parallel",)),
    )(x, z, pp)


def ple_bwd_elem(dx, z, pp, name):
    T, D = dx.shape
    tm = _tile(T, 256)

    def body(dx_ref, z_ref, p_ref, dz_ref, dp_ref):
        gate = _sigmoid(z_ref[...])
        d = dx_ref[...]
        dz_ref[...] = (d * p_ref[...] * gate * (1.0 - gate)).astype(BF16)
        dp_ref[...] = (d * gate).astype(BF16)

    row = pl.BlockSpec((tm, D), lambda i: (i, 0))
    shp = jax.ShapeDtypeStruct((T, D), BF16)
    return pl.pallas_call(
        body, name=name, grid=(T // tm,), in_specs=[row, row, row], out_specs=[row, row],
        out_shape=[shp, shp], compiler_params=_params(("parallel",)),
    )(dx, z, pp)


def _peer_list():
    x, y, c = lax.axis_index("x"), lax.axis_index("y"), lax.axis_index("c")
    me = 4 * x + 2 * y + c
    peers = []
    for fx in (0, 1):
        for fy in (0, 1):
            for fc in (0, 1):
                if fx or fy or fc:
                    px = 1 - x if fx else x
                    py = 1 - y if fy else y
                    pc = 1 - c if fc else c
                    peers.append(((px, py, pc), 4 * px + 2 * py + pc))
    return me, peers


_HBM = pl.BlockSpec(memory_space=pltpu.HBM)
_SEM = pl.BlockSpec(memory_space=pltpu.SEMAPHORE)
_EFFECT = pltpu.SideEffectType.DATAFLOW_SIDE_EFFECTING
N_PEERS = N_DEV - 1


def _in_hbm(a):
    return pltpu.with_memory_space_constraint(a, pltpu.HBM)


def _exchange_copies(srcs, lands, send_sems, recv_sems, blockwise):
    me, peers = _peer_list()
    sends, recvs = [], []
    for a in range(len(srcs)):
        for k, (dev, idx) in enumerate(peers):
            src = srcs[a].at[idx] if blockwise[a] else srcs[a]
            sends.append(pltpu.make_async_remote_copy(
                src_ref=src, dst_ref=lands[a].at[me], send_sem=send_sems[a].at[k],
                recv_sem=recv_sems[a].at[k], device_id=dev, device_id_type=MESH))
            recvs.append(pltpu.make_async_remote_copy(
                src_ref=src, dst_ref=lands[a].at[idx], send_sem=send_sems[a].at[k],
                recv_sem=recv_sems[a].at[k], device_id=dev, device_id_type=MESH))
    return sends, recvs


def exchange_start(srcs, lands, blockwise, name):
    n = len(srcs)

    def body(*refs):
        src_in, land_in = refs[:n], refs[n:2 * n]
        send_sems, recv_sems = refs[2 * n:3 * n], refs[3 * n:4 * n]
        token = refs[6 * n]
        sends, _ = _exchange_copies(src_in, land_in, send_sems, recv_sems, blockwise)
        for cp in sends:
            cp.start()
        token[...] = jnp.zeros_like(token)

    out_shape = ([pltpu.SemaphoreType.DMA((N_PEERS,))] * (2 * n)
                 + [pltpu.HBM(s.shape, s.dtype) for s in srcs]
                 + [pltpu.HBM(l.shape, l.dtype) for l in lands]
                 + [jax.ShapeDtypeStruct((8, LANES), F32)])
    aliases = {a: 2 * n + a for a in range(2 * n)}
    outs = pl.pallas_call(
        body, name=name, out_shape=out_shape,
        in_specs=[_HBM] * (2 * n),
        out_specs=[_SEM] * (2 * n) + [_HBM] * (2 * n) + [pl.BlockSpec(memory_space=pltpu.VMEM)],
        input_output_aliases=aliases,
        compiler_params=pltpu.CompilerParams(has_side_effects=_EFFECT),
    )(*[_in_hbm(s) for s in srcs], *[_in_hbm(l) for l in lands])
    return (outs[:n], outs[n:2 * n], outs[2 * n:3 * n], outs[3 * n:4 * n], outs[4 * n])


def exchange_wait(send_sems, recv_sems, srcs, lands, blockwise, after, name):
    n = len(srcs)

    def body(*refs):
        src_in, land_in = refs[:n], refs[n:2 * n]
        ss, rs = refs[2 * n:3 * n], refs[3 * n:4 * n]
        sends, recvs = _exchange_copies(src_in, land_in, ss, rs, blockwise)
        for cp in sends:
            cp.wait_send()
        for cp in recvs:
            cp.wait_recv()

    outs = pl.pallas_call(
        body, name=name,
        out_shape=[pltpu.HBM(s.shape, s.dtype) for s in srcs] + [pltpu.HBM(l.shape, l.dtype) for l in lands],
        in_specs=[_HBM] * (2 * n) + [_SEM] * (2 * n) + [pl.BlockSpec(memory_space=pl.ANY)],
        out_specs=[_HBM] * (2 * n),
        input_output_aliases={a: a for a in range(2 * n)},
        compiler_params=pltpu.CompilerParams(has_side_effects=_EFFECT),
    )(*srcs, *lands, *send_sems, *recv_sems, after)
    return outs[n:]


def _landing(own_block, me, slots=N_DEV):
    empty = lax.empty((slots,) + own_block.shape, own_block.dtype)
    return lax.dynamic_update_slice(empty, own_block[None], (me,) + (0,) * own_block.ndim)


N_CHIPS = N_DEV // 2
_CHIP_FLIPS = ((1, 0), (0, 1), (1, 1))


def _xyc():
    return lax.axis_index("x"), lax.axis_index("y"), lax.axis_index("c")


def _other_chips(x, y):
    return [(1 - x if fx else x, 1 - y if fy else y) for fx, fy in _CHIP_FLIPS]


def _remote(src, dst, send_sem, recv_sem, dev):
    return pltpu.make_async_remote_copy(src_ref=src, dst_ref=dst, send_sem=send_sem, recv_sem=recv_sem,
                                        device_id=dev, device_id_type=MESH)


def comm_call(name, bufs, sems_in, sems_out, fn, after=None, want_token=False):
    nb, ni, no = len(bufs), len(sems_in), len(sems_out)
    afters = [] if after is None else (list(after) if isinstance(after, (list, tuple)) else [after])
    na = len(afters)

    def body(*refs):
        buf_refs = refs[:nb]
        sin = refs[nb:nb + ni]
        sout = refs[nb + ni + na:nb + ni + na + no]
        fn(buf_refs, sin, sout)
        if want_token:
            tok = refs[nb + ni + na + no + nb]
            tok[...] = jnp.zeros_like(tok)

    out_shape = list(sems_out) + [pltpu.HBM(b.shape, b.dtype) for b in bufs]
    out_specs = [_SEM] * no + [_HBM] * nb
    if want_token:
        out_shape.append(jax.ShapeDtypeStruct((8, LANES), F32))
        out_specs.append(pl.BlockSpec(memory_space=pltpu.VMEM))
    args = [_in_hbm(b) for b in bufs] + list(sems_in) + afters
    outs = pl.pallas_call(
        body, name=name, out_shape=out_shape,
        in_specs=[_HBM] * nb + [_SEM] * ni + [pl.BlockSpec(memory_space=pl.ANY)] * na,
        out_specs=out_specs, input_output_aliases={a: no + a for a in range(nb)},
        compiler_params=pltpu.CompilerParams(has_side_effects=_EFFECT),
    )(*args)
    return list(outs[:no]), list(outs[no:no + nb]), (outs[no + nb] if want_token else None)


def _dma_sems(*sizes):
    return [pltpu.SemaphoreType.DMA((s,)) for s in sizes]


def gather_start(srcs, lands, name, after=None):
    n = len(srcs)

    def fn(bufs, sin, sout):
        x, y, c = _xyc()
        me = 4 * x + 2 * y + c
        for a in range(n):
            src, land = bufs[a], bufs[n + a]
            send, recv_d, recv_i = sout[3 * a:3 * a + 3]
            _remote(src, land.at[me], send.at[0], recv_d.at[0], (x, y, 1 - c)).start()
            for k, (px, py) in enumerate(_other_chips(x, y)):
                _remote(src, land.at[me], send.at[1 + k], recv_i.at[k], (px, py, c)).start()

    return comm_call(name, list(srcs) + list(lands), [], _dma_sems(4, 1, 3) * n, fn, after=after, want_token=True)


def gather_forward(srcs, lands, recv_i, after, name):
    n = len(srcs)

    def fn(bufs, sin, sout):
        x, y, c = _xyc()
        for a in range(n):
            src, land = bufs[a], bufs[n + a]
            f_send, f_recv = sout[2 * a:2 * a + 2]
            for k, (px, py) in enumerate(_other_chips(x, y)):
                blk = land.at[4 * px + 2 * py + c]
                _remote(src, blk, f_send.at[k], sin[a].at[k], (px, py, c)).wait_recv()
                _remote(blk, blk, f_send.at[k], f_recv.at[k], (x, y, 1 - c)).start()

    sems, bufs, _ = comm_call(name, list(srcs) + list(lands), recv_i, _dma_sems(3, 3) * n, fn, after=after)
    return sems, bufs


def gather_wait(srcs, lands, send, recv_d, f_send, f_recv, after, name):
    n = len(srcs)

    def fn(bufs, sin, sout):
        x, y, c = _xyc()
        sib = (x, y, 1 - c)
        for a in range(n):
            src, land = bufs[a], bufs[n + a]
            s_send, s_recv_d, s_fsend, s_frecv = sin[4 * a:4 * a + 4]
            sib_blk = land.at[4 * x + 2 * y + 1 - c]
            for k in range(4):
                _remote(src, sib_blk, s_send.at[k], s_recv_d.at[0], sib).wait_send()
            _remote(src, sib_blk, s_send.at[0], s_recv_d.at[0], sib).wait_recv()
            for k, (px, py) in enumerate(_other_chips(x, y)):
                cp = _remote(src, land.at[4 * px + 2 * py + 1 - c], s_fsend.at[k], s_frecv.at[k], sib)
                cp.wait_send()
                cp.wait_recv()

    sems_in = []
    for a in range(n):
        sems_in += [send[a], recv_d[a], f_send[a], f_recv[a]]
    _, bufs, _ = comm_call(name, list(srcs) + list(lands), sems_in, [], fn, after=after)
    return bufs[n:]


def scatter_pair_start(src4s, lands, name, after=None):
    n = len(src4s)

    def fn(bufs, sin, sout):
        x, y, c = _xyc()
        for a in range(n):
            _remote(bufs[a].at[:, 1 - c], bufs[n + a], sout[2 * a].at[0], sout[2 * a + 1].at[0],
                    (x, y, 1 - c)).start()

    return comm_call(name, list(src4s) + list(lands), [], _dma_sems(1, 1) * n, fn, after=after, want_token=True)


def scatter_pair_wait(src4s, lands, sems, after, name):
    n = len(src4s)

    def fn(bufs, sin, sout):
        x, y, c = _xyc()
        for a in range(n):
            cp = _remote(bufs[a].at[:, 1 - c], bufs[n + a], sin[2 * a].at[0], sin[2 * a + 1].at[0], (x, y, 1 - c))
            cp.wait_send()
            cp.wait_recv()

    _, bufs, _ = comm_call(name, list(src4s) + list(lands), sems, [], fn, after=after)
    return bufs[:n], bufs[n:]


def _row_tile(R):
    for cand in (128, 64, 32, 16):
        if R % cand == 0:
            return cand
    return R


def chip_sum(src4, land, c, name):
    _, _, R, C = src4.shape
    tr = R

    def body(c_ref, a_ref, b_ref, o_ref):
        o_ref[...] = (a_ref[...].astype(F32) + b_ref[...].astype(F32)).astype(BF16)

    grid_spec = pltpu.PrefetchScalarGridSpec(
        num_scalar_prefetch=1, grid=(N_CHIPS, R // tr),
        in_specs=[pl.BlockSpec((None, None, tr, C), lambda q, i, cr: (q, cr[0], i, 0)),
                  pl.BlockSpec((None, tr, C), lambda q, i, cr: (q, i, 0))],
        out_specs=pl.BlockSpec((None, tr, C), lambda q, i, cr: (q, i, 0)))
    return pl.pallas_call(
        body, name=name, grid_spec=grid_spec,
        out_shape=jax.ShapeDtypeStruct((N_CHIPS, R, C), BF16),
        compiler_params=_params(("parallel", "parallel")),
    )(c.reshape(1).astype(jnp.int32), src4, land)


def scatter_chip_start(sums, lands, name):
    n = len(sums)

    def fn(bufs, sin, sout):
        x, y, c = _xyc()
        for a in range(n):
            for k, (px, py) in enumerate(_other_chips(x, y)):
                _remote(bufs[a].at[2 * px + py], bufs[n + a].at[2 * x + y], sout[2 * a].at[k], sout[2 * a + 1].at[k],
                        (px, py, c)).start()

    return comm_call(name, list(sums) + list(lands), [], _dma_sems(3, 3) * n, fn, want_token=True)


def scatter_chip_wait(sums, lands, sems, after, name):
    n = len(sums)

    def fn(bufs, sin, sout):
        x, y, c = _xyc()
        for a in range(n):
            for k, (px, py) in enumerate(_other_chips(x, y)):
                cp = _remote(bufs[a].at[2 * px + py], bufs[n + a].at[2 * px + py], sin[2 * a].at[k],
                             sin[2 * a + 1].at[k], (px, py, c))
                cp.wait_send()
                cp.wait_recv()

    _, bufs, _ = comm_call(name, list(sums) + list(lands), sems, [], fn, after=after)
    return bufs[:n], bufs[n:]


def _adamw_math(w, g, m, v):
    m = ADAM_B1 * m + (1.0 - ADAM_B1) * g
    v = ADAM_B2 * v + (1.0 - ADAM_B2) * (g * g)
    m_hat = m / (1.0 - ADAM_B1 ** ADAM_STEP)
    v_hat = v / (1.0 - ADAM_B2 ** ADAM_STEP)
    delta = -ADAM_LR * (m_hat / (jnp.sqrt(v_hat) + ADAM_EPS) + ADAM_WD * w)
    return delta, m, v


def _sum_partials(p_ref, own_ref, mine):
    own = own_ref[...].astype(F32)
    g = None
    for s in range(p_ref.shape[0]):
        term = jnp.where(mine == s, own, p_ref[s].astype(F32))
        g = term if g is None else g + term
    return g


def adamw_sharded(parts, sums, my_chip, w, m, v, name):
    R, C = w.shape
    S = parts.shape[0]
    tr = _row_tile(R)

    def body(mc_ref, p_ref, o_ref, w_ref, m_ref, v_ref, g_ref, d_ref, nm_ref, nv_ref):
        g = _sum_partials(p_ref, o_ref, mc_ref[0])
        delta, nm, nv = _adamw_math(w_ref[...], g, m_ref[...], v_ref[...])
        g_ref[...] = g
        d_ref[...] = delta
        nm_ref[...] = nm
        nv_ref[...] = nv

    row = pl.BlockSpec((tr, C), lambda i, mc: (i, 0))
    shp = jax.ShapeDtypeStruct((R, C), F32)
    grid_spec = pltpu.PrefetchScalarGridSpec(
        num_scalar_prefetch=1, grid=(R // tr,),
        in_specs=[pl.BlockSpec((S, tr, C), lambda i, mc: (0, i, 0)),
                  pl.BlockSpec((None, tr, C), lambda i, mc: (mc[0], i, 0)), row, row, row],
        out_specs=[row, row, row, row])
    return pl.pallas_call(
        body, name=name, grid_spec=grid_spec, out_shape=[shp, shp, shp, shp],
        compiler_params=_params(("parallel",)),
    )(my_chip.reshape(1).astype(jnp.int32), parts, sums, w, m, v)


def adamw_small(parts, w, m, v, name):
    R, C = w.shape

    def body(p_ref, w_ref, m_ref, v_ref, g_ref, d_ref, nm_ref, nv_ref):
        g = p_ref[0]
        for s in range(1, N_DEV):
            g = g + p_ref[s]
        delta, nm, nv = _adamw_math(w_ref[...], g, m_ref[...], v_ref[...])
        g_ref[...] = g
        d_ref[...] = delta
        nm_ref[...] = nm
        nv_ref[...] = nv

    shp = jax.ShapeDtypeStruct((R, C), F32)
    return pl.pallas_call(
        body, name=name, out_shape=[shp, shp, shp, shp], compiler_params=_params(None),
    )(parts, w, m, v)


_ROW_NORM_FFN1, _ROW_NORM_MIX, _ROW_NORM_FFN2, _ROW_NORM_PLE, _ROW_NORM_FINAL = 0, 1, 2, 3, 4
_ROW_B_F, _ROW_REL, _ROW_LOSS, _SMALL_ROWS = 5, 6, 7, 8


def _pack_small(D, norm_ffn1, norm_mix, norm_ffn2, norm_ple, norm_final, b_f, rel_table):
    def row(v):
        v = v.reshape(1, -1)
        return jnp.pad(v, ((0, 0), (0, D - v.shape[1])))
    return jnp.concatenate([row(norm_ffn1), row(norm_mix), row(norm_ffn2), row(norm_ple),
                            row(norm_final), row(b_f), row(rel_table),
                            jnp.zeros((1, D), F32)], axis=0)


def _unpack_small(a, shapes):
    return {"norm_ffn1": a[_ROW_NORM_FFN1].reshape(shapes["norm_ffn1"]),
            "norm_mix": a[_ROW_NORM_MIX].reshape(shapes["norm_mix"]),
            "b_f": a[_ROW_B_F, :N_FOX].reshape(shapes["b_f"]),
            "norm_ffn2": a[_ROW_NORM_FFN2].reshape(shapes["norm_ffn2"]),
            "norm_ple": a[_ROW_NORM_PLE].reshape(shapes["norm_ple"]),
            "rel_table": a[_ROW_REL, :N_REL_BUCKETS * N_DIL].reshape(shapes["rel_table"]),
            "norm_final": a[_ROW_NORM_FINAL].reshape(shapes["norm_final"])}


def local_step(x, p, tgt, g_ffn1, g_mix, g_ffn2, g_ple, g_final, b_f, rel_table,
               forward, weights, emit, emit2, first_dep):
    T, D = x.shape
    P = p.shape[1]
    CW = D // N_DEV
    tq = _tile(T, 256)

    h1 = rms_fwd(x, g_ffn1, "rms_ffn1", dep=first_dep)
    tiles = bias_tiles(rel_table, T, tq)
    forward("ffn1_gu", [tiles, h1])
    wgu1, = weights("ffn1_gu", h1)
    a1, b1, s1 = ffn_up(h1, wgu1, 0, 1, 2, "ffn1_up")
    forward("ffn1_d", s1)
    wd1, = weights("ffn1_d", s1)
    x1 = ffn_down(s1, wd1, 0, 1, x, "ffn1_down")

    h2 = rms_fwd(x1, g_mix, "rms_mix")
    forward("mix", h2)
    w3, wf, wo = weights("mix", h2)
    qkv = mm_nt([(h2, w3)], "mix_qkv", tn=768, out_dtype=BF16)
    uf = mm_nt([(h2, wf)], "mix_forget", tn=LANES, out_dtype=F32)
    bfp = jnp.pad(b_f.reshape(1, N_FOX), ((0, 0), (0, LANES - N_FOX)))
    c, ct = fox_gate_fwd(uf, bfp, "fox_gate")
    cat = attention_fwd(qkv, c, ct, tiles, "attention")
    forward("ffn2", cat)
    x2 = mm_nn(cat, wo, "mix_out", tn=512, out_dtype=F32, res=x1)

    h3 = rms_fwd(x2, g_ffn2, "rms_ffn2")
    wgu2, wd2 = weights("ffn2", h3)
    a2, b2, s2 = ffn_up(h3, wgu2, 0, 1, 2, "ffn2_up")
    forward("ple", s2)
    x3 = ffn_down(s2, wd2, 0, 1, x2, "ffn2_down")

    h4 = rms_fwd(x3, g_ple, "rms_ple")
    wpg, wpp = weights("ple", h4)
    z = mm_nn(h4, wpg, "ple_gate", tn=512, out_dtype=F32)
    pp = mm_nn(p, wpp, "ple_proj", tn=CW, out_dtype=F32, n_out=D,
               b_block=(P, CW), b_map=lambda n, i: (n, 0))
    x4 = ple_combine(x3, z, pp, "ple_combine")
    loss_row, dx4, dg_final = final_loss_bwd(x4, g_final, tgt, "final_loss")

    grads = {}
    dz, dpp = ple_bwd_elem(dx4, z, pp, "ple_bwd_elem")
    grads["w_ple_proj"] = mm_tn(p, dpp, "ple_proj_dw", grid=(N_DEV,),
                                a_block=(T, P), a_map=lambda n: (0, 0),
                                b_block=(T, CW), b_map=lambda n: (0, n),
                                o_block=(P, CW), o_map=lambda n: (n, 0),
                                out_shape=(N_DEV * P, CW))
    grads["w_ple_gate"] = mm_tn_plain(h4, dz, "ple_gate_dw")
    tok = emit("ple", grads)
    dh4 = mm_nt([(dz, wpg)], "ple_gate_dh", tn=512, out_dtype=F32, dep=tok)
    tok = emit2("ple", dh4)
    dx3, dx3h, dg_ple = rms_bwd(dh4, x3, g_ple, dx4, "rms_ple_bwd", dep=tok)

    da2, db2 = ffn_bwd_act(dx3h, wd2, 0, 1, a2, b2, "ffn2_bwd_act")
    grads["ffn2_w_down"] = ffn_bwd_dw_down(s2, dx3h, "ffn2_down_dw")
    grads["ffn2_w_gate"] = ffn_bwd_dw_in(h3, da2, "ffn2_gate_dw")
    grads["ffn2_w_up"] = ffn_bwd_dw_in(h3, db2, "ffn2_up_dw")
    tok = emit("ffn2", grads)
    dh3 = ffn_bwd_dh(da2, db2, wgu2, 0, 1, 2, D, "ffn2_bwd_dh", dep=tok)
    tok = emit2("ffn2", dh3)
    dx2, _, dg_ffn2 = rms_bwd(dh3, x2, g_ffn2, dx3, "rms_ffn2_bwd", dep=tok)

    dcat = mm_nt([(dx2, wo)], "mix_out_dh", tn=512, out_dtype=BF16)
    grads["w_o"] = mm_tn_plain(cat, dx2, "mix_out_dw")
    dq, dk, dv, dct, dtiles = attention_bwd(qkv, c, ct, tiles, dcat, "attention_bwd")
    dctp = jnp.pad(dct[:, 0, :], ((0, LANES - N_HEADS), (0, 0)))
    duf, dbf = fox_gate_bwd(dctp, uf, bfp, "fox_gate_bwd")
    drel = rel_table_grad(dtiles, T, "rel_table_grad")[:, 0, :N_REL_BUCKETS].T
    du3 = jnp.concatenate([dq, dk, dv], axis=1)
    grads["w3"] = mm_tn_plain(du3, h2, "mix_qkv_dw", tm=768)
    grads["wf"] = mm_tn_plain(duf, h2, "mix_forget_dw", tm=LANES)
    tok = emit("mix", grads)
    dh2 = mm_nn_sum([(du3, w3), (duf, wf)], "mix_in_dh", tn=512, out_dtype=F32, dep=tok)
    tok = emit2("mix", dh2)
    dx1, dx1h, dg_mix = rms_bwd(dh2, x1, g_mix, dx2, "rms_mix_bwd", dep=tok)

    da1, db1 = ffn_bwd_act(dx1h, wd1, 0, 1, a1, b1, "ffn1_bwd_act")
    grads["ffn1_w_down"] = ffn_bwd_dw_down(s1, dx1h, "ffn1_down_dw")
    tok = emit("ffn1_d", grads)
    grads["ffn1_w_gate"] = ffn_bwd_dw_in(h1, da1, "ffn1_gate_dw", dep=tok)
    tok = emit2("ffn1_d", grads["ffn1_w_gate"])
    tok = emit("ffn1_g", grads, after=tok)
    grads["ffn1_w_up"] = ffn_bwd_dw_in(h1, db1, "ffn1_up_dw", dep=tok)
    tok = emit2("ffn1_g", grads["ffn1_w_up"])
    tok = emit("ffn1_u", grads, after=tok)
    dh1 = ffn_bwd_dh(da1, db1, wgu1, 0, 1, 2, D, "ffn1_bwd_dh", dep=tok)
    tok = emit2("ffn1_u", dh1)
    dx0, _, dg_ffn1 = rms_bwd(dh1, x, g_ffn1, dx1, "rms_ffn1_bwd", dep=tok)

    small = _pack_small(D, dg_ffn1, dg_mix, dg_ffn2, dg_ple, dg_final, dbf[:, :N_FOX], drel)
    small = small.at[_ROW_LOSS, :LANES].set(loss_row[0])
    grads["small"] = small
    emit("small", grads)
    return dx0


def _split_w_in(w_in_t):
    df, dd = N_FOX * HEAD_DIM, N_DIL * HEAD_DIM
    o = np.cumsum([0, df, df, df, N_FOX, dd, dd, dd]).tolist()
    qa, ka, va, f, qb, kb, vb = [w_in_t[o[i]:o[i + 1]] for i in range(7)]
    return jnp.concatenate([qa, qb, ka, kb, va, vb], axis=0), f


def _join_w_in(d3, dfg):
    df, dd = N_FOX * HEAD_DIM, N_DIL * HEAD_DIM
    o = np.cumsum([0, df, dd, df, dd, df, dd]).tolist()
    qa, qb, ka, kb, va, vb = [d3[o[i]:o[i + 1]] for i in range(6)]
    return jnp.concatenate([qa, ka, va, dfg, qb, kb, vb], axis=0)


def rows_to_bf16(a3, name, dep=None):
    R, _, C = a3.shape
    tc = _tile(C, 512)

    def body(a_ref, *rest):
        rest[-1][...] = a_ref[...].astype(BF16)

    in_specs = [pl.BlockSpec((R, None, tc), lambda n: (0, 0, n))]
    args = [a3]
    if dep is not None:
        in_specs.append(_dep_spec(1))
        args.append(dep)
    return pl.pallas_call(
        body, name=name, grid=(C // tc,), in_specs=in_specs,
        out_specs=pl.BlockSpec((R, tc), lambda n: (0, n)),
        out_shape=jax.ShapeDtypeStruct((R, C), BF16),
        compiler_params=_params(("parallel",)),
    )(*args)


def adamw_rows3d(parts, sums, my_chip, w3, m3, v3, name):
    R, _, C = w3.shape
    S = parts.shape[0]
    tc = _tile(C, 256)

    def body(mc_ref, p_ref, o_ref, w_ref, m_ref, v_ref, g_ref, d_ref, nm_ref, nv_ref):
        g = _sum_partials(p_ref, o_ref, mc_ref[0])
        delta, nm, nv = _adamw_math(w_ref[...], g, m_ref[...], v_ref[...])
        g_ref[...] = g
        d_ref[...] = delta
        nm_ref[...] = nm
        nv_ref[...] = nv

    col = pl.BlockSpec((R, None, tc), lambda n, mc: (0, 0, n))
    shp = jax.ShapeDtypeStruct((R, 1, C), F32)
    grid_spec = pltpu.PrefetchScalarGridSpec(
        num_scalar_prefetch=1, grid=(C // tc,),
        in_specs=[pl.BlockSpec((S, R, tc), lambda n, mc: (0, 0, n)),
                  pl.BlockSpec((None, R, tc), lambda n, mc: (mc[0], 0, n)), col, col, col],
        out_specs=[col, col, col, col])
    return pl.pallas_call(
        body, name=name, grid_spec=grid_spec, out_shape=[shp, shp, shp, shp],
        compiler_params=_params(("parallel",)),
    )(my_chip.reshape(1).astype(jnp.int32), parts, sums, w3, m3, v3)


def kernel(x, p, norm_ffn1, ffn1_w_gate, ffn1_w_up, ffn1_w_down, norm_mix, w_in, b_f, w_o, norm_ffn2, ffn2_w_gate, ffn2_w_up, ffn2_w_down, norm_ple, w_ple_gate, w_ple_proj, rel_table, norm_final, loss_target, m_norm_ffn1, m_ffn1_w_gate, m_ffn1_w_up, m_ffn1_w_down, m_norm_mix, m_w_in, m_b_f, m_w_o, m_norm_ffn2, m_ffn2_w_gate, m_ffn2_w_up, m_ffn2_w_down, m_norm_ple, m_w_ple_gate, m_w_ple_proj, m_rel_table, m_norm_final, v_norm_ffn1, v_ffn1_w_gate, v_ffn1_w_up, v_ffn1_w_down, v_norm_mix, v_w_in, v_b_f, v_w_o, v_norm_ffn2, v_ffn2_w_gate, v_ffn2_w_up, v_ffn2_w_down, v_norm_ple, v_w_ple_gate, v_w_ple_proj, v_rel_table, v_norm_final):
    names = ["norm_ffn1", "ffn1_w_gate", "ffn1_w_up", "ffn1_w_down", "norm_mix", "w_in", "b_f", "w_o",
             "norm_ffn2", "ffn2_w_gate", "ffn2_w_up", "ffn2_w_down", "norm_ple", "w_ple_gate",
             "w_ple_proj", "rel_table", "norm_final"]
    w = dict(zip(names, [norm_ffn1, ffn1_w_gate, ffn1_w_up, ffn1_w_down, norm_mix, w_in, b_f, w_o,
                         norm_ffn2, ffn2_w_gate, ffn2_w_up, ffn2_w_down, norm_ple, w_ple_gate,
                         w_ple_proj, rel_table, norm_final]))
    m = dict(zip(names, [m_norm_ffn1, m_ffn1_w_gate, m_ffn1_w_up, m_ffn1_w_down, m_norm_mix, m_w_in,
                         m_b_f, m_w_o, m_norm_ffn2, m_ffn2_w_gate, m_ffn2_w_up, m_ffn2_w_down,
                         m_norm_ple, m_w_ple_gate, m_w_ple_proj, m_rel_table, m_norm_final]))
    v = dict(zip(names, [v_norm_ffn1, v_ffn1_w_gate, v_ffn1_w_up, v_ffn1_w_down, v_norm_mix, v_w_in,
                         v_b_f, v_w_o, v_norm_ffn2, v_ffn2_w_gate, v_ffn2_w_up, v_ffn2_w_down,
                         v_norm_ple, v_w_ple_gate, v_w_ple_proj, v_rel_table, v_norm_final]))
    sharded = ["ffn1_w_gate", "ffn1_w_up", "ffn1_w_down", "w_in", "w_o", "ffn2_w_gate", "ffn2_w_up",
               "ffn2_w_down", "w_ple_gate", "w_ple_proj"]
    small_names = [n for n in names if n not in sharded]

    xs, ps, tgt = x[0], p[0, 0], loss_target[0]
    T, D = xs.shape
    transposed = ("ffn1_w_gate", "ffn1_w_up", "ffn2_w_gate", "ffn2_w_up")

    def view(t, n):
        if n in transposed:
            return t[n][0].T
        if n == "w_in":
            return jnp.transpose(t[n], (2, 0, 1))
        return t[n][0]

    def unview(a, n):
        if n in transposed:
            return a.T.reshape(w[n].shape)
        if n == "w_in":
            return jnp.transpose(a, (1, 2, 0))
        return a.reshape(w[n].shape)

    sh = {n: view(w, n) for n in sharded}
    m_sh = {n: view(m, n) for n in sharded}
    v_sh = {n: view(v, n) for n in sharded}
    F8 = sh["ffn1_w_down"].shape[0]
    WIN8 = sh["w_in"].shape[0]
    me = 4 * lax.axis_index("x") + 2 * lax.axis_index("y") + lax.axis_index("c")

    def start(groups, name, after=None):
        srcs = [s for grp in groups for s in grp]
        sems, bufs, token = gather_start(srcs, [_landing(s, me) for s in srcs], name, after=after)
        return sems, bufs[:len(srcs)], bufs[len(srcs):], token

    cat0 = lambda ns, z: (jnp.concatenate([sh[n] for n in ns], axis=0) + z).astype(BF16)
    sems_a, srcs_a, lands_a, token_a = start(
        [[cat0(["ffn1_w_gate", "ffn1_w_up"], 0.0)], [sh["ffn1_w_down"].astype(BF16)]], "gather_start_ffn1")
    zero = token_a[0, 0]
    w_in_bf = rows_to_bf16(sh["w_in"], "w_in_bf16", dep=token_a)
    sems_b, srcs_b, lands_b, g_token = start(
        [[w_in_bf, (sh["w_o"] + zero).astype(BF16)],
         [cat0(["ffn2_w_gate", "ffn2_w_up"], zero), (sh["ffn2_w_down"] + zero).astype(BF16)],
         [(sh["w_ple_gate"] + zero).astype(BF16), (sh["w_ple_proj"] + zero).astype(BF16)]],
        "gather_start_rest", after=token_a)
    order = ["ffn1_gu", "ffn1_d", "mix", "ffn2", "ple"]
    group_sizes = [1, 1, 2, 2, 2]
    g_sems, g_srcs, g_lands = sems_a + sems_b, srcs_a + srcs_b, lands_a + lands_b
    g_send, g_recv_d, g_recv_i = g_sems[0::3], g_sems[1::3], g_sems[2::3]
    first = np.cumsum([0] + group_sizes).tolist()
    passed = {}

    def arrays_of(group):
        k = order.index(group)
        return slice(first[k], first[k + 1])

    def forward(group, after):
        sl = arrays_of(group)
        f_sems, bufs = gather_forward(g_srcs[sl], g_lands[sl], g_recv_i[sl], after, "gather_forward_" + group)
        k = len(bufs) // 2
        passed[group] = (f_sems[0::2], f_sems[1::2], bufs[:k], bufs[k:])

    def weights(group, after):
        sl = arrays_of(group)
        f_send, f_recv, srcs, lands = passed[group]
        got = gather_wait(srcs, lands, g_send[sl], g_recv_d[sl], f_send, f_recv, after, "gather_wait_" + group)
        if group == "ffn1_gu":
            return (got[0].reshape(N_DEV * 2 * F8, D),)
        if group == "ffn1_d":
            return (got[0].reshape(N_DEV * F8, D),)
        a0, a1 = got
        if group == "ffn2":
            return a0.reshape(N_DEV * 2 * F8, D), a1.reshape(N_DEV * F8, D)
        if group == "ple":
            return a0.reshape(-1, D), a1.reshape(-1, a1.shape[2])
        w3, wf8 = _split_w_in(a0.reshape(N_DEV * WIN8, D))
        return w3, jnp.pad(wf8, ((0, LANES - N_FOX), (0, 0))), a1.reshape(-1, D)

    scatter_groups = {
        "ple": ["w_ple_gate", "w_ple_proj"],
        "ffn2": ["ffn2_w_gate", "ffn2_w_up", "ffn2_w_down"],
        "mix": ["w_in", "w_o"],
        "ffn1_d": ["ffn1_w_down"],
        "ffn1_g": ["ffn1_w_gate"],
        "ffn1_u": ["ffn1_w_up"],
    }
    x_i, y_i, c_i = _xyc()
    my_chip = 2 * x_i + y_i
    pair_stage, chip_stage, small_stage = {}, {}, {}

    def emit(group, grads, after=None):
        if group == "small":
            src = grads["small"]
            ss, rs, srcs, lands, token = exchange_start([src], [_landing(src, me)], [False], "scatter_start_small")
            small_stage["small"] = (ss, rs, srcs, lands)
            return token
        src4s = []
        for n in scatter_groups[group]:
            if n == "w_in":
                full = _join_w_in(grads["w3"], grads["wf"][:N_FOX])
                src4s.append(full.reshape(N_CHIPS, 2, WIN8, D))
            else:
                src4s.append(grads[n].reshape((N_CHIPS, 2) + sh[n].shape))
        lands = [lax.empty((N_CHIPS,) + s.shape[2:], BF16) for s in src4s]
        sems, bufs, token = scatter_pair_start(src4s, lands, "scatter_pair_start_" + group, after=after)
        k = len(src4s)
        pair_stage[group] = (sems, bufs[:k], bufs[k:])
        return token

    def emit2(group, after):
        sems, src4s, lands = pair_stage[group]
        src4s, lands = scatter_pair_wait(src4s, lands, sems, after, "scatter_pair_wait_" + group)
        sums = [chip_sum(s4, la, c_i, "chip_sum_" + n)
                for s4, la, n in zip(src4s, lands, scatter_groups[group])]
        chip_lands = [lax.empty(s.shape, s.dtype) for s in sums]
        sems, bufs, token = scatter_chip_start(sums, chip_lands, "scatter_chip_start_" + group)
        k = len(sums)
        chip_stage[group] = (sems, bufs[:k], bufs[k:])
        return token

    dx0 = local_step(
        xs, ps, tgt, w["norm_ffn1"], w["norm_mix"], w["norm_ffn2"], w["norm_ple"],
        w["norm_final"].reshape(1, D), w["b_f"], w["rel_table"], forward, weights, emit, emit2, g_token)

    res = {}
    after = dx0
    for group in ["ple", "ffn2", "mix", "ffn1_d", "ffn1_g", "ffn1_u"]:
        sems, sums, chip_lands = chip_stage[group]
        sums, parts = scatter_chip_wait(sums, chip_lands, sems, after, "scatter_chip_wait_" + group)
        for n, part, own in zip(scatter_groups[group], parts, sums):
            update = adamw_rows3d if n == "w_in" else adamw_sharded
            g, d, nm, nv = update(part, own, my_chip, sh[n], m_sh[n], v_sh[n], "adamw_" + n)
            res[n] = tuple(unview(a, n) for a in (g, d, nm, nv))
            after = g
    ss, rs, srcs, lands = small_stage["small"]
    small_parts, = exchange_wait(ss, rs, srcs, lands, [False], after, "scatter_wait_small")
    pack = lambda t: _pack_small(D, t["norm_ffn1"], t["norm_mix"], t["norm_ffn2"], t["norm_ple"],
                                 t["norm_final"], t["b_f"], t["rel_table"])
    gs, ds, ms, vs = adamw_small(small_parts, pack(w), pack(m), pack(v), "adamw_small")
    shapes = {n: w[n].shape for n in small_names}
    unpacked = [_unpack_small(a, shapes) for a in (gs, ds, ms, vs)]
    for n in small_names:
        res[n] = tuple(u[n] for u in unpacked)
    loss = gs[_ROW_LOSS, 0]

    out = [loss, dx0.reshape(x.shape)]
    for k in range(4):
        out += [res[n][k] for n in names]
    return tuple(out)
```

```python
import functools
import math

import numpy as np
import jax
import jax.numpy as jnp
from jax import lax
from jax.experimental import pallas as pl
from jax.experimental.pallas import tpu as pltpu

F32 = jnp.float32
BF16 = jnp.bfloat16

N_DEV = 8
HEAD_DIM = 128
N_FOX = 8
N_DIL = 8
N_HEADS = N_FOX + N_DIL
DILATED_PATTERNS = ((128, 1), (512, 4), (2048, 16))
N_REL_BUCKETS = 32
REL_MAX_DISTANCE = 2048
RMS_EPS = 1e-6
NEG_INF = -1e30
LANES = 128
VMEM_LIMIT = 56 * 1024 * 1024

ADAM_LR = 0.001
ADAM_B1 = 0.9
ADAM_B2 = 0.999
ADAM_EPS = 1e-08
ADAM_WD = 0.01
ADAM_STEP = 10

MESH = pl.DeviceIdType.MESH


def _params(sem):
    return pltpu.CompilerParams(dimension_semantics=sem, vmem_limit_bytes=VMEM_LIMIT)


def _dot(a, b, ca, cb, precision=None):
    return lax.dot_general(a, b, (((ca,), (cb,)), ((), ())),
                           preferred_element_type=F32, precision=precision)


def _sigmoid(z):
    return 1.0 / (1.0 + jnp.exp(-z))


def _tile(n, want):
    t = min(n, want)
    assert n % t == 0, (n, t)
    return t


def _dep_spec(ngrid):
    return pl.BlockSpec((8, LANES), lambda *_: (0, 0))


def rms_fwd(x, g, name, dep=None):
    T, D = x.shape
    tm = _tile(T, 256)

    def body(x_ref, g_ref, *rest):
        h_ref = rest[-1]
        xv = x_ref[...]
        r = lax.rsqrt(jnp.mean(xv * xv, axis=-1, keepdims=True) + RMS_EPS)
        h_ref[...] = (xv * r * g_ref[...]).astype(BF16)

    in_specs = [pl.BlockSpec((tm, D), lambda i: (i, 0)), pl.BlockSpec((1, D), lambda i: (0, 0))]
    args = [x, g]
    if dep is not None:
        in_specs.append(_dep_spec(1))
        args.append(dep)
    return pl.pallas_call(
        body, name=name, grid=(T // tm,), in_specs=in_specs,
        out_specs=pl.BlockSpec((tm, D), lambda i: (i, 0)),
        out_shape=jax.ShapeDtypeStruct((T, D), BF16),
        compiler_params=_params(("parallel",)),
    )(*args)


def rms_bwd(dh, x, g, dres, name, dep=None, half=True):
    T, D = x.shape
    tm = _tile(T, 256)

    def body(dh_ref, x_ref, g_ref, dres_ref, *rest):
        dx_ref, dg_ref = (rest[-3], rest[-1]) if half else (rest[-2], rest[-1])
        i = pl.program_id(0)
        xv = x_ref[...]
        r = lax.rsqrt(jnp.mean(xv * xv, axis=-1, keepdims=True) + RMS_EPS)
        xh = xv * r
        d = dh_ref[...]
        u = d * g_ref[...]
        dx = dres_ref[...] + r * (u - xh * jnp.mean(u * xh, axis=-1, keepdims=True))
        dx_ref[...] = dx
        if half:
            rest[-2][...] = (0.5 * dx).astype(BF16)
        part = jnp.sum(d * xh, axis=0, keepdims=True)

        @pl.when(i == 0)
        def _():
            dg_ref[...] = part

        @pl.when(i > 0)
        def _():
            dg_ref[...] += part

    row = pl.BlockSpec((tm, D), lambda i: (i, 0))
    vec = pl.BlockSpec((1, D), lambda i: (0, 0))
    in_specs = [row, row, vec, row]
    args = [dh, x, g, dres]
    if dep is not None:
        in_specs.append(_dep_spec(1))
        args.append(dep)
    out_specs = [row, row, vec] if half else [row, vec]
    out_shape = [jax.ShapeDtypeStruct((T, D), F32)] + ([jax.ShapeDtypeStruct((T, D), BF16)] if half else [])
    out_shape.append(jax.ShapeDtypeStruct((1, D), F32))
    outs = pl.pallas_call(
        body, name=name, grid=(T // tm,),
        in_specs=in_specs, out_specs=out_specs, out_shape=out_shape,
        compiler_params=_params(("arbitrary",)),
    )(*args)
    return tuple(outs) if half else (outs[0], None, outs[1])


def final_loss_bwd(x, g, target, name):
    T, D = x.shape
    tm = _tile(T, 256)

    def body(x_ref, g_ref, t_ref, loss_ref, dx_ref, dg_ref):
        i = pl.program_id(0)
        xv = x_ref[...]
        gv = g_ref[...]
        r = lax.rsqrt(jnp.mean(xv * xv, axis=-1, keepdims=True) + RMS_EPS)
        xh = xv * r
        e = xh * gv - t_ref[...]
        lpart = 0.5 * jnp.sum(jnp.mean(e * e, axis=-1, keepdims=True), axis=0, keepdims=True)
        lrow = jnp.broadcast_to(lpart, (1, LANES))
        d = e * (1.0 / D)
        u = d * gv
        dx_ref[...] = r * (u - xh * jnp.mean(u * xh, axis=-1, keepdims=True))
        part = jnp.sum(d * xh, axis=0, keepdims=True)

        @pl.when(i == 0)
        def _():
            dg_ref[...] = part
            loss_ref[...] = lrow

        @pl.when(i > 0)
        def _():
            dg_ref[...] += part
            loss_ref[...] += lrow

    row = pl.BlockSpec((tm, D), lambda i: (i, 0))
    vec = pl.BlockSpec((1, D), lambda i: (0, 0))
    return pl.pallas_call(
        body, name=name, grid=(T // tm,),
        in_specs=[row, vec, row],
        out_specs=[pl.BlockSpec((1, LANES), lambda i: (0, 0)), row, vec],
        out_shape=[jax.ShapeDtypeStruct((1, LANES), F32), jax.ShapeDtypeStruct((T, D), F32),
                   jax.ShapeDtypeStruct((1, D), F32)],
        compiler_params=_params(("arbitrary",)),
    )(x, g, target)


def _bf(v, scale=None):
    if scale is not None:
        v = v * scale
    return v.astype(BF16)


def mm_nn(a, b, name, *, tn, out_dtype, tm=512, n_out=None, b_block=None, b_map=None,
          res=None):
    T, K = a.shape
    N = n_out if n_out is not None else b.shape[1]
    tm = _tile(T, tm)
    tn = _tile(N, tn)
    b_block = b_block or (K, tn)
    b_map = b_map or (lambda n, i: (0, n))

    def body(*refs):
        a_ref, b_ref = refs[0], refs[1]
        o_ref = refs[-1]
        acc = _dot(_bf(a_ref[...]), _bf(b_ref[...]), 1, 0)
        if res is not None:
            acc = refs[2][...] + acc
        o_ref[...] = acc.astype(out_dtype)

    in_specs = [pl.BlockSpec((tm, K), lambda n, i: (i, 0)), pl.BlockSpec(b_block, b_map)]
    args = [a, b]
    if res is not None:
        in_specs.append(pl.BlockSpec((tm, tn), lambda n, i: (i, n)))
        args.append(res)
    return pl.pallas_call(
        body, name=name, grid=(N // tn, T // tm), in_specs=in_specs,
        out_specs=pl.BlockSpec((tm, tn), lambda n, i: (i, n)),
        out_shape=jax.ShapeDtypeStruct((T, N), out_dtype),
        compiler_params=_params(("parallel", "parallel")),
    )(*args)


def mm_nn_sum(pairs, name, *, tn, out_dtype, tm=512, dep=None):
    T = pairs[0][0].shape[0]
    N = pairs[0][1].shape[1]
    tm = _tile(T, tm)
    tn = _tile(N, tn)
    npair = len(pairs)

    def body(*refs):
        acc = None
        for q in range(npair):
            part = _dot(_bf(refs[2 * q][...]), _bf(refs[2 * q + 1][...]), 1, 0)
            acc = part if acc is None else acc + part
        refs[-1][...] = acc.astype(out_dtype)

    in_specs, args = [], []
    for a, b in pairs:
        K = a.shape[1]
        in_specs += [pl.BlockSpec((tm, K), lambda n, i: (i, 0)), pl.BlockSpec((K, tn), lambda n, i: (0, n))]
        args += [a, b]
    if dep is not None:
        in_specs.append(_dep_spec(2))
        args.append(dep)
    return pl.pallas_call(
        body, name=name, grid=(N // tn, T // tm), in_specs=in_specs,
        out_specs=pl.BlockSpec((tm, tn), lambda n, i: (i, n)),
        out_shape=jax.ShapeDtypeStruct((T, N), out_dtype),
        compiler_params=_params(("parallel", "parallel")),
    )(*args)


def mm_nt(pairs, name, *, tn, out_dtype, tm=512, dep=None):
    T = pairs[0][0].shape[0]
    N = pairs[0][1].shape[0]
    tm = _tile(T, tm)
    tn = _tile(N, tn)
    npair = len(pairs)

    def body(*refs):
        o_ref = refs[-1]
        acc = None
        for q in range(npair):
            part = _dot(_bf(refs[2 * q][...]), _bf(refs[2 * q + 1][...]), 1, 1)
            acc = part if acc is None else acc + part
        o_ref[...] = acc.astype(out_dtype)

    in_specs, args = [], []
    for a, b in pairs:
        K = a.shape[1]
        in_specs += [pl.BlockSpec((tm, K), lambda n, i: (i, 0)), pl.BlockSpec((tn, K), lambda n, i: (n, 0))]
        args += [a, b]
    if dep is not None:
        in_specs.append(_dep_spec(2))
        args.append(dep)
    return pl.pallas_call(
        body, name=name, grid=(N // tn, T // tm), in_specs=in_specs,
        out_specs=pl.BlockSpec((tm, tn), lambda n, i: (i, n)),
        out_shape=jax.ShapeDtypeStruct((T, N), out_dtype),
        compiler_params=_params(("parallel", "parallel")),
    )(*args)


def mm_tn(a, b, name, *, grid, a_block, a_map, b_block, b_map, o_block, o_map, out_shape,
          b_scale=None, dep=None):
    def body(a_ref, b_ref, *rest):
        rest[-1][...] = _dot(_bf(a_ref[...]), _bf(b_ref[...], b_scale), 0, 0).astype(BF16)

    in_specs = [pl.BlockSpec(a_block, a_map), pl.BlockSpec(b_block, b_map)]
    args = [a, b]
    if dep is not None:
        in_specs.append(_dep_spec(len(grid)))
        args.append(dep)
    return pl.pallas_call(
        body, name=name, grid=grid, in_specs=in_specs,
        out_specs=pl.BlockSpec(o_block, o_map),
        out_shape=jax.ShapeDtypeStruct(out_shape, BF16),
        compiler_params=_params(("parallel",) * len(grid)),
    )(*args)


def mm_tn_plain(a, b, name, *, tm=512, tn=512, b_scale=None):
    T, M = a.shape
    N = b.shape[1]
    tm = _tile(M, tm)
    tn = _tile(N, tn)
    return mm_tn(a, b, name, grid=(M // tm, N // tn),
                 a_block=(T, tm), a_map=lambda m, n: (0, m),
                 b_block=(T, tn), b_map=lambda m, n: (0, n),
                 o_block=(tm, tn), o_map=lambda m, n: (m, n),
                 out_shape=(M, N), b_scale=b_scale)


def ffn_up(h, wgu, gi, ui, nper, name):
    T, D = h.shape
    F8 = wgu.shape[0] // (N_DEV * nper)
    tm = _tile(T, 512)
    nt = T // tm

    def body(h_ref, wg_ref, wu_ref, ga_ref, gb_ref, s_ref):
        hv = h_ref[...]
        a = _dot(hv, wg_ref[...], 1, 1)
        b = _dot(hv, wu_ref[...], 1, 1)
        sg = _sigmoid(a)
        silu = a * sg
        ga_ref[...] = (b * (sg * (1.0 + a * (1.0 - sg)))).astype(BF16)
        gb_ref[...] = silu.astype(BF16)
        s_ref[...] = (silu * b).astype(BF16)

    blk = pl.BlockSpec((tm, F8), lambda j, i: (j * nt + i, 0))
    shp = jax.ShapeDtypeStruct((N_DEV * T, F8), BF16)
    return pl.pallas_call(
        body, name=name, grid=(N_DEV, nt),
        in_specs=[pl.BlockSpec((tm, D), lambda j, i: (i, 0)),
                  pl.BlockSpec((F8, D), lambda j, i: (j * nper + gi, 0)),
                  pl.BlockSpec((F8, D), lambda j, i: (j * nper + ui, 0))],
        out_specs=[blk, blk, blk], out_shape=[shp, shp, shp],
        compiler_params=_params(("parallel", "parallel")),
    )(h, wgu, wgu)


def ffn_down(s, wd, di, nper, x, name):
    T, D = x.shape
    F8 = s.shape[1]
    tm = _tile(T, 512)
    nt = T // tm

    def body(s_ref, w_ref, x_ref, o_ref, acc_ref):
        j = pl.program_id(1)
        part = _dot(s_ref[...], w_ref[...], 1, 0)

        @pl.when(j == 0)
        def _():
            acc_ref[...] = part

        @pl.when(j > 0)
        def _():
            acc_ref[...] += part

        @pl.when(j == N_DEV - 1)
        def _():
            o_ref[...] = x_ref[...] + 0.5 * acc_ref[...]

    return pl.pallas_call(
        body, name=name, grid=(nt, N_DEV),
        in_specs=[pl.BlockSpec((tm, F8), lambda i, j: (j * nt + i, 0)),
                  pl.BlockSpec((F8, D), lambda i, j: (j * nper + di, 0)),
                  pl.BlockSpec((tm, D), lambda i, j: (i, 0))],
        out_specs=pl.BlockSpec((tm, D), lambda i, j: (i, 0)),
        out_shape=jax.ShapeDtypeStruct((T, D), F32),
        scratch_shapes=[pltpu.VMEM((tm, D), F32)],
        compiler_params=_params(("parallel", "arbitrary")),
    )(s, wd, x)


def ffn_bwd_act(dxh, wd, di, nper_d, a, b, name, dep=None):
    T, D = dxh.shape
    F8 = a.shape[1]
    tm = _tile(T, 512)
    nt = T // tm

    def body(dx_ref, w_ref, a_ref, b_ref, *rest):
        da_ref, db_ref = rest[-2], rest[-1]
        ds = _dot(dx_ref[...], w_ref[...], 1, 1)
        da_ref[...] = (ds * a_ref[...].astype(F32)).astype(BF16)
        db_ref[...] = (ds * b_ref[...].astype(F32)).astype(BF16)

    blk = pl.BlockSpec((tm, F8), lambda j, i: (j * nt + i, 0))
    shp = jax.ShapeDtypeStruct((N_DEV * T, F8), BF16)
    in_specs = [pl.BlockSpec((tm, D), lambda j, i: (i, 0)),
                pl.BlockSpec((F8, D), lambda j, i: (j * nper_d + di, 0)), blk, blk]
    args = [dxh, wd, a, b]
    if dep is not None:
        in_specs.append(_dep_spec(2))
        args.append(dep)
    return pl.pallas_call(
        body, name=name, grid=(N_DEV, nt), in_specs=in_specs,
        out_specs=[blk, blk], out_shape=[shp, shp],
        compiler_params=_params(("parallel", "parallel")),
    )(*args)


def ffn_bwd_dh(da, db, wgu, gi, ui, nper, D, name, dep=None):
    F8 = da.shape[1]
    T = da.shape[0] // N_DEV
    tm = _tile(T, 512)
    nt = T // tm

    def body(da_ref, db_ref, wg_ref, wu_ref, *rest):
        o_ref, acc_ref = rest[-2], rest[-1]
        j = pl.program_id(1)
        part = _dot(da_ref[...], wg_ref[...], 1, 0) + _dot(db_ref[...], wu_ref[...], 1, 0)

        @pl.when(j == 0)
        def _():
            acc_ref[...] = part

        @pl.when(j > 0)
        def _():
            acc_ref[...] += part

        @pl.when(j == N_DEV - 1)
        def _():
            o_ref[...] = acc_ref[...]

    blk = pl.BlockSpec((tm, F8), lambda i, j: (j * nt + i, 0))
    in_specs = [blk, blk,
                pl.BlockSpec((F8, D), lambda i, j: (j * nper + gi, 0)),
                pl.BlockSpec((F8, D), lambda i, j: (j * nper + ui, 0))]
    args = [da, db, wgu, wgu]
    if dep is not None:
        in_specs.append(_dep_spec(2))
        args.append(dep)
    return pl.pallas_call(
        body, name=name, grid=(nt, N_DEV), in_specs=in_specs,
        out_specs=pl.BlockSpec((tm, D), lambda i, j: (i, 0)),
        out_shape=jax.ShapeDtypeStruct((T, D), F32),
        scratch_shapes=[pltpu.VMEM((tm, D), F32)],
        compiler_params=_params(("parallel", "arbitrary")),
    )(*args)


def ffn_bwd_dw_in(h, dact, name, dep=None):
    T, D = h.shape
    F8 = dact.shape[1]
    tm = _tile(D, 512)
    return mm_tn(dact, h, name, grid=(N_DEV, D // tm),
                 a_block=(T, F8), a_map=lambda j, m: (j, 0),
                 b_block=(T, tm), b_map=lambda j, m: (0, m),
                 o_block=(F8, tm), o_map=lambda j, m: (j, m),
                 out_shape=(N_DEV * F8, D), dep=dep)


def ffn_bwd_dw_down(s, dx, name):
    T, D = dx.shape
    F8 = s.shape[1]
    tn = _tile(D, 512)
    return mm_tn(s, dx, name, grid=(N_DEV, D // tn),
                 a_block=(T, F8), a_map=lambda j, n: (j, 0),
                 b_block=(T, tn), b_map=lambda j, n: (0, n),
                 o_block=(F8, tn), o_map=lambda j, n: (j, n),
                 out_shape=(N_DEV * F8, D))


def _t5_bucket_np(dist):
    max_exact = N_REL_BUCKETS // 2
    d = np.maximum(dist, 1).astype(np.float64)
    large = max_exact + (np.log(d / max_exact) / math.log(REL_MAX_DISTANCE / max_exact)
                         * (N_REL_BUCKETS - max_exact)).astype(np.int64)
    large32 = max_exact + (np.log(d.astype(np.float32) / np.float32(max_exact))
                           / np.float32(math.log(REL_MAX_DISTANCE / max_exact))
                           * np.float32(N_REL_BUCKETS - max_exact)).astype(np.int64)
    assert np.array_equal(large, large32)
    large = np.minimum(large, N_REL_BUCKETS - 1)
    return np.where(dist < max_exact, dist, large)


def _distance_tables(T, tq):
    dist = np.arange(T)
    mult = np.zeros(T, np.int64)
    for window, dilation in DILATED_PATTERNS:
        mult += ((dist % dilation == 0) & (dist // dilation <= window // dilation)).astype(np.int64)
    logm = np.where(mult > 0, np.log(np.maximum(mult, 1)), NEG_INF).astype(np.float32)
    bucket = _t5_bucket_np(dist).astype(np.int32)
    nkb = T // tq
    k = np.arange(nkb)[:, None, None]
    r = np.arange(tq)[None, :, None]
    c = np.arange(tq)[None, None, :]
    delta = k * tq + r - c
    return bucket, logm, delta


def _tile_buckets(T, tq):
    bucket, logm, delta = _distance_tables(T, tq)
    safe = np.maximum(delta, 0)
    bidx = np.where(delta >= 0, bucket[safe], -1).astype(np.int32)
    logm_t = np.where(delta >= 0, logm[safe], NEG_INF).astype(np.float32)
    present = [sorted(set(np.unique(bidx[k]).tolist()) - {-1}) for k in range(T // tq)]
    return bidx, logm_t, present


def bias_tiles(rel_table, T, tq):
    bidx, logm_t, present = _tile_buckets(T, tq)
    nkb = T // tq

    def body(tab_ref, b_ref, lm_ref, o_ref):
        slot = pl.program_id(0)

        @pl.when(slot == 0)
        def _():
            o_ref[...] = jnp.where(b_ref[...] >= 0, 0.0, NEG_INF)

        @pl.when(slot > 0)
        def _():
            for k in range(nkb):
                bi = b_ref[k]
                acc = lm_ref[k]
                for b in present[k]:
                    acc = acc + jnp.where(bi == b, tab_ref[b, slot - 1], 0.0)
                o_ref[k] = acc

    full = pl.BlockSpec((nkb, tq, tq), lambda s: (0, 0, 0))
    return pl.pallas_call(
        body, name="bias_tiles", grid=(1 + N_DIL,),
        in_specs=[pl.BlockSpec(memory_space=pltpu.SMEM), full, full],
        out_specs=pl.BlockSpec((None, nkb, tq, tq), lambda s: (s, 0, 0, 0)),
        out_shape=jax.ShapeDtypeStruct((1 + N_DIL, nkb, tq, tq), F32),
        compiler_params=_params(("parallel",)),
    )(rel_table, jnp.asarray(bidx), jnp.asarray(logm_t))


def fox_gate_fwd(uf, bf, name):
    T = uf.shape[0]
    tb = _tile(T, 512)

    def body(u_ref, b_ref, c_ref, ct_ref):
        lane = lax.broadcasted_iota(jnp.int32, (1, LANES), 1)
        tri = (lax.broadcasted_iota(jnp.int32, (tb, tb), 0)
               >= lax.broadcasted_iota(jnp.int32, (tb, tb), 1)).astype(F32)
        carry = jnp.zeros((1, LANES), F32)
        for blk in range(T // tb):
            z = u_ref[pl.ds(blk * tb, tb), :] + b_ref[...]
            lf = jnp.minimum(z, 0.0) - jnp.log1p(jnp.exp(-jnp.abs(z)))
            lf = jnp.where(lane < N_FOX, lf, 0.0)
            cb = _dot(tri, lf, 1, 0, precision=lax.Precision.HIGHEST) + carry
            c_ref[pl.ds(blk * tb, tb), :] = cb
            ct_ref[:, pl.ds(blk * tb, tb)] = cb.T
            carry = cb[tb - 1:tb, :]

    return pl.pallas_call(
        body, name=name,
        out_shape=[jax.ShapeDtypeStruct((T, LANES), F32), jax.ShapeDtypeStruct((LANES, T), F32)],
        compiler_params=_params(None),
    )(uf, bf)


def fox_gate_bwd(dct, uf, bf, name):
    T = uf.shape[0]
    tb = _tile(T, 512)

    def body(d_ref, u_ref, b_ref, du_ref, db_ref):
        lane = lax.broadcasted_iota(jnp.int32, (1, LANES), 1)
        triu = (lax.broadcasted_iota(jnp.int32, (tb, tb), 0)
                <= lax.broadcasted_iota(jnp.int32, (tb, tb), 1)).astype(F32)
        carry = jnp.zeros((1, LANES), F32)
        dbv = jnp.zeros((1, LANES), F32)
        for blk in reversed(range(T // tb)):
            dc = d_ref[:, pl.ds(blk * tb, tb)].T
            dlf = _dot(triu, dc, 1, 0, precision=lax.Precision.HIGHEST) + carry
            carry = dlf[0:1, :]
            z = u_ref[pl.ds(blk * tb, tb), :] + b_ref[...]
            dz = jnp.where(lane < N_FOX, dlf * (1.0 - _sigmoid(z)), 0.0)
            du_ref[pl.ds(blk * tb, tb), :] = dz
            dbv = dbv + jnp.sum(dz, axis=0, keepdims=True)
        db_ref[...] = dbv

    return pl.pallas_call(
        body, name=name,
        out_shape=[jax.ShapeDtypeStruct((T, LANES), F32), jax.ShapeDtypeStruct((1, LANES), F32)],
        compiler_params=_params(None),
    )(dct, uf, bf)


def _bias_slot(h):
    return jnp.maximum(h - (N_FOX - 1), 0)


def _scores(q_ref, k_ref, c_ref, ct_ref, tb_ref, h, i, tq, fox):
    scale = HEAD_DIM ** -0.5
    n = (i + 1) * tq
    rows = pl.ds(i * tq, tq)
    s = _dot(q_ref[rows, :], k_ref[pl.ds(0, n), :], 1, 1) * scale
    if not fox:
        return s + jnp.concatenate([tb_ref[i - jb] for jb in range(i + 1)], axis=1)
    lane = lax.broadcasted_iota(jnp.int32, (1, LANES), 1)
    c_col = jnp.sum(jnp.where(lane == h, c_ref[rows, :], 0.0), axis=1, keepdims=True)
    c_row = ct_ref[pl.ds(h, 1), pl.ds(0, n)]
    s = s + (c_col - c_row)
    if i == 0:
        return s + tb_ref[0]
    return jnp.concatenate([s[:, :i * tq], s[:, i * tq:] + tb_ref[0]], axis=1)


def _attn_specs(T, tq):
    nkb = T // tq
    return [
        pl.BlockSpec((T, HEAD_DIM), lambda h: (0, h)),
        pl.BlockSpec((T, HEAD_DIM), lambda h: (0, N_HEADS + h)),
        pl.BlockSpec((T, HEAD_DIM), lambda h: (0, 2 * N_HEADS + h)),
        pl.BlockSpec((T, LANES), lambda h: (0, 0)),
        pl.BlockSpec((LANES, T), lambda h: (0, 0)),
        pl.BlockSpec((None, nkb, tq, tq), lambda h: (_bias_slot(h), 0, 0, 0)),
    ]


def attention_fwd(qkv, c, ct, tiles, name):
    T = qkv.shape[0]
    tq = tiles.shape[2]

    def body(q_ref, k_ref, v_ref, c_ref, ct_ref, tb_ref, o_ref):
        h = pl.program_id(0)

        def head(fox):
            for i in range(T // tq):
                s = _scores(q_ref, k_ref, c_ref, ct_ref, tb_ref, h, i, tq, fox)
                p = jnp.exp(s - jnp.max(s, axis=1, keepdims=True))
                l = jnp.sum(p, axis=1, keepdims=True)
                o = _dot(p.astype(BF16), v_ref[pl.ds(0, (i + 1) * tq), :], 1, 0) * (1.0 / l)
                o_ref[pl.ds(i * tq, tq), :] = o.astype(BF16)

        pl.when(h < N_FOX)(functools.partial(head, True))
        pl.when(h >= N_FOX)(functools.partial(head, False))

    return pl.pallas_call(
        body, name=name, grid=(N_HEADS,),
        in_specs=_attn_specs(T, tq),
        out_specs=pl.BlockSpec((T, HEAD_DIM), lambda h: (0, h)),
        out_shape=jax.ShapeDtypeStruct((T, N_HEADS * HEAD_DIM), BF16),
        compiler_params=_params(("parallel",)),
    )(qkv, qkv, qkv, c, ct, tiles)


def attention_bwd(qkv, c, ct, tiles, do, name):
    T = qkv.shape[0]
    tq = tiles.shape[2]
    nkb = T // tq
    scale = HEAD_DIM ** -0.5

    def body(q_ref, k_ref, v_ref, c_ref, ct_ref, tb_ref, do_ref,
             dq_ref, dk_ref, dv_ref, dct_ref, dtb_ref, dk_acc, dv_acc):
        h = pl.program_id(0)
        dk_acc[...] = jnp.zeros_like(dk_acc)
        dv_acc[...] = jnp.zeros_like(dv_acc)
        dct_ref[...] = jnp.zeros_like(dct_ref)
        dtb_ref[...] = jnp.zeros_like(dtb_ref)

        def head(fox):
            for i in range(nkb):
                rows, keys = pl.ds(i * tq, tq), pl.ds(0, (i + 1) * tq)
                s = _scores(q_ref, k_ref, c_ref, ct_ref, tb_ref, h, i, tq, fox)
                p = jnp.exp(s - jnp.max(s, axis=1, keepdims=True))
                p = p * (1.0 / jnp.sum(p, axis=1, keepdims=True))
                dov = do_ref[rows, :]
                dp = _dot(dov, v_ref[keys, :], 1, 1)
                ds = p * (dp - jnp.sum(p * dp, axis=1, keepdims=True))
                ds_b = ds.astype(BF16)
                dq_ref[rows, :] = (_dot(ds_b, k_ref[keys, :], 1, 0) * scale).astype(BF16)
                dk_acc[keys, :] += _dot(ds_b, q_ref[rows, :], 0, 0) * scale
                dv_acc[keys, :] += _dot(p.astype(BF16), dov, 0, 0)
                if fox:
                    dct_ref[:, keys] += -jnp.sum(ds, axis=0, keepdims=True)
                else:
                    for jb in range(i + 1):
                        dtb_ref[i - jb] += ds[:, jb * tq:(jb + 1) * tq]

        pl.when(h < N_FOX)(functools.partial(head, True))
        pl.when(h >= N_FOX)(functools.partial(head, False))
        dk_ref[...] = dk_acc[...].astype(BF16)
        dv_ref[...] = dv_acc[...].astype(BF16)

    head_cols = jax.ShapeDtypeStruct((T, N_HEADS * HEAD_DIM), BF16)
    col = pl.BlockSpec((T, HEAD_DIM), lambda h: (0, h))
    return pl.pallas_call(
        body, name=name, grid=(N_HEADS,),
        in_specs=_attn_specs(T, tq) + [col],
        out_specs=[col, col, col,
                   pl.BlockSpec((None, 1, T), lambda h: (h, 0, 0)),
                   pl.BlockSpec((None, nkb, tq, tq), lambda h: (_bias_slot(h), 0, 0, 0))],
        out_shape=[head_cols, head_cols, head_cols,
                   jax.ShapeDtypeStruct((N_HEADS, 1, T), F32),
                   jax.ShapeDtypeStruct((1 + N_DIL, nkb, tq, tq), F32)],
        scratch_shapes=[pltpu.VMEM((T, HEAD_DIM), F32), pltpu.VMEM((T, HEAD_DIM), F32)],
        compiler_params=_params(("arbitrary",)),
    )(qkv, qkv, qkv, c, ct, tiles, do)


def rel_table_grad(dtiles, T, name):
    tq = dtiles.shape[2]
    nkb = T // tq
    bidx, _, present = _tile_buckets(T, tq)

    def body(d_ref, b_ref, o_ref):
        lane = lax.broadcasted_iota(jnp.int32, (1, LANES), 1)
        row = jnp.zeros((1, LANES), F32)
        for k in range(nkb):
            d = d_ref[k]
            bi = b_ref[k]
            for b in present[k]:
                v = jnp.sum(jnp.sum(jnp.where(bi == b, d, 0.0), axis=0, keepdims=True),
                            axis=1, keepdims=True)
                row = row + jnp.where(lane == b, v, 0.0)
        o_ref[...] = row

    return pl.pallas_call(
        body, name=name, grid=(N_DIL,),
        in_specs=[pl.BlockSpec((None, nkb, tq, tq), lambda h: (h + 1, 0, 0, 0)),
                  pl.BlockSpec((nkb, tq, tq), lambda h: (0, 0, 0))],
        out_specs=pl.BlockSpec((None, 1, LANES), lambda h: (h, 0, 0)),
        out_shape=jax.ShapeDtypeStruct((N_DIL, 1, LANES), F32),
        compiler_params=_params(("parallel",)),
    )(dtiles, jnp.asarray(bidx))


def ple_combine(x, z, pp, name):
    T, D = x.shape
    tm = _tile(T, 256)

    def body(x_ref, z_ref, p_ref, o_ref):
        o_ref[...] = x_ref[...] + _sigmoid(z_ref[...]) * p_ref[...]

    row = pl.BlockSpec((tm, D), lambda i: (i, 0))
    return pl.pallas_call(
        body, name=name, grid=(T // tm,), in_specs=[row, row, row], out_specs=row,
        out_shape=jax.ShapeDtypeStruct((T, D), F32), compiler_params=_params(("parallel",)),
    )(x, z, pp)


def ple_bwd_elem(dx, z, pp, name):
    T, D = dx.shape
    tm = _tile(T, 256)

    def body(dx_ref, z_ref, p_ref, dz_ref, dp_ref):
        gate = _sigmoid(z_ref[...])
        d = dx_ref[...]
        dz_ref[...] = (d * p_ref[...] * gate * (1.0 - gate)).astype(BF16)
        dp_ref[...] = (d * gate).astype(BF16)

    row = pl.BlockSpec((tm, D), lambda i: (i, 0))
    shp = jax.ShapeDtypeStruct((T, D), BF16)
    return pl.pallas_call(
        body, name=name, grid=(T // tm,), in_specs=[row, row, row], out_specs=[row, row],
        out_shape=[shp, shp], compiler_params=_params(("parallel",)),
    )(dx, z, pp)


def _peer_list():
    x, y, c = lax.axis_index("x"), lax.axis_index("y"), lax.axis_index("c")
    me = 4 * x + 2 * y + c
    peers = []
    for fx in (0, 1):
        for fy in (0, 1):
            for fc in (0, 1):
                if fx or fy or fc:
                    px = 1 - x if fx else x
                    py = 1 - y if fy else y
                    pc = 1 - c if fc else c
                    peers.append(((px, py, pc), 4 * px + 2 * py + pc))
    return me, peers


_HBM = pl.BlockSpec(memory_space=pltpu.HBM)
_SEM = pl.BlockSpec(memory_space=pltpu.SEMAPHORE)
_EFFECT = pltpu.SideEffectType.DATAFLOW_SIDE_EFFECTING
N_PEERS = N_DEV - 1


def _in_hbm(a):
    return pltpu.with_memory_space_constraint(a, pltpu.HBM)


def _exchange_copies(srcs, lands, send_sems, recv_sems, blockwise):
    me, peers = _peer_list()
    sends, recvs = [], []
    for a in range(len(srcs)):
        for k, (dev, idx) in enumerate(peers):
            src = srcs[a].at[idx] if blockwise[a] else srcs[a]
            sends.append(pltpu.make_async_remote_copy(
                src_ref=src, dst_ref=lands[a].at[me], send_sem=send_sems[a].at[k],
                recv_sem=recv_sems[a].at[k], device_id=dev, device_id_type=MESH))
            recvs.append(pltpu.make_async_remote_copy(
                src_ref=src, dst_ref=lands[a].at[idx], send_sem=send_sems[a].at[k],
                recv_sem=recv_sems[a].at[k], device_id=dev, device_id_type=MESH))
    return sends, recvs


def exchange_start(srcs, lands, blockwise, name):
    n = len(srcs)

    def body(*refs):
        src_in, land_in = refs[:n], refs[n:2 * n]
        send_sems, recv_sems = refs[2 * n:3 * n], refs[3 * n:4 * n]
        token = refs[6 * n]
        sends, _ = _exchange_copies(src_in, land_in, send_sems, recv_sems, blockwise)
        for cp in sends:
            cp.start()
        token[...] = jnp.zeros_like(token)

    out_shape = ([pltpu.SemaphoreType.DMA((N_PEERS,))] * (2 * n)
                 + [pltpu.HBM(s.shape, s.dtype) for s in srcs]
                 + [pltpu.HBM(l.shape, l.dtype) for l in lands]
                 + [jax.ShapeDtypeStruct((8, LANES), F32)])
    aliases = {a: 2 * n + a for a in range(2 * n)}
    outs = pl.pallas_call(
        body, name=name, out_shape=out_shape,
        in_specs=[_HBM] * (2 * n),
        out_specs=[_SEM] * (2 * n) + [_HBM] * (2 * n) + [pl.BlockSpec(memory_space=pltpu.VMEM)],
        input_output_aliases=aliases,
        compiler_params=pltpu.CompilerParams(has_side_effects=_EFFECT),
    )(*[_in_hbm(s) for s in srcs], *[_in_hbm(l) for l in lands])
    return (outs[:n], outs[n:2 * n], outs[2 * n:3 * n], outs[3 * n:4 * n], outs[4 * n])


def exchange_wait(send_sems, recv_sems, srcs, lands, blockwise, after, name):
    n = len(srcs)

    def body(*refs):
        src_in, land_in = refs[:n], refs[n:2 * n]
        ss, rs = refs[2 * n:3 * n], refs[3 * n:4 * n]
        sends, recvs = _exchange_copies(src_in, land_in, ss, rs, blockwise)
        for cp in sends:
            cp.wait_send()
        for cp in recvs:
            cp.wait_recv()

    outs = pl.pallas_call(
        body, name=name,
        out_shape=[pltpu.HBM(s.shape, s.dtype) for s in srcs] + [pltpu.HBM(l.shape, l.dtype) for l in lands],
        in_specs=[_HBM] * (2 * n) + [_SEM] * (2 * n) + [pl.BlockSpec(memory_space=pl.ANY)],
        out_specs=[_HBM] * (2 * n),
        input_output_aliases={a: a for a in range(2 * n)},
        compiler_params=pltpu.CompilerParams(has_side_effects=_EFFECT),
    )(*srcs, *lands, *send_sems, *recv_sems, after)
    return outs[n:]


def _landing(own_block, me, slots=N_DEV):
    empty = lax.empty((slots,) + own_block.shape, own_block.dtype)
    return lax.dynamic_update_slice(empty, own_block[None], (me,) + (0,) * own_block.ndim)


N_CHIPS = N_DEV // 2
_CHIP_FLIPS = ((1, 0), (0, 1), (1, 1))


def _xyc():
    return lax.axis_index("x"), lax.axis_index("y"), lax.axis_index("c")


def _other_chips(x, y):
    return [(1 - x if fx else x, 1 - y if fy else y) for fx, fy in _CHIP_FLIPS]


def _remote(src, dst, send_sem, recv_sem, dev):
    return pltpu.make_async_remote_copy(src_ref=src, dst_ref=dst, send_sem=send_sem, recv_sem=recv_sem,
                                        device_id=dev, device_id_type=MESH)


def comm_call(name, bufs, sems_in, sems_out, fn, after=None, want_token=False):
    nb, ni, no = len(bufs), len(sems_in), len(sems_out)
    afters = [] if after is None else (list(after) if isinstance(after, (list, tuple)) else [after])
    na = len(afters)

    def body(*refs):
        buf_refs = refs[:nb]
        sin = refs[nb:nb + ni]
        sout = refs[nb + ni + na:nb + ni + na + no]
        fn(buf_refs, sin, sout)
        if want_token:
            tok = refs[nb + ni + na + no + nb]
            tok[...] = jnp.zeros_like(tok)

    out_shape = list(sems_out) + [pltpu.HBM(b.shape, b.dtype) for b in bufs]
    out_specs = [_SEM] * no + [_HBM] * nb
    if want_token:
        out_shape.append(jax.ShapeDtypeStruct((8, LANES), F32))
        out_specs.append(pl.BlockSpec(memory_space=pltpu.VMEM))
    args = [_in_hbm(b) for b in bufs] + list(sems_in) + afters
    outs = pl.pallas_call(
        body, name=name, out_shape=out_shape,
        in_specs=[_HBM] * nb + [_SEM] * ni + [pl.BlockSpec(memory_space=pl.ANY)] * na,
        out_specs=out_specs, input_output_aliases={a: no + a for a in range(nb)},
        compiler_params=pltpu.CompilerParams(has_side_effects=_EFFECT),
    )(*args)
    return list(outs[:no]), list(outs[no:no + nb]), (outs[no + nb] if want_token else None)


def _dma_sems(*sizes):
    return [pltpu.SemaphoreType.DMA((s,)) for s in sizes]


def gather_start(srcs, lands, name, after=None):
    n = len(srcs)

    def fn(bufs, sin, sout):
        x, y, c = _xyc()
        me = 4 * x + 2 * y + c
        for a in range(n):
            src, land = bufs[a], bufs[n + a]
            send, recv_d, recv_i = sout[3 * a:3 * a + 3]
            _remote(src, land.at[me], send.at[0], recv_d.at[0], (x, y, 1 - c)).start()
            for k, (px, py) in enumerate(_other_chips(x, y)):
                _remote(src, land.at[me], send.at[1 + k], recv_i.at[k], (px, py, c)).start()

    return comm_call(name, list(srcs) + list(lands), [], _dma_sems(4, 1, 3) * n, fn, after=after, want_token=True)


def gather_forward(srcs, lands, recv_i, after, name):
    n = len(srcs)

    def fn(bufs, sin, sout):
        x, y, c = _xyc()
        for a in range(n):
            src, land = bufs[a], bufs[n + a]
            f_send, f_recv = sout[2 * a:2 * a + 2]
            for k, (px, py) in enumerate(_other_chips(x, y)):
                blk = land.at[4 * px + 2 * py + c]
                _remote(src, blk, f_send.at[k], sin[a].at[k], (px, py, c)).wait_recv()
                _remote(blk, blk, f_send.at[k], f_recv.at[k], (x, y, 1 - c)).start()

    sems, bufs, _ = comm_call(name, list(srcs) + list(lands), recv_i, _dma_sems(3, 3) * n, fn, after=after)
    return sems, bufs


def gather_wait(srcs, lands, send, recv_d, f_send, f_recv, after, name):
    n = len(srcs)

    def fn(bufs, sin, sout):
        x, y, c = _xyc()
        sib = (x, y, 1 - c)
        for a in range(n):
            src, land = bufs[a], bufs[n + a]
            s_send, s_recv_d, s_fsend, s_frecv = sin[4 * a:4 * a + 4]
            sib_blk = land.at[4 * x + 2 * y + 1 - c]
            for k in range(4):
                _remote(src, sib_blk, s_send.at[k], s_recv_d.at[0], sib).wait_send()
            _remote(src, sib_blk, s_send.at[0], s_recv_d.at[0], sib).wait_recv()
            for k, (px, py) in enumerate(_other_chips(x, y)):
                cp = _remote(src, land.at[4 * px + 2 * py + 1 - c], s_fsend.at[k], s_frecv.at[k], sib)
                cp.wait_send()
                cp.wait_recv()

    sems_in = []
    for a in range(n):
        sems_in += [send[a], recv_d[a], f_send[a], f_recv[a]]
    _, bufs, _ = comm_call(name, list(srcs) + list(lands), sems_in, [], fn, after=after)
    return bufs[n:]


def scatter_pair_start(src4s, lands, name, after=None):
    n = len(src4s)

    def fn(bufs, sin, sout):
        x, y, c = _xyc()
        for a in range(n):
            _remote(bufs[a].at[:, 1 - c], bufs[n + a], sout[2 * a].at[0], sout[2 * a + 1].at[0],
                    (x, y, 1 - c)).start()

    return comm_call(name, list(src4s) + list(lands), [], _dma_sems(1, 1) * n, fn, after=after, want_token=True)


def scatter_pair_wait(src4s, lands, sems, after, name):
    n = len(src4s)

    def fn(bufs, sin, sout):
        x, y, c = _xyc()
        for a in range(n):
            cp = _remote(bufs[a].at[:, 1 - c], bufs[n + a], sin[2 * a].at[0], sin[2 * a + 1].at[0], (x, y, 1 - c))
            cp.wait_send()
            cp.wait_recv()

    _, bufs, _ = comm_call(name, list(src4s) + list(lands), sems, [], fn, after=after)
    return bufs[:n], bufs[n:]


def _row_tile(R):
    for cand in (128, 64, 32, 16):
        if R % cand == 0:
            return cand
    return R


def chip_sum(src4, land, c, name):
    _, _, R, C = src4.shape
    tr = R

    def body(c_ref, a_ref, b_ref, o_ref):
        o_ref[...] = (a_ref[...].astype(F32) + b_ref[...].astype(F32)).astype(BF16)

    grid_spec = pltpu.PrefetchScalarGridSpec(
        num_scalar_prefetch=1, grid=(N_CHIPS, R // tr),
        in_specs=[pl.BlockSpec((None, None, tr, C), lambda q, i, cr: (q, cr[0], i, 0)),
                  pl.BlockSpec((None, tr, C), lambda q, i, cr: (q, i, 0))],
        out_specs=pl.BlockSpec((None, tr, C), lambda q, i, cr: (q, i, 0)))
    return pl.pallas_call(
        body, name=name, grid_spec=grid_spec,
        out_shape=jax.ShapeDtypeStruct((N_CHIPS, R, C), BF16),
        compiler_params=_params(("parallel", "parallel")),
    )(c.reshape(1).astype(jnp.int32), src4, land)


def scatter_chip_start(sums, lands, name):
    n = len(sums)

    def fn(bufs, sin, sout):
        x, y, c = _xyc()
        for a in range(n):
            for k, (px, py) in enumerate(_other_chips(x, y)):
                _remote(bufs[a].at[2 * px + py], bufs[n + a].at[2 * x + y], sout[2 * a].at[k], sout[2 * a + 1].at[k],
                        (px, py, c)).start()

    return comm_call(name, list(sums) + list(lands), [], _dma_sems(3, 3) * n, fn, want_token=True)


def scatter_chip_wait(sums, lands, sems, after, name):
    n = len(sums)

    def fn(bufs, sin, sout):
        x, y, c = _xyc()
        for a in range(n):
            for k, (px, py) in enumerate(_other_chips(x, y)):
                cp = _remote(bufs[a].at[2 * px + py], bufs[n + a].at[2 * px + py], sin[2 * a].at[k],
                             sin[2 * a + 1].at[k], (px, py, c))
                cp.wait_send()
                cp.wait_recv()

    _, bufs, _ = comm_call(name, list(sums) + list(lands), sems, [], fn, after=after)
    return bufs[:n], bufs[n:]


def _adamw_math(w, g, m, v):
    m = ADAM_B1 * m + (1.0 - ADAM_B1) * g
    v = ADAM_B2 * v + (1.0 - ADAM_B2) * (g * g)
    m_hat = m / (1.0 - ADAM_B1 ** ADAM_STEP)
    v_hat = v / (1.0 - ADAM_B2 ** ADAM_STEP)
    delta = -ADAM_LR * (m_hat / (jnp.sqrt(v_hat) + ADAM_EPS) + ADAM_WD * w)
    return delta, m, v


def _sum_partials(p_ref, own_ref, mine):
    own = own_ref[...].astype(F32)
    g = None
    for s in range(p_ref.shape[0]):
        term = jnp.where(mine == s, own, p_ref[s].astype(F32))
        g = term if g is None else g + term
    return g


def adamw_sharded(parts, sums, my_chip, w, m, v, name):
    R, C = w.shape
    S = parts.shape[0]
    tr = _row_tile(R)

    def body(mc_ref, p_ref, o_ref, w_ref, m_ref, v_ref, g_ref, d_ref, nm_ref, nv_ref):
        g = _sum_partials(p_ref, o_ref, mc_ref[0])
        delta, nm, nv = _adamw_math(w_ref[...], g, m_ref[...], v_ref[...])
        g_ref[...] = g
        d_ref[...] = delta
        nm_ref[...] = nm
        nv_ref[...] = nv

    row = pl.BlockSpec((tr, C), lambda i, mc: (i, 0))
    shp = jax.ShapeDtypeStruct((R, C), F32)
    grid_spec = pltpu.PrefetchScalarGridSpec(
        num_scalar_prefetch=1, grid=(R // tr,),
        in_specs=[pl.BlockSpec((S, tr, C), lambda i, mc: (0, i, 0)),
                  pl.BlockSpec((None, tr, C), lambda i, mc: (mc[0], i, 0)), row, row, row],
        out_specs=[row, row, row, row])
    return pl.pallas_call(
        body, name=name, grid_spec=grid_spec, out_shape=[shp, shp, shp, shp],
        compiler_params=_params(("parallel",)),
    )(my_chip.reshape(1).astype(jnp.int32), parts, sums, w, m, v)


def adamw_small(parts, w, m, v, name):
    R, C = w.shape

    def body(p_ref, w_ref, m_ref, v_ref, g_ref, d_ref, nm_ref, nv_ref):
        g = p_ref[0]
        for s in range(1, N_DEV):
            g = g + p_ref[s]
        delta, nm, nv = _adamw_math(w_ref[...], g, m_ref[...], v_ref[...])
        g_ref[...] = g
        d_ref[...] = delta
        nm_ref[...] = nm
        nv_ref[...] = nv

    shp = jax.ShapeDtypeStruct((R, C), F32)
    return pl.pallas_call(
        body, name=name, out_shape=[shp, shp, shp, shp], compiler_params=_params(None),
    )(parts, w, m, v)


_ROW_NORM_FFN1, _ROW_NORM_MIX, _ROW_NORM_FFN2, _ROW_NORM_PLE, _ROW_NORM_FINAL = 0, 1, 2, 3, 4
_ROW_B_F, _ROW_REL, _ROW_LOSS, _SMALL_ROWS = 5, 6, 7, 8


def _pack_small(D, norm_ffn1, norm_mix, norm_ffn2, norm_ple, norm_final, b_f, rel_table):
    def row(v):
        v = v.reshape(1, -1)
        return jnp.pad(v, ((0, 0), (0, D - v.shape[1])))
    return jnp.concatenate([row(norm_ffn1), row(norm_mix), row(norm_ffn2), row(norm_ple),
                            row(norm_final), row(b_f), row(rel_table),
                            jnp.zeros((1, D), F32)], axis=0)


def _unpack_small(a, shapes):
    return {"norm_ffn1": a[_ROW_NORM_FFN1].reshape(shapes["norm_ffn1"]),
            "norm_mix": a[_ROW_NORM_MIX].reshape(shapes["norm_mix"]),
            "b_f": a[_ROW_B_F, :N_FOX].reshape(shapes["b_f"]),
            "norm_ffn2": a[_ROW_NORM_FFN2].reshape(shapes["norm_ffn2"]),
            "norm_ple": a[_ROW_NORM_PLE].reshape(shapes["norm_ple"]),
            "rel_table": a[_ROW_REL, :N_REL_BUCKETS * N_DIL].reshape(shapes["rel_table"]),
            "norm_final": a[_ROW_NORM_FINAL].reshape(shapes["norm_final"])}


def local_step(x, p, tgt, g_ffn1, g_mix, g_ffn2, g_ple, g_final, b_f, rel_table,
               forward, weights, emit, emit2, first_dep):
    T, D = x.shape
    P = p.shape[1]
    CW = D // N_DEV
    tq = _tile(T, 256)

    h1 = rms_fwd(x, g_ffn1, "rms_ffn1", dep=first_dep)
    tiles = bias_tiles(rel_table, T, tq)
    forward("ffn1_gu", [tiles, h1])
    wgu1, = weights("ffn1_gu", h1)
    a1, b1, s1 = ffn_up(h1, wgu1, 0, 1, 2, "ffn1_up")
    forward("ffn1_d", s1)
    wd1, = weights("ffn1_d", s1)
    x1 = ffn_down(s1, wd1, 0, 1, x, "ffn1_down")

    h2 = rms_fwd(x1, g_mix, "rms_mix")
    forward("mix", h2)
    w3, wf, wo = weights("mix", h2)
    qkv = mm_nt([(h2, w3)], "mix_qkv", tn=768, out_dtype=BF16)
    uf = mm_nt([(h2, wf)], "mix_forget", tn=LANES, out_dtype=F32)
    bfp = jnp.pad(b_f.reshape(1, N_FOX), ((0, 0), (0, LANES - N_FOX)))
    c, ct = fox_gate_fwd(uf, bfp, "fox_gate")
    cat = attention_fwd(qkv, c, ct, tiles, "attention")
    x2 = mm_nn(cat, wo, "mix_out", tn=512, out_dtype=F32, res=x1)

    h3 = rms_fwd(x2, g_ffn2, "rms_ffn2")
    forward("ffn2", h3)
    wgu2, wd2 = weights("ffn2", h3)
    a2, b2, s2 = ffn_up(h3, wgu2, 0, 1, 2, "ffn2_up")
    forward("ple", s2)
    x3 = ffn_down(s2, wd2, 0, 1, x2, "ffn2_down")

    h4 = rms_fwd(x3, g_ple, "rms_ple")
    wpg, wpp = weights("ple", h4)
    z = mm_nn(h4, wpg, "ple_gate", tn=512, out_dtype=F32)
    pp = mm_nn(p, wpp, "ple_proj", tn=CW, out_dtype=F32, n_out=D,
               b_block=(P, CW), b_map=lambda n, i: (n, 0))
    x4 = ple_combine(x3, z, pp, "ple_combine")
    loss_row, dx4, dg_final = final_loss_bwd(x4, g_final, tgt, "final_loss")

    grads = {}
    dz, dpp = ple_bwd_elem(dx4, z, pp, "ple_bwd_elem")
    grads["w_ple_proj"] = mm_tn(p, dpp, "ple_proj_dw", grid=(N_DEV,),
                                a_block=(T, P), a_map=lambda n: (0, 0),
                                b_block=(T, CW), b_map=lambda n: (0, n),
                                o_block=(P, CW), o_map=lambda n: (n, 0),
                                out_shape=(N_DEV * P, CW))
    grads["w_ple_gate"] = mm_tn_plain(h4, dz, "ple_gate_dw")
    tok = emit("ple", grads)
    dh4 = mm_nt([(dz, wpg)], "ple_gate_dh", tn=512, out_dtype=F32, dep=tok)
    tok = emit2("ple", dh4)
    dx3, dx3h, dg_ple = rms_bwd(dh4, x3, g_ple, dx4, "rms_ple_bwd", dep=tok)

    da2, db2 = ffn_bwd_act(dx3h, wd2, 0, 1, a2, b2, "ffn2_bwd_act")
    grads["ffn2_w_down"] = ffn_bwd_dw_down(s2, dx3h, "ffn2_down_dw")
    grads["ffn2_w_gate"] = ffn_bwd_dw_in(h3, da2, "ffn2_gate_dw")
    grads["ffn2_w_up"] = ffn_bwd_dw_in(h3, db2, "ffn2_up_dw")
    tok = emit("ffn2", grads)
    dh3 = ffn_bwd_dh(da2, db2, wgu2, 0, 1, 2, D, "ffn2_bwd_dh", dep=tok)
    tok = emit2("ffn2", dh3)
    dx2, _, dg_ffn2 = rms_bwd(dh3, x2, g_ffn2, dx3, "rms_ffn2_bwd", dep=tok, half=False)

    dcat = mm_nt([(dx2, wo)], "mix_out_dh", tn=512, out_dtype=BF16)
    grads["w_o"] = mm_tn_plain(cat, dx2, "mix_out_dw")
    dq, dk, dv, dct, dtiles = attention_bwd(qkv, c, ct, tiles, dcat, "attention_bwd")
    dctp = jnp.pad(dct[:, 0, :], ((0, LANES - N_HEADS), (0, 0)))
    duf, dbf = fox_gate_bwd(dctp, uf, bfp, "fox_gate_bwd")
    drel = rel_table_grad(dtiles, T, "rel_table_grad")[:, 0, :N_REL_BUCKETS].T
    du3 = jnp.concatenate([dq, dk, dv], axis=1)
    grads["w3"] = mm_tn_plain(du3, h2, "mix_qkv_dw", tm=768)
    grads["wf"] = mm_tn_plain(duf, h2, "mix_forget_dw", tm=LANES)
    tok = emit("mix", grads)
    dh2 = mm_nn_sum([(du3, w3), (duf, wf)], "mix_in_dh", tn=512, out_dtype=F32, dep=tok)
    tok = emit2("mix", dh2)
    dx1, dx1h, dg_mix = rms_bwd(dh2, x1, g_mix, dx2, "rms_mix_bwd", dep=tok)

    da1, db1 = ffn_bwd_act(dx1h, wd1, 0, 1, a1, b1, "ffn1_bwd_act")
    grads["ffn1_w_down"] = ffn_bwd_dw_down(s1, dx1h, "ffn1_down_dw")
    tok = emit("ffn1_d", grads)
    grads["ffn1_w_gate"] = ffn_bwd_dw_in(h1, da1, "ffn1_gate_dw", dep=tok)
    tok = emit2("ffn1_d", grads["ffn1_w_gate"])
    tok = emit("ffn1_g", grads, after=tok)
    grads["ffn1_w_up"] = ffn_bwd_dw_in(h1, db1, "ffn1_up_dw", dep=tok)
    tok = emit2("ffn1_g", grads["ffn1_w_up"])
    tok = emit("ffn1_u", grads, after=tok)
    dh1 = ffn_bwd_dh(da1, db1, wgu1, 0, 1, 2, D, "ffn1_bwd_dh", dep=tok)
    tok = emit2("ffn1_u", dh1)
    dx0, _, dg_ffn1 = rms_bwd(dh1, x, g_ffn1, dx1, "rms_ffn1_bwd", dep=tok, half=False)

    small = _pack_small(D, dg_ffn1, dg_mix, dg_ffn2, dg_ple, dg_final, dbf[:, :N_FOX], drel)
    small = small.at[_ROW_LOSS, :LANES].set(loss_row[0])
    grads["small"] = small
    emit("small", grads)
    return dx0


def _split_w_in(w_in_t):
    df, dd = N_FOX * HEAD_DIM, N_DIL * HEAD_DIM
    o = np.cumsum([0, df, df, df, N_FOX, dd, dd, dd]).tolist()
    qa, ka, va, f, qb, kb, vb = [w_in_t[o[i]:o[i + 1]] for i in range(7)]
    return jnp.concatenate([qa, qb, ka, kb, va, vb], axis=0), f


def _join_w_in(d3, dfg):
    df, dd = N_FOX * HEAD_DIM, N_DIL * HEAD_DIM
    o = np.cumsum([0, df, dd, df, dd, df, dd]).tolist()
    qa, qb, ka, kb, va, vb = [d3[o[i]:o[i + 1]] for i in range(6)]
    return jnp.concatenate([qa, ka, va, dfg, qb, kb, vb], axis=0)


def rows_to_bf16(a3, name, dep=None):
    R, _, C = a3.shape
    tc = _tile(C, 512)

    def body(a_ref, *rest):
        rest[-1][...] = a_ref[...].astype(BF16)

    in_specs = [pl.BlockSpec((R, None, tc), lambda n: (0, 0, n))]
    args = [a3]
    if dep is not None:
        in_specs.append(_dep_spec(1))
        args.append(dep)
    return pl.pallas_call(
        body, name=name, grid=(C // tc,), in_specs=in_specs,
        out_specs=pl.BlockSpec((R, tc), lambda n: (0, n)),
        out_shape=jax.ShapeDtypeStruct((R, C), BF16),
        compiler_params=_params(("parallel",)),
    )(*args)


def adamw_rows3d(parts, sums, my_chip, w3, m3, v3, name):
    R, _, C = w3.shape
    S = parts.shape[0]
    tc = _tile(C, 256)

    def body(mc_ref, p_ref, o_ref, w_ref, m_ref, v_ref, g_ref, d_ref, nm_ref, nv_ref):
        g = _sum_partials(p_ref, o_ref, mc_ref[0])
        delta, nm, nv = _adamw_math(w_ref[...], g, m_ref[...], v_ref[...])
        g_ref[...] = g
        d_ref[...] = delta
        nm_ref[...] = nm
        nv_ref[...] = nv

    col = pl.BlockSpec((R, None, tc), lambda n, mc: (0, 0, n))
    shp = jax.ShapeDtypeStruct((R, 1, C), F32)
    grid_spec = pltpu.PrefetchScalarGridSpec(
        num_scalar_prefetch=1, grid=(C // tc,),
        in_specs=[pl.BlockSpec((S, R, tc), lambda n, mc: (0, 0, n)),
                  pl.BlockSpec((None, R, tc), lambda n, mc: (mc[0], 0, n)), col, col, col],
        out_specs=[col, col, col, col])
    return pl.pallas_call(
        body, name=name, grid_spec=grid_spec, out_shape=[shp, shp, shp, shp],
        compiler_params=_params(("parallel",)),
    )(my_chip.reshape(1).astype(jnp.int32), parts, sums, w3, m3, v3)


def kernel(x, p, norm_ffn1, ffn1_w_gate, ffn1_w_up, ffn1_w_down, norm_mix, w_in, b_f, w_o, norm_ffn2, ffn2_w_gate, ffn2_w_up, ffn2_w_down, norm_ple, w_ple_gate, w_ple_proj, rel_table, norm_final, loss_target, m_norm_ffn1, m_ffn1_w_gate, m_ffn1_w_up, m_ffn1_w_down, m_norm_mix, m_w_in, m_b_f, m_w_o, m_norm_ffn2, m_ffn2_w_gate, m_ffn2_w_up, m_ffn2_w_down, m_norm_ple, m_w_ple_gate, m_w_ple_proj, m_rel_table, m_norm_final, v_norm_ffn1, v_ffn1_w_gate, v_ffn1_w_up, v_ffn1_w_down, v_norm_mix, v_w_in, v_b_f, v_w_o, v_norm_ffn2, v_ffn2_w_gate, v_ffn2_w_up, v_ffn2_w_down, v_norm_ple, v_w_ple_gate, v_w_ple_proj, v_rel_table, v_norm_final):
    names = ["norm_ffn1", "ffn1_w_gate", "ffn1_w_up", "ffn1_w_down", "norm_mix", "w_in", "b_f", "w_o",
             "norm_ffn2", "ffn2_w_gate", "ffn2_w_up", "ffn2_w_down", "norm_ple", "w_ple_gate",
             "w_ple_proj", "rel_table", "norm_final"]
    w = dict(zip(names, [norm_ffn1, ffn1_w_gate, ffn1_w_up, ffn1_w_down, norm_mix, w_in, b_f, w_o,
                         norm_ffn2, ffn2_w_gate, ffn2_w_up, ffn2_w_down, norm_ple, w_ple_gate,
                         w_ple_proj, rel_table, norm_final]))
    m = dict(zip(names, [m_norm_ffn1, m_ffn1_w_gate, m_ffn1_w_up, m_ffn1_w_down, m_norm_mix, m_w_in,
                         m_b_f, m_w_o, m_norm_ffn2, m_ffn2_w_gate, m_ffn2_w_up, m_ffn2_w_down,
                         m_norm_ple, m_w_ple_gate, m_w_ple_proj, m_rel_table, m_norm_final]))
    v = dict(zip(names, [v_norm_ffn1, v_ffn1_w_gate, v_ffn1_w_up, v_ffn1_w_down, v_norm_mix, v_w_in,
                         v_b_f, v_w_o, v_norm_ffn2, v_ffn2_w_gate, v_ffn2_w_up, v_ffn2_w_down,
                         v_norm_ple, v_w_ple_gate, v_w_ple_proj, v_rel_table, v_norm_final]))
    sharded = ["ffn1_w_gate", "ffn1_w_up", "ffn1_w_down", "w_in", "w_o", "ffn2_w_gate", "ffn2_w_up",
               "ffn2_w_down", "w_ple_gate", "w_ple_proj"]
    small_names = [n for n in names if n not in sharded]

    xs, ps, tgt = x[0], p[0, 0], loss_target[0]
    T, D = xs.shape
    transposed = ("ffn1_w_gate", "ffn1_w_up", "ffn2_w_gate", "ffn2_w_up")

    def view(t, n):
        if n in transposed:
            return t[n][0].T
        if n == "w_in":
            return jnp.transpose(t[n], (2, 0, 1))
        return t[n][0]

    def unview(a, n):
        if n in transposed:
            return a.T.reshape(w[n].shape)
        if n == "w_in":
            return jnp.transpose(a, (1, 2, 0))
        return a.reshape(w[n].shape)

    sh = {n: view(w, n) for n in sharded}
    m_sh = {n: view(m, n) for n in sharded}
    v_sh = {n: view(v, n) for n in sharded}
    F8 = sh["ffn1_w_down"].shape[0]
    WIN8 = sh["w_in"].shape[0]
    me = 4 * lax.axis_index("x") + 2 * lax.axis_index("y") + lax.axis_index("c")

    def start(groups, name, after=None):
        srcs = [s for grp in groups for s in grp]
        sems, bufs, token = gather_start(srcs, [_landing(s, me) for s in srcs], name, after=after)
        return sems, bufs[:len(srcs)], bufs[len(srcs):], token

    cat0 = lambda ns, z: (jnp.concatenate([sh[n] for n in ns], axis=0) + z).astype(BF16)
    sems_a, srcs_a, lands_a, token_a = start(
        [[cat0(["ffn1_w_gate", "ffn1_w_up"], 0.0)], [sh["ffn1_w_down"].astype(BF16)]], "gather_start_ffn1")
    zero = token_a[0, 0]
    w_in_bf = rows_to_bf16(sh["w_in"], "w_in_bf16", dep=token_a)
    sems_b, srcs_b, lands_b, g_token = start(
        [[w_in_bf, (sh["w_o"] + zero).astype(BF16)],
         [cat0(["ffn2_w_gate", "ffn2_w_up"], zero), (sh["ffn2_w_down"] + zero).astype(BF16)],
         [(sh["w_ple_gate"] + zero).astype(BF16), (sh["w_ple_proj"] + zero).astype(BF16)]],
        "gather_start_rest", after=token_a)
    order = ["ffn1_gu", "ffn1_d", "mix", "ffn2", "ple"]
    group_sizes = [1, 1, 2, 2, 2]
    g_sems, g_srcs, g_lands = sems_a + sems_b, srcs_a + srcs_b, lands_a + lands_b
    g_send, g_recv_d, g_recv_i = g_sems[0::3], g_sems[1::3], g_sems[2::3]
    first = np.cumsum([0] + group_sizes).tolist()
    passed = {}

    def arrays_of(group):
        k = order.index(group)
        return slice(first[k], first[k + 1])

    def forward(group, after):
        sl = arrays_of(group)
        f_sems, bufs = gather_forward(g_srcs[sl], g_lands[sl], g_recv_i[sl], after, "gather_forward_" + group)
        k = len(bufs) // 2
        passed[group] = (f_sems[0::2], f_sems[1::2], bufs[:k], bufs[k:])

    def weights(group, after):
        sl = arrays_of(group)
        f_send, f_recv, srcs, lands = passed[group]
        got = gather_wait(srcs, lands, g_send[sl], g_recv_d[sl], f_send, f_recv, after, "gather_wait_" + group)
        if group == "ffn1_gu":
            return (got[0].reshape(N_DEV * 2 * F8, D),)
        if group == "ffn1_d":
            return (got[0].reshape(N_DEV * F8, D),)
        a0, a1 = got
        if group == "ffn2":
            return a0.reshape(N_DEV * 2 * F8, D), a1.reshape(N_DEV * F8, D)
        if group == "ple":
            return a0.reshape(-1, D), a1.reshape(-1, a1.shape[2])
        w3, wf8 = _split_w_in(a0.reshape(N_DEV * WIN8, D))
        return w3, jnp.pad(wf8, ((0, LANES - N_FOX), (0, 0))), a1.reshape(-1, D)

    scatter_groups = {
        "ple": ["w_ple_gate", "w_ple_proj"],
        "ffn2": ["ffn2_w_gate", "ffn2_w_up", "ffn2_w_down"],
        "mix": ["w_in", "w_o"],
        "ffn1_d": ["ffn1_w_down"],
        "ffn1_g": ["ffn1_w_gate"],
        "ffn1_u": ["ffn1_w_up"],
    }
    x_i, y_i, c_i = _xyc()
    my_chip = 2 * x_i + y_i
    pair_stage, chip_stage, small_stage = {}, {}, {}

    def emit(group, grads, after=None):
        if group == "small":
            src = grads["small"]
            ss, rs, srcs, lands, token = exchange_start([src], [_landing(src, me)], [False], "scatter_start_small")
            small_stage["small"] = (ss, rs, srcs, lands)
            return token
        src4s = []
        for n in scatter_groups[group]:
            if n == "w_in":
                full = _join_w_in(grads["w3"], grads["wf"][:N_FOX])
                src4s.append(full.reshape(N_CHIPS, 2, WIN8, D))
            else:
                src4s.append(grads[n].reshape((N_CHIPS, 2) + sh[n].shape))
        lands = [lax.empty((N_CHIPS,) + s.shape[2:], BF16) for s in src4s]
        sems, bufs, token = scatter_pair_start(src4s, lands, "scatter_pair_start_" + group, after=after)
        k = len(src4s)
        pair_stage[group] = (sems, bufs[:k], bufs[k:])
        return token

    def emit2(group, after):
        sems, src4s, lands = pair_stage[group]
        src4s, lands = scatter_pair_wait(src4s, lands, sems, after, "scatter_pair_wait_" + group)
        sums = [chip_sum(s4, la, c_i, "chip_sum_" + n)
                for s4, la, n in zip(src4s, lands, scatter_groups[group])]
        chip_lands = [lax.empty(s.shape, s.dtype) for s in sums]
        sems, bufs, token = scatter_chip_start(sums, chip_lands, "scatter_chip_start_" + group)
        k = len(sums)
        chip_stage[group] = (sems, bufs[:k], bufs[k:])
        return token

    dx0 = local_step(
        xs, ps, tgt, w["norm_ffn1"], w["norm_mix"], w["norm_ffn2"], w["norm_ple"],
        w["norm_final"].reshape(1, D), w["b_f"], w["rel_table"], forward, weights, emit, emit2, g_token)

    res = {}
    after = dx0
    for group in ["ple", "ffn2", "mix", "ffn1_d", "ffn1_g", "ffn1_u"]:
        sems, sums, chip_lands = chip_stage[group]
        sums, parts = scatter_chip_wait(sums, chip_lands, sems, after, "scatter_chip_wait_" + group)
        for n, part, own in zip(scatter_groups[group], parts, sums):
            update = adamw_rows3d if n == "w_in" else adamw_sharded
            g, d, nm, nv = update(part, own, my_chip, sh[n], m_sh[n], v_sh[n], "adamw_" + n)
            res[n] = tuple(unview(a, n) for a in (g, d, nm, nv))
            after = g
    ss, rs, srcs, lands = small_stage["small"]
    small_parts, = exchange_wait(ss, rs, srcs, lands, [False], after, "scatter_wait_small")
    pack = lambda t: _pack_small(D, t["norm_ffn1"], t["norm_mix"], t["norm_ffn2"], t["norm_ple"],
                                 t["norm_final"], t["b_f"], t["rel_table"])
    gs, ds, ms, vs = adamw_small(small_parts, pack(w), pack(m), pack(v), "adamw_small")
    shapes = {n: w[n].shape for n in small_names}
    unpacked = [_unpack_small(a, shapes) for a in (gs, ds, ms, vs)]
    for n in small_names:
        res[n] = tuple(u[n] for u in unpacked)
    loss = gs[_ROW_LOSS, 0]

    out = [loss, dx0.reshape(x.shape)]
    for k in range(4):
        out += [res[n][k] for n in names]
    return tuple(out)
```

```python
import functools
import math

import numpy as np
import jax
import jax.numpy as jnp
from jax import lax
from jax.experimental import pallas as pl
from jax.experimental.pallas import tpu as pltpu

F32 = jnp.float32
BF16 = jnp.bfloat16

N_DEV = 8
HEAD_DIM = 128
N_FOX = 8
N_DIL = 8
N_HEADS = N_FOX + N_DIL
DILATED_PATTERNS = ((128, 1), (512, 4), (2048, 16))
N_REL_BUCKETS = 32
REL_MAX_DISTANCE = 2048
RMS_EPS = 1e-6
NEG_INF = -1e30
LANES = 128
VMEM_LIMIT = 56 * 1024 * 1024

ADAM_LR = 0.001
ADAM_B1 = 0.9
ADAM_B2 = 0.999
ADAM_EPS = 1e-08
ADAM_WD = 0.01
ADAM_STEP = 10

MESH = pl.DeviceIdType.MESH


def _params(sem):
    return pltpu.CompilerParams(dimension_semantics=sem, vmem_limit_bytes=VMEM_LIMIT)


def _dot(a, b, ca, cb, precision=None):
    return lax.dot_general(a, b, (((ca,), (cb,)), ((), ())),
                           preferred_element_type=F32, precision=precision)


def _sigmoid(z):
    return 1.0 / (1.0 + jnp.exp(-z))


def _tile(n, want):
    t = min(n, want)
    assert n % t == 0, (n, t)
    return t


def _dep_spec(ngrid):
    return pl.BlockSpec((8, LANES), lambda *_: (0, 0))


def rms_fwd(x, g, name, dep=None):
    T, D = x.shape
    tm = _tile(T, 256)

    def body(x_ref, g_ref, *rest):
        h_ref = rest[-1]
        xv = x_ref[...]
        r = lax.rsqrt(jnp.mean(xv * xv, axis=-1, keepdims=True) + RMS_EPS)
        h_ref[...] = (xv * r * g_ref[...]).astype(BF16)

    in_specs = [pl.BlockSpec((tm, D), lambda i: (i, 0)), pl.BlockSpec((1, D), lambda i: (0, 0))]
    args = [x, g]
    if dep is not None:
        in_specs.append(_dep_spec(1))
        args.append(dep)
    return pl.pallas_call(
        body, name=name, grid=(T // tm,), in_specs=in_specs,
        out_specs=pl.BlockSpec((tm, D), lambda i: (i, 0)),
        out_shape=jax.ShapeDtypeStruct((T, D), BF16),
        compiler_params=_params(("parallel",)),
    )(*args)


def rms_bwd(dh, x, g, dres, name, dep=None, half=True):
    T, D = x.shape
    tm = _tile(T, 256)

    def body(dh_ref, x_ref, g_ref, dres_ref, *rest):
        dx_ref, dg_ref = (rest[-3], rest[-1]) if half else (rest[-2], rest[-1])
        i = pl.program_id(0)
        xv = x_ref[...]
        r = lax.rsqrt(jnp.mean(xv * xv, axis=-1, keepdims=True) + RMS_EPS)
        xh = xv * r
        d = dh_ref[...]
        u = d * g_ref[...]
        dx = dres_ref[...] + r * (u - xh * jnp.mean(u * xh, axis=-1, keepdims=True))
        dx_ref[...] = dx
        if half:
            rest[-2][...] = (0.5 * dx).astype(BF16)
        part = jnp.sum(d * xh, axis=0, keepdims=True)

        @pl.when(i == 0)
        def _():
            dg_ref[...] = part

        @pl.when(i > 0)
        def _():
            dg_ref[...] += part

    row = pl.BlockSpec((tm, D), lambda i: (i, 0))
    vec = pl.BlockSpec((1, D), lambda i: (0, 0))
    in_specs = [row, row, vec, row]
    args = [dh, x, g, dres]
    if dep is not None:
        in_specs.append(_dep_spec(1))
        args.append(dep)
    out_specs = [row, row, vec] if half else [row, vec]
    out_shape = [jax.ShapeDtypeStruct((T, D), F32)] + ([jax.ShapeDtypeStruct((T, D), BF16)] if half else [])
    out_shape.append(jax.ShapeDtypeStruct((1, D), F32))
    outs = pl.pallas_call(
        body, name=name, grid=(T // tm,),
        in_specs=in_specs, out_specs=out_specs, out_shape=out_shape,
        compiler_params=_params(("arbitrary",)),
    )(*args)
    return tuple(outs) if half else (outs[0], None, outs[1])


def final_loss_bwd(x, g, target, name):
    T, D = x.shape
    tm = _tile(T, 256)

    def body(x_ref, g_ref, t_ref, loss_ref, dx_ref, dg_ref):
        i = pl.program_id(0)
        xv = x_ref[...]
        gv = g_ref[...]
        r = lax.rsqrt(jnp.mean(xv * xv, axis=-1, keepdims=True) + RMS_EPS)
        xh = xv * r
        e = xh * gv - t_ref[...]
        lpart = 0.5 * jnp.sum(jnp.mean(e * e, axis=-1, keepdims=True), axis=0, keepdims=True)
        lrow = jnp.broadcast_to(lpart, (1, LANES))
        d = e * (1.0 / D)
        u = d * gv
        dx_ref[...] = r * (u - xh * jnp.mean(u * xh, axis=-1, keepdims=True))
        part = jnp.sum(d * xh, axis=0, keepdims=True)

        @pl.when(i == 0)
        def _():
            dg_ref[...] = part
            loss_ref[...] = lrow

        @pl.when(i > 0)
        def _():
            dg_ref[...] += part
            loss_ref[...] += lrow

    row = pl.BlockSpec((tm, D), lambda i: (i, 0))
    vec = pl.BlockSpec((1, D), lambda i: (0, 0))
    return pl.pallas_call(
        body, name=name, grid=(T // tm,),
        in_specs=[row, vec, row],
        out_specs=[pl.BlockSpec((1, LANES), lambda i: (0, 0)), row, vec],
        out_shape=[jax.ShapeDtypeStruct((1, LANES), F32), jax.ShapeDtypeStruct((T, D), F32),
                   jax.ShapeDtypeStruct((1, D), F32)],
        compiler_params=_params(("arbitrary",)),
    )(x, g, target)


def _bf(v, scale=None):
    if scale is not None:
        v = v * scale
    return v.astype(BF16)


def mm_nn(a, b, name, *, tn, out_dtype, tm=512, n_out=None, b_block=None, b_map=None,
          res=None):
    T, K = a.shape
    N = n_out if n_out is not None else b.shape[1]
    tm = _tile(T, tm)
    tn = _tile(N, tn)
    b_block = b_block or (K, tn)
    b_map = b_map or (lambda n, i: (0, n))

    def body(*refs):
        a_ref, b_ref = refs[0], refs[1]
        o_ref = refs[-1]
        acc = _dot(_bf(a_ref[...]), _bf(b_ref[...]), 1, 0)
        if res is not None:
            acc = refs[2][...] + acc
        o_ref[...] = acc.astype(out_dtype)

    in_specs = [pl.BlockSpec((tm, K), lambda n, i: (i, 0)), pl.BlockSpec(b_block, b_map)]
    args = [a, b]
    if res is not None:
        in_specs.append(pl.BlockSpec((tm, tn), lambda n, i: (i, n)))
        args.append(res)
    return pl.pallas_call(
        body, name=name, grid=(N // tn, T // tm), in_specs=in_specs,
        out_specs=pl.BlockSpec((tm, tn), lambda n, i: (i, n)),
        out_shape=jax.ShapeDtypeStruct((T, N), out_dtype),
        compiler_params=_params(("parallel", "parallel")),
    )(*args)


def mm_nn_sum(pairs, name, *, tn, out_dtype, tm=512, dep=None):
    T = pairs[0][0].shape[0]
    N = pairs[0][1].shape[1]
    tm = _tile(T, tm)
    tn = _tile(N, tn)
    npair = len(pairs)

    def body(*refs):
        acc = None
        for q in range(npair):
            part = _dot(_bf(refs[2 * q][...]), _bf(refs[2 * q + 1][...]), 1, 0)
            acc = part if acc is None else acc + part
        refs[-1][...] = acc.astype(out_dtype)

    in_specs, args = [], []
    for a, b in pairs:
        K = a.shape[1]
        in_specs += [pl.BlockSpec((tm, K), lambda n, i: (i, 0)), pl.BlockSpec((K, tn), lambda n, i: (0, n))]
        args += [a, b]
    if dep is not None:
        in_specs.append(_dep_spec(2))
        args.append(dep)
    return pl.pallas_call(
        body, name=name, grid=(N // tn, T // tm), in_specs=in_specs,
        out_specs=pl.BlockSpec((tm, tn), lambda n, i: (i, n)),
        out_shape=jax.ShapeDtypeStruct((T, N), out_dtype),
        compiler_params=_params(("parallel", "parallel")),
    )(*args)


def mm_nt(pairs, name, *, tn, out_dtype, tm=512, dep=None):
    T = pairs[0][0].shape[0]
    N = pairs[0][1].shape[0]
    tm = _tile(T, tm)
    tn = _tile(N, tn)
    npair = len(pairs)

    def body(*refs):
        o_ref = refs[-1]
        acc = None
        for q in range(npair):
            part = _dot(_bf(refs[2 * q][...]), _bf(refs[2 * q + 1][...]), 1, 1)
            acc = part if acc is None else acc + part
        o_ref[...] = acc.astype(out_dtype)

    in_specs, args = [], []
    for a, b in pairs:
        K = a.shape[1]
        in_specs += [pl.BlockSpec((tm, K), lambda n, i: (i, 0)), pl.BlockSpec((tn, K), lambda n, i: (n, 0))]
        args += [a, b]
    if dep is not None:
        in_specs.append(_dep_spec(2))
        args.append(dep)
    return pl.pallas_call(
        body, name=name, grid=(N // tn, T // tm), in_specs=in_specs,
        out_specs=pl.BlockSpec((tm, tn), lambda n, i: (i, n)),
        out_shape=jax.ShapeDtypeStruct((T, N), out_dtype),
        compiler_params=_params(("parallel", "parallel")),
    )(*args)


def mm_tn(a, b, name, *, grid, a_block, a_map, b_block, b_map, o_block, o_map, out_shape,
          b_scale=None, dep=None):
    def body(a_ref, b_ref, *rest):
        rest[-1][...] = _dot(_bf(a_ref[...]), _bf(b_ref[...], b_scale), 0, 0).astype(BF16)

    in_specs = [pl.BlockSpec(a_block, a_map), pl.BlockSpec(b_block, b_map)]
    args = [a, b]
    if dep is not None:
        in_specs.append(_dep_spec(len(grid)))
        args.append(dep)
    return pl.pallas_call(
        body, name=name, grid=grid, in_specs=in_specs,
        out_specs=pl.BlockSpec(o_block, o_map),
        out_shape=jax.ShapeDtypeStruct(out_shape, BF16),
        compiler_params=_params(("parallel",) * len(grid)),
    )(*args)


def mm_tn_plain(a, b, name, *, tm=512, tn=512, b_scale=None):
    T, M = a.shape
    N = b.shape[1]
    tm = _tile(M, tm)
    tn = _tile(N, tn)
    return mm_tn(a, b, name, grid=(M // tm, N // tn),
                 a_block=(T, tm), a_map=lambda m, n: (0, m),
                 b_block=(T, tn), b_map=lambda m, n: (0, n),
                 o_block=(tm, tn), o_map=lambda m, n: (m, n),
                 out_shape=(M, N), b_scale=b_scale)


def ffn_up(h, wgu, gi, ui, nper, name):
    T, D = h.shape
    F8 = wgu.shape[0] // (N_DEV * nper)
    tm = _tile(T, 512)
    nt = T // tm

    def body(h_ref, wg_ref, wu_ref, ga_ref, gb_ref, s_ref):
        hv = h_ref[...]
        a = _dot(hv, wg_ref[...], 1, 1)
        b = _dot(hv, wu_ref[...], 1, 1)
        sg = _sigmoid(a)
        silu = a * sg
        ga_ref[...] = (b * (sg * (1.0 + a * (1.0 - sg)))).astype(BF16)
        gb_ref[...] = silu.astype(BF16)
        s_ref[...] = (silu * b).astype(BF16)

    blk = pl.BlockSpec((tm, F8), lambda j, i: (j * nt + i, 0))
    shp = jax.ShapeDtypeStruct((N_DEV * T, F8), BF16)
    return pl.pallas_call(
        body, name=name, grid=(N_DEV, nt),
        in_specs=[pl.BlockSpec((tm, D), lambda j, i: (i, 0)),
                  pl.BlockSpec((F8, D), lambda j, i: (j * nper + gi, 0)),
                  pl.BlockSpec((F8, D), lambda j, i: (j * nper + ui, 0))],
        out_specs=[blk, blk, blk], out_shape=[shp, shp, shp],
        compiler_params=_params(("parallel", "parallel")),
    )(h, wgu, wgu)


def ffn_down(s, wd, di, nper, x, name):
    T, D = x.shape
    F8 = s.shape[1]
    tm = _tile(T, 512)
    nt = T // tm

    def body(s_ref, w_ref, x_ref, o_ref, acc_ref):
        j = pl.program_id(1)
        part = _dot(s_ref[...], w_ref[...], 1, 0)

        @pl.when(j == 0)
        def _():
            acc_ref[...] = part

        @pl.when(j > 0)
        def _():
            acc_ref[...] += part

        @pl.when(j == N_DEV - 1)
        def _():
            o_ref[...] = x_ref[...] + 0.5 * acc_ref[...]

    return pl.pallas_call(
        body, name=name, grid=(nt, N_DEV),
        in_specs=[pl.BlockSpec((tm, F8), lambda i, j: (j * nt + i, 0)),
                  pl.BlockSpec((F8, D), lambda i, j: (j * nper + di, 0)),
                  pl.BlockSpec((tm, D), lambda i, j: (i, 0))],
        out_specs=pl.BlockSpec((tm, D), lambda i, j: (i, 0)),
        out_shape=jax.ShapeDtypeStruct((T, D), F32),
        scratch_shapes=[pltpu.VMEM((tm, D), F32)],
        compiler_params=_params(("parallel", "arbitrary")),
    )(s, wd, x)


def ffn_bwd_act(dxh, wd, di, nper_d, a, b, name, dep=None):
    T, D = dxh.shape
    F8 = a.shape[1]
    tm = _tile(T, 512)
    nt = T // tm

    def body(dx_ref, w_ref, a_ref, b_ref, *rest):
        da_ref, db_ref = rest[-2], rest[-1]
        ds = _dot(dx_ref[...], w_ref[...], 1, 1)
        da_ref[...] = (ds * a_ref[...].astype(F32)).astype(BF16)
        db_ref[...] = (ds * b_ref[...].astype(F32)).astype(BF16)

    blk = pl.BlockSpec((tm, F8), lambda j, i: (j * nt + i, 0))
    shp = jax.ShapeDtypeStruct((N_DEV * T, F8), BF16)
    in_specs = [pl.BlockSpec((tm, D), lambda j, i: (i, 0)),
                pl.BlockSpec((F8, D), lambda j, i: (j * nper_d + di, 0)), blk, blk]
    args = [dxh, wd, a, b]
    if dep is not None:
        in_specs.append(_dep_spec(2))
        args.append(dep)
    return pl.pallas_call(
        body, name=name, grid=(N_DEV, nt), in_specs=in_specs,
        out_specs=[blk, blk], out_shape=[shp, shp],
        compiler_params=_params(("parallel", "parallel")),
    )(*args)


def ffn_bwd_dh(da, db, wgu, gi, ui, nper, D, name, dep=None):
    F8 = da.shape[1]
    T = da.shape[0] // N_DEV
    tm = _tile(T, 512)
    nt = T // tm

    def body(da_ref, db_ref, wg_ref, wu_ref, *rest):
        o_ref, acc_ref = rest[-2], rest[-1]
        j = pl.program_id(1)
        part = _dot(da_ref[...], wg_ref[...], 1, 0) + _dot(db_ref[...], wu_ref[...], 1, 0)

        @pl.when(j == 0)
        def _():
            acc_ref[...] = part

        @pl.when(j > 0)
        def _():
            acc_ref[...] += part

        @pl.when(j == N_DEV - 1)
        def _():
            o_ref[...] = acc_ref[...]

    blk = pl.BlockSpec((tm, F8), lambda i, j: (j * nt + i, 0))
    in_specs = [blk, blk,
                pl.BlockSpec((F8, D), lambda i, j: (j * nper + gi, 0)),
                pl.BlockSpec((F8, D), lambda i, j: (j * nper + ui, 0))]
    args = [da, db, wgu, wgu]
    if dep is not None:
        in_specs.append(_dep_spec(2))
        args.append(dep)
    return pl.pallas_call(
        body, name=name, grid=(nt, N_DEV), in_specs=in_specs,
        out_specs=pl.BlockSpec((tm, D), lambda i, j: (i, 0)),
        out_shape=jax.ShapeDtypeStruct((T, D), F32),
        scratch_shapes=[pltpu.VMEM((tm, D), F32)],
        compiler_params=_params(("parallel", "arbitrary")),
    )(*args)


def ffn_bwd_dw_in(h, dact, name, dep=None):
    T, D = h.shape
    F8 = dact.shape[1]
    tm = _tile(D, 512)
    return mm_tn(dact, h, name, grid=(N_DEV, D // tm),
                 a_block=(T, F8), a_map=lambda j, m: (j, 0),
                 b_block=(T, tm), b_map=lambda j, m: (0, m),
                 o_block=(F8, tm), o_map=lambda j, m: (j, m),
                 out_shape=(N_DEV * F8, D), dep=dep)


def ffn_bwd_dw_down(s, dx, name):
    T, D = dx.shape
    F8 = s.shape[1]
    tn = _tile(D, 512)
    return mm_tn(s, dx, name, grid=(N_DEV, D // tn),
                 a_block=(T, F8), a_map=lambda j, n: (j, 0),
                 b_block=(T, tn), b_map=lambda j, n: (0, n),
                 o_block=(F8, tn), o_map=lambda j, n: (j, n),
                 out_shape=(N_DEV * F8, D))


def _t5_bucket_np(dist):
    max_exact = N_REL_BUCKETS // 2
    d = np.maximum(dist, 1).astype(np.float64)
    large = max_exact + (np.log(d / max_exact) / math.log(REL_MAX_DISTANCE / max_exact)
                         * (N_REL_BUCKETS - max_exact)).astype(np.int64)
    large32 = max_exact + (np.log(d.astype(np.float32) / np.float32(max_exact))
                           / np.float32(math.log(REL_MAX_DISTANCE / max_exact))
                           * np.float32(N_REL_BUCKETS - max_exact)).astype(np.int64)
    assert np.array_equal(large, large32)
    large = np.minimum(large, N_REL_BUCKETS - 1)
    return np.where(dist < max_exact, dist, large)


def _distance_tables(T, tq):
    dist = np.arange(T)
    mult = np.zeros(T, np.int64)
    for window, dilation in DILATED_PATTERNS:
        mult += ((dist % dilation == 0) & (dist // dilation <= window // dilation)).astype(np.int64)
    logm = np.where(mult > 0, np.log(np.maximum(mult, 1)), NEG_INF).astype(np.float32)
    bucket = _t5_bucket_np(dist).astype(np.int32)
    nkb = T // tq
    k = np.arange(nkb)[:, None, None]
    r = np.arange(tq)[None, :, None]
    c = np.arange(tq)[None, None, :]
    delta = k * tq + r - c
    return bucket, logm, delta


def _tile_buckets(T, tq):
    bucket, logm, delta = _distance_tables(T, tq)
    safe = np.maximum(delta, 0)
    bidx = np.where(delta >= 0, bucket[safe], -1).astype(np.int32)
    logm_t = np.where(delta >= 0, logm[safe], NEG_INF).astype(np.float32)
    present = [sorted(set(np.unique(bidx[k]).tolist()) - {-1}) for k in range(T // tq)]
    return bidx, logm_t, present


def bias_tiles(rel_table, T, tq):
    bidx, logm_t, present = _tile_buckets(T, tq)
    nkb = T // tq

    def body(tab_ref, b_ref, lm_ref, o_ref):
        slot = pl.program_id(0)

        @pl.when(slot == 0)
        def _():
            o_ref[...] = jnp.where(b_ref[...] >= 0, 0.0, NEG_INF)

        @pl.when(slot > 0)
        def _():
            for k in range(nkb):
                bi = b_ref[k]
                acc = lm_ref[k]
                for b in present[k]:
                    acc = acc + jnp.where(bi == b, tab_ref[b, slot - 1], 0.0)
                o_ref[k] = acc

    full = pl.BlockSpec((nkb, tq, tq), lambda s: (0, 0, 0))
    return pl.pallas_call(
        body, name="bias_tiles", grid=(1 + N_DIL,),
        in_specs=[pl.BlockSpec(memory_space=pltpu.SMEM), full, full],
        out_specs=pl.BlockSpec((None, nkb, tq, tq), lambda s: (s, 0, 0, 0)),
        out_shape=jax.ShapeDtypeStruct((1 + N_DIL, nkb, tq, tq), F32),
        compiler_params=_params(("parallel",)),
    )(rel_table, jnp.asarray(bidx), jnp.asarray(logm_t))


def fox_gate_fwd(uf, bf, name):
    T = uf.shape[0]
    tb = _tile(T, 512)

    def body(u_ref, b_ref, c_ref, ct_ref):
        lane = lax.broadcasted_iota(jnp.int32, (1, LANES), 1)
        tri = (lax.broadcasted_iota(jnp.int32, (tb, tb), 0)
               >= lax.broadcasted_iota(jnp.int32, (tb, tb), 1)).astype(F32)
        carry = jnp.zeros((1, LANES), F32)
        for blk in range(T // tb):
            z = u_ref[pl.ds(blk * tb, tb), :] + b_ref[...]
            lf = jnp.minimum(z, 0.0) - jnp.log1p(jnp.exp(-jnp.abs(z)))
            lf = jnp.where(lane < N_FOX, lf, 0.0)
            cb = _dot(tri, lf, 1, 0, precision=lax.Precision.HIGHEST) + carry
            c_ref[pl.ds(blk * tb, tb), :] = cb
            ct_ref[:, pl.ds(blk * tb, tb)] = cb.T
            carry = cb[tb - 1:tb, :]

    return pl.pallas_call(
        body, name=name,
        out_shape=[jax.ShapeDtypeStruct((T, LANES), F32), jax.ShapeDtypeStruct((LANES, T), F32)],
        compiler_params=_params(None),
    )(uf, bf)


def fox_gate_bwd(dct, uf, bf, name):
    T = uf.shape[0]
    tb = _tile(T, 512)

    def body(d_ref, u_ref, b_ref, du_ref, db_ref):
        lane = lax.broadcasted_iota(jnp.int32, (1, LANES), 1)
        triu = (lax.broadcasted_iota(jnp.int32, (tb, tb), 0)
                <= lax.broadcasted_iota(jnp.int32, (tb, tb), 1)).astype(F32)
        carry = jnp.zeros((1, LANES), F32)
        dbv = jnp.zeros((1, LANES), F32)
        for blk in reversed(range(T // tb)):
            dc = d_ref[:, pl.ds(blk * tb, tb)].T
            dlf = _dot(triu, dc, 1, 0, precision=lax.Precision.HIGHEST) + carry
            carry = dlf[0:1, :]
            z = u_ref[pl.ds(blk * tb, tb), :] + b_ref[...]
            dz = jnp.where(lane < N_FOX, dlf * (1.0 - _sigmoid(z)), 0.0)
            du_ref[pl.ds(blk * tb, tb), :] = dz
            dbv = dbv + jnp.sum(dz, axis=0, keepdims=True)
        db_ref[...] = dbv

    return pl.pallas_call(
        body, name=name,
        out_shape=[jax.ShapeDtypeStruct((T, LANES), F32), jax.ShapeDtypeStruct((1, LANES), F32)],
        compiler_params=_params(None),
    )(dct, uf, bf)


def _bias_slot(h):
    return jnp.maximum(h - (N_FOX - 1), 0)


def _scores(q_ref, k_ref, c_ref, ct_ref, tb_ref, h, i, tq, fox):
    scale = HEAD_DIM ** -0.5
    n = (i + 1) * tq
    rows = pl.ds(i * tq, tq)
    s = _dot(q_ref[rows, :], k_ref[pl.ds(0, n), :], 1, 1) * scale
    if not fox:
        return s + jnp.concatenate([tb_ref[i - jb] for jb in range(i + 1)], axis=1)
    lane = lax.broadcasted_iota(jnp.int32, (1, LANES), 1)
    c_col = jnp.sum(jnp.where(lane == h, c_ref[rows, :], 0.0), axis=1, keepdims=True)
    c_row = ct_ref[pl.ds(h, 1), pl.ds(0, n)]
    s = s + (c_col - c_row)
    if i == 0:
        return s + tb_ref[0]
    return jnp.concatenate([s[:, :i * tq], s[:, i * tq:] + tb_ref[0]], axis=1)


def _attn_specs(T, tq):
    nkb = T // tq
    return [
        pl.BlockSpec((T, HEAD_DIM), lambda h: (0, h)),
        pl.BlockSpec((T, HEAD_DIM), lambda h: (0, N_HEADS + h)),
        pl.BlockSpec((T, HEAD_DIM), lambda h: (0, 2 * N_HEADS + h)),
        pl.BlockSpec((T, LANES), lambda h: (0, 0)),
        pl.BlockSpec((LANES, T), lambda h: (0, 0)),
        pl.BlockSpec((None, nkb, tq, tq), lambda h: (_bias_slot(h), 0, 0, 0)),
    ]


def attention_fwd(qkv, c, ct, tiles, name):
    T = qkv.shape[0]
    tq = tiles.shape[2]

    def body(q_ref, k_ref, v_ref, c_ref, ct_ref, tb_ref, o_ref, lse_ref):
        h = pl.program_id(0)
        lane = lax.broadcasted_iota(jnp.int32, (1, LANES), 1)

        @pl.when(h == 0)
        def _():
            lse_ref[...] = jnp.zeros_like(lse_ref)

        def head(fox):
            for i in range(T // tq):
                rows = pl.ds(i * tq, tq)
                s = _scores(q_ref, k_ref, c_ref, ct_ref, tb_ref, h, i, tq, fox)
                m = jnp.max(s, axis=1, keepdims=True)
                p = jnp.exp(s - m)
                l = jnp.sum(p, axis=1, keepdims=True)
                o = _dot(p.astype(BF16), v_ref[pl.ds(0, (i + 1) * tq), :], 1, 0) * (1.0 / l)
                o_ref[rows, :] = o.astype(BF16)
                lse_ref[rows, :] = jnp.where(lane == h, m + jnp.log(l), lse_ref[rows, :])

        pl.when(h < N_FOX)(functools.partial(head, True))
        pl.when(h >= N_FOX)(functools.partial(head, False))

    return pl.pallas_call(
        body, name=name, grid=(N_HEADS,),
        in_specs=_attn_specs(T, tq),
        out_specs=[pl.BlockSpec((T, HEAD_DIM), lambda h: (0, h)), pl.BlockSpec((T, LANES), lambda h: (0, 0))],
        out_shape=[jax.ShapeDtypeStruct((T, N_HEADS * HEAD_DIM), BF16), jax.ShapeDtypeStruct((T, LANES), F32)],
        compiler_params=_params(("arbitrary",)),
    )(qkv, qkv, qkv, c, ct, tiles)


def attention_bwd(qkv, c, ct, tiles, lse, o, do, name):
    T = qkv.shape[0]
    tq = tiles.shape[2]
    nkb = T // tq
    scale = HEAD_DIM ** -0.5

    def body(q_ref, k_ref, v_ref, c_ref, ct_ref, tb_ref, lse_ref, o_ref, do_ref,
             dq_ref, dk_ref, dv_ref, dct_ref, dtb_ref, dk_acc, dv_acc):
        h = pl.program_id(0)
        lane = lax.broadcasted_iota(jnp.int32, (1, LANES), 1)
        dk_acc[...] = jnp.zeros_like(dk_acc)
        dv_acc[...] = jnp.zeros_like(dv_acc)
        dct_ref[...] = jnp.zeros_like(dct_ref)
        dtb_ref[...] = jnp.zeros_like(dtb_ref)

        def head(fox):
            for i in range(nkb):
                rows, keys = pl.ds(i * tq, tq), pl.ds(0, (i + 1) * tq)
                s = _scores(q_ref, k_ref, c_ref, ct_ref, tb_ref, h, i, tq, fox)
                lse_col = jnp.sum(jnp.where(lane == h, lse_ref[rows, :], 0.0), axis=1, keepdims=True)
                p = jnp.exp(s - lse_col)
                p_b = p.astype(BF16)
                dov = do_ref[rows, :]
                dp = _dot(dov, v_ref[keys, :], 1, 1)
                if fox:
                    delta = jnp.sum(p * dp, axis=1, keepdims=True)
                else:
                    delta = jnp.sum(dov.astype(F32) * o_ref[rows, :].astype(F32), axis=1, keepdims=True)
                ds = p * (dp - delta)
                ds_b = ds.astype(BF16)
                dq_ref[rows, :] = (_dot(ds_b, k_ref[keys, :], 1, 0) * scale).astype(BF16)
                dk_acc[:, keys] += _dot(q_ref[rows, :], ds_b, 0, 0) * scale
                dv_acc[:, keys] += _dot(dov, p_b, 0, 0)
                if fox:
                    dct_ref[:, keys] += -jnp.sum(ds, axis=0, keepdims=True)
                else:
                    for jb in range(i + 1):
                        dtb_ref[i - jb] += ds[:, jb * tq:(jb + 1) * tq]

        pl.when(h < N_FOX)(functools.partial(head, True))
        pl.when(h >= N_FOX)(functools.partial(head, False))
        dk_ref[...] = dk_acc[...].T.astype(BF16)
        dv_ref[...] = dv_acc[...].T.astype(BF16)

    head_cols = jax.ShapeDtypeStruct((T, N_HEADS * HEAD_DIM), BF16)
    col = pl.BlockSpec((T, HEAD_DIM), lambda h: (0, h))
    return pl.pallas_call(
        body, name=name, grid=(N_HEADS,),
        in_specs=_attn_specs(T, tq) + [pl.BlockSpec((T, LANES), lambda h: (0, 0)), col, col],
        out_specs=[col, col, col,
                   pl.BlockSpec((None, 1, T), lambda h: (h, 0, 0)),
                   pl.BlockSpec((None, nkb, tq, tq), lambda h: (_bias_slot(h), 0, 0, 0))],
        out_shape=[head_cols, head_cols, head_cols,
                   jax.ShapeDtypeStruct((N_HEADS, 1, T), F32),
                   jax.ShapeDtypeStruct((1 + N_DIL, nkb, tq, tq), F32)],
        scratch_shapes=[pltpu.VMEM((HEAD_DIM, T), F32), pltpu.VMEM((HEAD_DIM, T), F32)],
        compiler_params=_params(("arbitrary",)),
    )(qkv, qkv, qkv, c, ct, tiles, lse, o, do)


def rel_table_grad(dtiles, T, name):
    tq = dtiles.shape[2]
    nkb = T // tq
    bidx, _, present = _tile_buckets(T, tq)

    def body(d_ref, b_ref, o_ref):
        lane = lax.broadcasted_iota(jnp.int32, (1, LANES), 1)
        row = jnp.zeros((1, LANES), F32)
        for k in range(nkb):
            d = d_ref[k]
            bi = b_ref[k]
            for b in present[k]:
                v = jnp.sum(jnp.sum(jnp.where(bi == b, d, 0.0), axis=0, keepdims=True),
                            axis=1, keepdims=True)
                row = row + jnp.where(lane == b, v, 0.0)
        o_ref[...] = row

    return pl.pallas_call(
        body, name=name, grid=(N_DIL,),
        in_specs=[pl.BlockSpec((None, nkb, tq, tq), lambda h: (h + 1, 0, 0, 0)),
                  pl.BlockSpec((nkb, tq, tq), lambda h: (0, 0, 0))],
        out_specs=pl.BlockSpec((None, 1, LANES), lambda h: (h, 0, 0)),
        out_shape=jax.ShapeDtypeStruct((N_DIL, 1, LANES), F32),
        compiler_params=_params(("parallel",)),
    )(dtiles, jnp.asarray(bidx))


def ple_combine(x, z, pp, name):
    T, D = x.shape
    tm = _tile(T, 256)

    def body(x_ref, z_ref, p_ref, o_ref):
        o_ref[...] = x_ref[...] + _sigmoid(z_ref[...]) * p_ref[...]

    row = pl.BlockSpec((tm, D), lambda i: (i, 0))
    return pl.pallas_call(
        body, name=name, grid=(T // tm,), in_specs=[row, row, row], out_specs=row,
        out_shape=jax.ShapeDtypeStruct((T, D), F32), compiler_params=_params(("parallel",)),
    )(x, z, pp)


def ple_bwd_elem(dx, z, pp, name):
    T, D = dx.shape
    tm = _tile(T, 256)

    def body(dx_ref, z_ref, p_ref, dz_ref, dp_ref):
        gate = _sigmoid(z_ref[...])
        d = dx_ref[...]
        dz_ref[...] = (d * p_ref[...] * gate * (1.0 - gate)).astype(BF16)
        dp_ref[...] = (d * gate).astype(BF16)

    row = pl.BlockSpec((tm, D), lambda i: (i, 0))
    shp = jax.ShapeDtypeStruct((T, D), BF16)
    return pl.pallas_call(
        body, name=name, grid=(T // tm,), in_specs=[row, row, row], out_specs=[row, row],
        out_shape=[shp, shp], compiler_params=_params(("parallel",)),
    )(dx, z, pp)


def _peer_list():
    x, y, c = lax.axis_index("x"), lax.axis_index("y"), lax.axis_index("c")
    me = 4 * x + 2 * y + c
    peers = []
    for fx in (0, 1):
        for fy in (0, 1):
            for fc in (0, 1):
                if fx or fy or fc:
                    px = 1 - x if fx else x
                    py = 1 - y if fy else y
                    pc = 1 - c if fc else c
                    peers.append(((px, py, pc), 4 * px + 2 * py + pc))
    return me, peers


_HBM = pl.BlockSpec(memory_space=pltpu.HBM)
_SEM = pl.BlockSpec(memory_space=pltpu.SEMAPHORE)
_EFFECT = pltpu.SideEffectType.DATAFLOW_SIDE_EFFECTING
N_PEERS = N_DEV - 1


def _in_hbm(a):
    return pltpu.with_memory_space_constraint(a, pltpu.HBM)


def _exchange_copies(srcs, lands, send_sems, recv_sems, blockwise):
    me, peers = _peer_list()
    sends, recvs = [], []
    for a in range(len(srcs)):
        for k, (dev, idx) in enumerate(peers):
            src = srcs[a].at[idx] if blockwise[a] else srcs[a]
            sends.append(pltpu.make_async_remote_copy(
                src_ref=src, dst_ref=lands[a].at[me], send_sem=send_sems[a].at[k],
                recv_sem=recv_sems[a].at[k], device_id=dev, device_id_type=MESH))
            recvs.append(pltpu.make_async_remote_copy(
                src_ref=src, dst_ref=lands[a].at[idx], send_sem=send_sems[a].at[k],
                recv_sem=recv_sems[a].at[k], device_id=dev, device_id_type=MESH))
    return sends, recvs


def exchange_start(srcs, lands, blockwise, name):
    n = len(srcs)

    def body(*refs):
        src_in, land_in = refs[:n], refs[n:2 * n]
        send_sems, recv_sems = refs[2 * n:3 * n], refs[3 * n:4 * n]
        token = refs[6 * n]
        sends, _ = _exchange_copies(src_in, land_in, send_sems, recv_sems, blockwise)
        for cp in sends:
            cp.start()
        token[...] = jnp.zeros_like(token)

    out_shape = ([pltpu.SemaphoreType.DMA((N_PEERS,))] * (2 * n)
                 + [pltpu.HBM(s.shape, s.dtype) for s in srcs]
                 + [pltpu.HBM(l.shape, l.dtype) for l in lands]
                 + [jax.ShapeDtypeStruct((8, LANES), F32)])
    aliases = {a: 2 * n + a for a in range(2 * n)}
    outs = pl.pallas_call(
        body, name=name, out_shape=out_shape,
        in_specs=[_HBM] * (2 * n),
        out_specs=[_SEM] * (2 * n) + [_HBM] * (2 * n) + [pl.BlockSpec(memory_space=pltpu.VMEM)],
        input_output_aliases=aliases,
        compiler_params=pltpu.CompilerParams(has_side_effects=_EFFECT),
    )(*[_in_hbm(s) for s in srcs], *[_in_hbm(l) for l in lands])
    return (outs[:n], outs[n:2 * n], outs[2 * n:3 * n], outs[3 * n:4 * n], outs[4 * n])


def exchange_wait(send_sems, recv_sems, srcs, lands, blockwise, after, name):
    n = len(srcs)

    def body(*refs):
        src_in, land_in = refs[:n], refs[n:2 * n]
        ss, rs = refs[2 * n:3 * n], refs[3 * n:4 * n]
        sends, recvs = _exchange_copies(src_in, land_in, ss, rs, blockwise)
        for cp in sends:
            cp.wait_send()
        for cp in recvs:
            cp.wait_recv()

    outs = pl.pallas_call(
        body, name=name,
        out_shape=[pltpu.HBM(s.shape, s.dtype) for s in srcs] + [pltpu.HBM(l.shape, l.dtype) for l in lands],
        in_specs=[_HBM] * (2 * n) + [_SEM] * (2 * n) + [pl.BlockSpec(memory_space=pl.ANY)],
        out_specs=[_HBM] * (2 * n),
        input_output_aliases={a: a for a in range(2 * n)},
        compiler_params=pltpu.CompilerParams(has_side_effects=_EFFECT),
    )(*srcs, *lands, *send_sems, *recv_sems, after)
    return outs[n:]


def _landing(own_block, me, slots=N_DEV):
    empty = lax.empty((slots,) + own_block.shape, own_block.dtype)
    return lax.dynamic_update_slice(empty, own_block[None], (me,) + (0,) * own_block.ndim)


N_CHIPS = N_DEV // 2
_CHIP_FLIPS = ((1, 0), (0, 1), (1, 1))


def _xyc():
    return lax.axis_index("x"), lax.axis_index("y"), lax.axis_index("c")


def _other_chips(x, y):
    return [(1 - x if fx else x, 1 - y if fy else y) for fx, fy in _CHIP_FLIPS]


def _remote(src, dst, send_sem, recv_sem, dev):
    return pltpu.make_async_remote_copy(src_ref=src, dst_ref=dst, send_sem=send_sem, recv_sem=recv_sem,
                                        device_id=dev, device_id_type=MESH)


def comm_call(name, bufs, sems_in, sems_out, fn, after=None, want_token=False):
    nb, ni, no = len(bufs), len(sems_in), len(sems_out)
    afters = [] if after is None else (list(after) if isinstance(after, (list, tuple)) else [after])
    na = len(afters)

    def body(*refs):
        buf_refs = refs[:nb]
        sin = refs[nb:nb + ni]
        sout = refs[nb + ni + na:nb + ni + na + no]
        fn(buf_refs, sin, sout)
        if want_token:
            tok = refs[nb + ni + na + no + nb]
            tok[...] = jnp.zeros_like(tok)

    out_shape = list(sems_out) + [pltpu.HBM(b.shape, b.dtype) for b in bufs]
    out_specs = [_SEM] * no + [_HBM] * nb
    if want_token:
        out_shape.append(jax.ShapeDtypeStruct((8, LANES), F32))
        out_specs.append(pl.BlockSpec(memory_space=pltpu.VMEM))
    args = [_in_hbm(b) for b in bufs] + list(sems_in) + afters
    outs = pl.pallas_call(
        body, name=name, out_shape=out_shape,
        in_specs=[_HBM] * nb + [_SEM] * ni + [pl.BlockSpec(memory_space=pl.ANY)] * na,
        out_specs=out_specs, input_output_aliases={a: no + a for a in range(nb)},
        compiler_params=pltpu.CompilerParams(has_side_effects=_EFFECT),
    )(*args)
    return list(outs[:no]), list(outs[no:no + nb]), (outs[no + nb] if want_token else None)


def _dma_sems(*sizes):
    return [pltpu.SemaphoreType.DMA((s,)) for s in sizes]


def gather_start(srcs, lands, name, after=None):
    n = len(srcs)

    def fn(bufs, sin, sout):
        x, y, c = _xyc()
        me = 4 * x + 2 * y + c
        for a in range(n):
            src, land = bufs[a], bufs[n + a]
            send, recv_d, recv_i = sout[3 * a:3 * a + 3]
            _remote(src, land.at[me], send.at[0], recv_d.at[0], (x, y, 1 - c)).start()
            for k, (px, py) in enumerate(_other_chips(x, y)):
                _remote(src, land.at[me], send.at[1 + k], recv_i.at[k], (px, py, c)).start()

    return comm_call(name, list(srcs) + list(lands), [], _dma_sems(4, 1, 3) * n, fn, after=after, want_token=True)


def gather_forward(srcs, lands, recv_i, after, name):
    n = len(srcs)

    def fn(bufs, sin, sout):
        x, y, c = _xyc()
        for a in range(n):
            src, land = bufs[a], bufs[n + a]
            f_send, f_recv = sout[2 * a:2 * a + 2]
            for k, (px, py) in enumerate(_other_chips(x, y)):
                blk = land.at[4 * px + 2 * py + c]
                _remote(src, blk, f_send.at[k], sin[a].at[k], (px, py, c)).wait_recv()
                _remote(blk, blk, f_send.at[k], f_recv.at[k], (x, y, 1 - c)).start()

    sems, bufs, _ = comm_call(name, list(srcs) + list(lands), recv_i, _dma_sems(3, 3) * n, fn, after=after)
    return sems, bufs


def gather_wait(srcs, lands, send, recv_d, f_send, f_recv, after, name):
    n = len(srcs)

    def fn(bufs, sin, sout):
        x, y, c = _xyc()
        sib = (x, y, 1 - c)
        for a in range(n):
            src, land = bufs[a], bufs[n + a]
            s_send, s_recv_d, s_fsend, s_frecv = sin[4 * a:4 * a + 4]
            sib_blk = land.at[4 * x + 2 * y + 1 - c]
            for k in range(4):
                _remote(src, sib_blk, s_send.at[k], s_recv_d.at[0], sib).wait_send()
            _remote(src, sib_blk, s_send.at[0], s_recv_d.at[0], sib).wait_recv()
            for k, (px, py) in enumerate(_other_chips(x, y)):
                cp = _remote(src, land.at[4 * px + 2 * py + 1 - c], s_fsend.at[k], s_frecv.at[k], sib)
                cp.wait_send()
                cp.wait_recv()

    sems_in = []
    for a in range(n):
        sems_in += [send[a], recv_d[a], f_send[a], f_recv[a]]
    _, bufs, _ = comm_call(name, list(srcs) + list(lands), sems_in, [], fn, after=after)
    return bufs[n:]


def scatter_pair_start(src4s, lands, name, after=None):
    n = len(src4s)

    def fn(bufs, sin, sout):
        x, y, c = _xyc()
        for a in range(n):
            _remote(bufs[a].at[:, 1 - c], bufs[n + a], sout[2 * a].at[0], sout[2 * a + 1].at[0],
                    (x, y, 1 - c)).start()

    return comm_call(name, list(src4s) + list(lands), [], _dma_sems(1, 1) * n, fn, after=after, want_token=True)


def scatter_pair_wait(src4s, lands, sems, after, name):
    n = len(src4s)

    def fn(bufs, sin, sout):
        x, y, c = _xyc()
        for a in range(n):
            cp = _remote(bufs[a].at[:, 1 - c], bufs[n + a], sin[2 * a].at[0], sin[2 * a + 1].at[0], (x, y, 1 - c))
            cp.wait_send()
            cp.wait_recv()

    _, bufs, _ = comm_call(name, list(src4s) + list(lands), sems, [], fn, after=after)
    return bufs[:n], bufs[n:]


def _row_tile(R):
    for cand in (128, 64, 32, 16):
        if R % cand == 0:
            return cand
    return R


def chip_sum(src4, land, c, name):
    _, _, R, C = src4.shape
    tr = R

    def body(c_ref, a_ref, b_ref, o_ref):
        o_ref[...] = (a_ref[...].astype(F32) + b_ref[...].astype(F32)).astype(BF16)

    grid_spec = pltpu.PrefetchScalarGridSpec(
        num_scalar_prefetch=1, grid=(N_CHIPS, R // tr),
        in_specs=[pl.BlockSpec((None, None, tr, C), lambda q, i, cr: (q, cr[0], i, 0)),
                  pl.BlockSpec((None, tr, C), lambda q, i, cr: (q, i, 0))],
        out_specs=pl.BlockSpec((None, tr, C), lambda q, i, cr: (q, i, 0)))
    return pl.pallas_call(
        body, name=name, grid_spec=grid_spec,
        out_shape=jax.ShapeDtypeStruct((N_CHIPS, R, C), BF16),
        compiler_params=_params(("parallel", "parallel")),
    )(c.reshape(1).astype(jnp.int32), src4, land)


def scatter_chip_start(sums, lands, name):
    n = len(sums)

    def fn(bufs, sin, sout):
        x, y, c = _xyc()
        for a in range(n):
            for k, (px, py) in enumerate(_other_chips(x, y)):
                _remote(bufs[a].at[2 * px + py], bufs[n + a].at[2 * x + y], sout[2 * a].at[k], sout[2 * a + 1].at[k],
                        (px, py, c)).start()

    return comm_call(name, list(sums) + list(lands), [], _dma_sems(3, 3) * n, fn, want_token=True)


def scatter_chip_wait(sums, lands, sems, after, name):
    n = len(sums)

    def fn(bufs, sin, sout):
        x, y, c = _xyc()
        for a in range(n):
            for k, (px, py) in enumerate(_other_chips(x, y)):
                cp = _remote(bufs[a].at[2 * px + py], bufs[n + a].at[2 * px + py], sin[2 * a].at[k],
                             sin[2 * a + 1].at[k], (px, py, c))
                cp.wait_send()
                cp.wait_recv()

    _, bufs, _ = comm_call(name, list(sums) + list(lands), sems, [], fn, after=after)
    return bufs[:n], bufs[n:]


def _adamw_math(w, g, m, v):
    m = ADAM_B1 * m + (1.0 - ADAM_B1) * g
    v = ADAM_B2 * v + (1.0 - ADAM_B2) * (g * g)
    m_hat = m / (1.0 - ADAM_B1 ** ADAM_STEP)
    v_hat = v / (1.0 - ADAM_B2 ** ADAM_STEP)
    delta = -ADAM_LR * (m_hat / (jnp.sqrt(v_hat) + ADAM_EPS) + ADAM_WD * w)
    return delta, m, v


def _sum_partials(p_ref, own_ref, mine):
    own = own_ref[...].astype(F32)
    g = None
    for s in range(p_ref.shape[0]):
        term = jnp.where(mine == s, own, p_ref[s].astype(F32))
        g = term if g is None else g + term
    return g


def adamw_sharded(parts, sums, my_chip, w, m, v, name):
    R, C = w.shape
    S = parts.shape[0]
    tr = _row_tile(R)

    def body(mc_ref, p_ref, o_ref, w_ref, m_ref, v_ref, g_ref, d_ref, nm_ref, nv_ref):
        g = _sum_partials(p_ref, o_ref, mc_ref[0])
        delta, nm, nv = _adamw_math(w_ref[...], g, m_ref[...], v_ref[...])
        g_ref[...] = g
        d_ref[...] = delta
        nm_ref[...] = nm
        nv_ref[...] = nv

    row = pl.BlockSpec((tr, C), lambda i, mc: (i, 0))
    shp = jax.ShapeDtypeStruct((R, C), F32)
    grid_spec = pltpu.PrefetchScalarGridSpec(
        num_scalar_prefetch=1, grid=(R // tr,),
        in_specs=[pl.BlockSpec((S, tr, C), lambda i, mc: (0, i, 0)),
                  pl.BlockSpec((None, tr, C), lambda i, mc: (mc[0], i, 0)), row, row, row],
        out_specs=[row, row, row, row])
    return pl.pallas_call(
        body, name=name, grid_spec=grid_spec, out_shape=[shp, shp, shp, shp],
        compiler_params=_params(("parallel",)),
    )(my_chip.reshape(1).astype(jnp.int32), parts, sums, w, m, v)


def adamw_small(parts, w, m, v, name):
    R, C = w.shape

    def body(p_ref, w_ref, m_ref, v_ref, g_ref, d_ref, nm_ref, nv_ref):
        g = p_ref[0]
        for s in range(1, N_DEV):
            g = g + p_ref[s]
        delta, nm, nv = _adamw_math(w_ref[...], g, m_ref[...], v_ref[...])
        g_ref[...] = g
        d_ref[...] = delta
        nm_ref[...] = nm
        nv_ref[...] = nv

    shp = jax.ShapeDtypeStruct((R, C), F32)
    return pl.pallas_call(
        body, name=name, out_shape=[shp, shp, shp, shp], compiler_params=_params(None),
    )(parts, w, m, v)


_ROW_NORM_FFN1, _ROW_NORM_MIX, _ROW_NORM_FFN2, _ROW_NORM_PLE, _ROW_NORM_FINAL = 0, 1, 2, 3, 4
_ROW_B_F, _ROW_REL, _ROW_LOSS, _SMALL_ROWS = 5, 6, 7, 8


def _pack_small(D, norm_ffn1, norm_mix, norm_ffn2, norm_ple, norm_final, b_f, rel_table):
    def row(v):
        v = v.reshape(1, -1)
        return jnp.pad(v, ((0, 0), (0, D - v.shape[1])))
    return jnp.concatenate([row(norm_ffn1), row(norm_mix), row(norm_ffn2), row(norm_ple),
                            row(norm_final), row(b_f), row(rel_table),
                            jnp.zeros((1, D), F32)], axis=0)


def _unpack_small(a, shapes):
    return {"norm_ffn1": a[_ROW_NORM_FFN1].reshape(shapes["norm_ffn1"]),
            "norm_mix": a[_ROW_NORM_MIX].reshape(shapes["norm_mix"]),
            "b_f": a[_ROW_B_F, :N_FOX].reshape(shapes["b_f"]),
            "norm_ffn2": a[_ROW_NORM_FFN2].reshape(shapes["norm_ffn2"]),
            "norm_ple": a[_ROW_NORM_PLE].reshape(shapes["norm_ple"]),
            "rel_table": a[_ROW_REL, :N_REL_BUCKETS * N_DIL].reshape(shapes["rel_table"]),
            "norm_final": a[_ROW_NORM_FINAL].reshape(shapes["norm_final"])}


def local_step(x, p, tgt, g_ffn1, g_mix, g_ffn2, g_ple, g_final, b_f, rel_table,
               forward, weights, emit, emit2, first_dep):
    T, D = x.shape
    P = p.shape[1]
    CW = D // N_DEV
    tq = _tile(T, 256)

    h1 = rms_fwd(x, g_ffn1, "rms_ffn1", dep=first_dep)
    tiles = bias_tiles(rel_table, T, tq)
    forward("ffn1_gu", [tiles, h1])
    wgu1, = weights("ffn1_gu", h1)
    a1, b1, s1 = ffn_up(h1, wgu1, 0, 1, 2, "ffn1_up")
    forward("ffn1_d", s1)
    wd1, = weights("ffn1_d", s1)
    x1 = ffn_down(s1, wd1, 0, 1, x, "ffn1_down")

    h2 = rms_fwd(x1, g_mix, "rms_mix")
    forward("mix", h2)
    w3, wf, wo = weights("mix", h2)
    qkv = mm_nt([(h2, w3)], "mix_qkv", tn=768, out_dtype=BF16)
    uf = mm_nt([(h2, wf)], "mix_forget", tn=LANES, out_dtype=F32)
    bfp = jnp.pad(b_f.reshape(1, N_FOX), ((0, 0), (0, LANES - N_FOX)))
    c, ct = fox_gate_fwd(uf, bfp, "fox_gate")
    cat, lse = attention_fwd(qkv, c, ct, tiles, "attention")
    x2 = mm_nn(cat, wo, "mix_out", tn=512, out_dtype=F32, res=x1)

    h3 = rms_fwd(x2, g_ffn2, "rms_ffn2")
    forward("ffn2", h3)
    wgu2, wd2 = weights("ffn2", h3)
    a2, b2, s2 = ffn_up(h3, wgu2, 0, 1, 2, "ffn2_up")
    forward("ple", s2)
    x3 = ffn_down(s2, wd2, 0, 1, x2, "ffn2_down")

    h4 = rms_fwd(x3, g_ple, "rms_ple")
    wpg, wpp = weights("ple", h4)
    z = mm_nn(h4, wpg, "ple_gate", tn=512, out_dtype=F32)
    pp = mm_nn(p, wpp, "ple_proj", tn=CW, out_dtype=F32, n_out=D,
               b_block=(P, CW), b_map=lambda n, i: (n, 0))
    x4 = ple_combine(x3, z, pp, "ple_combine")
    loss_row, dx4, dg_final = final_loss_bwd(x4, g_final, tgt, "final_loss")

    grads = {}
    dz, dpp = ple_bwd_elem(dx4, z, pp, "ple_bwd_elem")
    grads["w_ple_proj"] = mm_tn(p, dpp, "ple_proj_dw", grid=(N_DEV,),
                                a_block=(T, P), a_map=lambda n: (0, 0),
                                b_block=(T, CW), b_map=lambda n: (0, n),
                                o_block=(P, CW), o_map=lambda n: (n, 0),
                                out_shape=(N_DEV * P, CW))
    grads["w_ple_gate"] = mm_tn_plain(h4, dz, "ple_gate_dw")
    tok = emit("ple", grads)
    dh4 = mm_nt([(dz, wpg)], "ple_gate_dh", tn=512, out_dtype=F32, dep=tok)
    tok = emit2("ple", dh4)
    dx3, dx3h, dg_ple = rms_bwd(dh4, x3, g_ple, dx4, "rms_ple_bwd", dep=tok)

    da2, db2 = ffn_bwd_act(dx3h, wd2, 0, 1, a2, b2, "ffn2_bwd_act")
    grads["ffn2_w_down"] = ffn_bwd_dw_down(s2, dx3h, "ffn2_down_dw")
    grads["ffn2_w_gate"] = ffn_bwd_dw_in(h3, da2, "ffn2_gate_dw")
    grads["ffn2_w_up"] = ffn_bwd_dw_in(h3, db2, "ffn2_up_dw")
    tok = emit("ffn2", grads)
    dh3 = ffn_bwd_dh(da2, db2, wgu2, 0, 1, 2, D, "ffn2_bwd_dh", dep=tok)
    tok = emit2("ffn2", dh3)
    dx2, _, dg_ffn2 = rms_bwd(dh3, x2, g_ffn2, dx3, "rms_ffn2_bwd", dep=tok, half=False)

    dcat = mm_nt([(dx2, wo)], "mix_out_dh", tn=512, out_dtype=BF16)
    grads["w_o"] = mm_tn_plain(cat, dx2, "mix_out_dw")
    dq, dk, dv, dct, dtiles = attention_bwd(qkv, c, ct, tiles, lse, cat, dcat, "attention_bwd")
    dctp = jnp.pad(dct[:, 0, :], ((0, LANES - N_HEADS), (0, 0)))
    duf, dbf = fox_gate_bwd(dctp, uf, bfp, "fox_gate_bwd")
    drel = rel_table_grad(dtiles, T, "rel_table_grad")[:, 0, :N_REL_BUCKETS].T
    du3 = jnp.concatenate([dq, dk, dv], axis=1)
    grads["w3"] = mm_tn_plain(du3, h2, "mix_qkv_dw", tm=768)
    grads["wf"] = mm_tn_plain(duf, h2, "mix_forget_dw", tm=LANES)
    tok = emit("mix", grads)
    dh2 = mm_nn_sum([(du3, w3), (duf, wf)], "mix_in_dh", tn=512, out_dtype=F32, dep=tok)
    tok = emit2("mix", dh2)
    dx1, dx1h, dg_mix = rms_bwd(dh2, x1, g_mix, dx2, "rms_mix_bwd", dep=tok)

    da1, db1 = ffn_bwd_act(dx1h, wd1, 0, 1, a1, b1, "ffn1_bwd_act")
    grads["ffn1_w_down"] = ffn_bwd_dw_down(s1, dx1h, "ffn1_down_dw")
    tok = emit("ffn1_d", grads)
    grads["ffn1_w_gate"] = ffn_bwd_dw_in(h1, da1, "ffn1_gate_dw", dep=tok)
    tok = emit2("ffn1_d", grads["ffn1_w_gate"])
    tok = emit("ffn1_g", grads, after=tok)
    grads["ffn1_w_up"] = ffn_bwd_dw_in(h1, db1, "ffn1_up_dw", dep=tok)
    tok = emit2("ffn1_g", grads["ffn1_w_up"])
    tok = emit("ffn1_u", grads, after=tok)
    dh1 = ffn_bwd_dh(da1, db1, wgu1, 0, 1, 2, D, "ffn1_bwd_dh", dep=tok)
    tok = emit2("ffn1_u", dh1)
    dx0, _, dg_ffn1 = rms_bwd(dh1, x, g_ffn1, dx1, "rms_ffn1_bwd", dep=tok, half=False)

    small = _pack_small(D, dg_ffn1, dg_mix, dg_ffn2, dg_ple, dg_final, dbf[:, :N_FOX], drel)
    small = small.at[_ROW_LOSS, :LANES].set(loss_row[0])
    grads["small"] = small
    emit("small", grads)
    return dx0


def _split_w_in(w_in_t):
    df, dd = N_FOX * HEAD_DIM, N_DIL * HEAD_DIM
    o = np.cumsum([0, df, df, df, N_FOX, dd, dd, dd]).tolist()
    qa, ka, va, f, qb, kb, vb = [w_in_t[o[i]:o[i + 1]] for i in range(7)]
    return jnp.concatenate([qa, qb, ka, kb, va, vb], axis=0), f


def _join_w_in(d3, dfg):
    df, dd = N_FOX * HEAD_DIM, N_DIL * HEAD_DIM
    o = np.cumsum([0, df, dd, df, dd, df, dd]).tolist()
    qa, qb, ka, kb, va, vb = [d3[o[i]:o[i + 1]] for i in range(6)]
    return jnp.concatenate([qa, ka, va, dfg, qb, kb, vb], axis=0)


def rows_to_bf16(a3, name, dep=None):
    R, _, C = a3.shape
    tc = _tile(C, 512)

    def body(a_ref, *rest):
        rest[-1][...] = a_ref[...].astype(BF16)

    in_specs = [pl.BlockSpec((R, None, tc), lambda n: (0, 0, n))]
    args = [a3]
    if dep is not None:
        in_specs.append(_dep_spec(1))
        args.append(dep)
    return pl.pallas_call(
        body, name=name, grid=(C // tc,), in_specs=in_specs,
        out_specs=pl.BlockSpec((R, tc), lambda n: (0, n)),
        out_shape=jax.ShapeDtypeStruct((R, C), BF16),
        compiler_params=_params(("parallel",)),
    )(*args)


def adamw_rows3d(parts, sums, my_chip, w3, m3, v3, name):
    R, _, C = w3.shape
    S = parts.shape[0]
    tc = _tile(C, 256)

    def body(mc_ref, p_ref, o_ref, w_ref, m_ref, v_ref, g_ref, d_ref, nm_ref, nv_ref):
        g = _sum_partials(p_ref, o_ref, mc_ref[0])
        delta, nm, nv = _adamw_math(w_ref[...], g, m_ref[...], v_ref[...])
        g_ref[...] = g
        d_ref[...] = delta
        nm_ref[...] = nm
        nv_ref[...] = nv

    col = pl.BlockSpec((R, None, tc), lambda n, mc: (0, 0, n))
    shp = jax.ShapeDtypeStruct((R, 1, C), F32)
    grid_spec = pltpu.PrefetchScalarGridSpec(
        num_scalar_prefetch=1, grid=(C // tc,),
        in_specs=[pl.BlockSpec((S, R, tc), lambda n, mc: (0, 0, n)),
                  pl.BlockSpec((None, R, tc), lambda n, mc: (mc[0], 0, n)), col, col, col],
        out_specs=[col, col, col, col])
    return pl.pallas_call(
        body, name=name, grid_spec=grid_spec, out_shape=[shp, shp, shp, shp],
        compiler_params=_params(("parallel",)),
    )(my_chip.reshape(1).astype(jnp.int32), parts, sums, w3, m3, v3)


def kernel(x, p, norm_ffn1, ffn1_w_gate, ffn1_w_up, ffn1_w_down, norm_mix, w_in, b_f, w_o, norm_ffn2, ffn2_w_gate, ffn2_w_up, ffn2_w_down, norm_ple, w_ple_gate, w_ple_proj, rel_table, norm_final, loss_target, m_norm_ffn1, m_ffn1_w_gate, m_ffn1_w_up, m_ffn1_w_down, m_norm_mix, m_w_in, m_b_f, m_w_o, m_norm_ffn2, m_ffn2_w_gate, m_ffn2_w_up, m_ffn2_w_down, m_norm_ple, m_w_ple_gate, m_w_ple_proj, m_rel_table, m_norm_final, v_norm_ffn1, v_ffn1_w_gate, v_ffn1_w_up, v_ffn1_w_down, v_norm_mix, v_w_in, v_b_f, v_w_o, v_norm_ffn2, v_ffn2_w_gate, v_ffn2_w_up, v_ffn2_w_down, v_norm_ple, v_w_ple_gate, v_w_ple_proj, v_rel_table, v_norm_final):
    names = ["norm_ffn1", "ffn1_w_gate", "ffn1_w_up", "ffn1_w_down", "norm_mix", "w_in", "b_f", "w_o",
             "norm_ffn2", "ffn2_w_gate", "ffn2_w_up", "ffn2_w_down", "norm_ple", "w_ple_gate",
             "w_ple_proj", "rel_table", "norm_final"]
    w = dict(zip(names, [norm_ffn1, ffn1_w_gate, ffn1_w_up, ffn1_w_down, norm_mix, w_in, b_f, w_o,
                         norm_ffn2, ffn2_w_gate, ffn2_w_up, ffn2_w_down, norm_ple, w_ple_gate,
                         w_ple_proj, rel_table, norm_final]))
    m = dict(zip(names, [m_norm_ffn1, m_ffn1_w_gate, m_ffn1_w_up, m_ffn1_w_down, m_norm_mix, m_w_in,
                         m_b_f, m_w_o, m_norm_ffn2, m_ffn2_w_gate, m_ffn2_w_up, m_ffn2_w_down,
                         m_norm_ple, m_w_ple_gate, m_w_ple_proj, m_rel_table, m_norm_final]))
    v = dict(zip(names, [v_norm_ffn1, v_ffn1_w_gate, v_ffn1_w_up, v_ffn1_w_down, v_norm_mix, v_w_in,
                         v_b_f, v_w_o, v_norm_ffn2, v_ffn2_w_gate, v_ffn2_w_up, v_ffn2_w_down,
                         v_norm_ple, v_w_ple_gate, v_w_ple_proj, v_rel_table, v_norm_final]))
    sharded = ["ffn1_w_gate", "ffn1_w_up", "ffn1_w_down", "w_in", "w_o", "ffn2_w_gate", "ffn2_w_up",
               "ffn2_w_down", "w_ple_gate", "w_ple_proj"]
    small_names = [n for n in names if n not in sharded]

    xs, ps, tgt = x[0], p[0, 0], loss_target[0]
    T, D = xs.shape
    transposed = ("ffn1_w_gate", "ffn1_w_up", "ffn2_w_gate", "ffn2_w_up")

    def view(t, n):
        if n in transposed:
            return t[n][0].T
        if n == "w_in":
            return jnp.transpose(t[n], (2, 0, 1))
        return t[n][0]

    def unview(a, n):
        if n in transposed:
            return a.T.reshape(w[n].shape)
        if n == "w_in":
            return jnp.transpose(a, (1, 2, 0))
        return a.reshape(w[n].shape)

    sh = {n: view(w, n) for n in sharded}
    m_sh = {n: view(m, n) for n in sharded}
    v_sh = {n: view(v, n) for n in sharded}
    F8 = sh["ffn1_w_down"].shape[0]
    WIN8 = sh["w_in"].shape[0]
    me = 4 * lax.axis_index("x") + 2 * lax.axis_index("y") + lax.axis_index("c")

    def start(groups, name, after=None):
        srcs = [s for grp in groups for s in grp]
        sems, bufs, token = gather_start(srcs, [_landing(s, me) for s in srcs], name, after=after)
        return sems, bufs[:len(srcs)], bufs[len(srcs):], token

    cat0 = lambda ns, z: (jnp.concatenate([sh[n] for n in ns], axis=0) + z).astype(BF16)
    sems_a, srcs_a, lands_a, token_a = start(
        [[cat0(["ffn1_w_gate", "ffn1_w_up"], 0.0)], [sh["ffn1_w_down"].astype(BF16)]], "gather_start_ffn1")
    zero = token_a[0, 0]
    w_in_bf = rows_to_bf16(sh["w_in"], "w_in_bf16", dep=token_a)
    sems_b, srcs_b, lands_b, g_token = start(
        [[w_in_bf, (sh["w_o"] + zero).astype(BF16)],
         [cat0(["ffn2_w_gate", "ffn2_w_up"], zero), (sh["ffn2_w_down"] + zero).astype(BF16)],
         [(sh["w_ple_gate"] + zero).astype(BF16), (sh["w_ple_proj"] + zero).astype(BF16)]],
        "gather_start_rest", after=token_a)
    order = ["ffn1_gu", "ffn1_d", "mix", "ffn2", "ple"]
    group_sizes = [1, 1, 2, 2, 2]
    g_sems, g_srcs, g_lands = sems_a + sems_b, srcs_a + srcs_b, lands_a + lands_b
    g_send, g_recv_d, g_recv_i = g_sems[0::3], g_sems[1::3], g_sems[2::3]
    first = np.cumsum([0] + group_sizes).tolist()
    passed = {}

    def arrays_of(group):
        k = order.index(group)
        return slice(first[k], first[k + 1])

    def forward(group, after):
        sl = arrays_of(group)
        f_sems, bufs = gather_forward(g_srcs[sl], g_lands[sl], g_recv_i[sl], after, "gather_forward_" + group)
        k = len(bufs) // 2
        passed[group] = (f_sems[0::2], f_sems[1::2], bufs[:k], bufs[k:])

    def weights(group, after):
        sl = arrays_of(group)
        f_send, f_recv, srcs, lands = passed[group]
        got = gather_wait(srcs, lands, g_send[sl], g_recv_d[sl], f_send, f_recv, after, "gather_wait_" + group)
        if group == "ffn1_gu":
            return (got[0].reshape(N_DEV * 2 * F8, D),)
        if group == "ffn1_d":
            return (got[0].reshape(N_DEV * F8, D),)
        a0, a1 = got
        if group == "ffn2":
            return a0.reshape(N_DEV * 2 * F8, D), a1.reshape(N_DEV * F8, D)
        if group == "ple":
            return a0.reshape(-1, D), a1.reshape(-1, a1.shape[2])
        w3, wf8 = _split_w_in(a0.reshape(N_DEV * WIN8, D))
        return w3, jnp.pad(wf8, ((0, LANES - N_FOX), (0, 0))), a1.reshape(-1, D)

    scatter_groups = {
        "ple": ["w_ple_gate", "w_ple_proj"],
        "ffn2": ["ffn2_w_gate", "ffn2_w_up", "ffn2_w_down"],
        "mix": ["w_in", "w_o"],
        "ffn1_d": ["ffn1_w_down"],
        "ffn1_g": ["ffn1_w_gate"],
        "ffn1_u": ["ffn1_w_up"],
    }
    x_i, y_i, c_i = _xyc()
    my_chip = 2 * x_i + y_i
    pair_stage, chip_stage, small_stage = {}, {}, {}

    def emit(group, grads, after=None):
        if group == "small":
            src = grads["small"]
            ss, rs, srcs, lands, token = exchange_start([src], [_landing(src, me)], [False], "scatter_start_small")
            small_stage["small"] = (ss, rs, srcs, lands)
            return token
        src4s = []
        for n in scatter_groups[group]:
            if n == "w_in":
                full = _join_w_in(grads["w3"], grads["wf"][:N_FOX])
                src4s.append(full.reshape(N_CHIPS, 2, WIN8, D))
            else:
                src4s.append(grads[n].reshape((N_CHIPS, 2) + sh[n].shape))
        lands = [lax.empty((N_CHIPS,) + s.shape[2:], BF16) for s in src4s]
        sems, bufs, token = scatter_pair_start(src4s, lands, "scatter_pair_start_" + group, after=after)
        k = len(src4s)
        pair_stage[group] = (sems, bufs[:k], bufs[k:])
        return token

    def emit2(group, after):
        sems, src4s, lands = pair_stage[group]
        src4s, lands = scatter_pair_wait(src4s, lands, sems, after, "scatter_pair_wait_" + group)
        sums = [chip_sum(s4, la, c_i, "chip_sum_" + n)
                for s4, la, n in zip(src4s, lands, scatter_groups[group])]
        chip_lands = [lax.empty(s.shape, s.dtype) for s in sums]
        sems, bufs, token = scatter_chip_start(sums, chip_lands, "scatter_chip_start_" + group)
        k = len(sums)
        chip_stage[group] = (sems, bufs[:k], bufs[k:])
        return token

    dx0 = local_step(
        xs, ps, tgt, w["norm_ffn1"], w["norm_mix"], w["norm_ffn2"], w["norm_ple"],
        w["norm_final"].reshape(1, D), w["b_f"], w["rel_table"], forward, weights, emit, emit2, g_token)

    res = {}
    after = dx0
    for group in ["ple", "ffn2", "mix", "ffn1_d", "ffn1_g", "ffn1_u"]:
        sems, sums, chip_lands = chip_stage[group]
        sums, parts = scatter_chip_wait(sums, chip_lands, sems, after, "scatter_chip_wait_" + group)
        for n, part, own in zip(scatter_groups[group], parts, sums):
            update = adamw_rows3d if n == "w_in" else adamw_sharded
            g, d, nm, nv = update(part, own, my_chip, sh[n], m_sh[n], v_sh[n], "adamw_" + n)
            res[n] = tuple(unview(a, n) for a in (g, d, nm, nv))
            after = g
    ss, rs, srcs, lands = small_stage["small"]
    small_parts, = exchange_wait(ss, rs, srcs, lands, [False], after, "scatter_wait_small")
    pack = lambda t: _pack_small(D, t["norm_ffn1"], t["norm_mix"], t["norm_ffn2"], t["norm_ple"],
                                 t["norm_final"], t["b_f"], t["rel_table"])
    gs, ds, ms, vs = adamw_small(small_parts, pack(w), pack(m), pack(v), "adamw_small")
    shapes = {n: w[n].shape for n in small_names}
    unpacked = [_unpack_small(a, shapes) for a in (gs, ds, ms, vs)]
    for n in small_names:
        res[n] = tuple(u[n] for u in unpacked)
    loss = gs[_ROW_LOSS, 0]

    out = [loss, dx0.reshape(x.shape)]
    for k in range(4):
        out += [res[n][k] for n in names]
    return tuple(out)
```

```python
import functools
import math

import numpy as np
import jax
import jax.numpy as jnp
from jax import lax
from jax.experimental import pallas as pl
from jax.experimental.pallas import tpu as pltpu

F32 = jnp.float32
BF16 = jnp.bfloat16

N_DEV = 8
HEAD_DIM = 128
N_FOX = 8
N_DIL = 8
N_HEADS = N_FOX + N_DIL
DILATED_PATTERNS = ((128, 1), (512, 4), (2048, 16))
N_REL_BUCKETS = 32
REL_MAX_DISTANCE = 2048
RMS_EPS = 1e-6
NEG_INF = -1e30
LANES = 128
VMEM_LIMIT = 56 * 1024 * 1024

ADAM_LR = 0.001
ADAM_B1 = 0.9
ADAM_B2 = 0.999
ADAM_EPS = 1e-08
ADAM_WD = 0.01
ADAM_STEP = 10

MESH = pl.DeviceIdType.MESH


def _params(sem):
    return pltpu.CompilerParams(dimension_semantics=sem, vmem_limit_bytes=VMEM_LIMIT)


def _dot(a, b, ca, cb, precision=None):
    return lax.dot_general(a, b, (((ca,), (cb,)), ((), ())),
                           preferred_element_type=F32, precision=precision)


def _sigmoid(z):
    return 1.0 / (1.0 + jnp.exp(-z))


def _tile(n, want):
    t = min(n, want)
    assert n % t == 0, (n, t)
    return t


def _dep_spec(ngrid):
    return pl.BlockSpec((8, LANES), lambda *_: (0, 0))


def rms_fwd(x, g, name, dep=None):
    T, D = x.shape
    tm = _tile(T, 256)

    def body(x_ref, g_ref, *rest):
        h_ref = rest[-1]
        xv = x_ref[...]
        r = lax.rsqrt(jnp.mean(xv * xv, axis=-1, keepdims=True) + RMS_EPS)
        h_ref[...] = (xv * r * g_ref[...]).astype(BF16)

    in_specs = [pl.BlockSpec((tm, D), lambda i: (i, 0)), pl.BlockSpec((1, D), lambda i: (0, 0))]
    args = [x, g]
    if dep is not None:
        in_specs.append(_dep_spec(1))
        args.append(dep)
    return pl.pallas_call(
        body, name=name, grid=(T // tm,), in_specs=in_specs,
        out_specs=pl.BlockSpec((tm, D), lambda i: (i, 0)),
        out_shape=jax.ShapeDtypeStruct((T, D), BF16),
        compiler_params=_params(("parallel",)),
    )(*args)


def rms_bwd(dh, x, g, dres, name, dep=None, half=True):
    T, D = x.shape
    tm = _tile(T, 256)

    def body(dh_ref, x_ref, g_ref, dres_ref, *rest):
        dx_ref, dg_ref = (rest[-3], rest[-1]) if half else (rest[-2], rest[-1])
        i = pl.program_id(0)
        xv = x_ref[...]
        r = lax.rsqrt(jnp.mean(xv * xv, axis=-1, keepdims=True) + RMS_EPS)
        xh = xv * r
        d = dh_ref[...]
        u = d * g_ref[...]
        dx = dres_ref[...] + r * (u - xh * jnp.mean(u * xh, axis=-1, keepdims=True))
        dx_ref[...] = dx
        if half:
            rest[-2][...] = (0.5 * dx).astype(BF16)
        part = jnp.sum(d * xh, axis=0, keepdims=True)

        @pl.when(i == 0)
        def _():
            dg_ref[...] = part

        @pl.when(i > 0)
        def _():
            dg_ref[...] += part

    row = pl.BlockSpec((tm, D), lambda i: (i, 0))
    vec = pl.BlockSpec((1, D), lambda i: (0, 0))
    in_specs = [row, row, vec, row]
    args = [dh, x, g, dres]
    if dep is not None:
        in_specs.append(_dep_spec(1))
        args.append(dep)
    out_specs = [row, row, vec] if half else [row, vec]
    out_shape = [jax.ShapeDtypeStruct((T, D), F32)] + ([jax.ShapeDtypeStruct((T, D), BF16)] if half else [])
    out_shape.append(jax.ShapeDtypeStruct((1, D), F32))
    outs = pl.pallas_call(
        body, name=name, grid=(T // tm,),
        in_specs=in_specs, out_specs=out_specs, out_shape=out_shape,
        compiler_params=_params(("arbitrary",)),
    )(*args)
    return tuple(outs) if half else (outs[0], None, outs[1])


def ple_loss(x, z, pp, g, target, name):
    T, D = x.shape
    tm = _tile(T, 256)

    def body(x_ref, z_ref, p_ref, g_ref, t_ref, loss_ref, dx_ref, dg_ref, dz_ref, dp_ref):
        i = pl.program_id(0)
        gate = _sigmoid(z_ref[...])
        ppv = p_ref[...]
        xv = x_ref[...] + gate * ppv
        gv = g_ref[...]
        r = lax.rsqrt(jnp.mean(xv * xv, axis=-1, keepdims=True) + RMS_EPS)
        xh = xv * r
        e = xh * gv - t_ref[...]
        lpart = 0.5 * jnp.sum(jnp.mean(e * e, axis=-1, keepdims=True), axis=0, keepdims=True)
        lrow = jnp.broadcast_to(lpart, (1, LANES))
        d = e * (1.0 / D)
        u = d * gv
        dx = r * (u - xh * jnp.mean(u * xh, axis=-1, keepdims=True))
        dx_ref[...] = dx
        dz_ref[...] = (dx * ppv * gate * (1.0 - gate)).astype(BF16)
        dp_ref[...] = (dx * gate).astype(BF16)
        part = jnp.sum(d * xh, axis=0, keepdims=True)

        @pl.when(i == 0)
        def _():
            dg_ref[...] = part
            loss_ref[...] = lrow

        @pl.when(i > 0)
        def _():
            dg_ref[...] += part
            loss_ref[...] += lrow

    row = pl.BlockSpec((tm, D), lambda i: (i, 0))
    vec = pl.BlockSpec((1, D), lambda i: (0, 0))
    return pl.pallas_call(
        body, name=name, grid=(T // tm,),
        in_specs=[row, row, row, vec, row],
        out_specs=[pl.BlockSpec((1, LANES), lambda i: (0, 0)), row, vec, row, row],
        out_shape=[jax.ShapeDtypeStruct((1, LANES), F32), jax.ShapeDtypeStruct((T, D), F32),
                   jax.ShapeDtypeStruct((1, D), F32), jax.ShapeDtypeStruct((T, D), BF16),
                   jax.ShapeDtypeStruct((T, D), BF16)],
        compiler_params=_params(("arbitrary",)),
    )(x, z, pp, g, target)


def _bf(v, scale=None):
    if scale is not None:
        v = v * scale
    return v.astype(BF16)


def mm_nn(a, b, name, *, tn, out_dtype, tm=512, n_out=None, b_block=None, b_map=None,
          res=None):
    T, K = a.shape
    N = n_out if n_out is not None else b.shape[1]
    tm = _tile(T, tm)
    tn = _tile(N, tn)
    b_block = b_block or (K, tn)
    b_map = b_map or (lambda n, i: (0, n))

    def body(*refs):
        a_ref, b_ref = refs[0], refs[1]
        o_ref = refs[-1]
        acc = _dot(_bf(a_ref[...]), _bf(b_ref[...]), 1, 0)
        if res is not None:
            acc = refs[2][...] + acc
        o_ref[...] = acc.astype(out_dtype)

    in_specs = [pl.BlockSpec((tm, K), lambda n, i: (i, 0)), pl.BlockSpec(b_block, b_map)]
    args = [a, b]
    if res is not None:
        in_specs.append(pl.BlockSpec((tm, tn), lambda n, i: (i, n)))
        args.append(res)
    return pl.pallas_call(
        body, name=name, grid=(N // tn, T // tm), in_specs=in_specs,
        out_specs=pl.BlockSpec((tm, tn), lambda n, i: (i, n)),
        out_shape=jax.ShapeDtypeStruct((T, N), out_dtype),
        compiler_params=_params(("parallel", "parallel")),
    )(*args)


def mm_nn_sum(pairs, name, *, tn, out_dtype, tm=512, dep=None):
    T = pairs[0][0].shape[0]
    N = pairs[0][1].shape[1]
    tm = _tile(T, tm)
    tn = _tile(N, tn)
    npair = len(pairs)

    def body(*refs):
        acc = None
        for q in range(npair):
            part = _dot(_bf(refs[2 * q][...]), _bf(refs[2 * q + 1][...]), 1, 0)
            acc = part if acc is None else acc + part
        refs[-1][...] = acc.astype(out_dtype)

    in_specs, args = [], []
    for a, b in pairs:
        K = a.shape[1]
        in_specs += [pl.BlockSpec((tm, K), lambda n, i: (i, 0)), pl.BlockSpec((K, tn), lambda n, i: (0, n))]
        args += [a, b]
    if dep is not None:
        in_specs.append(_dep_spec(2))
        args.append(dep)
    return pl.pallas_call(
        body, name=name, grid=(N // tn, T // tm), in_specs=in_specs,
        out_specs=pl.BlockSpec((tm, tn), lambda n, i: (i, n)),
        out_shape=jax.ShapeDtypeStruct((T, N), out_dtype),
        compiler_params=_params(("parallel", "parallel")),
    )(*args)


def mm_nt(pairs, name, *, tn, out_dtype, tm=512, dep=None):
    T = pairs[0][0].shape[0]
    N = pairs[0][1].shape[0]
    tm = _tile(T, tm)
    tn = _tile(N, tn)
    npair = len(pairs)

    def body(*refs):
        o_ref = refs[-1]
        acc = None
        for q in range(npair):
            part = _dot(_bf(refs[2 * q][...]), _bf(refs[2 * q + 1][...]), 1, 1)
            acc = part if acc is None else acc + part
        o_ref[...] = acc.astype(out_dtype)

    in_specs, args = [], []
    for a, b in pairs:
        K = a.shape[1]
        in_specs += [pl.BlockSpec((tm, K), lambda n, i: (i, 0)), pl.BlockSpec((tn, K), lambda n, i: (n, 0))]
        args += [a, b]
    if dep is not None:
        in_specs.append(_dep_spec(2))
        args.append(dep)
    return pl.pallas_call(
        body, name=name, grid=(N // tn, T // tm), in_specs=in_specs,
        out_specs=pl.BlockSpec((tm, tn), lambda n, i: (i, n)),
        out_shape=jax.ShapeDtypeStruct((T, N), out_dtype),
        compiler_params=_params(("parallel", "parallel")),
    )(*args)


def mm_tn(a, b, name, *, grid, a_block, a_map, b_block, b_map, o_block, o_map, out_shape,
          b_scale=None, dep=None):
    def body(a_ref, b_ref, *rest):
        rest[-1][...] = _dot(_bf(a_ref[...]), _bf(b_ref[...], b_scale), 0, 0).astype(BF16)

    in_specs = [pl.BlockSpec(a_block, a_map), pl.BlockSpec(b_block, b_map)]
    args = [a, b]
    if dep is not None:
        in_specs.append(_dep_spec(len(grid)))
        args.append(dep)
    return pl.pallas_call(
        body, name=name, grid=grid, in_specs=in_specs,
        out_specs=pl.BlockSpec(o_block, o_map),
        out_shape=jax.ShapeDtypeStruct(out_shape, BF16),
        compiler_params=_params(("parallel",) * len(grid)),
    )(*args)


def mm_tn_plain(a, b, name, *, tm=512, tn=512, b_scale=None):
    T, M = a.shape
    N = b.shape[1]
    tm = _tile(M, tm)
    tn = _tile(N, tn)
    return mm_tn(a, b, name, grid=(M // tm, N // tn),
                 a_block=(T, tm), a_map=lambda m, n: (0, m),
                 b_block=(T, tn), b_map=lambda m, n: (0, n),
                 o_block=(tm, tn), o_map=lambda m, n: (m, n),
                 out_shape=(M, N), b_scale=b_scale)


def ffn_up(h, wgu, gi, ui, nper, name):
    T, D = h.shape
    F8 = wgu.shape[0] // (N_DEV * nper)
    tm = _tile(T, 512)
    nt = T // tm

    def body(h_ref, wg_ref, wu_ref, ga_ref, gb_ref, s_ref):
        hv = h_ref[...]
        a = _dot(hv, wg_ref[...], 1, 1)
        b = _dot(hv, wu_ref[...], 1, 1)
        sg = _sigmoid(a)
        silu = a * sg
        ga_ref[...] = (b * (sg * (1.0 + a * (1.0 - sg)))).astype(BF16)
        gb_ref[...] = silu.astype(BF16)
        s_ref[...] = (silu * b).astype(BF16)

    blk = pl.BlockSpec((tm, F8), lambda j, i: (j * nt + i, 0))
    shp = jax.ShapeDtypeStruct((N_DEV * T, F8), BF16)
    return pl.pallas_call(
        body, name=name, grid=(N_DEV, nt),
        in_specs=[pl.BlockSpec((tm, D), lambda j, i: (i, 0)),
                  pl.BlockSpec((F8, D), lambda j, i: (j * nper + gi, 0)),
                  pl.BlockSpec((F8, D), lambda j, i: (j * nper + ui, 0))],
        out_specs=[blk, blk, blk], out_shape=[shp, shp, shp],
        compiler_params=_params(("parallel", "parallel")),
    )(h, wgu, wgu)


def ffn_down(s, wd, di, nper, x, name):
    T, D = x.shape
    F8 = s.shape[1]
    tm = _tile(T, 512)
    nt = T // tm

    def body(s_ref, w_ref, x_ref, o_ref, acc_ref):
        j = pl.program_id(1)
        part = _dot(s_ref[...], w_ref[...], 1, 0)

        @pl.when(j == 0)
        def _():
            acc_ref[...] = part

        @pl.when(j > 0)
        def _():
            acc_ref[...] += part

        @pl.when(j == N_DEV - 1)
        def _():
            o_ref[...] = x_ref[...] + 0.5 * acc_ref[...]

    return pl.pallas_call(
        body, name=name, grid=(nt, N_DEV),
        in_specs=[pl.BlockSpec((tm, F8), lambda i, j: (j * nt + i, 0)),
                  pl.BlockSpec((F8, D), lambda i, j: (j * nper + di, 0)),
                  pl.BlockSpec((tm, D), lambda i, j: (i, 0))],
        out_specs=pl.BlockSpec((tm, D), lambda i, j: (i, 0)),
        out_shape=jax.ShapeDtypeStruct((T, D), F32),
        scratch_shapes=[pltpu.VMEM((tm, D), F32)],
        compiler_params=_params(("parallel", "arbitrary")),
    )(s, wd, x)


def ffn_bwd_act(dxh, wd, di, nper_d, a, b, name, dep=None):
    T, D = dxh.shape
    F8 = a.shape[1]
    tm = _tile(T, 512)
    nt = T // tm

    def body(dx_ref, w_ref, a_ref, b_ref, *rest):
        da_ref, db_ref = rest[-2], rest[-1]
        ds = _dot(dx_ref[...], w_ref[...], 1, 1)
        da_ref[...] = (ds * a_ref[...].astype(F32)).astype(BF16)
        db_ref[...] = (ds * b_ref[...].astype(F32)).astype(BF16)

    blk = pl.BlockSpec((tm, F8), lambda j, i: (j * nt + i, 0))
    shp = jax.ShapeDtypeStruct((N_DEV * T, F8), BF16)
    in_specs = [pl.BlockSpec((tm, D), lambda j, i: (i, 0)),
                pl.BlockSpec((F8, D), lambda j, i: (j * nper_d + di, 0)), blk, blk]
    args = [dxh, wd, a, b]
    if dep is not None:
        in_specs.append(_dep_spec(2))
        args.append(dep)
    return pl.pallas_call(
        body, name=name, grid=(N_DEV, nt), in_specs=in_specs,
        out_specs=[blk, blk], out_shape=[shp, shp],
        compiler_params=_params(("parallel", "parallel")),
    )(*args)


def ffn_bwd_dh(da, db, wgu, gi, ui, nper, D, name, dep=None):
    F8 = da.shape[1]
    T = da.shape[0] // N_DEV
    tm = _tile(T, 512)
    nt = T // tm

    def body(da_ref, db_ref, wg_ref, wu_ref, *rest):
        o_ref, acc_ref = rest[-2], rest[-1]
        j = pl.program_id(1)
        part = _dot(da_ref[...], wg_ref[...], 1, 0) + _dot(db_ref[...], wu_ref[...], 1, 0)

        @pl.when(j == 0)
        def _():
            acc_ref[...] = part

        @pl.when(j > 0)
        def _():
            acc_ref[...] += part

        @pl.when(j == N_DEV - 1)
        def _():
            o_ref[...] = acc_ref[...]

    blk = pl.BlockSpec((tm, F8), lambda i, j: (j * nt + i, 0))
    in_specs = [blk, blk,
                pl.BlockSpec((F8, D), lambda i, j: (j * nper + gi, 0)),
                pl.BlockSpec((F8, D), lambda i, j: (j * nper + ui, 0))]
    args = [da, db, wgu, wgu]
    if dep is not None:
        in_specs.append(_dep_spec(2))
        args.append(dep)
    return pl.pallas_call(
        body, name=name, grid=(nt, N_DEV), in_specs=in_specs,
        out_specs=pl.BlockSpec((tm, D), lambda i, j: (i, 0)),
        out_shape=jax.ShapeDtypeStruct((T, D), F32),
        scratch_shapes=[pltpu.VMEM((tm, D), F32)],
        compiler_params=_params(("parallel", "arbitrary")),
    )(*args)


def ffn_bwd_dw_in(h, dact, name, dep=None):
    T, D = h.shape
    F8 = dact.shape[1]
    tm = _tile(D, 512)
    return mm_tn(dact, h, name, grid=(N_DEV, D // tm),
                 a_block=(T, F8), a_map=lambda j, m: (j, 0),
                 b_block=(T, tm), b_map=lambda j, m: (0, m),
                 o_block=(F8, tm), o_map=lambda j, m: (j, m),
                 out_shape=(N_DEV * F8, D), dep=dep)


def ffn_bwd_dw_down(s, dx, name):
    T, D = dx.shape
    F8 = s.shape[1]
    tn = _tile(D, 512)
    return mm_tn(s, dx, name, grid=(N_DEV, D // tn),
                 a_block=(T, F8), a_map=lambda j, n: (j, 0),
                 b_block=(T, tn), b_map=lambda j, n: (0, n),
                 o_block=(F8, tn), o_map=lambda j, n: (j, n),
                 out_shape=(N_DEV * F8, D))


def _t5_bucket_np(dist):
    max_exact = N_REL_BUCKETS // 2
    d = np.maximum(dist, 1).astype(np.float64)
    large = max_exact + (np.log(d / max_exact) / math.log(REL_MAX_DISTANCE / max_exact)
                         * (N_REL_BUCKETS - max_exact)).astype(np.int64)
    large32 = max_exact + (np.log(d.astype(np.float32) / np.float32(max_exact))
                           / np.float32(math.log(REL_MAX_DISTANCE / max_exact))
                           * np.float32(N_REL_BUCKETS - max_exact)).astype(np.int64)
    assert np.array_equal(large, large32)
    large = np.minimum(large, N_REL_BUCKETS - 1)
    return np.where(dist < max_exact, dist, large)


def _distance_tables(T, tq):
    dist = np.arange(T)
    mult = np.zeros(T, np.int64)
    for window, dilation in DILATED_PATTERNS:
        mult += ((dist % dilation == 0) & (dist // dilation <= window // dilation)).astype(np.int64)
    logm = np.where(mult > 0, np.log(np.maximum(mult, 1)), NEG_INF).astype(np.float32)
    bucket = _t5_bucket_np(dist).astype(np.int32)
    nkb = T // tq
    k = np.arange(nkb)[:, None, None]
    r = np.arange(tq)[None, :, None]
    c = np.arange(tq)[None, None, :]
    delta = k * tq + r - c
    return bucket, logm, delta


def _tile_buckets(T, tq):
    bucket, logm, delta = _distance_tables(T, tq)
    safe = np.maximum(delta, 0)
    bidx = np.where(delta >= 0, bucket[safe], -1).astype(np.int32)
    logm_t = np.where(delta >= 0, logm[safe], NEG_INF).astype(np.float32)
    present = [sorted(set(np.unique(bidx[k]).tolist()) - {-1}) for k in range(T // tq)]
    return bidx, logm_t, present


def bias_tiles(rel_table, T, tq):
    bidx, logm_t, present = _tile_buckets(T, tq)
    nkb = T // tq

    def body(tab_ref, b_ref, lm_ref, o_ref):
        slot = pl.program_id(0)

        @pl.when(slot == 0)
        def _():
            o_ref[...] = jnp.where(b_ref[...] >= 0, 0.0, NEG_INF)

        @pl.when(slot > 0)
        def _():
            for k in range(nkb):
                bi = b_ref[k]
                acc = lm_ref[k]
                for b in present[k]:
                    acc = acc + jnp.where(bi == b, tab_ref[b, slot - 1], 0.0)
                o_ref[k] = acc

    full = pl.BlockSpec((nkb, tq, tq), lambda s: (0, 0, 0))
    return pl.pallas_call(
        body, name="bias_tiles", grid=(1 + N_DIL,),
        in_specs=[pl.BlockSpec(memory_space=pltpu.SMEM), full, full],
        out_specs=pl.BlockSpec((None, nkb, tq, tq), lambda s: (s, 0, 0, 0)),
        out_shape=jax.ShapeDtypeStruct((1 + N_DIL, nkb, tq, tq), F32),
        compiler_params=_params(("parallel",)),
    )(rel_table, jnp.asarray(bidx), jnp.asarray(logm_t))


def fox_gate_fwd(uf, bf, name):
    T = uf.shape[0]
    tb = _tile(T, 512)

    def body(u_ref, b_ref, c_ref, ct_ref):
        lane = lax.broadcasted_iota(jnp.int32, (1, LANES), 1)
        tri = (lax.broadcasted_iota(jnp.int32, (tb, tb), 0)
               >= lax.broadcasted_iota(jnp.int32, (tb, tb), 1)).astype(F32)
        carry = jnp.zeros((1, LANES), F32)
        for blk in range(T // tb):
            z = u_ref[pl.ds(blk * tb, tb), :] + b_ref[...]
            lf = jnp.minimum(z, 0.0) - jnp.log1p(jnp.exp(-jnp.abs(z)))
            lf = jnp.where(lane < N_FOX, lf, 0.0)
            cb = _dot(tri, lf, 1, 0, precision=lax.Precision.HIGHEST) + carry
            c_ref[pl.ds(blk * tb, tb), :] = cb
            ct_ref[:, pl.ds(blk * tb, tb)] = cb.T
            carry = cb[tb - 1:tb, :]

    return pl.pallas_call(
        body, name=name,
        out_shape=[jax.ShapeDtypeStruct((T, LANES), F32), jax.ShapeDtypeStruct((LANES, T), F32)],
        compiler_params=_params(None),
    )(uf, bf)


def fox_gate_bwd(dct, uf, bf, name):
    T = uf.shape[0]
    tb = _tile(T, 512)

    def body(d_ref, u_ref, b_ref, du_ref, db_ref):
        lane = lax.broadcasted_iota(jnp.int32, (1, LANES), 1)
        triu = (lax.broadcasted_iota(jnp.int32, (tb, tb), 0)
                <= lax.broadcasted_iota(jnp.int32, (tb, tb), 1)).astype(F32)
        carry = jnp.zeros((1, LANES), F32)
        dbv = jnp.zeros((1, LANES), F32)
        for blk in reversed(range(T // tb)):
            dc = d_ref[:, pl.ds(blk * tb, tb)].T
            dlf = _dot(triu, dc, 1, 0, precision=lax.Precision.HIGHEST) + carry
            carry = dlf[0:1, :]
            z = u_ref[pl.ds(blk * tb, tb), :] + b_ref[...]
            dz = jnp.where(lane < N_FOX, dlf * (1.0 - _sigmoid(z)), 0.0)
            du_ref[pl.ds(blk * tb, tb), :] = dz
            dbv = dbv + jnp.sum(dz, axis=0, keepdims=True)
        db_ref[...] = dbv

    return pl.pallas_call(
        body, name=name,
        out_shape=[jax.ShapeDtypeStruct((T, LANES), F32), jax.ShapeDtypeStruct((1, LANES), F32)],
        compiler_params=_params(None),
    )(dct, uf, bf)


def _bias_slot(h):
    return jnp.maximum(h - (N_FOX - 1), 0)


def _scores(q_ref, k_ref, c_ref, ct_ref, tb_ref, h, i, tq, fox):
    scale = HEAD_DIM ** -0.5
    n = (i + 1) * tq
    rows = pl.ds(i * tq, tq)
    s = _dot(q_ref[rows, :], k_ref[pl.ds(0, n), :], 1, 1) * scale
    if not fox:
        return s + jnp.concatenate([tb_ref[i - jb] for jb in range(i + 1)], axis=1)
    lane = lax.broadcasted_iota(jnp.int32, (1, LANES), 1)
    c_col = jnp.sum(jnp.where(lane == h, c_ref[rows, :], 0.0), axis=1, keepdims=True)
    c_row = ct_ref[pl.ds(h, 1), pl.ds(0, n)]
    s = s + (c_col - c_row)
    if i == 0:
        return s + tb_ref[0]
    return jnp.concatenate([s[:, :i * tq], s[:, i * tq:] + tb_ref[0]], axis=1)


def _attn_specs(T, tq):
    nkb = T // tq
    return [
        pl.BlockSpec((T, HEAD_DIM), lambda h: (0, h)),
        pl.BlockSpec((T, HEAD_DIM), lambda h: (0, N_HEADS + h)),
        pl.BlockSpec((T, HEAD_DIM), lambda h: (0, 2 * N_HEADS + h)),
        pl.BlockSpec((T, LANES), lambda h: (0, 0)),
        pl.BlockSpec((LANES, T), lambda h: (0, 0)),
        pl.BlockSpec((None, nkb, tq, tq), lambda h: (_bias_slot(h), 0, 0, 0)),
    ]


def attention_fwd(qkv, c, ct, tiles, name):
    T = qkv.shape[0]
    tq = tiles.shape[2]

    def body(q_ref, k_ref, v_ref, c_ref, ct_ref, tb_ref, o_ref, lse_ref):
        h = pl.program_id(0)
        lane = lax.broadcasted_iota(jnp.int32, (1, LANES), 1)

        @pl.when(h == 0)
        def _():
            lse_ref[...] = jnp.zeros_like(lse_ref)

        def head(fox):
            for i in range(T // tq):
                rows = pl.ds(i * tq, tq)
                s = _scores(q_ref, k_ref, c_ref, ct_ref, tb_ref, h, i, tq, fox)
                m = jnp.max(s, axis=1, keepdims=True)
                p = jnp.exp(s - m)
                l = jnp.sum(p, axis=1, keepdims=True)
                o = _dot(p.astype(BF16), v_ref[pl.ds(0, (i + 1) * tq), :], 1, 0) * (1.0 / l)
                o_ref[rows, :] = o.astype(BF16)
                lse_ref[rows, :] = jnp.where(lane == h, m + jnp.log(l), lse_ref[rows, :])

        pl.when(h < N_FOX)(functools.partial(head, True))
        pl.when(h >= N_FOX)(functools.partial(head, False))

    return pl.pallas_call(
        body, name=name, grid=(N_HEADS,),
        in_specs=_attn_specs(T, tq),
        out_specs=[pl.BlockSpec((T, HEAD_DIM), lambda h: (0, h)), pl.BlockSpec((T, LANES), lambda h: (0, 0))],
        out_shape=[jax.ShapeDtypeStruct((T, N_HEADS * HEAD_DIM), BF16), jax.ShapeDtypeStruct((T, LANES), F32)],
        compiler_params=_params(("arbitrary",)),
    )(qkv, qkv, qkv, c, ct, tiles)


def attention_bwd(qkv, c, ct, tiles, lse, o, do, name):
    T = qkv.shape[0]
    tq = tiles.shape[2]
    nkb = T // tq
    scale = HEAD_DIM ** -0.5

    def body(q_ref, k_ref, v_ref, c_ref, ct_ref, tb_ref, lse_ref, o_ref, do_ref,
             dq_ref, dk_ref, dv_ref, dct_ref, dtb_ref, dk_acc, dv_acc):
        h = pl.program_id(0)
        lane = lax.broadcasted_iota(jnp.int32, (1, LANES), 1)
        dk_acc[...] = jnp.zeros_like(dk_acc)
        dv_acc[...] = jnp.zeros_like(dv_acc)
        dct_ref[...] = jnp.zeros_like(dct_ref)
        dtb_ref[...] = jnp.zeros_like(dtb_ref)

        def head(fox):
            for i in range(nkb):
                rows, keys = pl.ds(i * tq, tq), pl.ds(0, (i + 1) * tq)
                s = _scores(q_ref, k_ref, c_ref, ct_ref, tb_ref, h, i, tq, fox)
                lse_col = jnp.sum(jnp.where(lane == h, lse_ref[rows, :], 0.0), axis=1, keepdims=True)
                p = jnp.exp(s - lse_col)
                p_b = p.astype(BF16)
                dov = do_ref[rows, :]
                dp = _dot(dov, v_ref[keys, :], 1, 1)
                if fox:
                    delta = jnp.sum(p * dp, axis=1, keepdims=True)
                else:
                    delta = jnp.sum(dov.astype(F32) * o_ref[rows, :].astype(F32), axis=1, keepdims=True)
                ds = p * (dp - delta)
                ds_b = ds.astype(BF16)
                dq_ref[rows, :] = (_dot(ds_b, k_ref[keys, :], 1, 0) * scale).astype(BF16)
                dk_acc[:, keys] += _dot(q_ref[rows, :], ds_b, 0, 0) * scale
                dv_acc[:, keys] += _dot(dov, p_b, 0, 0)
                if fox:
                    dct_ref[:, keys] += -jnp.sum(ds, axis=0, keepdims=True)
                else:
                    for jb in range(i + 1):
                        dtb_ref[i - jb] += ds[:, jb * tq:(jb + 1) * tq]

        pl.when(h < N_FOX)(functools.partial(head, True))
        pl.when(h >= N_FOX)(functools.partial(head, False))
        dk_ref[...] = dk_acc[...].T.astype(BF16)
        dv_ref[...] = dv_acc[...].T.astype(BF16)

    head_cols = jax.ShapeDtypeStruct((T, N_HEADS * HEAD_DIM), BF16)
    col = pl.BlockSpec((T, HEAD_DIM), lambda h: (0, h))
    return pl.pallas_call(
        body, name=name, grid=(N_HEADS,),
        in_specs=_attn_specs(T, tq) + [pl.BlockSpec((T, LANES), lambda h: (0, 0)), col, col],
        out_specs=[col, col, col,
                   pl.BlockSpec((None, 1, T), lambda h: (h, 0, 0)),
                   pl.BlockSpec((None, nkb, tq, tq), lambda h: (_bias_slot(h), 0, 0, 0))],
        out_shape=[head_cols, head_cols, head_cols,
                   jax.ShapeDtypeStruct((N_HEADS, 1, T), F32),
                   jax.ShapeDtypeStruct((1 + N_DIL, nkb, tq, tq), F32)],
        scratch_shapes=[pltpu.VMEM((HEAD_DIM, T), F32), pltpu.VMEM((HEAD_DIM, T), F32)],
        compiler_params=_params(("arbitrary",)),
    )(qkv, qkv, qkv, c, ct, tiles, lse, o, do)


def rel_table_grad(dtiles, T, name):
    tq = dtiles.shape[2]
    nkb = T // tq
    bidx, _, present = _tile_buckets(T, tq)

    def body(d_ref, b_ref, o_ref):
        lane = lax.broadcasted_iota(jnp.int32, (1, LANES), 1)
        row = jnp.zeros((1, LANES), F32)
        for k in range(nkb):
            d = d_ref[k]
            bi = b_ref[k]
            for b in present[k]:
                v = jnp.sum(jnp.sum(jnp.where(bi == b, d, 0.0), axis=0, keepdims=True),
                            axis=1, keepdims=True)
                row = row + jnp.where(lane == b, v, 0.0)
        o_ref[...] = row

    return pl.pallas_call(
        body, name=name, grid=(N_DIL,),
        in_specs=[pl.BlockSpec((None, nkb, tq, tq), lambda h: (h + 1, 0, 0, 0)),
                  pl.BlockSpec((nkb, tq, tq), lambda h: (0, 0, 0))],
        out_specs=pl.BlockSpec((None, 1, LANES), lambda h: (h, 0, 0)),
        out_shape=jax.ShapeDtypeStruct((N_DIL, 1, LANES), F32),
        compiler_params=_params(("parallel",)),
    )(dtiles, jnp.asarray(bidx))


def _peer_list():
    x, y, c = lax.axis_index("x"), lax.axis_index("y"), lax.axis_index("c")
    me = 4 * x + 2 * y + c
    peers = []
    for fx in (0, 1):
        for fy in (0, 1):
            for fc in (0, 1):
                if fx or fy or fc:
                    px = 1 - x if fx else x
                    py = 1 - y if fy else y
                    pc = 1 - c if fc else c
                    peers.append(((px, py, pc), 4 * px + 2 * py + pc))
    return me, peers


_HBM = pl.BlockSpec(memory_space=pltpu.HBM)
_SEM = pl.BlockSpec(memory_space=pltpu.SEMAPHORE)
_EFFECT = pltpu.SideEffectType.DATAFLOW_SIDE_EFFECTING
N_PEERS = N_DEV - 1


def _in_hbm(a):
    return pltpu.with_memory_space_constraint(a, pltpu.HBM)


def _exchange_copies(srcs, lands, send_sems, recv_sems, blockwise):
    me, peers = _peer_list()
    sends, recvs = [], []
    for a in range(len(srcs)):
        for k, (dev, idx) in enumerate(peers):
            src = srcs[a].at[idx] if blockwise[a] else srcs[a]
            sends.append(pltpu.make_async_remote_copy(
                src_ref=src, dst_ref=lands[a].at[me], send_sem=send_sems[a].at[k],
                recv_sem=recv_sems[a].at[k], device_id=dev, device_id_type=MESH))
            recvs.append(pltpu.make_async_remote_copy(
                src_ref=src, dst_ref=lands[a].at[idx], send_sem=send_sems[a].at[k],
                recv_sem=recv_sems[a].at[k], device_id=dev, device_id_type=MESH))
    return sends, recvs


def exchange_start(srcs, lands, blockwise, name):
    n = len(srcs)

    def body(*refs):
        src_in, land_in = refs[:n], refs[n:2 * n]
        send_sems, recv_sems = refs[2 * n:3 * n], refs[3 * n:4 * n]
        token = refs[6 * n]
        sends, _ = _exchange_copies(src_in, land_in, send_sems, recv_sems, blockwise)
        for cp in sends:
            cp.start()
        token[...] = jnp.zeros_like(token)

    out_shape = ([pltpu.SemaphoreType.DMA((N_PEERS,))] * (2 * n)
                 + [pltpu.HBM(s.shape, s.dtype) for s in srcs]
                 + [pltpu.HBM(l.shape, l.dtype) for l in lands]
                 + [jax.ShapeDtypeStruct((8, LANES), F32)])
    aliases = {a: 2 * n + a for a in range(2 * n)}
    outs = pl.pallas_call(
        body, name=name, out_shape=out_shape,
        in_specs=[_HBM] * (2 * n),
        out_specs=[_SEM] * (2 * n) + [_HBM] * (2 * n) + [pl.BlockSpec(memory_space=pltpu.VMEM)],
        input_output_aliases=aliases,
        compiler_params=pltpu.CompilerParams(has_side_effects=_EFFECT),
    )(*[_in_hbm(s) for s in srcs], *[_in_hbm(l) for l in lands])
    return (outs[:n], outs[n:2 * n], outs[2 * n:3 * n], outs[3 * n:4 * n], outs[4 * n])


def exchange_wait(send_sems, recv_sems, srcs, lands, blockwise, after, name):
    n = len(srcs)

    def body(*refs):
        src_in, land_in = refs[:n], refs[n:2 * n]
        ss, rs = refs[2 * n:3 * n], refs[3 * n:4 * n]
        sends, recvs = _exchange_copies(src_in, land_in, ss, rs, blockwise)
        for cp in sends:
            cp.wait_send()
        for cp in recvs:
            cp.wait_recv()

    outs = pl.pallas_call(
        body, name=name,
        out_shape=[pltpu.HBM(s.shape, s.dtype) for s in srcs] + [pltpu.HBM(l.shape, l.dtype) for l in lands],
        in_specs=[_HBM] * (2 * n) + [_SEM] * (2 * n) + [pl.BlockSpec(memory_space=pl.ANY)],
        out_specs=[_HBM] * (2 * n),
        input_output_aliases={a: a for a in range(2 * n)},
        compiler_params=pltpu.CompilerParams(has_side_effects=_EFFECT),
    )(*srcs, *lands, *send_sems, *recv_sems, after)
    return outs[n:]


def _landing(own_block, me, slots=N_DEV):
    empty = lax.empty((slots,) + own_block.shape, own_block.dtype)
    return lax.dynamic_update_slice(empty, own_block[None], (me,) + (0,) * own_block.ndim)


N_CHIPS = N_DEV // 2
_CHIP_FLIPS = ((1, 0), (0, 1), (1, 1))


def _xyc():
    return lax.axis_index("x"), lax.axis_index("y"), lax.axis_index("c")


def _other_chips(x, y):
    return [(1 - x if fx else x, 1 - y if fy else y) for fx, fy in _CHIP_FLIPS]


def _remote(src, dst, send_sem, recv_sem, dev):
    return pltpu.make_async_remote_copy(src_ref=src, dst_ref=dst, send_sem=send_sem, recv_sem=recv_sem,
                                        device_id=dev, device_id_type=MESH)


def comm_call(name, bufs, sems_in, sems_out, fn, after=None, want_token=False):
    nb, ni, no = len(bufs), len(sems_in), len(sems_out)
    afters = [] if after is None else (list(after) if isinstance(after, (list, tuple)) else [after])
    na = len(afters)

    def body(*refs):
        buf_refs = refs[:nb]
        sin = refs[nb:nb + ni]
        sout = refs[nb + ni + na:nb + ni + na + no]
        fn(buf_refs, sin, sout)
        if want_token:
            tok = refs[nb + ni + na + no + nb]
            tok[...] = jnp.zeros_like(tok)

    out_shape = list(sems_out) + [pltpu.HBM(b.shape, b.dtype) for b in bufs]
    out_specs = [_SEM] * no + [_HBM] * nb
    if want_token:
        out_shape.append(jax.ShapeDtypeStruct((8, LANES), F32))
        out_specs.append(pl.BlockSpec(memory_space=pltpu.VMEM))
    args = [_in_hbm(b) for b in bufs] + list(sems_in) + afters
    outs = pl.pallas_call(
        body, name=name, out_shape=out_shape,
        in_specs=[_HBM] * nb + [_SEM] * ni + [pl.BlockSpec(memory_space=pl.ANY)] * na,
        out_specs=out_specs, input_output_aliases={a: no + a for a in range(nb)},
        compiler_params=pltpu.CompilerParams(has_side_effects=_EFFECT),
    )(*args)
    return list(outs[:no]), list(outs[no:no + nb]), (outs[no + nb] if want_token else None)


def _dma_sems(*sizes):
    return [pltpu.SemaphoreType.DMA((s,)) for s in sizes]


def gather_start(srcs, lands, name, after=None):
    n = len(srcs)

    def fn(bufs, sin, sout):
        x, y, c = _xyc()
        me = 4 * x + 2 * y + c
        for a in range(n):
            src, land = bufs[a], bufs[n + a]
            send, recv_d, recv_i = sout[3 * a:3 * a + 3]
            _remote(src, land.at[me], send.at[0], recv_d.at[0], (x, y, 1 - c)).start()
            for k, (px, py) in enumerate(_other_chips(x, y)):
                _remote(src, land.at[me], send.at[1 + k], recv_i.at[k], (px, py, c)).start()

    return comm_call(name, list(srcs) + list(lands), [], _dma_sems(4, 1, 3) * n, fn, after=after, want_token=True)


def gather_forward(srcs, lands, recv_i, after, name):
    n = len(srcs)

    def fn(bufs, sin, sout):
        x, y, c = _xyc()
        for a in range(n):
            src, land = bufs[a], bufs[n + a]
            f_send, f_recv = sout[2 * a:2 * a + 2]
            for k, (px, py) in enumerate(_other_chips(x, y)):
                blk = land.at[4 * px + 2 * py + c]
                _remote(src, blk, f_send.at[k], sin[a].at[k], (px, py, c)).wait_recv()
                _remote(blk, blk, f_send.at[k], f_recv.at[k], (x, y, 1 - c)).start()

    sems, bufs, _ = comm_call(name, list(srcs) + list(lands), recv_i, _dma_sems(3, 3) * n, fn, after=after)
    return sems, bufs


def gather_wait(srcs, lands, send, recv_d, f_send, f_recv, after, name):
    n = len(srcs)

    def fn(bufs, sin, sout):
        x, y, c = _xyc()
        sib = (x, y, 1 - c)
        for a in range(n):
            src, land = bufs[a], bufs[n + a]
            s_send, s_recv_d, s_fsend, s_frecv = sin[4 * a:4 * a + 4]
            sib_blk = land.at[4 * x + 2 * y + 1 - c]
            for k in range(4):
                _remote(src, sib_blk, s_send.at[k], s_recv_d.at[0], sib).wait_send()
            _remote(src, sib_blk, s_send.at[0], s_recv_d.at[0], sib).wait_recv()
            for k, (px, py) in enumerate(_other_chips(x, y)):
                cp = _remote(src, land.at[4 * px + 2 * py + 1 - c], s_fsend.at[k], s_frecv.at[k], sib)
                cp.wait_send()
                cp.wait_recv()

    sems_in = []
    for a in range(n):
        sems_in += [send[a], recv_d[a], f_send[a], f_recv[a]]
    _, bufs, _ = comm_call(name, list(srcs) + list(lands), sems_in, [], fn, after=after)
    return bufs[n:]


def scatter_pair_start(src4s, lands, name, after=None):
    n = len(src4s)

    def fn(bufs, sin, sout):
        x, y, c = _xyc()
        for a in range(n):
            _remote(bufs[a].at[:, 1 - c], bufs[n + a], sout[2 * a].at[0], sout[2 * a + 1].at[0],
                    (x, y, 1 - c)).start()

    return comm_call(name, list(src4s) + list(lands), [], _dma_sems(1, 1) * n, fn, after=after, want_token=True)


def scatter_pair_wait(src4s, lands, sems, after, name):
    n = len(src4s)

    def fn(bufs, sin, sout):
        x, y, c = _xyc()
        for a in range(n):
            cp = _remote(bufs[a].at[:, 1 - c], bufs[n + a], sin[2 * a].at[0], sin[2 * a + 1].at[0], (x, y, 1 - c))
            cp.wait_send()
            cp.wait_recv()

    _, bufs, _ = comm_call(name, list(src4s) + list(lands), sems, [], fn, after=after)
    return bufs[:n], bufs[n:]


def _row_tile(R):
    for cand in range(256, 15, -16):
        if R % cand == 0:
            return cand
    return R


def chip_sum(src4, land, c, name):
    _, _, R, C = src4.shape
    tr = R

    def body(c_ref, a_ref, b_ref, o_ref):
        o_ref[...] = (a_ref[...].astype(F32) + b_ref[...].astype(F32)).astype(BF16)

    grid_spec = pltpu.PrefetchScalarGridSpec(
        num_scalar_prefetch=1, grid=(N_CHIPS, R // tr),
        in_specs=[pl.BlockSpec((None, None, tr, C), lambda q, i, cr: (q, cr[0], i, 0)),
                  pl.BlockSpec((None, tr, C), lambda q, i, cr: (q, i, 0))],
        out_specs=pl.BlockSpec((None, tr, C), lambda q, i, cr: (q, i, 0)))
    return pl.pallas_call(
        body, name=name, grid_spec=grid_spec,
        out_shape=jax.ShapeDtypeStruct((N_CHIPS, R, C), BF16),
        compiler_params=_params(("parallel", "parallel")),
    )(c.reshape(1).astype(jnp.int32), src4, land)


def scatter_chip_start(sums, lands, name):
    n = len(sums)

    def fn(bufs, sin, sout):
        x, y, c = _xyc()
        for a in range(n):
            for k, (px, py) in enumerate(_other_chips(x, y)):
                _remote(bufs[a].at[2 * px + py], bufs[n + a].at[2 * x + y], sout[2 * a].at[k], sout[2 * a + 1].at[k],
                        (px, py, c)).start()

    return comm_call(name, list(sums) + list(lands), [], _dma_sems(3, 3) * n, fn, want_token=True)


def scatter_chip_wait(sums, lands, sems, after, name):
    n = len(sums)

    def fn(bufs, sin, sout):
        x, y, c = _xyc()
        for a in range(n):
            for k, (px, py) in enumerate(_other_chips(x, y)):
                cp = _remote(bufs[a].at[2 * px + py], bufs[n + a].at[2 * px + py], sin[2 * a].at[k],
                             sin[2 * a + 1].at[k], (px, py, c))
                cp.wait_send()
                cp.wait_recv()

    _, bufs, _ = comm_call(name, list(sums) + list(lands), sems, [], fn, after=after)
    return bufs[:n], bufs[n:]


def _adamw_math(w, g, m, v):
    m = ADAM_B1 * m + (1.0 - ADAM_B1) * g
    v = ADAM_B2 * v + (1.0 - ADAM_B2) * (g * g)
    m_hat = m / (1.0 - ADAM_B1 ** ADAM_STEP)
    v_hat = v / (1.0 - ADAM_B2 ** ADAM_STEP)
    delta = -ADAM_LR * (m_hat / (jnp.sqrt(v_hat) + ADAM_EPS) + ADAM_WD * w)
    return delta, m, v


def _sum_partials(p_ref, own_ref, mine):
    own = own_ref[...].astype(F32)
    g = None
    for s in range(p_ref.shape[0]):
        term = jnp.where(mine == s, own, p_ref[s].astype(F32))
        g = term if g is None else g + term
    return g


def adamw_sharded(parts, sums, my_chip, w, m, v, name):
    R, C = w.shape
    S = parts.shape[0]
    tr = _row_tile(R)

    def body(mc_ref, p_ref, o_ref, w_ref, m_ref, v_ref, g_ref, d_ref, nm_ref, nv_ref):
        g = _sum_partials(p_ref, o_ref, mc_ref[0])
        delta, nm, nv = _adamw_math(w_ref[...], g, m_ref[...], v_ref[...])
        g_ref[...] = g
        d_ref[...] = delta
        nm_ref[...] = nm
        nv_ref[...] = nv

    row = pl.BlockSpec((tr, C), lambda i, mc: (i, 0))
    shp = jax.ShapeDtypeStruct((R, C), F32)
    grid_spec = pltpu.PrefetchScalarGridSpec(
        num_scalar_prefetch=1, grid=(R // tr,),
        in_specs=[pl.BlockSpec((S, tr, C), lambda i, mc: (0, i, 0)),
                  pl.BlockSpec((None, tr, C), lambda i, mc: (mc[0], i, 0)), row, row, row],
        out_specs=[row, row, row, row])
    return pl.pallas_call(
        body, name=name, grid_spec=grid_spec, out_shape=[shp, shp, shp, shp],
        compiler_params=_params(("parallel",)),
    )(my_chip.reshape(1).astype(jnp.int32), parts, sums, w, m, v)


def adamw_small(parts, w, m, v, name):
    R, C = w.shape

    def body(p_ref, w_ref, m_ref, v_ref, g_ref, d_ref, nm_ref, nv_ref):
        g = p_ref[0]
        for s in range(1, N_DEV):
            g = g + p_ref[s]
        delta, nm, nv = _adamw_math(w_ref[...], g, m_ref[...], v_ref[...])
        g_ref[...] = g
        d_ref[...] = delta
        nm_ref[...] = nm
        nv_ref[...] = nv

    shp = jax.ShapeDtypeStruct((R, C), F32)
    return pl.pallas_call(
        body, name=name, out_shape=[shp, shp, shp, shp], compiler_params=_params(None),
    )(parts, w, m, v)


_ROW_NORM_FFN1, _ROW_NORM_MIX, _ROW_NORM_FFN2, _ROW_NORM_PLE, _ROW_NORM_FINAL = 0, 1, 2, 3, 4
_ROW_B_F, _ROW_REL, _ROW_LOSS, _SMALL_ROWS = 5, 6, 7, 8


def _pack_small(D, norm_ffn1, norm_mix, norm_ffn2, norm_ple, norm_final, b_f, rel_table):
    def row(v):
        v = v.reshape(1, -1)
        return jnp.pad(v, ((0, 0), (0, D - v.shape[1])))
    return jnp.concatenate([row(norm_ffn1), row(norm_mix), row(norm_ffn2), row(norm_ple),
                            row(norm_final), row(b_f), row(rel_table),
                            jnp.zeros((1, D), F32)], axis=0)


def _unpack_small(a, shapes):
    return {"norm_ffn1": a[_ROW_NORM_FFN1].reshape(shapes["norm_ffn1"]),
            "norm_mix": a[_ROW_NORM_MIX].reshape(shapes["norm_mix"]),
            "b_f": a[_ROW_B_F, :N_FOX].reshape(shapes["b_f"]),
            "norm_ffn2": a[_ROW_NORM_FFN2].reshape(shapes["norm_ffn2"]),
            "norm_ple": a[_ROW_NORM_PLE].reshape(shapes["norm_ple"]),
            "rel_table": a[_ROW_REL, :N_REL_BUCKETS * N_DIL].reshape(shapes["rel_table"]),
            "norm_final": a[_ROW_NORM_FINAL].reshape(shapes["norm_final"])}


def local_step(x, p, tgt, g_ffn1, g_mix, g_ffn2, g_ple, g_final, b_f, rel_table,
               forward, weights, emit, emit2, first_dep):
    T, D = x.shape
    P = p.shape[1]
    CW = D // N_DEV
    tq = _tile(T, 256)

    h1 = rms_fwd(x, g_ffn1, "rms_ffn1", dep=first_dep)
    tiles = bias_tiles(rel_table, T, tq)
    forward("ffn1_gu", [tiles, h1])
    wgu1, = weights("ffn1_gu", h1)
    a1, b1, s1 = ffn_up(h1, wgu1, 0, 1, 2, "ffn1_up")
    forward("ffn1_d", s1)
    wd1, = weights("ffn1_d", s1)
    x1 = ffn_down(s1, wd1, 0, 1, x, "ffn1_down")

    h2 = rms_fwd(x1, g_mix, "rms_mix")
    forward("mix", h2)
    w3, wf, wo = weights("mix", h2)
    qkv = mm_nt([(h2, w3)], "mix_qkv", tn=768, out_dtype=BF16)
    uf = mm_nt([(h2, wf)], "mix_forget", tn=LANES, out_dtype=F32)
    bfp = jnp.pad(b_f.reshape(1, N_FOX), ((0, 0), (0, LANES - N_FOX)))
    c, ct = fox_gate_fwd(uf, bfp, "fox_gate")
    cat, lse = attention_fwd(qkv, c, ct, tiles, "attention")
    x2 = mm_nn(cat, wo, "mix_out", tn=512, out_dtype=F32, res=x1)

    h3 = rms_fwd(x2, g_ffn2, "rms_ffn2")
    forward("ffn2", h3)
    wgu2, wd2 = weights("ffn2", h3)
    a2, b2, s2 = ffn_up(h3, wgu2, 0, 1, 2, "ffn2_up")
    forward("ple", s2)
    x3 = ffn_down(s2, wd2, 0, 1, x2, "ffn2_down")

    h4 = rms_fwd(x3, g_ple, "rms_ple")
    wpg, wpp = weights("ple", h4)
    z = mm_nn(h4, wpg, "ple_gate", tn=512, out_dtype=F32)
    pp = mm_nn(p, wpp, "ple_proj", tn=CW, tm=T, out_dtype=F32, n_out=D,
               b_block=(P, CW), b_map=lambda n, i: (n, 0))
    loss_row, dx4, dg_final, dz, dpp = ple_loss(x3, z, pp, g_final, tgt, "ple_loss")

    grads = {}
    grads["w_ple_proj"] = mm_tn(p, dpp, "ple_proj_dw", grid=(N_DEV,),
                                a_block=(T, P), a_map=lambda n: (0, 0),
                                b_block=(T, CW), b_map=lambda n: (0, n),
                                o_block=(P, CW), o_map=lambda n: (n, 0),
                                out_shape=(N_DEV * P, CW))
    grads["w_ple_gate"] = mm_tn_plain(h4, dz, "ple_gate_dw")
    tok = emit("ple", grads)
    dh4 = mm_nt([(dz, wpg)], "ple_gate_dh", tn=512, out_dtype=F32, dep=tok)
    tok = emit2("ple", dh4)
    dx3, dx3h, dg_ple = rms_bwd(dh4, x3, g_ple, dx4, "rms_ple_bwd", dep=tok)

    da2, db2 = ffn_bwd_act(dx3h, wd2, 0, 1, a2, b2, "ffn2_bwd_act")
    grads["ffn2_w_down"] = ffn_bwd_dw_down(s2, dx3h, "ffn2_down_dw")
    grads["ffn2_w_gate"] = ffn_bwd_dw_in(h3, da2, "ffn2_gate_dw")
    grads["ffn2_w_up"] = ffn_bwd_dw_in(h3, db2, "ffn2_up_dw")
    tok = emit("ffn2", grads)
    dh3 = ffn_bwd_dh(da2, db2, wgu2, 0, 1, 2, D, "ffn2_bwd_dh", dep=tok)
    tok = emit2("ffn2", dh3)
    dx2, _, dg_ffn2 = rms_bwd(dh3, x2, g_ffn2, dx3, "rms_ffn2_bwd", dep=tok, half=False)

    dcat = mm_nt([(dx2, wo)], "mix_out_dh", tn=512, out_dtype=BF16)
    grads["w_o"] = mm_tn_plain(cat, dx2, "mix_out_dw")
    dq, dk, dv, dct, dtiles = attention_bwd(qkv, c, ct, tiles, lse, cat, dcat, "attention_bwd")
    dctp = jnp.pad(dct[:, 0, :], ((0, LANES - N_HEADS), (0, 0)))
    duf, dbf = fox_gate_bwd(dctp, uf, bfp, "fox_gate_bwd")
    drel = rel_table_grad(dtiles, T, "rel_table_grad")[:, 0, :N_REL_BUCKETS].T
    du3 = jnp.concatenate([dq, dk, dv], axis=1)
    grads["w3"] = mm_tn_plain(du3, h2, "mix_qkv_dw", tm=768)
    grads["wf"] = mm_tn_plain(duf, h2, "mix_forget_dw", tm=LANES)
    tok = emit("mix", grads)
    dh2 = mm_nn_sum([(du3, w3), (duf, wf)], "mix_in_dh", tn=512, out_dtype=F32, dep=tok)
    tok = emit2("mix", dh2)
    dx1, dx1h, dg_mix = rms_bwd(dh2, x1, g_mix, dx2, "rms_mix_bwd", dep=tok)

    da1, db1 = ffn_bwd_act(dx1h, wd1, 0, 1, a1, b1, "ffn1_bwd_act")
    grads["ffn1_w_down"] = ffn_bwd_dw_down(s1, dx1h, "ffn1_down_dw")
    tok = emit("ffn1_d", grads)
    grads["ffn1_w_gate"] = ffn_bwd_dw_in(h1, da1, "ffn1_gate_dw", dep=tok)
    tok = emit2("ffn1_d", grads["ffn1_w_gate"])
    tok = emit("ffn1_g", grads, after=tok)
    grads["ffn1_w_up"] = ffn_bwd_dw_in(h1, db1, "ffn1_up_dw", dep=tok)
    tok = emit2("ffn1_g", grads["ffn1_w_up"])
    tok = emit("ffn1_u", grads, after=tok)
    dh1 = ffn_bwd_dh(da1, db1, wgu1, 0, 1, 2, D, "ffn1_bwd_dh", dep=tok)
    tok = emit2("ffn1_u", dh1)
    dx0, _, dg_ffn1 = rms_bwd(dh1, x, g_ffn1, dx1, "rms_ffn1_bwd", dep=tok, half=False)

    small = _pack_small(D, dg_ffn1, dg_mix, dg_ffn2, dg_ple, dg_final, dbf[:, :N_FOX], drel)
    small = small.at[_ROW_LOSS, :LANES].set(loss_row[0])
    grads["small"] = small
    emit("small", grads)
    return dx0


def _split_w_in(w_in_t):
    df, dd = N_FOX * HEAD_DIM, N_DIL * HEAD_DIM
    o = np.cumsum([0, df, df, df, N_FOX, dd, dd, dd]).tolist()
    qa, ka, va, f, qb, kb, vb = [w_in_t[o[i]:o[i + 1]] for i in range(7)]
    return jnp.concatenate([qa, qb, ka, kb, va, vb], axis=0), f


def _join_w_in(d3, dfg):
    df, dd = N_FOX * HEAD_DIM, N_DIL * HEAD_DIM
    o = np.cumsum([0, df, dd, df, dd, df, dd]).tolist()
    qa, qb, ka, kb, va, vb = [d3[o[i]:o[i + 1]] for i in range(6)]
    return jnp.concatenate([qa, ka, va, dfg, qb, kb, vb], axis=0)


def rows_to_bf16(a3, name, dep=None):
    R, _, C = a3.shape
    tc = _tile(C, 512)

    def body(a_ref, *rest):
        rest[-1][...] = a_ref[...].astype(BF16)

    in_specs = [pl.BlockSpec((R, None, tc), lambda n: (0, 0, n))]
    args = [a3]
    if dep is not None:
        in_specs.append(_dep_spec(1))
        args.append(dep)
    return pl.pallas_call(
        body, name=name, grid=(C // tc,), in_specs=in_specs,
        out_specs=pl.BlockSpec((R, tc), lambda n: (0, n)),
        out_shape=jax.ShapeDtypeStruct((R, C), BF16),
        compiler_params=_params(("parallel",)),
    )(*args)


def adamw_rows3d(parts, sums, my_chip, w3, m3, v3, name):
    R, _, C = w3.shape
    S = parts.shape[0]
    tc = _tile(C, 256)

    def body(mc_ref, p_ref, o_ref, w_ref, m_ref, v_ref, g_ref, d_ref, nm_ref, nv_ref):
        g = _sum_partials(p_ref, o_ref, mc_ref[0])
        delta, nm, nv = _adamw_math(w_ref[...], g, m_ref[...], v_ref[...])
        g_ref[...] = g
        d_ref[...] = delta
        nm_ref[...] = nm
        nv_ref[...] = nv

    col = pl.BlockSpec((R, None, tc), lambda n, mc: (0, 0, n))
    shp = jax.ShapeDtypeStruct((R, 1, C), F32)
    grid_spec = pltpu.PrefetchScalarGridSpec(
        num_scalar_prefetch=1, grid=(C // tc,),
        in_specs=[pl.BlockSpec((S, R, tc), lambda n, mc: (0, 0, n)),
                  pl.BlockSpec((None, R, tc), lambda n, mc: (mc[0], 0, n)), col, col, col],
        out_specs=[col, col, col, col])
    return pl.pallas_call(
        body, name=name, grid_spec=grid_spec, out_shape=[shp, shp, shp, shp],
        compiler_params=_params(("parallel",)),
    )(my_chip.reshape(1).astype(jnp.int32), parts, sums, w3, m3, v3)


def kernel(x, p, norm_ffn1, ffn1_w_gate, ffn1_w_up, ffn1_w_down, norm_mix, w_in, b_f, w_o, norm_ffn2, ffn2_w_gate, ffn2_w_up, ffn2_w_down, norm_ple, w_ple_gate, w_ple_proj, rel_table, norm_final, loss_target, m_norm_ffn1, m_ffn1_w_gate, m_ffn1_w_up, m_ffn1_w_down, m_norm_mix, m_w_in, m_b_f, m_w_o, m_norm_ffn2, m_ffn2_w_gate, m_ffn2_w_up, m_ffn2_w_down, m_norm_ple, m_w_ple_gate, m_w_ple_proj, m_rel_table, m_norm_final, v_norm_ffn1, v_ffn1_w_gate, v_ffn1_w_up, v_ffn1_w_down, v_norm_mix, v_w_in, v_b_f, v_w_o, v_norm_ffn2, v_ffn2_w_gate, v_ffn2_w_up, v_ffn2_w_down, v_norm_ple, v_w_ple_gate, v_w_ple_proj, v_rel_table, v_norm_final):
    names = ["norm_ffn1", "ffn1_w_gate", "ffn1_w_up", "ffn1_w_down", "norm_mix", "w_in", "b_f", "w_o",
             "norm_ffn2", "ffn2_w_gate", "ffn2_w_up", "ffn2_w_down", "norm_ple", "w_ple_gate",
             "w_ple_proj", "rel_table", "norm_final"]
    w = dict(zip(names, [norm_ffn1, ffn1_w_gate, ffn1_w_up, ffn1_w_down, norm_mix, w_in, b_f, w_o,
                         norm_ffn2, ffn2_w_gate, ffn2_w_up, ffn2_w_down, norm_ple, w_ple_gate,
                         w_ple_proj, rel_table, norm_final]))
    m = dict(zip(names, [m_norm_ffn1, m_ffn1_w_gate, m_ffn1_w_up, m_ffn1_w_down, m_norm_mix, m_w_in,
                         m_b_f, m_w_o, m_norm_ffn2, m_ffn2_w_gate, m_ffn2_w_up, m_ffn2_w_down,
                         m_norm_ple, m_w_ple_gate, m_w_ple_proj, m_rel_table, m_norm_final]))
    v = dict(zip(names, [v_norm_ffn1, v_ffn1_w_gate, v_ffn1_w_up, v_ffn1_w_down, v_norm_mix, v_w_in,
                         v_b_f, v_w_o, v_norm_ffn2, v_ffn2_w_gate, v_ffn2_w_up, v_ffn2_w_down,
                         v_norm_ple, v_w_ple_gate, v_w_ple_proj, v_rel_table, v_norm_final]))
    sharded = ["ffn1_w_gate", "ffn1_w_up", "ffn1_w_down", "w_in", "w_o", "ffn2_w_gate", "ffn2_w_up",
               "ffn2_w_down", "w_ple_gate", "w_ple_proj"]
    small_names = [n for n in names if n not in sharded]

    xs, ps, tgt = x[0], p[0, 0], loss_target[0]
    T, D = xs.shape
    transposed = ("ffn1_w_gate", "ffn1_w_up", "ffn2_w_gate", "ffn2_w_up")

    def view(t, n):
        if n in transposed:
            return t[n][0].T
        if n == "w_in":
            return jnp.transpose(t[n], (2, 0, 1))
        return t[n][0]

    def unview(a, n):
        if n in transposed:
            return a.T.reshape(w[n].shape)
        if n == "w_in":
            return jnp.transpose(a, (1, 2, 0))
        return a.reshape(w[n].shape)

    sh = {n: view(w, n) for n in sharded}
    m_sh = {n: view(m, n) for n in sharded}
    v_sh = {n: view(v, n) for n in sharded}
    F8 = sh["ffn1_w_down"].shape[0]
    WIN8 = sh["w_in"].shape[0]
    me = 4 * lax.axis_index("x") + 2 * lax.axis_index("y") + lax.axis_index("c")

    def start(groups, name, after=None):
        srcs = [s for grp in groups for s in grp]
        sems, bufs, token = gather_start(srcs, [_landing(s, me) for s in srcs], name, after=after)
        return sems, bufs[:len(srcs)], bufs[len(srcs):], token

    cat0 = lambda ns, z: (jnp.concatenate([sh[n] for n in ns], axis=0) + z).astype(BF16)
    sems_a, srcs_a, lands_a, token_a = start(
        [[cat0(["ffn1_w_gate", "ffn1_w_up"], 0.0)], [sh["ffn1_w_down"].astype(BF16)]], "gather_start_ffn1")
    zero = token_a[0, 0]
    w_in_bf = rows_to_bf16(sh["w_in"], "w_in_bf16", dep=token_a)
    sems_b, srcs_b, lands_b, g_token = start(
        [[w_in_bf, (sh["w_o"] + zero).astype(BF16)],
         [cat0(["ffn2_w_gate", "ffn2_w_up"], zero), (sh["ffn2_w_down"] + zero).astype(BF16)],
         [(sh["w_ple_gate"] + zero).astype(BF16), (sh["w_ple_proj"] + zero).astype(BF16)]],
        "gather_start_rest", after=token_a)
    order = ["ffn1_gu", "ffn1_d", "mix", "ffn2", "ple"]
    group_sizes = [1, 1, 2, 2, 2]
    g_sems, g_srcs, g_lands = sems_a + sems_b, srcs_a + srcs_b, lands_a + lands_b
    g_send, g_recv_d, g_recv_i = g_sems[0::3], g_sems[1::3], g_sems[2::3]
    first = np.cumsum([0] + group_sizes).tolist()
    passed = {}

    def arrays_of(group):
        k = order.index(group)
        return slice(first[k], first[k + 1])

    def forward(group, after):
        sl = arrays_of(group)
        f_sems, bufs = gather_forward(g_srcs[sl], g_lands[sl], g_recv_i[sl], after, "gather_forward_" + group)
        k = len(bufs) // 2
        passed[group] = (f_sems[0::2], f_sems[1::2], bufs[:k], bufs[k:])

    def weights(group, after):
        sl = arrays_of(group)
        f_send, f_recv, srcs, lands = passed[group]
        got = gather_wait(srcs, lands, g_send[sl], g_recv_d[sl], f_send, f_recv, after, "gather_wait_" + group)
        if group == "ffn1_gu":
            return (got[0].reshape(N_DEV * 2 * F8, D),)
        if group == "ffn1_d":
            return (got[0].reshape(N_DEV * F8, D),)
        a0, a1 = got
        if group == "ffn2":
            return a0.reshape(N_DEV * 2 * F8, D), a1.reshape(N_DEV * F8, D)
        if group == "ple":
            return a0.reshape(-1, D), a1.reshape(-1, a1.shape[2])
        w3, wf8 = _split_w_in(a0.reshape(N_DEV * WIN8, D))
        return w3, jnp.pad(wf8, ((0, LANES - N_FOX), (0, 0))), a1.reshape(-1, D)

    scatter_groups = {
        "ple": ["w_ple_gate", "w_ple_proj"],
        "ffn2": ["ffn2_w_gate", "ffn2_w_up", "ffn2_w_down"],
        "mix": ["w_in", "w_o"],
        "ffn1_d": ["ffn1_w_down"],
        "ffn1_g": ["ffn1_w_gate"],
        "ffn1_u": ["ffn1_w_up"],
    }
    x_i, y_i, c_i = _xyc()
    my_chip = 2 * x_i + y_i
    pair_stage, chip_stage, small_stage = {}, {}, {}

    def emit(group, grads, after=None):
        if group == "small":
            src = grads["small"]
            ss, rs, srcs, lands, token = exchange_start([src], [_landing(src, me)], [False], "scatter_start_small")
            small_stage["small"] = (ss, rs, srcs, lands)
            return token
        src4s = []
        for n in scatter_groups[group]:
            if n == "w_in":
                full = _join_w_in(grads["w3"], grads["wf"][:N_FOX])
                src4s.append(full.reshape(N_CHIPS, 2, WIN8, D))
            else:
                src4s.append(grads[n].reshape((N_CHIPS, 2) + sh[n].shape))
        lands = [lax.empty((N_CHIPS,) + s.shape[2:], BF16) for s in src4s]
        sems, bufs, token = scatter_pair_start(src4s, lands, "scatter_pair_start_" + group, after=after)
        k = len(src4s)
        pair_stage[group] = (sems, bufs[:k], bufs[k:])
        return token

    def emit2(group, after):
        sems, src4s, lands = pair_stage[group]
        src4s, lands = scatter_pair_wait(src4s, lands, sems, after, "scatter_pair_wait_" + group)
        sums = [chip_sum(s4, la, c_i, "chip_sum_" + n)
                for s4, la, n in zip(src4s, lands, scatter_groups[group])]
        chip_lands = [lax.empty(s.shape, s.dtype) for s in sums]
        sems, bufs, token = scatter_chip_start(sums, chip_lands, "scatter_chip_start_" + group)
        k = len(sums)
        chip_stage[group] = (sems, bufs[:k], bufs[k:])
        return token

    dx0 = local_step(
        xs, ps, tgt, w["norm_ffn1"], w["norm_mix"], w["norm_ffn2"], w["norm_ple"],
        w["norm_final"].reshape(1, D), w["b_f"], w["rel_table"], forward, weights, emit, emit2, g_token)

    res = {}
    after = dx0
    for group in ["ple", "ffn2", "mix", "ffn1_d", "ffn1_g", "ffn1_u"]:
        sems, sums, chip_lands = chip_stage[group]
        sums, parts = scatter_chip_wait(sums, chip_lands, sems, after, "scatter_chip_wait_" + group)
        for n, part, own in zip(scatter_groups[group], parts, sums):
            update = adamw_rows3d if n == "w_in" else adamw_sharded
            g, d, nm, nv = update(part, own, my_chip, sh[n], m_sh[n], v_sh[n], "adamw_" + n)
            res[n] = tuple(unview(a, n) for a in (g, d, nm, nv))
            after = g
    ss, rs, srcs, lands = small_stage["small"]
    small_parts, = exchange_wait(ss, rs, srcs, lands, [False], after, "scatter_wait_small")
    pack = lambda t: _pack_small(D, t["norm_ffn1"], t["norm_mix"], t["norm_ffn2"], t["norm_ple"],
                                 t["norm_final"], t["b_f"], t["rel_table"])
    gs, ds, ms, vs = adamw_small(small_parts, pack(w), pack(m), pack(v), "adamw_small")
    shapes = {n: w[n].shape for n in small_names}
    unpacked = [_unpack_small(a, shapes) for a in (gs, ds, ms, vs)]
    for n in small_names:
        res[n] = tuple(u[n] for u in unpacked)
    loss = gs[_ROW_LOSS, 0]

    out = [loss, dx0.reshape(x.shape)]
    for k in range(4):
        out += [res[n][k] for n in names]
    return tuple(out)
```

```python
import functools
import math

import numpy as np
import jax
import jax.numpy as jnp
from jax import lax
from jax.experimental import pallas as pl
from jax.experimental.pallas import tpu as pltpu

F32 = jnp.float32
BF16 = jnp.bfloat16

N_DEV = 8
HEAD_DIM = 128
N_FOX = 8
N_DIL = 8
N_HEADS = N_FOX + N_DIL
DILATED_PATTERNS = ((128, 1), (512, 4), (2048, 16))
N_REL_BUCKETS = 32
REL_MAX_DISTANCE = 2048
RMS_EPS = 1e-6
NEG_INF = -1e30
LANES = 128
VMEM_LIMIT = 56 * 1024 * 1024

ADAM_LR = 0.001
ADAM_B1 = 0.9
ADAM_B2 = 0.999
ADAM_EPS = 1e-08
ADAM_WD = 0.01
ADAM_STEP = 10

MESH = pl.DeviceIdType.MESH


def _params(sem):
    return pltpu.CompilerParams(dimension_semantics=sem, vmem_limit_bytes=VMEM_LIMIT)


def _dot(a, b, ca, cb, precision=None):
    return lax.dot_general(a, b, (((ca,), (cb,)), ((), ())),
                           preferred_element_type=F32, precision=precision)


def _sigmoid(z):
    return 1.0 / (1.0 + jnp.exp(-z))


def _tile(n, want):
    t = min(n, want)
    assert n % t == 0, (n, t)
    return t


def _dep_spec(ngrid):
    return pl.BlockSpec((8, LANES), lambda *_: (0, 0))


def rms_fwd(x, g, name, dep=None):
    T, D = x.shape
    tm = _tile(T, 256)

    def body(x_ref, g_ref, *rest):
        h_ref = rest[-1]
        xv = x_ref[...]
        r = lax.rsqrt(jnp.mean(xv * xv, axis=-1, keepdims=True) + RMS_EPS)
        h_ref[...] = (xv * r * g_ref[...]).astype(BF16)

    in_specs = [pl.BlockSpec((tm, D), lambda i: (i, 0)), pl.BlockSpec((1, D), lambda i: (0, 0))]
    args = [x, g]
    if dep is not None:
        in_specs.append(_dep_spec(1))
        args.append(dep)
    return pl.pallas_call(
        body, name=name, grid=(T // tm,), in_specs=in_specs,
        out_specs=pl.BlockSpec((tm, D), lambda i: (i, 0)),
        out_shape=jax.ShapeDtypeStruct((T, D), BF16),
        compiler_params=_params(("parallel",)),
    )(*args)


def rms_bwd(dh, x, g, dres, name, dep=None, half=True):
    T, D = x.shape
    tm = _tile(T, 256)

    def body(dh_ref, x_ref, g_ref, dres_ref, *rest):
        dx_ref, dg_ref = (rest[-3], rest[-1]) if half else (rest[-2], rest[-1])
        i = pl.program_id(0)
        xv = x_ref[...]
        r = lax.rsqrt(jnp.mean(xv * xv, axis=-1, keepdims=True) + RMS_EPS)
        xh = xv * r
        d = dh_ref[...]
        u = d * g_ref[...]
        dx = dres_ref[...] + r * (u - xh * jnp.mean(u * xh, axis=-1, keepdims=True))
        dx_ref[...] = dx
        if half:
            rest[-2][...] = (0.5 * dx).astype(BF16)
        part = jnp.sum(d * xh, axis=0, keepdims=True)

        @pl.when(i == 0)
        def _():
            dg_ref[...] = part

        @pl.when(i > 0)
        def _():
            dg_ref[...] += part

    row = pl.BlockSpec((tm, D), lambda i: (i, 0))
    vec = pl.BlockSpec((1, D), lambda i: (0, 0))
    in_specs = [row, row, vec, row]
    args = [dh, x, g, dres]
    if dep is not None:
        in_specs.append(_dep_spec(1))
        args.append(dep)
    out_specs = [row, row, vec] if half else [row, vec]
    out_shape = [jax.ShapeDtypeStruct((T, D), F32)] + ([jax.ShapeDtypeStruct((T, D), BF16)] if half else [])
    out_shape.append(jax.ShapeDtypeStruct((1, D), F32))
    outs = pl.pallas_call(
        body, name=name, grid=(T // tm,),
        in_specs=in_specs, out_specs=out_specs, out_shape=out_shape,
        compiler_params=_params(("arbitrary",)),
    )(*args)
    return tuple(outs) if half else (outs[0], None, outs[1])


def ple_loss(x, z, pp, g, target, name):
    T, D = x.shape
    tm = _tile(T, 256)

    def body(x_ref, z_ref, p_ref, g_ref, t_ref, loss_ref, dx_ref, dg_ref, dz_ref, dp_ref):
        i = pl.program_id(0)
        gate = _sigmoid(z_ref[...])
        ppv = p_ref[...]
        xv = x_ref[...] + gate * ppv
        gv = g_ref[...]
        r = lax.rsqrt(jnp.mean(xv * xv, axis=-1, keepdims=True) + RMS_EPS)
        xh = xv * r
        e = xh * gv - t_ref[...]
        lpart = 0.5 * jnp.sum(jnp.mean(e * e, axis=-1, keepdims=True), axis=0, keepdims=True)
        lrow = jnp.broadcast_to(lpart, (1, LANES))
        d = e * (1.0 / D)
        u = d * gv
        dx = r * (u - xh * jnp.mean(u * xh, axis=-1, keepdims=True))
        dx_ref[...] = dx
        dz_ref[...] = (dx * ppv * gate * (1.0 - gate)).astype(BF16)
        dp_ref[...] = (dx * gate).astype(BF16)
        part = jnp.sum(d * xh, axis=0, keepdims=True)

        @pl.when(i == 0)
        def _():
            dg_ref[...] = part
            loss_ref[...] = lrow

        @pl.when(i > 0)
        def _():
            dg_ref[...] += part
            loss_ref[...] += lrow

    row = pl.BlockSpec((tm, D), lambda i: (i, 0))
    vec = pl.BlockSpec((1, D), lambda i: (0, 0))
    return pl.pallas_call(
        body, name=name, grid=(T // tm,),
        in_specs=[row, row, row, vec, row],
        out_specs=[pl.BlockSpec((1, LANES), lambda i: (0, 0)), row, vec, row, row],
        out_shape=[jax.ShapeDtypeStruct((1, LANES), F32), jax.ShapeDtypeStruct((T, D), F32),
                   jax.ShapeDtypeStruct((1, D), F32), jax.ShapeDtypeStruct((T, D), BF16),
                   jax.ShapeDtypeStruct((T, D), BF16)],
        compiler_params=_params(("arbitrary",)),
    )(x, z, pp, g, target)


def _bf(v, scale=None):
    if scale is not None:
        v = v * scale
    return v.astype(BF16)


def mm_nn(a, b, name, *, tn, out_dtype, tm=512, n_out=None, b_block=None, b_map=None,
          res=None):
    T, K = a.shape
    N = n_out if n_out is not None else b.shape[1]
    tm = _tile(T, tm)
    tn = _tile(N, tn)
    b_block = b_block or (K, tn)
    b_map = b_map or (lambda n, i: (0, n))

    def body(*refs):
        a_ref, b_ref = refs[0], refs[1]
        o_ref = refs[-1]
        acc = _dot(_bf(a_ref[...]), _bf(b_ref[...]), 1, 0)
        if res is not None:
            acc = refs[2][...] + acc
        o_ref[...] = acc.astype(out_dtype)

    in_specs = [pl.BlockSpec((tm, K), lambda n, i: (i, 0)), pl.BlockSpec(b_block, b_map)]
    args = [a, b]
    if res is not None:
        in_specs.append(pl.BlockSpec((tm, tn), lambda n, i: (i, n)))
        args.append(res)
    return pl.pallas_call(
        body, name=name, grid=(N // tn, T // tm), in_specs=in_specs,
        out_specs=pl.BlockSpec((tm, tn), lambda n, i: (i, n)),
        out_shape=jax.ShapeDtypeStruct((T, N), out_dtype),
        compiler_params=_params(("parallel", "parallel")),
    )(*args)


def mm_nn_sum(pairs, name, *, tn, out_dtype, tm=512, dep=None):
    T = pairs[0][0].shape[0]
    N = pairs[0][1].shape[1]
    tm = _tile(T, tm)
    tn = _tile(N, tn)
    npair = len(pairs)

    def body(*refs):
        acc = None
        for q in range(npair):
            part = _dot(_bf(refs[2 * q][...]), _bf(refs[2 * q + 1][...]), 1, 0)
            acc = part if acc is None else acc + part
        refs[-1][...] = acc.astype(out_dtype)

    in_specs, args = [], []
    for a, b in pairs:
        K = a.shape[1]
        in_specs += [pl.BlockSpec((tm, K), lambda n, i: (i, 0)), pl.BlockSpec((K, tn), lambda n, i: (0, n))]
        args += [a, b]
    if dep is not None:
        in_specs.append(_dep_spec(2))
        args.append(dep)
    return pl.pallas_call(
        body, name=name, grid=(N // tn, T // tm), in_specs=in_specs,
        out_specs=pl.BlockSpec((tm, tn), lambda n, i: (i, n)),
        out_shape=jax.ShapeDtypeStruct((T, N), out_dtype),
        compiler_params=_params(("parallel", "parallel")),
    )(*args)


def mm_nt(pairs, name, *, tn, out_dtype, tm=512, dep=None):
    T = pairs[0][0].shape[0]
    N = pairs[0][1].shape[0]
    tm = _tile(T, tm)
    tn = _tile(N, tn)
    npair = len(pairs)

    def body(*refs):
        o_ref = refs[-1]
        acc = None
        for q in range(npair):
            part = _dot(_bf(refs[2 * q][...]), _bf(refs[2 * q + 1][...]), 1, 1)
            acc = part if acc is None else acc + part
        o_ref[...] = acc.astype(out_dtype)

    in_specs, args = [], []
    for a, b in pairs:
        K = a.shape[1]
        in_specs += [pl.BlockSpec((tm, K), lambda n, i: (i, 0)), pl.BlockSpec((tn, K), lambda n, i: (n, 0))]
        args += [a, b]
    if dep is not None:
        in_specs.append(_dep_spec(2))
        args.append(dep)
    return pl.pallas_call(
        body, name=name, grid=(N // tn, T // tm), in_specs=in_specs,
        out_specs=pl.BlockSpec((tm, tn), lambda n, i: (i, n)),
        out_shape=jax.ShapeDtypeStruct((T, N), out_dtype),
        compiler_params=_params(("parallel", "parallel")),
    )(*args)


def mm_tn(a, b, name, *, grid, a_block, a_map, b_block, b_map, o_block, o_map, out_shape,
          b_scale=None, dep=None):
    def body(a_ref, b_ref, *rest):
        rest[-1][...] = _dot(_bf(a_ref[...]), _bf(b_ref[...], b_scale), 0, 0).astype(BF16)

    in_specs = [pl.BlockSpec(a_block, a_map), pl.BlockSpec(b_block, b_map)]
    args = [a, b]
    if dep is not None:
        in_specs.append(_dep_spec(len(grid)))
        args.append(dep)
    return pl.pallas_call(
        body, name=name, grid=grid, in_specs=in_specs,
        out_specs=pl.BlockSpec(o_block, o_map),
        out_shape=jax.ShapeDtypeStruct(out_shape, BF16),
        compiler_params=_params(("parallel",) * len(grid)),
    )(*args)


def mm_tn_plain(a, b, name, *, tm=512, tn=512, b_scale=None):
    T, M = a.shape
    N = b.shape[1]
    tm = _tile(M, tm)
    tn = _tile(N, tn)
    return mm_tn(a, b, name, grid=(M // tm, N // tn),
                 a_block=(T, tm), a_map=lambda m, n: (0, m),
                 b_block=(T, tn), b_map=lambda m, n: (0, n),
                 o_block=(tm, tn), o_map=lambda m, n: (m, n),
                 out_shape=(M, N), b_scale=b_scale)


def ffn_up(h, wgu, gi, ui, nper, name):
    T, D = h.shape
    F8 = wgu.shape[0] // (N_DEV * nper)
    tm = _tile(T, 512)
    nt = T // tm

    def body(h_ref, wg_ref, wu_ref, ga_ref, gb_ref, s_ref):
        hv = h_ref[...]
        a = _dot(hv, wg_ref[...], 1, 1)
        b = _dot(hv, wu_ref[...], 1, 1)
        sg = _sigmoid(a)
        silu = a * sg
        ga_ref[...] = (b * (sg * (1.0 + a * (1.0 - sg)))).astype(BF16)
        gb_ref[...] = silu.astype(BF16)
        s_ref[...] = (silu * b).astype(BF16)

    blk = pl.BlockSpec((tm, F8), lambda j, i: (j * nt + i, 0))
    shp = jax.ShapeDtypeStruct((N_DEV * T, F8), BF16)
    return pl.pallas_call(
        body, name=name, grid=(N_DEV, nt),
        in_specs=[pl.BlockSpec((tm, D), lambda j, i: (i, 0)),
                  pl.BlockSpec((F8, D), lambda j, i: (j * nper + gi, 0)),
                  pl.BlockSpec((F8, D), lambda j, i: (j * nper + ui, 0))],
        out_specs=[blk, blk, blk], out_shape=[shp, shp, shp],
        compiler_params=_params(("parallel", "parallel")),
    )(h, wgu, wgu)


def ffn_gate(h, wg, name):
    T, D = h.shape
    F8 = wg.shape[0] // N_DEV
    tm = _tile(T, 512)
    nt = T // tm

    def body(h_ref, wg_ref, a_ref):
        a_ref[...] = _dot(h_ref[...], wg_ref[...], 1, 1).astype(BF16)

    return pl.pallas_call(
        body, name=name, grid=(N_DEV, nt),
        in_specs=[pl.BlockSpec((tm, D), lambda j, i: (i, 0)), pl.BlockSpec((F8, D), lambda j, i: (j, 0))],
        out_specs=pl.BlockSpec((tm, F8), lambda j, i: (j * nt + i, 0)),
        out_shape=jax.ShapeDtypeStruct((N_DEV * T, F8), BF16),
        compiler_params=_params(("parallel", "parallel")),
    )(h, wg)


def ffn_up_gated(h, wu, a, name):
    T, D = h.shape
    F8 = wu.shape[0] // N_DEV
    tm = _tile(T, 512)
    nt = T // tm

    def body(h_ref, wu_ref, a_ref, ga_ref, gb_ref, s_ref):
        av = a_ref[...].astype(F32)
        b = _dot(h_ref[...], wu_ref[...], 1, 1)
        sg = _sigmoid(av)
        silu = av * sg
        ga_ref[...] = (b * (sg * (1.0 + av * (1.0 - sg)))).astype(BF16)
        gb_ref[...] = silu.astype(BF16)
        s_ref[...] = (silu * b).astype(BF16)

    blk = pl.BlockSpec((tm, F8), lambda j, i: (j * nt + i, 0))
    shp = jax.ShapeDtypeStruct((N_DEV * T, F8), BF16)
    return pl.pallas_call(
        body, name=name, grid=(N_DEV, nt),
        in_specs=[pl.BlockSpec((tm, D), lambda j, i: (i, 0)), pl.BlockSpec((F8, D), lambda j, i: (j, 0)), blk],
        out_specs=[blk, blk, blk], out_shape=[shp, shp, shp],
        compiler_params=_params(("parallel", "parallel")),
    )(h, wu, a)


def ffn_down(s, wd, di, nper, x, name):
    T, D = x.shape
    F8 = s.shape[1]
    tm = _tile(T, 512)
    nt = T // tm

    def body(s_ref, w_ref, x_ref, o_ref, acc_ref):
        j = pl.program_id(1)
        part = _dot(s_ref[...], w_ref[...], 1, 0)

        @pl.when(j == 0)
        def _():
            acc_ref[...] = part

        @pl.when(j > 0)
        def _():
            acc_ref[...] += part

        @pl.when(j == N_DEV - 1)
        def _():
            o_ref[...] = x_ref[...] + 0.5 * acc_ref[...]

    return pl.pallas_call(
        body, name=name, grid=(nt, N_DEV),
        in_specs=[pl.BlockSpec((tm, F8), lambda i, j: (j * nt + i, 0)),
                  pl.BlockSpec((F8, D), lambda i, j: (j * nper + di, 0)),
                  pl.BlockSpec((tm, D), lambda i, j: (i, 0))],
        out_specs=pl.BlockSpec((tm, D), lambda i, j: (i, 0)),
        out_shape=jax.ShapeDtypeStruct((T, D), F32),
        scratch_shapes=[pltpu.VMEM((tm, D), F32)],
        compiler_params=_params(("parallel", "arbitrary")),
    )(s, wd, x)


def ffn_bwd_act(dxh, wd, di, nper_d, a, b, name, dep=None):
    T, D = dxh.shape
    F8 = a.shape[1]
    tm = _tile(T, 512)
    nt = T // tm

    def body(dx_ref, w_ref, a_ref, b_ref, *rest):
        da_ref, db_ref = rest[-2], rest[-1]
        ds = _dot(dx_ref[...], w_ref[...], 1, 1)
        da_ref[...] = (ds * a_ref[...].astype(F32)).astype(BF16)
        db_ref[...] = (ds * b_ref[...].astype(F32)).astype(BF16)

    blk = pl.BlockSpec((tm, F8), lambda j, i: (j * nt + i, 0))
    shp = jax.ShapeDtypeStruct((N_DEV * T, F8), BF16)
    in_specs = [pl.BlockSpec((tm, D), lambda j, i: (i, 0)),
                pl.BlockSpec((F8, D), lambda j, i: (j * nper_d + di, 0)), blk, blk]
    args = [dxh, wd, a, b]
    if dep is not None:
        in_specs.append(_dep_spec(2))
        args.append(dep)
    return pl.pallas_call(
        body, name=name, grid=(N_DEV, nt), in_specs=in_specs,
        out_specs=[blk, blk], out_shape=[shp, shp],
        compiler_params=_params(("parallel", "parallel")),
    )(*args)


def ffn_bwd_dh(da, db, wg, wu, gi, ui, nper, D, name, dep=None):
    F8 = da.shape[1]
    T = da.shape[0] // N_DEV
    tm = _tile(T, 512)
    nt = T // tm

    def body(da_ref, db_ref, wg_ref, wu_ref, *rest):
        o_ref, acc_ref = rest[-2], rest[-1]
        j = pl.program_id(1)
        part = _dot(da_ref[...], wg_ref[...], 1, 0) + _dot(db_ref[...], wu_ref[...], 1, 0)

        @pl.when(j == 0)
        def _():
            acc_ref[...] = part

        @pl.when(j > 0)
        def _():
            acc_ref[...] += part

        @pl.when(j == N_DEV - 1)
        def _():
            o_ref[...] = acc_ref[...]

    blk = pl.BlockSpec((tm, F8), lambda i, j: (j * nt + i, 0))
    in_specs = [blk, blk,
                pl.BlockSpec((F8, D), lambda i, j: (j * nper + gi, 0)),
                pl.BlockSpec((F8, D), lambda i, j: (j * nper + ui, 0))]
    args = [da, db, wg, wu]
    if dep is not None:
        in_specs.append(_dep_spec(2))
        args.append(dep)
    return pl.pallas_call(
        body, name=name, grid=(nt, N_DEV), in_specs=in_specs,
        out_specs=pl.BlockSpec((tm, D), lambda i, j: (i, 0)),
        out_shape=jax.ShapeDtypeStruct((T, D), F32),
        scratch_shapes=[pltpu.VMEM((tm, D), F32)],
        compiler_params=_params(("parallel", "arbitrary")),
    )(*args)


def ffn_bwd_dw_in(h, dact, name, dep=None):
    T, D = h.shape
    F8 = dact.shape[1]
    tm = _tile(D, 512)
    return mm_tn(dact, h, name, grid=(N_DEV, D // tm),
                 a_block=(T, F8), a_map=lambda j, m: (j, 0),
                 b_block=(T, tm), b_map=lambda j, m: (0, m),
                 o_block=(F8, tm), o_map=lambda j, m: (j, m),
                 out_shape=(N_DEV * F8, D), dep=dep)


def ffn_bwd_dw_down(s, dx, name):
    T, D = dx.shape
    F8 = s.shape[1]
    tn = _tile(D, 512)
    return mm_tn(s, dx, name, grid=(N_DEV, D // tn),
                 a_block=(T, F8), a_map=lambda j, n: (j, 0),
                 b_block=(T, tn), b_map=lambda j, n: (0, n),
                 o_block=(F8, tn), o_map=lambda j, n: (j, n),
                 out_shape=(N_DEV * F8, D))


def _t5_bucket_np(dist):
    max_exact = N_REL_BUCKETS // 2
    d = np.maximum(dist, 1).astype(np.float64)
    large = max_exact + (np.log(d / max_exact) / math.log(REL_MAX_DISTANCE / max_exact)
                         * (N_REL_BUCKETS - max_exact)).astype(np.int64)
    large32 = max_exact + (np.log(d.astype(np.float32) / np.float32(max_exact))
                           / np.float32(math.log(REL_MAX_DISTANCE / max_exact))
                           * np.float32(N_REL_BUCKETS - max_exact)).astype(np.int64)
    assert np.array_equal(large, large32)
    large = np.minimum(large, N_REL_BUCKETS - 1)
    return np.where(dist < max_exact, dist, large)


def _distance_tables(T, tq):
    dist = np.arange(T)
    mult = np.zeros(T, np.int64)
    for window, dilation in DILATED_PATTERNS:
        mult += ((dist % dilation == 0) & (dist // dilation <= window // dilation)).astype(np.int64)
    logm = np.where(mult > 0, np.log(np.maximum(mult, 1)), NEG_INF).astype(np.float32)
    bucket = _t5_bucket_np(dist).astype(np.int32)
    nkb = T // tq
    k = np.arange(nkb)[:, None, None]
    r = np.arange(tq)[None, :, None]
    c = np.arange(tq)[None, None, :]
    delta = k * tq + r - c
    return bucket, logm, delta


def _tile_buckets(T, tq):
    bucket, logm, delta = _distance_tables(T, tq)
    safe = np.maximum(delta, 0)
    bidx = np.where(delta >= 0, bucket[safe], -1).astype(np.int32)
    logm_t = np.where(delta >= 0, logm[safe], NEG_INF).astype(np.float32)
    present = [sorted(set(np.unique(bidx[k]).tolist()) - {-1}) for k in range(T // tq)]
    return bidx, logm_t, present


def bias_tiles(rel_table, T, tq):
    bidx, logm_t, present = _tile_buckets(T, tq)
    nkb = T // tq

    def body(tab_ref, b_ref, lm_ref, o_ref):
        slot = pl.program_id(0)

        @pl.when(slot == 0)
        def _():
            o_ref[...] = jnp.where(b_ref[...] >= 0, 0.0, NEG_INF)

        @pl.when(slot > 0)
        def _():
            for k in range(nkb):
                bi = b_ref[k]
                acc = lm_ref[k]
                for b in present[k]:
                    acc = acc + jnp.where(bi == b, tab_ref[b, slot - 1], 0.0)
                o_ref[k] = acc

    full = pl.BlockSpec((nkb, tq, tq), lambda s: (0, 0, 0))
    return pl.pallas_call(
        body, name="bias_tiles", grid=(1 + N_DIL,),
        in_specs=[pl.BlockSpec(memory_space=pltpu.SMEM), full, full],
        out_specs=pl.BlockSpec((None, nkb, tq, tq), lambda s: (s, 0, 0, 0)),
        out_shape=jax.ShapeDtypeStruct((1 + N_DIL, nkb, tq, tq), F32),
        compiler_params=_params(("parallel",)),
    )(rel_table, jnp.asarray(bidx), jnp.asarray(logm_t))


def fox_gate_fwd(uf, bf, name):
    T = uf.shape[0]
    tb = _tile(T, 512)

    def body(u_ref, b_ref, c_ref, ct_ref):
        lane = lax.broadcasted_iota(jnp.int32, (1, LANES), 1)
        tri = (lax.broadcasted_iota(jnp.int32, (tb, tb), 0)
               >= lax.broadcasted_iota(jnp.int32, (tb, tb), 1)).astype(F32)
        carry = jnp.zeros((1, LANES), F32)
        for blk in range(T // tb):
            z = u_ref[pl.ds(blk * tb, tb), :] + b_ref[...]
            lf = jnp.minimum(z, 0.0) - jnp.log1p(jnp.exp(-jnp.abs(z)))
            lf = jnp.where(lane < N_FOX, lf, 0.0)
            cb = _dot(tri, lf, 1, 0, precision=lax.Precision.HIGHEST) + carry
            c_ref[pl.ds(blk * tb, tb), :] = cb
            ct_ref[:, pl.ds(blk * tb, tb)] = cb.T
            carry = cb[tb - 1:tb, :]

    return pl.pallas_call(
        body, name=name,
        out_shape=[jax.ShapeDtypeStruct((T, LANES), F32), jax.ShapeDtypeStruct((LANES, T), F32)],
        compiler_params=_params(None),
    )(uf, bf)


def fox_gate_bwd(dct, uf, bf, name):
    T = uf.shape[0]
    tb = _tile(T, 512)

    def body(d_ref, u_ref, b_ref, du_ref, db_ref):
        lane = lax.broadcasted_iota(jnp.int32, (1, LANES), 1)
        triu = (lax.broadcasted_iota(jnp.int32, (tb, tb), 0)
                <= lax.broadcasted_iota(jnp.int32, (tb, tb), 1)).astype(F32)
        carry = jnp.zeros((1, LANES), F32)
        dbv = jnp.zeros((1, LANES), F32)
        for blk in reversed(range(T // tb)):
            dc = d_ref[:, pl.ds(blk * tb, tb)].T
            dlf = _dot(triu, dc, 1, 0, precision=lax.Precision.HIGHEST) + carry
            carry = dlf[0:1, :]
            z = u_ref[pl.ds(blk * tb, tb), :] + b_ref[...]
            dz = jnp.where(lane < N_FOX, dlf * (1.0 - _sigmoid(z)), 0.0)
            du_ref[pl.ds(blk * tb, tb), :] = dz
            dbv = dbv + jnp.sum(dz, axis=0, keepdims=True)
        db_ref[...] = dbv

    return pl.pallas_call(
        body, name=name,
        out_shape=[jax.ShapeDtypeStruct((T, LANES), F32), jax.ShapeDtypeStruct((1, LANES), F32)],
        compiler_params=_params(None),
    )(dct, uf, bf)


def _bias_slot(h):
    return jnp.maximum(h - (N_FOX - 1), 0)


def _scores(q_ref, k_ref, c_ref, ct_ref, tb_ref, h, i, tq, fox):
    scale = HEAD_DIM ** -0.5
    n = (i + 1) * tq
    rows = pl.ds(i * tq, tq)
    s = _dot(q_ref[rows, :], k_ref[pl.ds(0, n), :], 1, 1) * scale
    if not fox:
        return s + jnp.concatenate([tb_ref[i - jb] for jb in range(i + 1)], axis=1)
    lane = lax.broadcasted_iota(jnp.int32, (1, LANES), 1)
    c_col = jnp.sum(jnp.where(lane == h, c_ref[rows, :], 0.0), axis=1, keepdims=True)
    c_row = ct_ref[pl.ds(h, 1), pl.ds(0, n)]
    s = s + (c_col - c_row)
    if i == 0:
        return s + tb_ref[0]
    return jnp.concatenate([s[:, :i * tq], s[:, i * tq:] + tb_ref[0]], axis=1)


def _attn_specs(T, tq):
    nkb = T // tq
    return [
        pl.BlockSpec((T, HEAD_DIM), lambda h: (0, h)),
        pl.BlockSpec((T, HEAD_DIM), lambda h: (0, N_HEADS + h)),
        pl.BlockSpec((T, HEAD_DIM), lambda h: (0, 2 * N_HEADS + h)),
        pl.BlockSpec((T, LANES), lambda h: (0, 0)),
        pl.BlockSpec((LANES, T), lambda h: (0, 0)),
        pl.BlockSpec((None, nkb, tq, tq), lambda h: (_bias_slot(h), 0, 0, 0)),
    ]


def attention_fwd(qkv, c, ct, tiles, name):
    T = qkv.shape[0]
    tq = tiles.shape[2]

    def body(q_ref, k_ref, v_ref, c_ref, ct_ref, tb_ref, o_ref, lse_ref):
        h = pl.program_id(0)
        lane = lax.broadcasted_iota(jnp.int32, (1, LANES), 1)

        @pl.when(h == 0)
        def _():
            lse_ref[...] = jnp.zeros_like(lse_ref)

        def head(fox):
            for i in range(T // tq):
                rows = pl.ds(i * tq, tq)
                s = _scores(q_ref, k_ref, c_ref, ct_ref, tb_ref, h, i, tq, fox)
                m = jnp.max(s, axis=1, keepdims=True)
                p = jnp.exp(s - m)
                l = jnp.sum(p, axis=1, keepdims=True)
                o = _dot(p.astype(BF16), v_ref[pl.ds(0, (i + 1) * tq), :], 1, 0) * (1.0 / l)
                o_ref[rows, :] = o.astype(BF16)
                lse_ref[rows, :] = jnp.where(lane == h, m + jnp.log(l), lse_ref[rows, :])

        pl.when(h < N_FOX)(functools.partial(head, True))
        pl.when(h >= N_FOX)(functools.partial(head, False))

    return pl.pallas_call(
        body, name=name, grid=(N_HEADS,),
        in_specs=_attn_specs(T, tq),
        out_specs=[pl.BlockSpec((T, HEAD_DIM), lambda h: (0, h)), pl.BlockSpec((T, LANES), lambda h: (0, 0))],
        out_shape=[jax.ShapeDtypeStruct((T, N_HEADS * HEAD_DIM), BF16), jax.ShapeDtypeStruct((T, LANES), F32)],
        compiler_params=_params(("arbitrary",)),
    )(qkv, qkv, qkv, c, ct, tiles)


def attention_bwd(qkv, c, ct, tiles, lse, o, do, name):
    T = qkv.shape[0]
    tq = tiles.shape[2]
    nkb = T // tq
    scale = HEAD_DIM ** -0.5

    def body(q_ref, k_ref, v_ref, c_ref, ct_ref, tb_ref, lse_ref, o_ref, do_ref,
             dq_ref, dk_ref, dv_ref, dct_ref, dtb_ref, dk_acc, dv_acc):
        h = pl.program_id(0)
        lane = lax.broadcasted_iota(jnp.int32, (1, LANES), 1)
        dk_acc[...] = jnp.zeros_like(dk_acc)
        dv_acc[...] = jnp.zeros_like(dv_acc)
        dct_ref[...] = jnp.zeros_like(dct_ref)
        dtb_ref[...] = jnp.zeros_like(dtb_ref)

        def head(fox):
            for i in range(nkb):
                rows, keys = pl.ds(i * tq, tq), pl.ds(0, (i + 1) * tq)
                s = _scores(q_ref, k_ref, c_ref, ct_ref, tb_ref, h, i, tq, fox)
                lse_col = jnp.sum(jnp.where(lane == h, lse_ref[rows, :], 0.0), axis=1, keepdims=True)
                p = jnp.exp(s - lse_col)
                p_b = p.astype(BF16)
                dov = do_ref[rows, :]
                dp = _dot(dov, v_ref[keys, :], 1, 1)
                if fox:
                    delta = jnp.sum(p * dp, axis=1, keepdims=True)
                else:
                    delta = jnp.sum(dov.astype(F32) * o_ref[rows, :].astype(F32), axis=1, keepdims=True)
                ds = p * (dp - delta)
                ds_b = ds.astype(BF16)
                dq_ref[rows, :] = (_dot(ds_b, k_ref[keys, :], 1, 0) * scale).astype(BF16)
                dk_acc[:, keys] += _dot(q_ref[rows, :], ds_b, 0, 0) * scale
                dv_acc[:, keys] += _dot(dov, p_b, 0, 0)
                if fox:
                    dct_ref[:, keys] += -jnp.sum(ds, axis=0, keepdims=True)
                else:
                    for jb in range(i + 1):
                        dtb_ref[i - jb] += ds[:, jb * tq:(jb + 1) * tq]

        pl.when(h < N_FOX)(functools.partial(head, True))
        pl.when(h >= N_FOX)(functools.partial(head, False))
        dk_ref[...] = dk_acc[...].T.astype(BF16)
        dv_ref[...] = dv_acc[...].T.astype(BF16)

    head_cols = jax.ShapeDtypeStruct((T, N_HEADS * HEAD_DIM), BF16)
    col = pl.BlockSpec((T, HEAD_DIM), lambda h: (0, h))
    return pl.pallas_call(
        body, name=name, grid=(N_HEADS,),
        in_specs=_attn_specs(T, tq) + [pl.BlockSpec((T, LANES), lambda h: (0, 0)), col, col],
        out_specs=[col, col, col,
                   pl.BlockSpec((None, 1, T), lambda h: (h, 0, 0)),
                   pl.BlockSpec((None, nkb, tq, tq), lambda h: (_bias_slot(h), 0, 0, 0))],
        out_shape=[head_cols, head_cols, head_cols,
                   jax.ShapeDtypeStruct((N_HEADS, 1, T), F32),
                   jax.ShapeDtypeStruct((1 + N_DIL, nkb, tq, tq), F32)],
        scratch_shapes=[pltpu.VMEM((HEAD_DIM, T), F32), pltpu.VMEM((HEAD_DIM, T), F32)],
        compiler_params=_params(("arbitrary",)),
    )(qkv, qkv, qkv, c, ct, tiles, lse, o, do)


def rel_table_grad(dtiles, T, name):
    tq = dtiles.shape[2]
    nkb = T // tq
    bidx, _, present = _tile_buckets(T, tq)

    def body(d_ref, b_ref, o_ref):
        lane = lax.broadcasted_iota(jnp.int32, (1, LANES), 1)
        row = jnp.zeros((1, LANES), F32)
        for k in range(nkb):
            d = d_ref[k]
            bi = b_ref[k]
            for b in present[k]:
                v = jnp.sum(jnp.sum(jnp.where(bi == b, d, 0.0), axis=0, keepdims=True),
                            axis=1, keepdims=True)
                row = row + jnp.where(lane == b, v, 0.0)
        o_ref[...] = row

    return pl.pallas_call(
        body, name=name, grid=(N_DIL,),
        in_specs=[pl.BlockSpec((None, nkb, tq, tq), lambda h: (h + 1, 0, 0, 0)),
                  pl.BlockSpec((nkb, tq, tq), lambda h: (0, 0, 0))],
        out_specs=pl.BlockSpec((None, 1, LANES), lambda h: (h, 0, 0)),
        out_shape=jax.ShapeDtypeStruct((N_DIL, 1, LANES), F32),
        compiler_params=_params(("parallel",)),
    )(dtiles, jnp.asarray(bidx))


def _peer_list():
    x, y, c = lax.axis_index("x"), lax.axis_index("y"), lax.axis_index("c")
    me = 4 * x + 2 * y + c
    peers = []
    for fx in (0, 1):
        for fy in (0, 1):
            for fc in (0, 1):
                if fx or fy or fc:
                    px = 1 - x if fx else x
                    py = 1 - y if fy else y
                    pc = 1 - c if fc else c
                    peers.append(((px, py, pc), 4 * px + 2 * py + pc))
    return me, peers


_HBM = pl.BlockSpec(memory_space=pltpu.HBM)
_SEM = pl.BlockSpec(memory_space=pltpu.SEMAPHORE)
_EFFECT = pltpu.SideEffectType.DATAFLOW_SIDE_EFFECTING
N_PEERS = N_DEV - 1


def _in_hbm(a):
    return pltpu.with_memory_space_constraint(a, pltpu.HBM)


def _exchange_copies(srcs, lands, send_sems, recv_sems, blockwise):
    me, peers = _peer_list()
    sends, recvs = [], []
    for a in range(len(srcs)):
        for k, (dev, idx) in enumerate(peers):
            src = srcs[a].at[idx] if blockwise[a] else srcs[a]
            sends.append(pltpu.make_async_remote_copy(
                src_ref=src, dst_ref=lands[a].at[me], send_sem=send_sems[a].at[k],
                recv_sem=recv_sems[a].at[k], device_id=dev, device_id_type=MESH))
            recvs.append(pltpu.make_async_remote_copy(
                src_ref=src, dst_ref=lands[a].at[idx], send_sem=send_sems[a].at[k],
                recv_sem=recv_sems[a].at[k], device_id=dev, device_id_type=MESH))
    return sends, recvs


def exchange_start(srcs, lands, blockwise, name):
    n = len(srcs)

    def body(*refs):
        src_in, land_in = refs[:n], refs[n:2 * n]
        send_sems, recv_sems = refs[2 * n:3 * n], refs[3 * n:4 * n]
        token = refs[6 * n]
        sends, _ = _exchange_copies(src_in, land_in, send_sems, recv_sems, blockwise)
        for cp in sends:
            cp.start()
        token[...] = jnp.zeros_like(token)

    out_shape = ([pltpu.SemaphoreType.DMA((N_PEERS,))] * (2 * n)
                 + [pltpu.HBM(s.shape, s.dtype) for s in srcs]
                 + [pltpu.HBM(l.shape, l.dtype) for l in lands]
                 + [jax.ShapeDtypeStruct((8, LANES), F32)])
    aliases = {a: 2 * n + a for a in range(2 * n)}
    outs = pl.pallas_call(
        body, name=name, out_shape=out_shape,
        in_specs=[_HBM] * (2 * n),
        out_specs=[_SEM] * (2 * n) + [_HBM] * (2 * n) + [pl.BlockSpec(memory_space=pltpu.VMEM)],
        input_output_aliases=aliases,
        compiler_params=pltpu.CompilerParams(has_side_effects=_EFFECT),
    )(*[_in_hbm(s) for s in srcs], *[_in_hbm(l) for l in lands])
    return (outs[:n], outs[n:2 * n], outs[2 * n:3 * n], outs[3 * n:4 * n], outs[4 * n])


def exchange_wait(send_sems, recv_sems, srcs, lands, blockwise, after, name):
    n = len(srcs)

    def body(*refs):
        src_in, land_in = refs[:n], refs[n:2 * n]
        ss, rs = refs[2 * n:3 * n], refs[3 * n:4 * n]
        sends, recvs = _exchange_copies(src_in, land_in, ss, rs, blockwise)
        for cp in sends:
            cp.wait_send()
        for cp in recvs:
            cp.wait_recv()

    outs = pl.pallas_call(
        body, name=name,
        out_shape=[pltpu.HBM(s.shape, s.dtype) for s in srcs] + [pltpu.HBM(l.shape, l.dtype) for l in lands],
        in_specs=[_HBM] * (2 * n) + [_SEM] * (2 * n) + [pl.BlockSpec(memory_space=pl.ANY)],
        out_specs=[_HBM] * (2 * n),
        input_output_aliases={a: a for a in range(2 * n)},
        compiler_params=pltpu.CompilerParams(has_side_effects=_EFFECT),
    )(*srcs, *lands, *send_sems, *recv_sems, after)
    return outs[n:]


def _landing(own_block, me, slots=N_DEV):
    empty = lax.empty((slots,) + own_block.shape, own_block.dtype)
    return lax.dynamic_update_slice(empty, own_block[None], (me,) + (0,) * own_block.ndim)


N_CHIPS = N_DEV // 2
_CHIP_FLIPS = ((1, 0), (0, 1), (1, 1))


def _xyc():
    return lax.axis_index("x"), lax.axis_index("y"), lax.axis_index("c")


def _other_chips(x, y):
    return [(1 - x if fx else x, 1 - y if fy else y) for fx, fy in _CHIP_FLIPS]


def _remote(src, dst, send_sem, recv_sem, dev):
    return pltpu.make_async_remote_copy(src_ref=src, dst_ref=dst, send_sem=send_sem, recv_sem=recv_sem,
                                        device_id=dev, device_id_type=MESH)


def comm_call(name, bufs, sems_in, sems_out, fn, after=None, want_token=False):
    nb, ni, no = len(bufs), len(sems_in), len(sems_out)
    afters = [] if after is None else (list(after) if isinstance(after, (list, tuple)) else [after])
    na = len(afters)

    def body(*refs):
        buf_refs = refs[:nb]
        sin = refs[nb:nb + ni]
        sout = refs[nb + ni + na:nb + ni + na + no]
        fn(buf_refs, sin, sout)
        if want_token:
            tok = refs[nb + ni + na + no + nb]
            tok[...] = jnp.zeros_like(tok)

    out_shape = list(sems_out) + [pltpu.HBM(b.shape, b.dtype) for b in bufs]
    out_specs = [_SEM] * no + [_HBM] * nb
    if want_token:
        out_shape.append(jax.ShapeDtypeStruct((8, LANES), F32))
        out_specs.append(pl.BlockSpec(memory_space=pltpu.VMEM))
    args = [_in_hbm(b) for b in bufs] + list(sems_in) + afters
    outs = pl.pallas_call(
        body, name=name, out_shape=out_shape,
        in_specs=[_HBM] * nb + [_SEM] * ni + [pl.BlockSpec(memory_space=pl.ANY)] * na,
        out_specs=out_specs, input_output_aliases={a: no + a for a in range(nb)},
        compiler_params=pltpu.CompilerParams(has_side_effects=_EFFECT),
    )(*args)
    return list(outs[:no]), list(outs[no:no + nb]), (outs[no + nb] if want_token else None)


def _dma_sems(*sizes):
    return [pltpu.SemaphoreType.DMA((s,)) for s in sizes]


def gather_start(srcs, lands, name, after=None):
    n = len(srcs)

    def fn(bufs, sin, sout):
        x, y, c = _xyc()
        me = 4 * x + 2 * y + c
        for a in range(n):
            src, land = bufs[a], bufs[n + a]
            send, recv_d, recv_i = sout[3 * a:3 * a + 3]
            _remote(src, land.at[me], send.at[0], recv_d.at[0], (x, y, 1 - c)).start()
            for k, (px, py) in enumerate(_other_chips(x, y)):
                _remote(src, land.at[me], send.at[1 + k], recv_i.at[k], (px, py, c)).start()

    return comm_call(name, list(srcs) + list(lands), [], _dma_sems(4, 1, 3) * n, fn, after=after, want_token=True)


def gather_forward(srcs, lands, recv_i, after, name):
    n = len(srcs)

    def fn(bufs, sin, sout):
        x, y, c = _xyc()
        for a in range(n):
            src, land = bufs[a], bufs[n + a]
            f_send, f_recv = sout[2 * a:2 * a + 2]
            for k, (px, py) in enumerate(_other_chips(x, y)):
                blk = land.at[4 * px + 2 * py + c]
                _remote(src, blk, f_send.at[k], sin[a].at[k], (px, py, c)).wait_recv()
                _remote(blk, blk, f_send.at[k], f_recv.at[k], (x, y, 1 - c)).start()

    sems, bufs, _ = comm_call(name, list(srcs) + list(lands), recv_i, _dma_sems(3, 3) * n, fn, after=after)
    return sems, bufs


def gather_wait(srcs, lands, send, recv_d, f_send, f_recv, after, name):
    n = len(srcs)

    def fn(bufs, sin, sout):
        x, y, c = _xyc()
        sib = (x, y, 1 - c)
        for a in range(n):
            src, land = bufs[a], bufs[n + a]
            s_send, s_recv_d, s_fsend, s_frecv = sin[4 * a:4 * a + 4]
            sib_blk = land.at[4 * x + 2 * y + 1 - c]
            for k in range(4):
                _remote(src, sib_blk, s_send.at[k], s_recv_d.at[0], sib).wait_send()
            _remote(src, sib_blk, s_send.at[0], s_recv_d.at[0], sib).wait_recv()
            for k, (px, py) in enumerate(_other_chips(x, y)):
                cp = _remote(src, land.at[4 * px + 2 * py + 1 - c], s_fsend.at[k], s_frecv.at[k], sib)
                cp.wait_send()
                cp.wait_recv()

    sems_in = []
    for a in range(n):
        sems_in += [send[a], recv_d[a], f_send[a], f_recv[a]]
    _, bufs, _ = comm_call(name, list(srcs) + list(lands), sems_in, [], fn, after=after)
    return bufs[n:]


def scatter_pair_start(src4s, lands, name, after=None):
    n = len(src4s)

    def fn(bufs, sin, sout):
        x, y, c = _xyc()
        for a in range(n):
            _remote(bufs[a].at[:, 1 - c], bufs[n + a], sout[2 * a].at[0], sout[2 * a + 1].at[0],
                    (x, y, 1 - c)).start()

    return comm_call(name, list(src4s) + list(lands), [], _dma_sems(1, 1) * n, fn, after=after, want_token=True)


def scatter_pair_wait(src4s, lands, sems, after, name):
    n = len(src4s)

    def fn(bufs, sin, sout):
        x, y, c = _xyc()
        for a in range(n):
            cp = _remote(bufs[a].at[:, 1 - c], bufs[n + a], sin[2 * a].at[0], sin[2 * a + 1].at[0], (x, y, 1 - c))
            cp.wait_send()
            cp.wait_recv()

    _, bufs, _ = comm_call(name, list(src4s) + list(lands), sems, [], fn, after=after)
    return bufs[:n], bufs[n:]


def _row_tile(R):
    for cand in range(256, 15, -16):
        if R % cand == 0 and R // cand >= 4:
            return cand
    return R


def chip_sum(src4, land, c, name):
    _, _, R, C = src4.shape
    tr = R

    def body(c_ref, a_ref, b_ref, o_ref):
        o_ref[...] = (a_ref[...].astype(F32) + b_ref[...].astype(F32)).astype(BF16)

    grid_spec = pltpu.PrefetchScalarGridSpec(
        num_scalar_prefetch=1, grid=(N_CHIPS, R // tr),
        in_specs=[pl.BlockSpec((None, None, tr, C), lambda q, i, cr: (q, cr[0], i, 0)),
                  pl.BlockSpec((None, tr, C), lambda q, i, cr: (q, i, 0))],
        out_specs=pl.BlockSpec((None, tr, C), lambda q, i, cr: (q, i, 0)))
    return pl.pallas_call(
        body, name=name, grid_spec=grid_spec,
        out_shape=jax.ShapeDtypeStruct((N_CHIPS, R, C), BF16),
        compiler_params=_params(("parallel", "parallel")),
    )(c.reshape(1).astype(jnp.int32), src4, land)


def scatter_chip_start(sums, lands, name):
    n = len(sums)

    def fn(bufs, sin, sout):
        x, y, c = _xyc()
        for a in range(n):
            for k, (px, py) in enumerate(_other_chips(x, y)):
                _remote(bufs[a].at[2 * px + py], bufs[n + a].at[2 * x + y], sout[2 * a].at[k], sout[2 * a + 1].at[k],
                        (px, py, c)).start()

    return comm_call(name, list(sums) + list(lands), [], _dma_sems(3, 3) * n, fn, want_token=True)


def scatter_chip_wait(sums, lands, sems, after, name):
    n = len(sums)

    def fn(bufs, sin, sout):
        x, y, c = _xyc()
        for a in range(n):
            for k, (px, py) in enumerate(_other_chips(x, y)):
                cp = _remote(bufs[a].at[2 * px + py], bufs[n + a].at[2 * px + py], sin[2 * a].at[k],
                             sin[2 * a + 1].at[k], (px, py, c))
                cp.wait_send()
                cp.wait_recv()

    _, bufs, _ = comm_call(name, list(sums) + list(lands), sems, [], fn, after=after)
    return bufs[:n], bufs[n:]


def _adamw_math(w, g, m, v):
    m = ADAM_B1 * m + (1.0 - ADAM_B1) * g
    v = ADAM_B2 * v + (1.0 - ADAM_B2) * (g * g)
    m_hat = m / (1.0 - ADAM_B1 ** ADAM_STEP)
    v_hat = v / (1.0 - ADAM_B2 ** ADAM_STEP)
    delta = -ADAM_LR * (m_hat / (jnp.sqrt(v_hat) + ADAM_EPS) + ADAM_WD * w)
    return delta, m, v


def _sum_partials(p_ref, own_ref, mine):
    own = own_ref[...].astype(F32)
    g = None
    for s in range(p_ref.shape[0]):
        term = jnp.where(mine == s, own, p_ref[s].astype(F32))
        g = term if g is None else g + term
    return g


def adamw_sharded(parts, sums, my_chip, w, m, v, name):
    R, C = w.shape
    S = parts.shape[0]
    tr = _row_tile(R)

    def body(mc_ref, p_ref, o_ref, w_ref, m_ref, v_ref, g_ref, d_ref, nm_ref, nv_ref):
        g = _sum_partials(p_ref, o_ref, mc_ref[0])
        delta, nm, nv = _adamw_math(w_ref[...], g, m_ref[...], v_ref[...])
        g_ref[...] = g
        d_ref[...] = delta
        nm_ref[...] = nm
        nv_ref[...] = nv

    row = pl.BlockSpec((tr, C), lambda i, mc: (i, 0))
    shp = jax.ShapeDtypeStruct((R, C), F32)
    grid_spec = pltpu.PrefetchScalarGridSpec(
        num_scalar_prefetch=1, grid=(R // tr,),
        in_specs=[pl.BlockSpec((S, tr, C), lambda i, mc: (0, i, 0)),
                  pl.BlockSpec((None, tr, C), lambda i, mc: (mc[0], i, 0)), row, row, row],
        out_specs=[row, row, row, row])
    return pl.pallas_call(
        body, name=name, grid_spec=grid_spec, out_shape=[shp, shp, shp, shp],
        compiler_params=_params(("parallel",)),
    )(my_chip.reshape(1).astype(jnp.int32), parts, sums, w, m, v)


def adamw_small(parts, w, m, v, name):
    R, C = w.shape

    def body(p_ref, w_ref, m_ref, v_ref, g_ref, d_ref, nm_ref, nv_ref):
        g = p_ref[0]
        for s in range(1, N_DEV):
            g = g + p_ref[s]
        delta, nm, nv = _adamw_math(w_ref[...], g, m_ref[...], v_ref[...])
        g_ref[...] = g
        d_ref[...] = delta
        nm_ref[...] = nm
        nv_ref[...] = nv

    shp = jax.ShapeDtypeStruct((R, C), F32)
    return pl.pallas_call(
        body, name=name, out_shape=[shp, shp, shp, shp], compiler_params=_params(None),
    )(parts, w, m, v)


_ROW_NORM_FFN1, _ROW_NORM_MIX, _ROW_NORM_FFN2, _ROW_NORM_PLE, _ROW_NORM_FINAL = 0, 1, 2, 3, 4
_ROW_B_F, _ROW_REL, _ROW_LOSS, _SMALL_ROWS = 5, 6, 7, 8


def _pack_small(D, norm_ffn1, norm_mix, norm_ffn2, norm_ple, norm_final, b_f, rel_table):
    def row(v):
        v = v.reshape(1, -1)
        return jnp.pad(v, ((0, 0), (0, D - v.shape[1])))
    return jnp.concatenate([row(norm_ffn1), row(norm_mix), row(norm_ffn2), row(norm_ple),
                            row(norm_final), row(b_f), row(rel_table),
                            jnp.zeros((1, D), F32)], axis=0)


def _unpack_small(a, shapes):
    return {"norm_ffn1": a[_ROW_NORM_FFN1].reshape(shapes["norm_ffn1"]),
            "norm_mix": a[_ROW_NORM_MIX].reshape(shapes["norm_mix"]),
            "b_f": a[_ROW_B_F, :N_FOX].reshape(shapes["b_f"]),
            "norm_ffn2": a[_ROW_NORM_FFN2].reshape(shapes["norm_ffn2"]),
            "norm_ple": a[_ROW_NORM_PLE].reshape(shapes["norm_ple"]),
            "rel_table": a[_ROW_REL, :N_REL_BUCKETS * N_DIL].reshape(shapes["rel_table"]),
            "norm_final": a[_ROW_NORM_FINAL].reshape(shapes["norm_final"])}


def local_step(x, p, tgt, g_ffn1, g_mix, g_ffn2, g_ple, g_final, b_f, rel_table,
               forward, weights, emit, emit2, first_dep):
    T, D = x.shape
    P = p.shape[1]
    CW = D // N_DEV
    tq = _tile(T, 256)

    h1 = rms_fwd(x, g_ffn1, "rms_ffn1", dep=first_dep)
    tiles = bias_tiles(rel_table, T, tq)
    forward("ffn1_g", [tiles, h1])
    wg1, = weights("ffn1_g", h1)
    gate1 = ffn_gate(h1, wg1, "ffn1_gate")
    forward("ffn1_u", gate1)
    wu1, = weights("ffn1_u", gate1)
    a1, b1, s1 = ffn_up_gated(h1, wu1, gate1, "ffn1_up")
    forward("ffn1_d", s1)
    wd1, = weights("ffn1_d", s1)
    x1 = ffn_down(s1, wd1, 0, 1, x, "ffn1_down")

    h2 = rms_fwd(x1, g_mix, "rms_mix")
    forward("mix", h2)
    w3, wf, wo = weights("mix", h2)
    qkv = mm_nt([(h2, w3)], "mix_qkv", tn=768, out_dtype=BF16)
    uf = mm_nt([(h2, wf)], "mix_forget", tn=LANES, out_dtype=F32)
    bfp = jnp.pad(b_f.reshape(1, N_FOX), ((0, 0), (0, LANES - N_FOX)))
    c, ct = fox_gate_fwd(uf, bfp, "fox_gate")
    cat, lse = attention_fwd(qkv, c, ct, tiles, "attention")
    x2 = mm_nn(cat, wo, "mix_out", tn=512, out_dtype=F32, res=x1)

    h3 = rms_fwd(x2, g_ffn2, "rms_ffn2")
    forward("ffn2", h3)
    wgu2, wd2 = weights("ffn2", h3)
    a2, b2, s2 = ffn_up(h3, wgu2, 0, 1, 2, "ffn2_up")
    forward("ple", s2)
    x3 = ffn_down(s2, wd2, 0, 1, x2, "ffn2_down")

    h4 = rms_fwd(x3, g_ple, "rms_ple")
    wpg, wpp = weights("ple", h4)
    z = mm_nn(h4, wpg, "ple_gate", tn=512, out_dtype=F32)
    pp = mm_nn(p, wpp, "ple_proj", tn=CW, tm=T, out_dtype=F32, n_out=D,
               b_block=(P, CW), b_map=lambda n, i: (n, 0))
    loss_row, dx4, dg_final, dz, dpp = ple_loss(x3, z, pp, g_final, tgt, "ple_loss")

    grads = {}
    grads["w_ple_proj"] = mm_tn(p, dpp, "ple_proj_dw", grid=(N_DEV,),
                                a_block=(T, P), a_map=lambda n: (0, 0),
                                b_block=(T, CW), b_map=lambda n: (0, n),
                                o_block=(P, CW), o_map=lambda n: (n, 0),
                                out_shape=(N_DEV * P, CW))
    grads["w_ple_gate"] = mm_tn_plain(h4, dz, "ple_gate_dw")
    tok = emit("ple", grads)
    dh4 = mm_nt([(dz, wpg)], "ple_gate_dh", tn=512, out_dtype=F32, dep=tok)
    tok = emit2("ple", dh4)
    dx3, dx3h, dg_ple = rms_bwd(dh4, x3, g_ple, dx4, "rms_ple_bwd", dep=tok)

    da2, db2 = ffn_bwd_act(dx3h, wd2, 0, 1, a2, b2, "ffn2_bwd_act")
    grads["ffn2_w_down"] = ffn_bwd_dw_down(s2, dx3h, "ffn2_down_dw")
    grads["ffn2_w_gate"] = ffn_bwd_dw_in(h3, da2, "ffn2_gate_dw")
    grads["ffn2_w_up"] = ffn_bwd_dw_in(h3, db2, "ffn2_up_dw")
    tok = emit("ffn2", grads)
    dh3 = ffn_bwd_dh(da2, db2, wgu2, wgu2, 0, 1, 2, D, "ffn2_bwd_dh", dep=tok)
    tok = emit2("ffn2", dh3)
    dx2, _, dg_ffn2 = rms_bwd(dh3, x2, g_ffn2, dx3, "rms_ffn2_bwd", dep=tok, half=False)

    dcat = mm_nt([(dx2, wo)], "mix_out_dh", tn=512, out_dtype=BF16)
    grads["w_o"] = mm_tn_plain(cat, dx2, "mix_out_dw")
    dq, dk, dv, dct, dtiles = attention_bwd(qkv, c, ct, tiles, lse, cat, dcat, "attention_bwd")
    dctp = jnp.pad(dct[:, 0, :], ((0, LANES - N_HEADS), (0, 0)))
    duf, dbf = fox_gate_bwd(dctp, uf, bfp, "fox_gate_bwd")
    drel = rel_table_grad(dtiles, T, "rel_table_grad")[:, 0, :N_REL_BUCKETS].T
    du3 = jnp.concatenate([dq, dk, dv], axis=1)
    grads["w3"] = mm_tn_plain(du3, h2, "mix_qkv_dw", tm=768)
    grads["wf"] = mm_tn_plain(duf, h2, "mix_forget_dw", tm=LANES)
    tok = emit("mix", grads)
    dh2 = mm_nn_sum([(du3, w3), (duf, wf)], "mix_in_dh", tn=512, out_dtype=F32, dep=tok)
    tok = emit2("mix", dh2)
    dx1, dx1h, dg_mix = rms_bwd(dh2, x1, g_mix, dx2, "rms_mix_bwd", dep=tok)

    da1, db1 = ffn_bwd_act(dx1h, wd1, 0, 1, a1, b1, "ffn1_bwd_act")
    grads["ffn1_w_down"] = ffn_bwd_dw_down(s1, dx1h, "ffn1_down_dw")
    tok = emit("ffn1_d", grads)
    grads["ffn1_w_gate"] = ffn_bwd_dw_in(h1, da1, "ffn1_gate_dw", dep=tok)
    tok = emit2("ffn1_d", grads["ffn1_w_gate"])
    tok = emit("ffn1_g", grads, after=tok)
    grads["ffn1_w_up"] = ffn_bwd_dw_in(h1, db1, "ffn1_up_dw", dep=tok)
    tok = emit2("ffn1_g", grads["ffn1_w_up"])
    tok = emit("ffn1_u", grads, after=tok)
    dh1 = ffn_bwd_dh(da1, db1, wg1, wu1, 0, 0, 1, D, "ffn1_bwd_dh", dep=tok)
    tok = emit2("ffn1_u", dh1)
    dx0, _, dg_ffn1 = rms_bwd(dh1, x, g_ffn1, dx1, "rms_ffn1_bwd", dep=tok, half=False)

    small = _pack_small(D, dg_ffn1, dg_mix, dg_ffn2, dg_ple, dg_final, dbf[:, :N_FOX], drel)
    small = small.at[_ROW_LOSS, :LANES].set(loss_row[0])
    grads["small"] = small
    emit("small", grads)
    return dx0


def _split_w_in(w_in_t):
    df, dd = N_FOX * HEAD_DIM, N_DIL * HEAD_DIM
    o = np.cumsum([0, df, df, df, N_FOX, dd, dd, dd]).tolist()
    qa, ka, va, f, qb, kb, vb = [w_in_t[o[i]:o[i + 1]] for i in range(7)]
    return jnp.concatenate([qa, qb, ka, kb, va, vb], axis=0), f


def _join_w_in(d3, dfg):
    df, dd = N_FOX * HEAD_DIM, N_DIL * HEAD_DIM
    o = np.cumsum([0, df, dd, df, dd, df, dd]).tolist()
    qa, qb, ka, kb, va, vb = [d3[o[i]:o[i + 1]] for i in range(6)]
    return jnp.concatenate([qa, ka, va, dfg, qb, kb, vb], axis=0)


def rows_to_bf16(a3, name, dep=None):
    R, _, C = a3.shape
    tc = _tile(C, 512)

    def body(a_ref, *rest):
        rest[-1][...] = a_ref[...].astype(BF16)

    in_specs = [pl.BlockSpec((R, None, tc), lambda n: (0, 0, n))]
    args = [a3]
    if dep is not None:
        in_specs.append(_dep_spec(1))
        args.append(dep)
    return pl.pallas_call(
        body, name=name, grid=(C // tc,), in_specs=in_specs,
        out_specs=pl.BlockSpec((R, tc), lambda n: (0, n)),
        out_shape=jax.ShapeDtypeStruct((R, C), BF16),
        compiler_params=_params(("parallel",)),
    )(*args)


def adamw_rows3d(parts, sums, my_chip, w3, m3, v3, name):
    R, _, C = w3.shape
    S = parts.shape[0]
    tc = _tile(C, 256)

    def body(mc_ref, p_ref, o_ref, w_ref, m_ref, v_ref, g_ref, d_ref, nm_ref, nv_ref):
        g = _sum_partials(p_ref, o_ref, mc_ref[0])
        delta, nm, nv = _adamw_math(w_ref[...], g, m_ref[...], v_ref[...])
        g_ref[...] = g
        d_ref[...] = delta
        nm_ref[...] = nm
        nv_ref[...] = nv

    col = pl.BlockSpec((R, None, tc), lambda n, mc: (0, 0, n))
    shp = jax.ShapeDtypeStruct((R, 1, C), F32)
    grid_spec = pltpu.PrefetchScalarGridSpec(
        num_scalar_prefetch=1, grid=(C // tc,),
        in_specs=[pl.BlockSpec((S, R, tc), lambda n, mc: (0, 0, n)),
                  pl.BlockSpec((None, R, tc), lambda n, mc: (mc[0], 0, n)), col, col, col],
        out_specs=[col, col, col, col])
    return pl.pallas_call(
        body, name=name, grid_spec=grid_spec, out_shape=[shp, shp, shp, shp],
        compiler_params=_params(("parallel",)),
    )(my_chip.reshape(1).astype(jnp.int32), parts, sums, w3, m3, v3)


def kernel(x, p, norm_ffn1, ffn1_w_gate, ffn1_w_up, ffn1_w_down, norm_mix, w_in, b_f, w_o, norm_ffn2, ffn2_w_gate, ffn2_w_up, ffn2_w_down, norm_ple, w_ple_gate, w_ple_proj, rel_table, norm_final, loss_target, m_norm_ffn1, m_ffn1_w_gate, m_ffn1_w_up, m_ffn1_w_down, m_norm_mix, m_w_in, m_b_f, m_w_o, m_norm_ffn2, m_ffn2_w_gate, m_ffn2_w_up, m_ffn2_w_down, m_norm_ple, m_w_ple_gate, m_w_ple_proj, m_rel_table, m_norm_final, v_norm_ffn1, v_ffn1_w_gate, v_ffn1_w_up, v_ffn1_w_down, v_norm_mix, v_w_in, v_b_f, v_w_o, v_norm_ffn2, v_ffn2_w_gate, v_ffn2_w_up, v_ffn2_w_down, v_norm_ple, v_w_ple_gate, v_w_ple_proj, v_rel_table, v_norm_final):
    names = ["norm_ffn1", "ffn1_w_gate", "ffn1_w_up", "ffn1_w_down", "norm_mix", "w_in", "b_f", "w_o",
             "norm_ffn2", "ffn2_w_gate", "ffn2_w_up", "ffn2_w_down", "norm_ple", "w_ple_gate",
             "w_ple_proj", "rel_table", "norm_final"]
    w = dict(zip(names, [norm_ffn1, ffn1_w_gate, ffn1_w_up, ffn1_w_down, norm_mix, w_in, b_f, w_o,
                         norm_ffn2, ffn2_w_gate, ffn2_w_up, ffn2_w_down, norm_ple, w_ple_gate,
                         w_ple_proj, rel_table, norm_final]))
    m = dict(zip(names, [m_norm_ffn1, m_ffn1_w_gate, m_ffn1_w_up, m_ffn1_w_down, m_norm_mix, m_w_in,
                         m_b_f, m_w_o, m_norm_ffn2, m_ffn2_w_gate, m_ffn2_w_up, m_ffn2_w_down,
                         m_norm_ple, m_w_ple_gate, m_w_ple_proj, m_rel_table, m_norm_final]))
    v = dict(zip(names, [v_norm_ffn1, v_ffn1_w_gate, v_ffn1_w_up, v_ffn1_w_down, v_norm_mix, v_w_in,
                         v_b_f, v_w_o, v_norm_ffn2, v_ffn2_w_gate, v_ffn2_w_up, v_ffn2_w_down,
                         v_norm_ple, v_w_ple_gate, v_w_ple_proj, v_rel_table, v_norm_final]))
    sharded = ["ffn1_w_gate", "ffn1_w_up", "ffn1_w_down", "w_in", "w_o", "ffn2_w_gate", "ffn2_w_up",
               "ffn2_w_down", "w_ple_gate", "w_ple_proj"]
    small_names = [n for n in names if n not in sharded]

    xs, ps, tgt = x[0], p[0, 0], loss_target[0]
    T, D = xs.shape
    transposed = ("ffn1_w_gate", "ffn1_w_up", "ffn2_w_gate", "ffn2_w_up")

    def view(t, n):
        if n in transposed:
            return t[n][0].T
        if n == "w_in":
            return jnp.transpose(t[n], (2, 0, 1))
        return t[n][0]

    def unview(a, n):
        if n in transposed:
            return a.T.reshape(w[n].shape)
        if n == "w_in":
            return jnp.transpose(a, (1, 2, 0))
        return a.reshape(w[n].shape)

    sh = {n: view(w, n) for n in sharded}
    m_sh = {n: view(m, n) for n in sharded}
    v_sh = {n: view(v, n) for n in sharded}
    F8 = sh["ffn1_w_down"].shape[0]
    WIN8 = sh["w_in"].shape[0]
    me = 4 * lax.axis_index("x") + 2 * lax.axis_index("y") + lax.axis_index("c")

    def start(groups, name, after=None):
        srcs = [s for grp in groups for s in grp]
        sems, bufs, token = gather_start(srcs, [_landing(s, me) for s in srcs], name, after=after)
        return sems, bufs[:len(srcs)], bufs[len(srcs):], token

    cat0 = lambda ns, z: (jnp.concatenate([sh[n] for n in ns], axis=0) + z).astype(BF16)
    sems_a, srcs_a, lands_a, token_a = start(
        [[sh["ffn1_w_gate"].astype(BF16)], [sh["ffn1_w_up"].astype(BF16)], [sh["ffn1_w_down"].astype(BF16)]],
        "gather_start_ffn1")
    zero = token_a[0, 0]
    w_in_bf = rows_to_bf16(sh["w_in"], "w_in_bf16", dep=token_a)
    sems_b, srcs_b, lands_b, g_token = start(
        [[w_in_bf, (sh["w_o"] + zero).astype(BF16)],
         [cat0(["ffn2_w_gate", "ffn2_w_up"], zero), (sh["ffn2_w_down"] + zero).astype(BF16)],
         [(sh["w_ple_gate"] + zero).astype(BF16), (sh["w_ple_proj"] + zero).astype(BF16)]],
        "gather_start_rest", after=token_a)
    order = ["ffn1_g", "ffn1_u", "ffn1_d", "mix", "ffn2", "ple"]
    group_sizes = [1, 1, 1, 2, 2, 2]
    g_sems, g_srcs, g_lands = sems_a + sems_b, srcs_a + srcs_b, lands_a + lands_b
    g_send, g_recv_d, g_recv_i = g_sems[0::3], g_sems[1::3], g_sems[2::3]
    first = np.cumsum([0] + group_sizes).tolist()
    passed = {}

    def arrays_of(group):
        k = order.index(group)
        return slice(first[k], first[k + 1])

    def forward(group, after):
        sl = arrays_of(group)
        f_sems, bufs = gather_forward(g_srcs[sl], g_lands[sl], g_recv_i[sl], after, "gather_forward_" + group)
        k = len(bufs) // 2
        passed[group] = (f_sems[0::2], f_sems[1::2], bufs[:k], bufs[k:])

    def weights(group, after):
        sl = arrays_of(group)
        f_send, f_recv, srcs, lands = passed[group]
        got = gather_wait(srcs, lands, g_send[sl], g_recv_d[sl], f_send, f_recv, after, "gather_wait_" + group)
        if group in ("ffn1_g", "ffn1_u", "ffn1_d"):
            return (got[0].reshape(N_DEV * F8, D),)
        a0, a1 = got
        if group == "ffn2":
            return a0.reshape(N_DEV * 2 * F8, D), a1.reshape(N_DEV * F8, D)
        if group == "ple":
            return a0.reshape(-1, D), a1.reshape(-1, a1.shape[2])
        w3, wf8 = _split_w_in(a0.reshape(N_DEV * WIN8, D))
        return w3, jnp.pad(wf8, ((0, LANES - N_FOX), (0, 0))), a1.reshape(-1, D)

    scatter_groups = {
        "ple": ["w_ple_gate", "w_ple_proj"],
        "ffn2": ["ffn2_w_gate", "ffn2_w_up", "ffn2_w_down"],
        "mix": ["w_in", "w_o"],
        "ffn1_d": ["ffn1_w_down"],
        "ffn1_g": ["ffn1_w_gate"],
        "ffn1_u": ["ffn1_w_up"],
    }
    x_i, y_i, c_i = _xyc()
    my_chip = 2 * x_i + y_i
    pair_stage, chip_stage, small_stage = {}, {}, {}

    def emit(group, grads, after=None):
        if group == "small":
            src = grads["small"]
            ss, rs, srcs, lands, token = exchange_start([src], [_landing(src, me)], [False], "scatter_start_small")
            small_stage["small"] = (ss, rs, srcs, lands)
            return token
        src4s = []
        for n in scatter_groups[group]:
            if n == "w_in":
                full = _join_w_in(grads["w3"], grads["wf"][:N_FOX])
                src4s.append(full.reshape(N_CHIPS, 2, WIN8, D))
            else:
                src4s.append(grads[n].reshape((N_CHIPS, 2) + sh[n].shape))
        lands = [lax.empty((N_CHIPS,) + s.shape[2:], BF16) for s in src4s]
        sems, bufs, token = scatter_pair_start(src4s, lands, "scatter_pair_start_" + group, after=after)
        k = len(src4s)
        pair_stage[group] = (sems, bufs[:k], bufs[k:])
        return token

    def emit2(group, after):
        sems, src4s, lands = pair_stage[group]
        src4s, lands = scatter_pair_wait(src4s, lands, sems, after, "scatter_pair_wait_" + group)
        sums = [chip_sum(s4, la, c_i, "chip_sum_" + n)
                for s4, la, n in zip(src4s, lands, scatter_groups[group])]
        chip_lands = [lax.empty(s.shape, s.dtype) for s in sums]
        sems, bufs, token = scatter_chip_start(sums, chip_lands, "scatter_chip_start_" + group)
        k = len(sums)
        chip_stage[group] = (sems, bufs[:k], bufs[k:])
        return token

    dx0 = local_step(
        xs, ps, tgt, w["norm_ffn1"], w["norm_mix"], w["norm_ffn2"], w["norm_ple"],
        w["norm_final"].reshape(1, D), w["b_f"], w["rel_table"], forward, weights, emit, emit2, g_token)

    res = {}
    after = dx0
    for group in ["ple", "ffn2", "mix", "ffn1_d", "ffn1_g", "ffn1_u"]:
        sems, sums, chip_lands = chip_stage[group]
        sums, parts = scatter_chip_wait(sums, chip_lands, sems, after, "scatter_chip_wait_" + group)
        for n, part, own in zip(scatter_groups[group], parts, sums):
            update = adamw_rows3d if n == "w_in" else adamw_sharded
            g, d, nm, nv = update(part, own, my_chip, sh[n], m_sh[n], v_sh[n], "adamw_" + n)
            res[n] = tuple(unview(a, n) for a in (g, d, nm, nv))
            after = g
    ss, rs, srcs, lands = small_stage["small"]
    small_parts, = exchange_wait(ss, rs, srcs, lands, [False], after, "scatter_wait_small")
    pack = lambda t: _pack_small(D, t["norm_ffn1"], t["norm_mix"], t["norm_ffn2"], t["norm_ple"],
                                 t["norm_final"], t["b_f"], t["rel_table"])
    gs, ds, ms, vs = adamw_small(small_parts, pack(w), pack(m), pack(v), "adamw_small")
    shapes = {n: w[n].shape for n in small_names}
    unpacked = [_unpack_small(a, shapes) for a in (gs, ds, ms, vs)]
    for n in small_names:
        res[n] = tuple(u[n] for u in unpacked)
    loss = gs[_ROW_LOSS, 0]

    out = [loss, dx0.reshape(x.shape)]
    for k in range(4):
        out += [res[n][k] for n in names]
    return tuple(out)
```

```python
import functools
import math

import numpy as np
import jax
import jax.numpy as jnp
from jax import lax
from jax.experimental import pallas as pl
from jax.experimental.pallas import tpu as pltpu

F32 = jnp.float32
BF16 = jnp.bfloat16

N_DEV = 8
HEAD_DIM = 128
N_FOX = 8
N_DIL = 8
N_HEADS = N_FOX + N_DIL
DILATED_PATTERNS = ((128, 1), (512, 4), (2048, 16))
N_REL_BUCKETS = 32
REL_MAX_DISTANCE = 2048
RMS_EPS = 1e-6
NEG_INF = -1e30
LANES = 128
VMEM_LIMIT = 56 * 1024 * 1024

ADAM_LR = 0.001
ADAM_B1 = 0.9
ADAM_B2 = 0.999
ADAM_EPS = 1e-08
ADAM_WD = 0.01
ADAM_STEP = 10

MESH = pl.DeviceIdType.MESH


def _params(sem):
    return pltpu.CompilerParams(dimension_semantics=sem, vmem_limit_bytes=VMEM_LIMIT)


def _dot(a, b, ca, cb, precision=None):
    return lax.dot_general(a, b, (((ca,), (cb,)), ((), ())),
                           preferred_element_type=F32, precision=precision)


def _sigmoid(z):
    return 1.0 / (1.0 + jnp.exp(-z))


def _tile(n, want):
    t = min(n, want)
    assert n % t == 0, (n, t)
    return t


def _dep_spec(ngrid):
    return pl.BlockSpec((8, LANES), lambda *_: (0, 0))


def rms_fwd(x, g, name, dep=None):
    T, D = x.shape
    tm = _tile(T, 256)

    def body(x_ref, g_ref, *rest):
        h_ref = rest[-1]
        xv = x_ref[...]
        r = lax.rsqrt(jnp.mean(xv * xv, axis=-1, keepdims=True) + RMS_EPS)
        h_ref[...] = (xv * r * g_ref[...]).astype(BF16)

    in_specs = [pl.BlockSpec((tm, D), lambda i: (i, 0)), pl.BlockSpec((1, D), lambda i: (0, 0))]
    args = [x, g]
    if dep is not None:
        in_specs.append(_dep_spec(1))
        args.append(dep)
    return pl.pallas_call(
        body, name=name, grid=(T // tm,), in_specs=in_specs,
        out_specs=pl.BlockSpec((tm, D), lambda i: (i, 0)),
        out_shape=jax.ShapeDtypeStruct((T, D), BF16),
        compiler_params=_params(("parallel",)),
    )(*args)


def rms_bwd(dh, x, g, dres, name, dep=None, half=True):
    T, D = x.shape
    tm = _tile(T, 256)

    def body(dh_ref, x_ref, g_ref, dres_ref, *rest):
        dx_ref, dg_ref = (rest[-3], rest[-1]) if half else (rest[-2], rest[-1])
        i = pl.program_id(0)
        xv = x_ref[...]
        r = lax.rsqrt(jnp.mean(xv * xv, axis=-1, keepdims=True) + RMS_EPS)
        xh = xv * r
        d = dh_ref[...]
        u = d * g_ref[...]
        dx = dres_ref[...] + r * (u - xh * jnp.mean(u * xh, axis=-1, keepdims=True))
        dx_ref[...] = dx
        if half:
            rest[-2][...] = (0.5 * dx).astype(BF16)
        part = jnp.sum(d * xh, axis=0, keepdims=True)

        @pl.when(i == 0)
        def _():
            dg_ref[...] = part

        @pl.when(i > 0)
        def _():
            dg_ref[...] += part

    row = pl.BlockSpec((tm, D), lambda i: (i, 0))
    vec = pl.BlockSpec((1, D), lambda i: (0, 0))
    in_specs = [row, row, vec, row]
    args = [dh, x, g, dres]
    if dep is not None:
        in_specs.append(_dep_spec(1))
        args.append(dep)
    out_specs = [row, row, vec] if half else [row, vec]
    out_shape = [jax.ShapeDtypeStruct((T, D), F32)] + ([jax.ShapeDtypeStruct((T, D), BF16)] if half else [])
    out_shape.append(jax.ShapeDtypeStruct((1, D), F32))
    outs = pl.pallas_call(
        body, name=name, grid=(T // tm,),
        in_specs=in_specs, out_specs=out_specs, out_shape=out_shape,
        compiler_params=_params(("arbitrary",)),
    )(*args)
    return tuple(outs) if half else (outs[0], None, outs[1])


def ple_loss(x, z, pp, g, target, name):
    T, D = x.shape
    tm = _tile(T, 256)

    def body(x_ref, z_ref, p_ref, g_ref, t_ref, loss_ref, dx_ref, dg_ref, dz_ref, dp_ref):
        i = pl.program_id(0)
        gate = _sigmoid(z_ref[...])
        ppv = p_ref[...]
        xv = x_ref[...] + gate * ppv
        gv = g_ref[...]
        r = lax.rsqrt(jnp.mean(xv * xv, axis=-1, keepdims=True) + RMS_EPS)
        xh = xv * r
        e = xh * gv - t_ref[...]
        lpart = 0.5 * jnp.sum(jnp.mean(e * e, axis=-1, keepdims=True), axis=0, keepdims=True)
        lrow = jnp.broadcast_to(lpart, (1, LANES))
        d = e * (1.0 / D)
        u = d * gv
        dx = r * (u - xh * jnp.mean(u * xh, axis=-1, keepdims=True))
        dx_ref[...] = dx
        dz_ref[...] = (dx * ppv * gate * (1.0 - gate)).astype(BF16)
        dp_ref[...] = (dx * gate).astype(BF16)
        part = jnp.sum(d * xh, axis=0, keepdims=True)

        @pl.when(i == 0)
        def _():
            dg_ref[...] = part
            loss_ref[...] = lrow

        @pl.when(i > 0)
        def _():
            dg_ref[...] += part
            loss_ref[...] += lrow

    row = pl.BlockSpec((tm, D), lambda i: (i, 0))
    vec = pl.BlockSpec((1, D), lambda i: (0, 0))
    return pl.pallas_call(
        body, name=name, grid=(T // tm,),
        in_specs=[row, row, row, vec, row],
        out_specs=[pl.BlockSpec((1, LANES), lambda i: (0, 0)), row, vec, row, row],
        out_shape=[jax.ShapeDtypeStruct((1, LANES), F32), jax.ShapeDtypeStruct((T, D), F32),
                   jax.ShapeDtypeStruct((1, D), F32), jax.ShapeDtypeStruct((T, D), BF16),
                   jax.ShapeDtypeStruct((T, D), BF16)],
        compiler_params=_params(("arbitrary",)),
    )(x, z, pp, g, target)


def _bf(v, scale=None):
    if scale is not None:
        v = v * scale
    return v.astype(BF16)


def mm_nn(a, b, name, *, tn, out_dtype, tm=512, n_out=None, b_block=None, b_map=None,
          res=None):
    T, K = a.shape
    N = n_out if n_out is not None else b.shape[1]
    tm = _tile(T, tm)
    tn = _tile(N, tn)
    b_block = b_block or (K, tn)
    b_map = b_map or (lambda n, i: (0, n))

    def body(*refs):
        a_ref, b_ref = refs[0], refs[1]
        o_ref = refs[-1]
        acc = _dot(_bf(a_ref[...]), _bf(b_ref[...]), 1, 0)
        if res is not None:
            acc = refs[2][...] + acc
        o_ref[...] = acc.astype(out_dtype)

    in_specs = [pl.BlockSpec((tm, K), lambda n, i: (i, 0)), pl.BlockSpec(b_block, b_map)]
    args = [a, b]
    if res is not None:
        in_specs.append(pl.BlockSpec((tm, tn), lambda n, i: (i, n)))
        args.append(res)
    return pl.pallas_call(
        body, name=name, grid=(N // tn, T // tm), in_specs=in_specs,
        out_specs=pl.BlockSpec((tm, tn), lambda n, i: (i, n)),
        out_shape=jax.ShapeDtypeStruct((T, N), out_dtype),
        compiler_params=_params(("parallel", "parallel")),
    )(*args)


def mm_nn_sum(pairs, name, *, tn, out_dtype, tm=512, dep=None):
    T = pairs[0][0].shape[0]
    N = pairs[0][1].shape[1]
    tm = _tile(T, tm)
    tn = _tile(N, tn)
    npair = len(pairs)

    def body(*refs):
        acc = None
        for q in range(npair):
            part = _dot(_bf(refs[2 * q][...]), _bf(refs[2 * q + 1][...]), 1, 0)
            acc = part if acc is None else acc + part
        refs[-1][...] = acc.astype(out_dtype)

    in_specs, args = [], []
    for a, b in pairs:
        K = a.shape[1]
        in_specs += [pl.BlockSpec((tm, K), lambda n, i: (i, 0)), pl.BlockSpec((K, tn), lambda n, i: (0, n))]
        args += [a, b]
    if dep is not None:
        in_specs.append(_dep_spec(2))
        args.append(dep)
    return pl.pallas_call(
        body, name=name, grid=(N // tn, T // tm), in_specs=in_specs,
        out_specs=pl.BlockSpec((tm, tn), lambda n, i: (i, n)),
        out_shape=jax.ShapeDtypeStruct((T, N), out_dtype),
        compiler_params=_params(("parallel", "parallel")),
    )(*args)


def mm_nt(pairs, name, *, tn, out_dtype, tm=512, dep=None):
    T = pairs[0][0].shape[0]
    N = pairs[0][1].shape[0]
    tm = _tile(T, tm)
    tn = _tile(N, tn)
    npair = len(pairs)

    def body(*refs):
        o_ref = refs[-1]
        acc = None
        for q in range(npair):
            part = _dot(_bf(refs[2 * q][...]), _bf(refs[2 * q + 1][...]), 1, 1)
            acc = part if acc is None else acc + part
        o_ref[...] = acc.astype(out_dtype)

    in_specs, args = [], []
    for a, b in pairs:
        K = a.shape[1]
        in_specs += [pl.BlockSpec((tm, K), lambda n, i: (i, 0)), pl.BlockSpec((tn, K), lambda n, i: (n, 0))]
        args += [a, b]
    if dep is not None:
        in_specs.append(_dep_spec(2))
        args.append(dep)
    return pl.pallas_call(
        body, name=name, grid=(N // tn, T // tm), in_specs=in_specs,
        out_specs=pl.BlockSpec((tm, tn), lambda n, i: (i, n)),
        out_shape=jax.ShapeDtypeStruct((T, N), out_dtype),
        compiler_params=_params(("parallel", "parallel")),
    )(*args)


def mm_tn(a, b, name, *, grid, a_block, a_map, b_block, b_map, o_block, o_map, out_shape,
          b_scale=None, dep=None):
    def body(a_ref, b_ref, *rest):
        rest[-1][...] = _dot(_bf(a_ref[...]), _bf(b_ref[...], b_scale), 0, 0).astype(BF16)

    in_specs = [pl.BlockSpec(a_block, a_map), pl.BlockSpec(b_block, b_map)]
    args = [a, b]
    if dep is not None:
        in_specs.append(_dep_spec(len(grid)))
        args.append(dep)
    return pl.pallas_call(
        body, name=name, grid=grid, in_specs=in_specs,
        out_specs=pl.BlockSpec(o_block, o_map),
        out_shape=jax.ShapeDtypeStruct(out_shape, BF16),
        compiler_params=_params(("parallel",) * len(grid)),
    )(*args)


def mm_tn_plain(a, b, name, *, tm=512, tn=512, b_scale=None):
    T, M = a.shape
    N = b.shape[1]
    tm = _tile(M, tm)
    tn = _tile(N, tn)
    return mm_tn(a, b, name, grid=(M // tm, N // tn),
                 a_block=(T, tm), a_map=lambda m, n: (0, m),
                 b_block=(T, tn), b_map=lambda m, n: (0, n),
                 o_block=(tm, tn), o_map=lambda m, n: (m, n),
                 out_shape=(M, N), b_scale=b_scale)


def ffn_up(h, wgu, gi, ui, nper, name):
    T, D = h.shape
    F8 = wgu.shape[0] // (N_DEV * nper)
    tm = _tile(T, 512)
    nt = T // tm

    def body(h_ref, wg_ref, wu_ref, ga_ref, gb_ref, s_ref):
        hv = h_ref[...]
        a = _dot(hv, wg_ref[...], 1, 1)
        b = _dot(hv, wu_ref[...], 1, 1)
        sg = _sigmoid(a)
        silu = a * sg
        ga_ref[...] = (b * (sg * (1.0 + a * (1.0 - sg)))).astype(BF16)
        gb_ref[...] = silu.astype(BF16)
        s_ref[...] = (silu * b).astype(BF16)

    blk = pl.BlockSpec((tm, F8), lambda j, i: (j * nt + i, 0))
    shp = jax.ShapeDtypeStruct((N_DEV * T, F8), BF16)
    return pl.pallas_call(
        body, name=name, grid=(N_DEV, nt),
        in_specs=[pl.BlockSpec((tm, D), lambda j, i: (i, 0)),
                  pl.BlockSpec((F8, D), lambda j, i: (j * nper + gi, 0)),
                  pl.BlockSpec((F8, D), lambda j, i: (j * nper + ui, 0))],
        out_specs=[blk, blk, blk], out_shape=[shp, shp, shp],
        compiler_params=_params(("parallel", "parallel")),
    )(h, wgu, wgu)


def ffn_gate(h, wg, name):
    T, D = h.shape
    F8 = wg.shape[0] // N_DEV
    tm = _tile(T, 512)
    nt = T // tm

    def body(h_ref, wg_ref, a_ref):
        a_ref[...] = _dot(h_ref[...], wg_ref[...], 1, 1).astype(BF16)

    return pl.pallas_call(
        body, name=name, grid=(N_DEV, nt),
        in_specs=[pl.BlockSpec((tm, D), lambda j, i: (i, 0)), pl.BlockSpec((F8, D), lambda j, i: (j, 0))],
        out_specs=pl.BlockSpec((tm, F8), lambda j, i: (j * nt + i, 0)),
        out_shape=jax.ShapeDtypeStruct((N_DEV * T, F8), BF16),
        compiler_params=_params(("parallel", "parallel")),
    )(h, wg)


def ffn_up_gated(h, wu, a, name):
    T, D = h.shape
    F8 = wu.shape[0] // N_DEV
    tm = _tile(T, 512)
    nt = T // tm

    def body(h_ref, wu_ref, a_ref, ga_ref, gb_ref, s_ref):
        av = a_ref[...].astype(F32)
        b = _dot(h_ref[...], wu_ref[...], 1, 1)
        sg = _sigmoid(av)
        silu = av * sg
        ga_ref[...] = (b * (sg * (1.0 + av * (1.0 - sg)))).astype(BF16)
        gb_ref[...] = silu.astype(BF16)
        s_ref[...] = (silu * b).astype(BF16)

    blk = pl.BlockSpec((tm, F8), lambda j, i: (j * nt + i, 0))
    shp = jax.ShapeDtypeStruct((N_DEV * T, F8), BF16)
    return pl.pallas_call(
        body, name=name, grid=(N_DEV, nt),
        in_specs=[pl.BlockSpec((tm, D), lambda j, i: (i, 0)), pl.BlockSpec((F8, D), lambda j, i: (j, 0)), blk],
        out_specs=[blk, blk, blk], out_shape=[shp, shp, shp],
        compiler_params=_params(("parallel", "parallel")),
    )(h, wu, a)


def ffn_down(s, wd, di, nper, x, name):
    T, D = x.shape
    F8 = s.shape[1]
    tm = _tile(T, 512)
    nt = T // tm

    def body(s_ref, w_ref, x_ref, o_ref, acc_ref):
        j = pl.program_id(1)
        part = _dot(s_ref[...], w_ref[...], 1, 0)

        @pl.when(j == 0)
        def _():
            acc_ref[...] = part

        @pl.when(j > 0)
        def _():
            acc_ref[...] += part

        @pl.when(j == N_DEV - 1)
        def _():
            o_ref[...] = x_ref[...] + 0.5 * acc_ref[...]

    return pl.pallas_call(
        body, name=name, grid=(nt, N_DEV),
        in_specs=[pl.BlockSpec((tm, F8), lambda i, j: (j * nt + i, 0)),
                  pl.BlockSpec((F8, D), lambda i, j: (j * nper + di, 0)),
                  pl.BlockSpec((tm, D), lambda i, j: (i, 0))],
        out_specs=pl.BlockSpec((tm, D), lambda i, j: (i, 0)),
        out_shape=jax.ShapeDtypeStruct((T, D), F32),
        scratch_shapes=[pltpu.VMEM((tm, D), F32)],
        compiler_params=_params(("parallel", "arbitrary")),
    )(s, wd, x)


def ffn_bwd_act(dxh, wd, di, nper_d, a, b, name, dep=None):
    T, D = dxh.shape
    F8 = a.shape[1]
    tm = _tile(T, 512)
    nt = T // tm

    def body(dx_ref, w_ref, a_ref, b_ref, *rest):
        da_ref, db_ref = rest[-2], rest[-1]
        ds = _dot(dx_ref[...], w_ref[...], 1, 1)
        da_ref[...] = (ds * a_ref[...].astype(F32)).astype(BF16)
        db_ref[...] = (ds * b_ref[...].astype(F32)).astype(BF16)

    blk = pl.BlockSpec((tm, F8), lambda j, i: (j * nt + i, 0))
    shp = jax.ShapeDtypeStruct((N_DEV * T, F8), BF16)
    in_specs = [pl.BlockSpec((tm, D), lambda j, i: (i, 0)),
                pl.BlockSpec((F8, D), lambda j, i: (j * nper_d + di, 0)), blk, blk]
    args = [dxh, wd, a, b]
    if dep is not None:
        in_specs.append(_dep_spec(2))
        args.append(dep)
    return pl.pallas_call(
        body, name=name, grid=(N_DEV, nt), in_specs=in_specs,
        out_specs=[blk, blk], out_shape=[shp, shp],
        compiler_params=_params(("parallel", "parallel")),
    )(*args)


def ffn_bwd_dh(da, db, wg, wu, gi, ui, nper, D, name, dep=None):
    F8 = da.shape[1]
    T = da.shape[0] // N_DEV
    tm = _tile(T, 512)
    nt = T // tm

    def body(da_ref, db_ref, wg_ref, wu_ref, *rest):
        o_ref, acc_ref = rest[-2], rest[-1]
        j = pl.program_id(1)
        part = _dot(da_ref[...], wg_ref[...], 1, 0) + _dot(db_ref[...], wu_ref[...], 1, 0)

        @pl.when(j == 0)
        def _():
            acc_ref[...] = part

        @pl.when(j > 0)
        def _():
            acc_ref[...] += part

        @pl.when(j == N_DEV - 1)
        def _():
            o_ref[...] = acc_ref[...]

    blk = pl.BlockSpec((tm, F8), lambda i, j: (j * nt + i, 0))
    in_specs = [blk, blk,
                pl.BlockSpec((F8, D), lambda i, j: (j * nper + gi, 0)),
                pl.BlockSpec((F8, D), lambda i, j: (j * nper + ui, 0))]
    args = [da, db, wg, wu]
    if dep is not None:
        in_specs.append(_dep_spec(2))
        args.append(dep)
    return pl.pallas_call(
        body, name=name, grid=(nt, N_DEV), in_specs=in_specs,
        out_specs=pl.BlockSpec((tm, D), lambda i, j: (i, 0)),
        out_shape=jax.ShapeDtypeStruct((T, D), F32),
        scratch_shapes=[pltpu.VMEM((tm, D), F32)],
        compiler_params=_params(("parallel", "arbitrary")),
    )(*args)


def ffn_bwd_dw_in(h, dact, name, dep=None):
    T, D = h.shape
    F8 = dact.shape[1]
    tm = _tile(D, 512)
    return mm_tn(dact, h, name, grid=(N_DEV, D // tm),
                 a_block=(T, F8), a_map=lambda j, m: (j, 0),
                 b_block=(T, tm), b_map=lambda j, m: (0, m),
                 o_block=(F8, tm), o_map=lambda j, m: (j, m),
                 out_shape=(N_DEV * F8, D), dep=dep)


def ffn_bwd_dw_down(s, dx, name):
    T, D = dx.shape
    F8 = s.shape[1]
    tn = _tile(D, 512)
    return mm_tn(s, dx, name, grid=(N_DEV, D // tn),
                 a_block=(T, F8), a_map=lambda j, n: (j, 0),
                 b_block=(T, tn), b_map=lambda j, n: (0, n),
                 o_block=(F8, tn), o_map=lambda j, n: (j, n),
                 out_shape=(N_DEV * F8, D))


def _t5_bucket_np(dist):
    max_exact = N_REL_BUCKETS // 2
    d = np.maximum(dist, 1).astype(np.float64)
    large = max_exact + (np.log(d / max_exact) / math.log(REL_MAX_DISTANCE / max_exact)
                         * (N_REL_BUCKETS - max_exact)).astype(np.int64)
    large32 = max_exact + (np.log(d.astype(np.float32) / np.float32(max_exact))
                           / np.float32(math.log(REL_MAX_DISTANCE / max_exact))
                           * np.float32(N_REL_BUCKETS - max_exact)).astype(np.int64)
    assert np.array_equal(large, large32)
    large = np.minimum(large, N_REL_BUCKETS - 1)
    return np.where(dist < max_exact, dist, large)


def _distance_tables(T, tq):
    dist = np.arange(T)
    mult = np.zeros(T, np.int64)
    for window, dilation in DILATED_PATTERNS:
        mult += ((dist % dilation == 0) & (dist // dilation <= window // dilation)).astype(np.int64)
    logm = np.where(mult > 0, np.log(np.maximum(mult, 1)), NEG_INF).astype(np.float32)
    bucket = _t5_bucket_np(dist).astype(np.int32)
    nkb = T // tq
    k = np.arange(nkb)[:, None, None]
    r = np.arange(tq)[None, :, None]
    c = np.arange(tq)[None, None, :]
    delta = k * tq + r - c
    return bucket, logm, delta


def _tile_buckets(T, tq):
    bucket, logm, delta = _distance_tables(T, tq)
    safe = np.maximum(delta, 0)
    bidx = np.where(delta >= 0, bucket[safe], -1).astype(np.int32)
    logm_t = np.where(delta >= 0, logm[safe], NEG_INF).astype(np.float32)
    present = [sorted(set(np.unique(bidx[k]).tolist()) - {-1}) for k in range(T // tq)]
    return bidx, logm_t, present


def bias_tiles(rel_table, T, tq):
    bidx, logm_t, present = _tile_buckets(T, tq)
    nkb = T // tq

    def body(tab_ref, b_ref, lm_ref, o_ref):
        slot = pl.program_id(0)

        @pl.when(slot == 0)
        def _():
            o_ref[...] = jnp.where(b_ref[...] >= 0, 0.0, NEG_INF)

        @pl.when(slot > 0)
        def _():
            for k in range(nkb):
                bi = b_ref[k]
                acc = lm_ref[k]
                for b in present[k]:
                    acc = acc + jnp.where(bi == b, tab_ref[b, slot - 1], 0.0)
                o_ref[k] = acc

    full = pl.BlockSpec((nkb, tq, tq), lambda s: (0, 0, 0))
    return pl.pallas_call(
        body, name="bias_tiles", grid=(1 + N_DIL,),
        in_specs=[pl.BlockSpec(memory_space=pltpu.SMEM), full, full],
        out_specs=pl.BlockSpec((None, nkb, tq, tq), lambda s: (s, 0, 0, 0)),
        out_shape=jax.ShapeDtypeStruct((1 + N_DIL, nkb, tq, tq), F32),
        compiler_params=_params(("parallel",)),
    )(rel_table, jnp.asarray(bidx), jnp.asarray(logm_t))


def fox_gate_fwd(uf, bf, name):
    T = uf.shape[0]
    tb = _tile(T, 512)

    def body(u_ref, b_ref, c_ref, ct_ref):
        lane = lax.broadcasted_iota(jnp.int32, (1, LANES), 1)
        tri = (lax.broadcasted_iota(jnp.int32, (tb, tb), 0)
               >= lax.broadcasted_iota(jnp.int32, (tb, tb), 1)).astype(F32)
        carry = jnp.zeros((1, LANES), F32)
        for blk in range(T // tb):
            z = u_ref[pl.ds(blk * tb, tb), :] + b_ref[...]
            lf = jnp.minimum(z, 0.0) - jnp.log1p(jnp.exp(-jnp.abs(z)))
            lf = jnp.where(lane < N_FOX, lf, 0.0)
            cb = _dot(tri, lf, 1, 0, precision=lax.Precision.HIGHEST) + carry
            c_ref[pl.ds(blk * tb, tb), :] = cb
            ct_ref[:, pl.ds(blk * tb, tb)] = cb.T
            carry = cb[tb - 1:tb, :]

    return pl.pallas_call(
        body, name=name,
        out_shape=[jax.ShapeDtypeStruct((T, LANES), F32), jax.ShapeDtypeStruct((LANES, T), F32)],
        compiler_params=_params(None),
    )(uf, bf)


def fox_gate_bwd(dct, uf, bf, name):
    T = uf.shape[0]
    tb = _tile(T, 512)

    def body(d_ref, u_ref, b_ref, du_ref, db_ref):
        lane = lax.broadcasted_iota(jnp.int32, (1, LANES), 1)
        triu = (lax.broadcasted_iota(jnp.int32, (tb, tb), 0)
                <= lax.broadcasted_iota(jnp.int32, (tb, tb), 1)).astype(F32)
        carry = jnp.zeros((1, LANES), F32)
        dbv = jnp.zeros((1, LANES), F32)
        for blk in reversed(range(T // tb)):
            dc = d_ref[:, pl.ds(blk * tb, tb)].T
            dlf = _dot(triu, dc, 1, 0, precision=lax.Precision.HIGHEST) + carry
            carry = dlf[0:1, :]
            z = u_ref[pl.ds(blk * tb, tb), :] + b_ref[...]
            dz = jnp.where(lane < N_FOX, dlf * (1.0 - _sigmoid(z)), 0.0)
            du_ref[pl.ds(blk * tb, tb), :] = dz
            dbv = dbv + jnp.sum(dz, axis=0, keepdims=True)
        db_ref[...] = dbv

    return pl.pallas_call(
        body, name=name,
        out_shape=[jax.ShapeDtypeStruct((T, LANES), F32), jax.ShapeDtypeStruct((1, LANES), F32)],
        compiler_params=_params(None),
    )(dct, uf, bf)


def _bias_slot(h):
    return jnp.maximum(h - (N_FOX - 1), 0)


def _scores(q_ref, k_ref, c_ref, ct_ref, tb_ref, h, i, tq, fox):
    scale = HEAD_DIM ** -0.5
    n = (i + 1) * tq
    rows = pl.ds(i * tq, tq)
    s = _dot(q_ref[rows, :], k_ref[pl.ds(0, n), :], 1, 1) * scale
    if not fox:
        return s + jnp.concatenate([tb_ref[i - jb] for jb in range(i + 1)], axis=1)
    lane = lax.broadcasted_iota(jnp.int32, (1, LANES), 1)
    c_col = jnp.sum(jnp.where(lane == h, c_ref[rows, :], 0.0), axis=1, keepdims=True)
    c_row = ct_ref[pl.ds(h, 1), pl.ds(0, n)]
    s = s + (c_col - c_row)
    if i == 0:
        return s + tb_ref[0]
    return jnp.concatenate([s[:, :i * tq], s[:, i * tq:] + tb_ref[0]], axis=1)


def _attn_specs(T, tq):
    nkb = T // tq
    return [
        pl.BlockSpec((T, HEAD_DIM), lambda h: (0, h)),
        pl.BlockSpec((T, HEAD_DIM), lambda h: (0, N_HEADS + h)),
        pl.BlockSpec((T, HEAD_DIM), lambda h: (0, 2 * N_HEADS + h)),
        pl.BlockSpec((T, LANES), lambda h: (0, 0)),
        pl.BlockSpec((LANES, T), lambda h: (0, 0)),
        pl.BlockSpec((None, nkb, tq, tq), lambda h: (_bias_slot(h), 0, 0, 0)),
    ]


def attention_fwd(qkv, c, ct, tiles, name):
    T = qkv.shape[0]
    tq = tiles.shape[2]

    def body(q_ref, k_ref, v_ref, c_ref, ct_ref, tb_ref, o_ref, lse_ref):
        h = pl.program_id(0)
        lane = lax.broadcasted_iota(jnp.int32, (1, LANES), 1)

        @pl.when(h == 0)
        def _():
            lse_ref[...] = jnp.zeros_like(lse_ref)

        def head(fox):
            for i in range(T // tq):
                rows = pl.ds(i * tq, tq)
                s = _scores(q_ref, k_ref, c_ref, ct_ref, tb_ref, h, i, tq, fox)
                m = jnp.max(s, axis=1, keepdims=True)
                p = jnp.exp(s - m)
                l = jnp.sum(p, axis=1, keepdims=True)
                o = _dot(p.astype(BF16), v_ref[pl.ds(0, (i + 1) * tq), :], 1, 0) * (1.0 / l)
                o_ref[rows, :] = o.astype(BF16)
                lse_ref[rows, :] = jnp.where(lane == h, m + jnp.log(l), lse_ref[rows, :])

        pl.when(h < N_FOX)(functools.partial(head, True))
        pl.when(h >= N_FOX)(functools.partial(head, False))

    return pl.pallas_call(
        body, name=name, grid=(N_HEADS,),
        in_specs=_attn_specs(T, tq),
        out_specs=[pl.BlockSpec((T, HEAD_DIM), lambda h: (0, h)), pl.BlockSpec((T, LANES), lambda h: (0, 0))],
        out_shape=[jax.ShapeDtypeStruct((T, N_HEADS * HEAD_DIM), BF16), jax.ShapeDtypeStruct((T, LANES), F32)],
        compiler_params=_params(("arbitrary",)),
    )(qkv, qkv, qkv, c, ct, tiles)


def attention_bwd(qkv, c, ct, tiles, lse, o, do, name):
    T = qkv.shape[0]
    tq = tiles.shape[2]
    nkb = T // tq
    scale = HEAD_DIM ** -0.5

    def body(q_ref, k_ref, v_ref, c_ref, ct_ref, tb_ref, lse_ref, o_ref, do_ref,
             dq_ref, dk_ref, dv_ref, dct_ref, dtb_ref, dk_acc, dv_acc):
        h = pl.program_id(0)
        lane = lax.broadcasted_iota(jnp.int32, (1, LANES), 1)
        dk_acc[...] = jnp.zeros_like(dk_acc)
        dv_acc[...] = jnp.zeros_like(dv_acc)
        dct_ref[...] = jnp.zeros_like(dct_ref)
        dtb_ref[...] = jnp.zeros_like(dtb_ref)

        def head(fox):
            for i in range(nkb):
                rows, keys = pl.ds(i * tq, tq), pl.ds(0, (i + 1) * tq)
                s = _scores(q_ref, k_ref, c_ref, ct_ref, tb_ref, h, i, tq, fox)
                lse_col = jnp.sum(jnp.where(lane == h, lse_ref[rows, :], 0.0), axis=1, keepdims=True)
                p = jnp.exp(s - lse_col)
                p_b = p.astype(BF16)
                dov = do_ref[rows, :]
                dp = _dot(dov, v_ref[keys, :], 1, 1)
                if fox:
                    delta = jnp.sum(p * dp, axis=1, keepdims=True)
                else:
                    delta = jnp.sum(dov.astype(F32) * o_ref[rows, :].astype(F32), axis=1, keepdims=True)
                ds = p * (dp - delta)
                ds_b = ds.astype(BF16)
                dq_ref[rows, :] = (_dot(ds_b, k_ref[keys, :], 1, 0) * scale).astype(BF16)
                dk_acc[:, keys] += _dot(q_ref[rows, :], ds_b, 0, 0) * scale
                dv_acc[:, keys] += _dot(dov, p_b, 0, 0)
                if fox:
                    dct_ref[:, keys] += -jnp.sum(ds, axis=0, keepdims=True)
                else:
                    for jb in range(i + 1):
                        dtb_ref[i - jb] += ds[:, jb * tq:(jb + 1) * tq]

        pl.when(h < N_FOX)(functools.partial(head, True))
        pl.when(h >= N_FOX)(functools.partial(head, False))
        dk_ref[...] = dk_acc[...].T.astype(BF16)
        dv_ref[...] = dv_acc[...].T.astype(BF16)

    head_cols = jax.ShapeDtypeStruct((T, N_HEADS * HEAD_DIM), BF16)
    col = pl.BlockSpec((T, HEAD_DIM), lambda h: (0, h))
    return pl.pallas_call(
        body, name=name, grid=(N_HEADS,),
        in_specs=_attn_specs(T, tq) + [pl.BlockSpec((T, LANES), lambda h: (0, 0)), col, col],
        out_specs=[col, col, col,
                   pl.BlockSpec((None, 1, T), lambda h: (h, 0, 0)),
                   pl.BlockSpec((None, nkb, tq, tq), lambda h: (_bias_slot(h), 0, 0, 0))],
        out_shape=[head_cols, head_cols, head_cols,
                   jax.ShapeDtypeStruct((N_HEADS, 1, T), F32),
                   jax.ShapeDtypeStruct((1 + N_DIL, nkb, tq, tq), F32)],
        scratch_shapes=[pltpu.VMEM((HEAD_DIM, T), F32), pltpu.VMEM((HEAD_DIM, T), F32)],
        compiler_params=_params(("arbitrary",)),
    )(qkv, qkv, qkv, c, ct, tiles, lse, o, do)


def rel_table_grad(dtiles, T, name):
    tq = dtiles.shape[2]
    nkb = T // tq
    bidx, _, present = _tile_buckets(T, tq)

    def body(d_ref, b_ref, o_ref):
        lane = lax.broadcasted_iota(jnp.int32, (1, LANES), 1)
        row = jnp.zeros((1, LANES), F32)
        for k in range(nkb):
            d = d_ref[k]
            bi = b_ref[k]
            for b in present[k]:
                v = jnp.sum(jnp.sum(jnp.where(bi == b, d, 0.0), axis=0, keepdims=True),
                            axis=1, keepdims=True)
                row = row + jnp.where(lane == b, v, 0.0)
        o_ref[...] = row

    return pl.pallas_call(
        body, name=name, grid=(N_DIL,),
        in_specs=[pl.BlockSpec((None, nkb, tq, tq), lambda h: (h + 1, 0, 0, 0)),
                  pl.BlockSpec((nkb, tq, tq), lambda h: (0, 0, 0))],
        out_specs=pl.BlockSpec((None, 1, LANES), lambda h: (h, 0, 0)),
        out_shape=jax.ShapeDtypeStruct((N_DIL, 1, LANES), F32),
        compiler_params=_params(("parallel",)),
    )(dtiles, jnp.asarray(bidx))


def _peer_list():
    x, y, c = lax.axis_index("x"), lax.axis_index("y"), lax.axis_index("c")
    me = 4 * x + 2 * y + c
    peers = []
    for fx in (0, 1):
        for fy in (0, 1):
            for fc in (0, 1):
                if fx or fy or fc:
                    px = 1 - x if fx else x
                    py = 1 - y if fy else y
                    pc = 1 - c if fc else c
                    peers.append(((px, py, pc), 4 * px + 2 * py + pc))
    return me, peers


_HBM = pl.BlockSpec(memory_space=pltpu.HBM)
_SEM = pl.BlockSpec(memory_space=pltpu.SEMAPHORE)
_EFFECT = pltpu.SideEffectType.DATAFLOW_SIDE_EFFECTING
N_PEERS = N_DEV - 1


def _in_hbm(a):
    return pltpu.with_memory_space_constraint(a, pltpu.HBM)


def _exchange_copies(srcs, lands, send_sems, recv_sems, blockwise):
    me, peers = _peer_list()
    sends, recvs = [], []
    for a in range(len(srcs)):
        for k, (dev, idx) in enumerate(peers):
            src = srcs[a].at[idx] if blockwise[a] else srcs[a]
            sends.append(pltpu.make_async_remote_copy(
                src_ref=src, dst_ref=lands[a].at[me], send_sem=send_sems[a].at[k],
                recv_sem=recv_sems[a].at[k], device_id=dev, device_id_type=MESH))
            recvs.append(pltpu.make_async_remote_copy(
                src_ref=src, dst_ref=lands[a].at[idx], send_sem=send_sems[a].at[k],
                recv_sem=recv_sems[a].at[k], device_id=dev, device_id_type=MESH))
    return sends, recvs


def exchange_start(srcs, lands, blockwise, name):
    n = len(srcs)

    def body(*refs):
        src_in, land_in = refs[:n], refs[n:2 * n]
        send_sems, recv_sems = refs[2 * n:3 * n], refs[3 * n:4 * n]
        token = refs[6 * n]
        sends, _ = _exchange_copies(src_in, land_in, send_sems, recv_sems, blockwise)
        for cp in sends:
            cp.start()
        token[...] = jnp.zeros_like(token)

    out_shape = ([pltpu.SemaphoreType.DMA((N_PEERS,))] * (2 * n)
                 + [pltpu.HBM(s.shape, s.dtype) for s in srcs]
                 + [pltpu.HBM(l.shape, l.dtype) for l in lands]
                 + [jax.ShapeDtypeStruct((8, LANES), F32)])
    aliases = {a: 2 * n + a for a in range(2 * n)}
    outs = pl.pallas_call(
        body, name=name, out_shape=out_shape,
        in_specs=[_HBM] * (2 * n),
        out_specs=[_SEM] * (2 * n) + [_HBM] * (2 * n) + [pl.BlockSpec(memory_space=pltpu.VMEM)],
        input_output_aliases=aliases,
        compiler_params=pltpu.CompilerParams(has_side_effects=_EFFECT),
    )(*[_in_hbm(s) for s in srcs], *[_in_hbm(l) for l in lands])
    return (outs[:n], outs[n:2 * n], outs[2 * n:3 * n], outs[3 * n:4 * n], outs[4 * n])


def exchange_wait(send_sems, recv_sems, srcs, lands, blockwise, after, name):
    n = len(srcs)

    def body(*refs):
        src_in, land_in = refs[:n], refs[n:2 * n]
        ss, rs = refs[2 * n:3 * n], refs[3 * n:4 * n]
        sends, recvs = _exchange_copies(src_in, land_in, ss, rs, blockwise)
        for cp in sends:
            cp.wait_send()
        for cp in recvs:
            cp.wait_recv()

    outs = pl.pallas_call(
        body, name=name,
        out_shape=[pltpu.HBM(s.shape, s.dtype) for s in srcs] + [pltpu.HBM(l.shape, l.dtype) for l in lands],
        in_specs=[_HBM] * (2 * n) + [_SEM] * (2 * n) + [pl.BlockSpec(memory_space=pl.ANY)],
        out_specs=[_HBM] * (2 * n),
        input_output_aliases={a: a for a in range(2 * n)},
        compiler_params=pltpu.CompilerParams(has_side_effects=_EFFECT),
    )(*srcs, *lands, *send_sems, *recv_sems, after)
    return outs[n:]


def _landing(own_block, me, slots=N_DEV):
    empty = lax.empty((slots,) + own_block.shape, own_block.dtype)
    return lax.dynamic_update_slice(empty, own_block[None], (me,) + (0,) * own_block.ndim)


N_CHIPS = N_DEV // 2
_CHIP_FLIPS = ((1, 0), (0, 1), (1, 1))


def _xyc():
    return lax.axis_index("x"), lax.axis_index("y"), lax.axis_index("c")


def _other_chips(x, y):
    return [(1 - x if fx else x, 1 - y if fy else y) for fx, fy in _CHIP_FLIPS]


def _remote(src, dst, send_sem, recv_sem, dev):
    return pltpu.make_async_remote_copy(src_ref=src, dst_ref=dst, send_sem=send_sem, recv_sem=recv_sem,
                                        device_id=dev, device_id_type=MESH)


def comm_call(name, bufs, sems_in, sems_out, fn, after=None, want_token=False):
    nb, ni, no = len(bufs), len(sems_in), len(sems_out)
    afters = [] if after is None else (list(after) if isinstance(after, (list, tuple)) else [after])
    na = len(afters)

    def body(*refs):
        buf_refs = refs[:nb]
        sin = refs[nb:nb + ni]
        sout = refs[nb + ni + na:nb + ni + na + no]
        fn(buf_refs, sin, sout)
        if want_token:
            tok = refs[nb + ni + na + no + nb]
            tok[...] = jnp.zeros_like(tok)

    out_shape = list(sems_out) + [pltpu.HBM(b.shape, b.dtype) for b in bufs]
    out_specs = [_SEM] * no + [_HBM] * nb
    if want_token:
        out_shape.append(jax.ShapeDtypeStruct((8, LANES), F32))
        out_specs.append(pl.BlockSpec(memory_space=pltpu.VMEM))
    args = [_in_hbm(b) for b in bufs] + list(sems_in) + afters
    outs = pl.pallas_call(
        body, name=name, out_shape=out_shape,
        in_specs=[_HBM] * nb + [_SEM] * ni + [pl.BlockSpec(memory_space=pl.ANY)] * na,
        out_specs=out_specs, input_output_aliases={a: no + a for a in range(nb)},
        compiler_params=pltpu.CompilerParams(has_side_effects=_EFFECT),
    )(*args)
    return list(outs[:no]), list(outs[no:no + nb]), (outs[no + nb] if want_token else None)


def _dma_sems(*sizes):
    return [pltpu.SemaphoreType.DMA((s,)) for s in sizes]


def gather_start(srcs, lands, name, after=None):
    n = len(srcs)

    def fn(bufs, sin, sout):
        x, y, c = _xyc()
        me = 4 * x + 2 * y + c
        for a in range(n):
            src, land = bufs[a], bufs[n + a]
            send, recv_d, recv_i = sout[3 * a:3 * a + 3]
            _remote(src, land.at[me], send.at[0], recv_d.at[0], (x, y, 1 - c)).start()
            for k, (px, py) in enumerate(_other_chips(x, y)):
                _remote(src, land.at[me], send.at[1 + k], recv_i.at[k], (px, py, c)).start()

    return comm_call(name, list(srcs) + list(lands), [], _dma_sems(4, 1, 3) * n, fn, after=after, want_token=True)


def gather_forward(srcs, lands, recv_i, after, name):
    n = len(srcs)

    def fn(bufs, sin, sout):
        x, y, c = _xyc()
        for a in range(n):
            src, land = bufs[a], bufs[n + a]
            f_send, f_recv = sout[2 * a:2 * a + 2]
            for k, (px, py) in enumerate(_other_chips(x, y)):
                blk = land.at[4 * px + 2 * py + c]
                _remote(src, blk, f_send.at[k], sin[a].at[k], (px, py, c)).wait_recv()
                _remote(blk, blk, f_send.at[k], f_recv.at[k], (x, y, 1 - c)).start()

    sems, bufs, _ = comm_call(name, list(srcs) + list(lands), recv_i, _dma_sems(3, 3) * n, fn, after=after)
    return sems, bufs


def gather_wait(srcs, lands, send, recv_d, f_send, f_recv, after, name):
    n = len(srcs)

    def fn(bufs, sin, sout):
        x, y, c = _xyc()
        sib = (x, y, 1 - c)
        for a in range(n):
            src, land = bufs[a], bufs[n + a]
            s_send, s_recv_d, s_fsend, s_frecv = sin[4 * a:4 * a + 4]
            sib_blk = land.at[4 * x + 2 * y + 1 - c]
            for k in range(4):
                _remote(src, sib_blk, s_send.at[k], s_recv_d.at[0], sib).wait_send()
            _remote(src, sib_blk, s_send.at[0], s_recv_d.at[0], sib).wait_recv()
            for k, (px, py) in enumerate(_other_chips(x, y)):
                cp = _remote(src, land.at[4 * px + 2 * py + 1 - c], s_fsend.at[k], s_frecv.at[k], sib)
                cp.wait_send()
                cp.wait_recv()

    sems_in = []
    for a in range(n):
        sems_in += [send[a], recv_d[a], f_send[a], f_recv[a]]
    _, bufs, _ = comm_call(name, list(srcs) + list(lands), sems_in, [], fn, after=after)
    return bufs[n:]


def scatter_pair_start(src4s, lands, name, after=None):
    n = len(src4s)

    def fn(bufs, sin, sout):
        x, y, c = _xyc()
        for a in range(n):
            _remote(bufs[a].at[:, 1 - c], bufs[n + a], sout[2 * a].at[0], sout[2 * a + 1].at[0],
                    (x, y, 1 - c)).start()

    return comm_call(name, list(src4s) + list(lands), [], _dma_sems(1, 1) * n, fn, after=after, want_token=True)


def scatter_pair_wait(src4s, lands, sems, after, name):
    n = len(src4s)

    def fn(bufs, sin, sout):
        x, y, c = _xyc()
        for a in range(n):
            cp = _remote(bufs[a].at[:, 1 - c], bufs[n + a], sin[2 * a].at[0], sin[2 * a + 1].at[0], (x, y, 1 - c))
            cp.wait_send()
            cp.wait_recv()

    _, bufs, _ = comm_call(name, list(src4s) + list(lands), sems, [], fn, after=after)
    return bufs[:n], bufs[n:]


def _row_tile(R):
    for cand in range(256, 15, -16):
        if R % cand == 0 and R // cand >= 4:
            return cand
    return R


def chip_sum(src4, land, c, name):
    _, _, R, C = src4.shape
    tr = R

    def body(c_ref, a_ref, b_ref, o_ref):
        o_ref[...] = (a_ref[...].astype(F32) + b_ref[...].astype(F32)).astype(BF16)

    grid_spec = pltpu.PrefetchScalarGridSpec(
        num_scalar_prefetch=1, grid=(N_CHIPS, R // tr),
        in_specs=[pl.BlockSpec((None, None, tr, C), lambda q, i, cr: (q, cr[0], i, 0)),
                  pl.BlockSpec((None, tr, C), lambda q, i, cr: (q, i, 0))],
        out_specs=pl.BlockSpec((None, tr, C), lambda q, i, cr: (q, i, 0)))
    return pl.pallas_call(
        body, name=name, grid_spec=grid_spec,
        out_shape=jax.ShapeDtypeStruct((N_CHIPS, R, C), BF16),
        compiler_params=_params(("parallel", "parallel")),
    )(c.reshape(1).astype(jnp.int32), src4, land)


def scatter_chip_start(sums, lands, name):
    n = len(sums)

    def fn(bufs, sin, sout):
        x, y, c = _xyc()
        for a in range(n):
            for k, (px, py) in enumerate(_other_chips(x, y)):
                _remote(bufs[a].at[2 * px + py], bufs[n + a].at[2 * x + y], sout[2 * a].at[k], sout[2 * a + 1].at[k],
                        (px, py, c)).start()

    return comm_call(name, list(sums) + list(lands), [], _dma_sems(3, 3) * n, fn, want_token=True)


def scatter_chip_wait(sums, lands, sems, after, name):
    n = len(sums)

    def fn(bufs, sin, sout):
        x, y, c = _xyc()
        for a in range(n):
            for k, (px, py) in enumerate(_other_chips(x, y)):
                cp = _remote(bufs[a].at[2 * px + py], bufs[n + a].at[2 * px + py], sin[2 * a].at[k],
                             sin[2 * a + 1].at[k], (px, py, c))
                cp.wait_send()
                cp.wait_recv()

    _, bufs, _ = comm_call(name, list(sums) + list(lands), sems, [], fn, after=after)
    return bufs[:n], bufs[n:]


def _adamw_math(w, g, m, v):
    m = ADAM_B1 * m + (1.0 - ADAM_B1) * g
    v = ADAM_B2 * v + (1.0 - ADAM_B2) * (g * g)
    m_hat = m / (1.0 - ADAM_B1 ** ADAM_STEP)
    v_hat = v / (1.0 - ADAM_B2 ** ADAM_STEP)
    delta = -ADAM_LR * (m_hat / (jnp.sqrt(v_hat) + ADAM_EPS) + ADAM_WD * w)
    return delta, m, v


def _sum_partials(p_ref, own_ref, mine):
    own = own_ref[...].astype(F32)
    g = None
    for s in range(p_ref.shape[0]):
        term = jnp.where(mine == s, own, p_ref[s].astype(F32))
        g = term if g is None else g + term
    return g


def adamw_sharded(parts, sums, my_chip, w, m, v, name):
    R, C = w.shape
    S = parts.shape[0]
    tr = _row_tile(R)

    def body(mc_ref, p_ref, o_ref, w_ref, m_ref, v_ref, g_ref, d_ref, nm_ref, nv_ref):
        g = _sum_partials(p_ref, o_ref, mc_ref[0])
        delta, nm, nv = _adamw_math(w_ref[...], g, m_ref[...], v_ref[...])
        g_ref[...] = g
        d_ref[...] = delta
        nm_ref[...] = nm
        nv_ref[...] = nv

    row = pl.BlockSpec((tr, C), lambda i, mc: (i, 0))
    shp = jax.ShapeDtypeStruct((R, C), F32)
    grid_spec = pltpu.PrefetchScalarGridSpec(
        num_scalar_prefetch=1, grid=(R // tr,),
        in_specs=[pl.BlockSpec((S, tr, C), lambda i, mc: (0, i, 0)),
                  pl.BlockSpec((None, tr, C), lambda i, mc: (mc[0], i, 0)), row, row, row],
        out_specs=[row, row, row, row])
    return pl.pallas_call(
        body, name=name, grid_spec=grid_spec, out_shape=[shp, shp, shp, shp],
        compiler_params=_params(("parallel",)),
    )(my_chip.reshape(1).astype(jnp.int32), parts, sums, w, m, v)


def adamw_small(parts, w, m, v, name):
    R, C = w.shape

    def body(p_ref, w_ref, m_ref, v_ref, g_ref, d_ref, nm_ref, nv_ref):
        g = p_ref[0]
        for s in range(1, N_DEV):
            g = g + p_ref[s]
        delta, nm, nv = _adamw_math(w_ref[...], g, m_ref[...], v_ref[...])
        g_ref[...] = g
        d_ref[...] = delta
        nm_ref[...] = nm
        nv_ref[...] = nv

    shp = jax.ShapeDtypeStruct((R, C), F32)
    return pl.pallas_call(
        body, name=name, out_shape=[shp, shp, shp, shp], compiler_params=_params(None),
    )(parts, w, m, v)


_ROW_NORM_FFN1, _ROW_NORM_MIX, _ROW_NORM_FFN2, _ROW_NORM_PLE, _ROW_NORM_FINAL = 0, 1, 2, 3, 4
_ROW_B_F, _ROW_REL, _ROW_LOSS, _SMALL_ROWS = 5, 6, 7, 8


def _pack_small(D, norm_ffn1, norm_mix, norm_ffn2, norm_ple, norm_final, b_f, rel_table):
    def row(v):
        v = v.reshape(1, -1)
        return jnp.pad(v, ((0, 0), (0, D - v.shape[1])))
    return jnp.concatenate([row(norm_ffn1), row(norm_mix), row(norm_ffn2), row(norm_ple),
                            row(norm_final), row(b_f), row(rel_table),
                            jnp.zeros((1, D), F32)], axis=0)


def _unpack_small(a, shapes):
    return {"norm_ffn1": a[_ROW_NORM_FFN1].reshape(shapes["norm_ffn1"]),
            "norm_mix": a[_ROW_NORM_MIX].reshape(shapes["norm_mix"]),
            "b_f": a[_ROW_B_F, :N_FOX].reshape(shapes["b_f"]),
            "norm_ffn2": a[_ROW_NORM_FFN2].reshape(shapes["norm_ffn2"]),
            "norm_ple": a[_ROW_NORM_PLE].reshape(shapes["norm_ple"]),
            "rel_table": a[_ROW_REL, :N_REL_BUCKETS * N_DIL].reshape(shapes["rel_table"]),
            "norm_final": a[_ROW_NORM_FINAL].reshape(shapes["norm_final"])}


def local_step(x, p, tgt, g_ffn1, g_mix, g_ffn2, g_ple, g_final, b_f, rel_table,
               forward, weights, emit, emit2, first_dep):
    T, D = x.shape
    P = p.shape[1]
    CW = D // N_DEV
    tq = _tile(T, 256)

    h1 = rms_fwd(x, g_ffn1, "rms_ffn1", dep=first_dep)
    tiles = bias_tiles(rel_table, T, tq)
    forward("ffn1_g", [tiles, h1])
    wg1, = weights("ffn1_g", h1)
    gate1 = ffn_gate(h1, wg1, "ffn1_gate")
    forward("ffn1_u", gate1)
    wu1, = weights("ffn1_u", gate1)
    a1, b1, s1 = ffn_up_gated(h1, wu1, gate1, "ffn1_up")
    forward("ffn1_d", s1)
    wd1, = weights("ffn1_d", s1)
    x1 = ffn_down(s1, wd1, 0, 1, x, "ffn1_down")

    h2 = rms_fwd(x1, g_mix, "rms_mix")
    forward("mix_in", h2)
    w3, wf = weights("mix_in", h2)
    qkv = mm_nt([(h2, w3)], "mix_qkv", tn=768, out_dtype=BF16)
    uf = mm_nt([(h2, wf)], "mix_forget", tn=LANES, out_dtype=F32)
    bfp = jnp.pad(b_f.reshape(1, N_FOX), ((0, 0), (0, LANES - N_FOX)))
    c, ct = fox_gate_fwd(uf, bfp, "fox_gate")
    cat, lse = attention_fwd(qkv, c, ct, tiles, "attention")
    forward("mix_out", cat)
    wo, = weights("mix_out", cat)
    x2 = mm_nn(cat, wo, "mix_out", tn=512, out_dtype=F32, res=x1)

    h3 = rms_fwd(x2, g_ffn2, "rms_ffn2")
    forward("ffn2_gu", h3)
    wgu2, = weights("ffn2_gu", h3)
    a2, b2, s2 = ffn_up(h3, wgu2, 0, 1, 2, "ffn2_up")
    forward("ffn2_d", s2)
    wd2, = weights("ffn2_d", s2)
    x3 = ffn_down(s2, wd2, 0, 1, x2, "ffn2_down")
    forward("ple", x3)

    h4 = rms_fwd(x3, g_ple, "rms_ple")
    wpg, wpp = weights("ple", h4)
    z = mm_nn(h4, wpg, "ple_gate", tn=512, out_dtype=F32)
    pp = mm_nn(p, wpp, "ple_proj", tn=CW, tm=T, out_dtype=F32, n_out=D,
               b_block=(P, CW), b_map=lambda n, i: (n, 0))
    loss_row, dx4, dg_final, dz, dpp = ple_loss(x3, z, pp, g_final, tgt, "ple_loss")

    grads = {}
    grads["w_ple_proj"] = mm_tn(p, dpp, "ple_proj_dw", grid=(N_DEV,),
                                a_block=(T, P), a_map=lambda n: (0, 0),
                                b_block=(T, CW), b_map=lambda n: (0, n),
                                o_block=(P, CW), o_map=lambda n: (n, 0),
                                out_shape=(N_DEV * P, CW))
    grads["w_ple_gate"] = mm_tn_plain(h4, dz, "ple_gate_dw")
    tok = emit("ple", grads)
    dh4 = mm_nt([(dz, wpg)], "ple_gate_dh", tn=512, out_dtype=F32, dep=tok)
    tok = emit2("ple", dh4)
    dx3, dx3h, dg_ple = rms_bwd(dh4, x3, g_ple, dx4, "rms_ple_bwd", dep=tok)

    da2, db2 = ffn_bwd_act(dx3h, wd2, 0, 1, a2, b2, "ffn2_bwd_act")
    grads["ffn2_w_down"] = ffn_bwd_dw_down(s2, dx3h, "ffn2_down_dw")
    grads["ffn2_w_gate"] = ffn_bwd_dw_in(h3, da2, "ffn2_gate_dw")
    grads["ffn2_w_up"] = ffn_bwd_dw_in(h3, db2, "ffn2_up_dw")
    tok = emit("ffn2", grads)
    dh3 = ffn_bwd_dh(da2, db2, wgu2, wgu2, 0, 1, 2, D, "ffn2_bwd_dh", dep=tok)
    tok = emit2("ffn2", dh3)
    dx2, _, dg_ffn2 = rms_bwd(dh3, x2, g_ffn2, dx3, "rms_ffn2_bwd", dep=tok, half=False)

    dcat = mm_nt([(dx2, wo)], "mix_out_dh", tn=512, out_dtype=BF16)
    grads["w_o"] = mm_tn_plain(cat, dx2, "mix_out_dw")
    dq, dk, dv, dct, dtiles = attention_bwd(qkv, c, ct, tiles, lse, cat, dcat, "attention_bwd")
    dctp = jnp.pad(dct[:, 0, :], ((0, LANES - N_HEADS), (0, 0)))
    duf, dbf = fox_gate_bwd(dctp, uf, bfp, "fox_gate_bwd")
    drel = rel_table_grad(dtiles, T, "rel_table_grad")[:, 0, :N_REL_BUCKETS].T
    du3 = jnp.concatenate([dq, dk, dv], axis=1)
    grads["w3"] = mm_tn_plain(du3, h2, "mix_qkv_dw", tm=768)
    grads["wf"] = mm_tn_plain(duf, h2, "mix_forget_dw", tm=LANES)
    tok = emit("mix", grads)
    dh2 = mm_nn_sum([(du3, w3), (duf, wf)], "mix_in_dh", tn=512, out_dtype=F32, dep=tok)
    tok = emit2("mix", dh2)
    dx1, dx1h, dg_mix = rms_bwd(dh2, x1, g_mix, dx2, "rms_mix_bwd", dep=tok)

    da1, db1 = ffn_bwd_act(dx1h, wd1, 0, 1, a1, b1, "ffn1_bwd_act")
    grads["ffn1_w_down"] = ffn_bwd_dw_down(s1, dx1h, "ffn1_down_dw")
    tok = emit("ffn1_d", grads)
    grads["ffn1_w_gate"] = ffn_bwd_dw_in(h1, da1, "ffn1_gate_dw", dep=tok)
    tok = emit2("ffn1_d", grads["ffn1_w_gate"])
    tok = emit("ffn1_g", grads, after=tok)
    grads["ffn1_w_up"] = ffn_bwd_dw_in(h1, db1, "ffn1_up_dw", dep=tok)
    tok = emit2("ffn1_g", grads["ffn1_w_up"])
    tok = emit("ffn1_u", grads, after=tok)
    dh1 = ffn_bwd_dh(da1, db1, wg1, wu1, 0, 0, 1, D, "ffn1_bwd_dh", dep=tok)
    tok = emit2("ffn1_u", dh1)
    dx0, _, dg_ffn1 = rms_bwd(dh1, x, g_ffn1, dx1, "rms_ffn1_bwd", dep=tok, half=False)

    small = _pack_small(D, dg_ffn1, dg_mix, dg_ffn2, dg_ple, dg_final, dbf[:, :N_FOX], drel)
    small = small.at[_ROW_LOSS, :LANES].set(loss_row[0])
    grads["small"] = small
    emit("small", grads)
    return dx0


def _split_w_in(w_in_t):
    df, dd = N_FOX * HEAD_DIM, N_DIL * HEAD_DIM
    o = np.cumsum([0, df, df, df, N_FOX, dd, dd, dd]).tolist()
    qa, ka, va, f, qb, kb, vb = [w_in_t[o[i]:o[i + 1]] for i in range(7)]
    return jnp.concatenate([qa, qb, ka, kb, va, vb], axis=0), f


def _join_w_in(d3, dfg):
    df, dd = N_FOX * HEAD_DIM, N_DIL * HEAD_DIM
    o = np.cumsum([0, df, dd, df, dd, df, dd]).tolist()
    qa, qb, ka, kb, va, vb = [d3[o[i]:o[i + 1]] for i in range(6)]
    return jnp.concatenate([qa, ka, va, dfg, qb, kb, vb], axis=0)


def rows_to_bf16(a3, name, dep=None):
    R, _, C = a3.shape
    tc = _tile(C, 512)

    def body(a_ref, *rest):
        rest[-1][...] = a_ref[...].astype(BF16)

    in_specs = [pl.BlockSpec((R, None, tc), lambda n: (0, 0, n))]
    args = [a3]
    if dep is not None:
        in_specs.append(_dep_spec(1))
        args.append(dep)
    return pl.pallas_call(
        body, name=name, grid=(C // tc,), in_specs=in_specs,
        out_specs=pl.BlockSpec((R, tc), lambda n: (0, n)),
        out_shape=jax.ShapeDtypeStruct((R, C), BF16),
        compiler_params=_params(("parallel",)),
    )(*args)


def adamw_rows3d(parts, sums, my_chip, w3, m3, v3, name):
    R, _, C = w3.shape
    S = parts.shape[0]
    tc = _tile(C, 256)

    def body(mc_ref, p_ref, o_ref, w_ref, m_ref, v_ref, g_ref, d_ref, nm_ref, nv_ref):
        g = _sum_partials(p_ref, o_ref, mc_ref[0])
        delta, nm, nv = _adamw_math(w_ref[...], g, m_ref[...], v_ref[...])
        g_ref[...] = g
        d_ref[...] = delta
        nm_ref[...] = nm
        nv_ref[...] = nv

    col = pl.BlockSpec((R, None, tc), lambda n, mc: (0, 0, n))
    shp = jax.ShapeDtypeStruct((R, 1, C), F32)
    grid_spec = pltpu.PrefetchScalarGridSpec(
        num_scalar_prefetch=1, grid=(C // tc,),
        in_specs=[pl.BlockSpec((S, R, tc), lambda n, mc: (0, 0, n)),
                  pl.BlockSpec((None, R, tc), lambda n, mc: (mc[0], 0, n)), col, col, col],
        out_specs=[col, col, col, col])
    return pl.pallas_call(
        body, name=name, grid_spec=grid_spec, out_shape=[shp, shp, shp, shp],
        compiler_params=_params(("parallel",)),
    )(my_chip.reshape(1).astype(jnp.int32), parts, sums, w3, m3, v3)


def kernel(x, p, norm_ffn1, ffn1_w_gate, ffn1_w_up, ffn1_w_down, norm_mix, w_in, b_f, w_o, norm_ffn2, ffn2_w_gate, ffn2_w_up, ffn2_w_down, norm_ple, w_ple_gate, w_ple_proj, rel_table, norm_final, loss_target, m_norm_ffn1, m_ffn1_w_gate, m_ffn1_w_up, m_ffn1_w_down, m_norm_mix, m_w_in, m_b_f, m_w_o, m_norm_ffn2, m_ffn2_w_gate, m_ffn2_w_up, m_ffn2_w_down, m_norm_ple, m_w_ple_gate, m_w_ple_proj, m_rel_table, m_norm_final, v_norm_ffn1, v_ffn1_w_gate, v_ffn1_w_up, v_ffn1_w_down, v_norm_mix, v_w_in, v_b_f, v_w_o, v_norm_ffn2, v_ffn2_w_gate, v_ffn2_w_up, v_ffn2_w_down, v_norm_ple, v_w_ple_gate, v_w_ple_proj, v_rel_table, v_norm_final):
    names = ["norm_ffn1", "ffn1_w_gate", "ffn1_w_up", "ffn1_w_down", "norm_mix", "w_in", "b_f", "w_o",
             "norm_ffn2", "ffn2_w_gate", "ffn2_w_up", "ffn2_w_down", "norm_ple", "w_ple_gate",
             "w_ple_proj", "rel_table", "norm_final"]
    w = dict(zip(names, [norm_ffn1, ffn1_w_gate, ffn1_w_up, ffn1_w_down, norm_mix, w_in, b_f, w_o,
                         norm_ffn2, ffn2_w_gate, ffn2_w_up, ffn2_w_down, norm_ple, w_ple_gate,
                         w_ple_proj, rel_table, norm_final]))
    m = dict(zip(names, [m_norm_ffn1, m_ffn1_w_gate, m_ffn1_w_up, m_ffn1_w_down, m_norm_mix, m_w_in,
                         m_b_f, m_w_o, m_norm_ffn2, m_ffn2_w_gate, m_ffn2_w_up, m_ffn2_w_down,
                         m_norm_ple, m_w_ple_gate, m_w_ple_proj, m_rel_table, m_norm_final]))
    v = dict(zip(names, [v_norm_ffn1, v_ffn1_w_gate, v_ffn1_w_up, v_ffn1_w_down, v_norm_mix, v_w_in,
                         v_b_f, v_w_o, v_norm_ffn2, v_ffn2_w_gate, v_ffn2_w_up, v_ffn2_w_down,
                         v_norm_ple, v_w_ple_gate, v_w_ple_proj, v_rel_table, v_norm_final]))
    sharded = ["ffn1_w_gate", "ffn1_w_up", "ffn1_w_down", "w_in", "w_o", "ffn2_w_gate", "ffn2_w_up",
               "ffn2_w_down", "w_ple_gate", "w_ple_proj"]
    small_names = [n for n in names if n not in sharded]

    xs, ps, tgt = x[0], p[0, 0], loss_target[0]
    T, D = xs.shape
    transposed = ("ffn1_w_gate", "ffn1_w_up", "ffn2_w_gate", "ffn2_w_up")

    def view(t, n):
        if n in transposed:
            return t[n][0].T
        if n == "w_in":
            return jnp.transpose(t[n], (2, 0, 1))
        return t[n][0]

    def unview(a, n):
        if n in transposed:
            return a.T.reshape(w[n].shape)
        if n == "w_in":
            return jnp.transpose(a, (1, 2, 0))
        return a.reshape(w[n].shape)

    sh = {n: view(w, n) for n in sharded}
    m_sh = {n: view(m, n) for n in sharded}
    v_sh = {n: view(v, n) for n in sharded}
    F8 = sh["ffn1_w_down"].shape[0]
    WIN8 = sh["w_in"].shape[0]
    me = 4 * lax.axis_index("x") + 2 * lax.axis_index("y") + lax.axis_index("c")

    def start(groups, name, after=None):
        srcs = [s for grp in groups for s in grp]
        sems, bufs, token = gather_start(srcs, [_landing(s, me) for s in srcs], name, after=after)
        return sems, bufs[:len(srcs)], bufs[len(srcs):], token

    cat0 = lambda ns, z: (jnp.concatenate([sh[n] for n in ns], axis=0) + z).astype(BF16)
    sems_a, srcs_a, lands_a, token_a = start(
        [[sh["ffn1_w_gate"].astype(BF16)], [sh["ffn1_w_up"].astype(BF16)], [sh["ffn1_w_down"].astype(BF16)]],
        "gather_start_ffn1")
    zero = token_a[0, 0]
    w_in_bf = rows_to_bf16(sh["w_in"], "w_in_bf16", dep=token_a)
    sems_b, srcs_b, lands_b, g_token = start(
        [[w_in_bf, (sh["w_o"] + zero).astype(BF16)],
         [cat0(["ffn2_w_gate", "ffn2_w_up"], zero), (sh["ffn2_w_down"] + zero).astype(BF16)],
         [(sh["w_ple_gate"] + zero).astype(BF16), (sh["w_ple_proj"] + zero).astype(BF16)]],
        "gather_start_rest", after=token_a)
    order = ["ffn1_g", "ffn1_u", "ffn1_d", "mix_in", "mix_out", "ffn2_gu", "ffn2_d", "ple"]
    group_sizes = [1, 1, 1, 1, 1, 1, 1, 2]
    g_sems, g_srcs, g_lands = sems_a + sems_b, srcs_a + srcs_b, lands_a + lands_b
    g_send, g_recv_d, g_recv_i = g_sems[0::3], g_sems[1::3], g_sems[2::3]
    first = np.cumsum([0] + group_sizes).tolist()
    passed = {}

    def arrays_of(group):
        k = order.index(group)
        return slice(first[k], first[k + 1])

    def forward(group, after):
        sl = arrays_of(group)
        f_sems, bufs = gather_forward(g_srcs[sl], g_lands[sl], g_recv_i[sl], after, "gather_forward_" + group)
        k = len(bufs) // 2
        passed[group] = (f_sems[0::2], f_sems[1::2], bufs[:k], bufs[k:])

    def weights(group, after):
        sl = arrays_of(group)
        f_send, f_recv, srcs, lands = passed[group]
        got = gather_wait(srcs, lands, g_send[sl], g_recv_d[sl], f_send, f_recv, after, "gather_wait_" + group)
        if group in ("ffn1_g", "ffn1_u", "ffn1_d"):
            return (got[0].reshape(N_DEV * F8, D),)
        if group == "ffn2_gu":
            return (got[0].reshape(N_DEV * 2 * F8, D),)
        if group in ("ffn2_d", "mix_out"):
            return (got[0].reshape(-1, D),)
        if group == "ple":
            return got[0].reshape(-1, D), got[1].reshape(-1, got[1].shape[2])
        w3, wf8 = _split_w_in(got[0].reshape(N_DEV * WIN8, D))
        return w3, jnp.pad(wf8, ((0, LANES - N_FOX), (0, 0)))

    scatter_groups = {
        "ple": ["w_ple_gate", "w_ple_proj"],
        "ffn2": ["ffn2_w_gate", "ffn2_w_up", "ffn2_w_down"],
        "mix": ["w_in", "w_o"],
        "ffn1_d": ["ffn1_w_down"],
        "ffn1_g": ["ffn1_w_gate"],
        "ffn1_u": ["ffn1_w_up"],
    }
    x_i, y_i, c_i = _xyc()
    my_chip = 2 * x_i + y_i
    pair_stage, chip_stage, small_stage = {}, {}, {}

    def emit(group, grads, after=None):
        if group == "small":
            src = grads["small"]
            ss, rs, srcs, lands, token = exchange_start([src], [_landing(src, me)], [False], "scatter_start_small")
            small_stage["small"] = (ss, rs, srcs, lands)
            return token
        src4s = []
        for n in scatter_groups[group]:
            if n == "w_in":
                full = _join_w_in(grads["w3"], grads["wf"][:N_FOX])
                src4s.append(full.reshape(N_CHIPS, 2, WIN8, D))
            else:
                src4s.append(grads[n].reshape((N_CHIPS, 2) + sh[n].shape))
        lands = [lax.empty((N_CHIPS,) + s.shape[2:], BF16) for s in src4s]
        sems, bufs, token = scatter_pair_start(src4s, lands, "scatter_pair_start_" + group, after=after)
        k = len(src4s)
        pair_stage[group] = (sems, bufs[:k], bufs[k:])
        return token

    def emit2(group, after):
        sems, src4s, lands = pair_stage[group]
        src4s, lands = scatter_pair_wait(src4s, lands, sems, after, "scatter_pair_wait_" + group)
        sums = [chip_sum(s4, la, c_i, "chip_sum_" + n)
                for s4, la, n in zip(src4s, lands, scatter_groups[group])]
        chip_lands = [lax.empty(s.shape, s.dtype) for s in sums]
        sems, bufs, token = scatter_chip_start(sums, chip_lands, "scatter_chip_start_" + group)
        k = len(sums)
        chip_stage[group] = (sems, bufs[:k], bufs[k:])
        return token

    dx0 = local_step(
        xs, ps, tgt, w["norm_ffn1"], w["norm_mix"], w["norm_ffn2"], w["norm_ple"],
        w["norm_final"].reshape(1, D), w["b_f"], w["rel_table"], forward, weights, emit, emit2, g_token)

    res = {}
    after = dx0
    for group in ["ple", "ffn2", "mix", "ffn1_d", "ffn1_g", "ffn1_u"]:
        sems, sums, chip_lands = chip_stage[group]
        sums, parts = scatter_chip_wait(sums, chip_lands, sems, after, "scatter_chip_wait_" + group)
        for n, part, own in zip(scatter_groups[group], parts, sums):
            update = adamw_rows3d if n == "w_in" else adamw_sharded
            g, d, nm, nv = update(part, own, my_chip, sh[n], m_sh[n], v_sh[n], "adamw_" + n)
            res[n] = tuple(unview(a, n) for a in (g, d, nm, nv))
            after = g
    ss, rs, srcs, lands = small_stage["small"]
    small_parts, = exchange_wait(ss, rs, srcs, lands, [False], after, "scatter_wait_small")
    pack = lambda t: _pack_small(D, t["norm_ffn1"], t["norm_mix"], t["norm_ffn2"], t["norm_ple"],
                                 t["norm_final"], t["b_f"], t["rel_table"])
    gs, ds, ms, vs = adamw_small(small_parts, pack(w), pack(m), pack(v), "adamw_small")
    shapes = {n: w[n].shape for n in small_names}
    unpacked = [_unpack_small(a, shapes) for a in (gs, ds, ms, vs)]
    for n in small_names:
        res[n] = tuple(u[n] for u in unpacked)
    loss = gs[_ROW_LOSS, 0]

    out = [loss, dx0.reshape(x.shape)]
    for k in range(4):
        out += [res[n][k] for n in names]
    return tuple(out)
```

```python
import functools
import math

import numpy as np
import jax
import jax.numpy as jnp
from jax import lax
from jax.experimental import pallas as pl
from jax.experimental.pallas import tpu as pltpu

F32 = jnp.float32
BF16 = jnp.bfloat16

N_DEV = 8
HEAD_DIM = 128
N_FOX = 8
N_DIL = 8
N_HEADS = N_FOX + N_DIL
DILATED_PATTERNS = ((128, 1), (512, 4), (2048, 16))
N_REL_BUCKETS = 32
REL_MAX_DISTANCE = 2048
RMS_EPS = 1e-6
NEG_INF = -1e30
LANES = 128
VMEM_LIMIT = 56 * 1024 * 1024

ADAM_LR = 0.001
ADAM_B1 = 0.9
ADAM_B2 = 0.999
ADAM_EPS = 1e-08
ADAM_WD = 0.01
ADAM_STEP = 10

MESH = pl.DeviceIdType.MESH


def _params(sem):
    return pltpu.CompilerParams(dimension_semantics=sem, vmem_limit_bytes=VMEM_LIMIT)


def _dot(a, b, ca, cb, precision=None):
    return lax.dot_general(a, b, (((ca,), (cb,)), ((), ())),
                           preferred_element_type=F32, precision=precision)


def _sigmoid(z):
    return 1.0 / (1.0 + jnp.exp(-z))


def _tile(n, want):
    t = min(n, want)
    assert n % t == 0, (n, t)
    return t


def _dep_spec(ngrid):
    return pl.BlockSpec((8, LANES), lambda *_: (0, 0))


def rms_fwd(x, g, name, dep=None):
    T, D = x.shape
    tm = _tile(T, 256)

    def body(x_ref, g_ref, *rest):
        h_ref = rest[-1]
        xv = x_ref[...]
        r = lax.rsqrt(jnp.mean(xv * xv, axis=-1, keepdims=True) + RMS_EPS)
        h_ref[...] = (xv * r * g_ref[...]).astype(BF16)

    in_specs = [pl.BlockSpec((tm, D), lambda i: (i, 0)), pl.BlockSpec((1, D), lambda i: (0, 0))]
    args = [x, g]
    if dep is not None:
        in_specs.append(_dep_spec(1))
        args.append(dep)
    return pl.pallas_call(
        body, name=name, grid=(T // tm,), in_specs=in_specs,
        out_specs=pl.BlockSpec((tm, D), lambda i: (i, 0)),
        out_shape=jax.ShapeDtypeStruct((T, D), BF16),
        compiler_params=_params(("parallel",)),
    )(*args)


def rms_bwd(dh, x, g, dres, name, dep=None, half=True):
    T, D = x.shape
    tm = _tile(T, 256)

    def body(dh_ref, x_ref, g_ref, dres_ref, *rest):
        dx_ref, dg_ref = (rest[-3], rest[-1]) if half else (rest[-2], rest[-1])
        i = pl.program_id(0)
        xv = x_ref[...]
        r = lax.rsqrt(jnp.mean(xv * xv, axis=-1, keepdims=True) + RMS_EPS)
        xh = xv * r
        d = dh_ref[...]
        u = d * g_ref[...]
        dx = dres_ref[...] + r * (u - xh * jnp.mean(u * xh, axis=-1, keepdims=True))
        dx_ref[...] = dx
        if half:
            rest[-2][...] = (0.5 * dx).astype(BF16)
        part = jnp.sum(d * xh, axis=0, keepdims=True)

        @pl.when(i == 0)
        def _():
            dg_ref[...] = part

        @pl.when(i > 0)
        def _():
            dg_ref[...] += part

    row = pl.BlockSpec((tm, D), lambda i: (i, 0))
    vec = pl.BlockSpec((1, D), lambda i: (0, 0))
    in_specs = [row, row, vec, row]
    args = [dh, x, g, dres]
    if dep is not None:
        in_specs.append(_dep_spec(1))
        args.append(dep)
    out_specs = [row, row, vec] if half else [row, vec]
    out_shape = [jax.ShapeDtypeStruct((T, D), F32)] + ([jax.ShapeDtypeStruct((T, D), BF16)] if half else [])
    out_shape.append(jax.ShapeDtypeStruct((1, D), F32))
    outs = pl.pallas_call(
        body, name=name, grid=(T // tm,),
        in_specs=in_specs, out_specs=out_specs, out_shape=out_shape,
        compiler_params=_params(("arbitrary",)),
    )(*args)
    return tuple(outs) if half else (outs[0], None, outs[1])


def ple_loss(x, z, pp, g, target, name):
    T, D = x.shape
    tm = _tile(T, 256)

    def body(x_ref, z_ref, p_ref, g_ref, t_ref, loss_ref, dx_ref, dg_ref, dz_ref, dp_ref):
        i = pl.program_id(0)
        gate = _sigmoid(z_ref[...])
        ppv = p_ref[...]
        xv = x_ref[...] + gate * ppv
        gv = g_ref[...]
        r = lax.rsqrt(jnp.mean(xv * xv, axis=-1, keepdims=True) + RMS_EPS)
        xh = xv * r
        e = xh * gv - t_ref[...]
        lpart = 0.5 * jnp.sum(jnp.mean(e * e, axis=-1, keepdims=True), axis=0, keepdims=True)
        lrow = jnp.broadcast_to(lpart, (1, LANES))
        d = e * (1.0 / D)
        u = d * gv
        dx = r * (u - xh * jnp.mean(u * xh, axis=-1, keepdims=True))
        dx_ref[...] = dx
        dz_ref[...] = (dx * ppv * gate * (1.0 - gate)).astype(BF16)
        dp_ref[...] = (dx * gate).astype(BF16)
        part = jnp.sum(d * xh, axis=0, keepdims=True)

        @pl.when(i == 0)
        def _():
            dg_ref[...] = part
            loss_ref[...] = lrow

        @pl.when(i > 0)
        def _():
            dg_ref[...] += part
            loss_ref[...] += lrow

    row = pl.BlockSpec((tm, D), lambda i: (i, 0))
    vec = pl.BlockSpec((1, D), lambda i: (0, 0))
    return pl.pallas_call(
        body, name=name, grid=(T // tm,),
        in_specs=[row, row, row, vec, row],
        out_specs=[pl.BlockSpec((1, LANES), lambda i: (0, 0)), row, vec, row, row],
        out_shape=[jax.ShapeDtypeStruct((1, LANES), F32), jax.ShapeDtypeStruct((T, D), F32),
                   jax.ShapeDtypeStruct((1, D), F32), jax.ShapeDtypeStruct((T, D), BF16),
                   jax.ShapeDtypeStruct((T, D), BF16)],
        compiler_params=_params(("arbitrary",)),
    )(x, z, pp, g, target)


def _bf(v, scale=None):
    if scale is not None:
        v = v * scale
    return v.astype(BF16)


def mm_nn(a, b, name, *, tn, out_dtype, tm=512, n_out=None, b_block=None, b_map=None,
          res=None):
    T, K = a.shape
    N = n_out if n_out is not None else b.shape[1]
    tm = _tile(T, tm)
    tn = _tile(N, tn)
    b_block = b_block or (K, tn)
    b_map = b_map or (lambda n, i: (0, n))

    def body(*refs):
        a_ref, b_ref = refs[0], refs[1]
        o_ref = refs[-1]
        acc = _dot(_bf(a_ref[...]), _bf(b_ref[...]), 1, 0)
        if res is not None:
            acc = refs[2][...] + acc
        o_ref[...] = acc.astype(out_dtype)

    in_specs = [pl.BlockSpec((tm, K), lambda n, i: (i, 0)), pl.BlockSpec(b_block, b_map)]
    args = [a, b]
    if res is not None:
        in_specs.append(pl.BlockSpec((tm, tn), lambda n, i: (i, n)))
        args.append(res)
    return pl.pallas_call(
        body, name=name, grid=(N // tn, T // tm), in_specs=in_specs,
        out_specs=pl.BlockSpec((tm, tn), lambda n, i: (i, n)),
        out_shape=jax.ShapeDtypeStruct((T, N), out_dtype),
        compiler_params=_params(("parallel", "parallel")),
    )(*args)


def mm_nn_sum(pairs, name, *, tn, out_dtype, tm=512, dep=None):
    T = pairs[0][0].shape[0]
    N = pairs[0][1].shape[1]
    tm = _tile(T, tm)
    tn = _tile(N, tn)
    npair = len(pairs)

    def body(*refs):
        acc = None
        for q in range(npair):
            part = _dot(_bf(refs[2 * q][...]), _bf(refs[2 * q + 1][...]), 1, 0)
            acc = part if acc is None else acc + part
        refs[-1][...] = acc.astype(out_dtype)

    in_specs, args = [], []
    for a, b in pairs:
        K = a.shape[1]
        in_specs += [pl.BlockSpec((tm, K), lambda n, i: (i, 0)), pl.BlockSpec((K, tn), lambda n, i: (0, n))]
        args += [a, b]
    if dep is not None:
        in_specs.append(_dep_spec(2))
        args.append(dep)
    return pl.pallas_call(
        body, name=name, grid=(N // tn, T // tm), in_specs=in_specs,
        out_specs=pl.BlockSpec((tm, tn), lambda n, i: (i, n)),
        out_shape=jax.ShapeDtypeStruct((T, N), out_dtype),
        compiler_params=_params(("parallel", "parallel")),
    )(*args)


def mm_nt(pairs, name, *, tn, out_dtype, tm=512, dep=None):
    T = pairs[0][0].shape[0]
    N = pairs[0][1].shape[0]
    tm = _tile(T, tm)
    tn = _tile(N, tn)
    npair = len(pairs)

    def body(*refs):
        o_ref = refs[-1]
        acc = None
        for q in range(npair):
            part = _dot(_bf(refs[2 * q][...]), _bf(refs[2 * q + 1][...]), 1, 1)
            acc = part if acc is None else acc + part
        o_ref[...] = acc.astype(out_dtype)

    in_specs, args = [], []
    for a, b in pairs:
        K = a.shape[1]
        in_specs += [pl.BlockSpec((tm, K), lambda n, i: (i, 0)), pl.BlockSpec((tn, K), lambda n, i: (n, 0))]
        args += [a, b]
    if dep is not None:
        in_specs.append(_dep_spec(2))
        args.append(dep)
    return pl.pallas_call(
        body, name=name, grid=(N // tn, T // tm), in_specs=in_specs,
        out_specs=pl.BlockSpec((tm, tn), lambda n, i: (i, n)),
        out_shape=jax.ShapeDtypeStruct((T, N), out_dtype),
        compiler_params=_params(("parallel", "parallel")),
    )(*args)


def mm_tn(a, b, name, *, grid, a_block, a_map, b_block, b_map, o_block, o_map, out_shape,
          b_scale=None, dep=None):
    def body(a_ref, b_ref, *rest):
        rest[-1][...] = _dot(_bf(a_ref[...]), _bf(b_ref[...], b_scale), 0, 0).astype(BF16)

    in_specs = [pl.BlockSpec(a_block, a_map), pl.BlockSpec(b_block, b_map)]
    args = [a, b]
    if dep is not None:
        in_specs.append(_dep_spec(len(grid)))
        args.append(dep)
    return pl.pallas_call(
        body, name=name, grid=grid, in_specs=in_specs,
        out_specs=pl.BlockSpec(o_block, o_map),
        out_shape=jax.ShapeDtypeStruct(out_shape, BF16),
        compiler_params=_params(("parallel",) * len(grid)),
    )(*args)


def mm_tn_plain(a, b, name, *, tm=512, tn=512, b_scale=None):
    T, M = a.shape
    N = b.shape[1]
    tm = _tile(M, tm)
    tn = _tile(N, tn)
    return mm_tn(a, b, name, grid=(M // tm, N // tn),
                 a_block=(T, tm), a_map=lambda m, n: (0, m),
                 b_block=(T, tn), b_map=lambda m, n: (0, n),
                 o_block=(tm, tn), o_map=lambda m, n: (m, n),
                 out_shape=(M, N), b_scale=b_scale)


def ffn_up(h, wgu, gi, ui, nper, name):
    T, D = h.shape
    F8 = wgu.shape[0] // (N_DEV * nper)
    tm = _tile(T, 512)
    nt = T // tm

    def body(h_ref, wg_ref, wu_ref, ga_ref, gb_ref, s_ref):
        hv = h_ref[...]
        a = _dot(hv, wg_ref[...], 1, 1)
        b = _dot(hv, wu_ref[...], 1, 1)
        sg = _sigmoid(a)
        silu = a * sg
        ga_ref[...] = (b * (sg * (1.0 + a * (1.0 - sg)))).astype(BF16)
        gb_ref[...] = silu.astype(BF16)
        s_ref[...] = (silu * b).astype(BF16)

    blk = pl.BlockSpec((tm, F8), lambda j, i: (j * nt + i, 0))
    shp = jax.ShapeDtypeStruct((N_DEV * T, F8), BF16)
    return pl.pallas_call(
        body, name=name, grid=(N_DEV, nt),
        in_specs=[pl.BlockSpec((tm, D), lambda j, i: (i, 0)),
                  pl.BlockSpec((F8, D), lambda j, i: (j * nper + gi, 0)),
                  pl.BlockSpec((F8, D), lambda j, i: (j * nper + ui, 0))],
        out_specs=[blk, blk, blk], out_shape=[shp, shp, shp],
        compiler_params=_params(("parallel", "parallel")),
    )(h, wgu, wgu)


def ffn_gate(h, wg, name):
    T, D = h.shape
    F8 = wg.shape[0] // N_DEV
    tm = _tile(T, 512)
    nt = T // tm

    def body(h_ref, wg_ref, a_ref):
        a_ref[...] = _dot(h_ref[...], wg_ref[...], 1, 1).astype(BF16)

    return pl.pallas_call(
        body, name=name, grid=(N_DEV, nt),
        in_specs=[pl.BlockSpec((tm, D), lambda j, i: (i, 0)), pl.BlockSpec((F8, D), lambda j, i: (j, 0))],
        out_specs=pl.BlockSpec((tm, F8), lambda j, i: (j * nt + i, 0)),
        out_shape=jax.ShapeDtypeStruct((N_DEV * T, F8), BF16),
        compiler_params=_params(("parallel", "parallel")),
    )(h, wg)


def ffn_up_gated(h, wu, a, name):
    T, D = h.shape
    F8 = wu.shape[0] // N_DEV
    tm = _tile(T, 512)
    nt = T // tm

    def body(h_ref, wu_ref, a_ref, ga_ref, gb_ref, s_ref):
        av = a_ref[...].astype(F32)
        b = _dot(h_ref[...], wu_ref[...], 1, 1)
        sg = _sigmoid(av)
        silu = av * sg
        ga_ref[...] = (b * (sg * (1.0 + av * (1.0 - sg)))).astype(BF16)
        gb_ref[...] = silu.astype(BF16)
        s_ref[...] = (silu * b).astype(BF16)

    blk = pl.BlockSpec((tm, F8), lambda j, i: (j * nt + i, 0))
    shp = jax.ShapeDtypeStruct((N_DEV * T, F8), BF16)
    return pl.pallas_call(
        body, name=name, grid=(N_DEV, nt),
        in_specs=[pl.BlockSpec((tm, D), lambda j, i: (i, 0)), pl.BlockSpec((F8, D), lambda j, i: (j, 0)), blk],
        out_specs=[blk, blk, blk], out_shape=[shp, shp, shp],
        compiler_params=_params(("parallel", "parallel")),
    )(h, wu, a)


def ffn_down(s, wd, di, nper, x, g_next, name):
    T, D = x.shape
    F8 = s.shape[1]
    tm = _tile(T, 512)
    nt = T // tm

    def body(s_ref, w_ref, x_ref, g_ref, o_ref, h_ref, acc_ref):
        j = pl.program_id(1)
        part = _dot(s_ref[...], w_ref[...], 1, 0)

        @pl.when(j == 0)
        def _():
            acc_ref[...] = part

        @pl.when(j > 0)
        def _():
            acc_ref[...] += part

        @pl.when(j == N_DEV - 1)
        def _():
            xv = x_ref[...] + 0.5 * acc_ref[...]
            o_ref[...] = xv
            r = lax.rsqrt(jnp.mean(xv * xv, axis=-1, keepdims=True) + RMS_EPS)
            h_ref[...] = (xv * r * g_ref[...]).astype(BF16)

    row = pl.BlockSpec((tm, D), lambda i, j: (i, 0))
    return pl.pallas_call(
        body, name=name, grid=(nt, N_DEV),
        in_specs=[pl.BlockSpec((tm, F8), lambda i, j: (j * nt + i, 0)),
                  pl.BlockSpec((F8, D), lambda i, j: (j * nper + di, 0)),
                  row, pl.BlockSpec((1, D), lambda i, j: (0, 0))],
        out_specs=[row, row],
        out_shape=[jax.ShapeDtypeStruct((T, D), F32), jax.ShapeDtypeStruct((T, D), BF16)],
        scratch_shapes=[pltpu.VMEM((tm, D), F32)],
        compiler_params=_params(("parallel", "arbitrary")),
    )(s, wd, x, g_next)


def ffn_bwd_act(dxh, wd, di, nper_d, a, b, name, dep=None):
    T, D = dxh.shape
    F8 = a.shape[1]
    tm = _tile(T, 512)
    nt = T // tm

    def body(dx_ref, w_ref, a_ref, b_ref, *rest):
        da_ref, db_ref = rest[-2], rest[-1]
        ds = _dot(dx_ref[...], w_ref[...], 1, 1)
        da_ref[...] = (ds * a_ref[...].astype(F32)).astype(BF16)
        db_ref[...] = (ds * b_ref[...].astype(F32)).astype(BF16)

    blk = pl.BlockSpec((tm, F8), lambda j, i: (j * nt + i, 0))
    shp = jax.ShapeDtypeStruct((N_DEV * T, F8), BF16)
    in_specs = [pl.BlockSpec((tm, D), lambda j, i: (i, 0)),
                pl.BlockSpec((F8, D), lambda j, i: (j * nper_d + di, 0)), blk, blk]
    args = [dxh, wd, a, b]
    if dep is not None:
        in_specs.append(_dep_spec(2))
        args.append(dep)
    return pl.pallas_call(
        body, name=name, grid=(N_DEV, nt), in_specs=in_specs,
        out_specs=[blk, blk], out_shape=[shp, shp],
        compiler_params=_params(("parallel", "parallel")),
    )(*args)


def ffn_bwd_dh(da, db, wg, wu, gi, ui, nper, D, name, dep=None):
    F8 = da.shape[1]
    T = da.shape[0] // N_DEV
    tm = _tile(T, 512)
    nt = T // tm

    def body(da_ref, db_ref, wg_ref, wu_ref, *rest):
        o_ref, acc_ref = rest[-2], rest[-1]
        j = pl.program_id(1)
        part = _dot(da_ref[...], wg_ref[...], 1, 0) + _dot(db_ref[...], wu_ref[...], 1, 0)

        @pl.when(j == 0)
        def _():
            acc_ref[...] = part

        @pl.when(j > 0)
        def _():
            acc_ref[...] += part

        @pl.when(j == N_DEV - 1)
        def _():
            o_ref[...] = acc_ref[...]

    blk = pl.BlockSpec((tm, F8), lambda i, j: (j * nt + i, 0))
    in_specs = [blk, blk,
                pl.BlockSpec((F8, D), lambda i, j: (j * nper + gi, 0)),
                pl.BlockSpec((F8, D), lambda i, j: (j * nper + ui, 0))]
    args = [da, db, wg, wu]
    if dep is not None:
        in_specs.append(_dep_spec(2))
        args.append(dep)
    return pl.pallas_call(
        body, name=name, grid=(nt, N_DEV), in_specs=in_specs,
        out_specs=pl.BlockSpec((tm, D), lambda i, j: (i, 0)),
        out_shape=jax.ShapeDtypeStruct((T, D), F32),
        scratch_shapes=[pltpu.VMEM((tm, D), F32)],
        compiler_params=_params(("parallel", "arbitrary")),
    )(*args)


def ffn_bwd_dw_in(h, dact, name, dep=None):
    T, D = h.shape
    F8 = dact.shape[1]
    tm = _tile(D, 512)
    return mm_tn(dact, h, name, grid=(N_DEV, D // tm),
                 a_block=(T, F8), a_map=lambda j, m: (j, 0),
                 b_block=(T, tm), b_map=lambda j, m: (0, m),
                 o_block=(F8, tm), o_map=lambda j, m: (j, m),
                 out_shape=(N_DEV * F8, D), dep=dep)


def ffn_bwd_dw_down(s, dx, name):
    T, D = dx.shape
    F8 = s.shape[1]
    tn = _tile(D, 512)
    return mm_tn(s, dx, name, grid=(N_DEV, D // tn),
                 a_block=(T, F8), a_map=lambda j, n: (j, 0),
                 b_block=(T, tn), b_map=lambda j, n: (0, n),
                 o_block=(F8, tn), o_map=lambda j, n: (j, n),
                 out_shape=(N_DEV * F8, D))


def _t5_bucket_np(dist):
    max_exact = N_REL_BUCKETS // 2
    d = np.maximum(dist, 1).astype(np.float64)
    large = max_exact + (np.log(d / max_exact) / math.log(REL_MAX_DISTANCE / max_exact)
                         * (N_REL_BUCKETS - max_exact)).astype(np.int64)
    large32 = max_exact + (np.log(d.astype(np.float32) / np.float32(max_exact))
                           / np.float32(math.log(REL_MAX_DISTANCE / max_exact))
                           * np.float32(N_REL_BUCKETS - max_exact)).astype(np.int64)
    assert np.array_equal(large, large32)
    large = np.minimum(large, N_REL_BUCKETS - 1)
    return np.where(dist < max_exact, dist, large)


def _distance_tables(T, tq):
    dist = np.arange(T)
    mult = np.zeros(T, np.int64)
    for window, dilation in DILATED_PATTERNS:
        mult += ((dist % dilation == 0) & (dist // dilation <= window // dilation)).astype(np.int64)
    logm = np.where(mult > 0, np.log(np.maximum(mult, 1)), NEG_INF).astype(np.float32)
    bucket = _t5_bucket_np(dist).astype(np.int32)
    nkb = T // tq
    k = np.arange(nkb)[:, None, None]
    r = np.arange(tq)[None, :, None]
    c = np.arange(tq)[None, None, :]
    delta = k * tq + r - c
    return bucket, logm, delta


def _tile_buckets(T, tq):
    bucket, logm, delta = _distance_tables(T, tq)
    safe = np.maximum(delta, 0)
    bidx = np.where(delta >= 0, bucket[safe], -1).astype(np.int32)
    logm_t = np.where(delta >= 0, logm[safe], NEG_INF).astype(np.float32)
    present = [sorted(set(np.unique(bidx[k]).tolist()) - {-1}) for k in range(T // tq)]
    return bidx, logm_t, present


def bias_tiles(rel_table, T, tq):
    bidx, logm_t, present = _tile_buckets(T, tq)
    nkb = T // tq

    def body(tab_ref, b_ref, lm_ref, o_ref):
        slot = pl.program_id(0)

        @pl.when(slot == 0)
        def _():
            o_ref[...] = jnp.where(b_ref[...] >= 0, 0.0, NEG_INF)

        @pl.when(slot > 0)
        def _():
            for k in range(nkb):
                bi = b_ref[k]
                acc = lm_ref[k]
                for b in present[k]:
                    acc = acc + jnp.where(bi == b, tab_ref[b, slot - 1], 0.0)
                o_ref[k] = acc

    full = pl.BlockSpec((nkb, tq, tq), lambda s: (0, 0, 0))
    return pl.pallas_call(
        body, name="bias_tiles", grid=(1 + N_DIL,),
        in_specs=[pl.BlockSpec(memory_space=pltpu.SMEM), full, full],
        out_specs=pl.BlockSpec((None, nkb, tq, tq), lambda s: (s, 0, 0, 0)),
        out_shape=jax.ShapeDtypeStruct((1 + N_DIL, nkb, tq, tq), F32),
        compiler_params=_params(("parallel",)),
    )(rel_table, jnp.asarray(bidx), jnp.asarray(logm_t))


def fox_gate_fwd(uf, bf, name):
    T = uf.shape[0]
    tb = _tile(T, 512)

    def body(u_ref, b_ref, c_ref, ct_ref):
        lane = lax.broadcasted_iota(jnp.int32, (1, LANES), 1)
        tri = (lax.broadcasted_iota(jnp.int32, (tb, tb), 0)
               >= lax.broadcasted_iota(jnp.int32, (tb, tb), 1)).astype(F32)
        carry = jnp.zeros((1, LANES), F32)
        for blk in range(T // tb):
            z = u_ref[pl.ds(blk * tb, tb), :] + b_ref[...]
            lf = jnp.minimum(z, 0.0) - jnp.log1p(jnp.exp(-jnp.abs(z)))
            lf = jnp.where(lane < N_FOX, lf, 0.0)
            cb = _dot(tri, lf, 1, 0, precision=lax.Precision.HIGHEST) + carry
            c_ref[pl.ds(blk * tb, tb), :] = cb
            ct_ref[:, pl.ds(blk * tb, tb)] = cb.T
            carry = cb[tb - 1:tb, :]

    return pl.pallas_call(
        body, name=name,
        out_shape=[jax.ShapeDtypeStruct((T, LANES), F32), jax.ShapeDtypeStruct((LANES, T), F32)],
        compiler_params=_params(None),
    )(uf, bf)


def fox_gate_bwd(dct, uf, bf, name):
    T = uf.shape[0]
    tb = _tile(T, 512)

    def body(d_ref, u_ref, b_ref, du_ref, db_ref):
        lane = lax.broadcasted_iota(jnp.int32, (1, LANES), 1)
        triu = (lax.broadcasted_iota(jnp.int32, (tb, tb), 0)
                <= lax.broadcasted_iota(jnp.int32, (tb, tb), 1)).astype(F32)
        carry = jnp.zeros((1, LANES), F32)
        dbv = jnp.zeros((1, LANES), F32)
        for blk in reversed(range(T // tb)):
            dc = d_ref[:, pl.ds(blk * tb, tb)].T
            dlf = _dot(triu, dc, 1, 0, precision=lax.Precision.HIGHEST) + carry
            carry = dlf[0:1, :]
            z = u_ref[pl.ds(blk * tb, tb), :] + b_ref[...]
            dz = jnp.where(lane < N_FOX, dlf * (1.0 - _sigmoid(z)), 0.0)
            du_ref[pl.ds(blk * tb, tb), :] = dz
            dbv = dbv + jnp.sum(dz, axis=0, keepdims=True)
        db_ref[...] = dbv

    return pl.pallas_call(
        body, name=name,
        out_shape=[jax.ShapeDtypeStruct((T, LANES), F32), jax.ShapeDtypeStruct((1, LANES), F32)],
        compiler_params=_params(None),
    )(dct, uf, bf)


def _bias_slot(h):
    return jnp.maximum(h - (N_FOX - 1), 0)


def _scores(q_ref, k_ref, c_ref, ct_ref, tb_ref, h, i, tq, fox):
    scale = HEAD_DIM ** -0.5
    n = (i + 1) * tq
    rows = pl.ds(i * tq, tq)
    s = _dot(q_ref[rows, :], k_ref[pl.ds(0, n), :], 1, 1) * scale
    if not fox:
        return s + jnp.concatenate([tb_ref[i - jb] for jb in range(i + 1)], axis=1)
    lane = lax.broadcasted_iota(jnp.int32, (1, LANES), 1)
    c_col = jnp.sum(jnp.where(lane == h, c_ref[rows, :], 0.0), axis=1, keepdims=True)
    c_row = ct_ref[pl.ds(h, 1), pl.ds(0, n)]
    s = s + (c_col - c_row)
    if i == 0:
        return s + tb_ref[0]
    return jnp.concatenate([s[:, :i * tq], s[:, i * tq:] + tb_ref[0]], axis=1)


def _attn_specs(T, tq):
    nkb = T // tq
    return [
        pl.BlockSpec((T, HEAD_DIM), lambda h: (0, h)),
        pl.BlockSpec((T, HEAD_DIM), lambda h: (0, N_HEADS + h)),
        pl.BlockSpec((T, HEAD_DIM), lambda h: (0, 2 * N_HEADS + h)),
        pl.BlockSpec((T, LANES), lambda h: (0, 0)),
        pl.BlockSpec((LANES, T), lambda h: (0, 0)),
        pl.BlockSpec((None, nkb, tq, tq), lambda h: (_bias_slot(h), 0, 0, 0)),
    ]


def attention_fwd(qkv, c, ct, tiles, name):
    T = qkv.shape[0]
    tq = tiles.shape[2]

    def body(q_ref, k_ref, v_ref, c_ref, ct_ref, tb_ref, o_ref, lse_ref):
        h = pl.program_id(0)
        lane = lax.broadcasted_iota(jnp.int32, (1, LANES), 1)

        @pl.when(h == 0)
        def _():
            lse_ref[...] = jnp.zeros_like(lse_ref)

        def head(fox):
            for i in range(T // tq):
                rows = pl.ds(i * tq, tq)
                s = _scores(q_ref, k_ref, c_ref, ct_ref, tb_ref, h, i, tq, fox)
                m = jnp.max(s, axis=1, keepdims=True)
                p = jnp.exp(s - m)
                l = jnp.sum(p, axis=1, keepdims=True)
                o = _dot(p.astype(BF16), v_ref[pl.ds(0, (i + 1) * tq), :], 1, 0) * (1.0 / l)
                o_ref[rows, :] = o.astype(BF16)
                lse_ref[rows, :] = jnp.where(lane == h, m + jnp.log(l), lse_ref[rows, :])

        pl.when(h < N_FOX)(functools.partial(head, True))
        pl.when(h >= N_FOX)(functools.partial(head, False))

    return pl.pallas_call(
        body, name=name, grid=(N_HEADS,),
        in_specs=_attn_specs(T, tq),
        out_specs=[pl.BlockSpec((T, HEAD_DIM), lambda h: (0, h)), pl.BlockSpec((T, LANES), lambda h: (0, 0))],
        out_shape=[jax.ShapeDtypeStruct((T, N_HEADS * HEAD_DIM), BF16), jax.ShapeDtypeStruct((T, LANES), F32)],
        compiler_params=_params(("arbitrary",)),
    )(qkv, qkv, qkv, c, ct, tiles)


def attention_bwd(qkv, c, ct, tiles, lse, o, do, name):
    T = qkv.shape[0]
    tq = tiles.shape[2]
    nkb = T // tq
    scale = HEAD_DIM ** -0.5

    def body(q_ref, k_ref, v_ref, c_ref, ct_ref, tb_ref, lse_ref, o_ref, do_ref,
             dq_ref, dk_ref, dv_ref, dct_ref, dtb_ref, dk_acc, dv_acc):
        h = pl.program_id(0)
        lane = lax.broadcasted_iota(jnp.int32, (1, LANES), 1)
        dk_acc[...] = jnp.zeros_like(dk_acc)
        dv_acc[...] = jnp.zeros_like(dv_acc)
        dct_ref[...] = jnp.zeros_like(dct_ref)
        dtb_ref[...] = jnp.zeros_like(dtb_ref)

        def head(fox):
            for i in range(nkb):
                rows, keys = pl.ds(i * tq, tq), pl.ds(0, (i + 1) * tq)
                s = _scores(q_ref, k_ref, c_ref, ct_ref, tb_ref, h, i, tq, fox)
                lse_col = jnp.sum(jnp.where(lane == h, lse_ref[rows, :], 0.0), axis=1, keepdims=True)
                p = jnp.exp(s - lse_col)
                p_b = p.astype(BF16)
                dov = do_ref[rows, :]
                dp = _dot(dov, v_ref[keys, :], 1, 1)
                if fox:
                    delta = jnp.sum(p * dp, axis=1, keepdims=True)
                else:
                    delta = jnp.sum(dov.astype(F32) * o_ref[rows, :].astype(F32), axis=1, keepdims=True)
                ds = p * (dp - delta)
                ds_b = ds.astype(BF16)
                dq_ref[rows, :] = (_dot(ds_b, k_ref[keys, :], 1, 0) * scale).astype(BF16)
                dk_acc[:, keys] += _dot(q_ref[rows, :], ds_b, 0, 0) * scale
                dv_acc[:, keys] += _dot(dov, p_b, 0, 0)
                if fox:
                    dct_ref[:, keys] += -jnp.sum(ds, axis=0, keepdims=True)
                else:
                    for jb in range(i + 1):
                        dtb_ref[i - jb] += ds[:, jb * tq:(jb + 1) * tq]

        pl.when(h < N_FOX)(functools.partial(head, True))
        pl.when(h >= N_FOX)(functools.partial(head, False))
        dk_ref[...] = dk_acc[...].T.astype(BF16)
        dv_ref[...] = dv_acc[...].T.astype(BF16)

    head_cols = jax.ShapeDtypeStruct((T, N_HEADS * HEAD_DIM), BF16)
    col = pl.BlockSpec((T, HEAD_DIM), lambda h: (0, h))
    return pl.pallas_call(
        body, name=name, grid=(N_HEADS,),
        in_specs=_attn_specs(T, tq) + [pl.BlockSpec((T, LANES), lambda h: (0, 0)), col, col],
        out_specs=[col, col, col,
                   pl.BlockSpec((None, 1, T), lambda h: (h, 0, 0)),
                   pl.BlockSpec((None, nkb, tq, tq), lambda h: (_bias_slot(h), 0, 0, 0))],
        out_shape=[head_cols, head_cols, head_cols,
                   jax.ShapeDtypeStruct((N_HEADS, 1, T), F32),
                   jax.ShapeDtypeStruct((1 + N_DIL, nkb, tq, tq), F32)],
        scratch_shapes=[pltpu.VMEM((HEAD_DIM, T), F32), pltpu.VMEM((HEAD_DIM, T), F32)],
        compiler_params=_params(("arbitrary",)),
    )(qkv, qkv, qkv, c, ct, tiles, lse, o, do)


def rel_table_grad(dtiles, T, name):
    tq = dtiles.shape[2]
    nkb = T // tq
    bidx, _, present = _tile_buckets(T, tq)

    def body(d_ref, b_ref, o_ref):
        lane = lax.broadcasted_iota(jnp.int32, (1, LANES), 1)
        row = jnp.zeros((1, LANES), F32)
        for k in range(nkb):
            d = d_ref[k]
            bi = b_ref[k]
            for b in present[k]:
                v = jnp.sum(jnp.sum(jnp.where(bi == b, d, 0.0), axis=0, keepdims=True),
                            axis=1, keepdims=True)
                row = row + jnp.where(lane == b, v, 0.0)
        o_ref[...] = row

    return pl.pallas_call(
        body, name=name, grid=(N_DIL,),
        in_specs=[pl.BlockSpec((None, nkb, tq, tq), lambda h: (h + 1, 0, 0, 0)),
                  pl.BlockSpec((nkb, tq, tq), lambda h: (0, 0, 0))],
        out_specs=pl.BlockSpec((None, 1, LANES), lambda h: (h, 0, 0)),
        out_shape=jax.ShapeDtypeStruct((N_DIL, 1, LANES), F32),
        compiler_params=_params(("parallel",)),
    )(dtiles, jnp.asarray(bidx))


def _peer_list():
    x, y, c = lax.axis_index("x"), lax.axis_index("y"), lax.axis_index("c")
    me = 4 * x + 2 * y + c
    peers = []
    for fx in (0, 1):
        for fy in (0, 1):
            for fc in (0, 1):
                if fx or fy or fc:
                    px = 1 - x if fx else x
                    py = 1 - y if fy else y
                    pc = 1 - c if fc else c
                    peers.append(((px, py, pc), 4 * px + 2 * py + pc))
    return me, peers


_HBM = pl.BlockSpec(memory_space=pltpu.HBM)
_SEM = pl.BlockSpec(memory_space=pltpu.SEMAPHORE)
_EFFECT = pltpu.SideEffectType.DATAFLOW_SIDE_EFFECTING
N_PEERS = N_DEV - 1


def _in_hbm(a):
    return pltpu.with_memory_space_constraint(a, pltpu.HBM)


def _exchange_copies(srcs, lands, send_sems, recv_sems, blockwise):
    me, peers = _peer_list()
    sends, recvs = [], []
    for a in range(len(srcs)):
        for k, (dev, idx) in enumerate(peers):
            src = srcs[a].at[idx] if blockwise[a] else srcs[a]
            sends.append(pltpu.make_async_remote_copy(
                src_ref=src, dst_ref=lands[a].at[me], send_sem=send_sems[a].at[k],
                recv_sem=recv_sems[a].at[k], device_id=dev, device_id_type=MESH))
            recvs.append(pltpu.make_async_remote_copy(
                src_ref=src, dst_ref=lands[a].at[idx], send_sem=send_sems[a].at[k],
                recv_sem=recv_sems[a].at[k], device_id=dev, device_id_type=MESH))
    return sends, recvs


def exchange_start(srcs, lands, blockwise, name):
    n = len(srcs)

    def body(*refs):
        src_in, land_in = refs[:n], refs[n:2 * n]
        send_sems, recv_sems = refs[2 * n:3 * n], refs[3 * n:4 * n]
        token = refs[6 * n]
        sends, _ = _exchange_copies(src_in, land_in, send_sems, recv_sems, blockwise)
        for cp in sends:
            cp.start()
        token[...] = jnp.zeros_like(token)

    out_shape = ([pltpu.SemaphoreType.DMA((N_PEERS,))] * (2 * n)
                 + [pltpu.HBM(s.shape, s.dtype) for s in srcs]
                 + [pltpu.HBM(l.shape, l.dtype) for l in lands]
                 + [jax.ShapeDtypeStruct((8, LANES), F32)])
    aliases = {a: 2 * n + a for a in range(2 * n)}
    outs = pl.pallas_call(
        body, name=name, out_shape=out_shape,
        in_specs=[_HBM] * (2 * n),
        out_specs=[_SEM] * (2 * n) + [_HBM] * (2 * n) + [pl.BlockSpec(memory_space=pltpu.VMEM)],
        input_output_aliases=aliases,
        compiler_params=pltpu.CompilerParams(has_side_effects=_EFFECT),
    )(*[_in_hbm(s) for s in srcs], *[_in_hbm(l) for l in lands])
    return (outs[:n], outs[n:2 * n], outs[2 * n:3 * n], outs[3 * n:4 * n], outs[4 * n])


def exchange_wait(send_sems, recv_sems, srcs, lands, blockwise, after, name):
    n = len(srcs)

    def body(*refs):
        src_in, land_in = refs[:n], refs[n:2 * n]
        ss, rs = refs[2 * n:3 * n], refs[3 * n:4 * n]
        sends, recvs = _exchange_copies(src_in, land_in, ss, rs, blockwise)
        for cp in sends:
            cp.wait_send()
        for cp in recvs:
            cp.wait_recv()

    outs = pl.pallas_call(
        body, name=name,
        out_shape=[pltpu.HBM(s.shape, s.dtype) for s in srcs] + [pltpu.HBM(l.shape, l.dtype) for l in lands],
        in_specs=[_HBM] * (2 * n) + [_SEM] * (2 * n) + [pl.BlockSpec(memory_space=pl.ANY)],
        out_specs=[_HBM] * (2 * n),
        input_output_aliases={a: a for a in range(2 * n)},
        compiler_params=pltpu.CompilerParams(has_side_effects=_EFFECT),
    )(*srcs, *lands, *send_sems, *recv_sems, after)
    return outs[n:]


def _landing(own_block, me, slots=N_DEV):
    empty = lax.empty((slots,) + own_block.shape, own_block.dtype)
    return lax.dynamic_update_slice(empty, own_block[None], (me,) + (0,) * own_block.ndim)


N_CHIPS = N_DEV // 2
_CHIP_FLIPS = ((1, 0), (0, 1), (1, 1))


def _xyc():
    return lax.axis_index("x"), lax.axis_index("y"), lax.axis_index("c")


def _other_chips(x, y):
    return [(1 - x if fx else x, 1 - y if fy else y) for fx, fy in _CHIP_FLIPS]


def _remote(src, dst, send_sem, recv_sem, dev):
    return pltpu.make_async_remote_copy(src_ref=src, dst_ref=dst, send_sem=send_sem, recv_sem=recv_sem,
                                        device_id=dev, device_id_type=MESH)


def comm_call(name, bufs, sems_in, sems_out, fn, after=None, want_token=False):
    nb, ni, no = len(bufs), len(sems_in), len(sems_out)
    afters = [] if after is None else (list(after) if isinstance(after, (list, tuple)) else [after])
    na = len(afters)

    def body(*refs):
        buf_refs = refs[:nb]
        sin = refs[nb:nb + ni]
        sout = refs[nb + ni + na:nb + ni + na + no]
        fn(buf_refs, sin, sout)
        if want_token:
            tok = refs[nb + ni + na + no + nb]
            tok[...] = jnp.zeros_like(tok)

    out_shape = list(sems_out) + [pltpu.HBM(b.shape, b.dtype) for b in bufs]
    out_specs = [_SEM] * no + [_HBM] * nb
    if want_token:
        out_shape.append(jax.ShapeDtypeStruct((8, LANES), F32))
        out_specs.append(pl.BlockSpec(memory_space=pltpu.VMEM))
    args = [_in_hbm(b) for b in bufs] + list(sems_in) + afters
    outs = pl.pallas_call(
        body, name=name, out_shape=out_shape,
        in_specs=[_HBM] * nb + [_SEM] * ni + [pl.BlockSpec(memory_space=pl.ANY)] * na,
        out_specs=out_specs, input_output_aliases={a: no + a for a in range(nb)},
        compiler_params=pltpu.CompilerParams(has_side_effects=_EFFECT),
    )(*args)
    return list(outs[:no]), list(outs[no:no + nb]), (outs[no + nb] if want_token else None)


def _dma_sems(*sizes):
    return [pltpu.SemaphoreType.DMA((s,)) for s in sizes]


def gather_start(srcs, lands, name, after=None):
    n = len(srcs)

    def fn(bufs, sin, sout):
        x, y, c = _xyc()
        me = 4 * x + 2 * y + c
        for a in range(n):
            src, land = bufs[a], bufs[n + a]
            send, recv_d, recv_i = sout[3 * a:3 * a + 3]
            _remote(src, land.at[me], send.at[0], recv_d.at[0], (x, y, 1 - c)).start()
            for k, (px, py) in enumerate(_other_chips(x, y)):
                _remote(src, land.at[me], send.at[1 + k], recv_i.at[k], (px, py, c)).start()

    return comm_call(name, list(srcs) + list(lands), [], _dma_sems(4, 1, 3) * n, fn, after=after, want_token=True)


def gather_forward(srcs, lands, recv_i, after, name):
    n = len(srcs)

    def fn(bufs, sin, sout):
        x, y, c = _xyc()
        for a in range(n):
            src, land = bufs[a], bufs[n + a]
            f_send, f_recv = sout[2 * a:2 * a + 2]
            for k, (px, py) in enumerate(_other_chips(x, y)):
                blk = land.at[4 * px + 2 * py + c]
                _remote(src, blk, f_send.at[k], sin[a].at[k], (px, py, c)).wait_recv()
                _remote(blk, blk, f_send.at[k], f_recv.at[k], (x, y, 1 - c)).start()

    sems, bufs, _ = comm_call(name, list(srcs) + list(lands), recv_i, _dma_sems(3, 3) * n, fn, after=after)
    return sems, bufs


def gather_wait(srcs, lands, send, recv_d, f_send, f_recv, after, name):
    n = len(srcs)

    def fn(bufs, sin, sout):
        x, y, c = _xyc()
        sib = (x, y, 1 - c)
        for a in range(n):
            src, land = bufs[a], bufs[n + a]
            s_send, s_recv_d, s_fsend, s_frecv = sin[4 * a:4 * a + 4]
            sib_blk = land.at[4 * x + 2 * y + 1 - c]
            for k in range(4):
                _remote(src, sib_blk, s_send.at[k], s_recv_d.at[0], sib).wait_send()
            _remote(src, sib_blk, s_send.at[0], s_recv_d.at[0], sib).wait_recv()
            for k, (px, py) in enumerate(_other_chips(x, y)):
                cp = _remote(src, land.at[4 * px + 2 * py + 1 - c], s_fsend.at[k], s_frecv.at[k], sib)
                cp.wait_send()
                cp.wait_recv()

    sems_in = []
    for a in range(n):
        sems_in += [send[a], recv_d[a], f_send[a], f_recv[a]]
    _, bufs, _ = comm_call(name, list(srcs) + list(lands), sems_in, [], fn, after=after)
    return bufs[n:]


def scatter_pair_start(src4s, lands, name, after=None):
    n = len(src4s)

    def fn(bufs, sin, sout):
        x, y, c = _xyc()
        for a in range(n):
            _remote(bufs[a].at[:, 1 - c], bufs[n + a], sout[2 * a].at[0], sout[2 * a + 1].at[0],
                    (x, y, 1 - c)).start()

    return comm_call(name, list(src4s) + list(lands), [], _dma_sems(1, 1) * n, fn, after=after, want_token=True)


def scatter_pair_wait(src4s, lands, sems, after, name):
    n = len(src4s)

    def fn(bufs, sin, sout):
        x, y, c = _xyc()
        for a in range(n):
            cp = _remote(bufs[a].at[:, 1 - c], bufs[n + a], sin[2 * a].at[0], sin[2 * a + 1].at[0], (x, y, 1 - c))
            cp.wait_send()
            cp.wait_recv()

    _, bufs, _ = comm_call(name, list(src4s) + list(lands), sems, [], fn, after=after)
    return bufs[:n], bufs[n:]


def _row_tile(R):
    for cand in range(256, 15, -16):
        if R % cand == 0 and R // cand >= 4:
            return cand
    return R


def chip_sum(src4, land, c, name):
    _, _, R, C = src4.shape
    tr = R

    def body(c_ref, a_ref, b_ref, o_ref):
        o_ref[...] = (a_ref[...].astype(F32) + b_ref[...].astype(F32)).astype(BF16)

    grid_spec = pltpu.PrefetchScalarGridSpec(
        num_scalar_prefetch=1, grid=(N_CHIPS, R // tr),
        in_specs=[pl.BlockSpec((None, None, tr, C), lambda q, i, cr: (q, cr[0], i, 0)),
                  pl.BlockSpec((None, tr, C), lambda q, i, cr: (q, i, 0))],
        out_specs=pl.BlockSpec((None, tr, C), lambda q, i, cr: (q, i, 0)))
    return pl.pallas_call(
        body, name=name, grid_spec=grid_spec,
        out_shape=jax.ShapeDtypeStruct((N_CHIPS, R, C), BF16),
        compiler_params=_params(("parallel", "parallel")),
    )(c.reshape(1).astype(jnp.int32), src4, land)


def scatter_chip_start(sums, lands, name):
    n = len(sums)

    def fn(bufs, sin, sout):
        x, y, c = _xyc()
        for a in range(n):
            for k, (px, py) in enumerate(_other_chips(x, y)):
                _remote(bufs[a].at[2 * px + py], bufs[n + a].at[2 * x + y], sout[2 * a].at[k], sout[2 * a + 1].at[k],
                        (px, py, c)).start()

    return comm_call(name, list(sums) + list(lands), [], _dma_sems(3, 3) * n, fn, want_token=True)


def scatter_chip_wait(sums, lands, sems, after, name):
    n = len(sums)

    def fn(bufs, sin, sout):
        x, y, c = _xyc()
        for a in range(n):
            for k, (px, py) in enumerate(_other_chips(x, y)):
                cp = _remote(bufs[a].at[2 * px + py], bufs[n + a].at[2 * px + py], sin[2 * a].at[k],
                             sin[2 * a + 1].at[k], (px, py, c))
                cp.wait_send()
                cp.wait_recv()

    _, bufs, _ = comm_call(name, list(sums) + list(lands), sems, [], fn, after=after)
    return bufs[:n], bufs[n:]


def _adamw_math(w, g, m, v):
    m = ADAM_B1 * m + (1.0 - ADAM_B1) * g
    v = ADAM_B2 * v + (1.0 - ADAM_B2) * (g * g)
    m_hat = m / (1.0 - ADAM_B1 ** ADAM_STEP)
    v_hat = v / (1.0 - ADAM_B2 ** ADAM_STEP)
    delta = -ADAM_LR * (m_hat / (jnp.sqrt(v_hat) + ADAM_EPS) + ADAM_WD * w)
    return delta, m, v


def _sum_partials(p_ref, own_ref, mine):
    own = own_ref[...].astype(F32)
    g = None
    for s in range(p_ref.shape[0]):
        term = jnp.where(mine == s, own, p_ref[s].astype(F32))
        g = term if g is None else g + term
    return g


def adamw_sharded(parts, sums, my_chip, w, m, v, name):
    R, C = w.shape
    S = parts.shape[0]
    tr = _row_tile(R)

    def body(mc_ref, p_ref, o_ref, w_ref, m_ref, v_ref, g_ref, d_ref, nm_ref, nv_ref):
        g = _sum_partials(p_ref, o_ref, mc_ref[0])
        delta, nm, nv = _adamw_math(w_ref[...], g, m_ref[...], v_ref[...])
        g_ref[...] = g
        d_ref[...] = delta
        nm_ref[...] = nm
        nv_ref[...] = nv

    row = pl.BlockSpec((tr, C), lambda i, mc: (i, 0))
    shp = jax.ShapeDtypeStruct((R, C), F32)
    grid_spec = pltpu.PrefetchScalarGridSpec(
        num_scalar_prefetch=1, grid=(R // tr,),
        in_specs=[pl.BlockSpec((S, tr, C), lambda i, mc: (0, i, 0)),
                  pl.BlockSpec((None, tr, C), lambda i, mc: (mc[0], i, 0)), row, row, row],
        out_specs=[row, row, row, row])
    return pl.pallas_call(
        body, name=name, grid_spec=grid_spec, out_shape=[shp, shp, shp, shp],
        compiler_params=_params(("parallel",)),
    )(my_chip.reshape(1).astype(jnp.int32), parts, sums, w, m, v)


def adamw_small(parts, w, m, v, name):
    R, C = w.shape

    def body(p_ref, w_ref, m_ref, v_ref, g_ref, d_ref, nm_ref, nv_ref):
        g = p_ref[0]
        for s in range(1, N_DEV):
            g = g + p_ref[s]
        delta, nm, nv = _adamw_math(w_ref[...], g, m_ref[...], v_ref[...])
        g_ref[...] = g
        d_ref[...] = delta
        nm_ref[...] = nm
        nv_ref[...] = nv

    shp = jax.ShapeDtypeStruct((R, C), F32)
    return pl.pallas_call(
        body, name=name, out_shape=[shp, shp, shp, shp], compiler_params=_params(None),
    )(parts, w, m, v)


_ROW_NORM_FFN1, _ROW_NORM_MIX, _ROW_NORM_FFN2, _ROW_NORM_PLE, _ROW_NORM_FINAL = 0, 1, 2, 3, 4
_ROW_B_F, _ROW_REL, _ROW_LOSS, _SMALL_ROWS = 5, 6, 7, 8


def _pack_small(D, norm_ffn1, norm_mix, norm_ffn2, norm_ple, norm_final, b_f, rel_table):
    def row(v):
        v = v.reshape(1, -1)
        return jnp.pad(v, ((0, 0), (0, D - v.shape[1])))
    return jnp.concatenate([row(norm_ffn1), row(norm_mix), row(norm_ffn2), row(norm_ple),
                            row(norm_final), row(b_f), row(rel_table),
                            jnp.zeros((1, D), F32)], axis=0)


def _unpack_small(a, shapes):
    return {"norm_ffn1": a[_ROW_NORM_FFN1].reshape(shapes["norm_ffn1"]),
            "norm_mix": a[_ROW_NORM_MIX].reshape(shapes["norm_mix"]),
            "b_f": a[_ROW_B_F, :N_FOX].reshape(shapes["b_f"]),
            "norm_ffn2": a[_ROW_NORM_FFN2].reshape(shapes["norm_ffn2"]),
            "norm_ple": a[_ROW_NORM_PLE].reshape(shapes["norm_ple"]),
            "rel_table": a[_ROW_REL, :N_REL_BUCKETS * N_DIL].reshape(shapes["rel_table"]),
            "norm_final": a[_ROW_NORM_FINAL].reshape(shapes["norm_final"])}


def local_step(x, p, tgt, g_ffn1, g_mix, g_ffn2, g_ple, g_final, b_f, rel_table,
               forward, weights, emit, emit2, first_dep):
    T, D = x.shape
    P = p.shape[1]
    CW = D // N_DEV
    tq = _tile(T, 256)

    h1 = rms_fwd(x, g_ffn1, "rms_ffn1", dep=first_dep)
    tiles = bias_tiles(rel_table, T, tq)
    forward("ffn1_g", [tiles, h1])
    wg1, = weights("ffn1_g", h1)
    gate1 = ffn_gate(h1, wg1, "ffn1_gate")
    forward("ffn1_u", gate1)
    wu1, = weights("ffn1_u", gate1)
    a1, b1, s1 = ffn_up_gated(h1, wu1, gate1, "ffn1_up")
    forward("ffn1_d", s1)
    wd1, = weights("ffn1_d", s1)
    x1, h2 = ffn_down(s1, wd1, 0, 1, x, g_mix, "ffn1_down")

    forward("mix_in", h2)
    w3, wf = weights("mix_in", h2)
    qkv = mm_nt([(h2, w3)], "mix_qkv", tn=768, out_dtype=BF16)
    uf = mm_nt([(h2, wf)], "mix_forget", tn=LANES, out_dtype=F32)
    bfp = jnp.pad(b_f.reshape(1, N_FOX), ((0, 0), (0, LANES - N_FOX)))
    c, ct = fox_gate_fwd(uf, bfp, "fox_gate")
    cat, lse = attention_fwd(qkv, c, ct, tiles, "attention")
    forward("mix_out", cat)
    wo, = weights("mix_out", cat)
    x2 = mm_nn(cat, wo, "mix_out", tn=512, out_dtype=F32, res=x1)

    h3 = rms_fwd(x2, g_ffn2, "rms_ffn2")
    forward("ffn2_gu", h3)
    wgu2, = weights("ffn2_gu", h3)
    a2, b2, s2 = ffn_up(h3, wgu2, 0, 1, 2, "ffn2_up")
    forward("ffn2_d", s2)
    wd2, = weights("ffn2_d", s2)
    x3, h4 = ffn_down(s2, wd2, 0, 1, x2, g_ple, "ffn2_down")
    forward("ple", x3)

    wpg, wpp = weights("ple", h4)
    z = mm_nn(h4, wpg, "ple_gate", tn=512, out_dtype=F32)
    pp = mm_nn(p, wpp, "ple_proj", tn=CW, tm=T, out_dtype=F32, n_out=D,
               b_block=(P, CW), b_map=lambda n, i: (n, 0))
    loss_row, dx4, dg_final, dz, dpp = ple_loss(x3, z, pp, g_final, tgt, "ple_loss")

    grads = {}
    grads["w_ple_proj"] = mm_tn(p, dpp, "ple_proj_dw", grid=(N_DEV,),
                                a_block=(T, P), a_map=lambda n: (0, 0),
                                b_block=(T, CW), b_map=lambda n: (0, n),
                                o_block=(P, CW), o_map=lambda n: (n, 0),
                                out_shape=(N_DEV * P, CW))
    grads["w_ple_gate"] = mm_tn_plain(h4, dz, "ple_gate_dw")
    tok = emit("ple", grads)
    dh4 = mm_nt([(dz, wpg)], "ple_gate_dh", tn=512, out_dtype=F32, dep=tok)
    tok = emit2("ple", dh4)
    dx3, dx3h, dg_ple = rms_bwd(dh4, x3, g_ple, dx4, "rms_ple_bwd", dep=tok)

    da2, db2 = ffn_bwd_act(dx3h, wd2, 0, 1, a2, b2, "ffn2_bwd_act")
    grads["ffn2_w_down"] = ffn_bwd_dw_down(s2, dx3h, "ffn2_down_dw")
    grads["ffn2_w_gate"] = ffn_bwd_dw_in(h3, da2, "ffn2_gate_dw")
    grads["ffn2_w_up"] = ffn_bwd_dw_in(h3, db2, "ffn2_up_dw")
    tok = emit("ffn2", grads)
    dh3 = ffn_bwd_dh(da2, db2, wgu2, wgu2, 0, 1, 2, D, "ffn2_bwd_dh", dep=tok)
    tok = emit2("ffn2", dh3)
    dx2, _, dg_ffn2 = rms_bwd(dh3, x2, g_ffn2, dx3, "rms_ffn2_bwd", dep=tok, half=False)

    dcat = mm_nt([(dx2, wo)], "mix_out_dh", tn=512, out_dtype=BF16)
    grads["w_o"] = mm_tn_plain(cat, dx2, "mix_out_dw")
    dq, dk, dv, dct, dtiles = attention_bwd(qkv, c, ct, tiles, lse, cat, dcat, "attention_bwd")
    dctp = jnp.pad(dct[:, 0, :], ((0, LANES - N_HEADS), (0, 0)))
    duf, dbf = fox_gate_bwd(dctp, uf, bfp, "fox_gate_bwd")
    drel = rel_table_grad(dtiles, T, "rel_table_grad")[:, 0, :N_REL_BUCKETS].T
    du3 = jnp.concatenate([dq, dk, dv], axis=1)
    grads["w3"] = mm_tn_plain(du3, h2, "mix_qkv_dw", tm=768)
    grads["wf"] = mm_tn_plain(duf, h2, "mix_forget_dw", tm=LANES)
    tok = emit("mix", grads)
    dh2 = mm_nn_sum([(du3, w3), (duf, wf)], "mix_in_dh", tn=512, out_dtype=F32, dep=tok)
    tok = emit2("mix", dh2)
    dx1, dx1h, dg_mix = rms_bwd(dh2, x1, g_mix, dx2, "rms_mix_bwd", dep=tok)

    da1, db1 = ffn_bwd_act(dx1h, wd1, 0, 1, a1, b1, "ffn1_bwd_act")
    grads["ffn1_w_down"] = ffn_bwd_dw_down(s1, dx1h, "ffn1_down_dw")
    tok = emit("ffn1_d", grads)
    grads["ffn1_w_gate"] = ffn_bwd_dw_in(h1, da1, "ffn1_gate_dw", dep=tok)
    tok = emit2("ffn1_d", grads["ffn1_w_gate"])
    tok = emit("ffn1_g", grads, after=tok)
    grads["ffn1_w_up"] = ffn_bwd_dw_in(h1, db1, "ffn1_up_dw", dep=tok)
    tok = emit2("ffn1_g", grads["ffn1_w_up"])
    tok = emit("ffn1_u", grads, after=tok)
    dh1 = ffn_bwd_dh(da1, db1, wg1, wu1, 0, 0, 1, D, "ffn1_bwd_dh", dep=tok)
    tok = emit2("ffn1_u", dh1)
    dx0, _, dg_ffn1 = rms_bwd(dh1, x, g_ffn1, dx1, "rms_ffn1_bwd", dep=tok, half=False)

    small = _pack_small(D, dg_ffn1, dg_mix, dg_ffn2, dg_ple, dg_final, dbf[:, :N_FOX], drel)
    small = small.at[_ROW_LOSS, :LANES].set(loss_row[0])
    grads["small"] = small
    emit("small", grads)
    return dx0


def _split_w_in(w_in_t):
    df, dd = N_FOX * HEAD_DIM, N_DIL * HEAD_DIM
    o = np.cumsum([0, df, df, df, N_FOX, dd, dd, dd]).tolist()
    qa, ka, va, f, qb, kb, vb = [w_in_t[o[i]:o[i + 1]] for i in range(7)]
    return jnp.concatenate([qa, qb, ka, kb, va, vb], axis=0), f


def _join_w_in(d3, dfg):
    df, dd = N_FOX * HEAD_DIM, N_DIL * HEAD_DIM
    o = np.cumsum([0, df, dd, df, dd, df, dd]).tolist()
    qa, qb, ka, kb, va, vb = [d3[o[i]:o[i + 1]] for i in range(6)]
    return jnp.concatenate([qa, ka, va, dfg, qb, kb, vb], axis=0)


def rows_to_bf16(a3, name, dep=None):
    R, _, C = a3.shape
    tc = _tile(C, 512)

    def body(a_ref, *rest):
        rest[-1][...] = a_ref[...].astype(BF16)

    in_specs = [pl.BlockSpec((R, None, tc), lambda n: (0, 0, n))]
    args = [a3]
    if dep is not None:
        in_specs.append(_dep_spec(1))
        args.append(dep)
    return pl.pallas_call(
        body, name=name, grid=(C // tc,), in_specs=in_specs,
        out_specs=pl.BlockSpec((R, tc), lambda n: (0, n)),
        out_shape=jax.ShapeDtypeStruct((R, C), BF16),
        compiler_params=_params(("parallel",)),
    )(*args)


def adamw_rows3d(parts, sums, my_chip, w3, m3, v3, name):
    R, _, C = w3.shape
    S = parts.shape[0]
    tc = _tile(C, 256)

    def body(mc_ref, p_ref, o_ref, w_ref, m_ref, v_ref, g_ref, d_ref, nm_ref, nv_ref):
        g = _sum_partials(p_ref, o_ref, mc_ref[0])
        delta, nm, nv = _adamw_math(w_ref[...], g, m_ref[...], v_ref[...])
        g_ref[...] = g
        d_ref[...] = delta
        nm_ref[...] = nm
        nv_ref[...] = nv

    col = pl.BlockSpec((R, None, tc), lambda n, mc: (0, 0, n))
    shp = jax.ShapeDtypeStruct((R, 1, C), F32)
    grid_spec = pltpu.PrefetchScalarGridSpec(
        num_scalar_prefetch=1, grid=(C // tc,),
        in_specs=[pl.BlockSpec((S, R, tc), lambda n, mc: (0, 0, n)),
                  pl.BlockSpec((None, R, tc), lambda n, mc: (mc[0], 0, n)), col, col, col],
        out_specs=[col, col, col, col])
    return pl.pallas_call(
        body, name=name, grid_spec=grid_spec, out_shape=[shp, shp, shp, shp],
        compiler_params=_params(("parallel",)),
    )(my_chip.reshape(1).astype(jnp.int32), parts, sums, w3, m3, v3)


def kernel(x, p, norm_ffn1, ffn1_w_gate, ffn1_w_up, ffn1_w_down, norm_mix, w_in, b_f, w_o, norm_ffn2, ffn2_w_gate, ffn2_w_up, ffn2_w_down, norm_ple, w_ple_gate, w_ple_proj, rel_table, norm_final, loss_target, m_norm_ffn1, m_ffn1_w_gate, m_ffn1_w_up, m_ffn1_w_down, m_norm_mix, m_w_in, m_b_f, m_w_o, m_norm_ffn2, m_ffn2_w_gate, m_ffn2_w_up, m_ffn2_w_down, m_norm_ple, m_w_ple_gate, m_w_ple_proj, m_rel_table, m_norm_final, v_norm_ffn1, v_ffn1_w_gate, v_ffn1_w_up, v_ffn1_w_down, v_norm_mix, v_w_in, v_b_f, v_w_o, v_norm_ffn2, v_ffn2_w_gate, v_ffn2_w_up, v_ffn2_w_down, v_norm_ple, v_w_ple_gate, v_w_ple_proj, v_rel_table, v_norm_final):
    names = ["norm_ffn1", "ffn1_w_gate", "ffn1_w_up", "ffn1_w_down", "norm_mix", "w_in", "b_f", "w_o",
             "norm_ffn2", "ffn2_w_gate", "ffn2_w_up", "ffn2_w_down", "norm_ple", "w_ple_gate",
             "w_ple_proj", "rel_table", "norm_final"]
    w = dict(zip(names, [norm_ffn1, ffn1_w_gate, ffn1_w_up, ffn1_w_down, norm_mix, w_in, b_f, w_o,
                         norm_ffn2, ffn2_w_gate, ffn2_w_up, ffn2_w_down, norm_ple, w_ple_gate,
                         w_ple_proj, rel_table, norm_final]))
    m = dict(zip(names, [m_norm_ffn1, m_ffn1_w_gate, m_ffn1_w_up, m_ffn1_w_down, m_norm_mix, m_w_in,
                         m_b_f, m_w_o, m_norm_ffn2, m_ffn2_w_gate, m_ffn2_w_up, m_ffn2_w_down,
                         m_norm_ple, m_w_ple_gate, m_w_ple_proj, m_rel_table, m_norm_final]))
    v = dict(zip(names, [v_norm_ffn1, v_ffn1_w_gate, v_ffn1_w_up, v_ffn1_w_down, v_norm_mix, v_w_in,
                         v_b_f, v_w_o, v_norm_ffn2, v_ffn2_w_gate, v_ffn2_w_up, v_ffn2_w_down,
                         v_norm_ple, v_w_ple_gate, v_w_ple_proj, v_rel_table, v_norm_final]))
    sharded = ["ffn1_w_gate", "ffn1_w_up", "ffn1_w_down", "w_in", "w_o", "ffn2_w_gate", "ffn2_w_up",
               "ffn2_w_down", "w_ple_gate", "w_ple_proj"]
    small_names = [n for n in names if n not in sharded]

    xs, ps, tgt = x[0], p[0, 0], loss_target[0]
    T, D = xs.shape
    transposed = ("ffn1_w_gate", "ffn1_w_up", "ffn2_w_gate", "ffn2_w_up")

    def view(t, n):
        if n in transposed:
            return t[n][0].T
        if n == "w_in":
            return jnp.transpose(t[n], (2, 0, 1))
        return t[n][0]

    def unview(a, n):
        if n in transposed:
            return a.T.reshape(w[n].shape)
        if n == "w_in":
            return jnp.transpose(a, (1, 2, 0))
        return a.reshape(w[n].shape)

    sh = {n: view(w, n) for n in sharded}
    m_sh = {n: view(m, n) for n in sharded}
    v_sh = {n: view(v, n) for n in sharded}
    F8 = sh["ffn1_w_down"].shape[0]
    WIN8 = sh["w_in"].shape[0]
    me = 4 * lax.axis_index("x") + 2 * lax.axis_index("y") + lax.axis_index("c")

    def start(groups, name, after=None):
        srcs = [s for grp in groups for s in grp]
        sems, bufs, token = gather_start(srcs, [_landing(s, me) for s in srcs], name, after=after)
        return sems, bufs[:len(srcs)], bufs[len(srcs):], token

    cat0 = lambda ns, z: (jnp.concatenate([sh[n] for n in ns], axis=0) + z).astype(BF16)
    sems_a, srcs_a, lands_a, token_a = start(
        [[sh["ffn1_w_gate"].astype(BF16)], [sh["ffn1_w_up"].astype(BF16)], [sh["ffn1_w_down"].astype(BF16)]],
        "gather_start_ffn1")
    zero = token_a[0, 0]
    w_in_bf = rows_to_bf16(sh["w_in"], "w_in_bf16", dep=token_a)
    sems_b, srcs_b, lands_b, g_token = start(
        [[w_in_bf, (sh["w_o"] + zero).astype(BF16)],
         [cat0(["ffn2_w_gate", "ffn2_w_up"], zero), (sh["ffn2_w_down"] + zero).astype(BF16)],
         [(sh["w_ple_gate"] + zero).astype(BF16), (sh["w_ple_proj"] + zero).astype(BF16)]],
        "gather_start_rest", after=token_a)
    order = ["ffn1_g", "ffn1_u", "ffn1_d", "mix_in", "mix_out", "ffn2_gu", "ffn2_d", "ple"]
    group_sizes = [1, 1, 1, 1, 1, 1, 1, 2]
    g_sems, g_srcs, g_lands = sems_a + sems_b, srcs_a + srcs_b, lands_a + lands_b
    g_send, g_recv_d, g_recv_i = g_sems[0::3], g_sems[1::3], g_sems[2::3]
    first = np.cumsum([0] + group_sizes).tolist()
    passed = {}

    def arrays_of(group):
        k = order.index(group)
        return slice(first[k], first[k + 1])

    def forward(group, after):
        sl = arrays_of(group)
        f_sems, bufs = gather_forward(g_srcs[sl], g_lands[sl], g_recv_i[sl], after, "gather_forward_" + group)
        k = len(bufs) // 2
        passed[group] = (f_sems[0::2], f_sems[1::2], bufs[:k], bufs[k:])

    def weights(group, after):
        sl = arrays_of(group)
        f_send, f_recv, srcs, lands = passed[group]
        got = gather_wait(srcs, lands, g_send[sl], g_recv_d[sl], f_send, f_recv, after, "gather_wait_" + group)
        if group in ("ffn1_g", "ffn1_u", "ffn1_d"):
            return (got[0].reshape(N_DEV * F8, D),)
        if group == "ffn2_gu":
            return (got[0].reshape(N_DEV * 2 * F8, D),)
        if group in ("ffn2_d", "mix_out"):
            return (got[0].reshape(-1, D),)
        if group == "ple":
            return got[0].reshape(-1, D), got[1].reshape(-1, got[1].shape[2])
        w3, wf8 = _split_w_in(got[0].reshape(N_DEV * WIN8, D))
        return w3, jnp.pad(wf8, ((0, LANES - N_FOX), (0, 0)))

    scatter_groups = {
        "ple": ["w_ple_gate", "w_ple_proj"],
        "ffn2": ["ffn2_w_gate", "ffn2_w_up", "ffn2_w_down"],
        "mix": ["w_in", "w_o"],
        "ffn1_d": ["ffn1_w_down"],
        "ffn1_g": ["ffn1_w_gate"],
        "ffn1_u": ["ffn1_w_up"],
    }
    x_i, y_i, c_i = _xyc()
    my_chip = 2 * x_i + y_i
    pair_stage, chip_stage, small_stage = {}, {}, {}

    def emit(group, grads, after=None):
        if group == "small":
            src = grads["small"]
            ss, rs, srcs, lands, token = exchange_start([src], [_landing(src, me)], [False], "scatter_start_small")
            small_stage["small"] = (ss, rs, srcs, lands)
            return token
        src4s = []
        for n in scatter_groups[group]:
            if n == "w_in":
                full = _join_w_in(grads["w3"], grads["wf"][:N_FOX])
                src4s.append(full.reshape(N_CHIPS, 2, WIN8, D))
            else:
                src4s.append(grads[n].reshape((N_CHIPS, 2) + sh[n].shape))
        lands = [lax.empty((N_CHIPS,) + s.shape[2:], BF16) for s in src4s]
        sems, bufs, token = scatter_pair_start(src4s, lands, "scatter_pair_start_" + group, after=after)
        k = len(src4s)
        pair_stage[group] = (sems, bufs[:k], bufs[k:])
        return token

    def emit2(group, after):
        sems, src4s, lands = pair_stage[group]
        src4s, lands = scatter_pair_wait(src4s, lands, sems, after, "scatter_pair_wait_" + group)
        sums = [chip_sum(s4, la, c_i, "chip_sum_" + n)
                for s4, la, n in zip(src4s, lands, scatter_groups[group])]
        chip_lands = [lax.empty(s.shape, s.dtype) for s in sums]
        sems, bufs, token = scatter_chip_start(sums, chip_lands, "scatter_chip_start_" + group)
        k = len(sums)
        chip_stage[group] = (sems, bufs[:k], bufs[k:])
        return token

    dx0 = local_step(
        xs, ps, tgt, w["norm_ffn1"], w["norm_mix"], w["norm_ffn2"], w["norm_ple"],
        w["norm_final"].reshape(1, D), w["b_f"], w["rel_table"], forward, weights, emit, emit2, g_token)

    res = {}
    after = dx0
    for group in ["ple", "ffn2", "mix", "ffn1_d", "ffn1_g", "ffn1_u"]:
        sems, sums, chip_lands = chip_stage[group]
        sums, parts = scatter_chip_wait(sums, chip_lands, sems, after, "scatter_chip_wait_" + group)
        for n, part, own in zip(scatter_groups[group], parts, sums):
            update = adamw_rows3d if n == "w_in" else adamw_sharded
            g, d, nm, nv = update(part, own, my_chip, sh[n], m_sh[n], v_sh[n], "adamw_" + n)
            res[n] = tuple(unview(a, n) for a in (g, d, nm, nv))
            after = g
    ss, rs, srcs, lands = small_stage["small"]
    small_parts, = exchange_wait(ss, rs, srcs, lands, [False], after, "scatter_wait_small")
    pack = lambda t: _pack_small(D, t["norm_ffn1"], t["norm_mix"], t["norm_ffn2"], t["norm_ple"],
                                 t["norm_final"], t["b_f"], t["rel_table"])
    gs, ds, ms, vs = adamw_small(small_parts, pack(w), pack(m), pack(v), "adamw_small")
    shapes = {n: w[n].shape for n in small_names}
    unpacked = [_unpack_small(a, shapes) for a in (gs, ds, ms, vs)]
    for n in small_names:
        res[n] = tuple(u[n] for u in unpacked)
    loss = gs[_ROW_LOSS, 0]

    out = [loss, dx0.reshape(x.shape)]
    for k in range(4):
        out += [res[n][k] for n in names]
    return tuple(out)
```

```python
import functools
import math

import numpy as np
import jax
import jax.numpy as jnp
from jax import lax
from jax.experimental import pallas as pl
from jax.experimental.pallas import tpu as pltpu

F32 = jnp.float32
BF16 = jnp.bfloat16

N_DEV = 8
HEAD_DIM = 128
N_FOX = 8
N_DIL = 8
N_HEADS = N_FOX + N_DIL
DILATED_PATTERNS = ((128, 1), (512, 4), (2048, 16))
N_REL_BUCKETS = 32
REL_MAX_DISTANCE = 2048
RMS_EPS = 1e-6
NEG_INF = -1e30
LANES = 128
VMEM_LIMIT = 56 * 1024 * 1024

ADAM_LR = 0.001
ADAM_B1 = 0.9
ADAM_B2 = 0.999
ADAM_EPS = 1e-08
ADAM_WD = 0.01
ADAM_STEP = 10

MESH = pl.DeviceIdType.MESH


def _params(sem):
    return pltpu.CompilerParams(dimension_semantics=sem, vmem_limit_bytes=VMEM_LIMIT)


def _dot(a, b, ca, cb, precision=None):
    return lax.dot_general(a, b, (((ca,), (cb,)), ((), ())),
                           preferred_element_type=F32, precision=precision)


def _sigmoid(z):
    return 1.0 / (1.0 + jnp.exp(-z))


def _tile(n, want):
    t = min(n, want)
    assert n % t == 0, (n, t)
    return t


def _dep_spec(ngrid):
    return pl.BlockSpec((8, LANES), lambda *_: (0, 0))


def rms_fwd(x, g, name, dep=None):
    T, D = x.shape
    tm = _tile(T, 256)

    def body(x_ref, g_ref, *rest):
        h_ref = rest[-1]
        xv = x_ref[...]
        r = lax.rsqrt(jnp.mean(xv * xv, axis=-1, keepdims=True) + RMS_EPS)
        h_ref[...] = (xv * r * g_ref[...]).astype(BF16)

    in_specs = [pl.BlockSpec((tm, D), lambda i: (i, 0)), pl.BlockSpec((1, D), lambda i: (0, 0))]
    args = [x, g]
    if dep is not None:
        in_specs.append(_dep_spec(1))
        args.append(dep)
    return pl.pallas_call(
        body, name=name, grid=(T // tm,), in_specs=in_specs,
        out_specs=pl.BlockSpec((tm, D), lambda i: (i, 0)),
        out_shape=jax.ShapeDtypeStruct((T, D), BF16),
        compiler_params=_params(("parallel",)),
    )(*args)


def rms_bwd(dh, x, g, dres, name, dep=None, half=True):
    T, D = x.shape
    tm = _tile(T, 256)

    def body(dh_ref, x_ref, g_ref, dres_ref, *rest):
        dx_ref, dg_ref = (rest[-3], rest[-1]) if half else (rest[-2], rest[-1])
        i = pl.program_id(0)
        xv = x_ref[...]
        r = lax.rsqrt(jnp.mean(xv * xv, axis=-1, keepdims=True) + RMS_EPS)
        xh = xv * r
        d = dh_ref[...]
        u = d * g_ref[...]
        dx = dres_ref[...] + r * (u - xh * jnp.mean(u * xh, axis=-1, keepdims=True))
        dx_ref[...] = dx
        if half:
            rest[-2][...] = (0.5 * dx).astype(BF16)
        part = jnp.sum(d * xh, axis=0, keepdims=True)

        @pl.when(i == 0)
        def _():
            dg_ref[...] = part

        @pl.when(i > 0)
        def _():
            dg_ref[...] += part

    row = pl.BlockSpec((tm, D), lambda i: (i, 0))
    vec = pl.BlockSpec((1, D), lambda i: (0, 0))
    in_specs = [row, row, vec, row]
    args = [dh, x, g, dres]
    if dep is not None:
        in_specs.append(_dep_spec(1))
        args.append(dep)
    out_specs = [row, row, vec] if half else [row, vec]
    out_shape = [jax.ShapeDtypeStruct((T, D), F32)] + ([jax.ShapeDtypeStruct((T, D), BF16)] if half else [])
    out_shape.append(jax.ShapeDtypeStruct((1, D), F32))
    outs = pl.pallas_call(
        body, name=name, grid=(T // tm,),
        in_specs=in_specs, out_specs=out_specs, out_shape=out_shape,
        compiler_params=_params(("arbitrary",)),
    )(*args)
    return tuple(outs) if half else (outs[0], None, outs[1])


def ple_loss(x, z, pp, g, target, name):
    T, D = x.shape
    tm = _tile(T, 256)

    def body(x_ref, z_ref, p_ref, g_ref, t_ref, loss_ref, dx_ref, dg_ref, dz_ref, dp_ref):
        i = pl.program_id(0)
        gate = _sigmoid(z_ref[...])
        ppv = p_ref[...]
        xv = x_ref[...] + gate * ppv
        gv = g_ref[...]
        r = lax.rsqrt(jnp.mean(xv * xv, axis=-1, keepdims=True) + RMS_EPS)
        xh = xv * r
        e = xh * gv - t_ref[...]
        lpart = 0.5 * jnp.sum(jnp.mean(e * e, axis=-1, keepdims=True), axis=0, keepdims=True)
        lrow = jnp.broadcast_to(lpart, (1, LANES))
        d = e * (1.0 / D)
        u = d * gv
        dx = r * (u - xh * jnp.mean(u * xh, axis=-1, keepdims=True))
        dx_ref[...] = dx
        dz_ref[...] = (dx * ppv * gate * (1.0 - gate)).astype(BF16)
        dp_ref[...] = (dx * gate).astype(BF16)
        part = jnp.sum(d * xh, axis=0, keepdims=True)

        @pl.when(i == 0)
        def _():
            dg_ref[...] = part
            loss_ref[...] = lrow

        @pl.when(i > 0)
        def _():
            dg_ref[...] += part
            loss_ref[...] += lrow

    row = pl.BlockSpec((tm, D), lambda i: (i, 0))
    vec = pl.BlockSpec((1, D), lambda i: (0, 0))
    return pl.pallas_call(
        body, name=name, grid=(T // tm,),
        in_specs=[row, row, row, vec, row],
        out_specs=[pl.BlockSpec((1, LANES), lambda i: (0, 0)), row, vec, row, row],
        out_shape=[jax.ShapeDtypeStruct((1, LANES), F32), jax.ShapeDtypeStruct((T, D), F32),
                   jax.ShapeDtypeStruct((1, D), F32), jax.ShapeDtypeStruct((T, D), BF16),
                   jax.ShapeDtypeStruct((T, D), BF16)],
        compiler_params=_params(("arbitrary",)),
    )(x, z, pp, g, target)


def _bf(v, scale=None):
    if scale is not None:
        v = v * scale
    return v.astype(BF16)


def mm_nn(a, b, name, *, tn, out_dtype, tm=512, n_out=None, b_block=None, b_map=None,
          res=None):
    T, K = a.shape
    N = n_out if n_out is not None else b.shape[1]
    tm = _tile(T, tm)
    tn = _tile(N, tn)
    b_block = b_block or (K, tn)
    b_map = b_map or (lambda n, i: (0, n))

    def body(*refs):
        a_ref, b_ref = refs[0], refs[1]
        o_ref = refs[-1]
        acc = _dot(_bf(a_ref[...]), _bf(b_ref[...]), 1, 0)
        if res is not None:
            acc = refs[2][...] + acc
        o_ref[...] = acc.astype(out_dtype)

    in_specs = [pl.BlockSpec((tm, K), lambda n, i: (i, 0)), pl.BlockSpec(b_block, b_map)]
    args = [a, b]
    if res is not None:
        in_specs.append(pl.BlockSpec((tm, tn), lambda n, i: (i, n)))
        args.append(res)
    return pl.pallas_call(
        body, name=name, grid=(N // tn, T // tm), in_specs=in_specs,
        out_specs=pl.BlockSpec((tm, tn), lambda n, i: (i, n)),
        out_shape=jax.ShapeDtypeStruct((T, N), out_dtype),
        compiler_params=_params(("parallel", "parallel")),
    )(*args)


def mm_nn_sum(pairs, name, *, tn, out_dtype, tm=512, dep=None):
    T = pairs[0][0].shape[0]
    N = pairs[0][1].shape[1]
    tm = _tile(T, tm)
    tn = _tile(N, tn)
    npair = len(pairs)

    def body(*refs):
        acc = None
        for q in range(npair):
            part = _dot(_bf(refs[2 * q][...]), _bf(refs[2 * q + 1][...]), 1, 0)
            acc = part if acc is None else acc + part
        refs[-1][...] = acc.astype(out_dtype)

    in_specs, args = [], []
    for a, b in pairs:
        K = a.shape[1]
        in_specs += [pl.BlockSpec((tm, K), lambda n, i: (i, 0)), pl.BlockSpec((K, tn), lambda n, i: (0, n))]
        args += [a, b]
    if dep is not None:
        in_specs.append(_dep_spec(2))
        args.append(dep)
    return pl.pallas_call(
        body, name=name, grid=(N // tn, T // tm), in_specs=in_specs,
        out_specs=pl.BlockSpec((tm, tn), lambda n, i: (i, n)),
        out_shape=jax.ShapeDtypeStruct((T, N), out_dtype),
        compiler_params=_params(("parallel", "parallel")),
    )(*args)


def mm_nt(pairs, name, *, tn, out_dtype, tm=512, dep=None):
    T = pairs[0][0].shape[0]
    N = pairs[0][1].shape[0]
    tm = _tile(T, tm)
    tn = _tile(N, tn)
    npair = len(pairs)

    def body(*refs):
        o_ref = refs[-1]
        acc = None
        for q in range(npair):
            part = _dot(_bf(refs[2 * q][...]), _bf(refs[2 * q + 1][...]), 1, 1)
            acc = part if acc is None else acc + part
        o_ref[...] = acc.astype(out_dtype)

    in_specs, args = [], []
    for a, b in pairs:
        K = a.shape[1]
        in_specs += [pl.BlockSpec((tm, K), lambda n, i: (i, 0)), pl.BlockSpec((tn, K), lambda n, i: (n, 0))]
        args += [a, b]
    if dep is not None:
        in_specs.append(_dep_spec(2))
        args.append(dep)
    return pl.pallas_call(
        body, name=name, grid=(N // tn, T // tm), in_specs=in_specs,
        out_specs=pl.BlockSpec((tm, tn), lambda n, i: (i, n)),
        out_shape=jax.ShapeDtypeStruct((T, N), out_dtype),
        compiler_params=_params(("parallel", "parallel")),
    )(*args)


def mm_tn(a, b, name, *, grid, a_block, a_map, b_block, b_map, o_block, o_map, out_shape,
          b_scale=None, dep=None):
    def body(a_ref, b_ref, *rest):
        rest[-1][...] = _dot(_bf(a_ref[...]), _bf(b_ref[...], b_scale), 0, 0).astype(BF16)

    in_specs = [pl.BlockSpec(a_block, a_map), pl.BlockSpec(b_block, b_map)]
    args = [a, b]
    if dep is not None:
        in_specs.append(_dep_spec(len(grid)))
        args.append(dep)
    return pl.pallas_call(
        body, name=name, grid=grid, in_specs=in_specs,
        out_specs=pl.BlockSpec(o_block, o_map),
        out_shape=jax.ShapeDtypeStruct(out_shape, BF16),
        compiler_params=_params(("parallel",) * len(grid)),
    )(*args)


def mm_tn_plain(a, b, name, *, tm=512, tn=512, b_scale=None):
    T, M = a.shape
    N = b.shape[1]
    tm = _tile(M, tm)
    tn = _tile(N, tn)
    return mm_tn(a, b, name, grid=(M // tm, N // tn),
                 a_block=(T, tm), a_map=lambda m, n: (0, m),
                 b_block=(T, tn), b_map=lambda m, n: (0, n),
                 o_block=(tm, tn), o_map=lambda m, n: (m, n),
                 out_shape=(M, N), b_scale=b_scale)


def ffn_up(h, wgu, gi, ui, nper, name):
    T, D = h.shape
    F8 = wgu.shape[0] // (N_DEV * nper)
    tm = _tile(T, 512)
    nt = T // tm

    def body(h_ref, wg_ref, wu_ref, ga_ref, gb_ref, s_ref):
        hv = h_ref[...]
        a = _dot(hv, wg_ref[...], 1, 1)
        b = _dot(hv, wu_ref[...], 1, 1)
        sg = _sigmoid(a)
        silu = a * sg
        ga_ref[...] = (b * (sg * (1.0 + a * (1.0 - sg)))).astype(BF16)
        gb_ref[...] = silu.astype(BF16)
        s_ref[...] = (silu * b).astype(BF16)

    blk = pl.BlockSpec((tm, F8), lambda j, i: (j * nt + i, 0))
    shp = jax.ShapeDtypeStruct((N_DEV * T, F8), BF16)
    return pl.pallas_call(
        body, name=name, grid=(N_DEV, nt),
        in_specs=[pl.BlockSpec((tm, D), lambda j, i: (i, 0)),
                  pl.BlockSpec((F8, D), lambda j, i: (j * nper + gi, 0)),
                  pl.BlockSpec((F8, D), lambda j, i: (j * nper + ui, 0))],
        out_specs=[blk, blk, blk], out_shape=[shp, shp, shp],
        compiler_params=_params(("parallel", "parallel")),
    )(h, wgu, wgu)


def ffn_gate(h, wg, name):
    T, D = h.shape
    F8 = wg.shape[0] // N_DEV
    tm = _tile(T, 512)
    nt = T // tm

    def body(h_ref, wg_ref, a_ref):
        a_ref[...] = _dot(h_ref[...], wg_ref[...], 1, 1).astype(BF16)

    return pl.pallas_call(
        body, name=name, grid=(N_DEV, nt),
        in_specs=[pl.BlockSpec((tm, D), lambda j, i: (i, 0)), pl.BlockSpec((F8, D), lambda j, i: (j, 0))],
        out_specs=pl.BlockSpec((tm, F8), lambda j, i: (j * nt + i, 0)),
        out_shape=jax.ShapeDtypeStruct((N_DEV * T, F8), BF16),
        compiler_params=_params(("parallel", "parallel")),
    )(h, wg)


def ffn_up_gated(h, wu, a, name):
    T, D = h.shape
    F8 = wu.shape[0] // N_DEV
    tm = _tile(T, 512)
    nt = T // tm

    def body(h_ref, wu_ref, a_ref, ga_ref, gb_ref, s_ref):
        av = a_ref[...].astype(F32)
        b = _dot(h_ref[...], wu_ref[...], 1, 1)
        sg = _sigmoid(av)
        silu = av * sg
        ga_ref[...] = (b * (sg * (1.0 + av * (1.0 - sg)))).astype(BF16)
        gb_ref[...] = silu.astype(BF16)
        s_ref[...] = (silu * b).astype(BF16)

    blk = pl.BlockSpec((tm, F8), lambda j, i: (j * nt + i, 0))
    shp = jax.ShapeDtypeStruct((N_DEV * T, F8), BF16)
    return pl.pallas_call(
        body, name=name, grid=(N_DEV, nt),
        in_specs=[pl.BlockSpec((tm, D), lambda j, i: (i, 0)), pl.BlockSpec((F8, D), lambda j, i: (j, 0)), blk],
        out_specs=[blk, blk, blk], out_shape=[shp, shp, shp],
        compiler_params=_params(("parallel", "parallel")),
    )(h, wu, a)


def ffn_down(s, wd, di, nper, x, g_next, name):
    T, D = x.shape
    F8 = s.shape[1]
    tm = _tile(T, 512)
    nt = T // tm

    def body(s_ref, w_ref, x_ref, g_ref, o_ref, h_ref, acc_ref):
        j = pl.program_id(1)
        part = _dot(s_ref[...], w_ref[...], 1, 0)

        @pl.when(j == 0)
        def _():
            acc_ref[...] = part

        @pl.when(j > 0)
        def _():
            acc_ref[...] += part

        @pl.when(j == N_DEV - 1)
        def _():
            xv = x_ref[...] + 0.5 * acc_ref[...]
            o_ref[...] = xv
            r = lax.rsqrt(jnp.mean(xv * xv, axis=-1, keepdims=True) + RMS_EPS)
            h_ref[...] = (xv * r * g_ref[...]).astype(BF16)

    row = pl.BlockSpec((tm, D), lambda i, j: (i, 0))
    return pl.pallas_call(
        body, name=name, grid=(nt, N_DEV),
        in_specs=[pl.BlockSpec((tm, F8), lambda i, j: (j * nt + i, 0)),
                  pl.BlockSpec((F8, D), lambda i, j: (j * nper + di, 0)),
                  row, pl.BlockSpec((1, D), lambda i, j: (0, 0))],
        out_specs=[row, row],
        out_shape=[jax.ShapeDtypeStruct((T, D), F32), jax.ShapeDtypeStruct((T, D), BF16)],
        scratch_shapes=[pltpu.VMEM((tm, D), F32)],
        compiler_params=_params(("parallel", "arbitrary")),
    )(s, wd, x, g_next)


def ffn_bwd_act(dxh, wd, di, nper_d, a, b, name, dep=None):
    T, D = dxh.shape
    F8 = a.shape[1]
    tm = _tile(T, 512)
    nt = T // tm

    def body(dx_ref, w_ref, a_ref, b_ref, *rest):
        da_ref, db_ref = rest[-2], rest[-1]
        ds = _dot(dx_ref[...], w_ref[...], 1, 1)
        da_ref[...] = (ds * a_ref[...].astype(F32)).astype(BF16)
        db_ref[...] = (ds * b_ref[...].astype(F32)).astype(BF16)

    blk = pl.BlockSpec((tm, F8), lambda j, i: (j * nt + i, 0))
    shp = jax.ShapeDtypeStruct((N_DEV * T, F8), BF16)
    in_specs = [pl.BlockSpec((tm, D), lambda j, i: (i, 0)),
                pl.BlockSpec((F8, D), lambda j, i: (j * nper_d + di, 0)), blk, blk]
    args = [dxh, wd, a, b]
    if dep is not None:
        in_specs.append(_dep_spec(2))
        args.append(dep)
    return pl.pallas_call(
        body, name=name, grid=(N_DEV, nt), in_specs=in_specs,
        out_specs=[blk, blk], out_shape=[shp, shp],
        compiler_params=_params(("parallel", "parallel")),
    )(*args)


def ffn_bwd_dh(da, db, wg, wu, gi, ui, nper, D, name, dep=None):
    F8 = da.shape[1]
    T = da.shape[0] // N_DEV
    tm = _tile(T, 512)
    nt = T // tm

    def body(da_ref, db_ref, wg_ref, wu_ref, *rest):
        o_ref, acc_ref = rest[-2], rest[-1]
        j = pl.program_id(1)
        part = _dot(da_ref[...], wg_ref[...], 1, 0) + _dot(db_ref[...], wu_ref[...], 1, 0)

        @pl.when(j == 0)
        def _():
            acc_ref[...] = part

        @pl.when(j > 0)
        def _():
            acc_ref[...] += part

        @pl.when(j == N_DEV - 1)
        def _():
            o_ref[...] = acc_ref[...]

    blk = pl.BlockSpec((tm, F8), lambda i, j: (j * nt + i, 0))
    in_specs = [blk, blk,
                pl.BlockSpec((F8, D), lambda i, j: (j * nper + gi, 0)),
                pl.BlockSpec((F8, D), lambda i, j: (j * nper + ui, 0))]
    args = [da, db, wg, wu]
    if dep is not None:
        in_specs.append(_dep_spec(2))
        args.append(dep)
    return pl.pallas_call(
        body, name=name, grid=(nt, N_DEV), in_specs=in_specs,
        out_specs=pl.BlockSpec((tm, D), lambda i, j: (i, 0)),
        out_shape=jax.ShapeDtypeStruct((T, D), F32),
        scratch_shapes=[pltpu.VMEM((tm, D), F32)],
        compiler_params=_params(("parallel", "arbitrary")),
    )(*args)


def ffn_dw_side(act, other, c, mine, name, add=None, dep=None):
    F8 = act.shape[1]
    T, D = other.shape
    tm = _tile(D, 512)

    def body(c_ref, a_ref, b_ref, *rest):
        acc = _dot(a_ref[...], b_ref[...], 0, 0)
        if add is not None:
            acc = acc + rest[0][...].astype(F32)
        rest[-1][...] = acc.astype(BF16)

    def shard(q, cr):
        return 2 * q + (cr[0] if mine else 1 - cr[0])

    in_specs = [pl.BlockSpec((T, F8), lambda q, m, cr: (shard(q, cr), 0)),
                pl.BlockSpec((T, tm), lambda q, m, cr: (0, m))]
    args = [act, other]
    if add is not None:
        in_specs.append(pl.BlockSpec((F8, tm), lambda q, m, cr: (q, m)))
        args.append(add)
    if dep is not None:
        in_specs.append(pl.BlockSpec((8, LANES), lambda q, m, cr: (0, 0)))
        args.append(dep)
    grid_spec = pltpu.PrefetchScalarGridSpec(
        num_scalar_prefetch=1, grid=(N_DEV // 2, D // tm), in_specs=in_specs,
        out_specs=pl.BlockSpec((F8, tm), lambda q, m, cr: (q, m)))
    return pl.pallas_call(
        body, name=name, grid_spec=grid_spec,
        out_shape=jax.ShapeDtypeStruct((N_DEV // 2 * F8, D), BF16),
        compiler_params=_params(("parallel", "parallel")),
    )(c.reshape(1).astype(jnp.int32), *args)


def _t5_bucket_np(dist):
    max_exact = N_REL_BUCKETS // 2
    d = np.maximum(dist, 1).astype(np.float64)
    large = max_exact + (np.log(d / max_exact) / math.log(REL_MAX_DISTANCE / max_exact)
                         * (N_REL_BUCKETS - max_exact)).astype(np.int64)
    large32 = max_exact + (np.log(d.astype(np.float32) / np.float32(max_exact))
                           / np.float32(math.log(REL_MAX_DISTANCE / max_exact))
                           * np.float32(N_REL_BUCKETS - max_exact)).astype(np.int64)
    assert np.array_equal(large, large32)
    large = np.minimum(large, N_REL_BUCKETS - 1)
    return np.where(dist < max_exact, dist, large)


def _distance_tables(T, tq):
    dist = np.arange(T)
    mult = np.zeros(T, np.int64)
    for window, dilation in DILATED_PATTERNS:
        mult += ((dist % dilation == 0) & (dist // dilation <= window // dilation)).astype(np.int64)
    logm = np.where(mult > 0, np.log(np.maximum(mult, 1)), NEG_INF).astype(np.float32)
    bucket = _t5_bucket_np(dist).astype(np.int32)
    nkb = T // tq
    k = np.arange(nkb)[:, None, None]
    r = np.arange(tq)[None, :, None]
    c = np.arange(tq)[None, None, :]
    delta = k * tq + r - c
    return bucket, logm, delta


def _tile_buckets(T, tq):
    bucket, logm, delta = _distance_tables(T, tq)
    safe = np.maximum(delta, 0)
    bidx = np.where(delta >= 0, bucket[safe], -1).astype(np.int32)
    logm_t = np.where(delta >= 0, logm[safe], NEG_INF).astype(np.float32)
    present = [sorted(set(np.unique(bidx[k]).tolist()) - {-1}) for k in range(T // tq)]
    return bidx, logm_t, present


def bias_tiles(rel_table, T, tq):
    bidx, logm_t, present = _tile_buckets(T, tq)
    nkb = T // tq

    def body(tab_ref, b_ref, lm_ref, o_ref):
        slot = pl.program_id(0)

        @pl.when(slot == 0)
        def _():
            o_ref[...] = jnp.where(b_ref[...] >= 0, 0.0, NEG_INF)

        @pl.when(slot > 0)
        def _():
            for k in range(nkb):
                bi = b_ref[k]
                acc = lm_ref[k]
                for b in present[k]:
                    acc = acc + jnp.where(bi == b, tab_ref[b, slot - 1], 0.0)
                o_ref[k] = acc

    full = pl.BlockSpec((nkb, tq, tq), lambda s: (0, 0, 0))
    return pl.pallas_call(
        body, name="bias_tiles", grid=(1 + N_DIL,),
        in_specs=[pl.BlockSpec(memory_space=pltpu.SMEM), full, full],
        out_specs=pl.BlockSpec((None, nkb, tq, tq), lambda s: (s, 0, 0, 0)),
        out_shape=jax.ShapeDtypeStruct((1 + N_DIL, nkb, tq, tq), F32),
        compiler_params=_params(("parallel",)),
    )(rel_table, jnp.asarray(bidx), jnp.asarray(logm_t))


def fox_gate_fwd(uf, bf, name):
    T = uf.shape[0]
    tb = _tile(T, 512)

    def body(u_ref, b_ref, c_ref, ct_ref):
        lane = lax.broadcasted_iota(jnp.int32, (1, LANES), 1)
        tri = (lax.broadcasted_iota(jnp.int32, (tb, tb), 0)
               >= lax.broadcasted_iota(jnp.int32, (tb, tb), 1)).astype(F32)
        carry = jnp.zeros((1, LANES), F32)
        for blk in range(T // tb):
            z = u_ref[pl.ds(blk * tb, tb), :] + b_ref[...]
            lf = jnp.minimum(z, 0.0) - jnp.log1p(jnp.exp(-jnp.abs(z)))
            lf = jnp.where(lane < N_FOX, lf, 0.0)
            cb = _dot(tri, lf, 1, 0, precision=lax.Precision.HIGHEST) + carry
            c_ref[pl.ds(blk * tb, tb), :] = cb
            ct_ref[:, pl.ds(blk * tb, tb)] = cb.T
            carry = cb[tb - 1:tb, :]

    return pl.pallas_call(
        body, name=name,
        out_shape=[jax.ShapeDtypeStruct((T, LANES), F32), jax.ShapeDtypeStruct((LANES, T), F32)],
        compiler_params=_params(None),
    )(uf, bf)


def fox_gate_bwd(dct, uf, bf, name):
    T = uf.shape[0]
    tb = _tile(T, 512)

    def body(d_ref, u_ref, b_ref, du_ref, db_ref):
        lane = lax.broadcasted_iota(jnp.int32, (1, LANES), 1)
        triu = (lax.broadcasted_iota(jnp.int32, (tb, tb), 0)
                <= lax.broadcasted_iota(jnp.int32, (tb, tb), 1)).astype(F32)
        carry = jnp.zeros((1, LANES), F32)
        dbv = jnp.zeros((1, LANES), F32)
        for blk in reversed(range(T // tb)):
            dc = d_ref[:, pl.ds(blk * tb, tb)].T
            dlf = _dot(triu, dc, 1, 0, precision=lax.Precision.HIGHEST) + carry
            carry = dlf[0:1, :]
            z = u_ref[pl.ds(blk * tb, tb), :] + b_ref[...]
            dz = jnp.where(lane < N_FOX, dlf * (1.0 - _sigmoid(z)), 0.0)
            du_ref[pl.ds(blk * tb, tb), :] = dz
            dbv = dbv + jnp.sum(dz, axis=0, keepdims=True)
        db_ref[...] = dbv

    return pl.pallas_call(
        body, name=name,
        out_shape=[jax.ShapeDtypeStruct((T, LANES), F32), jax.ShapeDtypeStruct((1, LANES), F32)],
        compiler_params=_params(None),
    )(dct, uf, bf)


def _bias_slot(h):
    return jnp.maximum(h - (N_FOX - 1), 0)


def _scores(q_ref, k_ref, c_ref, ct_ref, tb_ref, h, i, tq, fox):
    scale = HEAD_DIM ** -0.5
    n = (i + 1) * tq
    rows = pl.ds(i * tq, tq)
    s = _dot(q_ref[rows, :], k_ref[pl.ds(0, n), :], 1, 1) * scale
    if not fox:
        return s + jnp.concatenate([tb_ref[i - jb] for jb in range(i + 1)], axis=1)
    lane = lax.broadcasted_iota(jnp.int32, (1, LANES), 1)
    c_col = jnp.sum(jnp.where(lane == h, c_ref[rows, :], 0.0), axis=1, keepdims=True)
    c_row = ct_ref[pl.ds(h, 1), pl.ds(0, n)]
    s = s + (c_col - c_row)
    if i == 0:
        return s + tb_ref[0]
    return jnp.concatenate([s[:, :i * tq], s[:, i * tq:] + tb_ref[0]], axis=1)


def _attn_specs(T, tq):
    nkb = T // tq
    return [
        pl.BlockSpec((T, HEAD_DIM), lambda h: (0, h)),
        pl.BlockSpec((T, HEAD_DIM), lambda h: (0, N_HEADS + h)),
        pl.BlockSpec((T, HEAD_DIM), lambda h: (0, 2 * N_HEADS + h)),
        pl.BlockSpec((T, LANES), lambda h: (0, 0)),
        pl.BlockSpec((LANES, T), lambda h: (0, 0)),
        pl.BlockSpec((None, nkb, tq, tq), lambda h: (_bias_slot(h), 0, 0, 0)),
    ]


def attention_fwd(qkv, c, ct, tiles, name):
    T = qkv.shape[0]
    tq = tiles.shape[2]

    def body(q_ref, k_ref, v_ref, c_ref, ct_ref, tb_ref, o_ref, lse_ref):
        h = pl.program_id(0)
        lane = lax.broadcasted_iota(jnp.int32, (1, LANES), 1)

        @pl.when(h == 0)
        def _():
            lse_ref[...] = jnp.zeros_like(lse_ref)

        def head(fox):
            for i in range(T // tq):
                rows = pl.ds(i * tq, tq)
                s = _scores(q_ref, k_ref, c_ref, ct_ref, tb_ref, h, i, tq, fox)
                m = jnp.max(s, axis=1, keepdims=True)
                p = jnp.exp(s - m)
                l = jnp.sum(p, axis=1, keepdims=True)
                o = _dot(p.astype(BF16), v_ref[pl.ds(0, (i + 1) * tq), :], 1, 0) * (1.0 / l)
                o_ref[rows, :] = o.astype(BF16)
                lse_ref[rows, :] = jnp.where(lane == h, m + jnp.log(l), lse_ref[rows, :])

        pl.when(h < N_FOX)(functools.partial(head, True))
        pl.when(h >= N_FOX)(functools.partial(head, False))

    return pl.pallas_call(
        body, name=name, grid=(N_HEADS,),
        in_specs=_attn_specs(T, tq),
        out_specs=[pl.BlockSpec((T, HEAD_DIM), lambda h: (0, h)), pl.BlockSpec((T, LANES), lambda h: (0, 0))],
        out_shape=[jax.ShapeDtypeStruct((T, N_HEADS * HEAD_DIM), BF16), jax.ShapeDtypeStruct((T, LANES), F32)],
        compiler_params=_params(("arbitrary",)),
    )(qkv, qkv, qkv, c, ct, tiles)


def attention_bwd(qkv, c, ct, tiles, lse, o, do, name):
    T = qkv.shape[0]
    tq = tiles.shape[2]
    nkb = T // tq
    scale = HEAD_DIM ** -0.5

    def body(q_ref, k_ref, v_ref, c_ref, ct_ref, tb_ref, lse_ref, o_ref, do_ref,
             dq_ref, dk_ref, dv_ref, dct_ref, dtb_ref, dk_acc, dv_acc):
        h = pl.program_id(0)
        lane = lax.broadcasted_iota(jnp.int32, (1, LANES), 1)
        dk_acc[...] = jnp.zeros_like(dk_acc)
        dv_acc[...] = jnp.zeros_like(dv_acc)
        dct_ref[...] = jnp.zeros_like(dct_ref)
        dtb_ref[...] = jnp.zeros_like(dtb_ref)

        def head(fox):
            for i in range(nkb):
                rows, keys = pl.ds(i * tq, tq), pl.ds(0, (i + 1) * tq)
                s = _scores(q_ref, k_ref, c_ref, ct_ref, tb_ref, h, i, tq, fox)
                lse_col = jnp.sum(jnp.where(lane == h, lse_ref[rows, :], 0.0), axis=1, keepdims=True)
                p = jnp.exp(s - lse_col)
                p_b = p.astype(BF16)
                dov = do_ref[rows, :]
                dp = _dot(dov, v_ref[keys, :], 1, 1)
                if fox:
                    delta = jnp.sum(p * dp, axis=1, keepdims=True)
                else:
                    delta = jnp.sum(dov.astype(F32) * o_ref[rows, :].astype(F32), axis=1, keepdims=True)
                ds = p * (dp - delta)
                ds_b = ds.astype(BF16)
                dq_ref[rows, :] = (_dot(ds_b, k_ref[keys, :], 1, 0) * scale).astype(BF16)
                dk_acc[:, keys] += _dot(q_ref[rows, :], ds_b, 0, 0) * scale
                dv_acc[:, keys] += _dot(dov, p_b, 0, 0)
                if fox:
                    dct_ref[:, keys] += -jnp.sum(ds, axis=0, keepdims=True)
                else:
                    for jb in range(i + 1):
                        dtb_ref[i - jb] += ds[:, jb * tq:(jb + 1) * tq]

        pl.when(h < N_FOX)(functools.partial(head, True))
        pl.when(h >= N_FOX)(functools.partial(head, False))
        dk_ref[...] = dk_acc[...].T.astype(BF16)
        dv_ref[...] = dv_acc[...].T.astype(BF16)

    head_cols = jax.ShapeDtypeStruct((T, N_HEADS * HEAD_DIM), BF16)
    col = pl.BlockSpec((T, HEAD_DIM), lambda h: (0, h))
    return pl.pallas_call(
        body, name=name, grid=(N_HEADS,),
        in_specs=_attn_specs(T, tq) + [pl.BlockSpec((T, LANES), lambda h: (0, 0)), col, col],
        out_specs=[col, col, col,
                   pl.BlockSpec((None, 1, T), lambda h: (h, 0, 0)),
                   pl.BlockSpec((None, nkb, tq, tq), lambda h: (_bias_slot(h), 0, 0, 0))],
        out_shape=[head_cols, head_cols, head_cols,
                   jax.ShapeDtypeStruct((N_HEADS, 1, T), F32),
                   jax.ShapeDtypeStruct((1 + N_DIL, nkb, tq, tq), F32)],
        scratch_shapes=[pltpu.VMEM((HEAD_DIM, T), F32), pltpu.VMEM((HEAD_DIM, T), F32)],
        compiler_params=_params(("arbitrary",)),
    )(qkv, qkv, qkv, c, ct, tiles, lse, o, do)


def rel_table_grad(dtiles, T, name):
    tq = dtiles.shape[2]
    nkb = T // tq
    bidx, _, present = _tile_buckets(T, tq)

    def body(d_ref, b_ref, o_ref):
        lane = lax.broadcasted_iota(jnp.int32, (1, LANES), 1)
        row = jnp.zeros((1, LANES), F32)
        for k in range(nkb):
            d = d_ref[k]
            bi = b_ref[k]
            for b in present[k]:
                v = jnp.sum(jnp.sum(jnp.where(bi == b, d, 0.0), axis=0, keepdims=True),
                            axis=1, keepdims=True)
                row = row + jnp.where(lane == b, v, 0.0)
        o_ref[...] = row

    return pl.pallas_call(
        body, name=name, grid=(N_DIL,),
        in_specs=[pl.BlockSpec((None, nkb, tq, tq), lambda h: (h + 1, 0, 0, 0)),
                  pl.BlockSpec((nkb, tq, tq), lambda h: (0, 0, 0))],
        out_specs=pl.BlockSpec((None, 1, LANES), lambda h: (h, 0, 0)),
        out_shape=jax.ShapeDtypeStruct((N_DIL, 1, LANES), F32),
        compiler_params=_params(("parallel",)),
    )(dtiles, jnp.asarray(bidx))


def _peer_list():
    x, y, c = lax.axis_index("x"), lax.axis_index("y"), lax.axis_index("c")
    me = 4 * x + 2 * y + c
    peers = []
    for fx in (0, 1):
        for fy in (0, 1):
            for fc in (0, 1):
                if fx or fy or fc:
                    px = 1 - x if fx else x
                    py = 1 - y if fy else y
                    pc = 1 - c if fc else c
                    peers.append(((px, py, pc), 4 * px + 2 * py + pc))
    return me, peers


_HBM = pl.BlockSpec(memory_space=pltpu.HBM)
_SEM = pl.BlockSpec(memory_space=pltpu.SEMAPHORE)
_EFFECT = pltpu.SideEffectType.DATAFLOW_SIDE_EFFECTING
N_PEERS = N_DEV - 1


def _in_hbm(a):
    return pltpu.with_memory_space_constraint(a, pltpu.HBM)


def _exchange_copies(srcs, lands, send_sems, recv_sems, blockwise):
    me, peers = _peer_list()
    sends, recvs = [], []
    for a in range(len(srcs)):
        for k, (dev, idx) in enumerate(peers):
            src = srcs[a].at[idx] if blockwise[a] else srcs[a]
            sends.append(pltpu.make_async_remote_copy(
                src_ref=src, dst_ref=lands[a].at[me], send_sem=send_sems[a].at[k],
                recv_sem=recv_sems[a].at[k], device_id=dev, device_id_type=MESH))
            recvs.append(pltpu.make_async_remote_copy(
                src_ref=src, dst_ref=lands[a].at[idx], send_sem=send_sems[a].at[k],
                recv_sem=recv_sems[a].at[k], device_id=dev, device_id_type=MESH))
    return sends, recvs


def exchange_start(srcs, lands, blockwise, name):
    n = len(srcs)

    def body(*refs):
        src_in, land_in = refs[:n], refs[n:2 * n]
        send_sems, recv_sems = refs[2 * n:3 * n], refs[3 * n:4 * n]
        token = refs[6 * n]
        sends, _ = _exchange_copies(src_in, land_in, send_sems, recv_sems, blockwise)
        for cp in sends:
            cp.start()
        token[...] = jnp.zeros_like(token)

    out_shape = ([pltpu.SemaphoreType.DMA((N_PEERS,))] * (2 * n)
                 + [pltpu.HBM(s.shape, s.dtype) for s in srcs]
                 + [pltpu.HBM(l.shape, l.dtype) for l in lands]
                 + [jax.ShapeDtypeStruct((8, LANES), F32)])
    aliases = {a: 2 * n + a for a in range(2 * n)}
    outs = pl.pallas_call(
        body, name=name, out_shape=out_shape,
        in_specs=[_HBM] * (2 * n),
        out_specs=[_SEM] * (2 * n) + [_HBM] * (2 * n) + [pl.BlockSpec(memory_space=pltpu.VMEM)],
        input_output_aliases=aliases,
        compiler_params=pltpu.CompilerParams(has_side_effects=_EFFECT),
    )(*[_in_hbm(s) for s in srcs], *[_in_hbm(l) for l in lands])
    return (outs[:n], outs[n:2 * n], outs[2 * n:3 * n], outs[3 * n:4 * n], outs[4 * n])


def exchange_wait(send_sems, recv_sems, srcs, lands, blockwise, after, name):
    n = len(srcs)

    def body(*refs):
        src_in, land_in = refs[:n], refs[n:2 * n]
        ss, rs = refs[2 * n:3 * n], refs[3 * n:4 * n]
        sends, recvs = _exchange_copies(src_in, land_in, ss, rs, blockwise)
        for cp in sends:
            cp.wait_send()
        for cp in recvs:
            cp.wait_recv()

    outs = pl.pallas_call(
        body, name=name,
        out_shape=[pltpu.HBM(s.shape, s.dtype) for s in srcs] + [pltpu.HBM(l.shape, l.dtype) for l in lands],
        in_specs=[_HBM] * (2 * n) + [_SEM] * (2 * n) + [pl.BlockSpec(memory_space=pl.ANY)],
        out_specs=[_HBM] * (2 * n),
        input_output_aliases={a: a for a in range(2 * n)},
        compiler_params=pltpu.CompilerParams(has_side_effects=_EFFECT),
    )(*srcs, *lands, *send_sems, *recv_sems, after)
    return outs[n:]


def _landing(own_block, me, slots=N_DEV):
    empty = lax.empty((slots,) + own_block.shape, own_block.dtype)
    return lax.dynamic_update_slice(empty, own_block[None], (me,) + (0,) * own_block.ndim)


N_CHIPS = N_DEV // 2
_CHIP_FLIPS = ((1, 0), (0, 1), (1, 1))


def _xyc():
    return lax.axis_index("x"), lax.axis_index("y"), lax.axis_index("c")


def _other_chips(x, y):
    return [(1 - x if fx else x, 1 - y if fy else y) for fx, fy in _CHIP_FLIPS]


def _remote(src, dst, send_sem, recv_sem, dev):
    return pltpu.make_async_remote_copy(src_ref=src, dst_ref=dst, send_sem=send_sem, recv_sem=recv_sem,
                                        device_id=dev, device_id_type=MESH)


def comm_call(name, bufs, sems_in, sems_out, fn, after=None, want_token=False):
    nb, ni, no = len(bufs), len(sems_in), len(sems_out)
    afters = [] if after is None else (list(after) if isinstance(after, (list, tuple)) else [after])
    na = len(afters)

    def body(*refs):
        buf_refs = refs[:nb]
        sin = refs[nb:nb + ni]
        sout = refs[nb + ni + na:nb + ni + na + no]
        fn(buf_refs, sin, sout)
        if want_token:
            tok = refs[nb + ni + na + no + nb]
            tok[...] = jnp.zeros_like(tok)

    out_shape = list(sems_out) + [pltpu.HBM(b.shape, b.dtype) for b in bufs]
    out_specs = [_SEM] * no + [_HBM] * nb
    if want_token:
        out_shape.append(jax.ShapeDtypeStruct((8, LANES), F32))
        out_specs.append(pl.BlockSpec(memory_space=pltpu.VMEM))
    args = [_in_hbm(b) for b in bufs] + list(sems_in) + afters
    outs = pl.pallas_call(
        body, name=name, out_shape=out_shape,
        in_specs=[_HBM] * nb + [_SEM] * ni + [pl.BlockSpec(memory_space=pl.ANY)] * na,
        out_specs=out_specs, input_output_aliases={a: no + a for a in range(nb)},
        compiler_params=pltpu.CompilerParams(has_side_effects=_EFFECT),
    )(*args)
    return list(outs[:no]), list(outs[no:no + nb]), (outs[no + nb] if want_token else None)


def _dma_sems(*sizes):
    return [pltpu.SemaphoreType.DMA((s,)) for s in sizes]


def gather_start(srcs, lands, name, after=None):
    n = len(srcs)

    def fn(bufs, sin, sout):
        x, y, c = _xyc()
        me = 4 * x + 2 * y + c
        for a in range(n):
            src, land = bufs[a], bufs[n + a]
            send, recv_d, recv_i = sout[3 * a:3 * a + 3]
            _remote(src, land.at[me], send.at[0], recv_d.at[0], (x, y, 1 - c)).start()
            for k, (px, py) in enumerate(_other_chips(x, y)):
                _remote(src, land.at[me], send.at[1 + k], recv_i.at[k], (px, py, c)).start()

    return comm_call(name, list(srcs) + list(lands), [], _dma_sems(4, 1, 3) * n, fn, after=after, want_token=True)


def gather_forward(srcs, lands, recv_i, after, name):
    n = len(srcs)

    def fn(bufs, sin, sout):
        x, y, c = _xyc()
        for a in range(n):
            src, land = bufs[a], bufs[n + a]
            f_send, f_recv = sout[2 * a:2 * a + 2]
            for k, (px, py) in enumerate(_other_chips(x, y)):
                blk = land.at[4 * px + 2 * py + c]
                _remote(src, blk, f_send.at[k], sin[a].at[k], (px, py, c)).wait_recv()
                _remote(blk, blk, f_send.at[k], f_recv.at[k], (x, y, 1 - c)).start()

    sems, bufs, _ = comm_call(name, list(srcs) + list(lands), recv_i, _dma_sems(3, 3) * n, fn, after=after)
    return sems, bufs


def gather_wait(srcs, lands, send, recv_d, f_send, f_recv, after, name):
    n = len(srcs)

    def fn(bufs, sin, sout):
        x, y, c = _xyc()
        sib = (x, y, 1 - c)
        for a in range(n):
            src, land = bufs[a], bufs[n + a]
            s_send, s_recv_d, s_fsend, s_frecv = sin[4 * a:4 * a + 4]
            sib_blk = land.at[4 * x + 2 * y + 1 - c]
            for k in range(4):
                _remote(src, sib_blk, s_send.at[k], s_recv_d.at[0], sib).wait_send()
            _remote(src, sib_blk, s_send.at[0], s_recv_d.at[0], sib).wait_recv()
            for k, (px, py) in enumerate(_other_chips(x, y)):
                cp = _remote(src, land.at[4 * px + 2 * py + 1 - c], s_fsend.at[k], s_frecv.at[k], sib)
                cp.wait_send()
                cp.wait_recv()

    sems_in = []
    for a in range(n):
        sems_in += [send[a], recv_d[a], f_send[a], f_recv[a]]
    _, bufs, _ = comm_call(name, list(srcs) + list(lands), sems_in, [], fn, after=after)
    return bufs[n:]


def scatter_pair_start(src4s, lands, name, after=None, whole=False):
    n = len(src4s)

    def fn(bufs, sin, sout):
        x, y, c = _xyc()
        for a in range(n):
            src = bufs[a] if whole else bufs[a].at[:, 1 - c]
            _remote(src, bufs[n + a], sout[2 * a].at[0], sout[2 * a + 1].at[0], (x, y, 1 - c)).start()

    return comm_call(name, list(src4s) + list(lands), [], _dma_sems(1, 1) * n, fn, after=after, want_token=True)


def scatter_pair_wait(src4s, lands, sems, after, name, whole=False):
    n = len(src4s)

    def fn(bufs, sin, sout):
        x, y, c = _xyc()
        for a in range(n):
            src = bufs[a] if whole else bufs[a].at[:, 1 - c]
            cp = _remote(src, bufs[n + a], sin[2 * a].at[0], sin[2 * a + 1].at[0], (x, y, 1 - c))
            cp.wait_send()
            cp.wait_recv()

    _, bufs, _ = comm_call(name, list(src4s) + list(lands), sems, [], fn, after=after)
    return bufs[:n], bufs[n:]


def _row_tile(R):
    for cand in range(256, 15, -16):
        if R % cand == 0 and R // cand >= 4:
            return cand
    return R


def chip_sum(src4, land, c, name):
    _, _, R, C = src4.shape
    tr = R

    def body(c_ref, a_ref, b_ref, o_ref):
        o_ref[...] = (a_ref[...].astype(F32) + b_ref[...].astype(F32)).astype(BF16)

    grid_spec = pltpu.PrefetchScalarGridSpec(
        num_scalar_prefetch=1, grid=(N_CHIPS, R // tr),
        in_specs=[pl.BlockSpec((None, None, tr, C), lambda q, i, cr: (q, cr[0], i, 0)),
                  pl.BlockSpec((None, tr, C), lambda q, i, cr: (q, i, 0))],
        out_specs=pl.BlockSpec((None, tr, C), lambda q, i, cr: (q, i, 0)))
    return pl.pallas_call(
        body, name=name, grid_spec=grid_spec,
        out_shape=jax.ShapeDtypeStruct((N_CHIPS, R, C), BF16),
        compiler_params=_params(("parallel", "parallel")),
    )(c.reshape(1).astype(jnp.int32), src4, land)


def scatter_chip_start(sums, lands, name):
    n = len(sums)

    def fn(bufs, sin, sout):
        x, y, c = _xyc()
        for a in range(n):
            for k, (px, py) in enumerate(_other_chips(x, y)):
                _remote(bufs[a].at[2 * px + py], bufs[n + a].at[2 * x + y], sout[2 * a].at[k], sout[2 * a + 1].at[k],
                        (px, py, c)).start()

    return comm_call(name, list(sums) + list(lands), [], _dma_sems(3, 3) * n, fn, want_token=True)


def scatter_chip_wait(sums, lands, sems, after, name):
    n = len(sums)

    def fn(bufs, sin, sout):
        x, y, c = _xyc()
        for a in range(n):
            for k, (px, py) in enumerate(_other_chips(x, y)):
                cp = _remote(bufs[a].at[2 * px + py], bufs[n + a].at[2 * px + py], sin[2 * a].at[k],
                             sin[2 * a + 1].at[k], (px, py, c))
                cp.wait_send()
                cp.wait_recv()

    _, bufs, _ = comm_call(name, list(sums) + list(lands), sems, [], fn, after=after)
    return bufs[:n], bufs[n:]


def _adamw_math(w, g, m, v):
    m = ADAM_B1 * m + (1.0 - ADAM_B1) * g
    v = ADAM_B2 * v + (1.0 - ADAM_B2) * (g * g)
    m_hat = m / (1.0 - ADAM_B1 ** ADAM_STEP)
    v_hat = v / (1.0 - ADAM_B2 ** ADAM_STEP)
    delta = -ADAM_LR * (m_hat / (jnp.sqrt(v_hat) + ADAM_EPS) + ADAM_WD * w)
    return delta, m, v


def _sum_partials(p_ref, own_ref, mine):
    own = own_ref[...].astype(F32)
    g = None
    for s in range(p_ref.shape[0]):
        term = jnp.where(mine == s, own, p_ref[s].astype(F32))
        g = term if g is None else g + term
    return g


def adamw_sharded(parts, sums, my_chip, w, m, v, name):
    R, C = w.shape
    S = parts.shape[0]
    tr = _row_tile(R)

    def body(mc_ref, p_ref, o_ref, w_ref, m_ref, v_ref, g_ref, d_ref, nm_ref, nv_ref):
        g = _sum_partials(p_ref, o_ref, mc_ref[0])
        delta, nm, nv = _adamw_math(w_ref[...], g, m_ref[...], v_ref[...])
        g_ref[...] = g
        d_ref[...] = delta
        nm_ref[...] = nm
        nv_ref[...] = nv

    row = pl.BlockSpec((tr, C), lambda i, mc: (i, 0))
    shp = jax.ShapeDtypeStruct((R, C), F32)
    grid_spec = pltpu.PrefetchScalarGridSpec(
        num_scalar_prefetch=1, grid=(R // tr,),
        in_specs=[pl.BlockSpec((S, tr, C), lambda i, mc: (0, i, 0)),
                  pl.BlockSpec((None, tr, C), lambda i, mc: (mc[0], i, 0)), row, row, row],
        out_specs=[row, row, row, row])
    return pl.pallas_call(
        body, name=name, grid_spec=grid_spec, out_shape=[shp, shp, shp, shp],
        compiler_params=_params(("parallel",)),
    )(my_chip.reshape(1).astype(jnp.int32), parts, sums, w, m, v)


def adamw_small(parts, w, m, v, name):
    R, C = w.shape

    def body(p_ref, w_ref, m_ref, v_ref, g_ref, d_ref, nm_ref, nv_ref):
        g = p_ref[0]
        for s in range(1, N_DEV):
            g = g + p_ref[s]
        delta, nm, nv = _adamw_math(w_ref[...], g, m_ref[...], v_ref[...])
        g_ref[...] = g
        d_ref[...] = delta
        nm_ref[...] = nm
        nv_ref[...] = nv

    shp = jax.ShapeDtypeStruct((R, C), F32)
    return pl.pallas_call(
        body, name=name, out_shape=[shp, shp, shp, shp], compiler_params=_params(None),
    )(parts, w, m, v)


_ROW_NORM_FFN1, _ROW_NORM_MIX, _ROW_NORM_FFN2, _ROW_NORM_PLE, _ROW_NORM_FINAL = 0, 1, 2, 3, 4
_ROW_B_F, _ROW_REL, _ROW_LOSS, _SMALL_ROWS = 5, 6, 7, 8


def _pack_small(D, norm_ffn1, norm_mix, norm_ffn2, norm_ple, norm_final, b_f, rel_table):
    def row(v):
        v = v.reshape(1, -1)
        return jnp.pad(v, ((0, 0), (0, D - v.shape[1])))
    return jnp.concatenate([row(norm_ffn1), row(norm_mix), row(norm_ffn2), row(norm_ple),
                            row(norm_final), row(b_f), row(rel_table),
                            jnp.zeros((1, D), F32)], axis=0)


def _unpack_small(a, shapes):
    return {"norm_ffn1": a[_ROW_NORM_FFN1].reshape(shapes["norm_ffn1"]),
            "norm_mix": a[_ROW_NORM_MIX].reshape(shapes["norm_mix"]),
            "b_f": a[_ROW_B_F, :N_FOX].reshape(shapes["b_f"]),
            "norm_ffn2": a[_ROW_NORM_FFN2].reshape(shapes["norm_ffn2"]),
            "norm_ple": a[_ROW_NORM_PLE].reshape(shapes["norm_ple"]),
            "rel_table": a[_ROW_REL, :N_REL_BUCKETS * N_DIL].reshape(shapes["rel_table"]),
            "norm_final": a[_ROW_NORM_FINAL].reshape(shapes["norm_final"])}


def local_step(x, p, tgt, g_ffn1, g_mix, g_ffn2, g_ple, g_final, b_f, rel_table,
               forward, weights, emit, emit2, dw_theirs, dw_mine, first_dep):
    T, D = x.shape
    P = p.shape[1]
    CW = D // N_DEV
    tq = _tile(T, 256)

    h1 = rms_fwd(x, g_ffn1, "rms_ffn1", dep=first_dep)
    tiles = bias_tiles(rel_table, T, tq)
    forward("ffn1_g", [tiles, h1])
    wg1, = weights("ffn1_g", h1)
    gate1 = ffn_gate(h1, wg1, "ffn1_gate")
    forward("ffn1_u", gate1)
    wu1, = weights("ffn1_u", gate1)
    a1, b1, s1 = ffn_up_gated(h1, wu1, gate1, "ffn1_up")
    forward("ffn1_d", s1)
    wd1, = weights("ffn1_d", s1)
    x1, h2 = ffn_down(s1, wd1, 0, 1, x, g_mix, "ffn1_down")

    forward("mix_in", h2)
    w3, wf = weights("mix_in", h2)
    qkv = mm_nt([(h2, w3)], "mix_qkv", tn=768, out_dtype=BF16)
    uf = mm_nt([(h2, wf)], "mix_forget", tn=LANES, out_dtype=F32)
    bfp = jnp.pad(b_f.reshape(1, N_FOX), ((0, 0), (0, LANES - N_FOX)))
    c, ct = fox_gate_fwd(uf, bfp, "fox_gate")
    cat, lse = attention_fwd(qkv, c, ct, tiles, "attention")
    forward("mix_out", cat)
    wo, = weights("mix_out", cat)
    x2 = mm_nn(cat, wo, "mix_out", tn=512, out_dtype=F32, res=x1)

    h3 = rms_fwd(x2, g_ffn2, "rms_ffn2")
    forward("ffn2_gu", h3)
    wgu2, = weights("ffn2_gu", h3)
    a2, b2, s2 = ffn_up(h3, wgu2, 0, 1, 2, "ffn2_up")
    forward("ffn2_d", s2)
    wd2, = weights("ffn2_d", s2)
    x3, h4 = ffn_down(s2, wd2, 0, 1, x2, g_ple, "ffn2_down")
    forward("ple", x3)

    wpg, wpp = weights("ple", h4)
    z = mm_nn(h4, wpg, "ple_gate", tn=512, out_dtype=F32)
    pp = mm_nn(p, wpp, "ple_proj", tn=CW, tm=T, out_dtype=F32, n_out=D,
               b_block=(P, CW), b_map=lambda n, i: (n, 0))
    loss_row, dx4, dg_final, dz, dpp = ple_loss(x3, z, pp, g_final, tgt, "ple_loss")

    grads = {}
    grads["w_ple_proj"] = mm_tn(p, dpp, "ple_proj_dw", grid=(N_DEV,),
                                a_block=(T, P), a_map=lambda n: (0, 0),
                                b_block=(T, CW), b_map=lambda n: (0, n),
                                o_block=(P, CW), o_map=lambda n: (n, 0),
                                out_shape=(N_DEV * P, CW))
    grads["w_ple_gate"] = mm_tn_plain(h4, dz, "ple_gate_dw")
    tok = emit("ple", grads)
    dh4 = mm_nt([(dz, wpg)], "ple_gate_dh", tn=512, out_dtype=F32, dep=tok)
    tok = emit2("ple", dh4)
    dx3, dx3h, dg_ple = rms_bwd(dh4, x3, g_ple, dx4, "rms_ple_bwd", dep=tok)

    da2, db2 = ffn_bwd_act(dx3h, wd2, 0, 1, a2, b2, "ffn2_bwd_act")
    tok = dw_theirs("ffn2_w_down", s2, dx3h)
    tok = dw_theirs("ffn2_w_gate", da2, h3, dep=tok)
    tok = dw_theirs("ffn2_w_up", db2, h3, dep=tok)
    tok = dw_mine("ffn2_w_down", s2, dx3h, dep=tok)
    tok = dw_mine("ffn2_w_gate", da2, h3, dep=tok)
    tok = dw_mine("ffn2_w_up", db2, h3, dep=tok)
    dh3 = ffn_bwd_dh(da2, db2, wgu2, wgu2, 0, 1, 2, D, "ffn2_bwd_dh", dep=tok)
    dx2, _, dg_ffn2 = rms_bwd(dh3, x2, g_ffn2, dx3, "rms_ffn2_bwd", half=False)

    dcat = mm_nt([(dx2, wo)], "mix_out_dh", tn=512, out_dtype=BF16)
    grads["w_o"] = mm_tn_plain(cat, dx2, "mix_out_dw")
    dq, dk, dv, dct, dtiles = attention_bwd(qkv, c, ct, tiles, lse, cat, dcat, "attention_bwd")
    dctp = jnp.pad(dct[:, 0, :], ((0, LANES - N_HEADS), (0, 0)))
    duf, dbf = fox_gate_bwd(dctp, uf, bfp, "fox_gate_bwd")
    drel = rel_table_grad(dtiles, T, "rel_table_grad")[:, 0, :N_REL_BUCKETS].T
    du3 = jnp.concatenate([dq, dk, dv], axis=1)
    grads["w3"] = mm_tn_plain(du3, h2, "mix_qkv_dw", tm=768)
    grads["wf"] = mm_tn_plain(duf, h2, "mix_forget_dw", tm=LANES)
    tok = emit("mix", grads)
    dh2 = mm_nn_sum([(du3, w3), (duf, wf)], "mix_in_dh", tn=512, out_dtype=F32, dep=tok)
    tok = emit2("mix", dh2)
    dx1, dx1h, dg_mix = rms_bwd(dh2, x1, g_mix, dx2, "rms_mix_bwd", dep=tok)

    da1, db1 = ffn_bwd_act(dx1h, wd1, 0, 1, a1, b1, "ffn1_bwd_act")
    tok = dw_theirs("ffn1_w_down", s1, dx1h)
    tok = dw_theirs("ffn1_w_gate", da1, h1, dep=tok)
    tok = dw_mine("ffn1_w_down", s1, dx1h, dep=tok)
    tok = dw_theirs("ffn1_w_up", db1, h1, dep=tok)
    tok = dw_mine("ffn1_w_gate", da1, h1, dep=tok)
    tok = dw_mine("ffn1_w_up", db1, h1, dep=tok)
    dh1 = ffn_bwd_dh(da1, db1, wg1, wu1, 0, 0, 1, D, "ffn1_bwd_dh", dep=tok)
    dx0, _, dg_ffn1 = rms_bwd(dh1, x, g_ffn1, dx1, "rms_ffn1_bwd", half=False)

    small = _pack_small(D, dg_ffn1, dg_mix, dg_ffn2, dg_ple, dg_final, dbf[:, :N_FOX], drel)
    small = small.at[_ROW_LOSS, :LANES].set(loss_row[0])
    grads["small"] = small
    emit("small", grads)
    return dx0


def _split_w_in(w_in_t):
    df, dd = N_FOX * HEAD_DIM, N_DIL * HEAD_DIM
    o = np.cumsum([0, df, df, df, N_FOX, dd, dd, dd]).tolist()
    qa, ka, va, f, qb, kb, vb = [w_in_t[o[i]:o[i + 1]] for i in range(7)]
    return jnp.concatenate([qa, qb, ka, kb, va, vb], axis=0), f


def _join_w_in(d3, dfg):
    df, dd = N_FOX * HEAD_DIM, N_DIL * HEAD_DIM
    o = np.cumsum([0, df, dd, df, dd, df, dd]).tolist()
    qa, qb, ka, kb, va, vb = [d3[o[i]:o[i + 1]] for i in range(6)]
    return jnp.concatenate([qa, ka, va, dfg, qb, kb, vb], axis=0)


def rows_to_bf16(a3, name, dep=None):
    R, _, C = a3.shape
    tc = _tile(C, 512)

    def body(a_ref, *rest):
        rest[-1][...] = a_ref[...].astype(BF16)

    in_specs = [pl.BlockSpec((R, None, tc), lambda n: (0, 0, n))]
    args = [a3]
    if dep is not None:
        in_specs.append(_dep_spec(1))
        args.append(dep)
    return pl.pallas_call(
        body, name=name, grid=(C // tc,), in_specs=in_specs,
        out_specs=pl.BlockSpec((R, tc), lambda n: (0, n)),
        out_shape=jax.ShapeDtypeStruct((R, C), BF16),
        compiler_params=_params(("parallel",)),
    )(*args)


def adamw_rows3d(parts, sums, my_chip, w3, m3, v3, name):
    R, _, C = w3.shape
    S = parts.shape[0]
    tc = _tile(C, 256)

    def body(mc_ref, p_ref, o_ref, w_ref, m_ref, v_ref, g_ref, d_ref, nm_ref, nv_ref):
        g = _sum_partials(p_ref, o_ref, mc_ref[0])
        delta, nm, nv = _adamw_math(w_ref[...], g, m_ref[...], v_ref[...])
        g_ref[...] = g
        d_ref[...] = delta
        nm_ref[...] = nm
        nv_ref[...] = nv

    col = pl.BlockSpec((R, None, tc), lambda n, mc: (0, 0, n))
    shp = jax.ShapeDtypeStruct((R, 1, C), F32)
    grid_spec = pltpu.PrefetchScalarGridSpec(
        num_scalar_prefetch=1, grid=(C // tc,),
        in_specs=[pl.BlockSpec((S, R, tc), lambda n, mc: (0, 0, n)),
                  pl.BlockSpec((None, R, tc), lambda n, mc: (mc[0], 0, n)), col, col, col],
        out_specs=[col, col, col, col])
    return pl.pallas_call(
        body, name=name, grid_spec=grid_spec, out_shape=[shp, shp, shp, shp],
        compiler_params=_params(("parallel",)),
    )(my_chip.reshape(1).astype(jnp.int32), parts, sums, w3, m3, v3)


def kernel(x, p, norm_ffn1, ffn1_w_gate, ffn1_w_up, ffn1_w_down, norm_mix, w_in, b_f, w_o, norm_ffn2, ffn2_w_gate, ffn2_w_up, ffn2_w_down, norm_ple, w_ple_gate, w_ple_proj, rel_table, norm_final, loss_target, m_norm_ffn1, m_ffn1_w_gate, m_ffn1_w_up, m_ffn1_w_down, m_norm_mix, m_w_in, m_b_f, m_w_o, m_norm_ffn2, m_ffn2_w_gate, m_ffn2_w_up, m_ffn2_w_down, m_norm_ple, m_w_ple_gate, m_w_ple_proj, m_rel_table, m_norm_final, v_norm_ffn1, v_ffn1_w_gate, v_ffn1_w_up, v_ffn1_w_down, v_norm_mix, v_w_in, v_b_f, v_w_o, v_norm_ffn2, v_ffn2_w_gate, v_ffn2_w_up, v_ffn2_w_down, v_norm_ple, v_w_ple_gate, v_w_ple_proj, v_rel_table, v_norm_final):
    names = ["norm_ffn1", "ffn1_w_gate", "ffn1_w_up", "ffn1_w_down", "norm_mix", "w_in", "b_f", "w_o",
             "norm_ffn2", "ffn2_w_gate", "ffn2_w_up", "ffn2_w_down", "norm_ple", "w_ple_gate",
             "w_ple_proj", "rel_table", "norm_final"]
    w = dict(zip(names, [norm_ffn1, ffn1_w_gate, ffn1_w_up, ffn1_w_down, norm_mix, w_in, b_f, w_o,
                         norm_ffn2, ffn2_w_gate, ffn2_w_up, ffn2_w_down, norm_ple, w_ple_gate,
                         w_ple_proj, rel_table, norm_final]))
    m = dict(zip(names, [m_norm_ffn1, m_ffn1_w_gate, m_ffn1_w_up, m_ffn1_w_down, m_norm_mix, m_w_in,
                         m_b_f, m_w_o, m_norm_ffn2, m_ffn2_w_gate, m_ffn2_w_up, m_ffn2_w_down,
                         m_norm_ple, m_w_ple_gate, m_w_ple_proj, m_rel_table, m_norm_final]))
    v = dict(zip(names, [v_norm_ffn1, v_ffn1_w_gate, v_ffn1_w_up, v_ffn1_w_down, v_norm_mix, v_w_in,
                         v_b_f, v_w_o, v_norm_ffn2, v_ffn2_w_gate, v_ffn2_w_up, v_ffn2_w_down,
                         v_norm_ple, v_w_ple_gate, v_w_ple_proj, v_rel_table, v_norm_final]))
    sharded = ["ffn1_w_gate", "ffn1_w_up", "ffn1_w_down", "w_in", "w_o", "ffn2_w_gate", "ffn2_w_up",
               "ffn2_w_down", "w_ple_gate", "w_ple_proj"]
    small_names = [n for n in names if n not in sharded]

    xs, ps, tgt = x[0], p[0, 0], loss_target[0]
    T, D = xs.shape
    transposed = ("ffn1_w_gate", "ffn1_w_up", "ffn2_w_gate", "ffn2_w_up")

    def view(t, n):
        if n in transposed:
            return t[n][0].T
        if n == "w_in":
            return jnp.transpose(t[n], (2, 0, 1))
        return t[n][0]

    def unview(a, n):
        if n in transposed:
            return a.T.reshape(w[n].shape)
        if n == "w_in":
            return jnp.transpose(a, (1, 2, 0))
        return a.reshape(w[n].shape)

    sh = {n: view(w, n) for n in sharded}
    m_sh = {n: view(m, n) for n in sharded}
    v_sh = {n: view(v, n) for n in sharded}
    F8 = sh["ffn1_w_down"].shape[0]
    WIN8 = sh["w_in"].shape[0]
    me = 4 * lax.axis_index("x") + 2 * lax.axis_index("y") + lax.axis_index("c")

    def start(groups, name, after=None):
        srcs = [s for grp in groups for s in grp]
        sems, bufs, token = gather_start(srcs, [_landing(s, me) for s in srcs], name, after=after)
        return sems, bufs[:len(srcs)], bufs[len(srcs):], token

    cat0 = lambda ns, z: (jnp.concatenate([sh[n] for n in ns], axis=0) + z).astype(BF16)
    sems_a, srcs_a, lands_a, token_a = start(
        [[sh["ffn1_w_gate"].astype(BF16)], [sh["ffn1_w_up"].astype(BF16)], [sh["ffn1_w_down"].astype(BF16)]],
        "gather_start_ffn1")
    zero = token_a[0, 0]
    w_in_bf = rows_to_bf16(sh["w_in"], "w_in_bf16", dep=token_a)
    sems_b, srcs_b, lands_b, g_token = start(
        [[w_in_bf, (sh["w_o"] + zero).astype(BF16)],
         [cat0(["ffn2_w_gate", "ffn2_w_up"], zero), (sh["ffn2_w_down"] + zero).astype(BF16)],
         [(sh["w_ple_gate"] + zero).astype(BF16), (sh["w_ple_proj"] + zero).astype(BF16)]],
        "gather_start_rest", after=token_a)
    order = ["ffn1_g", "ffn1_u", "ffn1_d", "mix_in", "mix_out", "ffn2_gu", "ffn2_d", "ple"]
    group_sizes = [1, 1, 1, 1, 1, 1, 1, 2]
    g_sems, g_srcs, g_lands = sems_a + sems_b, srcs_a + srcs_b, lands_a + lands_b
    g_send, g_recv_d, g_recv_i = g_sems[0::3], g_sems[1::3], g_sems[2::3]
    first = np.cumsum([0] + group_sizes).tolist()
    passed = {}

    def arrays_of(group):
        k = order.index(group)
        return slice(first[k], first[k + 1])

    def forward(group, after):
        sl = arrays_of(group)
        f_sems, bufs = gather_forward(g_srcs[sl], g_lands[sl], g_recv_i[sl], after, "gather_forward_" + group)
        k = len(bufs) // 2
        passed[group] = (f_sems[0::2], f_sems[1::2], bufs[:k], bufs[k:])

    def weights(group, after):
        sl = arrays_of(group)
        f_send, f_recv, srcs, lands = passed[group]
        got = gather_wait(srcs, lands, g_send[sl], g_recv_d[sl], f_send, f_recv, after, "gather_wait_" + group)
        if group in ("ffn1_g", "ffn1_u", "ffn1_d"):
            return (got[0].reshape(N_DEV * F8, D),)
        if group == "ffn2_gu":
            return (got[0].reshape(N_DEV * 2 * F8, D),)
        if group in ("ffn2_d", "mix_out"):
            return (got[0].reshape(-1, D),)
        if group == "ple":
            return got[0].reshape(-1, D), got[1].reshape(-1, got[1].shape[2])
        w3, wf8 = _split_w_in(got[0].reshape(N_DEV * WIN8, D))
        return w3, jnp.pad(wf8, ((0, LANES - N_FOX), (0, 0)))

    ffn_names = ["ffn2_w_down", "ffn2_w_gate", "ffn2_w_up", "ffn1_w_down", "ffn1_w_gate", "ffn1_w_up"]
    scatter_groups = {"ple": ["w_ple_gate", "w_ple_proj"], "mix": ["w_in", "w_o"]}
    scatter_groups.update({n: [n] for n in ffn_names})
    x_i, y_i, c_i = _xyc()
    my_chip = 2 * x_i + y_i
    pair_stage, chip_stage, small_stage = {}, {}, {}

    def emit(group, grads, after=None):
        if group == "small":
            src = grads["small"]
            ss, rs, srcs, lands, token = exchange_start([src], [_landing(src, me)], [False], "scatter_start_small")
            small_stage["small"] = (ss, rs, srcs, lands)
            return token
        src4s = []
        for n in scatter_groups[group]:
            if n == "w_in":
                full = _join_w_in(grads["w3"], grads["wf"][:N_FOX])
                src4s.append(full.reshape(N_CHIPS, 2, WIN8, D))
            else:
                src4s.append(grads[n].reshape((N_CHIPS, 2) + sh[n].shape))
        lands = [lax.empty((N_CHIPS,) + s.shape[2:], BF16) for s in src4s]
        sems, bufs, token = scatter_pair_start(src4s, lands, "scatter_pair_start_" + group, after=after)
        k = len(src4s)
        pair_stage[group] = (sems, bufs[:k], bufs[k:])
        return token

    def emit2(group, after):
        sems, src4s, lands = pair_stage[group]
        src4s, lands = scatter_pair_wait(src4s, lands, sems, after, "scatter_pair_wait_" + group)
        sums = [chip_sum(s4, la, c_i, "chip_sum_" + n)
                for s4, la, n in zip(src4s, lands, scatter_groups[group])]
        chip_lands = [lax.empty(s.shape, s.dtype) for s in sums]
        sems, bufs, token = scatter_chip_start(sums, chip_lands, "scatter_chip_start_" + group)
        k = len(sums)
        chip_stage[group] = (sems, bufs[:k], bufs[k:])
        return token

    def dw_theirs(n, act, other, dep=None):
        theirs = ffn_dw_side(act, other, c_i, False, n + "_dw_theirs", dep=dep)
        theirs = theirs.reshape((N_CHIPS,) + sh[n].shape)
        sems, bufs, token = scatter_pair_start([theirs], [lax.empty(theirs.shape, BF16)],
                                               "scatter_pair_start_" + n, whole=True)
        pair_stage[n] = (sems, bufs[:1], bufs[1:])
        return token

    def dw_mine(n, act, other, dep=None):
        sems, theirs, lands = pair_stage[n]
        _, lands = scatter_pair_wait(theirs, lands, sems, dep, "scatter_pair_wait_" + n, whole=True)
        sums = ffn_dw_side(act, other, c_i, True, n + "_dw_mine", add=lands[0].reshape(-1, D))
        sums = sums.reshape((N_CHIPS,) + sh[n].shape)
        sems, bufs, token = scatter_chip_start([sums], [lax.empty(sums.shape, BF16)], "scatter_chip_start_" + n)
        chip_stage[n] = (sems, bufs[:1], bufs[1:])
        return token

    dx0 = local_step(
        xs, ps, tgt, w["norm_ffn1"], w["norm_mix"], w["norm_ffn2"], w["norm_ple"],
        w["norm_final"].reshape(1, D), w["b_f"], w["rel_table"], forward, weights, emit, emit2,
        dw_theirs, dw_mine, g_token)

    res = {}
    after = dx0
    for group in ["ple"] + ffn_names[:3] + ["mix"] + ffn_names[3:]:
        sems, sums, chip_lands = chip_stage[group]
        sums, parts = scatter_chip_wait(sums, chip_lands, sems, after, "scatter_chip_wait_" + group)
        for n, part, own in zip(scatter_groups[group], parts, sums):
            update = adamw_rows3d if n == "w_in" else adamw_sharded
            g, d, nm, nv = update(part, own, my_chip, sh[n], m_sh[n], v_sh[n], "adamw_" + n)
            res[n] = tuple(unview(a, n) for a in (g, d, nm, nv))
            after = g
    ss, rs, srcs, lands = small_stage["small"]
    small_parts, = exchange_wait(ss, rs, srcs, lands, [False], after, "scatter_wait_small")
    pack = lambda t: _pack_small(D, t["norm_ffn1"], t["norm_mix"], t["norm_ffn2"], t["norm_ple"],
                                 t["norm_final"], t["b_f"], t["rel_table"])
    gs, ds, ms, vs = adamw_small(small_parts, pack(w), pack(m), pack(v), "adamw_small")
    shapes = {n: w[n].shape for n in small_names}
    unpacked = [_unpack_small(a, shapes) for a in (gs, ds, ms, vs)]
    for n in small_names:
        res[n] = tuple(u[n] for u in unpacked)
    loss = gs[_ROW_LOSS, 0]

    out = [loss, dx0.reshape(x.shape)]
    for k in range(4):
        out += [res[n][k] for n in names]
    return tuple(out)
```

```python
import functools
import math

import numpy as np
import jax
import jax.numpy as jnp
from jax import lax
from jax.experimental import pallas as pl
from jax.experimental.pallas import tpu as pltpu

F32 = jnp.float32
BF16 = jnp.bfloat16

N_DEV = 8
HEAD_DIM = 128
N_FOX = 8
N_DIL = 8
N_HEADS = N_FOX + N_DIL
DILATED_PATTERNS = ((128, 1), (512, 4), (2048, 16))
N_REL_BUCKETS = 32
REL_MAX_DISTANCE = 2048
RMS_EPS = 1e-6
NEG_INF = -1e30
LANES = 128
VMEM_LIMIT = 56 * 1024 * 1024

ADAM_LR = 0.001
ADAM_B1 = 0.9
ADAM_B2 = 0.999
ADAM_EPS = 1e-08
ADAM_WD = 0.01
ADAM_STEP = 10

MESH = pl.DeviceIdType.MESH


def _params(sem):
    return pltpu.CompilerParams(dimension_semantics=sem, vmem_limit_bytes=VMEM_LIMIT)


def _dot(a, b, ca, cb, precision=None):
    return lax.dot_general(a, b, (((ca,), (cb,)), ((), ())),
                           preferred_element_type=F32, precision=precision)


def _sigmoid(z):
    return 1.0 / (1.0 + jnp.exp(-z))


def _tile(n, want):
    t = min(n, want)
    assert n % t == 0, (n, t)
    return t


def _dep_spec(ngrid):
    return pl.BlockSpec((8, LANES), lambda *_: (0, 0))


def rms_fwd(x, g, name, dep=None):
    T, D = x.shape
    tm = _tile(T, 256)

    def body(x_ref, g_ref, *rest):
        h_ref = rest[-1]
        xv = x_ref[...]
        r = lax.rsqrt(jnp.mean(xv * xv, axis=-1, keepdims=True) + RMS_EPS)
        h_ref[...] = (xv * r * g_ref[...]).astype(BF16)

    in_specs = [pl.BlockSpec((tm, D), lambda i: (i, 0)), pl.BlockSpec((1, D), lambda i: (0, 0))]
    args = [x, g]
    if dep is not None:
        in_specs.append(_dep_spec(1))
        args.append(dep)
    return pl.pallas_call(
        body, name=name, grid=(T // tm,), in_specs=in_specs,
        out_specs=pl.BlockSpec((tm, D), lambda i: (i, 0)),
        out_shape=jax.ShapeDtypeStruct((T, D), BF16),
        compiler_params=_params(("parallel",)),
    )(*args)


def rms_bwd(dh, x, g, dres, name, dep=None, half=True):
    T, D = x.shape
    tm = _tile(T, 256)

    def body(dh_ref, x_ref, g_ref, dres_ref, *rest):
        dx_ref, dg_ref = (rest[-3], rest[-1]) if half else (rest[-2], rest[-1])
        i = pl.program_id(0)
        xv = x_ref[...]
        r = lax.rsqrt(jnp.mean(xv * xv, axis=-1, keepdims=True) + RMS_EPS)
        xh = xv * r
        d = dh_ref[...]
        u = d * g_ref[...]
        dx = dres_ref[...] + r * (u - xh * jnp.mean(u * xh, axis=-1, keepdims=True))
        dx_ref[...] = dx
        if half:
            rest[-2][...] = (0.5 * dx).astype(BF16)
        part = jnp.sum(d * xh, axis=0, keepdims=True)

        @pl.when(i == 0)
        def _():
            dg_ref[...] = part

        @pl.when(i > 0)
        def _():
            dg_ref[...] += part

    row = pl.BlockSpec((tm, D), lambda i: (i, 0))
    vec = pl.BlockSpec((1, D), lambda i: (0, 0))
    in_specs = [row, row, vec, row]
    args = [dh, x, g, dres]
    if dep is not None:
        in_specs.append(_dep_spec(1))
        args.append(dep)
    out_specs = [row, row, vec] if half else [row, vec]
    out_shape = [jax.ShapeDtypeStruct((T, D), F32)] + ([jax.ShapeDtypeStruct((T, D), BF16)] if half else [])
    out_shape.append(jax.ShapeDtypeStruct((1, D), F32))
    outs = pl.pallas_call(
        body, name=name, grid=(T // tm,),
        in_specs=in_specs, out_specs=out_specs, out_shape=out_shape,
        compiler_params=_params(("arbitrary",)),
    )(*args)
    return tuple(outs) if half else (outs[0], None, outs[1])


def ple_loss(x, z, pp, g, target, name):
    T, D = x.shape
    tm = _tile(T, 256)

    def body(x_ref, z_ref, p_ref, g_ref, t_ref, loss_ref, dx_ref, dg_ref, dz_ref, dp_ref):
        i = pl.program_id(0)
        gate = _sigmoid(z_ref[...])
        ppv = p_ref[...]
        xv = x_ref[...] + gate * ppv
        gv = g_ref[...]
        r = lax.rsqrt(jnp.mean(xv * xv, axis=-1, keepdims=True) + RMS_EPS)
        xh = xv * r
        e = xh * gv - t_ref[...]
        lpart = 0.5 * jnp.sum(jnp.mean(e * e, axis=-1, keepdims=True), axis=0, keepdims=True)
        lrow = jnp.broadcast_to(lpart, (1, LANES))
        d = e * (1.0 / D)
        u = d * gv
        dx = r * (u - xh * jnp.mean(u * xh, axis=-1, keepdims=True))
        dx_ref[...] = dx
        dz_ref[...] = (dx * ppv * gate * (1.0 - gate)).astype(BF16)
        dp_ref[...] = (dx * gate).astype(BF16)
        part = jnp.sum(d * xh, axis=0, keepdims=True)

        @pl.when(i == 0)
        def _():
            dg_ref[...] = part
            loss_ref[...] = lrow

        @pl.when(i > 0)
        def _():
            dg_ref[...] += part
            loss_ref[...] += lrow

    row = pl.BlockSpec((tm, D), lambda i: (i, 0))
    vec = pl.BlockSpec((1, D), lambda i: (0, 0))
    return pl.pallas_call(
        body, name=name, grid=(T // tm,),
        in_specs=[row, row, row, vec, row],
        out_specs=[pl.BlockSpec((1, LANES), lambda i: (0, 0)), row, vec, row, row],
        out_shape=[jax.ShapeDtypeStruct((1, LANES), F32), jax.ShapeDtypeStruct((T, D), F32),
                   jax.ShapeDtypeStruct((1, D), F32), jax.ShapeDtypeStruct((T, D), BF16),
                   jax.ShapeDtypeStruct((T, D), BF16)],
        compiler_params=_params(("arbitrary",)),
    )(x, z, pp, g, target)


def _bf(v, scale=None):
    if scale is not None:
        v = v * scale
    return v.astype(BF16)


def mm_nn(a, b, name, *, tn, out_dtype, tm=512, n_out=None, b_block=None, b_map=None,
          res=None):
    T, K = a.shape
    N = n_out if n_out is not None else b.shape[1]
    tm = _tile(T, tm)
    tn = _tile(N, tn)
    b_block = b_block or (K, tn)
    b_map = b_map or (lambda n, i: (0, n))

    def body(*refs):
        a_ref, b_ref = refs[0], refs[1]
        o_ref = refs[-1]
        acc = _dot(_bf(a_ref[...]), _bf(b_ref[...]), 1, 0)
        if res is not None:
            acc = refs[2][...] + acc
        o_ref[...] = acc.astype(out_dtype)

    in_specs = [pl.BlockSpec((tm, K), lambda n, i: (i, 0)), pl.BlockSpec(b_block, b_map)]
    args = [a, b]
    if res is not None:
        in_specs.append(pl.BlockSpec((tm, tn), lambda n, i: (i, n)))
        args.append(res)
    return pl.pallas_call(
        body, name=name, grid=(N // tn, T // tm), in_specs=in_specs,
        out_specs=pl.BlockSpec((tm, tn), lambda n, i: (i, n)),
        out_shape=jax.ShapeDtypeStruct((T, N), out_dtype),
        compiler_params=_params(("parallel", "parallel")),
    )(*args)


def mm_nn_sum(pairs, name, *, tn, out_dtype, tm=512, dep=None):
    T = pairs[0][0].shape[0]
    N = pairs[0][1].shape[1]
    tm = _tile(T, tm)
    tn = _tile(N, tn)
    npair = len(pairs)

    def body(*refs):
        acc = None
        for q in range(npair):
            part = _dot(_bf(refs[2 * q][...]), _bf(refs[2 * q + 1][...]), 1, 0)
            acc = part if acc is None else acc + part
        refs[-1][...] = acc.astype(out_dtype)

    in_specs, args = [], []
    for a, b in pairs:
        K = a.shape[1]
        in_specs += [pl.BlockSpec((tm, K), lambda n, i: (i, 0)), pl.BlockSpec((K, tn), lambda n, i: (0, n))]
        args += [a, b]
    if dep is not None:
        in_specs.append(_dep_spec(2))
        args.append(dep)
    return pl.pallas_call(
        body, name=name, grid=(N // tn, T // tm), in_specs=in_specs,
        out_specs=pl.BlockSpec((tm, tn), lambda n, i: (i, n)),
        out_shape=jax.ShapeDtypeStruct((T, N), out_dtype),
        compiler_params=_params(("parallel", "parallel")),
    )(*args)


def mm_nt(pairs, name, *, tn, out_dtype, tm=512, dep=None):
    T = pairs[0][0].shape[0]
    N = pairs[0][1].shape[0]
    tm = _tile(T, tm)
    tn = _tile(N, tn)
    npair = len(pairs)

    def body(*refs):
        o_ref = refs[-1]
        acc = None
        for q in range(npair):
            part = _dot(_bf(refs[2 * q][...]), _bf(refs[2 * q + 1][...]), 1, 1)
            acc = part if acc is None else acc + part
        o_ref[...] = acc.astype(out_dtype)

    in_specs, args = [], []
    for a, b in pairs:
        K = a.shape[1]
        in_specs += [pl.BlockSpec((tm, K), lambda n, i: (i, 0)), pl.BlockSpec((tn, K), lambda n, i: (n, 0))]
        args += [a, b]
    if dep is not None:
        in_specs.append(_dep_spec(2))
        args.append(dep)
    return pl.pallas_call(
        body, name=name, grid=(N // tn, T // tm), in_specs=in_specs,
        out_specs=pl.BlockSpec((tm, tn), lambda n, i: (i, n)),
        out_shape=jax.ShapeDtypeStruct((T, N), out_dtype),
        compiler_params=_params(("parallel", "parallel")),
    )(*args)


def mm_tn(a, b, name, *, grid, a_block, a_map, b_block, b_map, o_block, o_map, out_shape,
          b_scale=None, dep=None):
    def body(a_ref, b_ref, *rest):
        rest[-1][...] = _dot(_bf(a_ref[...]), _bf(b_ref[...], b_scale), 0, 0).astype(BF16)

    in_specs = [pl.BlockSpec(a_block, a_map), pl.BlockSpec(b_block, b_map)]
    args = [a, b]
    if dep is not None:
        in_specs.append(_dep_spec(len(grid)))
        args.append(dep)
    return pl.pallas_call(
        body, name=name, grid=grid, in_specs=in_specs,
        out_specs=pl.BlockSpec(o_block, o_map),
        out_shape=jax.ShapeDtypeStruct(out_shape, BF16),
        compiler_params=_params(("parallel",) * len(grid)),
    )(*args)


def mm_tn_plain(a, b, name, *, tm=512, tn=512, b_scale=None):
    T, M = a.shape
    N = b.shape[1]
    tm = _tile(M, tm)
    tn = _tile(N, tn)
    return mm_tn(a, b, name, grid=(M // tm, N // tn),
                 a_block=(T, tm), a_map=lambda m, n: (0, m),
                 b_block=(T, tn), b_map=lambda m, n: (0, n),
                 o_block=(tm, tn), o_map=lambda m, n: (m, n),
                 out_shape=(M, N), b_scale=b_scale)


def ffn_up(h, wgu, gi, ui, nper, name):
    T, D = h.shape
    F8 = wgu.shape[0] // (N_DEV * nper)
    tm = _tile(T, 512)
    nt = T // tm

    def body(h_ref, wg_ref, wu_ref, ga_ref, gb_ref, s_ref):
        hv = h_ref[...]
        a = _dot(hv, wg_ref[...], 1, 1)
        b = _dot(hv, wu_ref[...], 1, 1)
        sg = _sigmoid(a)
        silu = a * sg
        ga_ref[...] = (b * (sg * (1.0 + a * (1.0 - sg)))).astype(BF16)
        gb_ref[...] = silu.astype(BF16)
        s_ref[...] = (silu * b).astype(BF16)

    blk = pl.BlockSpec((tm, F8), lambda j, i: (j * nt + i, 0))
    shp = jax.ShapeDtypeStruct((N_DEV * T, F8), BF16)
    return pl.pallas_call(
        body, name=name, grid=(N_DEV, nt),
        in_specs=[pl.BlockSpec((tm, D), lambda j, i: (i, 0)),
                  pl.BlockSpec((F8, D), lambda j, i: (j * nper + gi, 0)),
                  pl.BlockSpec((F8, D), lambda j, i: (j * nper + ui, 0))],
        out_specs=[blk, blk, blk], out_shape=[shp, shp, shp],
        compiler_params=_params(("parallel", "parallel")),
    )(h, wgu, wgu)


def ffn_gate(h, wg, name):
    T, D = h.shape
    F8 = wg.shape[0] // N_DEV
    tm = _tile(T, 512)
    nt = T // tm

    def body(h_ref, wg_ref, a_ref):
        a_ref[...] = _dot(h_ref[...], wg_ref[...], 1, 1).astype(BF16)

    return pl.pallas_call(
        body, name=name, grid=(N_DEV, nt),
        in_specs=[pl.BlockSpec((tm, D), lambda j, i: (i, 0)), pl.BlockSpec((F8, D), lambda j, i: (j, 0))],
        out_specs=pl.BlockSpec((tm, F8), lambda j, i: (j * nt + i, 0)),
        out_shape=jax.ShapeDtypeStruct((N_DEV * T, F8), BF16),
        compiler_params=_params(("parallel", "parallel")),
    )(h, wg)


def ffn_up_gated(h, wu, a, name):
    T, D = h.shape
    F8 = wu.shape[0] // N_DEV
    tm = _tile(T, 512)
    nt = T // tm

    def body(h_ref, wu_ref, a_ref, ga_ref, gb_ref, s_ref):
        av = a_ref[...].astype(F32)
        b = _dot(h_ref[...], wu_ref[...], 1, 1)
        sg = _sigmoid(av)
        silu = av * sg
        ga_ref[...] = (b * (sg * (1.0 + av * (1.0 - sg)))).astype(BF16)
        gb_ref[...] = silu.astype(BF16)
        s_ref[...] = (silu * b).astype(BF16)

    blk = pl.BlockSpec((tm, F8), lambda j, i: (j * nt + i, 0))
    shp = jax.ShapeDtypeStruct((N_DEV * T, F8), BF16)
    return pl.pallas_call(
        body, name=name, grid=(N_DEV, nt),
        in_specs=[pl.BlockSpec((tm, D), lambda j, i: (i, 0)), pl.BlockSpec((F8, D), lambda j, i: (j, 0)), blk],
        out_specs=[blk, blk, blk], out_shape=[shp, shp, shp],
        compiler_params=_params(("parallel", "parallel")),
    )(h, wu, a)


def ffn_down(s, wd, di, nper, x, g_next, name):
    T, D = x.shape
    F8 = s.shape[1]
    tm = _tile(T, 512)
    nt = T // tm

    def body(s_ref, w_ref, x_ref, g_ref, o_ref, h_ref, acc_ref):
        j = pl.program_id(1)
        part = _dot(s_ref[...], w_ref[...], 1, 0)

        @pl.when(j == 0)
        def _():
            acc_ref[...] = part

        @pl.when(j > 0)
        def _():
            acc_ref[...] += part

        @pl.when(j == N_DEV - 1)
        def _():
            xv = x_ref[...] + 0.5 * acc_ref[...]
            o_ref[...] = xv
            r = lax.rsqrt(jnp.mean(xv * xv, axis=-1, keepdims=True) + RMS_EPS)
            h_ref[...] = (xv * r * g_ref[...]).astype(BF16)

    row = pl.BlockSpec((tm, D), lambda i, j: (i, 0))
    return pl.pallas_call(
        body, name=name, grid=(nt, N_DEV),
        in_specs=[pl.BlockSpec((tm, F8), lambda i, j: (j * nt + i, 0)),
                  pl.BlockSpec((F8, D), lambda i, j: (j * nper + di, 0)),
                  row, pl.BlockSpec((1, D), lambda i, j: (0, 0))],
        out_specs=[row, row],
        out_shape=[jax.ShapeDtypeStruct((T, D), F32), jax.ShapeDtypeStruct((T, D), BF16)],
        scratch_shapes=[pltpu.VMEM((tm, D), F32)],
        compiler_params=_params(("parallel", "arbitrary")),
    )(s, wd, x, g_next)


def ffn_bwd_act(dxh, wd, di, nper_d, a, b, name, dep=None):
    T, D = dxh.shape
    F8 = a.shape[1]
    tm = _tile(T, 512)
    nt = T // tm

    def body(dx_ref, w_ref, a_ref, b_ref, *rest):
        da_ref, db_ref = rest[-2], rest[-1]
        ds = _dot(dx_ref[...], w_ref[...], 1, 1)
        da_ref[...] = (ds * a_ref[...].astype(F32)).astype(BF16)
        db_ref[...] = (ds * b_ref[...].astype(F32)).astype(BF16)

    blk = pl.BlockSpec((tm, F8), lambda j, i: (j * nt + i, 0))
    shp = jax.ShapeDtypeStruct((N_DEV * T, F8), BF16)
    in_specs = [pl.BlockSpec((tm, D), lambda j, i: (i, 0)),
                pl.BlockSpec((F8, D), lambda j, i: (j * nper_d + di, 0)), blk, blk]
    args = [dxh, wd, a, b]
    if dep is not None:
        in_specs.append(_dep_spec(2))
        args.append(dep)
    return pl.pallas_call(
        body, name=name, grid=(N_DEV, nt), in_specs=in_specs,
        out_specs=[blk, blk], out_shape=[shp, shp],
        compiler_params=_params(("parallel", "parallel")),
    )(*args)


def ffn_bwd_dh(da, db, wg, wu, gi, ui, nper, D, name, dep=None):
    F8 = da.shape[1]
    T = da.shape[0] // N_DEV
    tm = _tile(T, 512)
    nt = T // tm

    def body(da_ref, db_ref, wg_ref, wu_ref, *rest):
        o_ref, acc_ref = rest[-2], rest[-1]
        j = pl.program_id(1)
        part = _dot(da_ref[...], wg_ref[...], 1, 0) + _dot(db_ref[...], wu_ref[...], 1, 0)

        @pl.when(j == 0)
        def _():
            acc_ref[...] = part

        @pl.when(j > 0)
        def _():
            acc_ref[...] += part

        @pl.when(j == N_DEV - 1)
        def _():
            o_ref[...] = acc_ref[...]

    blk = pl.BlockSpec((tm, F8), lambda i, j: (j * nt + i, 0))
    in_specs = [blk, blk,
                pl.BlockSpec((F8, D), lambda i, j: (j * nper + gi, 0)),
                pl.BlockSpec((F8, D), lambda i, j: (j * nper + ui, 0))]
    args = [da, db, wg, wu]
    if dep is not None:
        in_specs.append(_dep_spec(2))
        args.append(dep)
    return pl.pallas_call(
        body, name=name, grid=(nt, N_DEV), in_specs=in_specs,
        out_specs=pl.BlockSpec((tm, D), lambda i, j: (i, 0)),
        out_shape=jax.ShapeDtypeStruct((T, D), F32),
        scratch_shapes=[pltpu.VMEM((tm, D), F32)],
        compiler_params=_params(("parallel", "arbitrary")),
    )(*args)


def ffn_dw_side(act, other, c, mine, name, add=None, dep=None):
    F8 = act.shape[1]
    T, D = other.shape
    tm = _tile(D, 1024)

    def body(c_ref, a_ref, b_ref, *rest):
        acc = _dot(a_ref[...], b_ref[...], 0, 0)
        if add is not None:
            acc = acc + rest[0][...].astype(F32)
        rest[-1][...] = acc.astype(BF16)

    def shard(q, cr):
        return 2 * q + (cr[0] if mine else 1 - cr[0])

    in_specs = [pl.BlockSpec((T, F8), lambda q, m, cr: (shard(q, cr), 0)),
                pl.BlockSpec((T, tm), lambda q, m, cr: (0, m))]
    args = [act, other]
    if add is not None:
        in_specs.append(pl.BlockSpec((F8, tm), lambda q, m, cr: (q, m)))
        args.append(add)
    if dep is not None:
        in_specs.append(pl.BlockSpec((8, LANES), lambda q, m, cr: (0, 0)))
        args.append(dep)
    grid_spec = pltpu.PrefetchScalarGridSpec(
        num_scalar_prefetch=1, grid=(N_DEV // 2, D // tm), in_specs=in_specs,
        out_specs=pl.BlockSpec((F8, tm), lambda q, m, cr: (q, m)))
    return pl.pallas_call(
        body, name=name, grid_spec=grid_spec,
        out_shape=jax.ShapeDtypeStruct((N_DEV // 2 * F8, D), BF16),
        compiler_params=_params(("parallel", "parallel")),
    )(c.reshape(1).astype(jnp.int32), *args)


def _t5_bucket_np(dist):
    max_exact = N_REL_BUCKETS // 2
    d = np.maximum(dist, 1).astype(np.float64)
    large = max_exact + (np.log(d / max_exact) / math.log(REL_MAX_DISTANCE / max_exact)
                         * (N_REL_BUCKETS - max_exact)).astype(np.int64)
    large32 = max_exact + (np.log(d.astype(np.float32) / np.float32(max_exact))
                           / np.float32(math.log(REL_MAX_DISTANCE / max_exact))
                           * np.float32(N_REL_BUCKETS - max_exact)).astype(np.int64)
    assert np.array_equal(large, large32)
    large = np.minimum(large, N_REL_BUCKETS - 1)
    return np.where(dist < max_exact, dist, large)


def _distance_tables(T, tq):
    dist = np.arange(T)
    mult = np.zeros(T, np.int64)
    for window, dilation in DILATED_PATTERNS:
        mult += ((dist % dilation == 0) & (dist // dilation <= window // dilation)).astype(np.int64)
    logm = np.where(mult > 0, np.log(np.maximum(mult, 1)), NEG_INF).astype(np.float32)
    bucket = _t5_bucket_np(dist).astype(np.int32)
    nkb = T // tq
    k = np.arange(nkb)[:, None, None]
    r = np.arange(tq)[None, :, None]
    c = np.arange(tq)[None, None, :]
    delta = k * tq + r - c
    return bucket, logm, delta


def _tile_buckets(T, tq):
    bucket, logm, delta = _distance_tables(T, tq)
    safe = np.maximum(delta, 0)
    bidx = np.where(delta >= 0, bucket[safe], -1).astype(np.int32)
    logm_t = np.where(delta >= 0, logm[safe], NEG_INF).astype(np.float32)
    present = [sorted(set(np.unique(bidx[k]).tolist()) - {-1}) for k in range(T // tq)]
    return bidx, logm_t, present


def bias_tiles(rel_table, T, tq):
    bidx, logm_t, present = _tile_buckets(T, tq)
    nkb = T // tq

    def body(tab_ref, b_ref, lm_ref, o_ref):
        slot = pl.program_id(0)

        @pl.when(slot == 0)
        def _():
            o_ref[...] = jnp.where(b_ref[...] >= 0, 0.0, NEG_INF)

        @pl.when(slot > 0)
        def _():
            for k in range(nkb):
                bi = b_ref[k]
                acc = lm_ref[k]
                for b in present[k]:
                    acc = acc + jnp.where(bi == b, tab_ref[b, slot - 1], 0.0)
                o_ref[k] = acc

    full = pl.BlockSpec((nkb, tq, tq), lambda s: (0, 0, 0))
    return pl.pallas_call(
        body, name="bias_tiles", grid=(1 + N_DIL,),
        in_specs=[pl.BlockSpec(memory_space=pltpu.SMEM), full, full],
        out_specs=pl.BlockSpec((None, nkb, tq, tq), lambda s: (s, 0, 0, 0)),
        out_shape=jax.ShapeDtypeStruct((1 + N_DIL, nkb, tq, tq), F32),
        compiler_params=_params(("parallel",)),
    )(rel_table, jnp.asarray(bidx), jnp.asarray(logm_t))


def fox_gate_fwd(uf, bf, name):
    T = uf.shape[0]
    tb = _tile(T, 512)

    def body(u_ref, b_ref, c_ref, ct_ref):
        lane = lax.broadcasted_iota(jnp.int32, (1, LANES), 1)
        tri = (lax.broadcasted_iota(jnp.int32, (tb, tb), 0)
               >= lax.broadcasted_iota(jnp.int32, (tb, tb), 1)).astype(F32)
        carry = jnp.zeros((1, LANES), F32)
        for blk in range(T // tb):
            z = u_ref[pl.ds(blk * tb, tb), :] + b_ref[...]
            lf = jnp.minimum(z, 0.0) - jnp.log1p(jnp.exp(-jnp.abs(z)))
            lf = jnp.where(lane < N_FOX, lf, 0.0)
            cb = _dot(tri, lf, 1, 0, precision=lax.Precision.HIGHEST) + carry
            c_ref[pl.ds(blk * tb, tb), :] = cb
            ct_ref[:, pl.ds(blk * tb, tb)] = cb.T
            carry = cb[tb - 1:tb, :]

    return pl.pallas_call(
        body, name=name,
        out_shape=[jax.ShapeDtypeStruct((T, LANES), F32), jax.ShapeDtypeStruct((LANES, T), F32)],
        compiler_params=_params(None),
    )(uf, bf)


def fox_gate_bwd(dct, uf, bf, name):
    T = uf.shape[0]
    tb = _tile(T, 512)

    def body(d_ref, u_ref, b_ref, du_ref, db_ref):
        lane = lax.broadcasted_iota(jnp.int32, (1, LANES), 1)
        triu = (lax.broadcasted_iota(jnp.int32, (tb, tb), 0)
                <= lax.broadcasted_iota(jnp.int32, (tb, tb), 1)).astype(F32)
        carry = jnp.zeros((1, LANES), F32)
        dbv = jnp.zeros((1, LANES), F32)
        for blk in reversed(range(T // tb)):
            dc = d_ref[:, pl.ds(blk * tb, tb)].T
            dlf = _dot(triu, dc, 1, 0, precision=lax.Precision.HIGHEST) + carry
            carry = dlf[0:1, :]
            z = u_ref[pl.ds(blk * tb, tb), :] + b_ref[...]
            dz = jnp.where(lane < N_FOX, dlf * (1.0 - _sigmoid(z)), 0.0)
            du_ref[pl.ds(blk * tb, tb), :] = dz
            dbv = dbv + jnp.sum(dz, axis=0, keepdims=True)
        db_ref[...] = dbv

    return pl.pallas_call(
        body, name=name,
        out_shape=[jax.ShapeDtypeStruct((T, LANES), F32), jax.ShapeDtypeStruct((1, LANES), F32)],
        compiler_params=_params(None),
    )(dct, uf, bf)


def _bias_slot(h):
    return jnp.maximum(h - (N_FOX - 1), 0)


def _scores(q_ref, k_ref, c_ref, ct_ref, tb_ref, h, i, tq, fox):
    scale = HEAD_DIM ** -0.5
    n = (i + 1) * tq
    rows = pl.ds(i * tq, tq)
    s = _dot(q_ref[rows, :], k_ref[pl.ds(0, n), :], 1, 1) * scale
    if not fox:
        return s + jnp.concatenate([tb_ref[i - jb] for jb in range(i + 1)], axis=1)
    lane = lax.broadcasted_iota(jnp.int32, (1, LANES), 1)
    c_col = jnp.sum(jnp.where(lane == h, c_ref[rows, :], 0.0), axis=1, keepdims=True)
    c_row = ct_ref[pl.ds(h, 1), pl.ds(0, n)]
    s = s + (c_col - c_row)
    if i == 0:
        return s + tb_ref[0]
    return jnp.concatenate([s[:, :i * tq], s[:, i * tq:] + tb_ref[0]], axis=1)


def _attn_specs(T, tq):
    nkb = T // tq
    return [
        pl.BlockSpec((T, HEAD_DIM), lambda h: (0, h)),
        pl.BlockSpec((T, HEAD_DIM), lambda h: (0, N_HEADS + h)),
        pl.BlockSpec((T, HEAD_DIM), lambda h: (0, 2 * N_HEADS + h)),
        pl.BlockSpec((T, LANES), lambda h: (0, 0)),
        pl.BlockSpec((LANES, T), lambda h: (0, 0)),
        pl.BlockSpec((None, nkb, tq, tq), lambda h: (_bias_slot(h), 0, 0, 0)),
    ]


def attention_fwd(qkv, c, ct, tiles, name):
    T = qkv.shape[0]
    tq = tiles.shape[2]

    def body(q_ref, k_ref, v_ref, c_ref, ct_ref, tb_ref, o_ref, lse_ref):
        h = pl.program_id(0)
        lane = lax.broadcasted_iota(jnp.int32, (1, LANES), 1)

        @pl.when(h == 0)
        def _():
            lse_ref[...] = jnp.zeros_like(lse_ref)

        def head(fox):
            for i in range(T // tq):
                rows = pl.ds(i * tq, tq)
                s = _scores(q_ref, k_ref, c_ref, ct_ref, tb_ref, h, i, tq, fox)
                m = jnp.max(s, axis=1, keepdims=True)
                p = jnp.exp(s - m)
                l = jnp.sum(p, axis=1, keepdims=True)
                o = _dot(p.astype(BF16), v_ref[pl.ds(0, (i + 1) * tq), :], 1, 0) * (1.0 / l)
                o_ref[rows, :] = o.astype(BF16)
                lse_ref[rows, :] = jnp.where(lane == h, m + jnp.log(l), lse_ref[rows, :])

        pl.when(h < N_FOX)(functools.partial(head, True))
        pl.when(h >= N_FOX)(functools.partial(head, False))

    return pl.pallas_call(
        body, name=name, grid=(N_HEADS,),
        in_specs=_attn_specs(T, tq),
        out_specs=[pl.BlockSpec((T, HEAD_DIM), lambda h: (0, h)), pl.BlockSpec((T, LANES), lambda h: (0, 0))],
        out_shape=[jax.ShapeDtypeStruct((T, N_HEADS * HEAD_DIM), BF16), jax.ShapeDtypeStruct((T, LANES), F32)],
        compiler_params=_params(("arbitrary",)),
    )(qkv, qkv, qkv, c, ct, tiles)


def attention_bwd(qkv, c, ct, tiles, lse, o, do, name):
    T = qkv.shape[0]
    tq = tiles.shape[2]
    nkb = T // tq
    scale = HEAD_DIM ** -0.5

    def body(q_ref, k_ref, v_ref, c_ref, ct_ref, tb_ref, lse_ref, o_ref, do_ref,
             dq_ref, dk_ref, dv_ref, dct_ref, dtb_ref, dk_acc, dv_acc):
        h = pl.program_id(0)
        lane = lax.broadcasted_iota(jnp.int32, (1, LANES), 1)
        dk_acc[...] = jnp.zeros_like(dk_acc)
        dv_acc[...] = jnp.zeros_like(dv_acc)
        dct_ref[...] = jnp.zeros_like(dct_ref)
        dtb_ref[...] = jnp.zeros_like(dtb_ref)

        def head(fox):
            for i in range(nkb):
                rows, keys = pl.ds(i * tq, tq), pl.ds(0, (i + 1) * tq)
                s = _scores(q_ref, k_ref, c_ref, ct_ref, tb_ref, h, i, tq, fox)
                lse_col = jnp.sum(jnp.where(lane == h, lse_ref[rows, :], 0.0), axis=1, keepdims=True)
                p = jnp.exp(s - lse_col)
                p_b = p.astype(BF16)
                dov = do_ref[rows, :]
                dp = _dot(dov, v_ref[keys, :], 1, 1)
                if fox:
                    delta = jnp.sum(p * dp, axis=1, keepdims=True)
                else:
                    delta = jnp.sum(dov.astype(F32) * o_ref[rows, :].astype(F32), axis=1, keepdims=True)
                ds = p * (dp - delta)
                ds_b = ds.astype(BF16)
                dq_ref[rows, :] = (_dot(ds_b, k_ref[keys, :], 1, 0) * scale).astype(BF16)
                dk_acc[:, keys] += _dot(q_ref[rows, :], ds_b, 0, 0) * scale
                dv_acc[:, keys] += _dot(dov, p_b, 0, 0)
                if fox:
                    dct_ref[:, keys] += -jnp.sum(ds, axis=0, keepdims=True)
                else:
                    for jb in range(i + 1):
                        dtb_ref[i - jb] += ds[:, jb * tq:(jb + 1) * tq]

        pl.when(h < N_FOX)(functools.partial(head, True))
        pl.when(h >= N_FOX)(functools.partial(head, False))
        dk_ref[...] = dk_acc[...].T.astype(BF16)
        dv_ref[...] = dv_acc[...].T.astype(BF16)

    head_cols = jax.ShapeDtypeStruct((T, N_HEADS * HEAD_DIM), BF16)
    col = pl.BlockSpec((T, HEAD_DIM), lambda h: (0, h))
    return pl.pallas_call(
        body, name=name, grid=(N_HEADS,),
        in_specs=_attn_specs(T, tq) + [pl.BlockSpec((T, LANES), lambda h: (0, 0)), col, col],
        out_specs=[col, col, col,
                   pl.BlockSpec((None, 1, T), lambda h: (h, 0, 0)),
                   pl.BlockSpec((None, nkb, tq, tq), lambda h: (_bias_slot(h), 0, 0, 0))],
        out_shape=[head_cols, head_cols, head_cols,
                   jax.ShapeDtypeStruct((N_HEADS, 1, T), F32),
                   jax.ShapeDtypeStruct((1 + N_DIL, nkb, tq, tq), F32)],
        scratch_shapes=[pltpu.VMEM((HEAD_DIM, T), F32), pltpu.VMEM((HEAD_DIM, T), F32)],
        compiler_params=_params(("arbitrary",)),
    )(qkv, qkv, qkv, c, ct, tiles, lse, o, do)


def rel_table_grad(dtiles, T, name):
    tq = dtiles.shape[2]
    nkb = T // tq
    bidx, _, present = _tile_buckets(T, tq)

    def body(d_ref, b_ref, o_ref):
        lane = lax.broadcasted_iota(jnp.int32, (1, LANES), 1)
        row = jnp.zeros((1, LANES), F32)
        for k in range(nkb):
            d = d_ref[k]
            bi = b_ref[k]
            for b in present[k]:
                v = jnp.sum(jnp.sum(jnp.where(bi == b, d, 0.0), axis=0, keepdims=True),
                            axis=1, keepdims=True)
                row = row + jnp.where(lane == b, v, 0.0)
        o_ref[...] = row

    return pl.pallas_call(
        body, name=name, grid=(N_DIL,),
        in_specs=[pl.BlockSpec((None, nkb, tq, tq), lambda h: (h + 1, 0, 0, 0)),
                  pl.BlockSpec((nkb, tq, tq), lambda h: (0, 0, 0))],
        out_specs=pl.BlockSpec((None, 1, LANES), lambda h: (h, 0, 0)),
        out_shape=jax.ShapeDtypeStruct((N_DIL, 1, LANES), F32),
        compiler_params=_params(("parallel",)),
    )(dtiles, jnp.asarray(bidx))


def _peer_list():
    x, y, c = lax.axis_index("x"), lax.axis_index("y"), lax.axis_index("c")
    me = 4 * x + 2 * y + c
    peers = []
    for fx in (0, 1):
        for fy in (0, 1):
            for fc in (0, 1):
                if fx or fy or fc:
                    px = 1 - x if fx else x
                    py = 1 - y if fy else y
                    pc = 1 - c if fc else c
                    peers.append(((px, py, pc), 4 * px + 2 * py + pc))
    return me, peers


_HBM = pl.BlockSpec(memory_space=pltpu.HBM)
_SEM = pl.BlockSpec(memory_space=pltpu.SEMAPHORE)
_EFFECT = pltpu.SideEffectType.DATAFLOW_SIDE_EFFECTING
N_PEERS = N_DEV - 1


def _in_hbm(a):
    return pltpu.with_memory_space_constraint(a, pltpu.HBM)


def _exchange_copies(srcs, lands, send_sems, recv_sems, blockwise):
    me, peers = _peer_list()
    sends, recvs = [], []
    for a in range(len(srcs)):
        for k, (dev, idx) in enumerate(peers):
            src = srcs[a].at[idx] if blockwise[a] else srcs[a]
            sends.append(pltpu.make_async_remote_copy(
                src_ref=src, dst_ref=lands[a].at[me], send_sem=send_sems[a].at[k],
                recv_sem=recv_sems[a].at[k], device_id=dev, device_id_type=MESH))
            recvs.append(pltpu.make_async_remote_copy(
                src_ref=src, dst_ref=lands[a].at[idx], send_sem=send_sems[a].at[k],
                recv_sem=recv_sems[a].at[k], device_id=dev, device_id_type=MESH))
    return sends, recvs


def exchange_start(srcs, lands, blockwise, name):
    n = len(srcs)

    def body(*refs):
        src_in, land_in = refs[:n], refs[n:2 * n]
        send_sems, recv_sems = refs[2 * n:3 * n], refs[3 * n:4 * n]
        token = refs[6 * n]
        sends, _ = _exchange_copies(src_in, land_in, send_sems, recv_sems, blockwise)
        for cp in sends:
            cp.start()
        token[...] = jnp.zeros_like(token)

    out_shape = ([pltpu.SemaphoreType.DMA((N_PEERS,))] * (2 * n)
                 + [pltpu.HBM(s.shape, s.dtype) for s in srcs]
                 + [pltpu.HBM(l.shape, l.dtype) for l in lands]
                 + [jax.ShapeDtypeStruct((8, LANES), F32)])
    aliases = {a: 2 * n + a for a in range(2 * n)}
    outs = pl.pallas_call(
        body, name=name, out_shape=out_shape,
        in_specs=[_HBM] * (2 * n),
        out_specs=[_SEM] * (2 * n) + [_HBM] * (2 * n) + [pl.BlockSpec(memory_space=pltpu.VMEM)],
        input_output_aliases=aliases,
        compiler_params=pltpu.CompilerParams(has_side_effects=_EFFECT),
    )(*[_in_hbm(s) for s in srcs], *[_in_hbm(l) for l in lands])
    return (outs[:n], outs[n:2 * n], outs[2 * n:3 * n], outs[3 * n:4 * n], outs[4 * n])


def exchange_wait(send_sems, recv_sems, srcs, lands, blockwise, after, name):
    n = len(srcs)

    def body(*refs):
        src_in, land_in = refs[:n], refs[n:2 * n]
        ss, rs = refs[2 * n:3 * n], refs[3 * n:4 * n]
        sends, recvs = _exchange_copies(src_in, land_in, ss, rs, blockwise)
        for cp in sends:
            cp.wait_send()
        for cp in recvs:
            cp.wait_recv()

    outs = pl.pallas_call(
        body, name=name,
        out_shape=[pltpu.HBM(s.shape, s.dtype) for s in srcs] + [pltpu.HBM(l.shape, l.dtype) for l in lands],
        in_specs=[_HBM] * (2 * n) + [_SEM] * (2 * n) + [pl.BlockSpec(memory_space=pl.ANY)],
        out_specs=[_HBM] * (2 * n),
        input_output_aliases={a: a for a in range(2 * n)},
        compiler_params=pltpu.CompilerParams(has_side_effects=_EFFECT),
    )(*srcs, *lands, *send_sems, *recv_sems, after)
    return outs[n:]


def _landing(own_block, me, slots=N_DEV):
    empty = lax.empty((slots,) + own_block.shape, own_block.dtype)
    return lax.dynamic_update_slice(empty, own_block[None], (me,) + (0,) * own_block.ndim)


N_CHIPS = N_DEV // 2
_CHIP_FLIPS = ((1, 0), (0, 1), (1, 1))


def _xyc():
    return lax.axis_index("x"), lax.axis_index("y"), lax.axis_index("c")


def _other_chips(x, y):
    return [(1 - x if fx else x, 1 - y if fy else y) for fx, fy in _CHIP_FLIPS]


def _remote(src, dst, send_sem, recv_sem, dev):
    return pltpu.make_async_remote_copy(src_ref=src, dst_ref=dst, send_sem=send_sem, recv_sem=recv_sem,
                                        device_id=dev, device_id_type=MESH)


def comm_call(name, bufs, sems_in, sems_out, fn, after=None, want_token=False):
    nb, ni, no = len(bufs), len(sems_in), len(sems_out)
    afters = [] if after is None else (list(after) if isinstance(after, (list, tuple)) else [after])
    na = len(afters)

    def body(*refs):
        buf_refs = refs[:nb]
        sin = refs[nb:nb + ni]
        sout = refs[nb + ni + na:nb + ni + na + no]
        fn(buf_refs, sin, sout)
        if want_token:
            tok = refs[nb + ni + na + no + nb]
            tok[...] = jnp.zeros_like(tok)

    out_shape = list(sems_out) + [pltpu.HBM(b.shape, b.dtype) for b in bufs]
    out_specs = [_SEM] * no + [_HBM] * nb
    if want_token:
        out_shape.append(jax.ShapeDtypeStruct((8, LANES), F32))
        out_specs.append(pl.BlockSpec(memory_space=pltpu.VMEM))
    args = [_in_hbm(b) for b in bufs] + list(sems_in) + afters
    outs = pl.pallas_call(
        body, name=name, out_shape=out_shape,
        in_specs=[_HBM] * nb + [_SEM] * ni + [pl.BlockSpec(memory_space=pl.ANY)] * na,
        out_specs=out_specs, input_output_aliases={a: no + a for a in range(nb)},
        compiler_params=pltpu.CompilerParams(has_side_effects=_EFFECT),
    )(*args)
    return list(outs[:no]), list(outs[no:no + nb]), (outs[no + nb] if want_token else None)


def _dma_sems(*sizes):
    return [pltpu.SemaphoreType.DMA((s,)) for s in sizes]


def gather_start(srcs, lands, name, after=None):
    n = len(srcs)

    def fn(bufs, sin, sout):
        x, y, c = _xyc()
        me = 4 * x + 2 * y + c
        for a in range(n):
            src, land = bufs[a], bufs[n + a]
            send, recv_d, recv_i = sout[3 * a:3 * a + 3]
            _remote(src, land.at[me], send.at[0], recv_d.at[0], (x, y, 1 - c)).start()
            for k, (px, py) in enumerate(_other_chips(x, y)):
                _remote(src, land.at[me], send.at[1 + k], recv_i.at[k], (px, py, c)).start()

    return comm_call(name, list(srcs) + list(lands), [], _dma_sems(4, 1, 3) * n, fn, after=after, want_token=True)


def gather_forward(srcs, lands, recv_i, after, name):
    n = len(srcs)

    def fn(bufs, sin, sout):
        x, y, c = _xyc()
        for a in range(n):
            src, land = bufs[a], bufs[n + a]
            f_send, f_recv = sout[2 * a:2 * a + 2]
            for k, (px, py) in enumerate(_other_chips(x, y)):
                blk = land.at[4 * px + 2 * py + c]
                _remote(src, blk, f_send.at[k], sin[a].at[k], (px, py, c)).wait_recv()
                _remote(blk, blk, f_send.at[k], f_recv.at[k], (x, y, 1 - c)).start()

    sems, bufs, _ = comm_call(name, list(srcs) + list(lands), recv_i, _dma_sems(3, 3) * n, fn, after=after)
    return sems, bufs


def gather_wait(srcs, lands, send, recv_d, f_send, f_recv, after, name):
    n = len(srcs)

    def fn(bufs, sin, sout):
        x, y, c = _xyc()
        sib = (x, y, 1 - c)
        for a in range(n):
            src, land = bufs[a], bufs[n + a]
            s_send, s_recv_d, s_fsend, s_frecv = sin[4 * a:4 * a + 4]
            sib_blk = land.at[4 * x + 2 * y + 1 - c]
            for k in range(4):
                _remote(src, sib_blk, s_send.at[k], s_recv_d.at[0], sib).wait_send()
            _remote(src, sib_blk, s_send.at[0], s_recv_d.at[0], sib).wait_recv()
            for k, (px, py) in enumerate(_other_chips(x, y)):
                cp = _remote(src, land.at[4 * px + 2 * py + 1 - c], s_fsend.at[k], s_frecv.at[k], sib)
                cp.wait_send()
                cp.wait_recv()

    sems_in = []
    for a in range(n):
        sems_in += [send[a], recv_d[a], f_send[a], f_recv[a]]
    _, bufs, _ = comm_call(name, list(srcs) + list(lands), sems_in, [], fn, after=after)
    return bufs[n:]


def scatter_pair_start(src4s, lands, name, after=None, whole=False):
    n = len(src4s)

    def fn(bufs, sin, sout):
        x, y, c = _xyc()
        for a in range(n):
            src = bufs[a] if whole else bufs[a].at[:, 1 - c]
            _remote(src, bufs[n + a], sout[2 * a].at[0], sout[2 * a + 1].at[0], (x, y, 1 - c)).start()

    return comm_call(name, list(src4s) + list(lands), [], _dma_sems(1, 1) * n, fn, after=after, want_token=True)


def scatter_pair_wait(src4s, lands, sems, after, name, whole=False):
    n = len(src4s)

    def fn(bufs, sin, sout):
        x, y, c = _xyc()
        for a in range(n):
            src = bufs[a] if whole else bufs[a].at[:, 1 - c]
            cp = _remote(src, bufs[n + a], sin[2 * a].at[0], sin[2 * a + 1].at[0], (x, y, 1 - c))
            cp.wait_send()
            cp.wait_recv()

    _, bufs, _ = comm_call(name, list(src4s) + list(lands), sems, [], fn, after=after)
    return bufs[:n], bufs[n:]


def _row_tile(R):
    for cand in range(256, 15, -16):
        if R % cand == 0 and R // cand >= 4:
            return cand
    return R


def chip_sum(src4, land, c, name):
    _, _, R, C = src4.shape
    tr = R

    def body(c_ref, a_ref, b_ref, o_ref):
        o_ref[...] = (a_ref[...].astype(F32) + b_ref[...].astype(F32)).astype(BF16)

    grid_spec = pltpu.PrefetchScalarGridSpec(
        num_scalar_prefetch=1, grid=(N_CHIPS, R // tr),
        in_specs=[pl.BlockSpec((None, None, tr, C), lambda q, i, cr: (q, cr[0], i, 0)),
                  pl.BlockSpec((None, tr, C), lambda q, i, cr: (q, i, 0))],
        out_specs=pl.BlockSpec((None, tr, C), lambda q, i, cr: (q, i, 0)))
    return pl.pallas_call(
        body, name=name, grid_spec=grid_spec,
        out_shape=jax.ShapeDtypeStruct((N_CHIPS, R, C), BF16),
        compiler_params=_params(("parallel", "parallel")),
    )(c.reshape(1).astype(jnp.int32), src4, land)


def scatter_chip_start(sums, lands, name):
    n = len(sums)

    def fn(bufs, sin, sout):
        x, y, c = _xyc()
        for a in range(n):
            for k, (px, py) in enumerate(_other_chips(x, y)):
                _remote(bufs[a].at[2 * px + py], bufs[n + a].at[2 * x + y], sout[2 * a].at[k], sout[2 * a + 1].at[k],
                        (px, py, c)).start()

    return comm_call(name, list(sums) + list(lands), [], _dma_sems(3, 3) * n, fn, want_token=True)


def scatter_chip_wait(sums, lands, sems, after, name):
    n = len(sums)

    def fn(bufs, sin, sout):
        x, y, c = _xyc()
        for a in range(n):
            for k, (px, py) in enumerate(_other_chips(x, y)):
                cp = _remote(bufs[a].at[2 * px + py], bufs[n + a].at[2 * px + py], sin[2 * a].at[k],
                             sin[2 * a + 1].at[k], (px, py, c))
                cp.wait_send()
                cp.wait_recv()

    _, bufs, _ = comm_call(name, list(sums) + list(lands), sems, [], fn, after=after)
    return bufs[:n], bufs[n:]


def _adamw_math(w, g, m, v):
    m = ADAM_B1 * m + (1.0 - ADAM_B1) * g
    v = ADAM_B2 * v + (1.0 - ADAM_B2) * (g * g)
    m_hat = m / (1.0 - ADAM_B1 ** ADAM_STEP)
    v_hat = v / (1.0 - ADAM_B2 ** ADAM_STEP)
    delta = -ADAM_LR * (m_hat / (jnp.sqrt(v_hat) + ADAM_EPS) + ADAM_WD * w)
    return delta, m, v


def _sum_partials(p_ref, own_ref, mine):
    own = own_ref[...].astype(F32)
    g = None
    for s in range(p_ref.shape[0]):
        term = jnp.where(mine == s, own, p_ref[s].astype(F32))
        g = term if g is None else g + term
    return g


def adamw_sharded(parts, sums, my_chip, w, m, v, name):
    R, C = w.shape
    S = parts.shape[0]
    tr = _row_tile(R)

    def body(mc_ref, p_ref, o_ref, w_ref, m_ref, v_ref, g_ref, d_ref, nm_ref, nv_ref):
        g = _sum_partials(p_ref, o_ref, mc_ref[0])
        delta, nm, nv = _adamw_math(w_ref[...], g, m_ref[...], v_ref[...])
        g_ref[...] = g
        d_ref[...] = delta
        nm_ref[...] = nm
        nv_ref[...] = nv

    row = pl.BlockSpec((tr, C), lambda i, mc: (i, 0))
    shp = jax.ShapeDtypeStruct((R, C), F32)
    grid_spec = pltpu.PrefetchScalarGridSpec(
        num_scalar_prefetch=1, grid=(R // tr,),
        in_specs=[pl.BlockSpec((S, tr, C), lambda i, mc: (0, i, 0)),
                  pl.BlockSpec((None, tr, C), lambda i, mc: (mc[0], i, 0)), row, row, row],
        out_specs=[row, row, row, row])
    return pl.pallas_call(
        body, name=name, grid_spec=grid_spec, out_shape=[shp, shp, shp, shp],
        compiler_params=_params(("parallel",)),
    )(my_chip.reshape(1).astype(jnp.int32), parts, sums, w, m, v)


def adamw_small(parts, w, m, v, name):
    R, C = w.shape

    def body(p_ref, w_ref, m_ref, v_ref, g_ref, d_ref, nm_ref, nv_ref):
        g = p_ref[0]
        for s in range(1, N_DEV):
            g = g + p_ref[s]
        delta, nm, nv = _adamw_math(w_ref[...], g, m_ref[...], v_ref[...])
        g_ref[...] = g
        d_ref[...] = delta
        nm_ref[...] = nm
        nv_ref[...] = nv

    shp = jax.ShapeDtypeStruct((R, C), F32)
    return pl.pallas_call(
        body, name=name, out_shape=[shp, shp, shp, shp], compiler_params=_params(None),
    )(parts, w, m, v)


_ROW_NORM_FFN1, _ROW_NORM_MIX, _ROW_NORM_FFN2, _ROW_NORM_PLE, _ROW_NORM_FINAL = 0, 1, 2, 3, 4
_ROW_B_F, _ROW_REL, _ROW_LOSS, _SMALL_ROWS = 5, 6, 7, 8


def _pack_small(D, norm_ffn1, norm_mix, norm_ffn2, norm_ple, norm_final, b_f, rel_table):
    def row(v):
        v = v.reshape(1, -1)
        return jnp.pad(v, ((0, 0), (0, D - v.shape[1])))
    return jnp.concatenate([row(norm_ffn1), row(norm_mix), row(norm_ffn2), row(norm_ple),
                            row(norm_final), row(b_f), row(rel_table),
                            jnp.zeros((1, D), F32)], axis=0)


def _unpack_small(a, shapes):
    return {"norm_ffn1": a[_ROW_NORM_FFN1].reshape(shapes["norm_ffn1"]),
            "norm_mix": a[_ROW_NORM_MIX].reshape(shapes["norm_mix"]),
            "b_f": a[_ROW_B_F, :N_FOX].reshape(shapes["b_f"]),
            "norm_ffn2": a[_ROW_NORM_FFN2].reshape(shapes["norm_ffn2"]),
            "norm_ple": a[_ROW_NORM_PLE].reshape(shapes["norm_ple"]),
            "rel_table": a[_ROW_REL, :N_REL_BUCKETS * N_DIL].reshape(shapes["rel_table"]),
            "norm_final": a[_ROW_NORM_FINAL].reshape(shapes["norm_final"])}


def local_step(x, p, tgt, g_ffn1, g_mix, g_ffn2, g_ple, g_final, b_f, rel_table,
               forward, weights, emit, emit2, dw_theirs, dw_mine, first_dep):
    T, D = x.shape
    P = p.shape[1]
    CW = D // N_DEV
    tq = _tile(T, 256)

    h1 = rms_fwd(x, g_ffn1, "rms_ffn1", dep=first_dep)
    tiles = bias_tiles(rel_table, T, tq)
    forward("ffn1_g", [tiles, h1])
    wg1, = weights("ffn1_g", h1)
    gate1 = ffn_gate(h1, wg1, "ffn1_gate")
    forward("ffn1_u", gate1)
    wu1, = weights("ffn1_u", gate1)
    a1, b1, s1 = ffn_up_gated(h1, wu1, gate1, "ffn1_up")
    forward("ffn1_d", s1)
    wd1, = weights("ffn1_d", s1)
    x1, h2 = ffn_down(s1, wd1, 0, 1, x, g_mix, "ffn1_down")

    forward("mix_in", h2)
    w3, wf = weights("mix_in", h2)
    qkv = mm_nt([(h2, w3)], "mix_qkv", tn=768, out_dtype=BF16)
    uf = mm_nt([(h2, wf)], "mix_forget", tn=LANES, out_dtype=F32)
    bfp = jnp.pad(b_f.reshape(1, N_FOX), ((0, 0), (0, LANES - N_FOX)))
    c, ct = fox_gate_fwd(uf, bfp, "fox_gate")
    cat, lse = attention_fwd(qkv, c, ct, tiles, "attention")
    forward("mix_out", cat)
    wo, = weights("mix_out", cat)
    x2 = mm_nn(cat, wo, "mix_out", tn=512, out_dtype=F32, res=x1)

    h3 = rms_fwd(x2, g_ffn2, "rms_ffn2")
    forward("ffn2_gu", h3)
    wgu2, = weights("ffn2_gu", h3)
    a2, b2, s2 = ffn_up(h3, wgu2, 0, 1, 2, "ffn2_up")
    forward("ffn2_d", s2)
    wd2, = weights("ffn2_d", s2)
    x3, h4 = ffn_down(s2, wd2, 0, 1, x2, g_ple, "ffn2_down")
    forward("ple", x3)

    wpg, wpp = weights("ple", h4)
    z = mm_nn(h4, wpg, "ple_gate", tn=512, out_dtype=F32)
    pp = mm_nn(p, wpp, "ple_proj", tn=CW, tm=T, out_dtype=F32, n_out=D,
               b_block=(P, CW), b_map=lambda n, i: (n, 0))
    loss_row, dx4, dg_final, dz, dpp = ple_loss(x3, z, pp, g_final, tgt, "ple_loss")

    grads = {}
    grads["w_ple_proj"] = mm_tn(p, dpp, "ple_proj_dw", grid=(N_DEV,),
                                a_block=(T, P), a_map=lambda n: (0, 0),
                                b_block=(T, CW), b_map=lambda n: (0, n),
                                o_block=(P, CW), o_map=lambda n: (n, 0),
                                out_shape=(N_DEV * P, CW))
    grads["w_ple_gate"] = mm_tn_plain(h4, dz, "ple_gate_dw")
    tok = emit("ple", grads)
    dh4 = mm_nt([(dz, wpg)], "ple_gate_dh", tn=512, out_dtype=F32, dep=tok)
    tok = emit2("ple", dh4)
    dx3, dx3h, dg_ple = rms_bwd(dh4, x3, g_ple, dx4, "rms_ple_bwd", dep=tok)

    da2, db2 = ffn_bwd_act(dx3h, wd2, 0, 1, a2, b2, "ffn2_bwd_act")
    tok = dw_theirs("ffn2_w_down", s2, dx3h)
    tok = dw_theirs("ffn2_w_gate", da2, h3, dep=tok)
    tok = dw_theirs("ffn2_w_up", db2, h3, dep=tok)
    tok = dw_mine("ffn2_w_down", s2, dx3h, dep=tok)
    tok = dw_mine("ffn2_w_gate", da2, h3, dep=tok)
    tok = dw_mine("ffn2_w_up", db2, h3, dep=tok)
    dh3 = ffn_bwd_dh(da2, db2, wgu2, wgu2, 0, 1, 2, D, "ffn2_bwd_dh", dep=tok)
    dx2, _, dg_ffn2 = rms_bwd(dh3, x2, g_ffn2, dx3, "rms_ffn2_bwd", half=False)

    dcat = mm_nt([(dx2, wo)], "mix_out_dh", tn=512, out_dtype=BF16)
    grads["w_o"] = mm_tn_plain(cat, dx2, "mix_out_dw")
    dq, dk, dv, dct, dtiles = attention_bwd(qkv, c, ct, tiles, lse, cat, dcat, "attention_bwd")
    dctp = jnp.pad(dct[:, 0, :], ((0, LANES - N_HEADS), (0, 0)))
    duf, dbf = fox_gate_bwd(dctp, uf, bfp, "fox_gate_bwd")
    drel = rel_table_grad(dtiles, T, "rel_table_grad")[:, 0, :N_REL_BUCKETS].T
    du3 = jnp.concatenate([dq, dk, dv], axis=1)
    grads["w3"] = mm_tn_plain(du3, h2, "mix_qkv_dw", tm=768)
    grads["wf"] = mm_tn_plain(duf, h2, "mix_forget_dw", tm=LANES)
    tok = emit("mix", grads)
    dh2 = mm_nn_sum([(du3, w3), (duf, wf)], "mix_in_dh", tn=512, out_dtype=F32, dep=tok)
    tok = emit2("mix", dh2)
    dx1, dx1h, dg_mix = rms_bwd(dh2, x1, g_mix, dx2, "rms_mix_bwd", dep=tok)

    da1, db1 = ffn_bwd_act(dx1h, wd1, 0, 1, a1, b1, "ffn1_bwd_act")
    tok = dw_theirs("ffn1_w_down", s1, dx1h)
    tok = dw_theirs("ffn1_w_gate", da1, h1, dep=tok)
    tok = dw_mine("ffn1_w_down", s1, dx1h, dep=tok)
    tok = dw_theirs("ffn1_w_up", db1, h1, dep=tok)
    tok = dw_mine("ffn1_w_gate", da1, h1, dep=tok)
    tok = dw_mine("ffn1_w_up", db1, h1, dep=tok)
    dh1 = ffn_bwd_dh(da1, db1, wg1, wu1, 0, 0, 1, D, "ffn1_bwd_dh", dep=tok)
    dx0, _, dg_ffn1 = rms_bwd(dh1, x, g_ffn1, dx1, "rms_ffn1_bwd", half=False)

    small = _pack_small(D, dg_ffn1, dg_mix, dg_ffn2, dg_ple, dg_final, dbf[:, :N_FOX], drel)
    small = small.at[_ROW_LOSS, :LANES].set(loss_row[0])
    grads["small"] = small
    emit("small", grads)
    return dx0


def _split_w_in(w_in_t):
    df, dd = N_FOX * HEAD_DIM, N_DIL * HEAD_DIM
    o = np.cumsum([0, df, df, df, N_FOX, dd, dd, dd]).tolist()
    qa, ka, va, f, qb, kb, vb = [w_in_t[o[i]:o[i + 1]] for i in range(7)]
    return jnp.concatenate([qa, qb, ka, kb, va, vb], axis=0), f


def _join_w_in(d3, dfg):
    df, dd = N_FOX * HEAD_DIM, N_DIL * HEAD_DIM
    o = np.cumsum([0, df, dd, df, dd, df, dd]).tolist()
    qa, qb, ka, kb, va, vb = [d3[o[i]:o[i + 1]] for i in range(6)]
    return jnp.concatenate([qa, ka, va, dfg, qb, kb, vb], axis=0)


def rows_to_bf16(a3, name, dep=None):
    R, _, C = a3.shape
    tc = _tile(C, 512)

    def body(a_ref, *rest):
        rest[-1][...] = a_ref[...].astype(BF16)

    in_specs = [pl.BlockSpec((R, None, tc), lambda n: (0, 0, n))]
    args = [a3]
    if dep is not None:
        in_specs.append(_dep_spec(1))
        args.append(dep)
    return pl.pallas_call(
        body, name=name, grid=(C // tc,), in_specs=in_specs,
        out_specs=pl.BlockSpec((R, tc), lambda n: (0, n)),
        out_shape=jax.ShapeDtypeStruct((R, C), BF16),
        compiler_params=_params(("parallel",)),
    )(*args)


def adamw_rows3d(parts, sums, my_chip, w3, m3, v3, name):
    R, _, C = w3.shape
    S = parts.shape[0]
    tc = _tile(C, 512)

    def body(mc_ref, p_ref, o_ref, w_ref, m_ref, v_ref, g_ref, d_ref, nm_ref, nv_ref):
        g = _sum_partials(p_ref, o_ref, mc_ref[0])
        delta, nm, nv = _adamw_math(w_ref[...], g, m_ref[...], v_ref[...])
        g_ref[...] = g
        d_ref[...] = delta
        nm_ref[...] = nm
        nv_ref[...] = nv

    col = pl.BlockSpec((R, None, tc), lambda n, mc: (0, 0, n))
    shp = jax.ShapeDtypeStruct((R, 1, C), F32)
    grid_spec = pltpu.PrefetchScalarGridSpec(
        num_scalar_prefetch=1, grid=(C // tc,),
        in_specs=[pl.BlockSpec((S, R, tc), lambda n, mc: (0, 0, n)),
                  pl.BlockSpec((None, R, tc), lambda n, mc: (mc[0], 0, n)), col, col, col],
        out_specs=[col, col, col, col])
    return pl.pallas_call(
        body, name=name, grid_spec=grid_spec, out_shape=[shp, shp, shp, shp],
        compiler_params=_params(("parallel",)),
    )(my_chip.reshape(1).astype(jnp.int32), parts, sums, w3, m3, v3)


def kernel(x, p, norm_ffn1, ffn1_w_gate, ffn1_w_up, ffn1_w_down, norm_mix, w_in, b_f, w_o, norm_ffn2, ffn2_w_gate, ffn2_w_up, ffn2_w_down, norm_ple, w_ple_gate, w_ple_proj, rel_table, norm_final, loss_target, m_norm_ffn1, m_ffn1_w_gate, m_ffn1_w_up, m_ffn1_w_down, m_norm_mix, m_w_in, m_b_f, m_w_o, m_norm_ffn2, m_ffn2_w_gate, m_ffn2_w_up, m_ffn2_w_down, m_norm_ple, m_w_ple_gate, m_w_ple_proj, m_rel_table, m_norm_final, v_norm_ffn1, v_ffn1_w_gate, v_ffn1_w_up, v_ffn1_w_down, v_norm_mix, v_w_in, v_b_f, v_w_o, v_norm_ffn2, v_ffn2_w_gate, v_ffn2_w_up, v_ffn2_w_down, v_norm_ple, v_w_ple_gate, v_w_ple_proj, v_rel_table, v_norm_final):
    names = ["norm_ffn1", "ffn1_w_gate", "ffn1_w_up", "ffn1_w_down", "norm_mix", "w_in", "b_f", "w_o",
             "norm_ffn2", "ffn2_w_gate", "ffn2_w_up", "ffn2_w_down", "norm_ple", "w_ple_gate",
             "w_ple_proj", "rel_table", "norm_final"]
    w = dict(zip(names, [norm_ffn1, ffn1_w_gate, ffn1_w_up, ffn1_w_down, norm_mix, w_in, b_f, w_o,
                         norm_ffn2, ffn2_w_gate, ffn2_w_up, ffn2_w_down, norm_ple, w_ple_gate,
                         w_ple_proj, rel_table, norm_final]))
    m = dict(zip(names, [m_norm_ffn1, m_ffn1_w_gate, m_ffn1_w_up, m_ffn1_w_down, m_norm_mix, m_w_in,
                         m_b_f, m_w_o, m_norm_ffn2, m_ffn2_w_gate, m_ffn2_w_up, m_ffn2_w_down,
                         m_norm_ple, m_w_ple_gate, m_w_ple_proj, m_rel_table, m_norm_final]))
    v = dict(zip(names, [v_norm_ffn1, v_ffn1_w_gate, v_ffn1_w_up, v_ffn1_w_down, v_norm_mix, v_w_in,
                         v_b_f, v_w_o, v_norm_ffn2, v_ffn2_w_gate, v_ffn2_w_up, v_ffn2_w_down,
                         v_norm_ple, v_w_ple_gate, v_w_ple_proj, v_rel_table, v_norm_final]))
    sharded = ["ffn1_w_gate", "ffn1_w_up", "ffn1_w_down", "w_in", "w_o", "ffn2_w_gate", "ffn2_w_up",
               "ffn2_w_down", "w_ple_gate", "w_ple_proj"]
    small_names = [n for n in names if n not in sharded]

    xs, ps, tgt = x[0], p[0, 0], loss_target[0]
    T, D = xs.shape
    transposed = ("ffn1_w_gate", "ffn1_w_up", "ffn2_w_gate", "ffn2_w_up")

    def view(t, n):
        if n in transposed:
            return t[n][0].T
        if n == "w_in":
            return jnp.transpose(t[n], (2, 0, 1))
        return t[n][0]

    def unview(a, n):
        if n in transposed:
            return a.T.reshape(w[n].shape)
        if n == "w_in":
            return jnp.transpose(a, (1, 2, 0))
        return a.reshape(w[n].shape)

    sh = {n: view(w, n) for n in sharded}
    m_sh = {n: view(m, n) for n in sharded}
    v_sh = {n: view(v, n) for n in sharded}
    F8 = sh["ffn1_w_down"].shape[0]
    WIN8 = sh["w_in"].shape[0]
    me = 4 * lax.axis_index("x") + 2 * lax.axis_index("y") + lax.axis_index("c")

    def start(groups, name, after=None):
        srcs = [s for grp in groups for s in grp]
        sems, bufs, token = gather_start(srcs, [_landing(s, me) for s in srcs], name, after=after)
        return sems, bufs[:len(srcs)], bufs[len(srcs):], token

    cat0 = lambda ns, z: (jnp.concatenate([sh[n] for n in ns], axis=0) + z).astype(BF16)
    sems_a, srcs_a, lands_a, token_a = start(
        [[sh["ffn1_w_gate"].astype(BF16)], [sh["ffn1_w_up"].astype(BF16)], [sh["ffn1_w_down"].astype(BF16)]],
        "gather_start_ffn1")
    zero = token_a[0, 0]
    w_in_bf = rows_to_bf16(sh["w_in"], "w_in_bf16", dep=token_a)
    sems_b, srcs_b, lands_b, g_token = start(
        [[w_in_bf, (sh["w_o"] + zero).astype(BF16)],
         [cat0(["ffn2_w_gate", "ffn2_w_up"], zero), (sh["ffn2_w_down"] + zero).astype(BF16)],
         [(sh["w_ple_gate"] + zero).astype(BF16), (sh["w_ple_proj"] + zero).astype(BF16)]],
        "gather_start_rest", after=token_a)
    order = ["ffn1_g", "ffn1_u", "ffn1_d", "mix_in", "mix_out", "ffn2_gu", "ffn2_d", "ple"]
    group_sizes = [1, 1, 1, 1, 1, 1, 1, 2]
    g_sems, g_srcs, g_lands = sems_a + sems_b, srcs_a + srcs_b, lands_a + lands_b
    g_send, g_recv_d, g_recv_i = g_sems[0::3], g_sems[1::3], g_sems[2::3]
    first = np.cumsum([0] + group_sizes).tolist()
    passed = {}

    def arrays_of(group):
        k = order.index(group)
        return slice(first[k], first[k + 1])

    def forward(group, after):
        sl = arrays_of(group)
        f_sems, bufs = gather_forward(g_srcs[sl], g_lands[sl], g_recv_i[sl], after, "gather_forward_" + group)
        k = len(bufs) // 2
        passed[group] = (f_sems[0::2], f_sems[1::2], bufs[:k], bufs[k:])

    def weights(group, after):
        sl = arrays_of(group)
        f_send, f_recv, srcs, lands = passed[group]
        got = gather_wait(srcs, lands, g_send[sl], g_recv_d[sl], f_send, f_recv, after, "gather_wait_" + group)
        if group in ("ffn1_g", "ffn1_u", "ffn1_d"):
            return (got[0].reshape(N_DEV * F8, D),)
        if group == "ffn2_gu":
            return (got[0].reshape(N_DEV * 2 * F8, D),)
        if group in ("ffn2_d", "mix_out"):
            return (got[0].reshape(-1, D),)
        if group == "ple":
            return got[0].reshape(-1, D), got[1].reshape(-1, got[1].shape[2])
        w3, wf8 = _split_w_in(got[0].reshape(N_DEV * WIN8, D))
        return w3, jnp.pad(wf8, ((0, LANES - N_FOX), (0, 0)))

    ffn_names = ["ffn2_w_down", "ffn2_w_gate", "ffn2_w_up", "ffn1_w_down", "ffn1_w_gate", "ffn1_w_up"]
    scatter_groups = {"ple": ["w_ple_gate", "w_ple_proj"], "mix": ["w_in", "w_o"]}
    scatter_groups.update({n: [n] for n in ffn_names})
    x_i, y_i, c_i = _xyc()
    my_chip = 2 * x_i + y_i
    pair_stage, chip_stage, small_stage = {}, {}, {}

    def emit(group, grads, after=None):
        if group == "small":
            src = grads["small"]
            ss, rs, srcs, lands, token = exchange_start([src], [_landing(src, me)], [False], "scatter_start_small")
            small_stage["small"] = (ss, rs, srcs, lands)
            return token
        src4s = []
        for n in scatter_groups[group]:
            if n == "w_in":
                full = _join_w_in(grads["w3"], grads["wf"][:N_FOX])
                src4s.append(full.reshape(N_CHIPS, 2, WIN8, D))
            else:
                src4s.append(grads[n].reshape((N_CHIPS, 2) + sh[n].shape))
        lands = [lax.empty((N_CHIPS,) + s.shape[2:], BF16) for s in src4s]
        sems, bufs, token = scatter_pair_start(src4s, lands, "scatter_pair_start_" + group, after=after)
        k = len(src4s)
        pair_stage[group] = (sems, bufs[:k], bufs[k:])
        return token

    def emit2(group, after):
        sems, src4s, lands = pair_stage[group]
        src4s, lands = scatter_pair_wait(src4s, lands, sems, after, "scatter_pair_wait_" + group)
        sums = [chip_sum(s4, la, c_i, "chip_sum_" + n)
                for s4, la, n in zip(src4s, lands, scatter_groups[group])]
        chip_lands = [lax.empty(s.shape, s.dtype) for s in sums]
        sems, bufs, token = scatter_chip_start(sums, chip_lands, "scatter_chip_start_" + group)
        k = len(sums)
        chip_stage[group] = (sems, bufs[:k], bufs[k:])
        return token

    def dw_theirs(n, act, other, dep=None):
        theirs = ffn_dw_side(act, other, c_i, False, n + "_dw_theirs", dep=dep)
        theirs = theirs.reshape((N_CHIPS,) + sh[n].shape)
        sems, bufs, token = scatter_pair_start([theirs], [lax.empty(theirs.shape, BF16)],
                                               "scatter_pair_start_" + n, whole=True)
        pair_stage[n] = (sems, bufs[:1], bufs[1:])
        return token

    def dw_mine(n, act, other, dep=None):
        sems, theirs, lands = pair_stage[n]
        _, lands = scatter_pair_wait(theirs, lands, sems, dep, "scatter_pair_wait_" + n, whole=True)
        sums = ffn_dw_side(act, other, c_i, True, n + "_dw_mine", add=lands[0].reshape(-1, D))
        sums = sums.reshape((N_CHIPS,) + sh[n].shape)
        sems, bufs, token = scatter_chip_start([sums], [lax.empty(sums.shape, BF16)], "scatter_chip_start_" + n)
        chip_stage[n] = (sems, bufs[:1], bufs[1:])
        return token

    dx0 = local_step(
        xs, ps, tgt, w["norm_ffn1"], w["norm_mix"], w["norm_ffn2"], w["norm_ple"],
        w["norm_final"].reshape(1, D), w["b_f"], w["rel_table"], forward, weights, emit, emit2,
        dw_theirs, dw_mine, g_token)

    res = {}
    after = dx0
    for group in ["ple"] + ffn_names[:3] + ["mix"] + ffn_names[3:]:
        sems, sums, chip_lands = chip_stage[group]
        sums, parts = scatter_chip_wait(sums, chip_lands, sems, after, "scatter_chip_wait_" + group)
        for n, part, own in zip(scatter_groups[group], parts, sums):
            update = adamw_rows3d if n == "w_in" else adamw_sharded
            g, d, nm, nv = update(part, own, my_chip, sh[n], m_sh[n], v_sh[n], "adamw_" + n)
            res[n] = tuple(unview(a, n) for a in (g, d, nm, nv))
            after = g
    ss, rs, srcs, lands = small_stage["small"]
    small_parts, = exchange_wait(ss, rs, srcs, lands, [False], after, "scatter_wait_small")
    pack = lambda t: _pack_small(D, t["norm_ffn1"], t["norm_mix"], t["norm_ffn2"], t["norm_ple"],
                                 t["norm_final"], t["b_f"], t["rel_table"])
    gs, ds, ms, vs = adamw_small(small_parts, pack(w), pack(m), pack(v), "adamw_small")
    shapes = {n: w[n].shape for n in small_names}
    unpacked = [_unpack_small(a, shapes) for a in (gs, ds, ms, vs)]
    for n in small_names:
        res[n] = tuple(u[n] for u in unpacked)
    loss = gs[_ROW_LOSS, 0]

    out = [loss, dx0.reshape(x.shape)]
    for k in range(4):
        out += [res[n][k] for n in names]
    return tuple(out)
```

```python
import functools
import math

import numpy as np
import jax
import jax.numpy as jnp
from jax import lax
from jax.experimental import pallas as pl
from jax.experimental.pallas import tpu as pltpu

F32 = jnp.float32
BF16 = jnp.bfloat16

N_DEV = 8
HEAD_DIM = 128
N_FOX = 8
N_DIL = 8
N_HEADS = N_FOX + N_DIL
DILATED_PATTERNS = ((128, 1), (512, 4), (2048, 16))
N_REL_BUCKETS = 32
REL_MAX_DISTANCE = 2048
RMS_EPS = 1e-6
NEG_INF = -1e30
LANES = 128
VMEM_LIMIT = 56 * 1024 * 1024

ADAM_LR = 0.001
ADAM_B1 = 0.9
ADAM_B2 = 0.999
ADAM_EPS = 1e-08
ADAM_WD = 0.01
ADAM_STEP = 10

MESH = pl.DeviceIdType.MESH


def _params(sem):
    return pltpu.CompilerParams(dimension_semantics=sem, vmem_limit_bytes=VMEM_LIMIT)


def _dot(a, b, ca, cb, precision=None):
    return lax.dot_general(a, b, (((ca,), (cb,)), ((), ())),
                           preferred_element_type=F32, precision=precision)


def _sigmoid(z):
    return 1.0 / (1.0 + jnp.exp(-z))


def _tile(n, want):
    t = min(n, want)
    assert n % t == 0, (n, t)
    return t


def _dep_spec(ngrid):
    return pl.BlockSpec((8, LANES), lambda *_: (0, 0))


def rms_fwd(x, g, name, dep=None):
    T, D = x.shape
    tm = _tile(T, 256)

    def body(x_ref, g_ref, *rest):
        h_ref = rest[-1]
        xv = x_ref[...]
        r = lax.rsqrt(jnp.mean(xv * xv, axis=-1, keepdims=True) + RMS_EPS)
        h_ref[...] = (xv * r * g_ref[...]).astype(BF16)

    in_specs = [pl.BlockSpec((tm, D), lambda i: (i, 0)), pl.BlockSpec((1, D), lambda i: (0, 0))]
    args = [x, g]
    if dep is not None:
        in_specs.append(_dep_spec(1))
        args.append(dep)
    return pl.pallas_call(
        body, name=name, grid=(T // tm,), in_specs=in_specs,
        out_specs=pl.BlockSpec((tm, D), lambda i: (i, 0)),
        out_shape=jax.ShapeDtypeStruct((T, D), BF16),
        compiler_params=_params(("parallel",)),
    )(*args)


def rms_bwd(dh, x, g, dres, name, dep=None, half=True):
    T, D = x.shape
    tm = _tile(T, 256)

    def body(dh_ref, x_ref, g_ref, dres_ref, *rest):
        dx_ref, dg_ref = (rest[-3], rest[-1]) if half else (rest[-2], rest[-1])
        i = pl.program_id(0)
        xv = x_ref[...]
        r = lax.rsqrt(jnp.mean(xv * xv, axis=-1, keepdims=True) + RMS_EPS)
        xh = xv * r
        d = dh_ref[...]
        u = d * g_ref[...]
        dx = dres_ref[...] + r * (u - xh * jnp.mean(u * xh, axis=-1, keepdims=True))
        dx_ref[...] = dx
        if half:
            rest[-2][...] = (0.5 * dx).astype(BF16)
        part = jnp.sum(d * xh, axis=0, keepdims=True)

        @pl.when(i == 0)
        def _():
            dg_ref[...] = part

        @pl.when(i > 0)
        def _():
            dg_ref[...] += part

    row = pl.BlockSpec((tm, D), lambda i: (i, 0))
    vec = pl.BlockSpec((1, D), lambda i: (0, 0))
    in_specs = [row, row, vec, row]
    args = [dh, x, g, dres]
    if dep is not None:
        in_specs.append(_dep_spec(1))
        args.append(dep)
    out_specs = [row, row, vec] if half else [row, vec]
    out_shape = [jax.ShapeDtypeStruct((T, D), F32)] + ([jax.ShapeDtypeStruct((T, D), BF16)] if half else [])
    out_shape.append(jax.ShapeDtypeStruct((1, D), F32))
    outs = pl.pallas_call(
        body, name=name, grid=(T // tm,),
        in_specs=in_specs, out_specs=out_specs, out_shape=out_shape,
        compiler_params=_params(("arbitrary",)),
    )(*args)
    return tuple(outs) if half else (outs[0], None, outs[1])


def ple_loss(x, z, pp, g, target, name):
    T, D = x.shape
    tm = _tile(T, 256)

    def body(x_ref, z_ref, p_ref, g_ref, t_ref, loss_ref, dx_ref, dg_ref, dz_ref, dp_ref):
        i = pl.program_id(0)
        gate = _sigmoid(z_ref[...])
        ppv = p_ref[...]
        xv = x_ref[...] + gate * ppv
        gv = g_ref[...]
        r = lax.rsqrt(jnp.mean(xv * xv, axis=-1, keepdims=True) + RMS_EPS)
        xh = xv * r
        e = xh * gv - t_ref[...]
        lpart = 0.5 * jnp.sum(jnp.mean(e * e, axis=-1, keepdims=True), axis=0, keepdims=True)
        lrow = jnp.broadcast_to(lpart, (1, LANES))
        d = e * (1.0 / D)
        u = d * gv
        dx = r * (u - xh * jnp.mean(u * xh, axis=-1, keepdims=True))
        dx_ref[...] = dx
        dz_ref[...] = (dx * ppv * gate * (1.0 - gate)).astype(BF16)
        dp_ref[...] = (dx * gate).astype(BF16)
        part = jnp.sum(d * xh, axis=0, keepdims=True)

        @pl.when(i == 0)
        def _():
            dg_ref[...] = part
            loss_ref[...] = lrow

        @pl.when(i > 0)
        def _():
            dg_ref[...] += part
            loss_ref[...] += lrow

    row = pl.BlockSpec((tm, D), lambda i: (i, 0))
    vec = pl.BlockSpec((1, D), lambda i: (0, 0))
    return pl.pallas_call(
        body, name=name, grid=(T // tm,),
        in_specs=[row, row, row, vec, row],
        out_specs=[pl.BlockSpec((1, LANES), lambda i: (0, 0)), row, vec, row, row],
        out_shape=[jax.ShapeDtypeStruct((1, LANES), F32), jax.ShapeDtypeStruct((T, D), F32),
                   jax.ShapeDtypeStruct((1, D), F32), jax.ShapeDtypeStruct((T, D), BF16),
                   jax.ShapeDtypeStruct((T, D), BF16)],
        compiler_params=_params(("arbitrary",)),
    )(x, z, pp, g, target)


def _bf(v, scale=None):
    if scale is not None:
        v = v * scale
    return v.astype(BF16)


def mm_nn(a, b, name, *, tn, out_dtype, tm=512, n_out=None, b_block=None, b_map=None,
          res=None):
    T, K = a.shape
    N = n_out if n_out is not None else b.shape[1]
    tm = _tile(T, tm)
    tn = _tile(N, tn)
    b_block = b_block or (K, tn)
    b_map = b_map or (lambda n, i: (0, n))

    def body(*refs):
        a_ref, b_ref = refs[0], refs[1]
        o_ref = refs[-1]
        acc = _dot(_bf(a_ref[...]), _bf(b_ref[...]), 1, 0)
        if res is not None:
            acc = refs[2][...] + acc
        o_ref[...] = acc.astype(out_dtype)

    in_specs = [pl.BlockSpec((tm, K), lambda n, i: (i, 0)), pl.BlockSpec(b_block, b_map)]
    args = [a, b]
    if res is not None:
        in_specs.append(pl.BlockSpec((tm, tn), lambda n, i: (i, n)))
        args.append(res)
    return pl.pallas_call(
        body, name=name, grid=(N // tn, T // tm), in_specs=in_specs,
        out_specs=pl.BlockSpec((tm, tn), lambda n, i: (i, n)),
        out_shape=jax.ShapeDtypeStruct((T, N), out_dtype),
        compiler_params=_params(("parallel", "parallel")),
    )(*args)


def mm_nn_sum(pairs, name, *, tn, out_dtype, tm=512, dep=None):
    T = pairs[0][0].shape[0]
    N = pairs[0][1].shape[1]
    tm = _tile(T, tm)
    tn = _tile(N, tn)
    npair = len(pairs)

    def body(*refs):
        acc = None
        for q in range(npair):
            part = _dot(_bf(refs[2 * q][...]), _bf(refs[2 * q + 1][...]), 1, 0)
            acc = part if acc is None else acc + part
        refs[-1][...] = acc.astype(out_dtype)

    in_specs, args = [], []
    for a, b in pairs:
        K = a.shape[1]
        in_specs += [pl.BlockSpec((tm, K), lambda n, i: (i, 0)), pl.BlockSpec((K, tn), lambda n, i: (0, n))]
        args += [a, b]
    if dep is not None:
        in_specs.append(_dep_spec(2))
        args.append(dep)
    return pl.pallas_call(
        body, name=name, grid=(N // tn, T // tm), in_specs=in_specs,
        out_specs=pl.BlockSpec((tm, tn), lambda n, i: (i, n)),
        out_shape=jax.ShapeDtypeStruct((T, N), out_dtype),
        compiler_params=_params(("parallel", "parallel")),
    )(*args)


def mm_nt(pairs, name, *, tn, out_dtype, tm=512, dep=None):
    T = pairs[0][0].shape[0]
    N = pairs[0][1].shape[0]
    tm = _tile(T, tm)
    tn = _tile(N, tn)
    npair = len(pairs)

    def body(*refs):
        o_ref = refs[-1]
        acc = None
        for q in range(npair):
            part = _dot(_bf(refs[2 * q][...]), _bf(refs[2 * q + 1][...]), 1, 1)
            acc = part if acc is None else acc + part
        o_ref[...] = acc.astype(out_dtype)

    in_specs, args = [], []
    for a, b in pairs:
        K = a.shape[1]
        in_specs += [pl.BlockSpec((tm, K), lambda n, i: (i, 0)), pl.BlockSpec((tn, K), lambda n, i: (n, 0))]
        args += [a, b]
    if dep is not None:
        in_specs.append(_dep_spec(2))
        args.append(dep)
    return pl.pallas_call(
        body, name=name, grid=(N // tn, T // tm), in_specs=in_specs,
        out_specs=pl.BlockSpec((tm, tn), lambda n, i: (i, n)),
        out_shape=jax.ShapeDtypeStruct((T, N), out_dtype),
        compiler_params=_params(("parallel", "parallel")),
    )(*args)


def mm_tn(a, b, name, *, grid, a_block, a_map, b_block, b_map, o_block, o_map, out_shape,
          b_scale=None, dep=None):
    def body(a_ref, b_ref, *rest):
        rest[-1][...] = _dot(_bf(a_ref[...]), _bf(b_ref[...], b_scale), 0, 0).astype(BF16)

    in_specs = [pl.BlockSpec(a_block, a_map), pl.BlockSpec(b_block, b_map)]
    args = [a, b]
    if dep is not None:
        in_specs.append(_dep_spec(len(grid)))
        args.append(dep)
    return pl.pallas_call(
        body, name=name, grid=grid, in_specs=in_specs,
        out_specs=pl.BlockSpec(o_block, o_map),
        out_shape=jax.ShapeDtypeStruct(out_shape, BF16),
        compiler_params=_params(("parallel",) * len(grid)),
    )(*args)


def mm_tn_plain(a, b, name, *, tm=512, tn=1024, b_scale=None):
    T, M = a.shape
    N = b.shape[1]
    tm = _tile(M, tm)
    tn = _tile(N, tn)
    return mm_tn(a, b, name, grid=(M // tm, N // tn),
                 a_block=(T, tm), a_map=lambda m, n: (0, m),
                 b_block=(T, tn), b_map=lambda m, n: (0, n),
                 o_block=(tm, tn), o_map=lambda m, n: (m, n),
                 out_shape=(M, N), b_scale=b_scale)


def ffn_up(h, wgu, gi, ui, nper, name):
    T, D = h.shape
    F8 = wgu.shape[0] // (N_DEV * nper)
    tm = _tile(T, 512)
    nt = T // tm

    def body(h_ref, wg_ref, wu_ref, ga_ref, gb_ref, s_ref):
        hv = h_ref[...]
        a = _dot(hv, wg_ref[...], 1, 1)
        b = _dot(hv, wu_ref[...], 1, 1)
        sg = _sigmoid(a)
        silu = a * sg
        ga_ref[...] = (b * (sg * (1.0 + a * (1.0 - sg)))).astype(BF16)
        gb_ref[...] = silu.astype(BF16)
        s_ref[...] = (silu * b).astype(BF16)

    blk = pl.BlockSpec((tm, F8), lambda j, i: (j * nt + i, 0))
    shp = jax.ShapeDtypeStruct((N_DEV * T, F8), BF16)
    return pl.pallas_call(
        body, name=name, grid=(N_DEV, nt),
        in_specs=[pl.BlockSpec((tm, D), lambda j, i: (i, 0)),
                  pl.BlockSpec((F8, D), lambda j, i: (j * nper + gi, 0)),
                  pl.BlockSpec((F8, D), lambda j, i: (j * nper + ui, 0))],
        out_specs=[blk, blk, blk], out_shape=[shp, shp, shp],
        compiler_params=_params(("parallel", "parallel")),
    )(h, wgu, wgu)


def ffn_gate(h, wg, name):
    T, D = h.shape
    F8 = wg.shape[0] // N_DEV
    tm = _tile(T, 512)
    nt = T // tm

    def body(h_ref, wg_ref, a_ref):
        a_ref[...] = _dot(h_ref[...], wg_ref[...], 1, 1).astype(BF16)

    return pl.pallas_call(
        body, name=name, grid=(N_DEV, nt),
        in_specs=[pl.BlockSpec((tm, D), lambda j, i: (i, 0)), pl.BlockSpec((F8, D), lambda j, i: (j, 0))],
        out_specs=pl.BlockSpec((tm, F8), lambda j, i: (j * nt + i, 0)),
        out_shape=jax.ShapeDtypeStruct((N_DEV * T, F8), BF16),
        compiler_params=_params(("parallel", "parallel")),
    )(h, wg)


def ffn_up_gated(h, wu, a, name):
    T, D = h.shape
    F8 = wu.shape[0] // N_DEV
    tm = _tile(T, 512)
    nt = T // tm

    def body(h_ref, wu_ref, a_ref, ga_ref, gb_ref, s_ref):
        av = a_ref[...].astype(F32)
        b = _dot(h_ref[...], wu_ref[...], 1, 1)
        sg = _sigmoid(av)
        silu = av * sg
        ga_ref[...] = (b * (sg * (1.0 + av * (1.0 - sg)))).astype(BF16)
        gb_ref[...] = silu.astype(BF16)
        s_ref[...] = (silu * b).astype(BF16)

    blk = pl.BlockSpec((tm, F8), lambda j, i: (j * nt + i, 0))
    shp = jax.ShapeDtypeStruct((N_DEV * T, F8), BF16)
    return pl.pallas_call(
        body, name=name, grid=(N_DEV, nt),
        in_specs=[pl.BlockSpec((tm, D), lambda j, i: (i, 0)), pl.BlockSpec((F8, D), lambda j, i: (j, 0)), blk],
        out_specs=[blk, blk, blk], out_shape=[shp, shp, shp],
        compiler_params=_params(("parallel", "parallel")),
    )(h, wu, a)


def ffn_down(s, wd, di, nper, x, g_next, name):
    T, D = x.shape
    F8 = s.shape[1]
    tm = _tile(T, 512)
    nt = T // tm

    def body(s_ref, w_ref, x_ref, g_ref, o_ref, h_ref, acc_ref):
        j = pl.program_id(1)
        part = _dot(s_ref[...], w_ref[...], 1, 0)

        @pl.when(j == 0)
        def _():
            acc_ref[...] = part

        @pl.when(j > 0)
        def _():
            acc_ref[...] += part

        @pl.when(j == N_DEV - 1)
        def _():
            xv = x_ref[...] + 0.5 * acc_ref[...]
            o_ref[...] = xv
            r = lax.rsqrt(jnp.mean(xv * xv, axis=-1, keepdims=True) + RMS_EPS)
            h_ref[...] = (xv * r * g_ref[...]).astype(BF16)

    row = pl.BlockSpec((tm, D), lambda i, j: (i, 0))
    return pl.pallas_call(
        body, name=name, grid=(nt, N_DEV),
        in_specs=[pl.BlockSpec((tm, F8), lambda i, j: (j * nt + i, 0)),
                  pl.BlockSpec((F8, D), lambda i, j: (j * nper + di, 0)),
                  row, pl.BlockSpec((1, D), lambda i, j: (0, 0))],
        out_specs=[row, row],
        out_shape=[jax.ShapeDtypeStruct((T, D), F32), jax.ShapeDtypeStruct((T, D), BF16)],
        scratch_shapes=[pltpu.VMEM((tm, D), F32)],
        compiler_params=_params(("parallel", "arbitrary")),
    )(s, wd, x, g_next)


def ffn_bwd_act(dxh, wd, di, nper_d, a, b, name, dep=None):
    T, D = dxh.shape
    F8 = a.shape[1]
    tm = _tile(T, 1024)
    nt = T // tm

    def body(dx_ref, w_ref, a_ref, b_ref, *rest):
        da_ref, db_ref = rest[-2], rest[-1]
        ds = _dot(dx_ref[...], w_ref[...], 1, 1)
        da_ref[...] = (ds * a_ref[...].astype(F32)).astype(BF16)
        db_ref[...] = (ds * b_ref[...].astype(F32)).astype(BF16)

    blk = pl.BlockSpec((tm, F8), lambda j, i: (j * nt + i, 0))
    shp = jax.ShapeDtypeStruct((N_DEV * T, F8), BF16)
    in_specs = [pl.BlockSpec((tm, D), lambda j, i: (i, 0)),
                pl.BlockSpec((F8, D), lambda j, i: (j * nper_d + di, 0)), blk, blk]
    args = [dxh, wd, a, b]
    if dep is not None:
        in_specs.append(_dep_spec(2))
        args.append(dep)
    return pl.pallas_call(
        body, name=name, grid=(N_DEV, nt), in_specs=in_specs,
        out_specs=[blk, blk], out_shape=[shp, shp],
        compiler_params=_params(("parallel", "parallel")),
    )(*args)


def ffn_bwd_dh(da, db, wg, wu, gi, ui, nper, D, name, dep=None):
    F8 = da.shape[1]
    T = da.shape[0] // N_DEV
    tm = _tile(T, 512)
    nt = T // tm

    def body(da_ref, db_ref, wg_ref, wu_ref, *rest):
        o_ref, acc_ref = rest[-2], rest[-1]
        j = pl.program_id(1)
        part = _dot(da_ref[...], wg_ref[...], 1, 0) + _dot(db_ref[...], wu_ref[...], 1, 0)

        @pl.when(j == 0)
        def _():
            acc_ref[...] = part

        @pl.when(j > 0)
        def _():
            acc_ref[...] += part

        @pl.when(j == N_DEV - 1)
        def _():
            o_ref[...] = acc_ref[...]

    blk = pl.BlockSpec((tm, F8), lambda i, j: (j * nt + i, 0))
    in_specs = [blk, blk,
                pl.BlockSpec((F8, D), lambda i, j: (j * nper + gi, 0)),
                pl.BlockSpec((F8, D), lambda i, j: (j * nper + ui, 0))]
    args = [da, db, wg, wu]
    if dep is not None:
        in_specs.append(_dep_spec(2))
        args.append(dep)
    return pl.pallas_call(
        body, name=name, grid=(nt, N_DEV), in_specs=in_specs,
        out_specs=pl.BlockSpec((tm, D), lambda i, j: (i, 0)),
        out_shape=jax.ShapeDtypeStruct((T, D), F32),
        scratch_shapes=[pltpu.VMEM((tm, D), F32)],
        compiler_params=_params(("parallel", "arbitrary")),
    )(*args)


def ffn_dw_side(act, other, c, mine, name, add=None, dep=None):
    F8 = act.shape[1]
    T, D = other.shape
    tm = _tile(D, 1024)

    def body(c_ref, a_ref, b_ref, *rest):
        acc = _dot(a_ref[...], b_ref[...], 0, 0)
        if add is not None:
            acc = acc + rest[0][...].astype(F32)
        rest[-1][...] = acc.astype(BF16)

    def shard(q, cr):
        return 2 * q + (cr[0] if mine else 1 - cr[0])

    in_specs = [pl.BlockSpec((T, F8), lambda q, m, cr: (shard(q, cr), 0)),
                pl.BlockSpec((T, tm), lambda q, m, cr: (0, m))]
    args = [act, other]
    if add is not None:
        in_specs.append(pl.BlockSpec((F8, tm), lambda q, m, cr: (q, m)))
        args.append(add)
    if dep is not None:
        in_specs.append(pl.BlockSpec((8, LANES), lambda q, m, cr: (0, 0)))
        args.append(dep)
    grid_spec = pltpu.PrefetchScalarGridSpec(
        num_scalar_prefetch=1, grid=(N_DEV // 2, D // tm), in_specs=in_specs,
        out_specs=pl.BlockSpec((F8, tm), lambda q, m, cr: (q, m)))
    return pl.pallas_call(
        body, name=name, grid_spec=grid_spec,
        out_shape=jax.ShapeDtypeStruct((N_DEV // 2 * F8, D), BF16),
        compiler_params=_params(("parallel", "parallel")),
    )(c.reshape(1).astype(jnp.int32), *args)


def _t5_bucket_np(dist):
    max_exact = N_REL_BUCKETS // 2
    d = np.maximum(dist, 1).astype(np.float64)
    large = max_exact + (np.log(d / max_exact) / math.log(REL_MAX_DISTANCE / max_exact)
                         * (N_REL_BUCKETS - max_exact)).astype(np.int64)
    large32 = max_exact + (np.log(d.astype(np.float32) / np.float32(max_exact))
                           / np.float32(math.log(REL_MAX_DISTANCE / max_exact))
                           * np.float32(N_REL_BUCKETS - max_exact)).astype(np.int64)
    assert np.array_equal(large, large32)
    large = np.minimum(large, N_REL_BUCKETS - 1)
    return np.where(dist < max_exact, dist, large)


def _distance_tables(T, tq):
    dist = np.arange(T)
    mult = np.zeros(T, np.int64)
    for window, dilation in DILATED_PATTERNS:
        mult += ((dist % dilation == 0) & (dist // dilation <= window // dilation)).astype(np.int64)
    logm = np.where(mult > 0, np.log(np.maximum(mult, 1)), NEG_INF).astype(np.float32)
    bucket = _t5_bucket_np(dist).astype(np.int32)
    nkb = T // tq
    k = np.arange(nkb)[:, None, None]
    r = np.arange(tq)[None, :, None]
    c = np.arange(tq)[None, None, :]
    delta = k * tq + r - c
    return bucket, logm, delta


def _tile_buckets(T, tq):
    bucket, logm, delta = _distance_tables(T, tq)
    safe = np.maximum(delta, 0)
    bidx = np.where(delta >= 0, bucket[safe], -1).astype(np.int32)
    logm_t = np.where(delta >= 0, logm[safe], NEG_INF).astype(np.float32)
    present = [sorted(set(np.unique(bidx[k]).tolist()) - {-1}) for k in range(T // tq)]
    return bidx, logm_t, present


def bias_tiles(rel_table, T, tq):
    bidx, logm_t, present = _tile_buckets(T, tq)
    nkb = T // tq

    def body(tab_ref, b_ref, lm_ref, o_ref):
        slot = pl.program_id(0)

        @pl.when(slot == 0)
        def _():
            o_ref[...] = jnp.where(b_ref[...] >= 0, 0.0, NEG_INF)

        @pl.when(slot > 0)
        def _():
            for k in range(nkb):
                bi = b_ref[k]
                acc = lm_ref[k]
                for b in present[k]:
                    acc = acc + jnp.where(bi == b, tab_ref[b, slot - 1], 0.0)
                o_ref[k] = acc

    full = pl.BlockSpec((nkb, tq, tq), lambda s: (0, 0, 0))
    return pl.pallas_call(
        body, name="bias_tiles", grid=(1 + N_DIL,),
        in_specs=[pl.BlockSpec(memory_space=pltpu.SMEM), full, full],
        out_specs=pl.BlockSpec((None, nkb, tq, tq), lambda s: (s, 0, 0, 0)),
        out_shape=jax.ShapeDtypeStruct((1 + N_DIL, nkb, tq, tq), F32),
        compiler_params=_params(("parallel",)),
    )(rel_table, jnp.asarray(bidx), jnp.asarray(logm_t))


def fox_gate_fwd(uf, bf, name):
    T = uf.shape[0]
    tb = _tile(T, 512)

    def body(u_ref, b_ref, c_ref, ct_ref):
        lane = lax.broadcasted_iota(jnp.int32, (1, LANES), 1)
        tri = (lax.broadcasted_iota(jnp.int32, (tb, tb), 0)
               >= lax.broadcasted_iota(jnp.int32, (tb, tb), 1)).astype(F32)
        carry = jnp.zeros((1, LANES), F32)
        for blk in range(T // tb):
            z = u_ref[pl.ds(blk * tb, tb), :] + b_ref[...]
            lf = jnp.minimum(z, 0.0) - jnp.log1p(jnp.exp(-jnp.abs(z)))
            lf = jnp.where(lane < N_FOX, lf, 0.0)
            cb = _dot(tri, lf, 1, 0, precision=lax.Precision.HIGHEST) + carry
            c_ref[pl.ds(blk * tb, tb), :] = cb
            ct_ref[:, pl.ds(blk * tb, tb)] = cb.T
            carry = cb[tb - 1:tb, :]

    return pl.pallas_call(
        body, name=name,
        out_shape=[jax.ShapeDtypeStruct((T, LANES), F32), jax.ShapeDtypeStruct((LANES, T), F32)],
        compiler_params=_params(None),
    )(uf, bf)


def fox_gate_bwd(dct, uf, bf, name):
    T = uf.shape[0]
    tb = _tile(T, 512)

    def body(d_ref, u_ref, b_ref, du_ref, db_ref):
        lane = lax.broadcasted_iota(jnp.int32, (1, LANES), 1)
        triu = (lax.broadcasted_iota(jnp.int32, (tb, tb), 0)
                <= lax.broadcasted_iota(jnp.int32, (tb, tb), 1)).astype(F32)
        carry = jnp.zeros((1, LANES), F32)
        dbv = jnp.zeros((1, LANES), F32)
        for blk in reversed(range(T // tb)):
            dc = d_ref[:, pl.ds(blk * tb, tb)].T
            dlf = _dot(triu, dc, 1, 0, precision=lax.Precision.HIGHEST) + carry
            carry = dlf[0:1, :]
            z = u_ref[pl.ds(blk * tb, tb), :] + b_ref[...]
            dz = jnp.where(lane < N_FOX, dlf * (1.0 - _sigmoid(z)), 0.0)
            du_ref[pl.ds(blk * tb, tb), :] = dz
            dbv = dbv + jnp.sum(dz, axis=0, keepdims=True)
        db_ref[...] = dbv

    return pl.pallas_call(
        body, name=name,
        out_shape=[jax.ShapeDtypeStruct((T, LANES), F32), jax.ShapeDtypeStruct((1, LANES), F32)],
        compiler_params=_params(None),
    )(dct, uf, bf)


def _bias_slot(h):
    return jnp.maximum(h - (N_FOX - 1), 0)


def _scores(q_ref, k_ref, c_ref, ct_ref, tb_ref, h, i, tq, fox):
    scale = HEAD_DIM ** -0.5
    n = (i + 1) * tq
    rows = pl.ds(i * tq, tq)
    s = _dot(q_ref[rows, :], k_ref[pl.ds(0, n), :], 1, 1) * scale
    if not fox:
        return s + jnp.concatenate([tb_ref[i - jb] for jb in range(i + 1)], axis=1)
    lane = lax.broadcasted_iota(jnp.int32, (1, LANES), 1)
    c_col = jnp.sum(jnp.where(lane == h, c_ref[rows, :], 0.0), axis=1, keepdims=True)
    c_row = ct_ref[pl.ds(h, 1), pl.ds(0, n)]
    s = s + (c_col - c_row)
    if i == 0:
        return s + tb_ref[0]
    return jnp.concatenate([s[:, :i * tq], s[:, i * tq:] + tb_ref[0]], axis=1)


def _attn_specs(T, tq):
    nkb = T // tq
    return [
        pl.BlockSpec((T, HEAD_DIM), lambda h: (0, h)),
        pl.BlockSpec((T, HEAD_DIM), lambda h: (0, N_HEADS + h)),
        pl.BlockSpec((T, HEAD_DIM), lambda h: (0, 2 * N_HEADS + h)),
        pl.BlockSpec((T, LANES), lambda h: (0, 0)),
        pl.BlockSpec((LANES, T), lambda h: (0, 0)),
        pl.BlockSpec((None, nkb, tq, tq), lambda h: (_bias_slot(h), 0, 0, 0)),
    ]


def attention_fwd(qkv, c, ct, tiles, name):
    T = qkv.shape[0]
    tq = tiles.shape[2]

    def body(q_ref, k_ref, v_ref, c_ref, ct_ref, tb_ref, o_ref, lse_ref):
        h = pl.program_id(0)
        lane = lax.broadcasted_iota(jnp.int32, (1, LANES), 1)

        @pl.when(h == 0)
        def _():
            lse_ref[...] = jnp.zeros_like(lse_ref)

        def head(fox):
            for i in range(T // tq):
                rows = pl.ds(i * tq, tq)
                s = _scores(q_ref, k_ref, c_ref, ct_ref, tb_ref, h, i, tq, fox)
                m = jnp.max(s, axis=1, keepdims=True)
                p = jnp.exp(s - m)
                l = jnp.sum(p, axis=1, keepdims=True)
                o = _dot(p.astype(BF16), v_ref[pl.ds(0, (i + 1) * tq), :], 1, 0) * (1.0 / l)
                o_ref[rows, :] = o.astype(BF16)
                lse_ref[rows, :] = jnp.where(lane == h, m + jnp.log(l), lse_ref[rows, :])

        pl.when(h < N_FOX)(functools.partial(head, True))
        pl.when(h >= N_FOX)(functools.partial(head, False))

    return pl.pallas_call(
        body, name=name, grid=(N_HEADS,),
        in_specs=_attn_specs(T, tq),
        out_specs=[pl.BlockSpec((T, HEAD_DIM), lambda h: (0, h)), pl.BlockSpec((T, LANES), lambda h: (0, 0))],
        out_shape=[jax.ShapeDtypeStruct((T, N_HEADS * HEAD_DIM), BF16), jax.ShapeDtypeStruct((T, LANES), F32)],
        compiler_params=_params(("arbitrary",)),
    )(qkv, qkv, qkv, c, ct, tiles)


def attention_bwd(qkv, c, ct, tiles, lse, o, do, name):
    T = qkv.shape[0]
    tq = tiles.shape[2]
    nkb = T // tq
    scale = HEAD_DIM ** -0.5

    def body(q_ref, k_ref, v_ref, c_ref, ct_ref, tb_ref, lse_ref, o_ref, do_ref,
             dq_ref, dk_ref, dv_ref, dct_ref, dtb_ref, dk_acc, dv_acc):
        h = pl.program_id(0)
        lane = lax.broadcasted_iota(jnp.int32, (1, LANES), 1)
        dk_acc[...] = jnp.zeros_like(dk_acc)
        dv_acc[...] = jnp.zeros_like(dv_acc)
        dct_ref[...] = jnp.zeros_like(dct_ref)
        dtb_ref[...] = jnp.zeros_like(dtb_ref)

        def head(fox):
            for i in range(nkb):
                rows, keys = pl.ds(i * tq, tq), pl.ds(0, (i + 1) * tq)
                s = _scores(q_ref, k_ref, c_ref, ct_ref, tb_ref, h, i, tq, fox)
                lse_col = jnp.sum(jnp.where(lane == h, lse_ref[rows, :], 0.0), axis=1, keepdims=True)
                p = jnp.exp(s - lse_col)
                p_b = p.astype(BF16)
                dov = do_ref[rows, :]
                dp = _dot(dov, v_ref[keys, :], 1, 1)
                if fox:
                    delta = jnp.sum(p * dp, axis=1, keepdims=True)
                else:
                    delta = jnp.sum(dov.astype(F32) * o_ref[rows, :].astype(F32), axis=1, keepdims=True)
                ds = p * (dp - delta)
                ds_b = ds.astype(BF16)
                dq_ref[rows, :] = (_dot(ds_b, k_ref[keys, :], 1, 0) * scale).astype(BF16)
                dk_acc[:, keys] += _dot(q_ref[rows, :], ds_b, 0, 0) * scale
                dv_acc[:, keys] += _dot(dov, p_b, 0, 0)
                if fox:
                    dct_ref[:, keys] += -jnp.sum(ds, axis=0, keepdims=True)
                else:
                    for jb in range(i + 1):
                        dtb_ref[i - jb] += ds[:, jb * tq:(jb + 1) * tq]

        pl.when(h < N_FOX)(functools.partial(head, True))
        pl.when(h >= N_FOX)(functools.partial(head, False))
        dk_ref[...] = dk_acc[...].T.astype(BF16)
        dv_ref[...] = dv_acc[...].T.astype(BF16)

    head_cols = jax.ShapeDtypeStruct((T, N_HEADS * HEAD_DIM), BF16)
    col = pl.BlockSpec((T, HEAD_DIM), lambda h: (0, h))
    return pl.pallas_call(
        body, name=name, grid=(N_HEADS,),
        in_specs=_attn_specs(T, tq) + [pl.BlockSpec((T, LANES), lambda h: (0, 0)), col, col],
        out_specs=[col, col, col,
                   pl.BlockSpec((None, 1, T), lambda h: (h, 0, 0)),
                   pl.BlockSpec((None, nkb, tq, tq), lambda h: (_bias_slot(h), 0, 0, 0))],
        out_shape=[head_cols, head_cols, head_cols,
                   jax.ShapeDtypeStruct((N_HEADS, 1, T), F32),
                   jax.ShapeDtypeStruct((1 + N_DIL, nkb, tq, tq), F32)],
        scratch_shapes=[pltpu.VMEM((HEAD_DIM, T), F32), pltpu.VMEM((HEAD_DIM, T), F32)],
        compiler_params=_params(("arbitrary",)),
    )(qkv, qkv, qkv, c, ct, tiles, lse, o, do)


def rel_table_grad(dtiles, T, name):
    tq = dtiles.shape[2]
    nkb = T // tq
    bidx, _, present = _tile_buckets(T, tq)

    def body(d_ref, b_ref, o_ref):
        lane = lax.broadcasted_iota(jnp.int32, (1, LANES), 1)
        row = jnp.zeros((1, LANES), F32)
        for k in range(nkb):
            d = d_ref[k]
            bi = b_ref[k]
            for b in present[k]:
                v = jnp.sum(jnp.sum(jnp.where(bi == b, d, 0.0), axis=0, keepdims=True),
                            axis=1, keepdims=True)
                row = row + jnp.where(lane == b, v, 0.0)
        o_ref[...] = row

    return pl.pallas_call(
        body, name=name, grid=(N_DIL,),
        in_specs=[pl.BlockSpec((None, nkb, tq, tq), lambda h: (h + 1, 0, 0, 0)),
                  pl.BlockSpec((nkb, tq, tq), lambda h: (0, 0, 0))],
        out_specs=pl.BlockSpec((None, 1, LANES), lambda h: (h, 0, 0)),
        out_shape=jax.ShapeDtypeStruct((N_DIL, 1, LANES), F32),
        compiler_params=_params(("parallel",)),
    )(dtiles, jnp.asarray(bidx))


def _peer_list():
    x, y, c = lax.axis_index("x"), lax.axis_index("y"), lax.axis_index("c")
    me = 4 * x + 2 * y + c
    peers = []
    for fx in (0, 1):
        for fy in (0, 1):
            for fc in (0, 1):
                if fx or fy or fc:
                    px = 1 - x if fx else x
                    py = 1 - y if fy else y
                    pc = 1 - c if fc else c
                    peers.append(((px, py, pc), 4 * px + 2 * py + pc))
    return me, peers


_HBM = pl.BlockSpec(memory_space=pltpu.HBM)
_SEM = pl.BlockSpec(memory_space=pltpu.SEMAPHORE)
_EFFECT = pltpu.SideEffectType.DATAFLOW_SIDE_EFFECTING
N_PEERS = N_DEV - 1


def _in_hbm(a):
    return pltpu.with_memory_space_constraint(a, pltpu.HBM)


def _exchange_copies(srcs, lands, send_sems, recv_sems, blockwise):
    me, peers = _peer_list()
    sends, recvs = [], []
    for a in range(len(srcs)):
        for k, (dev, idx) in enumerate(peers):
            src = srcs[a].at[idx] if blockwise[a] else srcs[a]
            sends.append(pltpu.make_async_remote_copy(
                src_ref=src, dst_ref=lands[a].at[me], send_sem=send_sems[a].at[k],
                recv_sem=recv_sems[a].at[k], device_id=dev, device_id_type=MESH))
            recvs.append(pltpu.make_async_remote_copy(
                src_ref=src, dst_ref=lands[a].at[idx], send_sem=send_sems[a].at[k],
                recv_sem=recv_sems[a].at[k], device_id=dev, device_id_type=MESH))
    return sends, recvs


def exchange_start(srcs, lands, blockwise, name):
    n = len(srcs)

    def body(*refs):
        src_in, land_in = refs[:n], refs[n:2 * n]
        send_sems, recv_sems = refs[2 * n:3 * n], refs[3 * n:4 * n]
        token = refs[6 * n]
        sends, _ = _exchange_copies(src_in, land_in, send_sems, recv_sems, blockwise)
        for cp in sends:
            cp.start()
        token[...] = jnp.zeros_like(token)

    out_shape = ([pltpu.SemaphoreType.DMA((N_PEERS,))] * (2 * n)
                 + [pltpu.HBM(s.shape, s.dtype) for s in srcs]
                 + [pltpu.HBM(l.shape, l.dtype) for l in lands]
                 + [jax.ShapeDtypeStruct((8, LANES), F32)])
    aliases = {a: 2 * n + a for a in range(2 * n)}
    outs = pl.pallas_call(
        body, name=name, out_shape=out_shape,
        in_specs=[_HBM] * (2 * n),
        out_specs=[_SEM] * (2 * n) + [_HBM] * (2 * n) + [pl.BlockSpec(memory_space=pltpu.VMEM)],
        input_output_aliases=aliases,
        compiler_params=pltpu.CompilerParams(has_side_effects=_EFFECT),
    )(*[_in_hbm(s) for s in srcs], *[_in_hbm(l) for l in lands])
    return (outs[:n], outs[n:2 * n], outs[2 * n:3 * n], outs[3 * n:4 * n], outs[4 * n])


def exchange_wait(send_sems, recv_sems, srcs, lands, blockwise, after, name):
    n = len(srcs)

    def body(*refs):
        src_in, land_in = refs[:n], refs[n:2 * n]
        ss, rs = refs[2 * n:3 * n], refs[3 * n:4 * n]
        sends, recvs = _exchange_copies(src_in, land_in, ss, rs, blockwise)
        for cp in sends:
            cp.wait_send()
        for cp in recvs:
            cp.wait_recv()

    outs = pl.pallas_call(
        body, name=name,
        out_shape=[pltpu.HBM(s.shape, s.dtype) for s in srcs] + [pltpu.HBM(l.shape, l.dtype) for l in lands],
        in_specs=[_HBM] * (2 * n) + [_SEM] * (2 * n) + [pl.BlockSpec(memory_space=pl.ANY)],
        out_specs=[_HBM] * (2 * n),
        input_output_aliases={a: a for a in range(2 * n)},
        compiler_params=pltpu.CompilerParams(has_side_effects=_EFFECT),
    )(*srcs, *lands, *send_sems, *recv_sems, after)
    return outs[n:]


def _landing(own_block, me, slots=N_DEV):
    empty = lax.empty((slots,) + own_block.shape, own_block.dtype)
    return lax.dynamic_update_slice(empty, own_block[None], (me,) + (0,) * own_block.ndim)


N_CHIPS = N_DEV // 2
_CHIP_FLIPS = ((1, 0), (0, 1), (1, 1))


def _xyc():
    return lax.axis_index("x"), lax.axis_index("y"), lax.axis_index("c")


def _other_chips(x, y):
    return [(1 - x if fx else x, 1 - y if fy else y) for fx, fy in _CHIP_FLIPS]


def _remote(src, dst, send_sem, recv_sem, dev):
    return pltpu.make_async_remote_copy(src_ref=src, dst_ref=dst, send_sem=send_sem, recv_sem=recv_sem,
                                        device_id=dev, device_id_type=MESH)


def comm_call(name, bufs, sems_in, sems_out, fn, after=None, want_token=False):
    nb, ni, no = len(bufs), len(sems_in), len(sems_out)
    afters = [] if after is None else (list(after) if isinstance(after, (list, tuple)) else [after])
    na = len(afters)

    def body(*refs):
        buf_refs = refs[:nb]
        sin = refs[nb:nb + ni]
        sout = refs[nb + ni + na:nb + ni + na + no]
        fn(buf_refs, sin, sout)
        if want_token:
            tok = refs[nb + ni + na + no + nb]
            tok[...] = jnp.zeros_like(tok)

    out_shape = list(sems_out) + [pltpu.HBM(b.shape, b.dtype) for b in bufs]
    out_specs = [_SEM] * no + [_HBM] * nb
    if want_token:
        out_shape.append(jax.ShapeDtypeStruct((8, LANES), F32))
        out_specs.append(pl.BlockSpec(memory_space=pltpu.VMEM))
    args = [_in_hbm(b) for b in bufs] + list(sems_in) + afters
    outs = pl.pallas_call(
        body, name=name, out_shape=out_shape,
        in_specs=[_HBM] * nb + [_SEM] * ni + [pl.BlockSpec(memory_space=pl.ANY)] * na,
        out_specs=out_specs, input_output_aliases={a: no + a for a in range(nb)},
        compiler_params=pltpu.CompilerParams(has_side_effects=_EFFECT),
    )(*args)
    return list(outs[:no]), list(outs[no:no + nb]), (outs[no + nb] if want_token else None)


def _dma_sems(*sizes):
    return [pltpu.SemaphoreType.DMA((s,)) for s in sizes]


def gather_start(srcs, lands, name, after=None):
    n = len(srcs)

    def fn(bufs, sin, sout):
        x, y, c = _xyc()
        me = 4 * x + 2 * y + c
        for a in range(n):
            src, land = bufs[a], bufs[n + a]
            send, recv_d, recv_i = sout[3 * a:3 * a + 3]
            _remote(src, land.at[me], send.at[0], recv_d.at[0], (x, y, 1 - c)).start()
            for k, (px, py) in enumerate(_other_chips(x, y)):
                _remote(src, land.at[me], send.at[1 + k], recv_i.at[k], (px, py, c)).start()

    return comm_call(name, list(srcs) + list(lands), [], _dma_sems(4, 1, 3) * n, fn, after=after, want_token=True)


def gather_forward(srcs, lands, recv_i, after, name):
    n = len(srcs)

    def fn(bufs, sin, sout):
        x, y, c = _xyc()
        for a in range(n):
            src, land = bufs[a], bufs[n + a]
            f_send, f_recv = sout[2 * a:2 * a + 2]
            for k, (px, py) in enumerate(_other_chips(x, y)):
                blk = land.at[4 * px + 2 * py + c]
                _remote(src, blk, f_send.at[k], sin[a].at[k], (px, py, c)).wait_recv()
                _remote(blk, blk, f_send.at[k], f_recv.at[k], (x, y, 1 - c)).start()

    sems, bufs, _ = comm_call(name, list(srcs) + list(lands), recv_i, _dma_sems(3, 3) * n, fn, after=after)
    return sems, bufs


def gather_wait(srcs, lands, send, recv_d, f_send, f_recv, after, name):
    n = len(srcs)

    def fn(bufs, sin, sout):
        x, y, c = _xyc()
        sib = (x, y, 1 - c)
        for a in range(n):
            src, land = bufs[a], bufs[n + a]
            s_send, s_recv_d, s_fsend, s_frecv = sin[4 * a:4 * a + 4]
            sib_blk = land.at[4 * x + 2 * y + 1 - c]
            for k in range(4):
                _remote(src, sib_blk, s_send.at[k], s_recv_d.at[0], sib).wait_send()
            _remote(src, sib_blk, s_send.at[0], s_recv_d.at[0], sib).wait_recv()
            for k, (px, py) in enumerate(_other_chips(x, y)):
                cp = _remote(src, land.at[4 * px + 2 * py + 1 - c], s_fsend.at[k], s_frecv.at[k], sib)
                cp.wait_send()
                cp.wait_recv()

    sems_in = []
    for a in range(n):
        sems_in += [send[a], recv_d[a], f_send[a], f_recv[a]]
    _, bufs, _ = comm_call(name, list(srcs) + list(lands), sems_in, [], fn, after=after)
    return bufs[n:]


def scatter_pair_start(src4s, lands, name, after=None, whole=False):
    n = len(src4s)

    def fn(bufs, sin, sout):
        x, y, c = _xyc()
        for a in range(n):
            src = bufs[a] if whole else bufs[a].at[:, 1 - c]
            _remote(src, bufs[n + a], sout[2 * a].at[0], sout[2 * a + 1].at[0], (x, y, 1 - c)).start()

    return comm_call(name, list(src4s) + list(lands), [], _dma_sems(1, 1) * n, fn, after=after, want_token=True)


def scatter_pair_wait(src4s, lands, sems, after, name, whole=False):
    n = len(src4s)

    def fn(bufs, sin, sout):
        x, y, c = _xyc()
        for a in range(n):
            src = bufs[a] if whole else bufs[a].at[:, 1 - c]
            cp = _remote(src, bufs[n + a], sin[2 * a].at[0], sin[2 * a + 1].at[0], (x, y, 1 - c))
            cp.wait_send()
            cp.wait_recv()

    _, bufs, _ = comm_call(name, list(src4s) + list(lands), sems, [], fn, after=after)
    return bufs[:n], bufs[n:]


def _row_tile(R):
    for cand in range(256, 15, -16):
        if R % cand == 0 and R // cand >= 4:
            return cand
    return R


def chip_sum(src4, land, c, name):
    _, _, R, C = src4.shape
    tr = R

    def body(c_ref, a_ref, b_ref, o_ref):
        o_ref[...] = (a_ref[...].astype(F32) + b_ref[...].astype(F32)).astype(BF16)

    grid_spec = pltpu.PrefetchScalarGridSpec(
        num_scalar_prefetch=1, grid=(N_CHIPS, R // tr),
        in_specs=[pl.BlockSpec((None, None, tr, C), lambda q, i, cr: (q, cr[0], i, 0)),
                  pl.BlockSpec((None, tr, C), lambda q, i, cr: (q, i, 0))],
        out_specs=pl.BlockSpec((None, tr, C), lambda q, i, cr: (q, i, 0)))
    return pl.pallas_call(
        body, name=name, grid_spec=grid_spec,
        out_shape=jax.ShapeDtypeStruct((N_CHIPS, R, C), BF16),
        compiler_params=_params(("parallel", "parallel")),
    )(c.reshape(1).astype(jnp.int32), src4, land)


def scatter_chip_start(sums, lands, name):
    n = len(sums)

    def fn(bufs, sin, sout):
        x, y, c = _xyc()
        for a in range(n):
            for k, (px, py) in enumerate(_other_chips(x, y)):
                _remote(bufs[a].at[2 * px + py], bufs[n + a].at[2 * x + y], sout[2 * a].at[k], sout[2 * a + 1].at[k],
                        (px, py, c)).start()

    return comm_call(name, list(sums) + list(lands), [], _dma_sems(3, 3) * n, fn, want_token=True)


def scatter_chip_wait(sums, lands, sems, after, name):
    n = len(sums)

    def fn(bufs, sin, sout):
        x, y, c = _xyc()
        for a in range(n):
            for k, (px, py) in enumerate(_other_chips(x, y)):
                cp = _remote(bufs[a].at[2 * px + py], bufs[n + a].at[2 * px + py], sin[2 * a].at[k],
                             sin[2 * a + 1].at[k], (px, py, c))
                cp.wait_send()
                cp.wait_recv()

    _, bufs, _ = comm_call(name, list(sums) + list(lands), sems, [], fn, after=after)
    return bufs[:n], bufs[n:]


def _adamw_math(w, g, m, v):
    m = ADAM_B1 * m + (1.0 - ADAM_B1) * g
    v = ADAM_B2 * v + (1.0 - ADAM_B2) * (g * g)
    m_hat = m / (1.0 - ADAM_B1 ** ADAM_STEP)
    v_hat = v / (1.0 - ADAM_B2 ** ADAM_STEP)
    delta = -ADAM_LR * (m_hat / (jnp.sqrt(v_hat) + ADAM_EPS) + ADAM_WD * w)
    return delta, m, v


def _sum_partials(p_ref, own_ref, mine):
    own = own_ref[...].astype(F32)
    g = None
    for s in range(p_ref.shape[0]):
        term = jnp.where(mine == s, own, p_ref[s].astype(F32))
        g = term if g is None else g + term
    return g


def adamw_sharded(parts, sums, my_chip, w, m, v, name):
    R, C = w.shape
    S = parts.shape[0]
    tr = _row_tile(R)

    def body(mc_ref, p_ref, o_ref, w_ref, m_ref, v_ref, g_ref, d_ref, nm_ref, nv_ref):
        g = _sum_partials(p_ref, o_ref, mc_ref[0])
        delta, nm, nv = _adamw_math(w_ref[...], g, m_ref[...], v_ref[...])
        g_ref[...] = g
        d_ref[...] = delta
        nm_ref[...] = nm
        nv_ref[...] = nv

    row = pl.BlockSpec((tr, C), lambda i, mc: (i, 0))
    shp = jax.ShapeDtypeStruct((R, C), F32)
    grid_spec = pltpu.PrefetchScalarGridSpec(
        num_scalar_prefetch=1, grid=(R // tr,),
        in_specs=[pl.BlockSpec((S, tr, C), lambda i, mc: (0, i, 0)),
                  pl.BlockSpec((None, tr, C), lambda i, mc: (mc[0], i, 0)), row, row, row],
        out_specs=[row, row, row, row])
    return pl.pallas_call(
        body, name=name, grid_spec=grid_spec, out_shape=[shp, shp, shp, shp],
        compiler_params=_params(("parallel",)),
    )(my_chip.reshape(1).astype(jnp.int32), parts, sums, w, m, v)


def adamw_small(parts, w, m, v, name):
    R, C = w.shape

    def body(p_ref, w_ref, m_ref, v_ref, g_ref, d_ref, nm_ref, nv_ref):
        g = p_ref[0]
        for s in range(1, N_DEV):
            g = g + p_ref[s]
        delta, nm, nv = _adamw_math(w_ref[...], g, m_ref[...], v_ref[...])
        g_ref[...] = g
        d_ref[...] = delta
        nm_ref[...] = nm
        nv_ref[...] = nv

    shp = jax.ShapeDtypeStruct((R, C), F32)
    return pl.pallas_call(
        body, name=name, out_shape=[shp, shp, shp, shp], compiler_params=_params(None),
    )(parts, w, m, v)


_ROW_NORM_FFN1, _ROW_NORM_MIX, _ROW_NORM_FFN2, _ROW_NORM_PLE, _ROW_NORM_FINAL = 0, 1, 2, 3, 4
_ROW_B_F, _ROW_REL, _ROW_LOSS, _SMALL_ROWS = 5, 6, 7, 8


def _pack_small(D, norm_ffn1, norm_mix, norm_ffn2, norm_ple, norm_final, b_f, rel_table):
    def row(v):
        v = v.reshape(1, -1)
        return jnp.pad(v, ((0, 0), (0, D - v.shape[1])))
    return jnp.concatenate([row(norm_ffn1), row(norm_mix), row(norm_ffn2), row(norm_ple),
                            row(norm_final), row(b_f), row(rel_table),
                            jnp.zeros((1, D), F32)], axis=0)


def _unpack_small(a, shapes):
    return {"norm_ffn1": a[_ROW_NORM_FFN1].reshape(shapes["norm_ffn1"]),
            "norm_mix": a[_ROW_NORM_MIX].reshape(shapes["norm_mix"]),
            "b_f": a[_ROW_B_F, :N_FOX].reshape(shapes["b_f"]),
            "norm_ffn2": a[_ROW_NORM_FFN2].reshape(shapes["norm_ffn2"]),
            "norm_ple": a[_ROW_NORM_PLE].reshape(shapes["norm_ple"]),
            "rel_table": a[_ROW_REL, :N_REL_BUCKETS * N_DIL].reshape(shapes["rel_table"]),
            "norm_final": a[_ROW_NORM_FINAL].reshape(shapes["norm_final"])}


def local_step(x, p, tgt, g_ffn1, g_mix, g_ffn2, g_ple, g_final, b_f, rel_table,
               forward, weights, emit, emit2, dw_theirs, dw_mine, first_dep):
    T, D = x.shape
    P = p.shape[1]
    CW = D // N_DEV
    tq = _tile(T, 256)

    h1 = rms_fwd(x, g_ffn1, "rms_ffn1", dep=first_dep)
    tiles = bias_tiles(rel_table, T, tq)
    forward("ffn1_g", [tiles, h1])
    wg1, = weights("ffn1_g", h1)
    gate1 = ffn_gate(h1, wg1, "ffn1_gate")
    forward("ffn1_u", gate1)
    wu1, = weights("ffn1_u", gate1)
    a1, b1, s1 = ffn_up_gated(h1, wu1, gate1, "ffn1_up")
    forward("ffn1_d", s1)
    wd1, = weights("ffn1_d", s1)
    x1, h2 = ffn_down(s1, wd1, 0, 1, x, g_mix, "ffn1_down")

    forward("mix_in", h2)
    w3, wf = weights("mix_in", h2)
    qkv = mm_nt([(h2, w3)], "mix_qkv", tn=768, out_dtype=BF16)
    uf = mm_nt([(h2, wf)], "mix_forget", tn=LANES, out_dtype=F32)
    bfp = jnp.pad(b_f.reshape(1, N_FOX), ((0, 0), (0, LANES - N_FOX)))
    c, ct = fox_gate_fwd(uf, bfp, "fox_gate")
    cat, lse = attention_fwd(qkv, c, ct, tiles, "attention")
    forward("mix_out", cat)
    wo, = weights("mix_out", cat)
    x2 = mm_nn(cat, wo, "mix_out", tn=512, out_dtype=F32, res=x1)

    h3 = rms_fwd(x2, g_ffn2, "rms_ffn2")
    forward("ffn2_gu", h3)
    wgu2, = weights("ffn2_gu", h3)
    a2, b2, s2 = ffn_up(h3, wgu2, 0, 1, 2, "ffn2_up")
    forward("ffn2_d", s2)
    wd2, = weights("ffn2_d", s2)
    x3, h4 = ffn_down(s2, wd2, 0, 1, x2, g_ple, "ffn2_down")
    forward("ple", x3)

    wpg, wpp = weights("ple", h4)
    z = mm_nn(h4, wpg, "ple_gate", tn=512, out_dtype=F32)
    pp = mm_nn(p, wpp, "ple_proj", tn=CW, tm=T, out_dtype=F32, n_out=D,
               b_block=(P, CW), b_map=lambda n, i: (n, 0))
    loss_row, dx4, dg_final, dz, dpp = ple_loss(x3, z, pp, g_final, tgt, "ple_loss")

    grads = {}
    grads["w_ple_proj"] = mm_tn(p, dpp, "ple_proj_dw", grid=(N_DEV,),
                                a_block=(T, P), a_map=lambda n: (0, 0),
                                b_block=(T, CW), b_map=lambda n: (0, n),
                                o_block=(P, CW), o_map=lambda n: (n, 0),
                                out_shape=(N_DEV * P, CW))
    grads["w_ple_gate"] = mm_tn_plain(h4, dz, "ple_gate_dw")
    tok = emit("ple", grads)
    dh4 = mm_nt([(dz, wpg)], "ple_gate_dh", tn=512, out_dtype=F32, dep=tok)
    tok = emit2("ple", dh4)
    dx3, dx3h, dg_ple = rms_bwd(dh4, x3, g_ple, dx4, "rms_ple_bwd", dep=tok)

    da2, db2 = ffn_bwd_act(dx3h, wd2, 0, 1, a2, b2, "ffn2_bwd_act")
    tok = dw_theirs("ffn2_w_down", s2, dx3h)
    tok = dw_theirs("ffn2_w_gate", da2, h3, dep=tok)
    tok = dw_theirs("ffn2_w_up", db2, h3, dep=tok)
    tok = dw_mine("ffn2_w_down", s2, dx3h, dep=tok)
    tok = dw_mine("ffn2_w_gate", da2, h3, dep=tok)
    tok = dw_mine("ffn2_w_up", db2, h3, dep=tok)
    dh3 = ffn_bwd_dh(da2, db2, wgu2, wgu2, 0, 1, 2, D, "ffn2_bwd_dh", dep=tok)
    dx2, _, dg_ffn2 = rms_bwd(dh3, x2, g_ffn2, dx3, "rms_ffn2_bwd", half=False)

    dcat = mm_nt([(dx2, wo)], "mix_out_dh", tn=512, out_dtype=BF16)
    grads["w_o"] = mm_tn_plain(cat, dx2, "mix_out_dw")
    dq, dk, dv, dct, dtiles = attention_bwd(qkv, c, ct, tiles, lse, cat, dcat, "attention_bwd")
    dctp = jnp.pad(dct[:, 0, :], ((0, LANES - N_HEADS), (0, 0)))
    duf, dbf = fox_gate_bwd(dctp, uf, bfp, "fox_gate_bwd")
    drel = rel_table_grad(dtiles, T, "rel_table_grad")[:, 0, :N_REL_BUCKETS].T
    du3 = jnp.concatenate([dq, dk, dv], axis=1)
    grads["w3"] = mm_tn_plain(du3, h2, "mix_qkv_dw", tm=768)
    grads["wf"] = mm_tn_plain(duf, h2, "mix_forget_dw", tm=LANES)
    tok = emit("mix", grads)
    dh2 = mm_nn_sum([(du3, w3), (duf, wf)], "mix_in_dh", tn=512, out_dtype=F32, dep=tok)
    tok = emit2("mix", dh2)
    dx1, dx1h, dg_mix = rms_bwd(dh2, x1, g_mix, dx2, "rms_mix_bwd", dep=tok)

    da1, db1 = ffn_bwd_act(dx1h, wd1, 0, 1, a1, b1, "ffn1_bwd_act")
    tok = dw_theirs("ffn1_w_down", s1, dx1h)
    tok = dw_theirs("ffn1_w_gate", da1, h1, dep=tok)
    tok = dw_mine("ffn1_w_down", s1, dx1h, dep=tok)
    tok = dw_theirs("ffn1_w_up", db1, h1, dep=tok)
    tok = dw_mine("ffn1_w_gate", da1, h1, dep=tok)
    tok = dw_mine("ffn1_w_up", db1, h1, dep=tok)
    dh1 = ffn_bwd_dh(da1, db1, wg1, wu1, 0, 0, 1, D, "ffn1_bwd_dh", dep=tok)
    dx0, _, dg_ffn1 = rms_bwd(dh1, x, g_ffn1, dx1, "rms_ffn1_bwd", half=False)

    small = _pack_small(D, dg_ffn1, dg_mix, dg_ffn2, dg_ple, dg_final, dbf[:, :N_FOX], drel)
    small = small.at[_ROW_LOSS, :LANES].set(loss_row[0])
    grads["small"] = small
    emit("small", grads)
    return dx0


def _split_w_in(w_in_t):
    df, dd = N_FOX * HEAD_DIM, N_DIL * HEAD_DIM
    o = np.cumsum([0, df, df, df, N_FOX, dd, dd, dd]).tolist()
    qa, ka, va, f, qb, kb, vb = [w_in_t[o[i]:o[i + 1]] for i in range(7)]
    return jnp.concatenate([qa, qb, ka, kb, va, vb], axis=0), f


def _join_w_in(d3, dfg):
    df, dd = N_FOX * HEAD_DIM, N_DIL * HEAD_DIM
    o = np.cumsum([0, df, dd, df, dd, df, dd]).tolist()
    qa, qb, ka, kb, va, vb = [d3[o[i]:o[i + 1]] for i in range(6)]
    return jnp.concatenate([qa, ka, va, dfg, qb, kb, vb], axis=0)


def rows_to_bf16(a3, name, dep=None):
    R, _, C = a3.shape
    tc = _tile(C, 512)

    def body(a_ref, *rest):
        rest[-1][...] = a_ref[...].astype(BF16)

    in_specs = [pl.BlockSpec((R, None, tc), lambda n: (0, 0, n))]
    args = [a3]
    if dep is not None:
        in_specs.append(_dep_spec(1))
        args.append(dep)
    return pl.pallas_call(
        body, name=name, grid=(C // tc,), in_specs=in_specs,
        out_specs=pl.BlockSpec((R, tc), lambda n: (0, n)),
        out_shape=jax.ShapeDtypeStruct((R, C), BF16),
        compiler_params=_params(("parallel",)),
    )(*args)


def adamw_rows3d(parts, sums, my_chip, w3, m3, v3, name):
    R, _, C = w3.shape
    S = parts.shape[0]
    tc = _tile(C, 512)

    def body(mc_ref, p_ref, o_ref, w_ref, m_ref, v_ref, g_ref, d_ref, nm_ref, nv_ref):
        g = _sum_partials(p_ref, o_ref, mc_ref[0])
        delta, nm, nv = _adamw_math(w_ref[...], g, m_ref[...], v_ref[...])
        g_ref[...] = g
        d_ref[...] = delta
        nm_ref[...] = nm
        nv_ref[...] = nv

    col = pl.BlockSpec((R, None, tc), lambda n, mc: (0, 0, n))
    shp = jax.ShapeDtypeStruct((R, 1, C), F32)
    grid_spec = pltpu.PrefetchScalarGridSpec(
        num_scalar_prefetch=1, grid=(C // tc,),
        in_specs=[pl.BlockSpec((S, R, tc), lambda n, mc: (0, 0, n)),
                  pl.BlockSpec((None, R, tc), lambda n, mc: (mc[0], 0, n)), col, col, col],
        out_specs=[col, col, col, col])
    return pl.pallas_call(
        body, name=name, grid_spec=grid_spec, out_shape=[shp, shp, shp, shp],
        compiler_params=_params(("parallel",)),
    )(my_chip.reshape(1).astype(jnp.int32), parts, sums, w3, m3, v3)


def kernel(x, p, norm_ffn1, ffn1_w_gate, ffn1_w_up, ffn1_w_down, norm_mix, w_in, b_f, w_o, norm_ffn2, ffn2_w_gate, ffn2_w_up, ffn2_w_down, norm_ple, w_ple_gate, w_ple_proj, rel_table, norm_final, loss_target, m_norm_ffn1, m_ffn1_w_gate, m_ffn1_w_up, m_ffn1_w_down, m_norm_mix, m_w_in, m_b_f, m_w_o, m_norm_ffn2, m_ffn2_w_gate, m_ffn2_w_up, m_ffn2_w_down, m_norm_ple, m_w_ple_gate, m_w_ple_proj, m_rel_table, m_norm_final, v_norm_ffn1, v_ffn1_w_gate, v_ffn1_w_up, v_ffn1_w_down, v_norm_mix, v_w_in, v_b_f, v_w_o, v_norm_ffn2, v_ffn2_w_gate, v_ffn2_w_up, v_ffn2_w_down, v_norm_ple, v_w_ple_gate, v_w_ple_proj, v_rel_table, v_norm_final):
    names = ["norm_ffn1", "ffn1_w_gate", "ffn1_w_up", "ffn1_w_down", "norm_mix", "w_in", "b_f", "w_o",
             "norm_ffn2", "ffn2_w_gate", "ffn2_w_up", "ffn2_w_down", "norm_ple", "w_ple_gate",
             "w_ple_proj", "rel_table", "norm_final"]
    w = dict(zip(names, [norm_ffn1, ffn1_w_gate, ffn1_w_up, ffn1_w_down, norm_mix, w_in, b_f, w_o,
                         norm_ffn2, ffn2_w_gate, ffn2_w_up, ffn2_w_down, norm_ple, w_ple_gate,
                         w_ple_proj, rel_table, norm_final]))
    m = dict(zip(names, [m_norm_ffn1, m_ffn1_w_gate, m_ffn1_w_up, m_ffn1_w_down, m_norm_mix, m_w_in,
                         m_b_f, m_w_o, m_norm_ffn2, m_ffn2_w_gate, m_ffn2_w_up, m_ffn2_w_down,
                         m_norm_ple, m_w_ple_gate, m_w_ple_proj, m_rel_table, m_norm_final]))
    v = dict(zip(names, [v_norm_ffn1, v_ffn1_w_gate, v_ffn1_w_up, v_ffn1_w_down, v_norm_mix, v_w_in,
                         v_b_f, v_w_o, v_norm_ffn2, v_ffn2_w_gate, v_ffn2_w_up, v_ffn2_w_down,
                         v_norm_ple, v_w_ple_gate, v_w_ple_proj, v_rel_table, v_norm_final]))
    sharded = ["ffn1_w_gate", "ffn1_w_up", "ffn1_w_down", "w_in", "w_o", "ffn2_w_gate", "ffn2_w_up",
               "ffn2_w_down", "w_ple_gate", "w_ple_proj"]
    small_names = [n for n in names if n not in sharded]

    xs, ps, tgt = x[0], p[0, 0], loss_target[0]
    T, D = xs.shape
    transposed = ("ffn1_w_gate", "ffn1_w_up", "ffn2_w_gate", "ffn2_w_up")

    def view(t, n):
        if n in transposed:
            return t[n][0].T
        if n == "w_in":
            return jnp.transpose(t[n], (2, 0, 1))
        return t[n][0]

    def unview(a, n):
        if n in transposed:
            return a.T.reshape(w[n].shape)
        if n == "w_in":
            return jnp.transpose(a, (1, 2, 0))
        return a.reshape(w[n].shape)

    sh = {n: view(w, n) for n in sharded}
    m_sh = {n: view(m, n) for n in sharded}
    v_sh = {n: view(v, n) for n in sharded}
    F8 = sh["ffn1_w_down"].shape[0]
    WIN8 = sh["w_in"].shape[0]
    me = 4 * lax.axis_index("x") + 2 * lax.axis_index("y") + lax.axis_index("c")

    def start(groups, name, after=None):
        srcs = [s for grp in groups for s in grp]
        sems, bufs, token = gather_start(srcs, [_landing(s, me) for s in srcs], name, after=after)
        return sems, bufs[:len(srcs)], bufs[len(srcs):], token

    cat0 = lambda ns, z: (jnp.concatenate([sh[n] for n in ns], axis=0) + z).astype(BF16)
    sems_a, srcs_a, lands_a, token_a = start(
        [[sh["ffn1_w_gate"].astype(BF16)], [sh["ffn1_w_up"].astype(BF16)], [sh["ffn1_w_down"].astype(BF16)]],
        "gather_start_ffn1")
    zero = token_a[0, 0]
    w_in_bf = rows_to_bf16(sh["w_in"], "w_in_bf16", dep=token_a)
    sems_b, srcs_b, lands_b, g_token = start(
        [[w_in_bf, (sh["w_o"] + zero).astype(BF16)],
         [cat0(["ffn2_w_gate", "ffn2_w_up"], zero), (sh["ffn2_w_down"] + zero).astype(BF16)],
         [(sh["w_ple_gate"] + zero).astype(BF16), (sh["w_ple_proj"] + zero).astype(BF16)]],
        "gather_start_rest", after=token_a)
    order = ["ffn1_g", "ffn1_u", "ffn1_d", "mix_in", "mix_out", "ffn2_gu", "ffn2_d", "ple"]
    group_sizes = [1, 1, 1, 1, 1, 1, 1, 2]
    g_sems, g_srcs, g_lands = sems_a + sems_b, srcs_a + srcs_b, lands_a + lands_b
    g_send, g_recv_d, g_recv_i = g_sems[0::3], g_sems[1::3], g_sems[2::3]
    first = np.cumsum([0] + group_sizes).tolist()
    passed = {}

    def arrays_of(group):
        k = order.index(group)
        return slice(first[k], first[k + 1])

    def forward(group, after):
        sl = arrays_of(group)
        f_sems, bufs = gather_forward(g_srcs[sl], g_lands[sl], g_recv_i[sl], after, "gather_forward_" + group)
        k = len(bufs) // 2
        passed[group] = (f_sems[0::2], f_sems[1::2], bufs[:k], bufs[k:])

    def weights(group, after):
        sl = arrays_of(group)
        f_send, f_recv, srcs, lands = passed[group]
        got = gather_wait(srcs, lands, g_send[sl], g_recv_d[sl], f_send, f_recv, after, "gather_wait_" + group)
        if group in ("ffn1_g", "ffn1_u", "ffn1_d"):
            return (got[0].reshape(N_DEV * F8, D),)
        if group == "ffn2_gu":
            return (got[0].reshape(N_DEV * 2 * F8, D),)
        if group in ("ffn2_d", "mix_out"):
            return (got[0].reshape(-1, D),)
        if group == "ple":
            return got[0].reshape(-1, D), got[1].reshape(-1, got[1].shape[2])
        w3, wf8 = _split_w_in(got[0].reshape(N_DEV * WIN8, D))
        return w3, jnp.pad(wf8, ((0, LANES - N_FOX), (0, 0)))

    ffn_names = ["ffn2_w_down", "ffn2_w_gate", "ffn2_w_up", "ffn1_w_down", "ffn1_w_gate", "ffn1_w_up"]
    scatter_groups = {"ple": ["w_ple_gate", "w_ple_proj"], "mix": ["w_in", "w_o"]}
    scatter_groups.update({n: [n] for n in ffn_names})
    x_i, y_i, c_i = _xyc()
    my_chip = 2 * x_i + y_i
    pair_stage, chip_stage, small_stage = {}, {}, {}

    def emit(group, grads, after=None):
        if group == "small":
            src = grads["small"]
            ss, rs, srcs, lands, token = exchange_start([src], [_landing(src, me)], [False], "scatter_start_small")
            small_stage["small"] = (ss, rs, srcs, lands)
            return token
        src4s = []
        for n in scatter_groups[group]:
            if n == "w_in":
                full = _join_w_in(grads["w3"], grads["wf"][:N_FOX])
                src4s.append(full.reshape(N_CHIPS, 2, WIN8, D))
            else:
                src4s.append(grads[n].reshape((N_CHIPS, 2) + sh[n].shape))
        lands = [lax.empty((N_CHIPS,) + s.shape[2:], BF16) for s in src4s]
        sems, bufs, token = scatter_pair_start(src4s, lands, "scatter_pair_start_" + group, after=after)
        k = len(src4s)
        pair_stage[group] = (sems, bufs[:k], bufs[k:])
        return token

    def emit2(group, after):
        sems, src4s, lands = pair_stage[group]
        src4s, lands = scatter_pair_wait(src4s, lands, sems, after, "scatter_pair_wait_" + group)
        sums = [chip_sum(s4, la, c_i, "chip_sum_" + n)
                for s4, la, n in zip(src4s, lands, scatter_groups[group])]
        chip_lands = [lax.empty(s.shape, s.dtype) for s in sums]
        sems, bufs, token = scatter_chip_start(sums, chip_lands, "scatter_chip_start_" + group)
        k = len(sums)
        chip_stage[group] = (sems, bufs[:k], bufs[k:])
        return token

    def dw_theirs(n, act, other, dep=None):
        theirs = ffn_dw_side(act, other, c_i, False, n + "_dw_theirs", dep=dep)
        theirs = theirs.reshape((N_CHIPS,) + sh[n].shape)
        sems, bufs, token = scatter_pair_start([theirs], [lax.empty(theirs.shape, BF16)],
                                               "scatter_pair_start_" + n, whole=True)
        pair_stage[n] = (sems, bufs[:1], bufs[1:])
        return token

    def dw_mine(n, act, other, dep=None):
        sems, theirs, lands = pair_stage[n]
        _, lands = scatter_pair_wait(theirs, lands, sems, dep, "scatter_pair_wait_" + n, whole=True)
        sums = ffn_dw_side(act, other, c_i, True, n + "_dw_mine", add=lands[0].reshape(-1, D))
        sums = sums.reshape((N_CHIPS,) + sh[n].shape)
        sems, bufs, token = scatter_chip_start([sums], [lax.empty(sums.shape, BF16)], "scatter_chip_start_" + n)
        chip_stage[n] = (sems, bufs[:1], bufs[1:])
        return token

    dx0 = local_step(
        xs, ps, tgt, w["norm_ffn1"], w["norm_mix"], w["norm_ffn2"], w["norm_ple"],
        w["norm_final"].reshape(1, D), w["b_f"], w["rel_table"], forward, weights, emit, emit2,
        dw_theirs, dw_mine, g_token)

    res = {}
    after = dx0
    for group in ["ple"] + ffn_names[:3] + ["mix"] + ffn_names[3:]:
        sems, sums, chip_lands = chip_stage[group]
        sums, parts = scatter_chip_wait(sums, chip_lands, sems, after, "scatter_chip_wait_" + group)
        for n, part, own in zip(scatter_groups[group], parts, sums):
            update = adamw_rows3d if n == "w_in" else adamw_sharded
            g, d, nm, nv = update(part, own, my_chip, sh[n], m_sh[n], v_sh[n], "adamw_" + n)
            res[n] = tuple(unview(a, n) for a in (g, d, nm, nv))
            after = g
    ss, rs, srcs, lands = small_stage["small"]
    small_parts, = exchange_wait(ss, rs, srcs, lands, [False], after, "scatter_wait_small")
    pack = lambda t: _pack_small(D, t["norm_ffn1"], t["norm_mix"], t["norm_ffn2"], t["norm_ple"],
                                 t["norm_final"], t["b_f"], t["rel_table"])
    gs, ds, ms, vs = adamw_small(small_parts, pack(w), pack(m), pack(v), "adamw_small")
    shapes = {n: w[n].shape for n in small_names}
    unpacked = [_unpack_small(a, shapes) for a in (gs, ds, ms, vs)]
    for n in small_names:
        res[n] = tuple(u[n] for u in unpacked)
    loss = gs[_ROW_LOSS, 0]

    out = [loss, dx0.reshape(x.shape)]
    for k in range(4):
        out += [res[n][k] for n in names]
    return tuple(out)
```

```python
import functools
import math

import numpy as np
import jax
import jax.numpy as jnp
from jax import lax
from jax.experimental import pallas as pl
from jax.experimental.pallas import tpu as pltpu

F32 = jnp.float32
BF16 = jnp.bfloat16

N_DEV = 8
HEAD_DIM = 128
N_FOX = 8
N_DIL = 8
N_HEADS = N_FOX + N_DIL
DILATED_PATTERNS = ((128, 1), (512, 4), (2048, 16))
N_REL_BUCKETS = 32
REL_MAX_DISTANCE = 2048
RMS_EPS = 1e-6
NEG_INF = -1e30
LANES = 128
VMEM_LIMIT = 56 * 1024 * 1024

ADAM_LR = 0.001
ADAM_B1 = 0.9
ADAM_B2 = 0.999
ADAM_EPS = 1e-08
ADAM_WD = 0.01
ADAM_STEP = 10

MESH = pl.DeviceIdType.MESH


def _params(sem):
    return pltpu.CompilerParams(dimension_semantics=sem, vmem_limit_bytes=VMEM_LIMIT)


def _dot(a, b, ca, cb, precision=None):
    return lax.dot_general(a, b, (((ca,), (cb,)), ((), ())),
                           preferred_element_type=F32, precision=precision)


def _sigmoid(z):
    return 1.0 / (1.0 + jnp.exp(-z))


def _tile(n, want):
    t = min(n, want)
    assert n % t == 0, (n, t)
    return t


def _dep_spec(ngrid):
    return pl.BlockSpec((8, LANES), lambda *_: (0, 0))


def rms_fwd(x, g, name, dep=None):
    T, D = x.shape
    tm = _tile(T, 256)

    def body(x_ref, g_ref, *rest):
        h_ref = rest[-1]
        xv = x_ref[...]
        r = lax.rsqrt(jnp.mean(xv * xv, axis=-1, keepdims=True) + RMS_EPS)
        h_ref[...] = (xv * r * g_ref[...]).astype(BF16)

    in_specs = [pl.BlockSpec((tm, D), lambda i: (i, 0)), pl.BlockSpec((1, D), lambda i: (0, 0))]
    args = [x, g]
    if dep is not None:
        in_specs.append(_dep_spec(1))
        args.append(dep)
    return pl.pallas_call(
        body, name=name, grid=(T // tm,), in_specs=in_specs,
        out_specs=pl.BlockSpec((tm, D), lambda i: (i, 0)),
        out_shape=jax.ShapeDtypeStruct((T, D), BF16),
        compiler_params=_params(("parallel",)),
    )(*args)


def rms_bwd(dh, x, g, dres, name, dep=None, half=True):
    T, D = x.shape
    tm = _tile(T, 256)

    def body(dh_ref, x_ref, g_ref, dres_ref, *rest):
        dx_ref, dg_ref = (rest[-3], rest[-1]) if half else (rest[-2], rest[-1])
        i = pl.program_id(0)
        xv = x_ref[...]
        r = lax.rsqrt(jnp.mean(xv * xv, axis=-1, keepdims=True) + RMS_EPS)
        xh = xv * r
        d = dh_ref[...]
        u = d * g_ref[...]
        dx = dres_ref[...] + r * (u - xh * jnp.mean(u * xh, axis=-1, keepdims=True))
        dx_ref[...] = dx
        if half:
            rest[-2][...] = (0.5 * dx).astype(BF16)
        part = jnp.sum(d * xh, axis=0, keepdims=True)

        @pl.when(i == 0)
        def _():
            dg_ref[...] = part

        @pl.when(i > 0)
        def _():
            dg_ref[...] += part

    row = pl.BlockSpec((tm, D), lambda i: (i, 0))
    vec = pl.BlockSpec((1, D), lambda i: (0, 0))
    in_specs = [row, row, vec, row]
    args = [dh, x, g, dres]
    if dep is not None:
        in_specs.append(_dep_spec(1))
        args.append(dep)
    out_specs = [row, row, vec] if half else [row, vec]
    out_shape = [jax.ShapeDtypeStruct((T, D), F32)] + ([jax.ShapeDtypeStruct((T, D), BF16)] if half else [])
    out_shape.append(jax.ShapeDtypeStruct((1, D), F32))
    outs = pl.pallas_call(
        body, name=name, grid=(T // tm,),
        in_specs=in_specs, out_specs=out_specs, out_shape=out_shape,
        compiler_params=_params(("arbitrary",)),
    )(*args)
    return tuple(outs) if half else (outs[0], None, outs[1])


def ple_loss(x, z, pp, g, target, name):
    T, D = x.shape
    tm = _tile(T, 256)

    def body(x_ref, z_ref, p_ref, g_ref, t_ref, loss_ref, dx_ref, dg_ref, dz_ref, dp_ref):
        i = pl.program_id(0)
        gate = _sigmoid(z_ref[...])
        ppv = p_ref[...]
        xv = x_ref[...] + gate * ppv
        gv = g_ref[...]
        r = lax.rsqrt(jnp.mean(xv * xv, axis=-1, keepdims=True) + RMS_EPS)
        xh = xv * r
        e = xh * gv - t_ref[...]
        lpart = 0.5 * jnp.sum(jnp.mean(e * e, axis=-1, keepdims=True), axis=0, keepdims=True)
        lrow = jnp.broadcast_to(lpart, (1, LANES))
        d = e * (1.0 / D)
        u = d * gv
        dx = r * (u - xh * jnp.mean(u * xh, axis=-1, keepdims=True))
        dx_ref[...] = dx
        dz_ref[...] = (dx * ppv * gate * (1.0 - gate)).astype(BF16)
        dp_ref[...] = (dx * gate).astype(BF16)
        part = jnp.sum(d * xh, axis=0, keepdims=True)

        @pl.when(i == 0)
        def _():
            dg_ref[...] = part
            loss_ref[...] = lrow

        @pl.when(i > 0)
        def _():
            dg_ref[...] += part
            loss_ref[...] += lrow

    row = pl.BlockSpec((tm, D), lambda i: (i, 0))
    vec = pl.BlockSpec((1, D), lambda i: (0, 0))
    return pl.pallas_call(
        body, name=name, grid=(T // tm,),
        in_specs=[row, row, row, vec, row],
        out_specs=[pl.BlockSpec((1, LANES), lambda i: (0, 0)), row, vec, row, row],
        out_shape=[jax.ShapeDtypeStruct((1, LANES), F32), jax.ShapeDtypeStruct((T, D), F32),
                   jax.ShapeDtypeStruct((1, D), F32), jax.ShapeDtypeStruct((T, D), BF16),
                   jax.ShapeDtypeStruct((T, D), BF16)],
        compiler_params=_params(("arbitrary",)),
    )(x, z, pp, g, target)


def _bf(v, scale=None):
    if scale is not None:
        v = v * scale
    return v.astype(BF16)


def mm_nn(a, b, name, *, tn, out_dtype, tm=512, n_out=None, b_block=None, b_map=None,
          res=None):
    T, K = a.shape
    N = n_out if n_out is not None else b.shape[1]
    tm = _tile(T, tm)
    tn = _tile(N, tn)
    b_block = b_block or (K, tn)
    b_map = b_map or (lambda n, i: (0, n))

    def body(*refs):
        a_ref, b_ref = refs[0], refs[1]
        o_ref = refs[-1]
        acc = _dot(_bf(a_ref[...]), _bf(b_ref[...]), 1, 0)
        if res is not None:
            acc = refs[2][...] + acc
        o_ref[...] = acc.astype(out_dtype)

    in_specs = [pl.BlockSpec((tm, K), lambda n, i: (i, 0)), pl.BlockSpec(b_block, b_map)]
    args = [a, b]
    if res is not None:
        in_specs.append(pl.BlockSpec((tm, tn), lambda n, i: (i, n)))
        args.append(res)
    return pl.pallas_call(
        body, name=name, grid=(N // tn, T // tm), in_specs=in_specs,
        out_specs=pl.BlockSpec((tm, tn), lambda n, i: (i, n)),
        out_shape=jax.ShapeDtypeStruct((T, N), out_dtype),
        compiler_params=_params(("parallel", "parallel")),
    )(*args)


def mm_nn_sum(pairs, name, *, tn, out_dtype, tm=512, dep=None):
    T = pairs[0][0].shape[0]
    N = pairs[0][1].shape[1]
    tm = _tile(T, tm)
    tn = _tile(N, tn)
    npair = len(pairs)

    def body(*refs):
        acc = None
        for q in range(npair):
            part = _dot(_bf(refs[2 * q][...]), _bf(refs[2 * q + 1][...]), 1, 0)
            acc = part if acc is None else acc + part
        refs[-1][...] = acc.astype(out_dtype)

    in_specs, args = [], []
    for a, b in pairs:
        K = a.shape[1]
        in_specs += [pl.BlockSpec((tm, K), lambda n, i: (i, 0)), pl.BlockSpec((K, tn), lambda n, i: (0, n))]
        args += [a, b]
    if dep is not None:
        in_specs.append(_dep_spec(2))
        args.append(dep)
    return pl.pallas_call(
        body, name=name, grid=(N // tn, T // tm), in_specs=in_specs,
        out_specs=pl.BlockSpec((tm, tn), lambda n, i: (i, n)),
        out_shape=jax.ShapeDtypeStruct((T, N), out_dtype),
        compiler_params=_params(("parallel", "parallel")),
    )(*args)


def mm_nt(pairs, name, *, tn, out_dtype, tm=512, dep=None):
    T = pairs[0][0].shape[0]
    N = pairs[0][1].shape[0]
    tm = _tile(T, tm)
    tn = _tile(N, tn)
    npair = len(pairs)

    def body(*refs):
        o_ref = refs[-1]
        acc = None
        for q in range(npair):
            part = _dot(_bf(refs[2 * q][...]), _bf(refs[2 * q + 1][...]), 1, 1)
            acc = part if acc is None else acc + part
        o_ref[...] = acc.astype(out_dtype)

    in_specs, args = [], []
    for a, b in pairs:
        K = a.shape[1]
        in_specs += [pl.BlockSpec((tm, K), lambda n, i: (i, 0)), pl.BlockSpec((tn, K), lambda n, i: (n, 0))]
        args += [a, b]
    if dep is not None:
        in_specs.append(_dep_spec(2))
        args.append(dep)
    return pl.pallas_call(
        body, name=name, grid=(N // tn, T // tm), in_specs=in_specs,
        out_specs=pl.BlockSpec((tm, tn), lambda n, i: (i, n)),
        out_shape=jax.ShapeDtypeStruct((T, N), out_dtype),
        compiler_params=_params(("parallel", "parallel")),
    )(*args)


def mm_tn(a, b, name, *, grid, a_block, a_map, b_block, b_map, o_block, o_map, out_shape,
          b_scale=None, dep=None):
    def body(a_ref, b_ref, *rest):
        rest[-1][...] = _dot(_bf(a_ref[...]), _bf(b_ref[...], b_scale), 0, 0).astype(BF16)

    in_specs = [pl.BlockSpec(a_block, a_map), pl.BlockSpec(b_block, b_map)]
    args = [a, b]
    if dep is not None:
        in_specs.append(_dep_spec(len(grid)))
        args.append(dep)
    return pl.pallas_call(
        body, name=name, grid=grid, in_specs=in_specs,
        out_specs=pl.BlockSpec(o_block, o_map),
        out_shape=jax.ShapeDtypeStruct(out_shape, BF16),
        compiler_params=_params(("parallel",) * len(grid)),
    )(*args)


def mm_tn_plain(a, b, name, *, tm=512, tn=1024, b_scale=None):
    T, M = a.shape
    N = b.shape[1]
    tm = _tile(M, tm)
    tn = _tile(N, tn)
    return mm_tn(a, b, name, grid=(M // tm, N // tn),
                 a_block=(T, tm), a_map=lambda m, n: (0, m),
                 b_block=(T, tn), b_map=lambda m, n: (0, n),
                 o_block=(tm, tn), o_map=lambda m, n: (m, n),
                 out_shape=(M, N), b_scale=b_scale)


def ffn_up(h, wgu, gi, ui, nper, name):
    T, D = h.shape
    F8 = wgu.shape[0] // (N_DEV * nper)
    tm = _tile(T, 512)
    nt = T // tm

    def body(h_ref, wg_ref, wu_ref, ga_ref, gb_ref, s_ref):
        hv = h_ref[...]
        a = _dot(hv, wg_ref[...], 1, 1)
        b = _dot(hv, wu_ref[...], 1, 1)
        sg = _sigmoid(a)
        silu = a * sg
        ga_ref[...] = (b * (sg * (1.0 + a * (1.0 - sg)))).astype(BF16)
        gb_ref[...] = silu.astype(BF16)
        s_ref[...] = (silu * b).astype(BF16)

    blk = pl.BlockSpec((tm, F8), lambda j, i: (j * nt + i, 0))
    shp = jax.ShapeDtypeStruct((N_DEV * T, F8), BF16)
    return pl.pallas_call(
        body, name=name, grid=(N_DEV, nt),
        in_specs=[pl.BlockSpec((tm, D), lambda j, i: (i, 0)),
                  pl.BlockSpec((F8, D), lambda j, i: (j * nper + gi, 0)),
                  pl.BlockSpec((F8, D), lambda j, i: (j * nper + ui, 0))],
        out_specs=[blk, blk, blk], out_shape=[shp, shp, shp],
        compiler_params=_params(("parallel", "parallel")),
    )(h, wgu, wgu)


def ffn_gate(h, wg, name):
    T, D = h.shape
    F8 = wg.shape[0] // N_DEV
    tm = _tile(T, 512)
    nt = T // tm

    def body(h_ref, wg_ref, a_ref):
        a_ref[...] = _dot(h_ref[...], wg_ref[...], 1, 1).astype(BF16)

    return pl.pallas_call(
        body, name=name, grid=(N_DEV, nt),
        in_specs=[pl.BlockSpec((tm, D), lambda j, i: (i, 0)), pl.BlockSpec((F8, D), lambda j, i: (j, 0))],
        out_specs=pl.BlockSpec((tm, F8), lambda j, i: (j * nt + i, 0)),
        out_shape=jax.ShapeDtypeStruct((N_DEV * T, F8), BF16),
        compiler_params=_params(("parallel", "parallel")),
    )(h, wg)


def ffn_up_gated(h, wu, a, name):
    T, D = h.shape
    F8 = wu.shape[0] // N_DEV
    tm = _tile(T, 512)
    nt = T // tm

    def body(h_ref, wu_ref, a_ref, ga_ref, gb_ref, s_ref):
        av = a_ref[...].astype(F32)
        b = _dot(h_ref[...], wu_ref[...], 1, 1)
        sg = _sigmoid(av)
        silu = av * sg
        ga_ref[...] = (b * (sg * (1.0 + av * (1.0 - sg)))).astype(BF16)
        gb_ref[...] = silu.astype(BF16)
        s_ref[...] = (silu * b).astype(BF16)

    blk = pl.BlockSpec((tm, F8), lambda j, i: (j * nt + i, 0))
    shp = jax.ShapeDtypeStruct((N_DEV * T, F8), BF16)
    return pl.pallas_call(
        body, name=name, grid=(N_DEV, nt),
        in_specs=[pl.BlockSpec((tm, D), lambda j, i: (i, 0)), pl.BlockSpec((F8, D), lambda j, i: (j, 0)), blk],
        out_specs=[blk, blk, blk], out_shape=[shp, shp, shp],
        compiler_params=_params(("parallel", "parallel")),
    )(h, wu, a)


def ffn_down(s, wd, di, nper, x, g_next, name):
    T, D = x.shape
    F8 = s.shape[1]
    tm = _tile(T, 512)
    nt = T // tm

    def body(s_ref, w_ref, x_ref, g_ref, o_ref, h_ref, acc_ref):
        j = pl.program_id(1)
        part = _dot(s_ref[...], w_ref[...], 1, 0)

        @pl.when(j == 0)
        def _():
            acc_ref[...] = part

        @pl.when(j > 0)
        def _():
            acc_ref[...] += part

        @pl.when(j == N_DEV - 1)
        def _():
            xv = x_ref[...] + 0.5 * acc_ref[...]
            o_ref[...] = xv
            r = lax.rsqrt(jnp.mean(xv * xv, axis=-1, keepdims=True) + RMS_EPS)
            h_ref[...] = (xv * r * g_ref[...]).astype(BF16)

    row = pl.BlockSpec((tm, D), lambda i, j: (i, 0))
    return pl.pallas_call(
        body, name=name, grid=(nt, N_DEV),
        in_specs=[pl.BlockSpec((tm, F8), lambda i, j: (j * nt + i, 0)),
                  pl.BlockSpec((F8, D), lambda i, j: (j * nper + di, 0)),
                  row, pl.BlockSpec((1, D), lambda i, j: (0, 0))],
        out_specs=[row, row],
        out_shape=[jax.ShapeDtypeStruct((T, D), F32), jax.ShapeDtypeStruct((T, D), BF16)],
        scratch_shapes=[pltpu.VMEM((tm, D), F32)],
        compiler_params=_params(("parallel", "arbitrary")),
    )(s, wd, x, g_next)


def ffn_bwd_act(dxh, wd, di, nper_d, a, b, name, dep=None):
    T, D = dxh.shape
    F8 = a.shape[1]
    tm = _tile(T, 1024)
    nt = T // tm

    def body(dx_ref, w_ref, a_ref, b_ref, *rest):
        da_ref, db_ref = rest[-2], rest[-1]
        ds = _dot(dx_ref[...], w_ref[...], 1, 1)
        da_ref[...] = (ds * a_ref[...].astype(F32)).astype(BF16)
        db_ref[...] = (ds * b_ref[...].astype(F32)).astype(BF16)

    blk = pl.BlockSpec((tm, F8), lambda j, i: (j * nt + i, 0))
    shp = jax.ShapeDtypeStruct((N_DEV * T, F8), BF16)
    in_specs = [pl.BlockSpec((tm, D), lambda j, i: (i, 0)),
                pl.BlockSpec((F8, D), lambda j, i: (j * nper_d + di, 0)), blk, blk]
    args = [dxh, wd, a, b]
    if dep is not None:
        in_specs.append(_dep_spec(2))
        args.append(dep)
    return pl.pallas_call(
        body, name=name, grid=(N_DEV, nt), in_specs=in_specs,
        out_specs=[blk, blk], out_shape=[shp, shp],
        compiler_params=_params(("parallel", "parallel")),
    )(*args)


def ffn_bwd_dh(da, db, wg, wu, gi, ui, nper, D, name, dep=None):
    F8 = da.shape[1]
    T = da.shape[0] // N_DEV
    tm = _tile(T, 512)
    nt = T // tm

    def body(da_ref, db_ref, wg_ref, wu_ref, *rest):
        o_ref, acc_ref = rest[-2], rest[-1]
        j = pl.program_id(1)
        part = _dot(da_ref[...], wg_ref[...], 1, 0) + _dot(db_ref[...], wu_ref[...], 1, 0)

        @pl.when(j == 0)
        def _():
            acc_ref[...] = part

        @pl.when(j > 0)
        def _():
            acc_ref[...] += part

        @pl.when(j == N_DEV - 1)
        def _():
            o_ref[...] = acc_ref[...]

    blk = pl.BlockSpec((tm, F8), lambda i, j: (j * nt + i, 0))
    in_specs = [blk, blk,
                pl.BlockSpec((F8, D), lambda i, j: (j * nper + gi, 0)),
                pl.BlockSpec((F8, D), lambda i, j: (j * nper + ui, 0))]
    args = [da, db, wg, wu]
    if dep is not None:
        in_specs.append(_dep_spec(2))
        args.append(dep)
    return pl.pallas_call(
        body, name=name, grid=(nt, N_DEV), in_specs=in_specs,
        out_specs=pl.BlockSpec((tm, D), lambda i, j: (i, 0)),
        out_shape=jax.ShapeDtypeStruct((T, D), F32),
        scratch_shapes=[pltpu.VMEM((tm, D), F32)],
        compiler_params=_params(("parallel", "arbitrary")),
    )(*args)


def ffn_dw_side(act, other, c, mine, name, add=None, dep=None):
    F8 = act.shape[1]
    T, D = other.shape
    tm = _tile(D, 1024)

    def body(c_ref, a_ref, b_ref, *rest):
        acc = _dot(a_ref[...], b_ref[...], 0, 0)
        if add is not None:
            acc = acc + rest[0][...].astype(F32)
        rest[-1][...] = acc.astype(BF16)

    def shard(q, cr):
        return 2 * q + (cr[0] if mine else 1 - cr[0])

    in_specs = [pl.BlockSpec((T, F8), lambda q, m, cr: (shard(q, cr), 0)),
                pl.BlockSpec((T, tm), lambda q, m, cr: (0, m))]
    args = [act, other]
    if add is not None:
        in_specs.append(pl.BlockSpec((F8, tm), lambda q, m, cr: (q, m)))
        args.append(add)
    if dep is not None:
        in_specs.append(pl.BlockSpec((8, LANES), lambda q, m, cr: (0, 0)))
        args.append(dep)
    grid_spec = pltpu.PrefetchScalarGridSpec(
        num_scalar_prefetch=1, grid=(N_DEV // 2, D // tm), in_specs=in_specs,
        out_specs=pl.BlockSpec((F8, tm), lambda q, m, cr: (q, m)))
    return pl.pallas_call(
        body, name=name, grid_spec=grid_spec,
        out_shape=jax.ShapeDtypeStruct((N_DEV // 2 * F8, D), BF16),
        compiler_params=_params(("parallel", "parallel")),
    )(c.reshape(1).astype(jnp.int32), *args)


def _t5_bucket_np(dist):
    max_exact = N_REL_BUCKETS // 2
    d = np.maximum(dist, 1).astype(np.float64)
    large = max_exact + (np.log(d / max_exact) / math.log(REL_MAX_DISTANCE / max_exact)
                         * (N_REL_BUCKETS - max_exact)).astype(np.int64)
    large32 = max_exact + (np.log(d.astype(np.float32) / np.float32(max_exact))
                           / np.float32(math.log(REL_MAX_DISTANCE / max_exact))
                           * np.float32(N_REL_BUCKETS - max_exact)).astype(np.int64)
    assert np.array_equal(large, large32)
    large = np.minimum(large, N_REL_BUCKETS - 1)
    return np.where(dist < max_exact, dist, large)


def _distance_tables(T, tq):
    dist = np.arange(T)
    mult = np.zeros(T, np.int64)
    for window, dilation in DILATED_PATTERNS:
        mult += ((dist % dilation == 0) & (dist // dilation <= window // dilation)).astype(np.int64)
    logm = np.where(mult > 0, np.log(np.maximum(mult, 1)), NEG_INF).astype(np.float32)
    bucket = _t5_bucket_np(dist).astype(np.int32)
    nkb = T // tq
    k = np.arange(nkb)[:, None, None]
    r = np.arange(tq)[None, :, None]
    c = np.arange(tq)[None, None, :]
    delta = k * tq + r - c
    return bucket, logm, delta


def _tile_buckets(T, tq):
    bucket, logm, delta = _distance_tables(T, tq)
    safe = np.maximum(delta, 0)
    bidx = np.where(delta >= 0, bucket[safe], -1).astype(np.int32)
    logm_t = np.where(delta >= 0, logm[safe], NEG_INF).astype(np.float32)
    present = [sorted(set(np.unique(bidx[k]).tolist()) - {-1}) for k in range(T // tq)]
    return bidx, logm_t, present


def bias_tiles(rel_table, T, tq):
    bidx, logm_t, present = _tile_buckets(T, tq)
    nkb = T // tq

    def body(tab_ref, b_ref, lm_ref, o_ref):
        slot = pl.program_id(0)

        @pl.when(slot == 0)
        def _():
            o_ref[...] = jnp.where(b_ref[...] >= 0, 0.0, NEG_INF)

        @pl.when(slot > 0)
        def _():
            for k in range(nkb):
                bi = b_ref[k]
                acc = lm_ref[k]
                for b in present[k]:
                    acc = acc + jnp.where(bi == b, tab_ref[b, slot - 1], 0.0)
                o_ref[k] = acc

    full = pl.BlockSpec((nkb, tq, tq), lambda s: (0, 0, 0))
    return pl.pallas_call(
        body, name="bias_tiles", grid=(1 + N_DIL,),
        in_specs=[pl.BlockSpec(memory_space=pltpu.SMEM), full, full],
        out_specs=pl.BlockSpec((None, nkb, tq, tq), lambda s: (s, 0, 0, 0)),
        out_shape=jax.ShapeDtypeStruct((1 + N_DIL, nkb, tq, tq), F32),
        compiler_params=_params(("parallel",)),
    )(rel_table, jnp.asarray(bidx), jnp.asarray(logm_t))


def fox_gate_fwd(uf, bf, name):
    T = uf.shape[0]
    tb = _tile(T, 512)

    def body(u_ref, b_ref, c_ref, ct_ref):
        lane = lax.broadcasted_iota(jnp.int32, (1, LANES), 1)
        tri = (lax.broadcasted_iota(jnp.int32, (tb, tb), 0)
               >= lax.broadcasted_iota(jnp.int32, (tb, tb), 1)).astype(F32)
        carry = jnp.zeros((1, LANES), F32)
        for blk in range(T // tb):
            z = u_ref[pl.ds(blk * tb, tb), :] + b_ref[...]
            lf = jnp.minimum(z, 0.0) - jnp.log1p(jnp.exp(-jnp.abs(z)))
            lf = jnp.where(lane < N_FOX, lf, 0.0)
            cb = _dot(tri, lf, 1, 0, precision=lax.Precision.HIGHEST) + carry
            c_ref[pl.ds(blk * tb, tb), :] = cb
            ct_ref[:, pl.ds(blk * tb, tb)] = cb.T
            carry = cb[tb - 1:tb, :]

    return pl.pallas_call(
        body, name=name,
        out_shape=[jax.ShapeDtypeStruct((T, LANES), F32), jax.ShapeDtypeStruct((LANES, T), F32)],
        compiler_params=_params(None),
    )(uf, bf)


def fox_gate_bwd(dct, uf, bf, name):
    T = uf.shape[0]
    tb = _tile(T, 512)

    def body(d_ref, u_ref, b_ref, du_ref, db_ref):
        lane = lax.broadcasted_iota(jnp.int32, (1, LANES), 1)
        triu = (lax.broadcasted_iota(jnp.int32, (tb, tb), 0)
                <= lax.broadcasted_iota(jnp.int32, (tb, tb), 1)).astype(F32)
        carry = jnp.zeros((1, LANES), F32)
        dbv = jnp.zeros((1, LANES), F32)
        for blk in reversed(range(T // tb)):
            dc = d_ref[:, pl.ds(blk * tb, tb)].T
            dlf = _dot(triu, dc, 1, 0, precision=lax.Precision.HIGHEST) + carry
            carry = dlf[0:1, :]
            z = u_ref[pl.ds(blk * tb, tb), :] + b_ref[...]
            dz = jnp.where(lane < N_FOX, dlf * (1.0 - _sigmoid(z)), 0.0)
            du_ref[pl.ds(blk * tb, tb), :] = dz
            dbv = dbv + jnp.sum(dz, axis=0, keepdims=True)
        db_ref[...] = dbv

    return pl.pallas_call(
        body, name=name,
        out_shape=[jax.ShapeDtypeStruct((T, LANES), F32), jax.ShapeDtypeStruct((1, LANES), F32)],
        compiler_params=_params(None),
    )(dct, uf, bf)


def _bias_slot(h):
    return jnp.maximum(h - (N_FOX - 1), 0)


def _scores(q_ref, k_ref, c_ref, ct_ref, tb_ref, h, i, tq, fox):
    scale = HEAD_DIM ** -0.5
    n = (i + 1) * tq
    rows = pl.ds(i * tq, tq)
    s = _dot(q_ref[rows, :], k_ref[pl.ds(0, n), :], 1, 1) * scale
    if not fox:
        return s + jnp.concatenate([tb_ref[i - jb] for jb in range(i + 1)], axis=1)
    lane = lax.broadcasted_iota(jnp.int32, (1, LANES), 1)
    c_col = jnp.sum(jnp.where(lane == h, c_ref[rows, :], 0.0), axis=1, keepdims=True)
    c_row = ct_ref[pl.ds(h, 1), pl.ds(0, n)]
    s = s + (c_col - c_row)
    if i == 0:
        return s + tb_ref[0]
    return jnp.concatenate([s[:, :i * tq], s[:, i * tq:] + tb_ref[0]], axis=1)


def _attn_specs(T, tq):
    nkb = T // tq
    return [
        pl.BlockSpec((T, HEAD_DIM), lambda h: (0, h)),
        pl.BlockSpec((T, HEAD_DIM), lambda h: (0, N_HEADS + h)),
        pl.BlockSpec((T, HEAD_DIM), lambda h: (0, 2 * N_HEADS + h)),
        pl.BlockSpec((T, LANES), lambda h: (0, 0)),
        pl.BlockSpec((LANES, T), lambda h: (0, 0)),
        pl.BlockSpec((None, nkb, tq, tq), lambda h: (_bias_slot(h), 0, 0, 0)),
    ]


def attention_fwd(qkv, c, ct, tiles, name):
    T = qkv.shape[0]
    tq = tiles.shape[2]

    def body(q_ref, k_ref, v_ref, c_ref, ct_ref, tb_ref, o_ref, lse_ref):
        h = pl.program_id(0)
        lane = lax.broadcasted_iota(jnp.int32, (1, LANES), 1)

        @pl.when(h == 0)
        def _():
            lse_ref[...] = jnp.zeros_like(lse_ref)

        def head(fox):
            for i in range(T // tq):
                rows = pl.ds(i * tq, tq)
                s = _scores(q_ref, k_ref, c_ref, ct_ref, tb_ref, h, i, tq, fox)
                m = jnp.max(s, axis=1, keepdims=True)
                p = jnp.exp(s - m)
                l = jnp.sum(p, axis=1, keepdims=True)
                o = _dot(p.astype(BF16), v_ref[pl.ds(0, (i + 1) * tq), :], 1, 0) * (1.0 / l)
                o_ref[rows, :] = o.astype(BF16)
                lse_ref[rows, :] = jnp.where(lane == h, m + jnp.log(l), lse_ref[rows, :])

        pl.when(h < N_FOX)(functools.partial(head, True))
        pl.when(h >= N_FOX)(functools.partial(head, False))

    return pl.pallas_call(
        body, name=name, grid=(N_HEADS,),
        in_specs=_attn_specs(T, tq),
        out_specs=[pl.BlockSpec((T, HEAD_DIM), lambda h: (0, h)), pl.BlockSpec((T, LANES), lambda h: (0, 0))],
        out_shape=[jax.ShapeDtypeStruct((T, N_HEADS * HEAD_DIM), BF16), jax.ShapeDtypeStruct((T, LANES), F32)],
        compiler_params=_params(("arbitrary",)),
    )(qkv, qkv, qkv, c, ct, tiles)


def attention_bwd(qkv, c, ct, tiles, lse, o, do, name):
    T = qkv.shape[0]
    tq = tiles.shape[2]
    nkb = T // tq
    scale = HEAD_DIM ** -0.5

    def body(q_ref, k_ref, v_ref, c_ref, ct_ref, tb_ref, lse_ref, o_ref, do_ref,
             dq_ref, dk_ref, dv_ref, dct_ref, dtb_ref, dk_acc, dv_acc):
        h = pl.program_id(0)
        lane = lax.broadcasted_iota(jnp.int32, (1, LANES), 1)
        dk_acc[...] = jnp.zeros_like(dk_acc)
        dv_acc[...] = jnp.zeros_like(dv_acc)
        dct_ref[...] = jnp.zeros_like(dct_ref)
        dtb_ref[...] = jnp.zeros_like(dtb_ref)

        def head(fox):
            for i in range(nkb):
                rows, keys = pl.ds(i * tq, tq), pl.ds(0, (i + 1) * tq)
                s = _scores(q_ref, k_ref, c_ref, ct_ref, tb_ref, h, i, tq, fox)
                lse_col = jnp.sum(jnp.where(lane == h, lse_ref[rows, :], 0.0), axis=1, keepdims=True)
                p = jnp.exp(s - lse_col)
                p_b = p.astype(BF16)
                dov = do_ref[rows, :]
                dp = _dot(dov, v_ref[keys, :], 1, 1)
                if fox:
                    delta = jnp.sum(p * dp, axis=1, keepdims=True)
                else:
                    delta = jnp.sum(dov.astype(F32) * o_ref[rows, :].astype(F32), axis=1, keepdims=True)
                ds = p * (dp - delta)
                ds_b = ds.astype(BF16)
                dq_ref[rows, :] = (_dot(ds_b, k_ref[keys, :], 1, 0) * scale).astype(BF16)
                dk_acc[:, keys] += _dot(q_ref[rows, :], ds_b, 0, 0) * scale
                dv_acc[:, keys] += _dot(dov, p_b, 0, 0)
                if fox:
                    dct_ref[:, keys] += -jnp.sum(ds, axis=0, keepdims=True)
                else:
                    for jb in range(i + 1):
                        dtb_ref[i - jb] += ds[:, jb * tq:(jb + 1) * tq]

        pl.when(h < N_FOX)(functools.partial(head, True))
        pl.when(h >= N_FOX)(functools.partial(head, False))
        dk_ref[...] = dk_acc[...].T.astype(BF16)
        dv_ref[...] = dv_acc[...].T.astype(BF16)

    head_cols = jax.ShapeDtypeStruct((T, N_HEADS * HEAD_DIM), BF16)
    col = pl.BlockSpec((T, HEAD_DIM), lambda h: (0, h))
    return pl.pallas_call(
        body, name=name, grid=(N_HEADS,),
        in_specs=_attn_specs(T, tq) + [pl.BlockSpec((T, LANES), lambda h: (0, 0)), col, col],
        out_specs=[col, col, col,
                   pl.BlockSpec((None, 1, T), lambda h: (h, 0, 0)),
                   pl.BlockSpec((None, nkb, tq, tq), lambda h: (_bias_slot(h), 0, 0, 0))],
        out_shape=[head_cols, head_cols, head_cols,
                   jax.ShapeDtypeStruct((N_HEADS, 1, T), F32),
                   jax.ShapeDtypeStruct((1 + N_DIL, nkb, tq, tq), F32)],
        scratch_shapes=[pltpu.VMEM((HEAD_DIM, T), F32), pltpu.VMEM((HEAD_DIM, T), F32)],
        compiler_params=_params(("arbitrary",)),
    )(qkv, qkv, qkv, c, ct, tiles, lse, o, do)


def rel_table_grad(dtiles, T, name):
    tq = dtiles.shape[2]
    nkb = T // tq
    bidx, _, present = _tile_buckets(T, tq)

    def body(d_ref, b_ref, o_ref):
        lane = lax.broadcasted_iota(jnp.int32, (1, LANES), 1)
        row = jnp.zeros((1, LANES), F32)
        for k in range(nkb):
            d = d_ref[k]
            bi = b_ref[k]
            for b in present[k]:
                v = jnp.sum(jnp.sum(jnp.where(bi == b, d, 0.0), axis=0, keepdims=True),
                            axis=1, keepdims=True)
                row = row + jnp.where(lane == b, v, 0.0)
        o_ref[...] = row

    return pl.pallas_call(
        body, name=name, grid=(N_DIL,),
        in_specs=[pl.BlockSpec((None, nkb, tq, tq), lambda h: (h + 1, 0, 0, 0)),
                  pl.BlockSpec((nkb, tq, tq), lambda h: (0, 0, 0))],
        out_specs=pl.BlockSpec((None, 1, LANES), lambda h: (h, 0, 0)),
        out_shape=jax.ShapeDtypeStruct((N_DIL, 1, LANES), F32),
        compiler_params=_params(("parallel",)),
    )(dtiles, jnp.asarray(bidx))


def _peer_list():
    x, y, c = lax.axis_index("x"), lax.axis_index("y"), lax.axis_index("c")
    me = 4 * x + 2 * y + c
    peers = []
    for fx in (0, 1):
        for fy in (0, 1):
            for fc in (0, 1):
                if fx or fy or fc:
                    px = 1 - x if fx else x
                    py = 1 - y if fy else y
                    pc = 1 - c if fc else c
                    peers.append(((px, py, pc), 4 * px + 2 * py + pc))
    return me, peers


_HBM = pl.BlockSpec(memory_space=pltpu.HBM)
_SEM = pl.BlockSpec(memory_space=pltpu.SEMAPHORE)
_EFFECT = pltpu.SideEffectType.DATAFLOW_SIDE_EFFECTING
N_PEERS = N_DEV - 1


def _in_hbm(a):
    return pltpu.with_memory_space_constraint(a, pltpu.HBM)


def _exchange_copies(srcs, lands, send_sems, recv_sems, blockwise):
    me, peers = _peer_list()
    sends, recvs = [], []
    for a in range(len(srcs)):
        for k, (dev, idx) in enumerate(peers):
            src = srcs[a].at[idx] if blockwise[a] else srcs[a]
            sends.append(pltpu.make_async_remote_copy(
                src_ref=src, dst_ref=lands[a].at[me], send_sem=send_sems[a].at[k],
                recv_sem=recv_sems[a].at[k], device_id=dev, device_id_type=MESH))
            recvs.append(pltpu.make_async_remote_copy(
                src_ref=src, dst_ref=lands[a].at[idx], send_sem=send_sems[a].at[k],
                recv_sem=recv_sems[a].at[k], device_id=dev, device_id_type=MESH))
    return sends, recvs


def exchange_start(srcs, lands, blockwise, name):
    n = len(srcs)

    def body(*refs):
        src_in, land_in = refs[:n], refs[n:2 * n]
        send_sems, recv_sems = refs[2 * n:3 * n], refs[3 * n:4 * n]
        token = refs[6 * n]
        sends, _ = _exchange_copies(src_in, land_in, send_sems, recv_sems, blockwise)
        for cp in sends:
            cp.start()
        token[...] = jnp.zeros_like(token)

    out_shape = ([pltpu.SemaphoreType.DMA((N_PEERS,))] * (2 * n)
                 + [pltpu.HBM(s.shape, s.dtype) for s in srcs]
                 + [pltpu.HBM(l.shape, l.dtype) for l in lands]
                 + [jax.ShapeDtypeStruct((8, LANES), F32)])
    aliases = {a: 2 * n + a for a in range(2 * n)}
    outs = pl.pallas_call(
        body, name=name, out_shape=out_shape,
        in_specs=[_HBM] * (2 * n),
        out_specs=[_SEM] * (2 * n) + [_HBM] * (2 * n) + [pl.BlockSpec(memory_space=pltpu.VMEM)],
        input_output_aliases=aliases,
        compiler_params=pltpu.CompilerParams(has_side_effects=_EFFECT),
    )(*[_in_hbm(s) for s in srcs], *[_in_hbm(l) for l in lands])
    return (outs[:n], outs[n:2 * n], outs[2 * n:3 * n], outs[3 * n:4 * n], outs[4 * n])


def exchange_wait(send_sems, recv_sems, srcs, lands, blockwise, after, name):
    n = len(srcs)

    def body(*refs):
        src_in, land_in = refs[:n], refs[n:2 * n]
        ss, rs = refs[2 * n:3 * n], refs[3 * n:4 * n]
        sends, recvs = _exchange_copies(src_in, land_in, ss, rs, blockwise)
        for cp in sends:
            cp.wait_send()
        for cp in recvs:
            cp.wait_recv()

    outs = pl.pallas_call(
        body, name=name,
        out_shape=[pltpu.HBM(s.shape, s.dtype) for s in srcs] + [pltpu.HBM(l.shape, l.dtype) for l in lands],
        in_specs=[_HBM] * (2 * n) + [_SEM] * (2 * n) + [pl.BlockSpec(memory_space=pl.ANY)],
        out_specs=[_HBM] * (2 * n),
        input_output_aliases={a: a for a in range(2 * n)},
        compiler_params=pltpu.CompilerParams(has_side_effects=_EFFECT),
    )(*srcs, *lands, *send_sems, *recv_sems, after)
    return outs[n:]


def _landing(own_block, me, slots=N_DEV):
    empty = lax.empty((slots,) + own_block.shape, own_block.dtype)
    return lax.dynamic_update_slice(empty, own_block[None], (me,) + (0,) * own_block.ndim)


N_CHIPS = N_DEV // 2
_CHIP_FLIPS = ((1, 0), (0, 1), (1, 1))


def _xyc():
    return lax.axis_index("x"), lax.axis_index("y"), lax.axis_index("c")


def _other_chips(x, y):
    return [(1 - x if fx else x, 1 - y if fy else y) for fx, fy in _CHIP_FLIPS]


def _remote(src, dst, send_sem, recv_sem, dev):
    return pltpu.make_async_remote_copy(src_ref=src, dst_ref=dst, send_sem=send_sem, recv_sem=recv_sem,
                                        device_id=dev, device_id_type=MESH)


def comm_call(name, bufs, sems_in, sems_out, fn, after=None, want_token=False):
    nb, ni, no = len(bufs), len(sems_in), len(sems_out)
    afters = [] if after is None else (list(after) if isinstance(after, (list, tuple)) else [after])
    na = len(afters)

    def body(*refs):
        buf_refs = refs[:nb]
        sin = refs[nb:nb + ni]
        sout = refs[nb + ni + na:nb + ni + na + no]
        fn(buf_refs, sin, sout)
        if want_token:
            tok = refs[nb + ni + na + no + nb]
            tok[...] = jnp.zeros_like(tok)

    out_shape = list(sems_out) + [pltpu.HBM(b.shape, b.dtype) for b in bufs]
    out_specs = [_SEM] * no + [_HBM] * nb
    if want_token:
        out_shape.append(jax.ShapeDtypeStruct((8, LANES), F32))
        out_specs.append(pl.BlockSpec(memory_space=pltpu.VMEM))
    args = [_in_hbm(b) for b in bufs] + list(sems_in) + afters
    outs = pl.pallas_call(
        body, name=name, out_shape=out_shape,
        in_specs=[_HBM] * nb + [_SEM] * ni + [pl.BlockSpec(memory_space=pl.ANY)] * na,
        out_specs=out_specs, input_output_aliases={a: no + a for a in range(nb)},
        compiler_params=pltpu.CompilerParams(has_side_effects=_EFFECT),
    )(*args)
    return list(outs[:no]), list(outs[no:no + nb]), (outs[no + nb] if want_token else None)


def _dma_sems(*sizes):
    return [pltpu.SemaphoreType.DMA((s,)) for s in sizes]


def gather_start(srcs, lands, name, after=None):
    n = len(srcs)

    def fn(bufs, sin, sout):
        x, y, c = _xyc()
        me = 4 * x + 2 * y + c
        for a in range(n):
            src, land = bufs[a], bufs[n + a]
            send, recv_d, recv_i = sout[3 * a:3 * a + 3]
            _remote(src, land.at[me], send.at[0], recv_d.at[0], (x, y, 1 - c)).start()
            for k, (px, py) in enumerate(_other_chips(x, y)):
                _remote(src, land.at[me], send.at[1 + k], recv_i.at[k], (px, py, c)).start()

    return comm_call(name, list(srcs) + list(lands), [], _dma_sems(4, 1, 3) * n, fn, after=after, want_token=True)


def gather_forward(srcs, lands, recv_i, after, name):
    n = len(srcs)

    def fn(bufs, sin, sout):
        x, y, c = _xyc()
        for a in range(n):
            src, land = bufs[a], bufs[n + a]
            f_send, f_recv = sout[2 * a:2 * a + 2]
            for k, (px, py) in enumerate(_other_chips(x, y)):
                blk = land.at[4 * px + 2 * py + c]
                _remote(src, blk, f_send.at[k], sin[a].at[k], (px, py, c)).wait_recv()
                _remote(blk, blk, f_send.at[k], f_recv.at[k], (x, y, 1 - c)).start()

    sems, bufs, _ = comm_call(name, list(srcs) + list(lands), recv_i, _dma_sems(3, 3) * n, fn, after=after)
    return sems, bufs


def gather_wait(srcs, lands, send, recv_d, f_send, f_recv, after, name):
    n = len(srcs)

    def fn(bufs, sin, sout):
        x, y, c = _xyc()
        sib = (x, y, 1 - c)
        for a in range(n):
            src, land = bufs[a], bufs[n + a]
            s_send, s_recv_d, s_fsend, s_frecv = sin[4 * a:4 * a + 4]
            sib_blk = land.at[4 * x + 2 * y + 1 - c]
            for k in range(4):
                _remote(src, sib_blk, s_send.at[k], s_recv_d.at[0], sib).wait_send()
            _remote(src, sib_blk, s_send.at[0], s_recv_d.at[0], sib).wait_recv()
            for k, (px, py) in enumerate(_other_chips(x, y)):
                cp = _remote(src, land.at[4 * px + 2 * py + 1 - c], s_fsend.at[k], s_frecv.at[k], sib)
                cp.wait_send()
                cp.wait_recv()

    sems_in = []
    for a in range(n):
        sems_in += [send[a], recv_d[a], f_send[a], f_recv[a]]
    _, bufs, _ = comm_call(name, list(srcs) + list(lands), sems_in, [], fn, after=after)
    return bufs[n:]


def scatter_pair_start(src4s, lands, name, after=None, whole=False):
    n = len(src4s)

    def fn(bufs, sin, sout):
        x, y, c = _xyc()
        for a in range(n):
            src = bufs[a] if whole else bufs[a].at[:, 1 - c]
            _remote(src, bufs[n + a], sout[2 * a].at[0], sout[2 * a + 1].at[0], (x, y, 1 - c)).start()

    return comm_call(name, list(src4s) + list(lands), [], _dma_sems(1, 1) * n, fn, after=after, want_token=True)


def scatter_pair_wait(src4s, lands, sems, after, name, whole=False):
    n = len(src4s)

    def fn(bufs, sin, sout):
        x, y, c = _xyc()
        for a in range(n):
            src = bufs[a] if whole else bufs[a].at[:, 1 - c]
            cp = _remote(src, bufs[n + a], sin[2 * a].at[0], sin[2 * a + 1].at[0], (x, y, 1 - c))
            cp.wait_send()
            cp.wait_recv()

    _, bufs, _ = comm_call(name, list(src4s) + list(lands), sems, [], fn, after=after)
    return bufs[:n], bufs[n:]


def _row_tile(R):
    for cand in range(256, 15, -16):
        if R % cand == 0 and R // cand >= 4:
            return cand
    return R


def chip_sum(src4, land, c, name):
    _, _, R, C = src4.shape
    tr = R

    def body(c_ref, a_ref, b_ref, o_ref):
        o_ref[...] = (a_ref[...].astype(F32) + b_ref[...].astype(F32)).astype(BF16)

    grid_spec = pltpu.PrefetchScalarGridSpec(
        num_scalar_prefetch=1, grid=(N_CHIPS, R // tr),
        in_specs=[pl.BlockSpec((None, None, tr, C), lambda q, i, cr: (q, cr[0], i, 0)),
                  pl.BlockSpec((None, tr, C), lambda q, i, cr: (q, i, 0))],
        out_specs=pl.BlockSpec((None, tr, C), lambda q, i, cr: (q, i, 0)))
    return pl.pallas_call(
        body, name=name, grid_spec=grid_spec,
        out_shape=jax.ShapeDtypeStruct((N_CHIPS, R, C), BF16),
        compiler_params=_params(("parallel", "parallel")),
    )(c.reshape(1).astype(jnp.int32), src4, land)


def scatter_chip_start(sums, lands, name):
    n = len(sums)

    def fn(bufs, sin, sout):
        x, y, c = _xyc()
        for a in range(n):
            for k, (px, py) in enumerate(_other_chips(x, y)):
                _remote(bufs[a].at[2 * px + py], bufs[n + a].at[2 * x + y], sout[2 * a].at[k], sout[2 * a + 1].at[k],
                        (px, py, c)).start()

    return comm_call(name, list(sums) + list(lands), [], _dma_sems(3, 3) * n, fn, want_token=True)


def scatter_chip_wait(sums, lands, sems, after, name):
    n = len(sums)

    def fn(bufs, sin, sout):
        x, y, c = _xyc()
        for a in range(n):
            for k, (px, py) in enumerate(_other_chips(x, y)):
                cp = _remote(bufs[a].at[2 * px + py], bufs[n + a].at[2 * px + py], sin[2 * a].at[k],
                             sin[2 * a + 1].at[k], (px, py, c))
                cp.wait_send()
                cp.wait_recv()

    _, bufs, _ = comm_call(name, list(sums) + list(lands), sems, [], fn, after=after)
    return bufs[:n], bufs[n:]


def _adamw_math(w, g, m, v):
    m = ADAM_B1 * m + (1.0 - ADAM_B1) * g
    v = ADAM_B2 * v + (1.0 - ADAM_B2) * (g * g)
    m_hat = m / (1.0 - ADAM_B1 ** ADAM_STEP)
    v_hat = v / (1.0 - ADAM_B2 ** ADAM_STEP)
    delta = -ADAM_LR * (m_hat / (jnp.sqrt(v_hat) + ADAM_EPS) + ADAM_WD * w)
    return delta, m, v


def _sum_partials(p_ref, own_ref, mine):
    own = own_ref[...].astype(F32)
    g = None
    for s in range(p_ref.shape[0]):
        term = jnp.where(mine == s, own, p_ref[s].astype(F32))
        g = term if g is None else g + term
    return g


def adamw_sharded(parts, sums, my_chip, w, m, v, name):
    R, C = w.shape
    S = parts.shape[0]
    tr = _row_tile(R)

    def body(mc_ref, p_ref, o_ref, w_ref, m_ref, v_ref, g_ref, d_ref, nm_ref, nv_ref):
        g = _sum_partials(p_ref, o_ref, mc_ref[0])
        delta, nm, nv = _adamw_math(w_ref[...], g, m_ref[...], v_ref[...])
        g_ref[...] = g
        d_ref[...] = delta
        nm_ref[...] = nm
        nv_ref[...] = nv

    row = pl.BlockSpec((tr, C), lambda i, mc: (i, 0))
    shp = jax.ShapeDtypeStruct((R, C), F32)
    grid_spec = pltpu.PrefetchScalarGridSpec(
        num_scalar_prefetch=1, grid=(R // tr,),
        in_specs=[pl.BlockSpec((S, tr, C), lambda i, mc: (0, i, 0)),
                  pl.BlockSpec((None, tr, C), lambda i, mc: (mc[0], i, 0)), row, row, row],
        out_specs=[row, row, row, row])
    return pl.pallas_call(
        body, name=name, grid_spec=grid_spec, out_shape=[shp, shp, shp, shp],
        compiler_params=_params(("parallel",)),
    )(my_chip.reshape(1).astype(jnp.int32), parts, sums, w, m, v)


def adamw_small(parts, w, m, v, name):
    R, C = w.shape

    def body(p_ref, w_ref, m_ref, v_ref, g_ref, d_ref, nm_ref, nv_ref):
        g = p_ref[0]
        for s in range(1, N_DEV):
            g = g + p_ref[s]
        delta, nm, nv = _adamw_math(w_ref[...], g, m_ref[...], v_ref[...])
        g_ref[...] = g
        d_ref[...] = delta
        nm_ref[...] = nm
        nv_ref[...] = nv

    shp = jax.ShapeDtypeStruct((R, C), F32)
    return pl.pallas_call(
        body, name=name, out_shape=[shp, shp, shp, shp], compiler_params=_params(None),
    )(parts, w, m, v)


_ROW_NORM_FFN1, _ROW_NORM_MIX, _ROW_NORM_FFN2, _ROW_NORM_PLE, _ROW_NORM_FINAL = 0, 1, 2, 3, 4
_ROW_B_F, _ROW_REL, _ROW_LOSS, _SMALL_ROWS = 5, 6, 7, 8


def _pack_small(D, norm_ffn1, norm_mix, norm_ffn2, norm_ple, norm_final, b_f, rel_table):
    def row(v):
        v = v.reshape(1, -1)
        return jnp.pad(v, ((0, 0), (0, D - v.shape[1])))
    return jnp.concatenate([row(norm_ffn1), row(norm_mix), row(norm_ffn2), row(norm_ple),
                            row(norm_final), row(b_f), row(rel_table),
                            jnp.zeros((1, D), F32)], axis=0)


def _unpack_small(a, shapes):
    return {"norm_ffn1": a[_ROW_NORM_FFN1].reshape(shapes["norm_ffn1"]),
            "norm_mix": a[_ROW_NORM_MIX].reshape(shapes["norm_mix"]),
            "b_f": a[_ROW_B_F, :N_FOX].reshape(shapes["b_f"]),
            "norm_ffn2": a[_ROW_NORM_FFN2].reshape(shapes["norm_ffn2"]),
            "norm_ple": a[_ROW_NORM_PLE].reshape(shapes["norm_ple"]),
            "rel_table": a[_ROW_REL, :N_REL_BUCKETS * N_DIL].reshape(shapes["rel_table"]),
            "norm_final": a[_ROW_NORM_FINAL].reshape(shapes["norm_final"])}


def local_step(x, p, tgt, g_ffn1, g_mix, g_ffn2, g_ple, g_final, b_f, rel_table,
               forward, weights, emit, emit2, dw_theirs, dw_mine, first_dep):
    T, D = x.shape
    P = p.shape[1]
    CW = D // N_DEV
    tq = _tile(T, 256)

    h1 = rms_fwd(x, g_ffn1, "rms_ffn1", dep=first_dep)
    tiles = bias_tiles(rel_table, T, tq)
    forward("ffn1_g", [tiles, h1])
    wg1, = weights("ffn1_g", h1)
    gate1 = ffn_gate(h1, wg1, "ffn1_gate")
    forward("ffn1_u", gate1)
    wu1, = weights("ffn1_u", gate1)
    a1, b1, s1 = ffn_up_gated(h1, wu1, gate1, "ffn1_up")
    forward("ffn1_d", s1)
    wd1, = weights("ffn1_d", s1)
    x1, h2 = ffn_down(s1, wd1, 0, 1, x, g_mix, "ffn1_down")

    forward("mix_in", h2)
    w3, wf = weights("mix_in", h2)
    qkv = mm_nt([(h2, w3)], "mix_qkv", tn=768, out_dtype=BF16)
    uf = mm_nt([(h2, wf)], "mix_forget", tn=LANES, out_dtype=F32)
    bfp = jnp.pad(b_f.reshape(1, N_FOX), ((0, 0), (0, LANES - N_FOX)))
    c, ct = fox_gate_fwd(uf, bfp, "fox_gate")
    cat, lse = attention_fwd(qkv, c, ct, tiles, "attention")
    forward("mix_out", cat)
    wo, = weights("mix_out", cat)
    x2 = mm_nn(cat, wo, "mix_out", tn=512, out_dtype=F32, res=x1)

    h3 = rms_fwd(x2, g_ffn2, "rms_ffn2")
    forward("ffn2_gu", h3)
    wgu2, = weights("ffn2_gu", h3)
    a2, b2, s2 = ffn_up(h3, wgu2, 0, 1, 2, "ffn2_up")
    forward("ffn2_d", s2)
    wd2, = weights("ffn2_d", s2)
    x3, h4 = ffn_down(s2, wd2, 0, 1, x2, g_ple, "ffn2_down")
    forward("ple", x3)

    wpg, wpp = weights("ple", h4)
    z = mm_nn(h4, wpg, "ple_gate", tn=512, out_dtype=F32)
    pp = mm_nn(p, wpp, "ple_proj", tn=CW, tm=T, out_dtype=F32, n_out=D,
               b_block=(P, CW), b_map=lambda n, i: (n, 0))
    loss_row, dx4, dg_final, dz, dpp = ple_loss(x3, z, pp, g_final, tgt, "ple_loss")

    grads = {}
    grads["w_ple_proj"] = mm_tn(p, dpp, "ple_proj_dw", grid=(N_DEV,),
                                a_block=(T, P), a_map=lambda n: (0, 0),
                                b_block=(T, CW), b_map=lambda n: (0, n),
                                o_block=(P, CW), o_map=lambda n: (n, 0),
                                out_shape=(N_DEV * P, CW))
    grads["w_ple_gate"] = mm_tn_plain(h4, dz, "ple_gate_dw")
    tok = emit("ple", grads)
    dh4 = mm_nt([(dz, wpg)], "ple_gate_dh", tn=1024, out_dtype=F32, dep=tok)
    tok = emit2("ple", dh4)
    dx3, dx3h, dg_ple = rms_bwd(dh4, x3, g_ple, dx4, "rms_ple_bwd", dep=tok)

    da2, db2 = ffn_bwd_act(dx3h, wd2, 0, 1, a2, b2, "ffn2_bwd_act")
    tok = dw_theirs("ffn2_w_down", s2, dx3h)
    tok = dw_theirs("ffn2_w_gate", da2, h3, dep=tok)
    tok = dw_theirs("ffn2_w_up", db2, h3, dep=tok)
    tok = dw_mine("ffn2_w_down", s2, dx3h, dep=tok)
    tok = dw_mine("ffn2_w_gate", da2, h3, dep=tok)
    tok = dw_mine("ffn2_w_up", db2, h3, dep=tok)
    dh3 = ffn_bwd_dh(da2, db2, wgu2, wgu2, 0, 1, 2, D, "ffn2_bwd_dh", dep=tok)
    dx2, _, dg_ffn2 = rms_bwd(dh3, x2, g_ffn2, dx3, "rms_ffn2_bwd", half=False)

    dcat = mm_nt([(dx2, wo)], "mix_out_dh", tn=1024, out_dtype=BF16)
    grads["w_o"] = mm_tn_plain(cat, dx2, "mix_out_dw")
    dq, dk, dv, dct, dtiles = attention_bwd(qkv, c, ct, tiles, lse, cat, dcat, "attention_bwd")
    dctp = jnp.pad(dct[:, 0, :], ((0, LANES - N_HEADS), (0, 0)))
    duf, dbf = fox_gate_bwd(dctp, uf, bfp, "fox_gate_bwd")
    drel = rel_table_grad(dtiles, T, "rel_table_grad")[:, 0, :N_REL_BUCKETS].T
    du3 = jnp.concatenate([dq, dk, dv], axis=1)
    grads["w3"] = mm_tn_plain(du3, h2, "mix_qkv_dw", tm=768)
    grads["wf"] = mm_tn_plain(duf, h2, "mix_forget_dw", tm=LANES)
    tok = emit("mix", grads)
    dh2 = mm_nn_sum([(du3, w3), (duf, wf)], "mix_in_dh", tn=512, out_dtype=F32, dep=tok)
    tok = emit2("mix", dh2)
    dx1, dx1h, dg_mix = rms_bwd(dh2, x1, g_mix, dx2, "rms_mix_bwd", dep=tok)

    da1, db1 = ffn_bwd_act(dx1h, wd1, 0, 1, a1, b1, "ffn1_bwd_act")
    tok = dw_theirs("ffn1_w_down", s1, dx1h)
    tok = dw_theirs("ffn1_w_gate", da1, h1, dep=tok)
    tok = dw_mine("ffn1_w_down", s1, dx1h, dep=tok)
    tok = dw_theirs("ffn1_w_up", db1, h1, dep=tok)
    tok = dw_mine("ffn1_w_gate", da1, h1, dep=tok)
    tok = dw_mine("ffn1_w_up", db1, h1, dep=tok)
    dh1 = ffn_bwd_dh(da1, db1, wg1, wu1, 0, 0, 1, D, "ffn1_bwd_dh", dep=tok)
    dx0, _, dg_ffn1 = rms_bwd(dh1, x, g_ffn1, dx1, "rms_ffn1_bwd", half=False)

    small = _pack_small(D, dg_ffn1, dg_mix, dg_ffn2, dg_ple, dg_final, dbf[:, :N_FOX], drel)
    small = small.at[_ROW_LOSS, :LANES].set(loss_row[0])
    grads["small"] = small
    emit("small", grads)
    return dx0


def _split_w_in(w_in_t):
    df, dd = N_FOX * HEAD_DIM, N_DIL * HEAD_DIM
    o = np.cumsum([0, df, df, df, N_FOX, dd, dd, dd]).tolist()
    qa, ka, va, f, qb, kb, vb = [w_in_t[o[i]:o[i + 1]] for i in range(7)]
    return jnp.concatenate([qa, qb, ka, kb, va, vb], axis=0), f


def _join_w_in(d3, dfg):
    df, dd = N_FOX * HEAD_DIM, N_DIL * HEAD_DIM
    o = np.cumsum([0, df, dd, df, dd, df, dd]).tolist()
    qa, qb, ka, kb, va, vb = [d3[o[i]:o[i + 1]] for i in range(6)]
    return jnp.concatenate([qa, ka, va, dfg, qb, kb, vb], axis=0)


def rows_to_bf16(a3, name, dep=None):
    R, _, C = a3.shape
    tc = _tile(C, 512)

    def body(a_ref, *rest):
        rest[-1][...] = a_ref[...].astype(BF16)

    in_specs = [pl.BlockSpec((R, None, tc), lambda n: (0, 0, n))]
    args = [a3]
    if dep is not None:
        in_specs.append(_dep_spec(1))
        args.append(dep)
    return pl.pallas_call(
        body, name=name, grid=(C // tc,), in_specs=in_specs,
        out_specs=pl.BlockSpec((R, tc), lambda n: (0, n)),
        out_shape=jax.ShapeDtypeStruct((R, C), BF16),
        compiler_params=_params(("parallel",)),
    )(*args)


def adamw_rows3d(parts, sums, my_chip, w3, m3, v3, name):
    R, _, C = w3.shape
    S = parts.shape[0]
    tc = _tile(C, 512)

    def body(mc_ref, p_ref, o_ref, w_ref, m_ref, v_ref, g_ref, d_ref, nm_ref, nv_ref):
        g = _sum_partials(p_ref, o_ref, mc_ref[0])
        delta, nm, nv = _adamw_math(w_ref[...], g, m_ref[...], v_ref[...])
        g_ref[...] = g
        d_ref[...] = delta
        nm_ref[...] = nm
        nv_ref[...] = nv

    col = pl.BlockSpec((R, None, tc), lambda n, mc: (0, 0, n))
    shp = jax.ShapeDtypeStruct((R, 1, C), F32)
    grid_spec = pltpu.PrefetchScalarGridSpec(
        num_scalar_prefetch=1, grid=(C // tc,),
        in_specs=[pl.BlockSpec((S, R, tc), lambda n, mc: (0, 0, n)),
                  pl.BlockSpec((None, R, tc), lambda n, mc: (mc[0], 0, n)), col, col, col],
        out_specs=[col, col, col, col])
    return pl.pallas_call(
        body, name=name, grid_spec=grid_spec, out_shape=[shp, shp, shp, shp],
        compiler_params=_params(("parallel",)),
    )(my_chip.reshape(1).astype(jnp.int32), parts, sums, w3, m3, v3)


def kernel(x, p, norm_ffn1, ffn1_w_gate, ffn1_w_up, ffn1_w_down, norm_mix, w_in, b_f, w_o, norm_ffn2, ffn2_w_gate, ffn2_w_up, ffn2_w_down, norm_ple, w_ple_gate, w_ple_proj, rel_table, norm_final, loss_target, m_norm_ffn1, m_ffn1_w_gate, m_ffn1_w_up, m_ffn1_w_down, m_norm_mix, m_w_in, m_b_f, m_w_o, m_norm_ffn2, m_ffn2_w_gate, m_ffn2_w_up, m_ffn2_w_down, m_norm_ple, m_w_ple_gate, m_w_ple_proj, m_rel_table, m_norm_final, v_norm_ffn1, v_ffn1_w_gate, v_ffn1_w_up, v_ffn1_w_down, v_norm_mix, v_w_in, v_b_f, v_w_o, v_norm_ffn2, v_ffn2_w_gate, v_ffn2_w_up, v_ffn2_w_down, v_norm_ple, v_w_ple_gate, v_w_ple_proj, v_rel_table, v_norm_final):
    names = ["norm_ffn1", "ffn1_w_gate", "ffn1_w_up", "ffn1_w_down", "norm_mix", "w_in", "b_f", "w_o",
             "norm_ffn2", "ffn2_w_gate", "ffn2_w_up", "ffn2_w_down", "norm_ple", "w_ple_gate",
             "w_ple_proj", "rel_table", "norm_final"]
    w = dict(zip(names, [norm_ffn1, ffn1_w_gate, ffn1_w_up, ffn1_w_down, norm_mix, w_in, b_f, w_o,
                         norm_ffn2, ffn2_w_gate, ffn2_w_up, ffn2_w_down, norm_ple, w_ple_gate,
                         w_ple_proj, rel_table, norm_final]))
    m = dict(zip(names, [m_norm_ffn1, m_ffn1_w_gate, m_ffn1_w_up, m_ffn1_w_down, m_norm_mix, m_w_in,
                         m_b_f, m_w_o, m_norm_ffn2, m_ffn2_w_gate, m_ffn2_w_up, m_ffn2_w_down,
                         m_norm_ple, m_w_ple_gate, m_w_ple_proj, m_rel_table, m_norm_final]))
    v = dict(zip(names, [v_norm_ffn1, v_ffn1_w_gate, v_ffn1_w_up, v_ffn1_w_down, v_norm_mix, v_w_in,
                         v_b_f, v_w_o, v_norm_ffn2, v_ffn2_w_gate, v_ffn2_w_up, v_ffn2_w_down,
                         v_norm_ple, v_w_ple_gate, v_w_ple_proj, v_rel_table, v_norm_final]))
    sharded = ["ffn1_w_gate", "ffn1_w_up", "ffn1_w_down", "w_in", "w_o", "ffn2_w_gate", "ffn2_w_up",
               "ffn2_w_down", "w_ple_gate", "w_ple_proj"]
    small_names = [n for n in names if n not in sharded]

    xs, ps, tgt = x[0], p[0, 0], loss_target[0]
    T, D = xs.shape
    transposed = ("ffn1_w_gate", "ffn1_w_up", "ffn2_w_gate", "ffn2_w_up")

    def view(t, n):
        if n in transposed:
            return t[n][0].T
        if n == "w_in":
            return jnp.transpose(t[n], (2, 0, 1))
        return t[n][0]

    def unview(a, n):
        if n in transposed:
            return a.T.reshape(w[n].shape)
        if n == "w_in":
            return jnp.transpose(a, (1, 2, 0))
        return a.reshape(w[n].shape)

    sh = {n: view(w, n) for n in sharded}
    m_sh = {n: view(m, n) for n in sharded}
    v_sh = {n: view(v, n) for n in sharded}
    F8 = sh["ffn1_w_down"].shape[0]
    WIN8 = sh["w_in"].shape[0]
    me = 4 * lax.axis_index("x") + 2 * lax.axis_index("y") + lax.axis_index("c")

    def start(groups, name, after=None):
        srcs = [s for grp in groups for s in grp]
        sems, bufs, token = gather_start(srcs, [_landing(s, me) for s in srcs], name, after=after)
        return sems, bufs[:len(srcs)], bufs[len(srcs):], token

    cat0 = lambda ns, z: (jnp.concatenate([sh[n] for n in ns], axis=0) + z).astype(BF16)
    sems_a, srcs_a, lands_a, token_a = start(
        [[sh["ffn1_w_gate"].astype(BF16)], [sh["ffn1_w_up"].astype(BF16)], [sh["ffn1_w_down"].astype(BF16)]],
        "gather_start_ffn1")
    zero = token_a[0, 0]
    w_in_bf = rows_to_bf16(sh["w_in"], "w_in_bf16", dep=token_a)
    sems_b, srcs_b, lands_b, g_token = start(
        [[w_in_bf, (sh["w_o"] + zero).astype(BF16)],
         [cat0(["ffn2_w_gate", "ffn2_w_up"], zero), (sh["ffn2_w_down"] + zero).astype(BF16)],
         [(sh["w_ple_gate"] + zero).astype(BF16), (sh["w_ple_proj"] + zero).astype(BF16)]],
        "gather_start_rest", after=token_a)
    order = ["ffn1_g", "ffn1_u", "ffn1_d", "mix_in", "mix_out", "ffn2_gu", "ffn2_d", "ple"]
    group_sizes = [1, 1, 1, 1, 1, 1, 1, 2]
    g_sems, g_srcs, g_lands = sems_a + sems_b, srcs_a + srcs_b, lands_a + lands_b
    g_send, g_recv_d, g_recv_i = g_sems[0::3], g_sems[1::3], g_sems[2::3]
    first = np.cumsum([0] + group_sizes).tolist()
    passed = {}

    def arrays_of(group):
        k = order.index(group)
        return slice(first[k], first[k + 1])

    def forward(group, after):
        sl = arrays_of(group)
        f_sems, bufs = gather_forward(g_srcs[sl], g_lands[sl], g_recv_i[sl], after, "gather_forward_" + group)
        k = len(bufs) // 2
        passed[group] = (f_sems[0::2], f_sems[1::2], bufs[:k], bufs[k:])

    def weights(group, after):
        sl = arrays_of(group)
        f_send, f_recv, srcs, lands = passed[group]
        got = gather_wait(srcs, lands, g_send[sl], g_recv_d[sl], f_send, f_recv, after, "gather_wait_" + group)
        if group in ("ffn1_g", "ffn1_u", "ffn1_d"):
            return (got[0].reshape(N_DEV * F8, D),)
        if group == "ffn2_gu":
            return (got[0].reshape(N_DEV * 2 * F8, D),)
        if group in ("ffn2_d", "mix_out"):
            return (got[0].reshape(-1, D),)
        if group == "ple":
            return got[0].reshape(-1, D), got[1].reshape(-1, got[1].shape[2])
        w3, wf8 = _split_w_in(got[0].reshape(N_DEV * WIN8, D))
        return w3, jnp.pad(wf8, ((0, LANES - N_FOX), (0, 0)))

    ffn_names = ["ffn2_w_down", "ffn2_w_gate", "ffn2_w_up", "ffn1_w_down", "ffn1_w_gate", "ffn1_w_up"]
    scatter_groups = {"ple": ["w_ple_gate", "w_ple_proj"], "mix": ["w_in", "w_o"]}
    scatter_groups.update({n: [n] for n in ffn_names})
    x_i, y_i, c_i = _xyc()
    my_chip = 2 * x_i + y_i
    pair_stage, chip_stage, small_stage = {}, {}, {}

    def emit(group, grads, after=None):
        if group == "small":
            src = grads["small"]
            ss, rs, srcs, lands, token = exchange_start([src], [_landing(src, me)], [False], "scatter_start_small")
            small_stage["small"] = (ss, rs, srcs, lands)
            return token
        src4s = []
        for n in scatter_groups[group]:
            if n == "w_in":
                full = _join_w_in(grads["w3"], grads["wf"][:N_FOX])
                src4s.append(full.reshape(N_CHIPS, 2, WIN8, D))
            else:
                src4s.append(grads[n].reshape((N_CHIPS, 2) + sh[n].shape))
        lands = [lax.empty((N_CHIPS,) + s.shape[2:], BF16) for s in src4s]
        sems, bufs, token = scatter_pair_start(src4s, lands, "scatter_pair_start_" + group, after=after)
        k = len(src4s)
        pair_stage[group] = (sems, bufs[:k], bufs[k:])
        return token

    def emit2(group, after):
        sems, src4s, lands = pair_stage[group]
        src4s, lands = scatter_pair_wait(src4s, lands, sems, after, "scatter_pair_wait_" + group)
        sums = [chip_sum(s4, la, c_i, "chip_sum_" + n)
                for s4, la, n in zip(src4s, lands, scatter_groups[group])]
        chip_lands = [lax.empty(s.shape, s.dtype) for s in sums]
        sems, bufs, token = scatter_chip_start(sums, chip_lands, "scatter_chip_start_" + group)
        k = len(sums)
        chip_stage[group] = (sems, bufs[:k], bufs[k:])
        return token

    def dw_theirs(n, act, other, dep=None):
        theirs = ffn_dw_side(act, other, c_i, False, n + "_dw_theirs", dep=dep)
        theirs = theirs.reshape((N_CHIPS,) + sh[n].shape)
        sems, bufs, token = scatter_pair_start([theirs], [lax.empty(theirs.shape, BF16)],
                                               "scatter_pair_start_" + n, whole=True)
        pair_stage[n] = (sems, bufs[:1], bufs[1:])
        return token

    def dw_mine(n, act, other, dep=None):
        sems, theirs, lands = pair_stage[n]
        _, lands = scatter_pair_wait(theirs, lands, sems, dep, "scatter_pair_wait_" + n, whole=True)
        sums = ffn_dw_side(act, other, c_i, True, n + "_dw_mine", add=lands[0].reshape(-1, D))
        sums = sums.reshape((N_CHIPS,) + sh[n].shape)
        sems, bufs, token = scatter_chip_start([sums], [lax.empty(sums.shape, BF16)], "scatter_chip_start_" + n)
        chip_stage[n] = (sems, bufs[:1], bufs[1:])
        return token

    dx0 = local_step(
        xs, ps, tgt, w["norm_ffn1"], w["norm_mix"], w["norm_ffn2"], w["norm_ple"],
        w["norm_final"].reshape(1, D), w["b_f"], w["rel_table"], forward, weights, emit, emit2,
        dw_theirs, dw_mine, g_token)

    res = {}
    after = dx0
    for group in ["ple"] + ffn_names[:3] + ["mix"] + ffn_names[3:]:
        sems, sums, chip_lands = chip_stage[group]
        sums, parts = scatter_chip_wait(sums, chip_lands, sems, after, "scatter_chip_wait_" + group)
        for n, part, own in zip(scatter_groups[group], parts, sums):
            update = adamw_rows3d if n == "w_in" else adamw_sharded
            g, d, nm, nv = update(part, own, my_chip, sh[n], m_sh[n], v_sh[n], "adamw_" + n)
            res[n] = tuple(unview(a, n) for a in (g, d, nm, nv))
            after = g
    ss, rs, srcs, lands = small_stage["small"]
    small_parts, = exchange_wait(ss, rs, srcs, lands, [False], after, "scatter_wait_small")
    pack = lambda t: _pack_small(D, t["norm_ffn1"], t["norm_mix"], t["norm_ffn2"], t["norm_ple"],
                                 t["norm_final"], t["b_f"], t["rel_table"])
    gs, ds, ms, vs = adamw_small(small_parts, pack(w), pack(m), pack(v), "adamw_small")
    shapes = {n: w[n].shape for n in small_names}
    unpacked = [_unpack_small(a, shapes) for a in (gs, ds, ms, vs)]
    for n in small_names:
        res[n] = tuple(u[n] for u in unpacked)
    loss = gs[_ROW_LOSS, 0]

    out = [loss, dx0.reshape(x.shape)]
    for k in range(4):
        out += [res[n][k] for n in names]
    return tuple(out)
```

```python
import functools
import math

import numpy as np
import jax
import jax.numpy as jnp
from jax import lax
from jax.experimental import pallas as pl
from jax.experimental.pallas import tpu as pltpu

F32 = jnp.float32
BF16 = jnp.bfloat16

N_DEV = 8
HEAD_DIM = 128
N_FOX = 8
N_DIL = 8
N_HEADS = N_FOX + N_DIL
DILATED_PATTERNS = ((128, 1), (512, 4), (2048, 16))
N_REL_BUCKETS = 32
REL_MAX_DISTANCE = 2048
RMS_EPS = 1e-6
NEG_INF = -1e30
LANES = 128
VMEM_LIMIT = 56 * 1024 * 1024

ADAM_LR = 0.001
ADAM_B1 = 0.9
ADAM_B2 = 0.999
ADAM_EPS = 1e-08
ADAM_WD = 0.01
ADAM_STEP = 10

MESH = pl.DeviceIdType.MESH


def _params(sem):
    return pltpu.CompilerParams(dimension_semantics=sem, vmem_limit_bytes=VMEM_LIMIT)


def _dot(a, b, ca, cb, precision=None):
    return lax.dot_general(a, b, (((ca,), (cb,)), ((), ())),
                           preferred_element_type=F32, precision=precision)


def _sigmoid(z):
    return 1.0 / (1.0 + jnp.exp(-z))


def _tile(n, want):
    t = min(n, want)
    assert n % t == 0, (n, t)
    return t


def _dep_spec(ngrid):
    return pl.BlockSpec((8, LANES), lambda *_: (0, 0))


def rms_fwd(x, g, name, dep=None):
    T, D = x.shape
    tm = _tile(T, 256)

    def body(x_ref, g_ref, *rest):
        h_ref = rest[-1]
        xv = x_ref[...]
        r = lax.rsqrt(jnp.mean(xv * xv, axis=-1, keepdims=True) + RMS_EPS)
        h_ref[...] = (xv * r * g_ref[...]).astype(BF16)

    in_specs = [pl.BlockSpec((tm, D), lambda i: (i, 0)), pl.BlockSpec((1, D), lambda i: (0, 0))]
    args = [x, g]
    if dep is not None:
        in_specs.append(_dep_spec(1))
        args.append(dep)
    return pl.pallas_call(
        body, name=name, grid=(T // tm,), in_specs=in_specs,
        out_specs=pl.BlockSpec((tm, D), lambda i: (i, 0)),
        out_shape=jax.ShapeDtypeStruct((T, D), BF16),
        compiler_params=_params(("parallel",)),
    )(*args)


def rms_bwd(dh, x, g, dres, name, dep=None, half=True):
    T, D = x.shape
    tm = _tile(T, 256)

    def body(dh_ref, x_ref, g_ref, dres_ref, *rest):
        dx_ref, dg_ref = (rest[-3], rest[-1]) if half else (rest[-2], rest[-1])
        i = pl.program_id(0)
        xv = x_ref[...]
        r = lax.rsqrt(jnp.mean(xv * xv, axis=-1, keepdims=True) + RMS_EPS)
        xh = xv * r
        d = dh_ref[...]
        u = d * g_ref[...]
        dx = dres_ref[...] + r * (u - xh * jnp.mean(u * xh, axis=-1, keepdims=True))
        dx_ref[...] = dx
        if half:
            rest[-2][...] = (0.5 * dx).astype(BF16)
        part = jnp.sum(d * xh, axis=0, keepdims=True)

        @pl.when(i == 0)
        def _():
            dg_ref[...] = part

        @pl.when(i > 0)
        def _():
            dg_ref[...] += part

    row = pl.BlockSpec((tm, D), lambda i: (i, 0))
    vec = pl.BlockSpec((1, D), lambda i: (0, 0))
    in_specs = [row, row, vec, row]
    args = [dh, x, g, dres]
    if dep is not None:
        in_specs.append(_dep_spec(1))
        args.append(dep)
    out_specs = [row, row, vec] if half else [row, vec]
    out_shape = [jax.ShapeDtypeStruct((T, D), F32)] + ([jax.ShapeDtypeStruct((T, D), BF16)] if half else [])
    out_shape.append(jax.ShapeDtypeStruct((1, D), F32))
    outs = pl.pallas_call(
        body, name=name, grid=(T // tm,),
        in_specs=in_specs, out_specs=out_specs, out_shape=out_shape,
        compiler_params=_params(("arbitrary",)),
    )(*args)
    return tuple(outs) if half else (outs[0], None, outs[1])


def ple_loss(x, z, pp, g, target, name):
    T, D = x.shape
    tm = _tile(T, 256)

    def body(x_ref, z_ref, p_ref, g_ref, t_ref, loss_ref, dx_ref, dg_ref, dz_ref, dp_ref):
        i = pl.program_id(0)
        gate = _sigmoid(z_ref[...])
        ppv = p_ref[...]
        xv = x_ref[...] + gate * ppv
        gv = g_ref[...]
        r = lax.rsqrt(jnp.mean(xv * xv, axis=-1, keepdims=True) + RMS_EPS)
        xh = xv * r
        e = xh * gv - t_ref[...]
        lpart = 0.5 * jnp.sum(jnp.mean(e * e, axis=-1, keepdims=True), axis=0, keepdims=True)
        lrow = jnp.broadcast_to(lpart, (1, LANES))
        d = e * (1.0 / D)
        u = d * gv
        dx = r * (u - xh * jnp.mean(u * xh, axis=-1, keepdims=True))
        dx_ref[...] = dx
        dz_ref[...] = (dx * ppv * gate * (1.0 - gate)).astype(BF16)
        dp_ref[...] = (dx * gate).astype(BF16)
        part = jnp.sum(d * xh, axis=0, keepdims=True)

        @pl.when(i == 0)
        def _():
            dg_ref[...] = part
            loss_ref[...] = lrow

        @pl.when(i > 0)
        def _():
            dg_ref[...] += part
            loss_ref[...] += lrow

    row = pl.BlockSpec((tm, D), lambda i: (i, 0))
    vec = pl.BlockSpec((1, D), lambda i: (0, 0))
    return pl.pallas_call(
        body, name=name, grid=(T // tm,),
        in_specs=[row, row, row, vec, row],
        out_specs=[pl.BlockSpec((1, LANES), lambda i: (0, 0)), row, vec, row, row],
        out_shape=[jax.ShapeDtypeStruct((1, LANES), F32), jax.ShapeDtypeStruct((T, D), F32),
                   jax.ShapeDtypeStruct((1, D), F32), jax.ShapeDtypeStruct((T, D), BF16),
                   jax.ShapeDtypeStruct((T, D), BF16)],
        compiler_params=_params(("arbitrary",)),
    )(x, z, pp, g, target)


def _bf(v, scale=None):
    if scale is not None:
        v = v * scale
    return v.astype(BF16)


def mm_nn(a, b, name, *, tn, out_dtype, tm=512, n_out=None, b_block=None, b_map=None,
          res=None):
    T, K = a.shape
    N = n_out if n_out is not None else b.shape[1]
    tm = _tile(T, tm)
    tn = _tile(N, tn)
    b_block = b_block or (K, tn)
    b_map = b_map or (lambda n, i: (0, n))

    def body(*refs):
        a_ref, b_ref = refs[0], refs[1]
        o_ref = refs[-1]
        acc = _dot(_bf(a_ref[...]), _bf(b_ref[...]), 1, 0)
        if res is not None:
            acc = refs[2][...] + acc
        o_ref[...] = acc.astype(out_dtype)

    in_specs = [pl.BlockSpec((tm, K), lambda n, i: (i, 0)), pl.BlockSpec(b_block, b_map)]
    args = [a, b]
    if res is not None:
        in_specs.append(pl.BlockSpec((tm, tn), lambda n, i: (i, n)))
        args.append(res)
    return pl.pallas_call(
        body, name=name, grid=(N // tn, T // tm), in_specs=in_specs,
        out_specs=pl.BlockSpec((tm, tn), lambda n, i: (i, n)),
        out_shape=jax.ShapeDtypeStruct((T, N), out_dtype),
        compiler_params=_params(("parallel", "parallel")),
    )(*args)


def mm_nn_sum(pairs, name, *, tn, out_dtype, tm=512, dep=None):
    T = pairs[0][0].shape[0]
    N = pairs[0][1].shape[1]
    tm = _tile(T, tm)
    tn = _tile(N, tn)
    npair = len(pairs)

    def body(*refs):
        acc = None
        for q in range(npair):
            part = _dot(_bf(refs[2 * q][...]), _bf(refs[2 * q + 1][...]), 1, 0)
            acc = part if acc is None else acc + part
        refs[-1][...] = acc.astype(out_dtype)

    in_specs, args = [], []
    for a, b in pairs:
        K = a.shape[1]
        in_specs += [pl.BlockSpec((tm, K), lambda n, i: (i, 0)), pl.BlockSpec((K, tn), lambda n, i: (0, n))]
        args += [a, b]
    if dep is not None:
        in_specs.append(_dep_spec(2))
        args.append(dep)
    return pl.pallas_call(
        body, name=name, grid=(N // tn, T // tm), in_specs=in_specs,
        out_specs=pl.BlockSpec((tm, tn), lambda n, i: (i, n)),
        out_shape=jax.ShapeDtypeStruct((T, N), out_dtype),
        compiler_params=_params(("parallel", "parallel")),
    )(*args)


def mm_nt(pairs, name, *, tn, out_dtype, tm=512, dep=None):
    T = pairs[0][0].shape[0]
    N = pairs[0][1].shape[0]
    tm = _tile(T, tm)
    tn = _tile(N, tn)
    npair = len(pairs)

    def body(*refs):
        o_ref = refs[-1]
        acc = None
        for q in range(npair):
            part = _dot(_bf(refs[2 * q][...]), _bf(refs[2 * q + 1][...]), 1, 1)
            acc = part if acc is None else acc + part
        o_ref[...] = acc.astype(out_dtype)

    in_specs, args = [], []
    for a, b in pairs:
        K = a.shape[1]
        in_specs += [pl.BlockSpec((tm, K), lambda n, i: (i, 0)), pl.BlockSpec((tn, K), lambda n, i: (n, 0))]
        args += [a, b]
    if dep is not None:
        in_specs.append(_dep_spec(2))
        args.append(dep)
    return pl.pallas_call(
        body, name=name, grid=(N // tn, T // tm), in_specs=in_specs,
        out_specs=pl.BlockSpec((tm, tn), lambda n, i: (i, n)),
        out_shape=jax.ShapeDtypeStruct((T, N), out_dtype),
        compiler_params=_params(("parallel", "parallel")),
    )(*args)


def mm_tn(a, b, name, *, grid, a_block, a_map, b_block, b_map, o_block, o_map, out_shape,
          b_scale=None, dep=None):
    def body(a_ref, b_ref, *rest):
        rest[-1][...] = _dot(_bf(a_ref[...]), _bf(b_ref[...], b_scale), 0, 0).astype(BF16)

    in_specs = [pl.BlockSpec(a_block, a_map), pl.BlockSpec(b_block, b_map)]
    args = [a, b]
    if dep is not None:
        in_specs.append(_dep_spec(len(grid)))
        args.append(dep)
    return pl.pallas_call(
        body, name=name, grid=grid, in_specs=in_specs,
        out_specs=pl.BlockSpec(o_block, o_map),
        out_shape=jax.ShapeDtypeStruct(out_shape, BF16),
        compiler_params=_params(("parallel",) * len(grid)),
    )(*args)


def mm_tn_plain(a, b, name, *, tm=512, tn=1024, b_scale=None):
    T, M = a.shape
    N = b.shape[1]
    tm = _tile(M, tm)
    tn = _tile(N, tn)
    return mm_tn(a, b, name, grid=(M // tm, N // tn),
                 a_block=(T, tm), a_map=lambda m, n: (0, m),
                 b_block=(T, tn), b_map=lambda m, n: (0, n),
                 o_block=(tm, tn), o_map=lambda m, n: (m, n),
                 out_shape=(M, N), b_scale=b_scale)


def ffn_up(h, wgu, gi, ui, nper, name):
    T, D = h.shape
    F8 = wgu.shape[0] // (N_DEV * nper)
    tm = _tile(T, 512)
    nt = T // tm

    def body(h_ref, wg_ref, wu_ref, ga_ref, gb_ref, s_ref):
        hv = h_ref[...]
        a = _dot(hv, wg_ref[...], 1, 1)
        b = _dot(hv, wu_ref[...], 1, 1)
        sg = _sigmoid(a)
        silu = a * sg
        ga_ref[...] = (b * (sg * (1.0 + a * (1.0 - sg)))).astype(BF16)
        gb_ref[...] = silu.astype(BF16)
        s_ref[...] = (silu * b).astype(BF16)

    blk = pl.BlockSpec((tm, F8), lambda j, i: (j * nt + i, 0))
    shp = jax.ShapeDtypeStruct((N_DEV * T, F8), BF16)
    return pl.pallas_call(
        body, name=name, grid=(N_DEV, nt),
        in_specs=[pl.BlockSpec((tm, D), lambda j, i: (i, 0)),
                  pl.BlockSpec((F8, D), lambda j, i: (j * nper + gi, 0)),
                  pl.BlockSpec((F8, D), lambda j, i: (j * nper + ui, 0))],
        out_specs=[blk, blk, blk], out_shape=[shp, shp, shp],
        compiler_params=_params(("parallel", "parallel")),
    )(h, wgu, wgu)


def ffn_gate(h, wg, name):
    T, D = h.shape
    F8 = wg.shape[0] // N_DEV
    tm = _tile(T, 512)
    nt = T // tm

    def body(h_ref, wg_ref, a_ref):
        a_ref[...] = _dot(h_ref[...], wg_ref[...], 1, 1).astype(BF16)

    return pl.pallas_call(
        body, name=name, grid=(N_DEV, nt),
        in_specs=[pl.BlockSpec((tm, D), lambda j, i: (i, 0)), pl.BlockSpec((F8, D), lambda j, i: (j, 0))],
        out_specs=pl.BlockSpec((tm, F8), lambda j, i: (j * nt + i, 0)),
        out_shape=jax.ShapeDtypeStruct((N_DEV * T, F8), BF16),
        compiler_params=_params(("parallel", "parallel")),
    )(h, wg)


def ffn_up_gated(h, wu, a, name):
    T, D = h.shape
    F8 = wu.shape[0] // N_DEV
    tm = _tile(T, 512)
    nt = T // tm

    def body(h_ref, wu_ref, a_ref, ga_ref, gb_ref, s_ref):
        av = a_ref[...].astype(F32)
        b = _dot(h_ref[...], wu_ref[...], 1, 1)
        sg = _sigmoid(av)
        silu = av * sg
        ga_ref[...] = (b * (sg * (1.0 + av * (1.0 - sg)))).astype(BF16)
        gb_ref[...] = silu.astype(BF16)
        s_ref[...] = (silu * b).astype(BF16)

    blk = pl.BlockSpec((tm, F8), lambda j, i: (j * nt + i, 0))
    shp = jax.ShapeDtypeStruct((N_DEV * T, F8), BF16)
    return pl.pallas_call(
        body, name=name, grid=(N_DEV, nt),
        in_specs=[pl.BlockSpec((tm, D), lambda j, i: (i, 0)), pl.BlockSpec((F8, D), lambda j, i: (j, 0)), blk],
        out_specs=[blk, blk, blk], out_shape=[shp, shp, shp],
        compiler_params=_params(("parallel", "parallel")),
    )(h, wu, a)


def ffn_down(s, wd, di, nper, x, g_next, name):
    T, D = x.shape
    F8 = s.shape[1]
    tm = _tile(T, 512)
    nt = T // tm

    def body(s_ref, w_ref, x_ref, g_ref, o_ref, h_ref, acc_ref):
        j = pl.program_id(1)
        part = _dot(s_ref[...], w_ref[...], 1, 0)

        @pl.when(j == 0)
        def _():
            acc_ref[...] = part

        @pl.when(j > 0)
        def _():
            acc_ref[...] += part

        @pl.when(j == N_DEV - 1)
        def _():
            xv = x_ref[...] + 0.5 * acc_ref[...]
            o_ref[...] = xv
            r = lax.rsqrt(jnp.mean(xv * xv, axis=-1, keepdims=True) + RMS_EPS)
            h_ref[...] = (xv * r * g_ref[...]).astype(BF16)

    row = pl.BlockSpec((tm, D), lambda i, j: (i, 0))
    return pl.pallas_call(
        body, name=name, grid=(nt, N_DEV),
        in_specs=[pl.BlockSpec((tm, F8), lambda i, j: (j * nt + i, 0)),
                  pl.BlockSpec((F8, D), lambda i, j: (j * nper + di, 0)),
                  row, pl.BlockSpec((1, D), lambda i, j: (0, 0))],
        out_specs=[row, row],
        out_shape=[jax.ShapeDtypeStruct((T, D), F32), jax.ShapeDtypeStruct((T, D), BF16)],
        scratch_shapes=[pltpu.VMEM((tm, D), F32)],
        compiler_params=_params(("parallel", "arbitrary")),
    )(s, wd, x, g_next)


def ffn_bwd_act(dxh, wd, di, nper_d, a, b, name, dep=None):
    T, D = dxh.shape
    F8 = a.shape[1]
    tm = _tile(T, 1024)
    nt = T // tm

    def body(dx_ref, w_ref, a_ref, b_ref, *rest):
        da_ref, db_ref = rest[-2], rest[-1]
        ds = _dot(dx_ref[...], w_ref[...], 1, 1)
        da_ref[...] = (ds * a_ref[...].astype(F32)).astype(BF16)
        db_ref[...] = (ds * b_ref[...].astype(F32)).astype(BF16)

    blk = pl.BlockSpec((tm, F8), lambda j, i: (j * nt + i, 0))
    shp = jax.ShapeDtypeStruct((N_DEV * T, F8), BF16)
    in_specs = [pl.BlockSpec((tm, D), lambda j, i: (i, 0)),
                pl.BlockSpec((F8, D), lambda j, i: (j * nper_d + di, 0)), blk, blk]
    args = [dxh, wd, a, b]
    if dep is not None:
        in_specs.append(_dep_spec(2))
        args.append(dep)
    return pl.pallas_call(
        body, name=name, grid=(N_DEV, nt), in_specs=in_specs,
        out_specs=[blk, blk], out_shape=[shp, shp],
        compiler_params=_params(("parallel", "parallel")),
    )(*args)


def ffn_bwd_dh(da, db, wg, wu, gi, ui, nper, D, name, dep=None):
    F8 = da.shape[1]
    T = da.shape[0] // N_DEV
    tm = _tile(T, 512)
    nt = T // tm

    def body(da_ref, db_ref, wg_ref, wu_ref, *rest):
        o_ref, acc_ref = rest[-2], rest[-1]
        j = pl.program_id(1)
        part = _dot(da_ref[...], wg_ref[...], 1, 0) + _dot(db_ref[...], wu_ref[...], 1, 0)

        @pl.when(j == 0)
        def _():
            acc_ref[...] = part

        @pl.when(j > 0)
        def _():
            acc_ref[...] += part

        @pl.when(j == N_DEV - 1)
        def _():
            o_ref[...] = acc_ref[...]

    blk = pl.BlockSpec((tm, F8), lambda i, j: (j * nt + i, 0))
    in_specs = [blk, blk,
                pl.BlockSpec((F8, D), lambda i, j: (j * nper + gi, 0)),
                pl.BlockSpec((F8, D), lambda i, j: (j * nper + ui, 0))]
    args = [da, db, wg, wu]
    if dep is not None:
        in_specs.append(_dep_spec(2))
        args.append(dep)
    return pl.pallas_call(
        body, name=name, grid=(nt, N_DEV), in_specs=in_specs,
        out_specs=pl.BlockSpec((tm, D), lambda i, j: (i, 0)),
        out_shape=jax.ShapeDtypeStruct((T, D), F32),
        scratch_shapes=[pltpu.VMEM((tm, D), F32)],
        compiler_params=_params(("parallel", "arbitrary")),
    )(*args)


def ffn_dw_side(act, other, c, mine, name, add=None, dep=None):
    F8 = act.shape[1]
    T, D = other.shape
    tm = _tile(D, 1024)

    def body(c_ref, a_ref, b_ref, *rest):
        acc = _dot(a_ref[...], b_ref[...], 0, 0)
        if add is not None:
            acc = acc + rest[0][...].astype(F32)
        rest[-1][...] = acc.astype(BF16)

    def shard(q, cr):
        return 2 * q + (cr[0] if mine else 1 - cr[0])

    in_specs = [pl.BlockSpec((T, F8), lambda q, m, cr: (shard(q, cr), 0)),
                pl.BlockSpec((T, tm), lambda q, m, cr: (0, m))]
    args = [act, other]
    if add is not None:
        in_specs.append(pl.BlockSpec((F8, tm), lambda q, m, cr: (q, m)))
        args.append(add)
    if dep is not None:
        in_specs.append(pl.BlockSpec((8, LANES), lambda q, m, cr: (0, 0)))
        args.append(dep)
    grid_spec = pltpu.PrefetchScalarGridSpec(
        num_scalar_prefetch=1, grid=(N_DEV // 2, D // tm), in_specs=in_specs,
        out_specs=pl.BlockSpec((F8, tm), lambda q, m, cr: (q, m)))
    return pl.pallas_call(
        body, name=name, grid_spec=grid_spec,
        out_shape=jax.ShapeDtypeStruct((N_DEV // 2 * F8, D), BF16),
        compiler_params=_params(("parallel", "parallel")),
    )(c.reshape(1).astype(jnp.int32), *args)


def _t5_bucket_np(dist):
    max_exact = N_REL_BUCKETS // 2
    d = np.maximum(dist, 1).astype(np.float64)
    large = max_exact + (np.log(d / max_exact) / math.log(REL_MAX_DISTANCE / max_exact)
                         * (N_REL_BUCKETS - max_exact)).astype(np.int64)
    large32 = max_exact + (np.log(d.astype(np.float32) / np.float32(max_exact))
                           / np.float32(math.log(REL_MAX_DISTANCE / max_exact))
                           * np.float32(N_REL_BUCKETS - max_exact)).astype(np.int64)
    assert np.array_equal(large, large32)
    large = np.minimum(large, N_REL_BUCKETS - 1)
    return np.where(dist < max_exact, dist, large)


def _distance_tables(T, tq):
    dist = np.arange(T)
    mult = np.zeros(T, np.int64)
    for window, dilation in DILATED_PATTERNS:
        mult += ((dist % dilation == 0) & (dist // dilation <= window // dilation)).astype(np.int64)
    logm = np.where(mult > 0, np.log(np.maximum(mult, 1)), NEG_INF).astype(np.float32)
    bucket = _t5_bucket_np(dist).astype(np.int32)
    nkb = T // tq
    k = np.arange(nkb)[:, None, None]
    r = np.arange(tq)[None, :, None]
    c = np.arange(tq)[None, None, :]
    delta = k * tq + r - c
    return bucket, logm, delta


def _tile_buckets(T, tq):
    bucket, logm, delta = _distance_tables(T, tq)
    safe = np.maximum(delta, 0)
    bidx = np.where(delta >= 0, bucket[safe], -1).astype(np.int32)
    logm_t = np.where(delta >= 0, logm[safe], NEG_INF).astype(np.float32)
    present = [sorted(set(np.unique(bidx[k]).tolist()) - {-1}) for k in range(T // tq)]
    return bidx, logm_t, present


def bias_tiles(rel_table, T, tq):
    bidx, logm_t, present = _tile_buckets(T, tq)
    nkb = T // tq

    def body(tab_ref, b_ref, lm_ref, o_ref):
        slot = pl.program_id(0)

        @pl.when(slot == 0)
        def _():
            o_ref[...] = jnp.where(b_ref[...] >= 0, 0.0, NEG_INF)

        @pl.when(slot > 0)
        def _():
            for k in range(nkb):
                bi = b_ref[k]
                acc = lm_ref[k]
                for b in present[k]:
                    acc = acc + jnp.where(bi == b, tab_ref[b, slot - 1], 0.0)
                o_ref[k] = acc

    full = pl.BlockSpec((nkb, tq, tq), lambda s: (0, 0, 0))
    return pl.pallas_call(
        body, name="bias_tiles", grid=(1 + N_DIL,),
        in_specs=[pl.BlockSpec(memory_space=pltpu.SMEM), full, full],
        out_specs=pl.BlockSpec((None, nkb, tq, tq), lambda s: (s, 0, 0, 0)),
        out_shape=jax.ShapeDtypeStruct((1 + N_DIL, nkb, tq, tq), F32),
        compiler_params=_params(("parallel",)),
    )(rel_table, jnp.asarray(bidx), jnp.asarray(logm_t))


def fox_gate_fwd(uf, bf, name):
    T = uf.shape[0]
    tb = _tile(T, 512)

    def body(u_ref, b_ref, c_ref, ct_ref):
        lane = lax.broadcasted_iota(jnp.int32, (1, LANES), 1)
        tri = (lax.broadcasted_iota(jnp.int32, (tb, tb), 0)
               >= lax.broadcasted_iota(jnp.int32, (tb, tb), 1)).astype(F32)
        carry = jnp.zeros((1, LANES), F32)
        for blk in range(T // tb):
            z = u_ref[pl.ds(blk * tb, tb), :] + b_ref[...]
            lf = jnp.minimum(z, 0.0) - jnp.log1p(jnp.exp(-jnp.abs(z)))
            lf = jnp.where(lane < N_FOX, lf, 0.0)
            cb = _dot(tri, lf, 1, 0, precision=lax.Precision.HIGHEST) + carry
            c_ref[pl.ds(blk * tb, tb), :] = cb
            ct_ref[:, pl.ds(blk * tb, tb)] = cb.T
            carry = cb[tb - 1:tb, :]

    return pl.pallas_call(
        body, name=name,
        out_shape=[jax.ShapeDtypeStruct((T, LANES), F32), jax.ShapeDtypeStruct((LANES, T), F32)],
        compiler_params=_params(None),
    )(uf, bf)


def fox_gate_bwd(dct, uf, bf, name):
    T = uf.shape[0]
    tb = _tile(T, 512)

    def body(d_ref, u_ref, b_ref, du_ref, db_ref):
        lane = lax.broadcasted_iota(jnp.int32, (1, LANES), 1)
        triu = (lax.broadcasted_iota(jnp.int32, (tb, tb), 0)
                <= lax.broadcasted_iota(jnp.int32, (tb, tb), 1)).astype(F32)
        carry = jnp.zeros((1, LANES), F32)
        dbv = jnp.zeros((1, LANES), F32)
        for blk in reversed(range(T // tb)):
            dc = d_ref[:, pl.ds(blk * tb, tb)].T
            dlf = _dot(triu, dc, 1, 0, precision=lax.Precision.HIGHEST) + carry
            carry = dlf[0:1, :]
            z = u_ref[pl.ds(blk * tb, tb), :] + b_ref[...]
            dz = jnp.where(lane < N_FOX, dlf * (1.0 - _sigmoid(z)), 0.0)
            du_ref[pl.ds(blk * tb, tb), :] = dz
            dbv = dbv + jnp.sum(dz, axis=0, keepdims=True)
        db_ref[...] = dbv

    return pl.pallas_call(
        body, name=name,
        out_shape=[jax.ShapeDtypeStruct((T, LANES), F32), jax.ShapeDtypeStruct((1, LANES), F32)],
        compiler_params=_params(None),
    )(dct, uf, bf)


def _bias_slot(h):
    return jnp.maximum(h - (N_FOX - 1), 0)


def _scores(q_ref, k_ref, c_ref, ct_ref, tb_ref, h, i, tq, fox):
    scale = HEAD_DIM ** -0.5
    n = (i + 1) * tq
    rows = pl.ds(i * tq, tq)
    s = _dot(q_ref[rows, :], k_ref[pl.ds(0, n), :], 1, 1) * scale
    if not fox:
        return s + jnp.concatenate([tb_ref[i - jb] for jb in range(i + 1)], axis=1)
    lane = lax.broadcasted_iota(jnp.int32, (1, LANES), 1)
    c_col = jnp.sum(jnp.where(lane == h, c_ref[rows, :], 0.0), axis=1, keepdims=True)
    c_row = ct_ref[pl.ds(h, 1), pl.ds(0, n)]
    s = s + (c_col - c_row)
    if i == 0:
        return s + tb_ref[0]
    return jnp.concatenate([s[:, :i * tq], s[:, i * tq:] + tb_ref[0]], axis=1)


def _attn_specs(T, tq):
    nkb = T // tq
    return [
        pl.BlockSpec((T, HEAD_DIM), lambda h: (0, h)),
        pl.BlockSpec((T, HEAD_DIM), lambda h: (0, N_HEADS + h)),
        pl.BlockSpec((T, HEAD_DIM), lambda h: (0, 2 * N_HEADS + h)),
        pl.BlockSpec((T, LANES), lambda h: (0, 0)),
        pl.BlockSpec((LANES, T), lambda h: (0, 0)),
        pl.BlockSpec((None, nkb, tq, tq), lambda h: (_bias_slot(h), 0, 0, 0)),
    ]


def attention_fwd(qkv, c, ct, tiles, name):
    T = qkv.shape[0]
    tq = tiles.shape[2]

    def body(q_ref, k_ref, v_ref, c_ref, ct_ref, tb_ref, o_ref, lse_ref):
        h = pl.program_id(0)
        lane = lax.broadcasted_iota(jnp.int32, (1, LANES), 1)

        @pl.when(h == 0)
        def _():
            lse_ref[...] = jnp.zeros_like(lse_ref)

        def head(fox):
            for i in range(T // tq):
                rows = pl.ds(i * tq, tq)
                s = _scores(q_ref, k_ref, c_ref, ct_ref, tb_ref, h, i, tq, fox)
                m = jnp.max(s, axis=1, keepdims=True)
                p = jnp.exp(s - m)
                l = jnp.sum(p, axis=1, keepdims=True)
                o = _dot(p.astype(BF16), v_ref[pl.ds(0, (i + 1) * tq), :], 1, 0) * (1.0 / l)
                o_ref[rows, :] = o.astype(BF16)
                lse_ref[rows, :] = jnp.where(lane == h, m + jnp.log(l), lse_ref[rows, :])

        pl.when(h < N_FOX)(functools.partial(head, True))
        pl.when(h >= N_FOX)(functools.partial(head, False))

    return pl.pallas_call(
        body, name=name, grid=(N_HEADS,),
        in_specs=_attn_specs(T, tq),
        out_specs=[pl.BlockSpec((T, HEAD_DIM), lambda h: (0, h)), pl.BlockSpec((T, LANES), lambda h: (0, 0))],
        out_shape=[jax.ShapeDtypeStruct((T, N_HEADS * HEAD_DIM), BF16), jax.ShapeDtypeStruct((T, LANES), F32)],
        compiler_params=_params(("arbitrary",)),
    )(qkv, qkv, qkv, c, ct, tiles)


def attention_bwd(qkv, c, ct, tiles, lse, o, do, name):
    T = qkv.shape[0]
    tq = tiles.shape[2]
    nkb = T // tq
    scale = HEAD_DIM ** -0.5

    def body(q_ref, k_ref, v_ref, c_ref, ct_ref, tb_ref, lse_ref, o_ref, do_ref,
             dq_ref, dk_ref, dv_ref, dct_ref, dtb_ref, dk_acc, dv_acc):
        h = pl.program_id(0)
        lane = lax.broadcasted_iota(jnp.int32, (1, LANES), 1)
        dk_acc[...] = jnp.zeros_like(dk_acc)
        dv_acc[...] = jnp.zeros_like(dv_acc)
        dct_ref[...] = jnp.zeros_like(dct_ref)
        dtb_ref[...] = jnp.zeros_like(dtb_ref)

        def head(fox):
            for i in range(nkb):
                rows, keys = pl.ds(i * tq, tq), pl.ds(0, (i + 1) * tq)
                s = _scores(q_ref, k_ref, c_ref, ct_ref, tb_ref, h, i, tq, fox)
                lse_col = jnp.sum(jnp.where(lane == h, lse_ref[rows, :], 0.0), axis=1, keepdims=True)
                p = jnp.exp(s - lse_col)
                p_b = p.astype(BF16)
                dov = do_ref[rows, :]
                dp = _dot(dov, v_ref[keys, :], 1, 1)
                if fox:
                    delta = jnp.sum(p * dp, axis=1, keepdims=True)
                else:
                    delta = jnp.sum(dov.astype(F32) * o_ref[rows, :].astype(F32), axis=1, keepdims=True)
                ds = p * (dp - delta)
                ds_b = ds.astype(BF16)
                dq_ref[rows, :] = (_dot(ds_b, k_ref[keys, :], 1, 0) * scale).astype(BF16)
                dk_acc[:, keys] += _dot(q_ref[rows, :], ds_b, 0, 0) * scale
                dv_acc[:, keys] += _dot(dov, p_b, 0, 0)
                if fox:
                    dct_ref[:, keys] += -jnp.sum(ds, axis=0, keepdims=True)
                else:
                    for jb in range(i + 1):
                        dtb_ref[i - jb] += ds[:, jb * tq:(jb + 1) * tq]

        pl.when(h < N_FOX)(functools.partial(head, True))
        pl.when(h >= N_FOX)(functools.partial(head, False))
        dk_ref[...] = dk_acc[...].T.astype(BF16)
        dv_ref[...] = dv_acc[...].T.astype(BF16)

    head_cols = jax.ShapeDtypeStruct((T, N_HEADS * HEAD_DIM), BF16)
    col = pl.BlockSpec((T, HEAD_DIM), lambda h: (0, h))
    return pl.pallas_call(
        body, name=name, grid=(N_HEADS,),
        in_specs=_attn_specs(T, tq) + [pl.BlockSpec((T, LANES), lambda h: (0, 0)), col, col],
        out_specs=[col, col, col,
                   pl.BlockSpec((None, 1, T), lambda h: (h, 0, 0)),
                   pl.BlockSpec((None, nkb, tq, tq), lambda h: (_bias_slot(h), 0, 0, 0))],
        out_shape=[head_cols, head_cols, head_cols,
                   jax.ShapeDtypeStruct((N_HEADS, 1, T), F32),
                   jax.ShapeDtypeStruct((1 + N_DIL, nkb, tq, tq), F32)],
        scratch_shapes=[pltpu.VMEM((HEAD_DIM, T), F32), pltpu.VMEM((HEAD_DIM, T), F32)],
        compiler_params=_params(("arbitrary",)),
    )(qkv, qkv, qkv, c, ct, tiles, lse, o, do)


def rel_table_grad(dtiles, T, name):
    tq = dtiles.shape[2]
    nkb = T // tq
    bidx, _, present = _tile_buckets(T, tq)

    def body(d_ref, b_ref, o_ref):
        lane = lax.broadcasted_iota(jnp.int32, (1, LANES), 1)
        row = jnp.zeros((1, LANES), F32)
        for k in range(nkb):
            d = d_ref[k]
            bi = b_ref[k]
            for b in present[k]:
                v = jnp.sum(jnp.sum(jnp.where(bi == b, d, 0.0), axis=0, keepdims=True),
                            axis=1, keepdims=True)
                row = row + jnp.where(lane == b, v, 0.0)
        o_ref[...] = row

    return pl.pallas_call(
        body, name=name, grid=(N_DIL,),
        in_specs=[pl.BlockSpec((None, nkb, tq, tq), lambda h: (h + 1, 0, 0, 0)),
                  pl.BlockSpec((nkb, tq, tq), lambda h: (0, 0, 0))],
        out_specs=pl.BlockSpec((None, 1, LANES), lambda h: (h, 0, 0)),
        out_shape=jax.ShapeDtypeStruct((N_DIL, 1, LANES), F32),
        compiler_params=_params(("parallel",)),
    )(dtiles, jnp.asarray(bidx))


def _peer_list():
    x, y, c = lax.axis_index("x"), lax.axis_index("y"), lax.axis_index("c")
    me = 4 * x + 2 * y + c
    peers = []
    for fx in (0, 1):
        for fy in (0, 1):
            for fc in (0, 1):
                if fx or fy or fc:
                    px = 1 - x if fx else x
                    py = 1 - y if fy else y
                    pc = 1 - c if fc else c
                    peers.append(((px, py, pc), 4 * px + 2 * py + pc))
    return me, peers


_HBM = pl.BlockSpec(memory_space=pltpu.HBM)
_SEM = pl.BlockSpec(memory_space=pltpu.SEMAPHORE)
_EFFECT = pltpu.SideEffectType.DATAFLOW_SIDE_EFFECTING
N_PEERS = N_DEV - 1


def _in_hbm(a):
    return pltpu.with_memory_space_constraint(a, pltpu.HBM)


def _exchange_copies(srcs, lands, send_sems, recv_sems, blockwise):
    me, peers = _peer_list()
    sends, recvs = [], []
    for a in range(len(srcs)):
        for k, (dev, idx) in enumerate(peers):
            src = srcs[a].at[idx] if blockwise[a] else srcs[a]
            sends.append(pltpu.make_async_remote_copy(
                src_ref=src, dst_ref=lands[a].at[me], send_sem=send_sems[a].at[k],
                recv_sem=recv_sems[a].at[k], device_id=dev, device_id_type=MESH))
            recvs.append(pltpu.make_async_remote_copy(
                src_ref=src, dst_ref=lands[a].at[idx], send_sem=send_sems[a].at[k],
                recv_sem=recv_sems[a].at[k], device_id=dev, device_id_type=MESH))
    return sends, recvs


def exchange_start(srcs, lands, blockwise, name):
    n = len(srcs)

    def body(*refs):
        src_in, land_in = refs[:n], refs[n:2 * n]
        send_sems, recv_sems = refs[2 * n:3 * n], refs[3 * n:4 * n]
        token = refs[6 * n]
        sends, _ = _exchange_copies(src_in, land_in, send_sems, recv_sems, blockwise)
        for cp in sends:
            cp.start()
        token[...] = jnp.zeros_like(token)

    out_shape = ([pltpu.SemaphoreType.DMA((N_PEERS,))] * (2 * n)
                 + [pltpu.HBM(s.shape, s.dtype) for s in srcs]
                 + [pltpu.HBM(l.shape, l.dtype) for l in lands]
                 + [jax.ShapeDtypeStruct((8, LANES), F32)])
    aliases = {a: 2 * n + a for a in range(2 * n)}
    outs = pl.pallas_call(
        body, name=name, out_shape=out_shape,
        in_specs=[_HBM] * (2 * n),
        out_specs=[_SEM] * (2 * n) + [_HBM] * (2 * n) + [pl.BlockSpec(memory_space=pltpu.VMEM)],
        input_output_aliases=aliases,
        compiler_params=pltpu.CompilerParams(has_side_effects=_EFFECT),
    )(*[_in_hbm(s) for s in srcs], *[_in_hbm(l) for l in lands])
    return (outs[:n], outs[n:2 * n], outs[2 * n:3 * n], outs[3 * n:4 * n], outs[4 * n])


def exchange_wait(send_sems, recv_sems, srcs, lands, blockwise, after, name):
    n = len(srcs)

    def body(*refs):
        src_in, land_in = refs[:n], refs[n:2 * n]
        ss, rs = refs[2 * n:3 * n], refs[3 * n:4 * n]
        sends, recvs = _exchange_copies(src_in, land_in, ss, rs, blockwise)
        for cp in sends:
            cp.wait_send()
        for cp in recvs:
            cp.wait_recv()

    outs = pl.pallas_call(
        body, name=name,
        out_shape=[pltpu.HBM(s.shape, s.dtype) for s in srcs] + [pltpu.HBM(l.shape, l.dtype) for l in lands],
        in_specs=[_HBM] * (2 * n) + [_SEM] * (2 * n) + [pl.BlockSpec(memory_space=pl.ANY)],
        out_specs=[_HBM] * (2 * n),
        input_output_aliases={a: a for a in range(2 * n)},
        compiler_params=pltpu.CompilerParams(has_side_effects=_EFFECT),
    )(*srcs, *lands, *send_sems, *recv_sems, after)
    return outs[n:]


def _landing(own_block, me, slots=N_DEV):
    empty = lax.empty((slots,) + own_block.shape, own_block.dtype)
    return lax.dynamic_update_slice(empty, own_block[None], (me,) + (0,) * own_block.ndim)


N_CHIPS = N_DEV // 2
_CHIP_FLIPS = ((1, 0), (0, 1), (1, 1))


def _xyc():
    return lax.axis_index("x"), lax.axis_index("y"), lax.axis_index("c")


def _other_chips(x, y):
    return [(1 - x if fx else x, 1 - y if fy else y) for fx, fy in _CHIP_FLIPS]


def _remote(src, dst, send_sem, recv_sem, dev):
    return pltpu.make_async_remote_copy(src_ref=src, dst_ref=dst, send_sem=send_sem, recv_sem=recv_sem,
                                        device_id=dev, device_id_type=MESH)


def comm_call(name, bufs, sems_in, sems_out, fn, after=None, want_token=False):
    nb, ni, no = len(bufs), len(sems_in), len(sems_out)
    afters = [] if after is None else (list(after) if isinstance(after, (list, tuple)) else [after])
    na = len(afters)

    def body(*refs):
        buf_refs = refs[:nb]
        sin = refs[nb:nb + ni]
        sout = refs[nb + ni + na:nb + ni + na + no]
        fn(buf_refs, sin, sout)
        if want_token:
            tok = refs[nb + ni + na + no + nb]
            tok[...] = jnp.zeros_like(tok)

    out_shape = list(sems_out) + [pltpu.HBM(b.shape, b.dtype) for b in bufs]
    out_specs = [_SEM] * no + [_HBM] * nb
    if want_token:
        out_shape.append(jax.ShapeDtypeStruct((8, LANES), F32))
        out_specs.append(pl.BlockSpec(memory_space=pltpu.VMEM))
    args = [_in_hbm(b) for b in bufs] + list(sems_in) + afters
    outs = pl.pallas_call(
        body, name=name, out_shape=out_shape,
        in_specs=[_HBM] * nb + [_SEM] * ni + [pl.BlockSpec(memory_space=pl.ANY)] * na,
        out_specs=out_specs, input_output_aliases={a: no + a for a in range(nb)},
        compiler_params=pltpu.CompilerParams(has_side_effects=_EFFECT),
    )(*args)
    return list(outs[:no]), list(outs[no:no + nb]), (outs[no + nb] if want_token else None)


def _dma_sems(*sizes):
    return [pltpu.SemaphoreType.DMA((s,)) for s in sizes]


def gather_start(srcs, lands, name, after=None):
    n = len(srcs)

    def fn(bufs, sin, sout):
        x, y, c = _xyc()
        me = 4 * x + 2 * y + c
        for a in range(n):
            src, land = bufs[a], bufs[n + a]
            send, recv_d, recv_i = sout[3 * a:3 * a + 3]
            _remote(src, land.at[me], send.at[0], recv_d.at[0], (x, y, 1 - c)).start()
            for k, (px, py) in enumerate(_other_chips(x, y)):
                _remote(src, land.at[me], send.at[1 + k], recv_i.at[k], (px, py, c)).start()

    return comm_call(name, list(srcs) + list(lands), [], _dma_sems(4, 1, 3) * n, fn, after=after, want_token=True)


def gather_forward(srcs, lands, recv_i, after, name):
    n = len(srcs)

    def fn(bufs, sin, sout):
        x, y, c = _xyc()
        for a in range(n):
            src, land = bufs[a], bufs[n + a]
            f_send, f_recv = sout[2 * a:2 * a + 2]
            for k, (px, py) in enumerate(_other_chips(x, y)):
                blk = land.at[4 * px + 2 * py + c]
                _remote(src, blk, f_send.at[k], sin[a].at[k], (px, py, c)).wait_recv()
                _remote(blk, blk, f_send.at[k], f_recv.at[k], (x, y, 1 - c)).start()

    sems, bufs, _ = comm_call(name, list(srcs) + list(lands), recv_i, _dma_sems(3, 3) * n, fn, after=after)
    return sems, bufs


def gather_wait(srcs, lands, send, recv_d, f_send, f_recv, after, name):
    n = len(srcs)

    def fn(bufs, sin, sout):
        x, y, c = _xyc()
        sib = (x, y, 1 - c)
        for a in range(n):
            src, land = bufs[a], bufs[n + a]
            s_send, s_recv_d, s_fsend, s_frecv = sin[4 * a:4 * a + 4]
            sib_blk = land.at[4 * x + 2 * y + 1 - c]
            for k in range(4):
                _remote(src, sib_blk, s_send.at[k], s_recv_d.at[0], sib).wait_send()
            _remote(src, sib_blk, s_send.at[0], s_recv_d.at[0], sib).wait_recv()
            for k, (px, py) in enumerate(_other_chips(x, y)):
                cp = _remote(src, land.at[4 * px + 2 * py + 1 - c], s_fsend.at[k], s_frecv.at[k], sib)
                cp.wait_send()
                cp.wait_recv()

    sems_in = []
    for a in range(n):
        sems_in += [send[a], recv_d[a], f_send[a], f_recv[a]]
    _, bufs, _ = comm_call(name, list(srcs) + list(lands), sems_in, [], fn, after=after)
    return bufs[n:]


def scatter_pair_start(src4s, lands, name, after=None, whole=False):
    n = len(src4s)

    def fn(bufs, sin, sout):
        x, y, c = _xyc()
        for a in range(n):
            src = bufs[a] if whole else bufs[a].at[:, 1 - c]
            _remote(src, bufs[n + a], sout[2 * a].at[0], sout[2 * a + 1].at[0], (x, y, 1 - c)).start()

    return comm_call(name, list(src4s) + list(lands), [], _dma_sems(1, 1) * n, fn, after=after, want_token=True)


def scatter_pair_wait(src4s, lands, sems, after, name, whole=False):
    n = len(src4s)

    def fn(bufs, sin, sout):
        x, y, c = _xyc()
        for a in range(n):
            src = bufs[a] if whole else bufs[a].at[:, 1 - c]
            cp = _remote(src, bufs[n + a], sin[2 * a].at[0], sin[2 * a + 1].at[0], (x, y, 1 - c))
            cp.wait_send()
            cp.wait_recv()

    _, bufs, _ = comm_call(name, list(src4s) + list(lands), sems, [], fn, after=after)
    return bufs[:n], bufs[n:]


def _row_tile(R):
    for cand in range(256, 15, -16):
        if R % cand == 0 and R // cand >= 4:
            return cand
    return R


def chip_sum(src4, land, c, name):
    _, _, R, C = src4.shape
    tr = R

    def body(c_ref, a_ref, b_ref, o_ref):
        o_ref[...] = (a_ref[...].astype(F32) + b_ref[...].astype(F32)).astype(BF16)

    grid_spec = pltpu.PrefetchScalarGridSpec(
        num_scalar_prefetch=1, grid=(N_CHIPS, R // tr),
        in_specs=[pl.BlockSpec((None, None, tr, C), lambda q, i, cr: (q, cr[0], i, 0)),
                  pl.BlockSpec((None, tr, C), lambda q, i, cr: (q, i, 0))],
        out_specs=pl.BlockSpec((None, tr, C), lambda q, i, cr: (q, i, 0)))
    return pl.pallas_call(
        body, name=name, grid_spec=grid_spec,
        out_shape=jax.ShapeDtypeStruct((N_CHIPS, R, C), BF16),
        compiler_params=_params(("parallel", "parallel")),
    )(c.reshape(1).astype(jnp.int32), src4, land)


def scatter_chip_start(sums, lands, name):
    n = len(sums)

    def fn(bufs, sin, sout):
        x, y, c = _xyc()
        for a in range(n):
            for k, (px, py) in enumerate(_other_chips(x, y)):
                _remote(bufs[a].at[2 * px + py], bufs[n + a].at[2 * x + y], sout[2 * a].at[k], sout[2 * a + 1].at[k],
                        (px, py, c)).start()

    return comm_call(name, list(sums) + list(lands), [], _dma_sems(3, 3) * n, fn, want_token=True)


def scatter_chip_wait(sums, lands, sems, after, name):
    n = len(sums)

    def fn(bufs, sin, sout):
        x, y, c = _xyc()
        for a in range(n):
            for k, (px, py) in enumerate(_other_chips(x, y)):
                cp = _remote(bufs[a].at[2 * px + py], bufs[n + a].at[2 * px + py], sin[2 * a].at[k],
                             sin[2 * a + 1].at[k], (px, py, c))
                cp.wait_send()
                cp.wait_recv()

    _, bufs, _ = comm_call(name, list(sums) + list(lands), sems, [], fn, after=after)
    return bufs[:n], bufs[n:]


def _adamw_math(w, g, m, v):
    m = ADAM_B1 * m + (1.0 - ADAM_B1) * g
    v = ADAM_B2 * v + (1.0 - ADAM_B2) * (g * g)
    m_hat = m / (1.0 - ADAM_B1 ** ADAM_STEP)
    v_hat = v / (1.0 - ADAM_B2 ** ADAM_STEP)
    delta = -ADAM_LR * (m_hat / (jnp.sqrt(v_hat) + ADAM_EPS) + ADAM_WD * w)
    return delta, m, v


def _sum_partials(p_ref, own_ref, mine):
    own = own_ref[...].astype(F32)
    g = None
    for s in range(p_ref.shape[0]):
        term = jnp.where(mine == s, own, p_ref[s].astype(F32))
        g = term if g is None else g + term
    return g


def adamw_sharded(parts, sums, my_chip, w, m, v, name):
    R, C = w.shape
    S = parts.shape[0]
    tr = _row_tile(R)

    def body(mc_ref, p_ref, o_ref, w_ref, m_ref, v_ref, g_ref, d_ref, nm_ref, nv_ref):
        g = _sum_partials(p_ref, o_ref, mc_ref[0])
        delta, nm, nv = _adamw_math(w_ref[...], g, m_ref[...], v_ref[...])
        g_ref[...] = g
        d_ref[...] = delta
        nm_ref[...] = nm
        nv_ref[...] = nv

    row = pl.BlockSpec((tr, C), lambda i, mc: (i, 0))
    shp = jax.ShapeDtypeStruct((R, C), F32)
    grid_spec = pltpu.PrefetchScalarGridSpec(
        num_scalar_prefetch=1, grid=(R // tr,),
        in_specs=[pl.BlockSpec((S, tr, C), lambda i, mc: (0, i, 0)),
                  pl.BlockSpec((None, tr, C), lambda i, mc: (mc[0], i, 0)), row, row, row],
        out_specs=[row, row, row, row])
    return pl.pallas_call(
        body, name=name, grid_spec=grid_spec, out_shape=[shp, shp, shp, shp],
        compiler_params=_params(("parallel",)),
    )(my_chip.reshape(1).astype(jnp.int32), parts, sums, w, m, v)


def adamw_small(parts, w, m, v, name):
    R, C = w.shape

    def body(p_ref, w_ref, m_ref, v_ref, g_ref, d_ref, nm_ref, nv_ref):
        g = p_ref[0]
        for s in range(1, N_DEV):
            g = g + p_ref[s]
        delta, nm, nv = _adamw_math(w_ref[...], g, m_ref[...], v_ref[...])
        g_ref[...] = g
        d_ref[...] = delta
        nm_ref[...] = nm
        nv_ref[...] = nv

    shp = jax.ShapeDtypeStruct((R, C), F32)
    return pl.pallas_call(
        body, name=name, out_shape=[shp, shp, shp, shp], compiler_params=_params(None),
    )(parts, w, m, v)


_ROW_NORM_FFN1, _ROW_NORM_MIX, _ROW_NORM_FFN2, _ROW_NORM_PLE, _ROW_NORM_FINAL = 0, 1, 2, 3, 4
_ROW_B_F, _ROW_REL, _ROW_LOSS, _SMALL_ROWS = 5, 6, 7, 8


def _pack_small(D, norm_ffn1, norm_mix, norm_ffn2, norm_ple, norm_final, b_f, rel_table):
    def row(v):
        v = v.reshape(1, -1)
        return jnp.pad(v, ((0, 0), (0, D - v.shape[1])))
    return jnp.concatenate([row(norm_ffn1), row(norm_mix), row(norm_ffn2), row(norm_ple),
                            row(norm_final), row(b_f), row(rel_table),
                            jnp.zeros((1, D), F32)], axis=0)


def _unpack_small(a, shapes):
    return {"norm_ffn1": a[_ROW_NORM_FFN1].reshape(shapes["norm_ffn1"]),
            "norm_mix": a[_ROW_NORM_MIX].reshape(shapes["norm_mix"]),
            "b_f": a[_ROW_B_F, :N_FOX].reshape(shapes["b_f"]),
            "norm_ffn2": a[_ROW_NORM_FFN2].reshape(shapes["norm_ffn2"]),
            "norm_ple": a[_ROW_NORM_PLE].reshape(shapes["norm_ple"]),
            "rel_table": a[_ROW_REL, :N_REL_BUCKETS * N_DIL].reshape(shapes["rel_table"]),
            "norm_final": a[_ROW_NORM_FINAL].reshape(shapes["norm_final"])}


def local_step(x, p, tgt, g_ffn1, g_mix, g_ffn2, g_ple, g_final, b_f, rel_table,
               forward, weights, emit, emit2, dw_theirs, dw_mine, first_dep):
    T, D = x.shape
    P = p.shape[1]
    CW = D // N_DEV
    tq = _tile(T, 256)

    h1 = rms_fwd(x, g_ffn1, "rms_ffn1", dep=first_dep)
    tiles = bias_tiles(rel_table, T, tq)
    forward("ffn1_g", [tiles, h1])
    wg1, = weights("ffn1_g", h1)
    gate1 = ffn_gate(h1, wg1, "ffn1_gate")
    forward("ffn1_u", gate1)
    wu1, = weights("ffn1_u", gate1)
    a1, b1, s1 = ffn_up_gated(h1, wu1, gate1, "ffn1_up")
    forward("ffn1_d", s1)
    wd1, = weights("ffn1_d", s1)
    x1, h2 = ffn_down(s1, wd1, 0, 1, x, g_mix, "ffn1_down")

    forward("mix_in", h2)
    w3, wf = weights("mix_in", h2)
    qkv = mm_nt([(h2, w3)], "mix_qkv", tn=768, out_dtype=BF16)
    uf = mm_nt([(h2, wf)], "mix_forget", tn=LANES, out_dtype=F32)
    bfp = jnp.pad(b_f.reshape(1, N_FOX), ((0, 0), (0, LANES - N_FOX)))
    c, ct = fox_gate_fwd(uf, bfp, "fox_gate")
    cat, lse = attention_fwd(qkv, c, ct, tiles, "attention")
    forward("mix_out", cat)
    wo, = weights("mix_out", cat)
    x2 = mm_nn(cat, wo, "mix_out", tn=512, out_dtype=F32, res=x1)

    h3 = rms_fwd(x2, g_ffn2, "rms_ffn2")
    forward("ffn2_gu", h3)
    wgu2, = weights("ffn2_gu", h3)
    a2, b2, s2 = ffn_up(h3, wgu2, 0, 1, 2, "ffn2_up")
    forward("ffn2_d", s2)
    wd2, = weights("ffn2_d", s2)
    x3, h4 = ffn_down(s2, wd2, 0, 1, x2, g_ple, "ffn2_down")
    forward("ple", x3)

    wpg, wpp = weights("ple", h4)
    z = mm_nn(h4, wpg, "ple_gate", tn=512, out_dtype=F32)
    pp = mm_nn(p, wpp, "ple_proj", tn=CW, tm=T, out_dtype=F32, n_out=D,
               b_block=(P, CW), b_map=lambda n, i: (n, 0))
    loss_row, dx4, dg_final, dz, dpp = ple_loss(x3, z, pp, g_final, tgt, "ple_loss")

    grads = {}
    grads["w_ple_proj"] = mm_tn(p, dpp, "ple_proj_dw", grid=(N_DEV,),
                                a_block=(T, P), a_map=lambda n: (0, 0),
                                b_block=(T, CW), b_map=lambda n: (0, n),
                                o_block=(P, CW), o_map=lambda n: (n, 0),
                                out_shape=(N_DEV * P, CW))
    grads["w_ple_gate"] = mm_tn_plain(h4, dz, "ple_gate_dw")
    tok = emit("ple", grads)
    dh4 = mm_nt([(dz, wpg)], "ple_gate_dh", tn=1024, out_dtype=F32, dep=tok)
    tok = emit2("ple", dh4)
    dx3, dx3h, dg_ple = rms_bwd(dh4, x3, g_ple, dx4, "rms_ple_bwd", dep=tok)

    da2, db2 = ffn_bwd_act(dx3h, wd2, 0, 1, a2, b2, "ffn2_bwd_act")
    tok = dw_theirs("ffn2_w_down", s2, dx3h)
    tok = dw_theirs("ffn2_w_gate", da2, h3, dep=tok)
    tok = dw_theirs("ffn2_w_up", db2, h3, dep=tok)
    tok = dw_mine("ffn2_w_down", s2, dx3h, dep=tok)
    tok = dw_mine("ffn2_w_gate", da2, h3, dep=tok)
    tok = dw_mine("ffn2_w_up", db2, h3, dep=tok)
    dh3 = ffn_bwd_dh(da2, db2, wgu2, wgu2, 0, 1, 2, D, "ffn2_bwd_dh", dep=tok)
    dx2, _, dg_ffn2 = rms_bwd(dh3, x2, g_ffn2, dx3, "rms_ffn2_bwd", half=False)

    dcat = mm_nt([(dx2, wo)], "mix_out_dh", tn=1024, out_dtype=BF16)
    grads["w_o"] = mm_tn_plain(cat, dx2, "mix_out_dw")
    dq, dk, dv, dct, dtiles = attention_bwd(qkv, c, ct, tiles, lse, cat, dcat, "attention_bwd")
    dctp = jnp.pad(dct[:, 0, :], ((0, LANES - N_HEADS), (0, 0)))
    duf, dbf = fox_gate_bwd(dctp, uf, bfp, "fox_gate_bwd")
    drel = rel_table_grad(dtiles, T, "rel_table_grad")[:, 0, :N_REL_BUCKETS].T
    du3 = jnp.concatenate([dq, dk, dv], axis=1)
    grads["w3"] = mm_tn_plain(du3, h2, "mix_qkv_dw", tm=768)
    grads["wf"] = mm_tn_plain(duf, h2, "mix_forget_dw", tm=LANES)
    tok = emit("mix", grads)
    dh2 = mm_nn_sum([(du3, w3), (duf, wf)], "mix_in_dh", tn=1024, out_dtype=F32, dep=tok)
    tok = emit2("mix", dh2)
    dx1, dx1h, dg_mix = rms_bwd(dh2, x1, g_mix, dx2, "rms_mix_bwd", dep=tok)

    da1, db1 = ffn_bwd_act(dx1h, wd1, 0, 1, a1, b1, "ffn1_bwd_act")
    tok = dw_theirs("ffn1_w_down", s1, dx1h)
    tok = dw_theirs("ffn1_w_gate", da1, h1, dep=tok)
    tok = dw_mine("ffn1_w_down", s1, dx1h, dep=tok)
    tok = dw_theirs("ffn1_w_up", db1, h1, dep=tok)
    tok = dw_mine("ffn1_w_gate", da1, h1, dep=tok)
    tok = dw_mine("ffn1_w_up", db1, h1, dep=tok)
    dh1 = ffn_bwd_dh(da1, db1, wg1, wu1, 0, 0, 1, D, "ffn1_bwd_dh", dep=tok)
    dx0, _, dg_ffn1 = rms_bwd(dh1, x, g_ffn1, dx1, "rms_ffn1_bwd", half=False)

    small = _pack_small(D, dg_ffn1, dg_mix, dg_ffn2, dg_ple, dg_final, dbf[:, :N_FOX], drel)
    small = small.at[_ROW_LOSS, :LANES].set(loss_row[0])
    grads["small"] = small
    emit("small", grads)
    return dx0


def _split_w_in(w_in_t):
    df, dd = N_FOX * HEAD_DIM, N_DIL * HEAD_DIM
    o = np.cumsum([0, df, df, df, N_FOX, dd, dd, dd]).tolist()
    qa, ka, va, f, qb, kb, vb = [w_in_t[o[i]:o[i + 1]] for i in range(7)]
    return jnp.concatenate([qa, qb, ka, kb, va, vb], axis=0), f


def _join_w_in(d3, dfg):
    df, dd = N_FOX * HEAD_DIM, N_DIL * HEAD_DIM
    o = np.cumsum([0, df, dd, df, dd, df, dd]).tolist()
    qa, qb, ka, kb, va, vb = [d3[o[i]:o[i + 1]] for i in range(6)]
    return jnp.concatenate([qa, ka, va, dfg, qb, kb, vb], axis=0)


def rows_to_bf16(a3, name, dep=None):
    R, _, C = a3.shape
    tc = _tile(C, 512)

    def body(a_ref, *rest):
        rest[-1][...] = a_ref[...].astype(BF16)

    in_specs = [pl.BlockSpec((R, None, tc), lambda n: (0, 0, n))]
    args = [a3]
    if dep is not None:
        in_specs.append(_dep_spec(1))
        args.append(dep)
    return pl.pallas_call(
        body, name=name, grid=(C // tc,), in_specs=in_specs,
        out_specs=pl.BlockSpec((R, tc), lambda n: (0, n)),
        out_shape=jax.ShapeDtypeStruct((R, C), BF16),
        compiler_params=_params(("parallel",)),
    )(*args)


def adamw_rows3d(parts, sums, my_chip, w3, m3, v3, name):
    R, _, C = w3.shape
    S = parts.shape[0]
    tc = _tile(C, 512)

    def body(mc_ref, p_ref, o_ref, w_ref, m_ref, v_ref, g_ref, d_ref, nm_ref, nv_ref):
        g = _sum_partials(p_ref, o_ref, mc_ref[0])
        delta, nm, nv = _adamw_math(w_ref[...], g, m_ref[...], v_ref[...])
        g_ref[...] = g
        d_ref[...] = delta
        nm_ref[...] = nm
        nv_ref[...] = nv

    col = pl.BlockSpec((R, None, tc), lambda n, mc: (0, 0, n))
    shp = jax.ShapeDtypeStruct((R, 1, C), F32)
    grid_spec = pltpu.PrefetchScalarGridSpec(
        num_scalar_prefetch=1, grid=(C // tc,),
        in_specs=[pl.BlockSpec((S, R, tc), lambda n, mc: (0, 0, n)),
                  pl.BlockSpec((None, R, tc), lambda n, mc: (mc[0], 0, n)), col, col, col],
        out_specs=[col, col, col, col])
    return pl.pallas_call(
        body, name=name, grid_spec=grid_spec, out_shape=[shp, shp, shp, shp],
        compiler_params=_params(("parallel",)),
    )(my_chip.reshape(1).astype(jnp.int32), parts, sums, w3, m3, v3)


def kernel(x, p, norm_ffn1, ffn1_w_gate, ffn1_w_up, ffn1_w_down, norm_mix, w_in, b_f, w_o, norm_ffn2, ffn2_w_gate, ffn2_w_up, ffn2_w_down, norm_ple, w_ple_gate, w_ple_proj, rel_table, norm_final, loss_target, m_norm_ffn1, m_ffn1_w_gate, m_ffn1_w_up, m_ffn1_w_down, m_norm_mix, m_w_in, m_b_f, m_w_o, m_norm_ffn2, m_ffn2_w_gate, m_ffn2_w_up, m_ffn2_w_down, m_norm_ple, m_w_ple_gate, m_w_ple_proj, m_rel_table, m_norm_final, v_norm_ffn1, v_ffn1_w_gate, v_ffn1_w_up, v_ffn1_w_down, v_norm_mix, v_w_in, v_b_f, v_w_o, v_norm_ffn2, v_ffn2_w_gate, v_ffn2_w_up, v_ffn2_w_down, v_norm_ple, v_w_ple_gate, v_w_ple_proj, v_rel_table, v_norm_final):
    names = ["norm_ffn1", "ffn1_w_gate", "ffn1_w_up", "ffn1_w_down", "norm_mix", "w_in", "b_f", "w_o",
             "norm_ffn2", "ffn2_w_gate", "ffn2_w_up", "ffn2_w_down", "norm_ple", "w_ple_gate",
             "w_ple_proj", "rel_table", "norm_final"]
    w = dict(zip(names, [norm_ffn1, ffn1_w_gate, ffn1_w_up, ffn1_w_down, norm_mix, w_in, b_f, w_o,
                         norm_ffn2, ffn2_w_gate, ffn2_w_up, ffn2_w_down, norm_ple, w_ple_gate,
                         w_ple_proj, rel_table, norm_final]))
    m = dict(zip(names, [m_norm_ffn1, m_ffn1_w_gate, m_ffn1_w_up, m_ffn1_w_down, m_norm_mix, m_w_in,
                         m_b_f, m_w_o, m_norm_ffn2, m_ffn2_w_gate, m_ffn2_w_up, m_ffn2_w_down,
                         m_norm_ple, m_w_ple_gate, m_w_ple_proj, m_rel_table, m_norm_final]))
    v = dict(zip(names, [v_norm_ffn1, v_ffn1_w_gate, v_ffn1_w_up, v_ffn1_w_down, v_norm_mix, v_w_in,
                         v_b_f, v_w_o, v_norm_ffn2, v_ffn2_w_gate, v_ffn2_w_up, v_ffn2_w_down,
                         v_norm_ple, v_w_ple_gate, v_w_ple_proj, v_rel_table, v_norm_final]))
    sharded = ["ffn1_w_gate", "ffn1_w_up", "ffn1_w_down", "w_in", "w_o", "ffn2_w_gate", "ffn2_w_up",
               "ffn2_w_down", "w_ple_gate", "w_ple_proj"]
    small_names = [n for n in names if n not in sharded]

    xs, ps, tgt = x[0], p[0, 0], loss_target[0]
    T, D = xs.shape
    transposed = ("ffn1_w_gate", "ffn1_w_up", "ffn2_w_gate", "ffn2_w_up")

    def view(t, n):
        if n in transposed:
            return t[n][0].T
        if n == "w_in":
            return jnp.transpose(t[n], (2, 0, 1))
        return t[n][0]

    def unview(a, n):
        if n in transposed:
            return a.T.reshape(w[n].shape)
        if n == "w_in":
            return jnp.transpose(a, (1, 2, 0))
        return a.reshape(w[n].shape)

    sh = {n: view(w, n) for n in sharded}
    m_sh = {n: view(m, n) for n in sharded}
    v_sh = {n: view(v, n) for n in sharded}
    F8 = sh["ffn1_w_down"].shape[0]
    WIN8 = sh["w_in"].shape[0]
    me = 4 * lax.axis_index("x") + 2 * lax.axis_index("y") + lax.axis_index("c")

    def start(groups, name, after=None):
        srcs = [s for grp in groups for s in grp]
        sems, bufs, token = gather_start(srcs, [_landing(s, me) for s in srcs], name, after=after)
        return sems, bufs[:len(srcs)], bufs[len(srcs):], token

    cat0 = lambda ns, z: (jnp.concatenate([sh[n] for n in ns], axis=0) + z).astype(BF16)
    sems_a, srcs_a, lands_a, token_a = start(
        [[sh["ffn1_w_gate"].astype(BF16)], [sh["ffn1_w_up"].astype(BF16)], [sh["ffn1_w_down"].astype(BF16)]],
        "gather_start_ffn1")
    zero = token_a[0, 0]
    w_in_bf = rows_to_bf16(sh["w_in"], "w_in_bf16", dep=token_a)
    sems_b, srcs_b, lands_b, g_token = start(
        [[w_in_bf, (sh["w_o"] + zero).astype(BF16)],
         [cat0(["ffn2_w_gate", "ffn2_w_up"], zero), (sh["ffn2_w_down"] + zero).astype(BF16)],
         [(sh["w_ple_gate"] + zero).astype(BF16), (sh["w_ple_proj"] + zero).astype(BF16)]],
        "gather_start_rest", after=token_a)
    order = ["ffn1_g", "ffn1_u", "ffn1_d", "mix_in", "mix_out", "ffn2_gu", "ffn2_d", "ple"]
    group_sizes = [1, 1, 1, 1, 1, 1, 1, 2]
    g_sems, g_srcs, g_lands = sems_a + sems_b, srcs_a + srcs_b, lands_a + lands_b
    g_send, g_recv_d, g_recv_i = g_sems[0::3], g_sems[1::3], g_sems[2::3]
    first = np.cumsum([0] + group_sizes).tolist()
    passed = {}

    def arrays_of(group):
        k = order.index(group)
        return slice(first[k], first[k + 1])

    def forward(group, after):
        sl = arrays_of(group)
        f_sems, bufs = gather_forward(g_srcs[sl], g_lands[sl], g_recv_i[sl], after, "gather_forward_" + group)
        k = len(bufs) // 2
        passed[group] = (f_sems[0::2], f_sems[1::2], bufs[:k], bufs[k:])

    def weights(group, after):
        sl = arrays_of(group)
        f_send, f_recv, srcs, lands = passed[group]
        got = gather_wait(srcs, lands, g_send[sl], g_recv_d[sl], f_send, f_recv, after, "gather_wait_" + group)
        if group in ("ffn1_g", "ffn1_u", "ffn1_d"):
            return (got[0].reshape(N_DEV * F8, D),)
        if group == "ffn2_gu":
            return (got[0].reshape(N_DEV * 2 * F8, D),)
        if group in ("ffn2_d", "mix_out"):
            return (got[0].reshape(-1, D),)
        if group == "ple":
            return got[0].reshape(-1, D), got[1].reshape(-1, got[1].shape[2])
        w3, wf8 = _split_w_in(got[0].reshape(N_DEV * WIN8, D))
        return w3, jnp.pad(wf8, ((0, LANES - N_FOX), (0, 0)))

    ffn_names = ["ffn2_w_down", "ffn2_w_gate", "ffn2_w_up", "ffn1_w_down", "ffn1_w_gate", "ffn1_w_up"]
    scatter_groups = {"ple": ["w_ple_gate", "w_ple_proj"], "mix": ["w_in", "w_o"]}
    scatter_groups.update({n: [n] for n in ffn_names})
    x_i, y_i, c_i = _xyc()
    my_chip = 2 * x_i + y_i
    pair_stage, chip_stage, small_stage = {}, {}, {}

    def emit(group, grads, after=None):
        if group == "small":
            src = grads["small"]
            ss, rs, srcs, lands, token = exchange_start([src], [_landing(src, me)], [False], "scatter_start_small")
            small_stage["small"] = (ss, rs, srcs, lands)
            return token
        src4s = []
        for n in scatter_groups[group]:
            if n == "w_in":
                full = _join_w_in(grads["w3"], grads["wf"][:N_FOX])
                src4s.append(full.reshape(N_CHIPS, 2, WIN8, D))
            else:
                src4s.append(grads[n].reshape((N_CHIPS, 2) + sh[n].shape))
        lands = [lax.empty((N_CHIPS,) + s.shape[2:], BF16) for s in src4s]
        sems, bufs, token = scatter_pair_start(src4s, lands, "scatter_pair_start_" + group, after=after)
        k = len(src4s)
        pair_stage[group] = (sems, bufs[:k], bufs[k:])
        return token

    def emit2(group, after):
        sems, src4s, lands = pair_stage[group]
        src4s, lands = scatter_pair_wait(src4s, lands, sems, after, "scatter_pair_wait_" + group)
        sums = [chip_sum(s4, la, c_i, "chip_sum_" + n)
                for s4, la, n in zip(src4s, lands, scatter_groups[group])]
        chip_lands = [lax.empty(s.shape, s.dtype) for s in sums]
        sems, bufs, token = scatter_chip_start(sums, chip_lands, "scatter_chip_start_" + group)
        k = len(sums)
        chip_stage[group] = (sems, bufs[:k], bufs[k:])
        return token

    def dw_theirs(n, act, other, dep=None):
        theirs = ffn_dw_side(act, other, c_i, False, n + "_dw_theirs", dep=dep)
        theirs = theirs.reshape((N_CHIPS,) + sh[n].shape)
        sems, bufs, token = scatter_pair_start([theirs], [lax.empty(theirs.shape, BF16)],
                                               "scatter_pair_start_" + n, whole=True)
        pair_stage[n] = (sems, bufs[:1], bufs[1:])
        return token

    def dw_mine(n, act, other, dep=None):
        sems, theirs, lands = pair_stage[n]
        _, lands = scatter_pair_wait(theirs, lands, sems, dep, "scatter_pair_wait_" + n, whole=True)
        sums = ffn_dw_side(act, other, c_i, True, n + "_dw_mine", add=lands[0].reshape(-1, D))
        sums = sums.reshape((N_CHIPS,) + sh[n].shape)
        sems, bufs, token = scatter_chip_start([sums], [lax.empty(sums.shape, BF16)], "scatter_chip_start_" + n)
        chip_stage[n] = (sems, bufs[:1], bufs[1:])
        return token

    dx0 = local_step(
        xs, ps, tgt, w["norm_ffn1"], w["norm_mix"], w["norm_ffn2"], w["norm_ple"],
        w["norm_final"].reshape(1, D), w["b_f"], w["rel_table"], forward, weights, emit, emit2,
        dw_theirs, dw_mine, g_token)

    res = {}
    after = dx0
    for group in ["ple"] + ffn_names[:3] + ["mix"] + ffn_names[3:]:
        sems, sums, chip_lands = chip_stage[group]
        sums, parts = scatter_chip_wait(sums, chip_lands, sems, after, "scatter_chip_wait_" + group)
        for n, part, own in zip(scatter_groups[group], parts, sums):
            update = adamw_rows3d if n == "w_in" else adamw_sharded
            g, d, nm, nv = update(part, own, my_chip, sh[n], m_sh[n], v_sh[n], "adamw_" + n)
            res[n] = tuple(unview(a, n) for a in (g, d, nm, nv))
            after = g
    ss, rs, srcs, lands = small_stage["small"]
    small_parts, = exchange_wait(ss, rs, srcs, lands, [False], after, "scatter_wait_small")
    pack = lambda t: _pack_small(D, t["norm_ffn1"], t["norm_mix"], t["norm_ffn2"], t["norm_ple"],
                                 t["norm_final"], t["b_f"], t["rel_table"])
    gs, ds, ms, vs = adamw_small(small_parts, pack(w), pack(m), pack(v), "adamw_small")
    shapes = {n: w[n].shape for n in small_names}
    unpacked = [_unpack_small(a, shapes) for a in (gs, ds, ms, vs)]
    for n in small_names:
        res[n] = tuple(u[n] for u in unpacked)
    loss = gs[_ROW_LOSS, 0]

    out = [loss, dx0.reshape(x.shape)]
    for k in range(4):
        out += [res[n][k] for n in names]
    return tuple(out)
```

```python
import functools
import math

import numpy as np
import jax
import jax.numpy as jnp
from jax import lax
from jax.experimental import pallas as pl
from jax.experimental.pallas import tpu as pltpu

F32 = jnp.float32
BF16 = jnp.bfloat16

N_DEV = 8
HEAD_DIM = 128
N_FOX = 8
N_DIL = 8
N_HEADS = N_FOX + N_DIL
DILATED_PATTERNS = ((128, 1), (512, 4), (2048, 16))
N_REL_BUCKETS = 32
REL_MAX_DISTANCE = 2048
RMS_EPS = 1e-6
NEG_INF = -1e30
LANES = 128
VMEM_LIMIT = 56 * 1024 * 1024

ADAM_LR = 0.001
ADAM_B1 = 0.9
ADAM_B2 = 0.999
ADAM_EPS = 1e-08
ADAM_WD = 0.01
ADAM_STEP = 10

MESH = pl.DeviceIdType.MESH


def _params(sem):
    return pltpu.CompilerParams(dimension_semantics=sem, vmem_limit_bytes=VMEM_LIMIT)


def _dot(a, b, ca, cb, precision=None):
    return lax.dot_general(a, b, (((ca,), (cb,)), ((), ())),
                           preferred_element_type=F32, precision=precision)


def _sigmoid(z):
    return 1.0 / (1.0 + jnp.exp(-z))


def _tile(n, want):
    t = min(n, want)
    assert n % t == 0, (n, t)
    return t


def _dep_spec(ngrid):
    return pl.BlockSpec((8, LANES), lambda *_: (0, 0))


def rms_fwd(x, g, name, dep=None):
    T, D = x.shape
    tm = _tile(T, 256)

    def body(x_ref, g_ref, *rest):
        h_ref = rest[-1]
        xv = x_ref[...]
        r = lax.rsqrt(jnp.mean(xv * xv, axis=-1, keepdims=True) + RMS_EPS)
        h_ref[...] = (xv * r * g_ref[...]).astype(BF16)

    in_specs = [pl.BlockSpec((tm, D), lambda i: (i, 0)), pl.BlockSpec((1, D), lambda i: (0, 0))]
    args = [x, g]
    if dep is not None:
        in_specs.append(_dep_spec(1))
        args.append(dep)
    return pl.pallas_call(
        body, name=name, grid=(T // tm,), in_specs=in_specs,
        out_specs=pl.BlockSpec((tm, D), lambda i: (i, 0)),
        out_shape=jax.ShapeDtypeStruct((T, D), BF16),
        compiler_params=_params(("parallel",)),
    )(*args)


def rms_bwd(dh, x, g, dres, name, dep=None, half=True):
    T, D = x.shape
    tm = _tile(T, 256)

    def body(dh_ref, x_ref, g_ref, dres_ref, *rest):
        dx_ref, dg_ref = (rest[-3], rest[-1]) if half else (rest[-2], rest[-1])
        i = pl.program_id(0)
        xv = x_ref[...]
        r = lax.rsqrt(jnp.mean(xv * xv, axis=-1, keepdims=True) + RMS_EPS)
        xh = xv * r
        d = dh_ref[...]
        u = d * g_ref[...]
        dx = dres_ref[...] + r * (u - xh * jnp.mean(u * xh, axis=-1, keepdims=True))
        dx_ref[...] = dx
        if half:
            rest[-2][...] = (0.5 * dx).astype(BF16)
        part = jnp.sum(d * xh, axis=0, keepdims=True)

        @pl.when(i == 0)
        def _():
            dg_ref[...] = part

        @pl.when(i > 0)
        def _():
            dg_ref[...] += part

    row = pl.BlockSpec((tm, D), lambda i: (i, 0))
    vec = pl.BlockSpec((1, D), lambda i: (0, 0))
    in_specs = [row, row, vec, row]
    args = [dh, x, g, dres]
    if dep is not None:
        in_specs.append(_dep_spec(1))
        args.append(dep)
    out_specs = [row, row, vec] if half else [row, vec]
    out_shape = [jax.ShapeDtypeStruct((T, D), F32)] + ([jax.ShapeDtypeStruct((T, D), BF16)] if half else [])
    out_shape.append(jax.ShapeDtypeStruct((1, D), F32))
    outs = pl.pallas_call(
        body, name=name, grid=(T // tm,),
        in_specs=in_specs, out_specs=out_specs, out_shape=out_shape,
        compiler_params=_params(("arbitrary",)),
    )(*args)
    return tuple(outs) if half else (outs[0], None, outs[1])


def ple_loss(x, z, pp, g, target, name):
    T, D = x.shape
    tm = _tile(T, 256)

    def body(x_ref, z_ref, p_ref, g_ref, t_ref, loss_ref, dx_ref, dg_ref, dz_ref, dp_ref):
        i = pl.program_id(0)
        gate = _sigmoid(z_ref[...])
        ppv = p_ref[...]
        xv = x_ref[...] + gate * ppv
        gv = g_ref[...]
        r = lax.rsqrt(jnp.mean(xv * xv, axis=-1, keepdims=True) + RMS_EPS)
        xh = xv * r
        e = xh * gv - t_ref[...]
        lpart = 0.5 * jnp.sum(jnp.mean(e * e, axis=-1, keepdims=True), axis=0, keepdims=True)
        lrow = jnp.broadcast_to(lpart, (1, LANES))
        d = e * (1.0 / D)
        u = d * gv
        dx = r * (u - xh * jnp.mean(u * xh, axis=-1, keepdims=True))
        dx_ref[...] = dx
        dz_ref[...] = (dx * ppv * gate * (1.0 - gate)).astype(BF16)
        dp_ref[...] = (dx * gate).astype(BF16)
        part = jnp.sum(d * xh, axis=0, keepdims=True)

        @pl.when(i == 0)
        def _():
            dg_ref[...] = part
            loss_ref[...] = lrow

        @pl.when(i > 0)
        def _():
            dg_ref[...] += part
            loss_ref[...] += lrow

    row = pl.BlockSpec((tm, D), lambda i: (i, 0))
    vec = pl.BlockSpec((1, D), lambda i: (0, 0))
    return pl.pallas_call(
        body, name=name, grid=(T // tm,),
        in_specs=[row, row, row, vec, row],
        out_specs=[pl.BlockSpec((1, LANES), lambda i: (0, 0)), row, vec, row, row],
        out_shape=[jax.ShapeDtypeStruct((1, LANES), F32), jax.ShapeDtypeStruct((T, D), F32),
                   jax.ShapeDtypeStruct((1, D), F32), jax.ShapeDtypeStruct((T, D), BF16),
                   jax.ShapeDtypeStruct((T, D), BF16)],
        compiler_params=_params(("arbitrary",)),
    )(x, z, pp, g, target)


def _bf(v, scale=None):
    if scale is not None:
        v = v * scale
    return v.astype(BF16)


def mm_nn(a, b, name, *, tn, out_dtype, tm=512, n_out=None, b_block=None, b_map=None,
          res=None):
    T, K = a.shape
    N = n_out if n_out is not None else b.shape[1]
    tm = _tile(T, tm)
    tn = _tile(N, tn)
    b_block = b_block or (K, tn)
    b_map = b_map or (lambda n, i: (0, n))

    def body(*refs):
        a_ref, b_ref = refs[0], refs[1]
        o_ref = refs[-1]
        acc = _dot(_bf(a_ref[...]), _bf(b_ref[...]), 1, 0)
        if res is not None:
            acc = refs[2][...] + acc
        o_ref[...] = acc.astype(out_dtype)

    in_specs = [pl.BlockSpec((tm, K), lambda n, i: (i, 0)), pl.BlockSpec(b_block, b_map)]
    args = [a, b]
    if res is not None:
        in_specs.append(pl.BlockSpec((tm, tn), lambda n, i: (i, n)))
        args.append(res)
    return pl.pallas_call(
        body, name=name, grid=(N // tn, T // tm), in_specs=in_specs,
        out_specs=pl.BlockSpec((tm, tn), lambda n, i: (i, n)),
        out_shape=jax.ShapeDtypeStruct((T, N), out_dtype),
        compiler_params=_params(("parallel", "parallel")),
    )(*args)


def mm_nn_sum(pairs, name, *, tn, out_dtype, tm=512, dep=None):
    T = pairs[0][0].shape[0]
    N = pairs[0][1].shape[1]
    tm = _tile(T, tm)
    tn = _tile(N, tn)
    npair = len(pairs)

    def body(*refs):
        acc = None
        for q in range(npair):
            part = _dot(_bf(refs[2 * q][...]), _bf(refs[2 * q + 1][...]), 1, 0)
            acc = part if acc is None else acc + part
        refs[-1][...] = acc.astype(out_dtype)

    in_specs, args = [], []
    for a, b in pairs:
        K = a.shape[1]
        in_specs += [pl.BlockSpec((tm, K), lambda n, i: (i, 0)), pl.BlockSpec((K, tn), lambda n, i: (0, n))]
        args += [a, b]
    if dep is not None:
        in_specs.append(_dep_spec(2))
        args.append(dep)
    return pl.pallas_call(
        body, name=name, grid=(N // tn, T // tm), in_specs=in_specs,
        out_specs=pl.BlockSpec((tm, tn), lambda n, i: (i, n)),
        out_shape=jax.ShapeDtypeStruct((T, N), out_dtype),
        compiler_params=_params(("parallel", "parallel")),
    )(*args)


def mm_nt(pairs, name, *, tn, out_dtype, tm=512, dep=None):
    T = pairs[0][0].shape[0]
    N = pairs[0][1].shape[0]
    tm = _tile(T, tm)
    tn = _tile(N, tn)
    npair = len(pairs)

    def body(*refs):
        o_ref = refs[-1]
        acc = None
        for q in range(npair):
            part = _dot(_bf(refs[2 * q][...]), _bf(refs[2 * q + 1][...]), 1, 1)
            acc = part if acc is None else acc + part
        o_ref[...] = acc.astype(out_dtype)

    in_specs, args = [], []
    for a, b in pairs:
        K = a.shape[1]
        in_specs += [pl.BlockSpec((tm, K), lambda n, i: (i, 0)), pl.BlockSpec((tn, K), lambda n, i: (n, 0))]
        args += [a, b]
    if dep is not None:
        in_specs.append(_dep_spec(2))
        args.append(dep)
    return pl.pallas_call(
        body, name=name, grid=(N // tn, T // tm), in_specs=in_specs,
        out_specs=pl.BlockSpec((tm, tn), lambda n, i: (i, n)),
        out_shape=jax.ShapeDtypeStruct((T, N), out_dtype),
        compiler_params=_params(("parallel", "parallel")),
    )(*args)


def mm_tn(a, b, name, *, grid, a_block, a_map, b_block, b_map, o_block, o_map, out_shape,
          b_scale=None, dep=None):
    def body(a_ref, b_ref, *rest):
        rest[-1][...] = _dot(_bf(a_ref[...]), _bf(b_ref[...], b_scale), 0, 0).astype(BF16)

    in_specs = [pl.BlockSpec(a_block, a_map), pl.BlockSpec(b_block, b_map)]
    args = [a, b]
    if dep is not None:
        in_specs.append(_dep_spec(len(grid)))
        args.append(dep)
    return pl.pallas_call(
        body, name=name, grid=grid, in_specs=in_specs,
        out_specs=pl.BlockSpec(o_block, o_map),
        out_shape=jax.ShapeDtypeStruct(out_shape, BF16),
        compiler_params=_params(("parallel",) * len(grid)),
    )(*args)


def mm_tn_plain(a, b, name, *, tm=512, tn=1024, b_scale=None):
    T, M = a.shape
    N = b.shape[1]
    tm = _tile(M, tm)
    tn = _tile(N, tn)
    return mm_tn(a, b, name, grid=(M // tm, N // tn),
                 a_block=(T, tm), a_map=lambda m, n: (0, m),
                 b_block=(T, tn), b_map=lambda m, n: (0, n),
                 o_block=(tm, tn), o_map=lambda m, n: (m, n),
                 out_shape=(M, N), b_scale=b_scale)


def ffn_up(h, wgu, gi, ui, nper, name):
    T, D = h.shape
    F8 = wgu.shape[0] // (N_DEV * nper)
    tm = _tile(T, 512)
    nt = T // tm

    def body(h_ref, wg_ref, wu_ref, ga_ref, gb_ref, s_ref):
        hv = h_ref[...]
        a = _dot(hv, wg_ref[...], 1, 1)
        b = _dot(hv, wu_ref[...], 1, 1)
        sg = _sigmoid(a)
        silu = a * sg
        ga_ref[...] = (b * (sg * (1.0 + a * (1.0 - sg)))).astype(BF16)
        gb_ref[...] = silu.astype(BF16)
        s_ref[...] = (silu * b).astype(BF16)

    blk = pl.BlockSpec((tm, F8), lambda j, i: (j * nt + i, 0))
    shp = jax.ShapeDtypeStruct((N_DEV * T, F8), BF16)
    return pl.pallas_call(
        body, name=name, grid=(N_DEV, nt),
        in_specs=[pl.BlockSpec((tm, D), lambda j, i: (i, 0)),
                  pl.BlockSpec((F8, D), lambda j, i: (j * nper + gi, 0)),
                  pl.BlockSpec((F8, D), lambda j, i: (j * nper + ui, 0))],
        out_specs=[blk, blk, blk], out_shape=[shp, shp, shp],
        compiler_params=_params(("parallel", "parallel")),
    )(h, wgu, wgu)


def ffn_gate(h, wg, name):
    T, D = h.shape
    F8 = wg.shape[0] // N_DEV
    tm = _tile(T, 512)
    nt = T // tm

    def body(h_ref, wg_ref, a_ref):
        a_ref[...] = _dot(h_ref[...], wg_ref[...], 1, 1).astype(BF16)

    return pl.pallas_call(
        body, name=name, grid=(N_DEV, nt),
        in_specs=[pl.BlockSpec((tm, D), lambda j, i: (i, 0)), pl.BlockSpec((F8, D), lambda j, i: (j, 0))],
        out_specs=pl.BlockSpec((tm, F8), lambda j, i: (j * nt + i, 0)),
        out_shape=jax.ShapeDtypeStruct((N_DEV * T, F8), BF16),
        compiler_params=_params(("parallel", "parallel")),
    )(h, wg)


def ffn_up_gated(h, wu, a, name):
    T, D = h.shape
    F8 = wu.shape[0] // N_DEV
    tm = _tile(T, 512)
    nt = T // tm

    def body(h_ref, wu_ref, a_ref, ga_ref, gb_ref, s_ref):
        av = a_ref[...].astype(F32)
        b = _dot(h_ref[...], wu_ref[...], 1, 1)
        sg = _sigmoid(av)
        silu = av * sg
        ga_ref[...] = (b * (sg * (1.0 + av * (1.0 - sg)))).astype(BF16)
        gb_ref[...] = silu.astype(BF16)
        s_ref[...] = (silu * b).astype(BF16)

    blk = pl.BlockSpec((tm, F8), lambda j, i: (j * nt + i, 0))
    shp = jax.ShapeDtypeStruct((N_DEV * T, F8), BF16)
    return pl.pallas_call(
        body, name=name, grid=(N_DEV, nt),
        in_specs=[pl.BlockSpec((tm, D), lambda j, i: (i, 0)), pl.BlockSpec((F8, D), lambda j, i: (j, 0)), blk],
        out_specs=[blk, blk, blk], out_shape=[shp, shp, shp],
        compiler_params=_params(("parallel", "parallel")),
    )(h, wu, a)


def ffn_down(s, wd, di, nper, x, g_next, name):
    T, D = x.shape
    F8 = s.shape[1]
    tm = _tile(T, 512)
    nt = T // tm

    def body(s_ref, w_ref, x_ref, g_ref, o_ref, h_ref, acc_ref):
        j = pl.program_id(1)
        part = _dot(s_ref[...], w_ref[...], 1, 0)

        @pl.when(j == 0)
        def _():
            acc_ref[...] = part

        @pl.when(j > 0)
        def _():
            acc_ref[...] += part

        @pl.when(j == N_DEV - 1)
        def _():
            xv = x_ref[...] + 0.5 * acc_ref[...]
            o_ref[...] = xv
            r = lax.rsqrt(jnp.mean(xv * xv, axis=-1, keepdims=True) + RMS_EPS)
            h_ref[...] = (xv * r * g_ref[...]).astype(BF16)

    row = pl.BlockSpec((tm, D), lambda i, j: (i, 0))
    return pl.pallas_call(
        body, name=name, grid=(nt, N_DEV),
        in_specs=[pl.BlockSpec((tm, F8), lambda i, j: (j * nt + i, 0)),
                  pl.BlockSpec((F8, D), lambda i, j: (j * nper + di, 0)),
                  row, pl.BlockSpec((1, D), lambda i, j: (0, 0))],
        out_specs=[row, row],
        out_shape=[jax.ShapeDtypeStruct((T, D), F32), jax.ShapeDtypeStruct((T, D), BF16)],
        scratch_shapes=[pltpu.VMEM((tm, D), F32)],
        compiler_params=_params(("parallel", "arbitrary")),
    )(s, wd, x, g_next)


def ffn_bwd_act(dxh, wd, di, nper_d, a, b, name, dep=None):
    T, D = dxh.shape
    F8 = a.shape[1]
    tm = _tile(T, 1024)
    nt = T // tm

    def body(dx_ref, w_ref, a_ref, b_ref, *rest):
        da_ref, db_ref = rest[-2], rest[-1]
        ds = _dot(dx_ref[...], w_ref[...], 1, 1)
        da_ref[...] = (ds * a_ref[...].astype(F32)).astype(BF16)
        db_ref[...] = (ds * b_ref[...].astype(F32)).astype(BF16)

    blk = pl.BlockSpec((tm, F8), lambda j, i: (j * nt + i, 0))
    shp = jax.ShapeDtypeStruct((N_DEV * T, F8), BF16)
    in_specs = [pl.BlockSpec((tm, D), lambda j, i: (i, 0)),
                pl.BlockSpec((F8, D), lambda j, i: (j * nper_d + di, 0)), blk, blk]
    args = [dxh, wd, a, b]
    if dep is not None:
        in_specs.append(_dep_spec(2))
        args.append(dep)
    return pl.pallas_call(
        body, name=name, grid=(N_DEV, nt), in_specs=in_specs,
        out_specs=[blk, blk], out_shape=[shp, shp],
        compiler_params=_params(("parallel", "parallel")),
    )(*args)


def ffn_bwd_dh(da, db, wg, wu, gi, ui, nper, D, name, dep=None):
    F8 = da.shape[1]
    T = da.shape[0] // N_DEV
    tm = _tile(T, 1024)
    nt = T // tm

    def body(da_ref, db_ref, wg_ref, wu_ref, *rest):
        o_ref, acc_ref = rest[-2], rest[-1]
        j = pl.program_id(1)
        part = _dot(da_ref[...], wg_ref[...], 1, 0) + _dot(db_ref[...], wu_ref[...], 1, 0)

        @pl.when(j == 0)
        def _():
            acc_ref[...] = part

        @pl.when(j > 0)
        def _():
            acc_ref[...] += part

        @pl.when(j == N_DEV - 1)
        def _():
            o_ref[...] = acc_ref[...]

    blk = pl.BlockSpec((tm, F8), lambda i, j: (j * nt + i, 0))
    in_specs = [blk, blk,
                pl.BlockSpec((F8, D), lambda i, j: (j * nper + gi, 0)),
                pl.BlockSpec((F8, D), lambda i, j: (j * nper + ui, 0))]
    args = [da, db, wg, wu]
    if dep is not None:
        in_specs.append(_dep_spec(2))
        args.append(dep)
    return pl.pallas_call(
        body, name=name, grid=(nt, N_DEV), in_specs=in_specs,
        out_specs=pl.BlockSpec((tm, D), lambda i, j: (i, 0)),
        out_shape=jax.ShapeDtypeStruct((T, D), F32),
        scratch_shapes=[pltpu.VMEM((tm, D), F32)],
        compiler_params=_params(("parallel", "arbitrary")),
    )(*args)


def ffn_dw_side(act, other, c, mine, name, add=None, dep=None):
    F8 = act.shape[1]
    T, D = other.shape
    tm = _tile(D, 1024)

    def body(c_ref, a_ref, b_ref, *rest):
        acc = _dot(a_ref[...], b_ref[...], 0, 0)
        if add is not None:
            acc = acc + rest[0][...].astype(F32)
        rest[-1][...] = acc.astype(BF16)

    def shard(q, cr):
        return 2 * q + (cr[0] if mine else 1 - cr[0])

    in_specs = [pl.BlockSpec((T, F8), lambda q, m, cr: (shard(q, cr), 0)),
                pl.BlockSpec((T, tm), lambda q, m, cr: (0, m))]
    args = [act, other]
    if add is not None:
        in_specs.append(pl.BlockSpec((F8, tm), lambda q, m, cr: (q, m)))
        args.append(add)
    if dep is not None:
        in_specs.append(pl.BlockSpec((8, LANES), lambda q, m, cr: (0, 0)))
        args.append(dep)
    grid_spec = pltpu.PrefetchScalarGridSpec(
        num_scalar_prefetch=1, grid=(N_DEV // 2, D // tm), in_specs=in_specs,
        out_specs=pl.BlockSpec((F8, tm), lambda q, m, cr: (q, m)))
    return pl.pallas_call(
        body, name=name, grid_spec=grid_spec,
        out_shape=jax.ShapeDtypeStruct((N_DEV // 2 * F8, D), BF16),
        compiler_params=_params(("parallel", "parallel")),
    )(c.reshape(1).astype(jnp.int32), *args)


def _t5_bucket_np(dist):
    max_exact = N_REL_BUCKETS // 2
    d = np.maximum(dist, 1).astype(np.float64)
    large = max_exact + (np.log(d / max_exact) / math.log(REL_MAX_DISTANCE / max_exact)
                         * (N_REL_BUCKETS - max_exact)).astype(np.int64)
    large32 = max_exact + (np.log(d.astype(np.float32) / np.float32(max_exact))
                           / np.float32(math.log(REL_MAX_DISTANCE / max_exact))
                           * np.float32(N_REL_BUCKETS - max_exact)).astype(np.int64)
    assert np.array_equal(large, large32)
    large = np.minimum(large, N_REL_BUCKETS - 1)
    return np.where(dist < max_exact, dist, large)


def _distance_tables(T, tq):
    dist = np.arange(T)
    mult = np.zeros(T, np.int64)
    for window, dilation in DILATED_PATTERNS:
        mult += ((dist % dilation == 0) & (dist // dilation <= window // dilation)).astype(np.int64)
    logm = np.where(mult > 0, np.log(np.maximum(mult, 1)), NEG_INF).astype(np.float32)
    bucket = _t5_bucket_np(dist).astype(np.int32)
    nkb = T // tq
    k = np.arange(nkb)[:, None, None]
    r = np.arange(tq)[None, :, None]
    c = np.arange(tq)[None, None, :]
    delta = k * tq + r - c
    return bucket, logm, delta


def _tile_buckets(T, tq):
    bucket, logm, delta = _distance_tables(T, tq)
    safe = np.maximum(delta, 0)
    bidx = np.where(delta >= 0, bucket[safe], -1).astype(np.int32)
    logm_t = np.where(delta >= 0, logm[safe], NEG_INF).astype(np.float32)
    present = [sorted(set(np.unique(bidx[k]).tolist()) - {-1}) for k in range(T // tq)]
    return bidx, logm_t, present


def bias_tiles(rel_table, T, tq):
    bidx, logm_t, present = _tile_buckets(T, tq)
    nkb = T // tq

    def body(tab_ref, b_ref, lm_ref, o_ref):
        slot = pl.program_id(0)

        @pl.when(slot == 0)
        def _():
            o_ref[...] = jnp.where(b_ref[...] >= 0, 0.0, NEG_INF)

        @pl.when(slot > 0)
        def _():
            for k in range(nkb):
                bi = b_ref[k]
                acc = lm_ref[k]
                for b in present[k]:
                    acc = acc + jnp.where(bi == b, tab_ref[b, slot - 1], 0.0)
                o_ref[k] = acc

    full = pl.BlockSpec((nkb, tq, tq), lambda s: (0, 0, 0))
    return pl.pallas_call(
        body, name="bias_tiles", grid=(1 + N_DIL,),
        in_specs=[pl.BlockSpec(memory_space=pltpu.SMEM), full, full],
        out_specs=pl.BlockSpec((None, nkb, tq, tq), lambda s: (s, 0, 0, 0)),
        out_shape=jax.ShapeDtypeStruct((1 + N_DIL, nkb, tq, tq), F32),
        compiler_params=_params(("parallel",)),
    )(rel_table, jnp.asarray(bidx), jnp.asarray(logm_t))


def fox_gate_fwd(uf, bf, name):
    T = uf.shape[0]
    tb = _tile(T, 512)

    def body(u_ref, b_ref, c_ref, ct_ref):
        lane = lax.broadcasted_iota(jnp.int32, (1, LANES), 1)
        tri = (lax.broadcasted_iota(jnp.int32, (tb, tb), 0)
               >= lax.broadcasted_iota(jnp.int32, (tb, tb), 1)).astype(F32)
        carry = jnp.zeros((1, LANES), F32)
        for blk in range(T // tb):
            z = u_ref[pl.ds(blk * tb, tb), :] + b_ref[...]
            lf = jnp.minimum(z, 0.0) - jnp.log1p(jnp.exp(-jnp.abs(z)))
            lf = jnp.where(lane < N_FOX, lf, 0.0)
            cb = _dot(tri, lf, 1, 0, precision=lax.Precision.HIGHEST) + carry
            c_ref[pl.ds(blk * tb, tb), :] = cb
            ct_ref[:, pl.ds(blk * tb, tb)] = cb.T
            carry = cb[tb - 1:tb, :]

    return pl.pallas_call(
        body, name=name,
        out_shape=[jax.ShapeDtypeStruct((T, LANES), F32), jax.ShapeDtypeStruct((LANES, T), F32)],
        compiler_params=_params(None),
    )(uf, bf)


def fox_gate_bwd(dct, uf, bf, name):
    T = uf.shape[0]
    tb = _tile(T, 512)

    def body(d_ref, u_ref, b_ref, du_ref, db_ref):
        lane = lax.broadcasted_iota(jnp.int32, (1, LANES), 1)
        triu = (lax.broadcasted_iota(jnp.int32, (tb, tb), 0)
                <= lax.broadcasted_iota(jnp.int32, (tb, tb), 1)).astype(F32)
        carry = jnp.zeros((1, LANES), F32)
        dbv = jnp.zeros((1, LANES), F32)
        for blk in reversed(range(T // tb)):
            dc = d_ref[:, pl.ds(blk * tb, tb)].T
            dlf = _dot(triu, dc, 1, 0, precision=lax.Precision.HIGHEST) + carry
            carry = dlf[0:1, :]
            z = u_ref[pl.ds(blk * tb, tb), :] + b_ref[...]
            dz = jnp.where(lane < N_FOX, dlf * (1.0 - _sigmoid(z)), 0.0)
            du_ref[pl.ds(blk * tb, tb), :] = dz
            dbv = dbv + jnp.sum(dz, axis=0, keepdims=True)
        db_ref[...] = dbv

    return pl.pallas_call(
        body, name=name,
        out_shape=[jax.ShapeDtypeStruct((T, LANES), F32), jax.ShapeDtypeStruct((1, LANES), F32)],
        compiler_params=_params(None),
    )(dct, uf, bf)


def _bias_slot(h):
    return jnp.maximum(h - (N_FOX - 1), 0)


def _scores(q_ref, k_ref, c_ref, ct_ref, tb_ref, h, i, tq, fox):
    scale = HEAD_DIM ** -0.5
    n = (i + 1) * tq
    rows = pl.ds(i * tq, tq)
    s = _dot(q_ref[rows, :], k_ref[pl.ds(0, n), :], 1, 1) * scale
    if not fox:
        return s + jnp.concatenate([tb_ref[i - jb] for jb in range(i + 1)], axis=1)
    lane = lax.broadcasted_iota(jnp.int32, (1, LANES), 1)
    c_col = jnp.sum(jnp.where(lane == h, c_ref[rows, :], 0.0), axis=1, keepdims=True)
    c_row = ct_ref[pl.ds(h, 1), pl.ds(0, n)]
    s = s + (c_col - c_row)
    if i == 0:
        return s + tb_ref[0]
    return jnp.concatenate([s[:, :i * tq], s[:, i * tq:] + tb_ref[0]], axis=1)


def _attn_specs(T, tq):
    nkb = T // tq
    return [
        pl.BlockSpec((T, HEAD_DIM), lambda h: (0, h)),
        pl.BlockSpec((T, HEAD_DIM), lambda h: (0, N_HEADS + h)),
        pl.BlockSpec((T, HEAD_DIM), lambda h: (0, 2 * N_HEADS + h)),
        pl.BlockSpec((T, LANES), lambda h: (0, 0)),
        pl.BlockSpec((LANES, T), lambda h: (0, 0)),
        pl.BlockSpec((None, nkb, tq, tq), lambda h: (_bias_slot(h), 0, 0, 0)),
    ]


def attention_fwd(qkv, c, ct, tiles, name):
    T = qkv.shape[0]
    tq = tiles.shape[2]

    def body(q_ref, k_ref, v_ref, c_ref, ct_ref, tb_ref, o_ref, lse_ref):
        h = pl.program_id(0)
        lane = lax.broadcasted_iota(jnp.int32, (1, LANES), 1)

        @pl.when(h == 0)
        def _():
            lse_ref[...] = jnp.zeros_like(lse_ref)

        def head(fox):
            for i in range(T // tq):
                rows = pl.ds(i * tq, tq)
                s = _scores(q_ref, k_ref, c_ref, ct_ref, tb_ref, h, i, tq, fox)
                m = jnp.max(s, axis=1, keepdims=True)
                p = jnp.exp(s - m)
                l = jnp.sum(p, axis=1, keepdims=True)
                o = _dot(p.astype(BF16), v_ref[pl.ds(0, (i + 1) * tq), :], 1, 0) * (1.0 / l)
                o_ref[rows, :] = o.astype(BF16)
                lse_ref[rows, :] = jnp.where(lane == h, m + jnp.log(l), lse_ref[rows, :])

        pl.when(h < N_FOX)(functools.partial(head, True))
        pl.when(h >= N_FOX)(functools.partial(head, False))

    return pl.pallas_call(
        body, name=name, grid=(N_HEADS,),
        in_specs=_attn_specs(T, tq),
        out_specs=[pl.BlockSpec((T, HEAD_DIM), lambda h: (0, h)), pl.BlockSpec((T, LANES), lambda h: (0, 0))],
        out_shape=[jax.ShapeDtypeStruct((T, N_HEADS * HEAD_DIM), BF16), jax.ShapeDtypeStruct((T, LANES), F32)],
        compiler_params=_params(("arbitrary",)),
    )(qkv, qkv, qkv, c, ct, tiles)


def attention_bwd(qkv, c, ct, tiles, lse, o, do, name):
    T = qkv.shape[0]
    tq = tiles.shape[2]
    nkb = T // tq
    scale = HEAD_DIM ** -0.5

    def body(q_ref, k_ref, v_ref, c_ref, ct_ref, tb_ref, lse_ref, o_ref, do_ref,
             dq_ref, dk_ref, dv_ref, dct_ref, dtb_ref, dk_acc, dv_acc):
        h = pl.program_id(0)
        lane = lax.broadcasted_iota(jnp.int32, (1, LANES), 1)
        dk_acc[...] = jnp.zeros_like(dk_acc)
        dv_acc[...] = jnp.zeros_like(dv_acc)
        dct_ref[...] = jnp.zeros_like(dct_ref)
        dtb_ref[...] = jnp.zeros_like(dtb_ref)

        def head(fox):
            for i in range(nkb):
                rows, keys = pl.ds(i * tq, tq), pl.ds(0, (i + 1) * tq)
                s = _scores(q_ref, k_ref, c_ref, ct_ref, tb_ref, h, i, tq, fox)
                lse_col = jnp.sum(jnp.where(lane == h, lse_ref[rows, :], 0.0), axis=1, keepdims=True)
                p = jnp.exp(s - lse_col)
                p_b = p.astype(BF16)
                dov = do_ref[rows, :]
                dp = _dot(dov, v_ref[keys, :], 1, 1)
                if fox:
                    delta = jnp.sum(p * dp, axis=1, keepdims=True)
                else:
                    delta = jnp.sum(dov.astype(F32) * o_ref[rows, :].astype(F32), axis=1, keepdims=True)
                ds = p * (dp - delta)
                ds_b = ds.astype(BF16)
                dq_ref[rows, :] = (_dot(ds_b, k_ref[keys, :], 1, 0) * scale).astype(BF16)
                dk_acc[:, keys] += _dot(q_ref[rows, :], ds_b, 0, 0) * scale
                dv_acc[:, keys] += _dot(dov, p_b, 0, 0)
                if fox:
                    dct_ref[:, keys] += -jnp.sum(ds, axis=0, keepdims=True)
                else:
                    for jb in range(i + 1):
                        dtb_ref[i - jb] += ds[:, jb * tq:(jb + 1) * tq]

        pl.when(h < N_FOX)(functools.partial(head, True))
        pl.when(h >= N_FOX)(functools.partial(head, False))
        dk_ref[...] = dk_acc[...].T.astype(BF16)
        dv_ref[...] = dv_acc[...].T.astype(BF16)

    head_cols = jax.ShapeDtypeStruct((T, N_HEADS * HEAD_DIM), BF16)
    col = pl.BlockSpec((T, HEAD_DIM), lambda h: (0, h))
    return pl.pallas_call(
        body, name=name, grid=(N_HEADS,),
        in_specs=_attn_specs(T, tq) + [pl.BlockSpec((T, LANES), lambda h: (0, 0)), col, col],
        out_specs=[col, col, col,
                   pl.BlockSpec((None, 1, T), lambda h: (h, 0, 0)),
                   pl.BlockSpec((None, nkb, tq, tq), lambda h: (_bias_slot(h), 0, 0, 0))],
        out_shape=[head_cols, head_cols, head_cols,
                   jax.ShapeDtypeStruct((N_HEADS, 1, T), F32),
                   jax.ShapeDtypeStruct((1 + N_DIL, nkb, tq, tq), F32)],
        scratch_shapes=[pltpu.VMEM((HEAD_DIM, T), F32), pltpu.VMEM((HEAD_DIM, T), F32)],
        compiler_params=_params(("arbitrary",)),
    )(qkv, qkv, qkv, c, ct, tiles, lse, o, do)


def rel_table_grad(dtiles, T, name):
    tq = dtiles.shape[2]
    nkb = T // tq
    bidx, _, present = _tile_buckets(T, tq)

    def body(d_ref, b_ref, o_ref):
        lane = lax.broadcasted_iota(jnp.int32, (1, LANES), 1)
        row = jnp.zeros((1, LANES), F32)
        for k in range(nkb):
            d = d_ref[k]
            bi = b_ref[k]
            for b in present[k]:
                v = jnp.sum(jnp.sum(jnp.where(bi == b, d, 0.0), axis=0, keepdims=True),
                            axis=1, keepdims=True)
                row = row + jnp.where(lane == b, v, 0.0)
        o_ref[...] = row

    return pl.pallas_call(
        body, name=name, grid=(N_DIL,),
        in_specs=[pl.BlockSpec((None, nkb, tq, tq), lambda h: (h + 1, 0, 0, 0)),
                  pl.BlockSpec((nkb, tq, tq), lambda h: (0, 0, 0))],
        out_specs=pl.BlockSpec((None, 1, LANES), lambda h: (h, 0, 0)),
        out_shape=jax.ShapeDtypeStruct((N_DIL, 1, LANES), F32),
        compiler_params=_params(("parallel",)),
    )(dtiles, jnp.asarray(bidx))


def _peer_list():
    x, y, c = lax.axis_index("x"), lax.axis_index("y"), lax.axis_index("c")
    me = 4 * x + 2 * y + c
    peers = []
    for fx in (0, 1):
        for fy in (0, 1):
            for fc in (0, 1):
                if fx or fy or fc:
                    px = 1 - x if fx else x
                    py = 1 - y if fy else y
                    pc = 1 - c if fc else c
                    peers.append(((px, py, pc), 4 * px + 2 * py + pc))
    return me, peers


_HBM = pl.BlockSpec(memory_space=pltpu.HBM)
_SEM = pl.BlockSpec(memory_space=pltpu.SEMAPHORE)
_EFFECT = pltpu.SideEffectType.DATAFLOW_SIDE_EFFECTING
N_PEERS = N_DEV - 1


def _in_hbm(a):
    return pltpu.with_memory_space_constraint(a, pltpu.HBM)


def _exchange_copies(srcs, lands, send_sems, recv_sems, blockwise):
    me, peers = _peer_list()
    sends, recvs = [], []
    for a in range(len(srcs)):
        for k, (dev, idx) in enumerate(peers):
            src = srcs[a].at[idx] if blockwise[a] else srcs[a]
            sends.append(pltpu.make_async_remote_copy(
                src_ref=src, dst_ref=lands[a].at[me], send_sem=send_sems[a].at[k],
                recv_sem=recv_sems[a].at[k], device_id=dev, device_id_type=MESH))
            recvs.append(pltpu.make_async_remote_copy(
                src_ref=src, dst_ref=lands[a].at[idx], send_sem=send_sems[a].at[k],
                recv_sem=recv_sems[a].at[k], device_id=dev, device_id_type=MESH))
    return sends, recvs


def exchange_start(srcs, lands, blockwise, name):
    n = len(srcs)

    def body(*refs):
        src_in, land_in = refs[:n], refs[n:2 * n]
        send_sems, recv_sems = refs[2 * n:3 * n], refs[3 * n:4 * n]
        token = refs[6 * n]
        sends, _ = _exchange_copies(src_in, land_in, send_sems, recv_sems, blockwise)
        for cp in sends:
            cp.start()
        token[...] = jnp.zeros_like(token)

    out_shape = ([pltpu.SemaphoreType.DMA((N_PEERS,))] * (2 * n)
                 + [pltpu.HBM(s.shape, s.dtype) for s in srcs]
                 + [pltpu.HBM(l.shape, l.dtype) for l in lands]
                 + [jax.ShapeDtypeStruct((8, LANES), F32)])
    aliases = {a: 2 * n + a for a in range(2 * n)}
    outs = pl.pallas_call(
        body, name=name, out_shape=out_shape,
        in_specs=[_HBM] * (2 * n),
        out_specs=[_SEM] * (2 * n) + [_HBM] * (2 * n) + [pl.BlockSpec(memory_space=pltpu.VMEM)],
        input_output_aliases=aliases,
        compiler_params=pltpu.CompilerParams(has_side_effects=_EFFECT),
    )(*[_in_hbm(s) for s in srcs], *[_in_hbm(l) for l in lands])
    return (outs[:n], outs[n:2 * n], outs[2 * n:3 * n], outs[3 * n:4 * n], outs[4 * n])


def exchange_wait(send_sems, recv_sems, srcs, lands, blockwise, after, name):
    n = len(srcs)

    def body(*refs):
        src_in, land_in = refs[:n], refs[n:2 * n]
        ss, rs = refs[2 * n:3 * n], refs[3 * n:4 * n]
        sends, recvs = _exchange_copies(src_in, land_in, ss, rs, blockwise)
        for cp in sends:
            cp.wait_send()
        for cp in recvs:
            cp.wait_recv()

    outs = pl.pallas_call(
        body, name=name,
        out_shape=[pltpu.HBM(s.shape, s.dtype) for s in srcs] + [pltpu.HBM(l.shape, l.dtype) for l in lands],
        in_specs=[_HBM] * (2 * n) + [_SEM] * (2 * n) + [pl.BlockSpec(memory_space=pl.ANY)],
        out_specs=[_HBM] * (2 * n),
        input_output_aliases={a: a for a in range(2 * n)},
        compiler_params=pltpu.CompilerParams(has_side_effects=_EFFECT),
    )(*srcs, *lands, *send_sems, *recv_sems, after)
    return outs[n:]


def _landing(own_block, me, slots=N_DEV):
    empty = lax.empty((slots,) + own_block.shape, own_block.dtype)
    return lax.dynamic_update_slice(empty, own_block[None], (me,) + (0,) * own_block.ndim)


N_CHIPS = N_DEV // 2
_CHIP_FLIPS = ((1, 0), (0, 1), (1, 1))


def _xyc():
    return lax.axis_index("x"), lax.axis_index("y"), lax.axis_index("c")


def _other_chips(x, y):
    return [(1 - x if fx else x, 1 - y if fy else y) for fx, fy in _CHIP_FLIPS]


def _remote(src, dst, send_sem, recv_sem, dev):
    return pltpu.make_async_remote_copy(src_ref=src, dst_ref=dst, send_sem=send_sem, recv_sem=recv_sem,
                                        device_id=dev, device_id_type=MESH)


def comm_call(name, bufs, sems_in, sems_out, fn, after=None, want_token=False):
    nb, ni, no = len(bufs), len(sems_in), len(sems_out)
    afters = [] if after is None else (list(after) if isinstance(after, (list, tuple)) else [after])
    na = len(afters)

    def body(*refs):
        buf_refs = refs[:nb]
        sin = refs[nb:nb + ni]
        sout = refs[nb + ni + na:nb + ni + na + no]
        fn(buf_refs, sin, sout)
        if want_token:
            tok = refs[nb + ni + na + no + nb]
            tok[...] = jnp.zeros_like(tok)

    out_shape = list(sems_out) + [pltpu.HBM(b.shape, b.dtype) for b in bufs]
    out_specs = [_SEM] * no + [_HBM] * nb
    if want_token:
        out_shape.append(jax.ShapeDtypeStruct((8, LANES), F32))
        out_specs.append(pl.BlockSpec(memory_space=pltpu.VMEM))
    args = [_in_hbm(b) for b in bufs] + list(sems_in) + afters
    outs = pl.pallas_call(
        body, name=name, out_shape=out_shape,
        in_specs=[_HBM] * nb + [_SEM] * ni + [pl.BlockSpec(memory_space=pl.ANY)] * na,
        out_specs=out_specs, input_output_aliases={a: no + a for a in range(nb)},
        compiler_params=pltpu.CompilerParams(has_side_effects=_EFFECT),
    )(*args)
    return list(outs[:no]), list(outs[no:no + nb]), (outs[no + nb] if want_token else None)


def _dma_sems(*sizes):
    return [pltpu.SemaphoreType.DMA((s,)) for s in sizes]


def gather_start(srcs, lands, name, after=None):
    n = len(srcs)

    def fn(bufs, sin, sout):
        x, y, c = _xyc()
        me = 4 * x + 2 * y + c
        for a in range(n):
            src, land = bufs[a], bufs[n + a]
            send, recv_d, recv_i = sout[3 * a:3 * a + 3]
            _remote(src, land.at[me], send.at[0], recv_d.at[0], (x, y, 1 - c)).start()
            for k, (px, py) in enumerate(_other_chips(x, y)):
                _remote(src, land.at[me], send.at[1 + k], recv_i.at[k], (px, py, c)).start()

    return comm_call(name, list(srcs) + list(lands), [], _dma_sems(4, 1, 3) * n, fn, after=after, want_token=True)


def gather_forward(srcs, lands, recv_i, after, name):
    n = len(srcs)

    def fn(bufs, sin, sout):
        x, y, c = _xyc()
        for a in range(n):
            src, land = bufs[a], bufs[n + a]
            f_send, f_recv = sout[2 * a:2 * a + 2]
            for k, (px, py) in enumerate(_other_chips(x, y)):
                blk = land.at[4 * px + 2 * py + c]
                _remote(src, blk, f_send.at[k], sin[a].at[k], (px, py, c)).wait_recv()
                _remote(blk, blk, f_send.at[k], f_recv.at[k], (x, y, 1 - c)).start()

    sems, bufs, _ = comm_call(name, list(srcs) + list(lands), recv_i, _dma_sems(3, 3) * n, fn, after=after)
    return sems, bufs


def gather_wait(srcs, lands, send, recv_d, f_send, f_recv, after, name):
    n = len(srcs)

    def fn(bufs, sin, sout):
        x, y, c = _xyc()
        sib = (x, y, 1 - c)
        for a in range(n):
            src, land = bufs[a], bufs[n + a]
            s_send, s_recv_d, s_fsend, s_frecv = sin[4 * a:4 * a + 4]
            sib_blk = land.at[4 * x + 2 * y + 1 - c]
            for k in range(4):
                _remote(src, sib_blk, s_send.at[k], s_recv_d.at[0], sib).wait_send()
            _remote(src, sib_blk, s_send.at[0], s_recv_d.at[0], sib).wait_recv()
            for k, (px, py) in enumerate(_other_chips(x, y)):
                cp = _remote(src, land.at[4 * px + 2 * py + 1 - c], s_fsend.at[k], s_frecv.at[k], sib)
                cp.wait_send()
                cp.wait_recv()

    sems_in = []
    for a in range(n):
        sems_in += [send[a], recv_d[a], f_send[a], f_recv[a]]
    _, bufs, _ = comm_call(name, list(srcs) + list(lands), sems_in, [], fn, after=after)
    return bufs[n:]


def scatter_pair_start(src4s, lands, name, after=None, whole=False):
    n = len(src4s)

    def fn(bufs, sin, sout):
        x, y, c = _xyc()
        for a in range(n):
            src = bufs[a] if whole else bufs[a].at[:, 1 - c]
            _remote(src, bufs[n + a], sout[2 * a].at[0], sout[2 * a + 1].at[0], (x, y, 1 - c)).start()

    return comm_call(name, list(src4s) + list(lands), [], _dma_sems(1, 1) * n, fn, after=after, want_token=True)


def scatter_pair_wait(src4s, lands, sems, after, name, whole=False):
    n = len(src4s)

    def fn(bufs, sin, sout):
        x, y, c = _xyc()
        for a in range(n):
            src = bufs[a] if whole else bufs[a].at[:, 1 - c]
            cp = _remote(src, bufs[n + a], sin[2 * a].at[0], sin[2 * a + 1].at[0], (x, y, 1 - c))
            cp.wait_send()
            cp.wait_recv()

    _, bufs, _ = comm_call(name, list(src4s) + list(lands), sems, [], fn, after=after)
    return bufs[:n], bufs[n:]


def _row_tile(R):
    for cand in range(256, 15, -16):
        if R % cand == 0 and R // cand >= 4:
            return cand
    return R


def chip_sum(src4, land, c, name):
    _, _, R, C = src4.shape
    tr = R

    def body(c_ref, a_ref, b_ref, o_ref):
        o_ref[...] = (a_ref[...].astype(F32) + b_ref[...].astype(F32)).astype(BF16)

    grid_spec = pltpu.PrefetchScalarGridSpec(
        num_scalar_prefetch=1, grid=(N_CHIPS, R // tr),
        in_specs=[pl.BlockSpec((None, None, tr, C), lambda q, i, cr: (q, cr[0], i, 0)),
                  pl.BlockSpec((None, tr, C), lambda q, i, cr: (q, i, 0))],
        out_specs=pl.BlockSpec((None, tr, C), lambda q, i, cr: (q, i, 0)))
    return pl.pallas_call(
        body, name=name, grid_spec=grid_spec,
        out_shape=jax.ShapeDtypeStruct((N_CHIPS, R, C), BF16),
        compiler_params=_params(("parallel", "parallel")),
    )(c.reshape(1).astype(jnp.int32), src4, land)


def scatter_chip_start(sums, lands, name):
    n = len(sums)

    def fn(bufs, sin, sout):
        x, y, c = _xyc()
        for a in range(n):
            for k, (px, py) in enumerate(_other_chips(x, y)):
                _remote(bufs[a].at[2 * px + py], bufs[n + a].at[2 * x + y], sout[2 * a].at[k], sout[2 * a + 1].at[k],
                        (px, py, c)).start()

    return comm_call(name, list(sums) + list(lands), [], _dma_sems(3, 3) * n, fn, want_token=True)


def scatter_chip_wait(sums, lands, sems, after, name):
    n = len(sums)

    def fn(bufs, sin, sout):
        x, y, c = _xyc()
        for a in range(n):
            for k, (px, py) in enumerate(_other_chips(x, y)):
                cp = _remote(bufs[a].at[2 * px + py], bufs[n + a].at[2 * px + py], sin[2 * a].at[k],
                             sin[2 * a + 1].at[k], (px, py, c))
                cp.wait_send()
                cp.wait_recv()

    _, bufs, _ = comm_call(name, list(sums) + list(lands), sems, [], fn, after=after)
    return bufs[:n], bufs[n:]


def _adamw_math(w, g, m, v):
    m = ADAM_B1 * m + (1.0 - ADAM_B1) * g
    v = ADAM_B2 * v + (1.0 - ADAM_B2) * (g * g)
    m_hat = m / (1.0 - ADAM_B1 ** ADAM_STEP)
    v_hat = v / (1.0 - ADAM_B2 ** ADAM_STEP)
    delta = -ADAM_LR * (m_hat / (jnp.sqrt(v_hat) + ADAM_EPS) + ADAM_WD * w)
    return delta, m, v


def _sum_partials(p_ref, own_ref, mine):
    own = own_ref[...].astype(F32)
    g = None
    for s in range(p_ref.shape[0]):
        term = jnp.where(mine == s, own, p_ref[s].astype(F32))
        g = term if g is None else g + term
    return g


def adamw_sharded(parts, sums, my_chip, w, m, v, name):
    R, C = w.shape
    S = parts.shape[0]
    tr = _row_tile(R)

    def body(mc_ref, p_ref, o_ref, w_ref, m_ref, v_ref, g_ref, d_ref, nm_ref, nv_ref):
        g = _sum_partials(p_ref, o_ref, mc_ref[0])
        delta, nm, nv = _adamw_math(w_ref[...], g, m_ref[...], v_ref[...])
        g_ref[...] = g
        d_ref[...] = delta
        nm_ref[...] = nm
        nv_ref[...] = nv

    row = pl.BlockSpec((tr, C), lambda i, mc: (i, 0))
    shp = jax.ShapeDtypeStruct((R, C), F32)
    grid_spec = pltpu.PrefetchScalarGridSpec(
        num_scalar_prefetch=1, grid=(R // tr,),
        in_specs=[pl.BlockSpec((S, tr, C), lambda i, mc: (0, i, 0)),
                  pl.BlockSpec((None, tr, C), lambda i, mc: (mc[0], i, 0)), row, row, row],
        out_specs=[row, row, row, row])
    return pl.pallas_call(
        body, name=name, grid_spec=grid_spec, out_shape=[shp, shp, shp, shp],
        compiler_params=_params(("parallel",)),
    )(my_chip.reshape(1).astype(jnp.int32), parts, sums, w, m, v)


def adamw_small(parts, w, m, v, name):
    R, C = w.shape

    def body(p_ref, w_ref, m_ref, v_ref, g_ref, d_ref, nm_ref, nv_ref):
        g = p_ref[0]
        for s in range(1, N_DEV):
            g = g + p_ref[s]
        delta, nm, nv = _adamw_math(w_ref[...], g, m_ref[...], v_ref[...])
        g_ref[...] = g
        d_ref[...] = delta
        nm_ref[...] = nm
        nv_ref[...] = nv

    shp = jax.ShapeDtypeStruct((R, C), F32)
    return pl.pallas_call(
        body, name=name, out_shape=[shp, shp, shp, shp], compiler_params=_params(None),
    )(parts, w, m, v)


_ROW_NORM_FFN1, _ROW_NORM_MIX, _ROW_NORM_FFN2, _ROW_NORM_PLE, _ROW_NORM_FINAL = 0, 1, 2, 3, 4
_ROW_B_F, _ROW_REL, _ROW_LOSS, _SMALL_ROWS = 5, 6, 7, 8


def _pack_small(D, norm_ffn1, norm_mix, norm_ffn2, norm_ple, norm_final, b_f, rel_table):
    def row(v):
        v = v.reshape(1, -1)
        return jnp.pad(v, ((0, 0), (0, D - v.shape[1])))
    return jnp.concatenate([row(norm_ffn1), row(norm_mix), row(norm_ffn2), row(norm_ple),
                            row(norm_final), row(b_f), row(rel_table),
                            jnp.zeros((1, D), F32)], axis=0)


def _unpack_small(a, shapes):
    return {"norm_ffn1": a[_ROW_NORM_FFN1].reshape(shapes["norm_ffn1"]),
            "norm_mix": a[_ROW_NORM_MIX].reshape(shapes["norm_mix"]),
            "b_f": a[_ROW_B_F, :N_FOX].reshape(shapes["b_f"]),
            "norm_ffn2": a[_ROW_NORM_FFN2].reshape(shapes["norm_ffn2"]),
            "norm_ple": a[_ROW_NORM_PLE].reshape(shapes["norm_ple"]),
            "rel_table": a[_ROW_REL, :N_REL_BUCKETS * N_DIL].reshape(shapes["rel_table"]),
            "norm_final": a[_ROW_NORM_FINAL].reshape(shapes["norm_final"])}


def local_step(x, p, tgt, g_ffn1, g_mix, g_ffn2, g_ple, g_final, b_f, rel_table,
               forward, weights, emit, emit2, dw_theirs, dw_mine, first_dep):
    T, D = x.shape
    P = p.shape[1]
    CW = D // N_DEV
    tq = _tile(T, 256)

    h1 = rms_fwd(x, g_ffn1, "rms_ffn1", dep=first_dep)
    tiles = bias_tiles(rel_table, T, tq)
    forward("ffn1_g", [tiles, h1])
    wg1, = weights("ffn1_g", h1)
    gate1 = ffn_gate(h1, wg1, "ffn1_gate")
    forward("ffn1_u", gate1)
    wu1, = weights("ffn1_u", gate1)
    a1, b1, s1 = ffn_up_gated(h1, wu1, gate1, "ffn1_up")
    forward("ffn1_d", s1)
    wd1, = weights("ffn1_d", s1)
    x1, h2 = ffn_down(s1, wd1, 0, 1, x, g_mix, "ffn1_down")

    forward("mix_in", h2)
    w3, wf = weights("mix_in", h2)
    qkv = mm_nt([(h2, w3)], "mix_qkv", tn=768, out_dtype=BF16)
    uf = mm_nt([(h2, wf)], "mix_forget", tn=LANES, out_dtype=F32)
    bfp = jnp.pad(b_f.reshape(1, N_FOX), ((0, 0), (0, LANES - N_FOX)))
    c, ct = fox_gate_fwd(uf, bfp, "fox_gate")
    cat, lse = attention_fwd(qkv, c, ct, tiles, "attention")
    forward("mix_out", cat)
    wo, = weights("mix_out", cat)
    x2 = mm_nn(cat, wo, "mix_out", tn=512, out_dtype=F32, res=x1)

    h3 = rms_fwd(x2, g_ffn2, "rms_ffn2")
    forward("ffn2_gu", h3)
    wgu2, = weights("ffn2_gu", h3)
    a2, b2, s2 = ffn_up(h3, wgu2, 0, 1, 2, "ffn2_up")
    forward("ffn2_d", s2)
    wd2, = weights("ffn2_d", s2)
    x3, h4 = ffn_down(s2, wd2, 0, 1, x2, g_ple, "ffn2_down")
    forward("ple", x3)

    wpg, wpp = weights("ple", h4)
    z = mm_nn(h4, wpg, "ple_gate", tn=512, out_dtype=F32)
    pp = mm_nn(p, wpp, "ple_proj", tn=CW, tm=T, out_dtype=F32, n_out=D,
               b_block=(P, CW), b_map=lambda n, i: (n, 0))
    loss_row, dx4, dg_final, dz, dpp = ple_loss(x3, z, pp, g_final, tgt, "ple_loss")

    grads = {}
    grads["w_ple_proj"] = mm_tn(p, dpp, "ple_proj_dw", grid=(N_DEV,),
                                a_block=(T, P), a_map=lambda n: (0, 0),
                                b_block=(T, CW), b_map=lambda n: (0, n),
                                o_block=(P, CW), o_map=lambda n: (n, 0),
                                out_shape=(N_DEV * P, CW))
    grads["w_ple_gate"] = mm_tn_plain(h4, dz, "ple_gate_dw")
    tok = emit("ple", grads)
    dh4 = mm_nt([(dz, wpg)], "ple_gate_dh", tn=1024, out_dtype=F32, dep=tok)
    tok = emit2("ple", dh4)
    dx3, dx3h, dg_ple = rms_bwd(dh4, x3, g_ple, dx4, "rms_ple_bwd", dep=tok)

    da2, db2 = ffn_bwd_act(dx3h, wd2, 0, 1, a2, b2, "ffn2_bwd_act")
    tok = dw_theirs("ffn2_w_down", s2, dx3h)
    tok = dw_theirs("ffn2_w_gate", da2, h3, dep=tok)
    tok = dw_theirs("ffn2_w_up", db2, h3, dep=tok)
    tok = dw_mine("ffn2_w_down", s2, dx3h, dep=tok)
    tok = dw_mine("ffn2_w_gate", da2, h3, dep=tok)
    tok = dw_mine("ffn2_w_up", db2, h3, dep=tok)
    dh3 = ffn_bwd_dh(da2, db2, wgu2, wgu2, 0, 1, 2, D, "ffn2_bwd_dh", dep=tok)
    dx2, _, dg_ffn2 = rms_bwd(dh3, x2, g_ffn2, dx3, "rms_ffn2_bwd", half=False)

    dcat = mm_nt([(dx2, wo)], "mix_out_dh", tn=1024, out_dtype=BF16)
    grads["w_o"] = mm_tn_plain(cat, dx2, "mix_out_dw")
    dq, dk, dv, dct, dtiles = attention_bwd(qkv, c, ct, tiles, lse, cat, dcat, "attention_bwd")
    dctp = jnp.pad(dct[:, 0, :], ((0, LANES - N_HEADS), (0, 0)))
    duf, dbf = fox_gate_bwd(dctp, uf, bfp, "fox_gate_bwd")
    drel = rel_table_grad(dtiles, T, "rel_table_grad")[:, 0, :N_REL_BUCKETS].T
    du3 = jnp.concatenate([dq, dk, dv], axis=1)
    grads["w3"] = mm_tn_plain(du3, h2, "mix_qkv_dw", tm=768)
    grads["wf"] = mm_tn_plain(duf, h2, "mix_forget_dw", tm=LANES)
    tok = emit("mix", grads)
    dh2 = mm_nn_sum([(du3, w3), (duf, wf)], "mix_in_dh", tn=1024, out_dtype=F32, dep=tok)
    tok = emit2("mix", dh2)
    dx1, dx1h, dg_mix = rms_bwd(dh2, x1, g_mix, dx2, "rms_mix_bwd", dep=tok)

    da1, db1 = ffn_bwd_act(dx1h, wd1, 0, 1, a1, b1, "ffn1_bwd_act")
    tok = dw_theirs("ffn1_w_down", s1, dx1h)
    tok = dw_theirs("ffn1_w_gate", da1, h1, dep=tok)
    tok = dw_mine("ffn1_w_down", s1, dx1h, dep=tok)
    tok = dw_theirs("ffn1_w_up", db1, h1, dep=tok)
    tok = dw_mine("ffn1_w_gate", da1, h1, dep=tok)
    tok = dw_mine("ffn1_w_up", db1, h1, dep=tok)
    dh1 = ffn_bwd_dh(da1, db1, wg1, wu1, 0, 0, 1, D, "ffn1_bwd_dh", dep=tok)
    dx0, _, dg_ffn1 = rms_bwd(dh1, x, g_ffn1, dx1, "rms_ffn1_bwd", half=False)

    small = _pack_small(D, dg_ffn1, dg_mix, dg_ffn2, dg_ple, dg_final, dbf[:, :N_FOX], drel)
    small = small.at[_ROW_LOSS, :LANES].set(loss_row[0])
    grads["small"] = small
    emit("small", grads)
    return dx0


def _split_w_in(w_in_t):
    df, dd = N_FOX * HEAD_DIM, N_DIL * HEAD_DIM
    o = np.cumsum([0, df, df, df, N_FOX, dd, dd, dd]).tolist()
    qa, ka, va, f, qb, kb, vb = [w_in_t[o[i]:o[i + 1]] for i in range(7)]
    return jnp.concatenate([qa, qb, ka, kb, va, vb], axis=0), f


def _join_w_in(d3, dfg):
    df, dd = N_FOX * HEAD_DIM, N_DIL * HEAD_DIM
    o = np.cumsum([0, df, dd, df, dd, df, dd]).tolist()
    qa, qb, ka, kb, va, vb = [d3[o[i]:o[i + 1]] for i in range(6)]
    return jnp.concatenate([qa, ka, va, dfg, qb, kb, vb], axis=0)


def rows_to_bf16(a3, name, dep=None):
    R, _, C = a3.shape
    tc = _tile(C, 512)

    def body(a_ref, *rest):
        rest[-1][...] = a_ref[...].astype(BF16)

    in_specs = [pl.BlockSpec((R, None, tc), lambda n: (0, 0, n))]
    args = [a3]
    if dep is not None:
        in_specs.append(_dep_spec(1))
        args.append(dep)
    return pl.pallas_call(
        body, name=name, grid=(C // tc,), in_specs=in_specs,
        out_specs=pl.BlockSpec((R, tc), lambda n: (0, n)),
        out_shape=jax.ShapeDtypeStruct((R, C), BF16),
        compiler_params=_params(("parallel",)),
    )(*args)


def adamw_rows3d(parts, sums, my_chip, w3, m3, v3, name):
    R, _, C = w3.shape
    S = parts.shape[0]
    tc = _tile(C, 512)

    def body(mc_ref, p_ref, o_ref, w_ref, m_ref, v_ref, g_ref, d_ref, nm_ref, nv_ref):
        g = _sum_partials(p_ref, o_ref, mc_ref[0])
        delta, nm, nv = _adamw_math(w_ref[...], g, m_ref[...], v_ref[...])
        g_ref[...] = g
        d_ref[...] = delta
        nm_ref[...] = nm
        nv_ref[...] = nv

    col = pl.BlockSpec((R, None, tc), lambda n, mc: (0, 0, n))
    shp = jax.ShapeDtypeStruct((R, 1, C), F32)
    grid_spec = pltpu.PrefetchScalarGridSpec(
        num_scalar_prefetch=1, grid=(C // tc,),
        in_specs=[pl.BlockSpec((S, R, tc), lambda n, mc: (0, 0, n)),
                  pl.BlockSpec((None, R, tc), lambda n, mc: (mc[0], 0, n)), col, col, col],
        out_specs=[col, col, col, col])
    return pl.pallas_call(
        body, name=name, grid_spec=grid_spec, out_shape=[shp, shp, shp, shp],
        compiler_params=_params(("parallel",)),
    )(my_chip.reshape(1).astype(jnp.int32), parts, sums, w3, m3, v3)


def kernel(x, p, norm_ffn1, ffn1_w_gate, ffn1_w_up, ffn1_w_down, norm_mix, w_in, b_f, w_o, norm_ffn2, ffn2_w_gate, ffn2_w_up, ffn2_w_down, norm_ple, w_ple_gate, w_ple_proj, rel_table, norm_final, loss_target, m_norm_ffn1, m_ffn1_w_gate, m_ffn1_w_up, m_ffn1_w_down, m_norm_mix, m_w_in, m_b_f, m_w_o, m_norm_ffn2, m_ffn2_w_gate, m_ffn2_w_up, m_ffn2_w_down, m_norm_ple, m_w_ple_gate, m_w_ple_proj, m_rel_table, m_norm_final, v_norm_ffn1, v_ffn1_w_gate, v_ffn1_w_up, v_ffn1_w_down, v_norm_mix, v_w_in, v_b_f, v_w_o, v_norm_ffn2, v_ffn2_w_gate, v_ffn2_w_up, v_ffn2_w_down, v_norm_ple, v_w_ple_gate, v_w_ple_proj, v_rel_table, v_norm_final):
    names = ["norm_ffn1", "ffn1_w_gate", "ffn1_w_up", "ffn1_w_down", "norm_mix", "w_in", "b_f", "w_o",
             "norm_ffn2", "ffn2_w_gate", "ffn2_w_up", "ffn2_w_down", "norm_ple", "w_ple_gate",
             "w_ple_proj", "rel_table", "norm_final"]
    w = dict(zip(names, [norm_ffn1, ffn1_w_gate, ffn1_w_up, ffn1_w_down, norm_mix, w_in, b_f, w_o,
                         norm_ffn2, ffn2_w_gate, ffn2_w_up, ffn2_w_down, norm_ple, w_ple_gate,
                         w_ple_proj, rel_table, norm_final]))
    m = dict(zip(names, [m_norm_ffn1, m_ffn1_w_gate, m_ffn1_w_up, m_ffn1_w_down, m_norm_mix, m_w_in,
                         m_b_f, m_w_o, m_norm_ffn2, m_ffn2_w_gate, m_ffn2_w_up, m_ffn2_w_down,
                         m_norm_ple, m_w_ple_gate, m_w_ple_proj, m_rel_table, m_norm_final]))
    v = dict(zip(names, [v_norm_ffn1, v_ffn1_w_gate, v_ffn1_w_up, v_ffn1_w_down, v_norm_mix, v_w_in,
                         v_b_f, v_w_o, v_norm_ffn2, v_ffn2_w_gate, v_ffn2_w_up, v_ffn2_w_down,
                         v_norm_ple, v_w_ple_gate, v_w_ple_proj, v_rel_table, v_norm_final]))
    sharded = ["ffn1_w_gate", "ffn1_w_up", "ffn1_w_down", "w_in", "w_o", "ffn2_w_gate", "ffn2_w_up",
               "ffn2_w_down", "w_ple_gate", "w_ple_proj"]
    small_names = [n for n in names if n not in sharded]

    xs, ps, tgt = x[0], p[0, 0], loss_target[0]
    T, D = xs.shape
    transposed = ("ffn1_w_gate", "ffn1_w_up", "ffn2_w_gate", "ffn2_w_up")

    def view(t, n):
        if n in transposed:
            return t[n][0].T
        if n == "w_in":
            return jnp.transpose(t[n], (2, 0, 1))
        return t[n][0]

    def unview(a, n):
        if n in transposed:
            return a.T.reshape(w[n].shape)
        if n == "w_in":
            return jnp.transpose(a, (1, 2, 0))
        return a.reshape(w[n].shape)

    sh = {n: view(w, n) for n in sharded}
    m_sh = {n: view(m, n) for n in sharded}
    v_sh = {n: view(v, n) for n in sharded}
    F8 = sh["ffn1_w_down"].shape[0]
    WIN8 = sh["w_in"].shape[0]
    me = 4 * lax.axis_index("x") + 2 * lax.axis_index("y") + lax.axis_index("c")

    def start(groups, name, after=None):
        srcs = [s for grp in groups for s in grp]
        sems, bufs, token = gather_start(srcs, [_landing(s, me) for s in srcs], name, after=after)
        return sems, bufs[:len(srcs)], bufs[len(srcs):], token

    cat0 = lambda ns, z: (jnp.concatenate([sh[n] for n in ns], axis=0) + z).astype(BF16)
    sems_a, srcs_a, lands_a, token_a = start(
        [[sh["ffn1_w_gate"].astype(BF16)], [sh["ffn1_w_up"].astype(BF16)], [sh["ffn1_w_down"].astype(BF16)]],
        "gather_start_ffn1")
    zero = token_a[0, 0]
    w_in_bf = rows_to_bf16(sh["w_in"], "w_in_bf16", dep=token_a)
    sems_b, srcs_b, lands_b, g_token = start(
        [[w_in_bf, (sh["w_o"] + zero).astype(BF16)],
         [cat0(["ffn2_w_gate", "ffn2_w_up"], zero), (sh["ffn2_w_down"] + zero).astype(BF16)],
         [(sh["w_ple_gate"] + zero).astype(BF16), (sh["w_ple_proj"] + zero).astype(BF16)]],
        "gather_start_rest", after=token_a)
    order = ["ffn1_g", "ffn1_u", "ffn1_d", "mix_in", "mix_out", "ffn2_gu", "ffn2_d", "ple"]
    group_sizes = [1, 1, 1, 1, 1, 1, 1, 2]
    g_sems, g_srcs, g_lands = sems_a + sems_b, srcs_a + srcs_b, lands_a + lands_b
    g_send, g_recv_d, g_recv_i = g_sems[0::3], g_sems[1::3], g_sems[2::3]
    first = np.cumsum([0] + group_sizes).tolist()
    passed = {}

    def arrays_of(group):
        k = order.index(group)
        return slice(first[k], first[k + 1])

    def forward(group, after):
        sl = arrays_of(group)
        f_sems, bufs = gather_forward(g_srcs[sl], g_lands[sl], g_recv_i[sl], after, "gather_forward_" + group)
        k = len(bufs) // 2
        passed[group] = (f_sems[0::2], f_sems[1::2], bufs[:k], bufs[k:])

    def weights(group, after):
        sl = arrays_of(group)
        f_send, f_recv, srcs, lands = passed[group]
        got = gather_wait(srcs, lands, g_send[sl], g_recv_d[sl], f_send, f_recv, after, "gather_wait_" + group)
        if group in ("ffn1_g", "ffn1_u", "ffn1_d"):
            return (got[0].reshape(N_DEV * F8, D),)
        if group == "ffn2_gu":
            return (got[0].reshape(N_DEV * 2 * F8, D),)
        if group in ("ffn2_d", "mix_out"):
            return (got[0].reshape(-1, D),)
        if group == "ple":
            return got[0].reshape(-1, D), got[1].reshape(-1, got[1].shape[2])
        w3, wf8 = _split_w_in(got[0].reshape(N_DEV * WIN8, D))
        return w3, jnp.pad(wf8, ((0, LANES - N_FOX), (0, 0)))

    ffn_names = ["ffn2_w_down", "ffn2_w_gate", "ffn2_w_up", "ffn1_w_down", "ffn1_w_gate", "ffn1_w_up"]
    scatter_groups = {"ple": ["w_ple_gate", "w_ple_proj"], "mix": ["w_in", "w_o"]}
    scatter_groups.update({n: [n] for n in ffn_names})
    x_i, y_i, c_i = _xyc()
    my_chip = 2 * x_i + y_i
    pair_stage, chip_stage, small_stage = {}, {}, {}

    def emit(group, grads, after=None):
        if group == "small":
            src = grads["small"]
            ss, rs, srcs, lands, token = exchange_start([src], [_landing(src, me)], [False], "scatter_start_small")
            small_stage["small"] = (ss, rs, srcs, lands)
            return token
        src4s = []
        for n in scatter_groups[group]:
            if n == "w_in":
                full = _join_w_in(grads["w3"], grads["wf"][:N_FOX])
                src4s.append(full.reshape(N_CHIPS, 2, WIN8, D))
            else:
                src4s.append(grads[n].reshape((N_CHIPS, 2) + sh[n].shape))
        lands = [lax.empty((N_CHIPS,) + s.shape[2:], BF16) for s in src4s]
        sems, bufs, token = scatter_pair_start(src4s, lands, "scatter_pair_start_" + group, after=after)
        k = len(src4s)
        pair_stage[group] = (sems, bufs[:k], bufs[k:])
        return token

    def emit2(group, after):
        sems, src4s, lands = pair_stage[group]
        src4s, lands = scatter_pair_wait(src4s, lands, sems, after, "scatter_pair_wait_" + group)
        sums = [chip_sum(s4, la, c_i, "chip_sum_" + n)
                for s4, la, n in zip(src4s, lands, scatter_groups[group])]
        chip_lands = [lax.empty(s.shape, s.dtype) for s in sums]
        sems, bufs, token = scatter_chip_start(sums, chip_lands, "scatter_chip_start_" + group)
        k = len(sums)
        chip_stage[group] = (sems, bufs[:k], bufs[k:])
        return token

    def dw_theirs(n, act, other, dep=None):
        theirs = ffn_dw_side(act, other, c_i, False, n + "_dw_theirs", dep=dep)
        theirs = theirs.reshape((N_CHIPS,) + sh[n].shape)
        sems, bufs, token = scatter_pair_start([theirs], [lax.empty(theirs.shape, BF16)],
                                               "scatter_pair_start_" + n, whole=True)
        pair_stage[n] = (sems, bufs[:1], bufs[1:])
        return token

    def dw_mine(n, act, other, dep=None):
        sems, theirs, lands = pair_stage[n]
        _, lands = scatter_pair_wait(theirs, lands, sems, dep, "scatter_pair_wait_" + n, whole=True)
        sums = ffn_dw_side(act, other, c_i, True, n + "_dw_mine", add=lands[0].reshape(-1, D))
        sums = sums.reshape((N_CHIPS,) + sh[n].shape)
        sems, bufs, token = scatter_chip_start([sums], [lax.empty(sums.shape, BF16)], "scatter_chip_start_" + n)
        chip_stage[n] = (sems, bufs[:1], bufs[1:])
        return token

    dx0 = local_step(
        xs, ps, tgt, w["norm_ffn1"], w["norm_mix"], w["norm_ffn2"], w["norm_ple"],
        w["norm_final"].reshape(1, D), w["b_f"], w["rel_table"], forward, weights, emit, emit2,
        dw_theirs, dw_mine, g_token)

    res = {}
    after = dx0
    for group in ["ple"] + ffn_names[:3] + ["mix"] + ffn_names[3:]:
        sems, sums, chip_lands = chip_stage[group]
        sums, parts = scatter_chip_wait(sums, chip_lands, sems, after, "scatter_chip_wait_" + group)
        for n, part, own in zip(scatter_groups[group], parts, sums):
            update = adamw_rows3d if n == "w_in" else adamw_sharded
            g, d, nm, nv = update(part, own, my_chip, sh[n], m_sh[n], v_sh[n], "adamw_" + n)
            res[n] = tuple(unview(a, n) for a in (g, d, nm, nv))
            after = g
    ss, rs, srcs, lands = small_stage["small"]
    small_parts, = exchange_wait(ss, rs, srcs, lands, [False], after, "scatter_wait_small")
    pack = lambda t: _pack_small(D, t["norm_ffn1"], t["norm_mix"], t["norm_ffn2"], t["norm_ple"],
                                 t["norm_final"], t["b_f"], t["rel_table"])
    gs, ds, ms, vs = adamw_small(small_parts, pack(w), pack(m), pack(v), "adamw_small")
    shapes = {n: w[n].shape for n in small_names}
    unpacked = [_unpack_small(a, shapes) for a in (gs, ds, ms, vs)]
    for n in small_names:
        res[n] = tuple(u[n] for u in unpacked)
    loss = gs[_ROW_LOSS, 0]

    out = [loss, dx0.reshape(x.shape)]
    for k in range(4):
        out += [res[n][k] for n in names]
    return tuple(out)
```
